```python
import math
import jax, jax.numpy as jnp
from jax import lax
import numpy as np

D_MODEL = 2048
BATCH = 4
SEQ = 2048
DEPTH = 2
DEC_BATCH = 8
DEC_SEQ = 4
PAST_LEN = 16384
PAGE_SIZE = 128

N_HEADS = 16
HEAD_DIM = 128
N_KV_HEADS = 4
KV_REP = N_HEADS // N_KV_HEADS
IDX_HEADS = 16
IDX_DIM = 64
TOPK_MAX = 256
Q_BLOCK = 128
CONV_DIM = 1024
CONV_WIDTH = 31
SGU_DIM = 1024
SGU_GROUPS = 8
SGU_GROUP_DIM = SGU_DIM // SGU_GROUPS
CHUNK = 128
N_BRANCH = 3
D_FF = 5632
N_EXPERTS = 8
TOP_K_EXPERTS = 2
D_EXPERT = 5632
REL_BUCKETS = 32
REL_MAX_DIST = 128
LN_EPS = 1e-5
DN_ALPHA = (2 * DEPTH) ** 0.25
DN_BETA = (8 * DEPTH) ** -0.25
N_DENSE = (DEPTH + 1) // 2
N_MOE = DEPTH // 2
N_IN = (N_HEADS * HEAD_DIM + 2 * N_KV_HEADS * HEAD_DIM + IDX_HEADS * IDX_DIM + IDX_DIM
        + IDX_HEADS + 2 * CONV_DIM + 2 * SGU_DIM + N_BRANCH * D_MODEL)

kernel_name = 'hybrid_dsa_conformer_gmlp_step'

F32 = jnp.float32


def _split_points():
    widths = [N_HEADS * HEAD_DIM, N_KV_HEADS * HEAD_DIM, N_KV_HEADS * HEAD_DIM,
              IDX_HEADS * IDX_DIM, IDX_DIM, IDX_HEADS, 2 * CONV_DIM, 2 * SGU_DIM,
              N_BRANCH * D_MODEL]
    return [int(v) for v in np.cumsum(widths)[:-1]]


def layer_norm(x, g, b):
    xf = x.astype(F32)
    mu = jnp.mean(xf, axis=-1, keepdims=True)
    var = jnp.mean(jnp.square(xf - mu), axis=-1, keepdims=True)
    return ((xf - mu) * lax.rsqrt(var + LN_EPS) * g + b).astype(x.dtype)


def rel_bucket(dist):
    n = jnp.maximum(dist, 0)
    max_exact = REL_BUCKETS // 2
    nf = jnp.maximum(n, 1).astype(F32)
    large = max_exact + (jnp.log(nf / max_exact) / math.log(REL_MAX_DIST / max_exact)
                         * (REL_BUCKETS - max_exact)).astype(jnp.int32)
    large = jnp.minimum(large, REL_BUCKETS - 1)
    return jnp.where(n < max_exact, n, large)


def index_scores(iq, iw, ik):
    s = jnp.einsum('bqhd,bld->bqhl', iq, ik, preferred_element_type=F32) * (IDX_DIM ** -0.5)
    return jnp.einsum('bqhl,bqh->bql', jax.nn.relu(s), iw.astype(F32))


def select_keys(scores, pos_q, k_sel):
    key_pos = jnp.arange(scores.shape[-1])
    causal = key_pos[None, None, :] <= pos_q[None, :, None]
    _, idx = lax.top_k(jnp.where(causal, scores, -jnp.inf), k_sel)
    return idx


def sparse_attend(q, k_sel, v_sel, idx, pos_q, rel_bias):
    B, Q, K = idx.shape
    qg = q.reshape(B, Q, N_KV_HEADS, KV_REP, HEAD_DIM)
    logits = jnp.einsum('bqgrd,bqkgd->bqgrk', qg, k_sel, preferred_element_type=F32) * (HEAD_DIM ** -0.5)
    dist = pos_q[None, :, None] - idx
    bias = rel_bias[rel_bucket(dist)].reshape(B, Q, K, N_KV_HEADS, KV_REP).transpose(0, 1, 3, 4, 2)
    logits = logits + bias.astype(F32)
    logits = jnp.where((dist >= 0)[:, :, None, None, :], logits, -jnp.inf)
    p = jax.nn.softmax(logits, axis=-1)
    out = jnp.einsum('bqgrk,bqkgd->bqgrd', p.astype(v_sel.dtype), v_sel)
    return out.reshape(B, Q, N_HEADS * HEAD_DIM)


def gather_rows(a, idx):
    return jax.vmap(lambda ab, ib: ab[ib])(a, idx)


def attn_prompt(q, k, v, iq, iw, ik, rel_bias):
    B, S = q.shape[:2]
    k_sel = min(TOPK_MAX, S // 4)

    def block(i):
        t0 = i * Q_BLOCK
        qb = lax.dynamic_slice_in_dim(q, t0, Q_BLOCK, axis=1)
        iqb = lax.dynamic_slice_in_dim(iq, t0, Q_BLOCK, axis=1)
        iwb = lax.dynamic_slice_in_dim(iw, t0, Q_BLOCK, axis=1)
        pos_q = t0 + jnp.arange(Q_BLOCK)
        idx = select_keys(index_scores(iqb, iwb, ik), pos_q, k_sel)
        return sparse_attend(qb, gather_rows(k, idx), gather_rows(v, idx), idx, pos_q, rel_bias)

    out = lax.map(block, jnp.arange(S // Q_BLOCK))
    return out.transpose(1, 0, 2, 3).reshape(B, S, N_HEADS * HEAD_DIM)


def attn_sample(q, k, v, iq, iw, ik, ck, cv, cik, page_table, rel_bias):
    DB, DS = q.shape[:2]
    past = page_table.shape[1] * PAGE_SIZE
    k_sel = min(TOPK_MAX, (past + DS) // 4)
    ik_past = cik[page_table].reshape(DB, past, IDX_DIM)
    ik_all = jnp.concatenate([ik_past, ik], axis=1)
    pos_q = past + jnp.arange(DS)
    idx = select_keys(index_scores(iq, iw, ik_all), pos_q, k_sel)
    in_past = (idx < past)[..., None, None]
    pidx = jnp.minimum(idx, past - 1)
    phys = jnp.take_along_axis(page_table, (pidx // PAGE_SIZE).reshape(DB, -1), axis=1).reshape(idx.shape)
    off = pidx % PAGE_SIZE
    nidx = jnp.clip(idx - past, 0, DS - 1)
    ks = jnp.where(in_past, ck[phys, off], gather_rows(k, nidx))
    vs = jnp.where(in_past, cv[phys, off], gather_rows(v, nidx))
    return sparse_attend(q, ks, vs, idx, pos_q, rel_bias)


def conv_branch(glu_in, buf, conv_w, conv_b, ln_g, ln_b):
    a, g = jnp.split(glu_in, 2, axis=-1)
    h = a * jax.nn.sigmoid(g)
    hp = jnp.concatenate([buf.astype(h.dtype), h], axis=1)
    c = lax.conv_general_dilated(hp, conv_w[:, None, :].astype(h.dtype), window_strides=(1,),
                                 padding='VALID', dimension_numbers=('NWC', 'WIO', 'NWC'),
                                 feature_group_count=CONV_DIM) + conv_b
    c = layer_norm(c, ln_g, ln_b)
    return jax.nn.silu(c), hp[:, -(CONV_WIDTH - 1):]


def sgu_prompt(uv, ln_g, ln_b, w_s, b_s):
    u, vv = jnp.split(uv, 2, axis=-1)
    vn = layer_norm(vv, ln_g, ln_b)
    B, S = vn.shape[:2]
    vc = vn.reshape(B, S // CHUNK, CHUNK, SGU_GROUPS, SGU_GROUP_DIM)
    wm = w_s * jnp.tril(jnp.ones((CHUNK, CHUNK), w_s.dtype))
    mixed = jnp.einsum('gts,bcsgd->bctgd', wm, vc) + b_s.T[None, None, :, :, None]
    return u * mixed.reshape(B, S, SGU_DIM), vn


def sgu_sample(uv, ln_g, ln_b, w_s, b_s):
    u, vv = jnp.split(uv, 2, axis=-1)
    vn = layer_norm(vv, ln_g, ln_b)
    DB, n = vn.shape[:2]
    vc = vn.reshape(DB, n, SGU_GROUPS, SGU_GROUP_DIM)
    wm = w_s[:, :n, :n] * jnp.tril(jnp.ones((n, n), w_s.dtype))
    mixed = jnp.einsum('gts,bsgd->btgd', wm, vc) + b_s[:, :n].T[None, :, :, None]
    return u * mixed.reshape(DB, n, SGU_DIM), vn


def swiglu(x, w1, w3, w2):
    return (jax.nn.silu(x @ w1) * (x @ w3)) @ w2


def moe_ffn(x, w_router, b_router, w1, w3, w2):
    logits = jnp.einsum('bsd,de->bse', x, w_router, preferred_element_type=F32) + b_router.astype(F32)
    top_v, top_i = lax.top_k(logits, TOP_K_EXPERTS)
    gate = jax.nn.softmax(top_v, axis=-1)
    combine = jnp.sum(jax.nn.one_hot(top_i, N_EXPERTS, dtype=F32) * gate[..., None], axis=-2)
    out = jnp.zeros_like(x)
    for e in range(N_EXPERTS):
        out = out + combine[..., e:e + 1].astype(x.dtype) * swiglu(x, w1[e], w3[e], w2[e])
    return out


def run_trunk(x, W, attn_fn, conv_bufs, sgu_fn):
    B, S = x.shape[:2]
    new_k, new_v, new_ik, new_conv, new_sgu = [], [], [], [], []
    for l in range(DEPTH):
        h = x @ W['w_in'][l]
        q, k, v, iq, ik, iw, glu_in, uv, gate_in = jnp.split(h, _split_points(), axis=-1)
        q = q.reshape(B, S, N_HEADS, HEAD_DIM)
        k = k.reshape(B, S, N_KV_HEADS, HEAD_DIM)
        v = v.reshape(B, S, N_KV_HEADS, HEAD_DIM)
        iq = iq.reshape(B, S, IDX_HEADS, IDX_DIM)
        iw = iw * (IDX_HEADS ** -0.5)
        a_out = attn_fn(l, q, k, v, iq, iw, ik)
        c_out, conv_state = conv_branch(glu_in, conv_bufs[l], W['conv_w'][l], W['conv_b'][l],
                                        W['conv_ln_g'][l], W['conv_ln_b'][l])
        s_out, v_rows = sgu_fn(uv, W['sgu_ln_g'][l], W['sgu_ln_b'][l], W['sgu_w'][l], W['sgu_b'][l])
        gates = jax.nn.sigmoid(gate_in).reshape(B, S, N_BRANCH, D_MODEL)
        merged = (gates[:, :, 0] * (a_out @ W['w_pa'][l])
                  + gates[:, :, 1] * (c_out @ W['w_pb'][l])
                  + gates[:, :, 2] * (s_out @ W['w_pc'][l]))
        x = layer_norm(DN_ALPHA * x + merged @ W['w_out'][l], W['ln1_g'][l], W['ln1_b'][l])
        j = l // 2
        if l % 2 == 0:
            f = swiglu(x, W['ffn_w1'][j], W['ffn_w3'][j], W['ffn_w2'][j])
        else:
            f = moe_ffn(x, W['moe_router'][j], W['moe_router_b'][j], W['moe_w1'][j], W['moe_w3'][j], W['moe_w2'][j])
        x = layer_norm(DN_ALPHA * x + f, W['ln2_g'][l], W['ln2_b'][l])
        new_k.append(k)
        new_v.append(v)
        new_ik.append(ik)
        new_conv.append(conv_state)
        new_sgu.append(v_rows)
    return x, jnp.stack(new_k), jnp.stack(new_v), jnp.stack(new_ik), jnp.stack(new_conv), jnp.stack(new_sgu)


def setup_inputs(seed: int = 0) -> dict:
    key = jax.random.key(seed)
    it = iter(jax.random.split(key, 48))
    n_pages = PAST_LEN // PAGE_SIZE
    n_pool = (DEC_BATCH * n_pages * 5) // 4

    def nrm(shape, scale):
        return jax.random.normal(next(it), shape, F32) * scale

    def gain(shape):
        return 1.0 + 0.01 * jax.random.normal(next(it), shape, F32)

    page_table = jax.random.permutation(next(it), n_pool)[: DEC_BATCH * n_pages]
    page_table = page_table.reshape(DEC_BATCH, n_pages).astype(jnp.int32)
    return {
        'x_prompt': nrm((BATCH, SEQ, D_MODEL), 1.0),
        'x_sample': nrm((DEC_BATCH, DEC_SEQ, D_MODEL), 1.0),
        'cache_k': nrm((DEPTH, n_pool, PAGE_SIZE, N_KV_HEADS, HEAD_DIM), 1.0),
        'cache_v': nrm((DEPTH, n_pool, PAGE_SIZE, N_KV_HEADS, HEAD_DIM), 1.0),
        'cache_ik': nrm((DEPTH, n_pool, PAGE_SIZE, IDX_DIM), 1.0),
        'state_conv': nrm((DEPTH, DEC_BATCH, CONV_WIDTH - 1, CONV_DIM), 0.5),
        'page_table': page_table,
        'w_in': nrm((DEPTH, D_MODEL, N_IN), D_MODEL ** -0.5),
        'conv_w': nrm((DEPTH, CONV_WIDTH, CONV_DIM), CONV_WIDTH ** -0.5),
        'conv_b': nrm((DEPTH, CONV_DIM), 0.01),
        'conv_ln_g': gain((DEPTH, CONV_DIM)),
        'conv_ln_b': nrm((DEPTH, CONV_DIM), 0.01),
        'sgu_ln_g': gain((DEPTH, SGU_DIM)),
        'sgu_ln_b': nrm((DEPTH, SGU_DIM), 0.01),
        'sgu_w': nrm((DEPTH, SGU_GROUPS, CHUNK, CHUNK), CHUNK ** -0.5),
        'sgu_b': gain((DEPTH, SGU_GROUPS, CHUNK)),
        'w_pa': nrm((DEPTH, N_HEADS * HEAD_DIM, D_MODEL), (N_HEADS * HEAD_DIM) ** -0.5),
        'w_pb': nrm((DEPTH, CONV_DIM, D_MODEL), CONV_DIM ** -0.5),
        'w_pc': nrm((DEPTH, SGU_DIM, D_MODEL), SGU_DIM ** -0.5),
        'w_out': nrm((DEPTH, D_MODEL, D_MODEL), DN_BETA * D_MODEL ** -0.5),
        'ln1_g': gain((DEPTH, D_MODEL)),
        'ln1_b': nrm((DEPTH, D_MODEL), 0.01),
        'ln2_g': gain((DEPTH, D_MODEL)),
        'ln2_b': nrm((DEPTH, D_MODEL), 0.01),
        'rel_bias': nrm((REL_BUCKETS, N_HEADS), 0.5),
        'ffn_w1': nrm((N_DENSE, D_MODEL, D_FF), D_MODEL ** -0.5),
        'ffn_w3': nrm((N_DENSE, D_MODEL, D_FF), D_MODEL ** -0.5),
        'ffn_w2': nrm((N_DENSE, D_FF, D_MODEL), DN_BETA * D_FF ** -0.5),
        'moe_router': nrm((N_MOE, D_MODEL, N_EXPERTS), D_MODEL ** -0.5),
        'moe_router_b': nrm((N_MOE, N_EXPERTS), 0.01),
        'moe_w1': nrm((N_MOE, N_EXPERTS, D_MODEL, D_EXPERT), D_MODEL ** -0.5),
        'moe_w3': nrm((N_MOE, N_EXPERTS, D_MODEL, D_EXPERT), D_MODEL ** -0.5),
        'moe_w2': nrm((N_MOE, N_EXPERTS, D_EXPERT, D_MODEL), DN_BETA * D_EXPERT ** -0.5),
    }


def reference(x_prompt, x_sample, cache_k, cache_v, cache_ik, state_conv, page_table,
              w_in, conv_w, conv_b, conv_ln_g, conv_ln_b, sgu_ln_g, sgu_ln_b, sgu_w, sgu_b,
              w_pa, w_pb, w_pc, w_out, ln1_g, ln1_b, ln2_g, ln2_b, rel_bias,
              ffn_w1, ffn_w3, ffn_w2, moe_router, moe_router_b, moe_w1, moe_w3, moe_w2):
    W = {'w_in': w_in, 'conv_w': conv_w, 'conv_b': conv_b, 'conv_ln_g': conv_ln_g,
         'conv_ln_b': conv_ln_b, 'sgu_ln_g': sgu_ln_g, 'sgu_ln_b': sgu_ln_b, 'sgu_w': sgu_w,
         'sgu_b': sgu_b, 'w_pa': w_pa, 'w_pb': w_pb, 'w_pc': w_pc, 'w_out': w_out,
         'ln1_g': ln1_g, 'ln1_b': ln1_b, 'ln2_g': ln2_g, 'ln2_b': ln2_b,
         'ffn_w1': ffn_w1, 'ffn_w3': ffn_w3, 'ffn_w2': ffn_w2, 'moe_router': moe_router,
         'moe_router_b': moe_router_b, 'moe_w1': moe_w1, 'moe_w3': moe_w3, 'moe_w2': moe_w2}

    def prompt_attn(l, q, k, v, iq, iw, ik):
        return attn_prompt(q, k, v, iq, iw, ik, rel_bias)

    def sample_attn(l, q, k, v, iq, iw, ik):
        return attn_sample(q, k, v, iq, iw, ik, cache_k[l], cache_v[l], cache_ik[l], page_table, rel_bias)

    prompt_bufs = [jnp.zeros((x_prompt.shape[0], CONV_WIDTH - 1, CONV_DIM), x_prompt.dtype) for _ in range(DEPTH)]
    sample_bufs = [state_conv[l] for l in range(DEPTH)]
    y_prompt, k_p, v_p, ik_p, conv_p, _ = run_trunk(x_prompt, W, prompt_attn, prompt_bufs, sgu_prompt)
    y_sample, k_s, v_s, ik_s, conv_s, sgu_s = run_trunk(x_sample, W, sample_attn, sample_bufs, sgu_sample)
    return (y_prompt, y_sample, k_p, v_p, ik_p, conv_p, k_s, v_s, ik_s, conv_s, sgu_s)
```

```python
import functools
import math

import jax
import jax.numpy as jnp
import numpy as np
from jax import lax
from jax.experimental import pallas as pl
from jax.experimental.pallas import tpu as pltpu

F32 = jnp.float32
BF16 = jnp.bfloat16
I32 = jnp.int32

N_HEADS = 16
HEAD_DIM = 128
N_KV_HEADS = 4
KV_REP = N_HEADS // N_KV_HEADS
IDX_HEADS = 16
IDX_DIM = 64
TOPK_MAX = 256
PAGE_SIZE = 128
CONV_DIM = 1024
CONV_WIDTH = 31
SGU_DIM = 1024
SGU_GROUPS = 8
CHUNK = 128
N_BRANCH = 3
TOP_K_EXPERTS = 2
REL_BUCKETS = 32
REL_MAX_DIST = 128
LN_EPS = 1e-5

V7X_VMEM_BYTES = 64 * 2**20
VMEM_LIMIT = V7X_VMEM_BYTES - 8 * 2**20
LANES = 128
INT_MIN = -(2**31)
MASK_NEG = -1e30

QKV_W = N_HEADS * HEAD_DIM
KV_W = N_KV_HEADS * HEAD_DIM
IQ_W = IDX_HEADS * IDX_DIM
HA_Q, HA_GLU, HA_UV, HA_GATE, HA_IQ = 0, 2048, 4096, 6144, 12288
HA_W = 13312
HB_K, HB_V, HB_IK, HB_IW = 0, 512, 1024, 1152
HB_W = 1280

N_ROW_TILES = 8
MOE_SUB = 256
MOE_SUPER = 1024
MOE_TF = 256
COMBINE_TB = 128


def _cp(sem, vmem=VMEM_LIMIT):
    return pltpu.CompilerParams(dimension_semantics=sem, vmem_limit_bytes=vmem)


def _round_up(x, m):
    return (x + m - 1) // m * m


def _ln_rows(z, g, b):
    mu = jnp.mean(z, axis=-1, keepdims=True)
    d = z - mu
    var = jnp.mean(d * d, axis=-1, keepdims=True)
    return d * lax.rsqrt(var + LN_EPS) * g + b


def _sigmoid(x):
    return 1.0 / (1.0 + jnp.exp(-x))


def _mm_body(x_ref, w_ref, o_ref):
    o_ref[...] = jnp.dot(x_ref[...], w_ref[...], preferred_element_type=F32).astype(o_ref.dtype)


def _matmul(x, w, out_dtype, tm, tn):
    m, k = x.shape
    n = w.shape[1]
    return pl.pallas_call(
        _mm_body,
        grid=(n // tn, m // tm),
        in_specs=[pl.BlockSpec((tm, k), lambda j, i: (i, 0)),
                  pl.BlockSpec((k, tn), lambda j, i: (0, j))],
        out_specs=pl.BlockSpec((tm, tn), lambda j, i: (i, j)),
        out_shape=jax.ShapeDtypeStruct((m, n), out_dtype),
        compiler_params=_cp(("arbitrary", "arbitrary")),
        name="in_proj",
    )(x, w)


def _mm_ln_body(x_ref, w_ref, r_ref, g_ref, b_ref, of_ref, ob_ref, z_ref, *, alpha, nn, tn):
    j = pl.program_id(1)
    z = jnp.dot(x_ref[...], w_ref[...].astype(BF16), preferred_element_type=F32)
    z_ref[j] = z + alpha * r_ref[...]

    @pl.when(j == nn - 1)
    def _():
        n_total = nn * tn
        tot = z_ref[0].sum(axis=-1, keepdims=True)
        for c in range(1, nn):
            tot = tot + z_ref[c].sum(axis=-1, keepdims=True)
        mu = tot / n_total
        sq = None
        for c in range(nn):
            d = z_ref[c] - mu
            s = (d * d).sum(axis=-1, keepdims=True)
            sq = s if sq is None else sq + s
        inv = lax.rsqrt(sq / n_total + LN_EPS)
        for c in range(nn):
            y = (z_ref[c] - mu) * inv * g_ref[:, c * tn:(c + 1) * tn] + b_ref[:, c * tn:(c + 1) * tn]
            of_ref[:, c * tn:(c + 1) * tn] = y
            ob_ref[:, c * tn:(c + 1) * tn] = y.astype(BF16)


def _mm_ln(x, w, layer, resid, g, b, alpha, tm, tn):
    m, k = x.shape
    n = w.shape[2]
    nn = n // tn
    body = functools.partial(_mm_ln_body, alpha=alpha, nn=nn, tn=tn)
    return pl.pallas_call(
        body,
        grid=(m // tm, nn),
        in_specs=[pl.BlockSpec((tm, k), lambda i, j: (i, 0)),
                  pl.BlockSpec((None, k, tn), lambda i, j: (layer, 0, j)),
                  pl.BlockSpec((tm, tn), lambda i, j: (i, j)),
                  pl.BlockSpec((1, n), lambda i, j: (0, 0)),
                  pl.BlockSpec((1, n), lambda i, j: (0, 0))],
        out_specs=[pl.BlockSpec((tm, n), lambda i, j: (i, 0)),
                   pl.BlockSpec((tm, n), lambda i, j: (i, 0))],
        out_shape=[jax.ShapeDtypeStruct((m, n), F32), jax.ShapeDtypeStruct((m, n), BF16)],
        scratch_shapes=[pltpu.VMEM((nn, tm, tn), F32)],
        compiler_params=_cp(("arbitrary", "arbitrary")),
        name="proj_ln",
    )(x, w, resid, g, b)


def _swiglu_up_body(x_ref, w1_ref, w3_ref, o_ref):
    x = x_ref[...]
    a = jnp.dot(x, w1_ref[...].astype(BF16), preferred_element_type=F32)
    c = jnp.dot(x, w3_ref[...].astype(BF16), preferred_element_type=F32)
    o_ref[...] = (a * _sigmoid(a) * c).astype(o_ref.dtype)


def _swiglu_up(x, w1, w3, layer, tm, tf):
    m, k = x.shape
    f = w1.shape[2]
    return pl.pallas_call(
        _swiglu_up_body,
        grid=(f // tf, m // tm),
        in_specs=[pl.BlockSpec((tm, k), lambda j, i: (i, 0)),
                  pl.BlockSpec((None, k, tf), lambda j, i: (layer, 0, j)),
                  pl.BlockSpec((None, k, tf), lambda j, i: (layer, 0, j))],
        out_specs=pl.BlockSpec((tm, tf), lambda j, i: (i, j)),
        out_shape=jax.ShapeDtypeStruct((m, f), BF16),
        compiler_params=_cp(("arbitrary", "arbitrary")),
        name="ffn_up",
    )(x, w1, w3)


def _merge_body(a_ref, c_ref, s_ref, g0_ref, g1_ref, g2_ref, wa_ref, wb_ref, wc_ref, o_ref):
    pa = jnp.dot(a_ref[...], wa_ref[...].astype(BF16), preferred_element_type=F32)
    pb = jnp.dot(c_ref[...], wb_ref[...].astype(BF16), preferred_element_type=F32)
    pc = jnp.dot(s_ref[...], wc_ref[...].astype(BF16), preferred_element_type=F32)
    o = (_sigmoid(g0_ref[...].astype(F32)) * pa + _sigmoid(g1_ref[...].astype(F32)) * pb
         + _sigmoid(g2_ref[...].astype(F32)) * pc)
    o_ref[...] = o.astype(o_ref.dtype)


def _merge(a, c, s, ha, w_pa, w_pb, w_pc, layer, tm, tn):
    m = a.shape[0]
    d = w_pa.shape[2]
    gate0 = HA_GATE // tn
    per_branch = d // tn
    return pl.pallas_call(
        _merge_body,
        grid=(d // tn, m // tm),
        in_specs=[pl.BlockSpec((tm, a.shape[1]), lambda j, i: (i, 0)),
                  pl.BlockSpec((tm, c.shape[1]), lambda j, i: (i, 0)),
                  pl.BlockSpec((tm, s.shape[1]), lambda j, i: (i, 0)),
                  pl.BlockSpec((tm, tn), lambda j, i: (i, gate0 + j)),
                  pl.BlockSpec((tm, tn), lambda j, i: (i, gate0 + per_branch + j)),
                  pl.BlockSpec((tm, tn), lambda j, i: (i, gate0 + 2 * per_branch + j)),
                  pl.BlockSpec((None, w_pa.shape[1], tn), lambda j, i: (layer, 0, j)),
                  pl.BlockSpec((None, w_pb.shape[1], tn), lambda j, i: (layer, 0, j)),
                  pl.BlockSpec((None, w_pc.shape[1], tn), lambda j, i: (layer, 0, j))],
        out_specs=pl.BlockSpec((tm, tn), lambda j, i: (i, j)),
        out_shape=jax.ShapeDtypeStruct((m, d), BF16),
        compiler_params=_cp(("arbitrary", "arbitrary")),
        name="branch_merge",
    )(a, c, s, ha, ha, ha, w_pa, w_pb, w_pc)


def _sortable_key(score):
    bits = pltpu.bitcast(score, I32)
    return bits ^ ((bits >> 31) & jnp.int32(0x7FFFFFFF))


def _kth_largest_key(count_ge, rows, k):
    def step(it, t):
        cand = t + jnp.left_shift(jnp.int32(1), 31 - it)
        return jnp.where(count_ge(cand) >= k, cand, t)

    return lax.fori_loop(0, 32, step, jnp.full((rows, 1), INT_MIN, I32))


def _attn_prompt_body(q_ref, iq_ref, iw_ref, kv_ref, ik_ref, bias_ref, o_ref,
                      keys_ref, mb_ref, m_ref, l_ref, acc_ref, *, tq, k_sel):
    i = pl.program_id(1)
    nkb = i + 1
    tk = tq
    row = lax.broadcasted_iota(I32, (tq, tk), 0)
    col = lax.broadcasted_iota(I32, (tq, tk), 1)
    lane = lax.broadcasted_iota(I32, (tq, LANES), 1)
    low_half = lane < IDX_DIM
    iw = iw_ref[...] * (IDX_DIM ** -0.5 * IDX_HEADS ** -0.5)
    iq_pairs = [iq_ref[:, p * LANES:(p + 1) * LANES] for p in range(IDX_HEADS // 2)]
    zero_b = jnp.zeros((tq, LANES), BF16)
    iq_heads = []
    for h in range(IDX_HEADS):
        keep = low_half if h % 2 == 0 else jnp.logical_not(low_half)
        iq_heads.append(jnp.where(keep, iq_pairs[h // 2], zero_b))

    def idx_block(kb, carry):
        ikb = ik_ref[pl.ds(pl.multiple_of(kb * tk, tk), tk), :].astype(BF16)
        acc = jnp.zeros((tq, tk), F32)
        for h in range(IDX_HEADS):
            s = lax.dot_general(iq_heads[h], ikb, (((1,), (1,)), ((), ())), preferred_element_type=F32)
            acc = acc + jnp.maximum(s, 0.0) * iw[:, h:h + 1]
        causal = (col + kb * tk) <= (row + i * tq)
        keys_ref[kb] = jnp.where(causal, _sortable_key(acc), jnp.int32(INT_MIN))
        return carry

    lax.fori_loop(0, nkb, idx_block, 0)

    def count_ge(cand):
        def cb(kb, c):
            return c + jnp.where(keys_ref[kb] >= cand, 1.0, 0.0)
        cnt = lax.fori_loop(0, nkb, cb, jnp.zeros((tq, tk), F32))
        return cnt.sum(axis=-1, keepdims=True)

    thr = _kth_largest_key(count_ge, tq, float(k_sel))
    thr = jnp.maximum(thr, jnp.int32(INT_MIN + 1))

    def mask_block(kb, carry):
        mb_ref[kb] = jnp.where(keys_ref[kb] >= thr, 0.0, MASK_NEG)
        return carry

    lax.fori_loop(0, nkb, mask_block, 0)

    scale = HEAD_DIM ** -0.5
    for g in range(N_KV_HEADS):
        qg = jnp.concatenate(
            [q_ref[:, (KV_REP * g + r) * HEAD_DIM:(KV_REP * g + r + 1) * HEAD_DIM] for r in range(KV_REP)], axis=0)
        m_ref[...] = jnp.full(m_ref.shape, MASK_NEG, F32)
        l_ref[...] = jnp.zeros(l_ref.shape, F32)
        acc_ref[...] = jnp.zeros(acc_ref.shape, F32)

        def block(kb, bias_idx, g=g, qg=qg):
            r0 = pl.multiple_of(kb * tk, tk)
            kblk = kv_ref[pl.ds(r0, tk), HB_K + g * HEAD_DIM:HB_K + (g + 1) * HEAD_DIM].astype(BF16)
            vblk = kv_ref[pl.ds(r0, tk), HB_V + g * HEAD_DIM:HB_V + (g + 1) * HEAD_DIM].astype(BF16)
            s = lax.dot_general(qg, kblk, (((1,), (1,)), ((), ())), preferred_element_type=F32) * scale
            s = s.reshape(KV_REP, tq, tk) + mb_ref[kb][None]
            if bias_idx is not None:
                s = s + bias_ref[bias_idx, KV_REP * g:KV_REP * (g + 1)]
            s = s.reshape(KV_REP * tq, tk)
            m_old = m_ref[...]
            m_new = jnp.maximum(m_old, s.max(axis=-1, keepdims=True))
            p = jnp.exp(s - m_new)
            alpha = jnp.exp(m_old - m_new)
            l_ref[...] = alpha * l_ref[...] + p.sum(axis=-1, keepdims=True)
            acc_ref[...] = alpha * acc_ref[...] + jnp.dot(p.astype(BF16), vblk, preferred_element_type=F32)
            m_ref[...] = m_new

        def far(kb, carry):
            block(kb, None)
            return carry

        lax.fori_loop(0, jnp.maximum(i - 1, 0), far, 0)

        @pl.when(i >= 1)
        def _():
            block(i - 1, 1)

        block(i, 0)
        out = acc_ref[...] / l_ref[...]
        for r in range(KV_REP):
            h = KV_REP * g + r
            o_ref[:, h * HEAD_DIM:(h + 1) * HEAD_DIM] = out[r * tq:(r + 1) * tq].astype(o_ref.dtype)


def _attn_prompt(ha, hb, bias_tiles, batch, seq, tq):
    nq = seq // tq
    k_sel = min(TOPK_MAX, seq // 4)
    body = functools.partial(_attn_prompt_body, tq=tq, k_sel=k_sel)
    return pl.pallas_call(
        body,
        grid=(batch, nq),
        in_specs=[pl.BlockSpec((tq, QKV_W), lambda b, i: (b * nq + i, HA_Q // QKV_W)),
                  pl.BlockSpec((tq, IQ_W), lambda b, i: (b * nq + i, HA_IQ // IQ_W)),
                  pl.BlockSpec((tq, LANES), lambda b, i: (b * nq + i, HB_IW // LANES)),
                  pl.BlockSpec((seq, 2 * KV_W), lambda b, i: (b, 0), pipeline_mode=pl.Buffered(1)),
                  pl.BlockSpec((seq, LANES), lambda b, i: (b, HB_IK // LANES), pipeline_mode=pl.Buffered(1)),
                  pl.BlockSpec((2, N_HEADS, tq, tq), lambda b, i: (0, 0, 0, 0), pipeline_mode=pl.Buffered(1))],
        out_specs=pl.BlockSpec((tq, QKV_W), lambda b, i: (b * nq + i, 0)),
        out_shape=jax.ShapeDtypeStruct((batch * seq, QKV_W), BF16),
        scratch_shapes=[pltpu.VMEM((nq, tq, tq), I32),
                        pltpu.VMEM((nq, tq, tq), F32),
                        pltpu.VMEM((KV_REP * tq, 1), F32),
                        pltpu.VMEM((KV_REP * tq, 1), F32),
                        pltpu.VMEM((KV_REP * tq, HEAD_DIM), F32)],
        compiler_params=_cp(("arbitrary", "arbitrary")),
        name="attn_prompt",
    )(ha, ha, hb, hb, hb, bias_tiles)


def _idx_sample_body(pt_ref, iq_ref, iw_ref, page_ref, o_ref, *, n_tok):
    page = page_ref[...].astype(BF16)
    s = lax.dot_general(iq_ref[0], page, (((1,), (1,)), ((), ())), preferred_element_type=F32)
    r = jnp.maximum(s, 0.0) * iw_ref[0]
    o_ref[0, 0] = r.reshape(n_tok, IDX_HEADS, PAGE_SIZE).sum(axis=1)


def _idx_sample(page_table, iq_rows, iw_rows, cache_ik, layer, n_tok):
    db, n_pages = page_table.shape
    rows = n_tok * IDX_HEADS
    body = functools.partial(_idx_sample_body, n_tok=n_tok)
    grid_spec = pltpu.PrefetchScalarGridSpec(
        num_scalar_prefetch=1,
        grid=(db, n_pages),
        in_specs=[pl.BlockSpec((1, rows, IDX_DIM), lambda b, p, pt: (b, 0, 0)),
                  pl.BlockSpec((1, rows, 1), lambda b, p, pt: (b, 0, 0)),
                  pl.BlockSpec((None, None, PAGE_SIZE, IDX_DIM), lambda b, p, pt: (layer, pt[b, p], 0, 0))],
        out_specs=pl.BlockSpec((1, 1, n_tok, PAGE_SIZE), lambda b, p, pt: (b, p, 0, 0)),
    )
    return pl.pallas_call(
        body,
        grid_spec=grid_spec,
        out_shape=jax.ShapeDtypeStruct((db, n_pages, n_tok, PAGE_SIZE), F32),
        compiler_params=_cp(("arbitrary", "arbitrary")),
        name="idx_sample",
    )(page_table, iq_rows, iw_rows, cache_ik)


def _attn_sample_body(pt_ref, sc_ref, scn_ref, q_ref, kn_ref, vn_ref, bl_ref, bn_ref, kp_ref, vp_ref, o_ref,
                      thr_ref, m_ref, l_ref, acc_ref, *, n_tok, n_pages, k_sel):
    p = pl.program_id(1)
    rows = N_HEADS * n_tok
    grp = KV_REP * n_tok
    scale = HEAD_DIM ** -0.5

    @pl.when(p == 0)
    def _():
        keys = _sortable_key(sc_ref[0])
        r_i = lax.broadcasted_iota(I32, (n_tok, LANES), 0)
        c_i = lax.broadcasted_iota(I32, (n_tok, LANES), 1)
        keys_new = jnp.where(c_i <= r_i, _sortable_key(scn_ref[0]), jnp.int32(INT_MIN))

        def count_ge(cand):
            c = jnp.where(keys >= cand[None], 1.0, 0.0).sum(axis=0) + jnp.where(keys_new >= cand, 1.0, 0.0)
            return c.sum(axis=-1, keepdims=True)

        thr = _kth_largest_key(count_ge, n_tok, float(k_sel))
        thr_ref[...] = jnp.maximum(thr, jnp.int32(INT_MIN + 1))
        m_ref[...] = jnp.full(m_ref.shape, MASK_NEG, F32)
        l_ref[...] = jnp.zeros(l_ref.shape, F32)
        acc_ref[...] = jnp.zeros(acc_ref.shape, F32)

    thr = thr_ref[...]

    def update(g, s, vblk):
        sl = slice(g * grp, (g + 1) * grp)
        m_old = m_ref[sl, :]
        m_new = jnp.maximum(m_old, s.max(axis=-1, keepdims=True))
        pr = jnp.exp(s - m_new)
        alpha = jnp.exp(m_old - m_new)
        l_ref[sl, :] = alpha * l_ref[sl, :] + pr.sum(axis=-1, keepdims=True)
        acc_ref[sl, :] = alpha * acc_ref[sl, :] + jnp.dot(pr.astype(BF16), vblk, preferred_element_type=F32)
        m_ref[sl, :] = m_new

    keys_p = _sortable_key(sc_ref[0, p])
    mb = jnp.where(keys_p >= thr, 0.0, MASK_NEG)
    mb_g = jnp.concatenate([mb] * KV_REP, axis=0)
    is_last = p == n_pages - 1
    for g in range(N_KV_HEADS):
        qg = q_ref[0, g * grp:(g + 1) * grp, :]
        kblk = kp_ref[:, g * HEAD_DIM:(g + 1) * HEAD_DIM].astype(BF16)
        vblk = vp_ref[:, g * HEAD_DIM:(g + 1) * HEAD_DIM].astype(BF16)
        s = lax.dot_general(qg, kblk, (((1,), (1,)), ((), ())), preferred_element_type=F32) * scale
        s = s + mb_g + jnp.where(is_last, bl_ref[g * grp:(g + 1) * grp, :], 0.0)
        update(g, s, vblk)

    @pl.when(is_last)
    def _():
        r_i = lax.broadcasted_iota(I32, (n_tok, LANES), 0)
        c_i = lax.broadcasted_iota(I32, (n_tok, LANES), 1)
        keys_new = jnp.where(c_i <= r_i, _sortable_key(scn_ref[0]), jnp.int32(INT_MIN))
        mbn = jnp.where(keys_new >= thr, 0.0, MASK_NEG)
        mbn_g = jnp.concatenate([mbn] * KV_REP, axis=0)
        for g in range(N_KV_HEADS):
            qg = q_ref[0, g * grp:(g + 1) * grp, :]
            kblk = kn_ref[0, :, g * HEAD_DIM:(g + 1) * HEAD_DIM].astype(BF16)
            vblk = vn_ref[0, :, g * HEAD_DIM:(g + 1) * HEAD_DIM].astype(BF16)
            s = lax.dot_general(qg, kblk, (((1,), (1,)), ((), ())), preferred_element_type=F32) * scale
            s = s + mbn_g + bn_ref[g * grp:(g + 1) * grp, :]
            update(g, s, vblk)
        o_ref[0] = acc_ref[...] / l_ref[...]


def _attn_sample(page_table, scores, scores_new, q_rows, k_new, v_new, bias_last, bias_new,
                 cache_k, cache_v, layer, n_tok):
    db, n_pages = page_table.shape
    past = n_pages * PAGE_SIZE
    rows = N_HEADS * n_tok
    k_sel = min(TOPK_MAX, (past + n_tok) // 4)
    body = functools.partial(_attn_sample_body, n_tok=n_tok, n_pages=n_pages, k_sel=k_sel)
    grid_spec = pltpu.PrefetchScalarGridSpec(
        num_scalar_prefetch=1,
        grid=(db, n_pages),
        in_specs=[pl.BlockSpec((1, n_pages, n_tok, PAGE_SIZE), lambda b, p, pt: (b, 0, 0, 0)),
                  pl.BlockSpec((1, n_tok, LANES), lambda b, p, pt: (b, 0, 0)),
                  pl.BlockSpec((1, rows, HEAD_DIM), lambda b, p, pt: (b, 0, 0)),
                  pl.BlockSpec((1, LANES, KV_W), lambda b, p, pt: (b, 0, 0)),
                  pl.BlockSpec((1, LANES, KV_W), lambda b, p, pt: (b, 0, 0)),
                  pl.BlockSpec((rows, LANES), lambda b, p, pt: (0, 0)),
                  pl.BlockSpec((rows, LANES), lambda b, p, pt: (0, 0)),
                  pl.BlockSpec((None, None, PAGE_SIZE, KV_W), lambda b, p, pt: (layer, pt[b, p], 0, 0)),
                  pl.BlockSpec((None, None, PAGE_SIZE, KV_W), lambda b, p, pt: (layer, pt[b, p], 0, 0))],
        out_specs=pl.BlockSpec((1, rows, HEAD_DIM), lambda b, p, pt: (b, 0, 0)),
        scratch_shapes=[pltpu.VMEM((n_tok, 1), I32),
                        pltpu.VMEM((rows, 1), F32),
                        pltpu.VMEM((rows, 1), F32),
                        pltpu.VMEM((rows, HEAD_DIM), F32)],
    )
    return pl.pallas_call(
        body,
        grid_spec=grid_spec,
        out_shape=jax.ShapeDtypeStruct((db, rows, HEAD_DIM), F32),
        compiler_params=_cp(("arbitrary", "arbitrary")),
        name="attn_sample",
    )(page_table, scores, scores_new, q_rows, k_new, v_new, bias_last, bias_new, cache_k, cache_v)


def _idx_new_body(iq_ref, iw_ref, ik_ref, o_ref, *, n_tok):
    for b in range(iq_ref.shape[0]):
        s = lax.dot_general(iq_ref[b], ik_ref[b].astype(BF16), (((1,), (1,)), ((), ())),
                            preferred_element_type=F32)
        r = jnp.maximum(s, 0.0) * iw_ref[b]
        o_ref[b] = r.reshape(n_tok, IDX_HEADS, LANES).sum(axis=1)


def _idx_new(iq_rows, iw_rows, ik_new, n_tok):
    db = iq_rows.shape[0]
    return pl.pallas_call(
        functools.partial(_idx_new_body, n_tok=n_tok),
        out_shape=jax.ShapeDtypeStruct((db, n_tok, LANES), F32),
        name="idx_new",
    )(iq_rows, iw_rows, ik_new)


HALO = CONV_WIDTH - 1
HALO_PAD = 32


def _conv_prompt_body(glu_ref, cw_ref, cb_ref, g_ref, b_ref, o_ref, st_ref, hp_ref, c_ref, *, ts, ns):
    s_idx = pl.program_id(1)

    @pl.when(s_idx == 0)
    def _():
        hp_ref[0:HALO_PAD, :] = jnp.zeros((HALO_PAD, CONV_DIM), F32)

    a = glu_ref[:, :CONV_DIM].astype(F32)
    gt = glu_ref[:, CONV_DIM:].astype(F32)
    hp_ref[HALO_PAD:HALO_PAD + ts, :] = a * _sigmoid(gt)
    off = HALO_PAD - HALO
    for c in range(CONV_DIM // LANES):
        cs = slice(c * LANES, (c + 1) * LANES)
        acc = jnp.zeros((ts, LANES), F32)
        for w in range(CONV_WIDTH):
            acc = acc + hp_ref[off + w:off + w + ts, cs] * cw_ref[w:w + 1, cs]
        c_ref[:, cs] = acc + cb_ref[:, cs]
    y = _ln_rows(c_ref[...], g_ref[...], b_ref[...])
    o_ref[...] = (y * _sigmoid(y)).astype(o_ref.dtype)
    tail = hp_ref[ts + off:ts + HALO_PAD, :]

    @pl.when(s_idx == ns - 1)
    def _():
        st_ref[0] = tail

    hp_ref[off:HALO_PAD, :] = tail


def _conv_prompt(ha, conv_w, conv_b, ln_g, ln_b, batch, seq, ts):
    ns = seq // ts
    body = functools.partial(_conv_prompt_body, ts=ts, ns=ns)
    vec = pl.BlockSpec((1, CONV_DIM), lambda b, s: (0, 0))
    return pl.pallas_call(
        body,
        grid=(batch, ns),
        in_specs=[pl.BlockSpec((ts, 2 * CONV_DIM), lambda b, s: (b * ns + s, HA_GLU // (2 * CONV_DIM))),
                  pl.BlockSpec((CONV_WIDTH, CONV_DIM), lambda b, s: (0, 0)), vec, vec, vec],
        out_specs=[pl.BlockSpec((ts, CONV_DIM), lambda b, s: (b * ns + s, 0)),
                   pl.BlockSpec((1, HALO, CONV_DIM), lambda b, s: (b, 0, 0))],
        out_shape=[jax.ShapeDtypeStruct((batch * seq, CONV_DIM), BF16),
                   jax.ShapeDtypeStruct((batch, HALO, CONV_DIM), F32)],
        scratch_shapes=[pltpu.VMEM((HALO_PAD + ts, CONV_DIM), F32), pltpu.VMEM((ts, CONV_DIM), F32)],
        compiler_params=_cp(("arbitrary", "arbitrary")),
        name="conv_prompt",
    )(ha, conv_w, conv_b, ln_g, ln_b)


def _sgu_prompt_body(uv_ref, g_ref, b_ref, w_ref, bs_ref, o_ref, *, ts):
    u = uv_ref[:, :SGU_DIM].astype(F32)
    vn = _ln_rows(uv_ref[:, SGU_DIM:].astype(F32), g_ref[...], b_ref[...]).astype(BF16)
    r_i = lax.broadcasted_iota(I32, (CHUNK, CHUNK), 0)
    c_i = lax.broadcasted_iota(I32, (CHUNK, CHUNK), 1)
    gd = SGU_DIM // SGU_GROUPS
    for g in range(SGU_GROUPS):
        wm = jnp.where(c_i <= r_i, w_ref[g], 0.0).astype(BF16)
        bias = bs_ref[:, g:g + 1]
        for c in range(ts // CHUNK):
            rs = slice(c * CHUNK, (c + 1) * CHUNK)
            gs = slice(g * gd, (g + 1) * gd)
            mixed = jnp.dot(wm, vn[rs, gs], preferred_element_type=F32) + bias
            o_ref[rs, gs] = (u[rs, gs] * mixed).astype(o_ref.dtype)


def _sgu_prompt(ha, ln_g, ln_b, w_s, b_s_t, batch, seq, ts):
    ns = seq // ts
    vec = pl.BlockSpec((1, SGU_DIM), lambda b, s: (0, 0))
    return pl.pallas_call(
        functools.partial(_sgu_prompt_body, ts=ts),
        grid=(batch, ns),
        in_specs=[pl.BlockSpec((ts, 2 * SGU_DIM), lambda b, s: (b * ns + s, HA_UV // (2 * SGU_DIM))),
                  vec, vec,
                  pl.BlockSpec((SGU_GROUPS, CHUNK, CHUNK), lambda b, s: (0, 0, 0)),
                  pl.BlockSpec((CHUNK, SGU_GROUPS), lambda b, s: (0, 0))],
        out_specs=pl.BlockSpec((ts, SGU_DIM), lambda b, s: (b * ns + s, 0)),
        out_shape=jax.ShapeDtypeStruct((batch * seq, SGU_DIM), BF16),
        compiler_params=_cp(("arbitrary", "arbitrary")),
        name="sgu_prompt",
    )(ha, ln_g, ln_b, w_s, b_s_t)


def _mix_sample_body(glu_ref, uv_ref, st_ref, cw_ref, cb_ref, cg_ref, cbt_ref, sg_ref, sb_ref, wv_ref, bv_ref,
                     co_ref, nst_ref, so_ref, vn_ref, hp_ref, *, db, n_tok):
    cw = cw_ref[...]
    for b in range(db):
        rs = slice(b * n_tok, (b + 1) * n_tok)
        a = glu_ref[rs, :CONV_DIM].astype(F32)
        gt = glu_ref[rs, CONV_DIM:].astype(F32)
        hp_ref[0:HALO, :] = st_ref[b]
        hp_ref[HALO:HALO + n_tok, :] = a * _sigmoid(gt)
        rows = [jnp.sum(hp_ref[t:t + CONV_WIDTH, :] * cw, axis=0, keepdims=True) for t in range(n_tok)]
        c = jnp.concatenate(rows, axis=0) + cb_ref[...]
        y = _ln_rows(c, cg_ref[...], cbt_ref[...])
        co_ref[rs, :] = y * _sigmoid(y)
        nst_ref[b] = hp_ref[n_tok:n_tok + HALO, :]
        u = uv_ref[rs, :SGU_DIM].astype(F32)
        vn = _ln_rows(uv_ref[rs, SGU_DIM:].astype(F32), sg_ref[...], sb_ref[...])
        vn_ref[rs, :] = vn
        mixed = []
        for t in range(n_tok):
            acc = bv_ref[t:t + 1, :]
            for s in range(t + 1):
                acc = acc + wv_ref[t, s:s + 1, :] * vn[s:s + 1, :]
            mixed.append(acc)
        so_ref[rs, :] = u * jnp.concatenate(mixed, axis=0)


def _mix_sample(glu, uv, state, conv_w, conv_b, cln_g, cln_b, sln_g, sln_b, wv, bv, db, n_tok):
    rows = db * n_tok
    return pl.pallas_call(
        functools.partial(_mix_sample_body, db=db, n_tok=n_tok),
        out_shape=[jax.ShapeDtypeStruct((rows, CONV_DIM), F32),
                   jax.ShapeDtypeStruct((db, HALO, CONV_DIM), F32),
                   jax.ShapeDtypeStruct((rows, SGU_DIM), F32),
                   jax.ShapeDtypeStruct((rows, SGU_DIM), F32)],
        scratch_shapes=[pltpu.VMEM((_round_up(HALO + n_tok, 8), CONV_DIM), F32)],
        name="mix_sample",
    )(glu, uv, state, conv_w, conv_b, cln_g, cln_b, sln_g, sln_b, wv, bv)


def _router_body(x_ref, w_ref, b_ref, ti_ref, tg_ref, *, n_exp):
    logits = jnp.dot(x_ref[...], w_ref[...], preferred_element_type=F32,
                     precision=lax.Precision.HIGHEST) + b_ref[...]
    lane = lax.broadcasted_iota(I32, logits.shape, 1)
    lane_f = lane.astype(F32)
    neg = -jnp.inf
    lg = jnp.where(lane < n_exp, logits, neg)
    v1 = lg.max(axis=-1, keepdims=True)
    i1 = jnp.where(lg == v1, lane_f, float(LANES)).min(axis=-1, keepdims=True)
    lg2 = jnp.where(lane_f == i1, neg, lg)
    v2 = lg2.max(axis=-1, keepdims=True)
    i2 = jnp.where(lg2 == v2, lane_f, float(LANES)).min(axis=-1, keepdims=True)
    e = jnp.exp(v2 - v1)
    den = 1.0 + e
    ti_ref[...] = jnp.where(lane == 0, i1, jnp.where(lane == 1, i2, 0.0)).astype(I32)
    tg_ref[...] = jnp.where(lane == 0, 1.0 / den, jnp.where(lane == 1, e / den, 0.0))


def _router(x, w_pad, b_pad, n_exp, tm):
    m, d = x.shape
    return pl.pallas_call(
        functools.partial(_router_body, n_exp=n_exp),
        grid=(m // tm,),
        in_specs=[pl.BlockSpec((tm, d), lambda i: (i, 0)),
                  pl.BlockSpec((d, LANES), lambda i: (0, 0)),
                  pl.BlockSpec((1, LANES), lambda i: (0, 0))],
        out_specs=[pl.BlockSpec((tm, LANES), lambda i: (i, 0)),
                   pl.BlockSpec((tm, LANES), lambda i: (i, 0))],
        out_shape=[jax.ShapeDtypeStruct((m, LANES), I32), jax.ShapeDtypeStruct((m, LANES), F32)],
        compiler_params=_cp(("arbitrary",)),
        name="moe_router",
    )(x, w_pad, b_pad)


def _row_copy(src_hbm, dst, src_row, dst_row, sem):
    return pltpu.make_async_copy(src_hbm.at[pl.ds(src_row, 1)], dst.at[pl.ds(dst_row, 1)], sem)


def _dispatch_body(src_ref, x_hbm, o_hbm, sem, *, rows):
    base = pl.program_id(0) * rows

    def start(r, carry):
        _row_copy(x_hbm, o_hbm, src_ref[base + r], base + r, sem).start()
        return carry

    lax.fori_loop(0, rows, start, 0)

    def wait(r, carry):
        _row_copy(x_hbm, o_hbm, 0, base + r, sem).wait()
        return carry

    lax.fori_loop(0, rows, wait, 0)


def _dispatch(src, x, n_slots, rows):
    grid_spec = pltpu.PrefetchScalarGridSpec(
        num_scalar_prefetch=1,
        grid=(n_slots // rows,),
        in_specs=[pl.BlockSpec(memory_space=pl.ANY)],
        out_specs=pl.BlockSpec(memory_space=pl.ANY),
        scratch_shapes=[pltpu.SemaphoreType.DMA(())],
    )
    return pl.pallas_call(
        functools.partial(_dispatch_body, rows=rows),
        grid_spec=grid_spec,
        out_shape=jax.ShapeDtypeStruct((n_slots, x.shape[1]), x.dtype),
        compiler_params=_cp(("arbitrary",)),
        name="moe_dispatch",
    )(src, x)


def _tile_copy(src, dst, sem):
    return pltpu.make_async_copy(src, dst, sem)


def _moe_ffn_body(exp_ref, row0_ref, nsub_ref, jmap_ref, nzero_ref, xs_hbm, w1_ref, w3_ref, w2_ref, ys_hbm,
                  xbuf, acc, w1b, w3b, w2b, sem, *, nj, sub):
    w = pl.program_id(0)
    j = pl.program_id(1)
    nsub = nsub_ref[w]
    row0 = row0_ref[w]
    nzero = nzero_ref[w]

    def zero_copy(t):
        r = pl.multiple_of(row0 + t * sub, sub)
        return _tile_copy(acc.at[pl.ds(0, sub)], ys_hbm.at[pl.ds(r, sub)], sem)

    @pl.when(jnp.logical_and(j == 0, nzero > 0))
    def _():
        acc[0:sub, :] = jnp.zeros((sub, acc.shape[1]), F32)

        def zs(t, carry):
            zero_copy(t).start()
            return carry

        def zw(t, carry):
            zero_copy(t).wait()
            return carry

        lax.fori_loop(0, nzero, zs, 0)
        lax.fori_loop(0, nzero, zw, 0)

    def in_copy(t):
        r = pl.multiple_of(t * sub, sub)
        return _tile_copy(xs_hbm.at[pl.ds(pl.multiple_of(row0 + r, sub), sub)], xbuf.at[pl.ds(r, sub)], sem)

    def out_copy(t):
        r = pl.multiple_of(t * sub, sub)
        return _tile_copy(acc.at[pl.ds(r, sub)], ys_hbm.at[pl.ds(pl.multiple_of(row0 + r, sub), sub)], sem)

    def for_tiles(fn):
        def it(t, carry):
            fn(t)
            return carry
        lax.fori_loop(0, nsub, it, 0)

    @pl.when(j == 0)
    def _():
        for_tiles(lambda t: in_copy(t).start())
        for_tiles(lambda t: in_copy(t).wait())

    @pl.when(nsub > 0)
    def _():
        w1b[...] = w1_ref[...].astype(BF16)
        w3b[...] = w3_ref[...].astype(BF16)
        w2b[...] = w2_ref[...].astype(BF16)

    def tile(t):
        r = pl.multiple_of(t * sub, sub)
        x = xbuf[pl.ds(r, sub), :].astype(BF16)
        a = jnp.dot(x, w1b[...], preferred_element_type=F32)
        c = jnp.dot(x, w3b[...], preferred_element_type=F32)
        h = (a * _sigmoid(a) * c).astype(BF16)
        y = jnp.dot(h, w2b[...], preferred_element_type=F32)

        @pl.when(j == 0)
        def _():
            acc[pl.ds(r, sub), :] = y

        @pl.when(j > 0)
        def _():
            acc[pl.ds(r, sub), :] += y

    for_tiles(tile)

    @pl.when(j == nj - 1)
    def _():
        for_tiles(lambda t: out_copy(t).start())
        for_tiles(lambda t: out_copy(t).wait())


def _moe_ffn(item_exp, item_row0, item_nsub, item_jlast, item_nzero, xs, w1, w3, w2, layer, n_items, super_rows,
             sub, tf):
    n_slots, d = xs.shape
    f = w1.shape[3]
    nj = f // tf

    def jj(w, j, jl):
        return jnp.where(jl[w] > 0, j, nj - 1)

    grid_spec = pltpu.PrefetchScalarGridSpec(
        num_scalar_prefetch=5,
        grid=(n_items, nj),
        in_specs=[pl.BlockSpec(memory_space=pl.ANY),
                  pl.BlockSpec((None, None, d, tf), lambda w, j, ex, r0, ns, jl, nz: (layer, ex[w], 0, jj(w, j, jl))),
                  pl.BlockSpec((None, None, d, tf), lambda w, j, ex, r0, ns, jl, nz: (layer, ex[w], 0, jj(w, j, jl))),
                  pl.BlockSpec((None, None, tf, d), lambda w, j, ex, r0, ns, jl, nz: (layer, ex[w], jj(w, j, jl), 0))],
        out_specs=pl.BlockSpec(memory_space=pl.ANY),
        scratch_shapes=[pltpu.VMEM((super_rows, d), F32),
                        pltpu.VMEM((super_rows, d), F32),
                        pltpu.VMEM((d, tf), BF16),
                        pltpu.VMEM((d, tf), BF16),
                        pltpu.VMEM((tf, d), BF16),
                        pltpu.SemaphoreType.DMA(())],
    )
    return pl.pallas_call(
        functools.partial(_moe_ffn_body, nj=nj, sub=sub),
        grid_spec=grid_spec,
        out_shape=jax.ShapeDtypeStruct((n_slots, d), F32),
        compiler_params=_cp(("arbitrary", "arbitrary")),
        name="moe_ffn",
    )(item_exp, item_row0, item_nsub, item_jlast, item_nzero, xs, w1, w3, w2)


def _combine_body(pos_ref, ys_hbm, tg_ref, x_ref, g_ref, b_ref, of_ref, ob_ref, buf, sem, *, tb, alpha):
    base = pl.program_id(0) * tb

    def start(r, carry):
        for k in range(TOP_K_EXPERTS):
            _row_copy(ys_hbm, buf.at[k], pos_ref[TOP_K_EXPERTS * (base + r) + k], r, sem).start()
        return carry

    lax.fori_loop(0, tb, start, 0)

    def wait(r, carry):
        for k in range(TOP_K_EXPERTS):
            _row_copy(ys_hbm, buf.at[k], 0, r, sem).wait()
        return carry

    lax.fori_loop(0, tb, wait, 0)
    f = tg_ref[:, 0:1] * buf[0] + tg_ref[:, 1:2] * buf[1]
    y = _ln_rows(alpha * x_ref[...] + f, g_ref[...], b_ref[...])
    of_ref[...] = y
    ob_ref[...] = y.astype(BF16)


def _combine_ln(pos, ys, tg, x, g, b, alpha, tb):
    m, d = x.shape
    grid_spec = pltpu.PrefetchScalarGridSpec(
        num_scalar_prefetch=1,
        grid=(m // tb,),
        in_specs=[pl.BlockSpec(memory_space=pl.ANY),
                  pl.BlockSpec((tb, LANES), lambda i, p: (i, 0)),
                  pl.BlockSpec((tb, d), lambda i, p: (i, 0)),
                  pl.BlockSpec((1, d), lambda i, p: (0, 0)),
                  pl.BlockSpec((1, d), lambda i, p: (0, 0))],
        out_specs=[pl.BlockSpec((tb, d), lambda i, p: (i, 0)),
                   pl.BlockSpec((tb, d), lambda i, p: (i, 0))],
        scratch_shapes=[pltpu.VMEM((TOP_K_EXPERTS, tb, d), F32), pltpu.SemaphoreType.DMA(())],
    )
    return pl.pallas_call(
        functools.partial(_combine_body, tb=tb, alpha=alpha),
        grid_spec=grid_spec,
        out_shape=[jax.ShapeDtypeStruct((m, d), F32), jax.ShapeDtypeStruct((m, d), BF16)],
        compiler_params=_cp(("arbitrary",)),
        name="moe_combine_ln",
    )(pos, ys, tg, x, g, b)


def _moe_plan(top_i, n_exp, n_slots, n_items, super_rows, sub):
    a = top_i.shape[0] * TOP_K_EXPERTS
    e_flat = top_i.reshape(-1)
    onehot = (e_flat[:, None] == jnp.arange(n_exp, dtype=I32)[None, :]).astype(I32)
    csum = jnp.cumsum(onehot, axis=0)
    rank = jnp.take_along_axis(csum, e_flat[:, None], axis=1)[:, 0] - 1
    counts = csum[-1]
    padded = (counts + sub - 1) // sub * sub
    gstart = jnp.cumsum(padded) - padded
    pos = (gstart[e_flat] + rank).astype(I32)
    src = jnp.zeros((n_slots,), I32).at[pos].set(jnp.arange(a, dtype=I32) // TOP_K_EXPERTS)
    n_super = (padded + super_rows - 1) // super_rows
    iend = jnp.cumsum(n_super)
    istart = iend - n_super
    total = iend[-1]
    wi = jnp.arange(n_items, dtype=I32)
    valid = wi < total
    e_w = jnp.minimum(jnp.searchsorted(iend, wi, side="right"), n_exp - 1).astype(I32)
    last_e = e_w[jnp.maximum(total - 1, 0)]
    e_w = jnp.where(valid, e_w, last_e)
    k_w = wi - istart[e_w]
    nsub = jnp.where(valid, jnp.minimum(super_rows, padded[e_w] - k_w * super_rows) // sub, 0).astype(I32)
    z0 = jnp.sum(padded) + (wi - total) * super_rows
    nzero = jnp.where(valid, 0, jnp.clip((n_slots - z0) // sub, 0, super_rows // sub)).astype(I32)
    row0 = jnp.where(valid, gstart[e_w] + k_w * super_rows, jnp.where(nzero > 0, z0, 0)).astype(I32)
    return pos, src, e_w, row0, nsub, valid.astype(I32), nzero


def _rel_bucket(dist):
    n = jnp.maximum(dist, 0)
    max_exact = REL_BUCKETS // 2
    nf = jnp.maximum(n, 1).astype(F32)
    large = max_exact + (jnp.log(nf / max_exact) / math.log(REL_MAX_DIST / max_exact)
                         * (REL_BUCKETS - max_exact)).astype(I32)
    large = jnp.minimum(large, REL_BUCKETS - 1)
    return jnp.where(n < max_exact, n, large)


def _shifted_bias(rel_bias, dist):
    t = rel_bias[_rel_bucket(dist)] - rel_bias[REL_BUCKETS - 1]
    t = jnp.where((dist >= 0)[..., None], t, 0.0)
    return jnp.moveaxis(t, -1, 0)


def _row_tile(m_rows, n_tiles):
    return _round_up(-(-m_rows // n_tiles), 32)


def kernel(x_prompt, x_sample, cache_k, cache_v, cache_ik, state_conv, page_table, w_in, conv_w, conv_b,
           conv_ln_g, conv_ln_b, sgu_ln_g, sgu_ln_b, sgu_w, sgu_b, w_pa, w_pb, w_pc, w_out, ln1_g, ln1_b,
           ln2_g, ln2_b, rel_bias, ffn_w1, ffn_w3, ffn_w2, moe_router, moe_router_b, moe_w1, moe_w3, moe_w2):
    batch, seq, d_model = x_prompt.shape
    db, n_tok, _ = x_sample.shape
    depth = w_in.shape[0]
    n_pool = cache_k.shape[1]
    n_exp = moe_router.shape[2]
    mp, ms = batch * seq, db * n_tok
    tm = _row_tile(mp + ms, N_ROW_TILES)
    m_all = tm * N_ROW_TILES
    alpha = (2 * depth) ** 0.25
    tq = min(256, seq)

    xf = jnp.concatenate([x_prompt.reshape(mp, d_model), x_sample.reshape(ms, d_model),
                          jnp.zeros((m_all - mp - ms, d_model), F32)], axis=0)
    xb = xf.astype(BF16)

    ar = jnp.arange(tq, dtype=I32)
    d0 = ar[:, None] - ar[None, :]
    bias_tiles = jnp.stack([_shifted_bias(rel_bias, d0), _shifted_bias(rel_bias, d0 + tq)])
    tok = jnp.arange(n_tok, dtype=I32)
    lane = jnp.arange(LANES, dtype=I32)
    d_last = PAGE_SIZE + tok[:, None] - lane[None, :]
    bias_last = _shifted_bias(rel_bias, d_last).reshape(N_HEADS * n_tok, LANES)
    bias_new = _shifted_bias(rel_bias, tok[:, None] - lane[None, :]).reshape(N_HEADS * n_tok, LANES)

    ck = cache_k.reshape(depth, n_pool, PAGE_SIZE, KV_W)
    cv = cache_v.reshape(depth, n_pool, PAGE_SIZE, KV_W)
    sp = [int(v) for v in np.cumsum([QKV_W, KV_W, KV_W, IQ_W, IDX_DIM, IDX_HEADS, 2 * CONV_DIM, 2 * SGU_DIM])]

    n_slots = _round_up(TOP_K_EXPERTS * m_all + n_exp * (MOE_SUB - 1), MOE_SUB)
    n_items = -(-n_slots // MOE_SUPER) + n_exp

    outs = {k: [] for k in ("k", "v", "ik", "conv_p", "conv_s", "sgu_s")}
    for l in range(depth):
        wl = w_in[l]
        w_q, w_k, w_v, w_iq = wl[:, :sp[0]], wl[:, sp[0]:sp[1]], wl[:, sp[1]:sp[2]], wl[:, sp[2]:sp[3]]
        w_ik, w_iw = wl[:, sp[3]:sp[4]], wl[:, sp[4]:sp[5]]
        w_glu, w_uv, w_gate = wl[:, sp[5]:sp[6]], wl[:, sp[6]:sp[7]], wl[:, sp[7]:]
        wa = jnp.concatenate([w_q, w_glu, w_uv, w_gate, w_iq], axis=1).astype(BF16)
        wb = jnp.concatenate([w_k, w_v, w_ik, w_ik, w_iw,
                              jnp.zeros((d_model, LANES - IDX_HEADS), F32)], axis=1).astype(BF16)
        ha = _matmul(xb, wa, BF16, tm, 1024)
        hb = _matmul(xb, wb, F32, tm, HB_W)

        a_p = _attn_prompt(ha, hb, bias_tiles, batch, seq, tq)
        c_p, conv_state_p = _conv_prompt(ha, conv_w[l], conv_b[l][None], conv_ln_g[l][None], conv_ln_b[l][None],
                                         batch, seq, min(256, seq))
        s_p = _sgu_prompt(ha, sgu_ln_g[l][None], sgu_ln_b[l][None], sgu_w[l], sgu_b[l].T, batch, seq,
                          min(512, seq))

        has, hbs = ha[mp:mp + ms], hb[mp:mp + ms]
        iq_rows = has[:, HA_IQ:HA_IQ + IQ_W].reshape(db, n_tok * IDX_HEADS, IDX_DIM)
        iw_rows = (hbs[:, HB_IW:HB_IW + IDX_HEADS] * (IDX_DIM ** -0.5 * IDX_HEADS ** -0.5)
                   ).reshape(db, n_tok * IDX_HEADS, 1)
        scores = _idx_sample(page_table, iq_rows, iw_rows, cache_ik, l, n_tok)
        pad_rows = LANES - n_tok
        ik_new = jnp.pad(hbs[:, HB_IK:HB_IK + IDX_DIM].reshape(db, n_tok, IDX_DIM), ((0, 0), (0, pad_rows), (0, 0)))
        scores_new = _idx_new(iq_rows, iw_rows, ik_new, n_tok)
        q_rows = has[:, HA_Q:HA_Q + QKV_W].reshape(db, n_tok, N_HEADS, HEAD_DIM).transpose(0, 2, 1, 3)
        q_rows = q_rows.reshape(db, N_HEADS * n_tok, HEAD_DIM)
        k_new = jnp.pad(hbs[:, HB_K:HB_K + KV_W].reshape(db, n_tok, KV_W), ((0, 0), (0, pad_rows), (0, 0)))
        v_new = jnp.pad(hbs[:, HB_V:HB_V + KV_W].reshape(db, n_tok, KV_W), ((0, 0), (0, pad_rows), (0, 0)))
        a_s = _attn_sample(page_table, scores, scores_new, q_rows, k_new, v_new, bias_last, bias_new,
                           ck, cv, l, n_tok)
        a_s = a_s.reshape(db, N_HEADS, n_tok, HEAD_DIM).transpose(0, 2, 1, 3).reshape(ms, QKV_W)
        gd = SGU_DIM // SGU_GROUPS
        wv = jnp.repeat(sgu_w[l][:, :n_tok, :n_tok].transpose(1, 2, 0), gd, axis=-1)
        bv = jnp.repeat(sgu_b[l][:, :n_tok].T, gd, axis=-1)
        c_s, conv_state_s, s_s, vn_s = _mix_sample(
            has[:, HA_GLU:HA_GLU + 2 * CONV_DIM], has[:, HA_UV:HA_UV + 2 * SGU_DIM], state_conv[l],
            conv_w[l], conv_b[l][None], conv_ln_g[l][None], conv_ln_b[l][None],
            sgu_ln_g[l][None], sgu_ln_b[l][None], wv, bv, db, n_tok)

        def join(p, s):
            return jnp.concatenate([p, s.astype(BF16), jnp.zeros((m_all - mp - ms, p.shape[1]), BF16)], axis=0)

        merged = _merge(join(a_p, a_s), join(c_p, c_s), join(s_p, s_s), ha, w_pa, w_pb, w_pc, l, tm, 512)
        x1f, x1b = _mm_ln(merged, w_out, l, xf, ln1_g[l][None], ln1_b[l][None], alpha, tm // 2, 512)

        j = l // 2
        if l % 2 == 0:
            h = _swiglu_up(x1b, ffn_w1, ffn_w3, j, tm, 512)
            xf, xb = _mm_ln(h, ffn_w2, j, x1f, ln2_g[l][None], ln2_b[l][None], alpha, tm // 2, 256)
        else:
            w_r = jnp.pad(moe_router[j], ((0, 0), (0, LANES - n_exp)))
            b_r = jnp.pad(moe_router_b[j], (0, LANES - n_exp))[None]
            ti, tg = _router(x1f, w_r, b_r, n_exp, tm // 2)
            pos, src, it_e, it_r0, it_ns, it_valid, it_nz = _moe_plan(ti[:, :TOP_K_EXPERTS], n_exp, n_slots,
                                                                      n_items, MOE_SUPER, MOE_SUB)
            xs = _dispatch(src, x1f, n_slots, MOE_SUB)
            ys = _moe_ffn(it_e, it_r0, it_ns, it_valid, it_nz, xs, moe_w1, moe_w3, moe_w2, j, n_items,
                          MOE_SUPER, MOE_SUB, MOE_TF)
            tb = COMBINE_TB if m_all % COMBINE_TB == 0 else 16
            xf, xb = _combine_ln(pos, ys, tg, x1f, ln2_g[l][None], ln2_b[l][None], alpha, tb)

        outs["k"].append(hb[:, HB_K:HB_K + KV_W])
        outs["v"].append(hb[:, HB_V:HB_V + KV_W])
        outs["ik"].append(hb[:, HB_IK:HB_IK + IDX_DIM])
        outs["conv_p"].append(conv_state_p)
        outs["conv_s"].append(conv_state_s)
        outs["sgu_s"].append(vn_s.reshape(db, n_tok, SGU_DIM))

    def split(name, width_shape):
        st = jnp.stack(outs[name])
        p = st[:, :mp].reshape((depth, batch, seq) + width_shape)
        s = st[:, mp:mp + ms].reshape((depth, db, n_tok) + width_shape)
        return p, s

    k_p, k_s = split("k", (N_KV_HEADS, HEAD_DIM))
    v_p, v_s = split("v", (N_KV_HEADS, HEAD_DIM))
    ik_p, ik_s = split("ik", (IDX_DIM,))
    y_prompt = xf[:mp].reshape(batch, seq, d_model)
    y_sample = xf[mp:mp + ms].reshape(db, n_tok, d_model)
    return (y_prompt, y_sample, k_p, v_p, ik_p, jnp.stack(outs["conv_p"]), k_s, v_s, ik_s,
            jnp.stack(outs["conv_s"]), jnp.stack(outs["sgu_s"]))
```

```python
import functools
import math

import jax
import jax.numpy as jnp
import numpy as np
from jax import lax
from jax.experimental import pallas as pl
from jax.experimental.pallas import tpu as pltpu

F32 = jnp.float32
BF16 = jnp.bfloat16
I32 = jnp.int32

N_HEADS = 16
HEAD_DIM = 128
N_KV_HEADS = 4
KV_REP = N_HEADS // N_KV_HEADS
IDX_HEADS = 16
IDX_DIM = 64
TOPK_MAX = 256
PAGE_SIZE = 128
CONV_DIM = 1024
CONV_WIDTH = 31
SGU_DIM = 1024
SGU_GROUPS = 8
CHUNK = 128
N_BRANCH = 3
TOP_K_EXPERTS = 2
REL_BUCKETS = 32
REL_MAX_DIST = 128
LN_EPS = 1e-5

V7X_VMEM_BYTES = 64 * 2**20
VMEM_LIMIT = V7X_VMEM_BYTES - 8 * 2**20
LANES = 128
INT_MIN = -(2**31)
MASK_NEG = -1e30

QKV_W = N_HEADS * HEAD_DIM
KV_W = N_KV_HEADS * HEAD_DIM
IQ_W = IDX_HEADS * IDX_DIM
HA_Q, HA_GLU, HA_UV, HA_GATE, HA_IQ = 0, 2048, 4096, 6144, 12288
HA_W = 13312
HB_K, HB_V, HB_IK, HB_IW = 0, 512, 1024, 1152
HB_W = 1280

N_ROW_TILES = 8
MOE_SUB = 256
MOE_SUPER = 2048
MOE_TF = 256
COMBINE_TB = 128
IDX_PAGE_GROUP = 32
ATTN_PAGE_GROUP = 16


def _cp(sem, vmem=VMEM_LIMIT):
    return pltpu.CompilerParams(dimension_semantics=sem, vmem_limit_bytes=vmem)


def _round_up(x, m):
    return (x + m - 1) // m * m


def _ln_rows(z, g, b):
    mu = jnp.mean(z, axis=-1, keepdims=True)
    d = z - mu
    var = jnp.mean(d * d, axis=-1, keepdims=True)
    return d * lax.rsqrt(var + LN_EPS) * g + b


def _sigmoid(x):
    return 1.0 / (1.0 + jnp.exp(-x))


def _mm_body(x_ref, w_ref, o_ref):
    o_ref[...] = jnp.dot(x_ref[...], w_ref[...], preferred_element_type=F32).astype(o_ref.dtype)


def _matmul(x, w, out_dtype, tm, tn):
    m, k = x.shape
    n = w.shape[1]
    return pl.pallas_call(
        _mm_body,
        grid=(n // tn, m // tm),
        in_specs=[pl.BlockSpec((tm, k), lambda j, i: (i, 0)),
                  pl.BlockSpec((k, tn), lambda j, i: (0, j))],
        out_specs=pl.BlockSpec((tm, tn), lambda j, i: (i, j)),
        out_shape=jax.ShapeDtypeStruct((m, n), out_dtype),
        compiler_params=_cp(("arbitrary", "arbitrary")),
        name="in_proj",
    )(x, w)


def _mm_ln_body(x_ref, w_ref, r_ref, g_ref, b_ref, of_ref, ob_ref, z_ref, *, alpha, nn, tn):
    j = pl.program_id(1)
    z = jnp.dot(x_ref[...], w_ref[...].astype(BF16), preferred_element_type=F32)
    z_ref[j] = z + alpha * r_ref[...]

    @pl.when(j == nn - 1)
    def _():
        n_total = nn * tn
        tot = z_ref[0].sum(axis=-1, keepdims=True)
        for c in range(1, nn):
            tot = tot + z_ref[c].sum(axis=-1, keepdims=True)
        mu = tot / n_total
        sq = None
        for c in range(nn):
            d = z_ref[c] - mu
            s = (d * d).sum(axis=-1, keepdims=True)
            sq = s if sq is None else sq + s
        inv = lax.rsqrt(sq / n_total + LN_EPS)
        for c in range(nn):
            y = (z_ref[c] - mu) * inv * g_ref[:, c * tn:(c + 1) * tn] + b_ref[:, c * tn:(c + 1) * tn]
            of_ref[:, c * tn:(c + 1) * tn] = y
            ob_ref[:, c * tn:(c + 1) * tn] = y.astype(BF16)


def _mm_ln(x, w, layer, resid, g, b, alpha, tm, tn):
    m, k = x.shape
    n = w.shape[2]
    nn = n // tn
    body = functools.partial(_mm_ln_body, alpha=alpha, nn=nn, tn=tn)
    return pl.pallas_call(
        body,
        grid=(m // tm, nn),
        in_specs=[pl.BlockSpec((tm, k), lambda i, j: (i, 0)),
                  pl.BlockSpec((None, k, tn), lambda i, j: (layer, 0, j)),
                  pl.BlockSpec((tm, tn), lambda i, j: (i, j)),
                  pl.BlockSpec((1, n), lambda i, j: (0, 0)),
                  pl.BlockSpec((1, n), lambda i, j: (0, 0))],
        out_specs=[pl.BlockSpec((tm, n), lambda i, j: (i, 0)),
                   pl.BlockSpec((tm, n), lambda i, j: (i, 0))],
        out_shape=[jax.ShapeDtypeStruct((m, n), F32), jax.ShapeDtypeStruct((m, n), BF16)],
        scratch_shapes=[pltpu.VMEM((nn, tm, tn), F32)],
        compiler_params=_cp(("arbitrary", "arbitrary")),
        name="proj_ln",
    )(x, w, resid, g, b)


def _swiglu_up_body(x_ref, w1_ref, w3_ref, o_ref):
    x = x_ref[...]
    a = jnp.dot(x, w1_ref[...].astype(BF16), preferred_element_type=F32)
    c = jnp.dot(x, w3_ref[...].astype(BF16), preferred_element_type=F32)
    o_ref[...] = (a * _sigmoid(a) * c).astype(o_ref.dtype)


def _swiglu_up(x, w1, w3, layer, tm, tf):
    m, k = x.shape
    f = w1.shape[2]
    return pl.pallas_call(
        _swiglu_up_body,
        grid=(f // tf, m // tm),
        in_specs=[pl.BlockSpec((tm, k), lambda j, i: (i, 0)),
                  pl.BlockSpec((None, k, tf), lambda j, i: (layer, 0, j)),
                  pl.BlockSpec((None, k, tf), lambda j, i: (layer, 0, j))],
        out_specs=pl.BlockSpec((tm, tf), lambda j, i: (i, j)),
        out_shape=jax.ShapeDtypeStruct((m, f), BF16),
        compiler_params=_cp(("arbitrary", "arbitrary")),
        name="ffn_up",
    )(x, w1, w3)


def _merge_body(a_ref, c_ref, s_ref, g0_ref, g1_ref, g2_ref, wa_ref, wb_ref, wc_ref, o_ref):
    pa = jnp.dot(a_ref[...], wa_ref[...].astype(BF16), preferred_element_type=F32)
    pb = jnp.dot(c_ref[...], wb_ref[...].astype(BF16), preferred_element_type=F32)
    pc = jnp.dot(s_ref[...], wc_ref[...].astype(BF16), preferred_element_type=F32)
    o = (_sigmoid(g0_ref[...].astype(F32)) * pa + _sigmoid(g1_ref[...].astype(F32)) * pb
         + _sigmoid(g2_ref[...].astype(F32)) * pc)
    o_ref[...] = o.astype(o_ref.dtype)


def _merge(a, c, s, ha, w_pa, w_pb, w_pc, layer, tm, tn):
    m = a.shape[0]
    d = w_pa.shape[2]
    gate0 = HA_GATE // tn
    per_branch = d // tn
    return pl.pallas_call(
        _merge_body,
        grid=(d // tn, m // tm),
        in_specs=[pl.BlockSpec((tm, a.shape[1]), lambda j, i: (i, 0)),
                  pl.BlockSpec((tm, c.shape[1]), lambda j, i: (i, 0)),
                  pl.BlockSpec((tm, s.shape[1]), lambda j, i: (i, 0)),
                  pl.BlockSpec((tm, tn), lambda j, i: (i, gate0 + j)),
                  pl.BlockSpec((tm, tn), lambda j, i: (i, gate0 + per_branch + j)),
                  pl.BlockSpec((tm, tn), lambda j, i: (i, gate0 + 2 * per_branch + j)),
                  pl.BlockSpec((None, w_pa.shape[1], tn), lambda j, i: (layer, 0, j)),
                  pl.BlockSpec((None, w_pb.shape[1], tn), lambda j, i: (layer, 0, j)),
                  pl.BlockSpec((None, w_pc.shape[1], tn), lambda j, i: (layer, 0, j))],
        out_specs=pl.BlockSpec((tm, tn), lambda j, i: (i, j)),
        out_shape=jax.ShapeDtypeStruct((m, d), BF16),
        compiler_params=_cp(("arbitrary", "arbitrary")),
        name="branch_merge",
    )(a, c, s, ha, ha, ha, w_pa, w_pb, w_pc)


def _sortable_key(score):
    bits = pltpu.bitcast(score, I32)
    return bits ^ ((bits >> 31) & jnp.int32(0x7FFFFFFF))


def _kth_largest_key(count_ge, rows, k):
    def step(it, t):
        cand = t + jnp.left_shift(jnp.int32(1), 31 - it)
        return jnp.where(count_ge(cand) >= k, cand, t)

    return lax.fori_loop(0, 32, step, jnp.full((rows, 1), INT_MIN, I32))


def _lane_tile(x, width):
    return x if width == LANES else jnp.concatenate([x] * (width // LANES), axis=1)


def _attn_prompt_body(q_ref, iq_ref, iw_ref, kv_ref, ik_ref, bias_ref, o_ref,
                      keys_ref, mb_ref, iwb_ref, m_ref, l_ref, acc_ref, *, tq, k_sel):
    i = pl.program_id(1)
    nkb = i + 1
    tk = tq
    row = lax.broadcasted_iota(I32, (tq, tk), 0)
    col = lax.broadcasted_iota(I32, (tq, tk), 1)
    lane = lax.broadcasted_iota(I32, (tq, LANES), 1)
    low_half = lane < IDX_DIM
    iw = iw_ref[...] * (IDX_DIM ** -0.5 * IDX_HEADS ** -0.5)
    for h in range(IDX_HEADS):
        iwb_ref[h] = jnp.broadcast_to(iw[:, h:h + 1], (tq, LANES))
    iq_pairs = [iq_ref[:, p * LANES:(p + 1) * LANES] for p in range(IDX_HEADS // 2)]
    zero_b = jnp.zeros((tq, LANES), BF16)
    iq_heads = []
    for h in range(IDX_HEADS):
        keep = low_half if h % 2 == 0 else jnp.logical_not(low_half)
        iq_heads.append(jnp.where(keep, iq_pairs[h // 2], zero_b))

    def idx_block(kb, carry):
        ikb = ik_ref[pl.ds(pl.multiple_of(kb * tk, tk), tk), :].astype(BF16)
        acc = jnp.zeros((tq, tk), F32)
        for h in range(IDX_HEADS):
            s = lax.dot_general(iq_heads[h], ikb, (((1,), (1,)), ((), ())), preferred_element_type=F32)
            acc = acc + jnp.maximum(s, 0.0) * _lane_tile(iwb_ref[h], tk)
        causal = (col + kb * tk) <= (row + i * tq)
        keys_ref[kb] = jnp.where(causal, _sortable_key(acc), jnp.int32(INT_MIN))
        return carry

    lax.fori_loop(0, nkb, idx_block, 0)

    def count_ge(cand):
        cand_b = jnp.broadcast_to(cand, (tq, LANES))

        def cb(kb, c):
            keys = keys_ref[kb]
            for t in range(tk // LANES):
                c = c + jnp.where(keys[:, t * LANES:(t + 1) * LANES] >= cand_b, 1.0, 0.0)
            return c

        cnt = lax.fori_loop(0, nkb, cb, jnp.zeros((tq, LANES), F32))
        return cnt.sum(axis=-1, keepdims=True)

    thr = _kth_largest_key(count_ge, tq, float(k_sel))
    thr = jnp.maximum(thr, jnp.int32(INT_MIN + 1))
    thr_b = _lane_tile(jnp.broadcast_to(thr, (tq, LANES)), tk)

    def mask_block(kb, carry):
        mb_ref[kb] = jnp.where(keys_ref[kb] >= thr_b, 0.0, MASK_NEG)
        return carry

    lax.fori_loop(0, nkb, mask_block, 0)

    scale = HEAD_DIM ** -0.5
    for g in range(N_KV_HEADS):
        qg = jnp.concatenate(
            [q_ref[:, (KV_REP * g + r) * HEAD_DIM:(KV_REP * g + r + 1) * HEAD_DIM] for r in range(KV_REP)], axis=0)
        m_ref[...] = jnp.full(m_ref.shape, MASK_NEG, F32)
        l_ref[...] = jnp.zeros(l_ref.shape, F32)
        acc_ref[...] = jnp.zeros(acc_ref.shape, F32)

        def block(kb, bias_idx, g=g, qg=qg):
            r0 = pl.multiple_of(kb * tk, tk)
            kblk = kv_ref[pl.ds(r0, tk), HB_K + g * HEAD_DIM:HB_K + (g + 1) * HEAD_DIM].astype(BF16)
            vblk = kv_ref[pl.ds(r0, tk), HB_V + g * HEAD_DIM:HB_V + (g + 1) * HEAD_DIM].astype(BF16)
            s = lax.dot_general(qg, kblk, (((1,), (1,)), ((), ())), preferred_element_type=F32) * scale
            s = s.reshape(KV_REP, tq, tk) + mb_ref[kb][None]
            if bias_idx is not None:
                s = s + bias_ref[bias_idx, KV_REP * g:KV_REP * (g + 1)]
            s = s.reshape(KV_REP * tq, tk)
            m_old = m_ref[...]
            m_new = jnp.maximum(m_old, s.max(axis=-1, keepdims=True))
            p = jnp.exp(s - _lane_tile(m_new, tk))
            alpha = jnp.exp(m_old - m_new)
            l_ref[...] = alpha * l_ref[...] + p.sum(axis=-1, keepdims=True)
            acc_ref[...] = alpha * acc_ref[...] + jnp.dot(p.astype(BF16), vblk, preferred_element_type=F32)
            m_ref[...] = m_new

        def far(kb, carry):
            block(kb, None)
            return carry

        lax.fori_loop(0, jnp.maximum(i - 1, 0), far, 0)

        @pl.when(i >= 1)
        def _():
            block(i - 1, 1)

        block(i, 0)
        out = acc_ref[...] / l_ref[...]
        for r in range(KV_REP):
            h = KV_REP * g + r
            o_ref[:, h * HEAD_DIM:(h + 1) * HEAD_DIM] = out[r * tq:(r + 1) * tq].astype(o_ref.dtype)


def _attn_prompt(ha, hb, bias_tiles, batch, seq, tq):
    nq = seq // tq
    k_sel = min(TOPK_MAX, seq // 4)
    body = functools.partial(_attn_prompt_body, tq=tq, k_sel=k_sel)
    return pl.pallas_call(
        body,
        grid=(batch, nq),
        in_specs=[pl.BlockSpec((tq, QKV_W), lambda b, i: (b * nq + i, HA_Q // QKV_W)),
                  pl.BlockSpec((tq, IQ_W), lambda b, i: (b * nq + i, HA_IQ // IQ_W)),
                  pl.BlockSpec((tq, LANES), lambda b, i: (b * nq + i, HB_IW // LANES)),
                  pl.BlockSpec((seq, 2 * KV_W), lambda b, i: (b, 0), pipeline_mode=pl.Buffered(1)),
                  pl.BlockSpec((seq, LANES), lambda b, i: (b, HB_IK // LANES), pipeline_mode=pl.Buffered(1)),
                  pl.BlockSpec((2, N_HEADS, tq, tq), lambda b, i: (0, 0, 0, 0), pipeline_mode=pl.Buffered(1))],
        out_specs=pl.BlockSpec((tq, QKV_W), lambda b, i: (b * nq + i, 0)),
        out_shape=jax.ShapeDtypeStruct((batch * seq, QKV_W), BF16),
        scratch_shapes=[pltpu.VMEM((nq, tq, tq), I32),
                        pltpu.VMEM((nq, tq, tq), F32),
                        pltpu.VMEM((IDX_HEADS, tq, LANES), F32),
                        pltpu.VMEM((KV_REP * tq, LANES), F32),
                        pltpu.VMEM((KV_REP * tq, LANES), F32),
                        pltpu.VMEM((KV_REP * tq, HEAD_DIM), F32)],
        compiler_params=_cp(("arbitrary", "arbitrary")),
        name="attn_prompt",
    )(ha, ha, hb, hb, hb, bias_tiles)


def _page_spec(block, layer, group, slot):
    return pl.BlockSpec((None, None) + block, lambda b, p, pt: (layer, pt[b, p * group + slot], 0, 0))


def _idx_sample_body(pt_ref, iq_ref, iw_ref, *refs, n_tok, group):
    pages, o_ref = refs[:group], refs[group]
    ikt = jnp.concatenate([r[...].astype(BF16) for r in pages], axis=1)
    s = jnp.dot(iq_ref[0], ikt, preferred_element_type=F32)
    r = jnp.maximum(s, 0.0) * _lane_tile(iw_ref[0], group * PAGE_SIZE)
    tok_scores = r.reshape(n_tok, IDX_HEADS, group * PAGE_SIZE).sum(axis=1)
    for t in range(group):
        o_ref[0, t] = tok_scores[:, t * PAGE_SIZE:(t + 1) * PAGE_SIZE]


def _idx_sample(page_table, iq_rows, iw_rows, cache_ik_t, layer, n_tok, group):
    db, n_pages = page_table.shape
    rows = n_tok * IDX_HEADS
    body = functools.partial(_idx_sample_body, n_tok=n_tok, group=group)
    grid_spec = pltpu.PrefetchScalarGridSpec(
        num_scalar_prefetch=1,
        grid=(db, n_pages // group),
        in_specs=[pl.BlockSpec((1, rows, IDX_DIM), lambda b, p, pt: (b, 0, 0)),
                  pl.BlockSpec((1, rows, LANES), lambda b, p, pt: (b, 0, 0))]
        + [_page_spec((IDX_DIM, PAGE_SIZE), layer, group, t) for t in range(group)],
        out_specs=pl.BlockSpec((1, group, n_tok, PAGE_SIZE), lambda b, p, pt: (b, p, 0, 0)),
    )
    return pl.pallas_call(
        body,
        grid_spec=grid_spec,
        out_shape=jax.ShapeDtypeStruct((db, n_pages, n_tok, PAGE_SIZE), F32),
        compiler_params=_cp(("arbitrary", "arbitrary")),
        name="idx_sample",
    )(page_table, iq_rows, iw_rows, *([cache_ik_t] * group))


def _attn_sample_body(pt_ref, sc_ref, scn_ref, q_ref, kn_ref, vn_ref, bl_ref, bn_ref, *refs,
                      n_tok, n_pages, k_sel, group):
    kp_refs, vp_refs = refs[:group], refs[group:2 * group]
    o_ref, thr_ref, m_ref, l_ref, acc_ref = refs[2 * group:]
    p = pl.program_id(1)
    n_steps = n_pages // group
    grp = KV_REP * n_tok
    width = group * PAGE_SIZE
    scale = HEAD_DIM ** -0.5
    r_i = lax.broadcasted_iota(I32, (n_tok, LANES), 0)
    c_i = lax.broadcasted_iota(I32, (n_tok, LANES), 1)
    keys_new = jnp.where(c_i <= r_i, _sortable_key(scn_ref[0]), jnp.int32(INT_MIN))

    @pl.when(p == 0)
    def _():
        keys = _sortable_key(sc_ref[0])

        def count_ge(cand):
            cand_b = jnp.broadcast_to(cand, (n_tok, LANES))
            c = jnp.where(keys >= cand_b[None], 1.0, 0.0).sum(axis=0) + jnp.where(keys_new >= cand_b, 1.0, 0.0)
            return c.sum(axis=-1, keepdims=True)

        thr = _kth_largest_key(count_ge, n_tok, float(k_sel))
        thr_ref[...] = jnp.broadcast_to(jnp.maximum(thr, jnp.int32(INT_MIN + 1)), (n_tok, LANES))
        m_ref[...] = jnp.full(m_ref.shape, MASK_NEG, F32)
        l_ref[...] = jnp.zeros(l_ref.shape, F32)
        acc_ref[...] = jnp.zeros(acc_ref.shape, F32)

    thr_b = thr_ref[...]

    def update(g, s, vblk):
        sl = slice(g * grp, (g + 1) * grp)
        m_old = m_ref[sl, :]
        m_new = jnp.maximum(m_old, s.max(axis=-1, keepdims=True))
        pr = jnp.exp(s - _lane_tile(m_new, s.shape[1]))
        alpha = jnp.exp(m_old - m_new)
        l_ref[sl, :] = alpha * l_ref[sl, :] + pr.sum(axis=-1, keepdims=True)
        acc_ref[sl, :] = alpha * acc_ref[sl, :] + jnp.dot(pr.astype(BF16), vblk, preferred_element_type=F32)
        m_ref[sl, :] = m_new

    def group_rows(mask_tok):
        return jnp.concatenate([mask_tok] * KV_REP, axis=0)

    base = p * group
    mb = jnp.concatenate(
        [jnp.where(_sortable_key(sc_ref[0, base + t]) >= thr_b, 0.0, MASK_NEG) for t in range(group)], axis=1)
    mb_g = group_rows(mb)
    is_last = p == n_steps - 1
    zeros_head = jnp.zeros((grp, width - PAGE_SIZE), F32)
    for g in range(N_KV_HEADS):
        qg = q_ref[0, g * grp:(g + 1) * grp, :]
        kg = jnp.concatenate([r[pl.ds(g, PAGE_SIZE, stride=N_KV_HEADS), :].astype(BF16) for r in kp_refs], axis=0)
        vg = jnp.concatenate([r[pl.ds(g, PAGE_SIZE, stride=N_KV_HEADS), :].astype(BF16) for r in vp_refs], axis=0)
        s = lax.dot_general(qg, kg, (((1,), (1,)), ((), ())), preferred_element_type=F32) * scale
        near = jnp.where(is_last, bl_ref[g * grp:(g + 1) * grp, :], 0.0)
        s = s + mb_g + (near if group == 1 else jnp.concatenate([zeros_head, near], axis=1))
        update(g, s, vg)

    @pl.when(is_last)
    def _():
        mbn_g = group_rows(jnp.where(keys_new >= thr_b, 0.0, MASK_NEG))
        for g in range(N_KV_HEADS):
            qg = q_ref[0, g * grp:(g + 1) * grp, :]
            kblk = kn_ref[0, :, g * HEAD_DIM:(g + 1) * HEAD_DIM].astype(BF16)
            vblk = vn_ref[0, :, g * HEAD_DIM:(g + 1) * HEAD_DIM].astype(BF16)
            s = lax.dot_general(qg, kblk, (((1,), (1,)), ((), ())), preferred_element_type=F32) * scale
            s = s + mbn_g + bn_ref[g * grp:(g + 1) * grp, :]
            update(g, s, vblk)
        o_ref[0] = acc_ref[...] / l_ref[...]


def _attn_sample(page_table, scores, scores_new, q_rows, k_new, v_new, bias_last, bias_new,
                 cache_k, cache_v, layer, n_tok, group):
    db, n_pages = page_table.shape
    past = n_pages * PAGE_SIZE
    rows = N_HEADS * n_tok
    k_sel = min(TOPK_MAX, (past + n_tok) // 4)
    body = functools.partial(_attn_sample_body, n_tok=n_tok, n_pages=n_pages, k_sel=k_sel, group=group)
    page = (PAGE_SIZE * N_KV_HEADS, HEAD_DIM)
    grid_spec = pltpu.PrefetchScalarGridSpec(
        num_scalar_prefetch=1,
        grid=(db, n_pages // group),
        in_specs=[pl.BlockSpec((1, n_pages, n_tok, PAGE_SIZE), lambda b, p, pt: (b, 0, 0, 0)),
                  pl.BlockSpec((1, n_tok, LANES), lambda b, p, pt: (b, 0, 0)),
                  pl.BlockSpec((1, rows, HEAD_DIM), lambda b, p, pt: (b, 0, 0)),
                  pl.BlockSpec((1, LANES, KV_W), lambda b, p, pt: (b, 0, 0)),
                  pl.BlockSpec((1, LANES, KV_W), lambda b, p, pt: (b, 0, 0)),
                  pl.BlockSpec((rows, LANES), lambda b, p, pt: (0, 0)),
                  pl.BlockSpec((rows, LANES), lambda b, p, pt: (0, 0))]
        + [_page_spec(page, layer, group, t) for t in range(group)]
        + [_page_spec(page, layer, group, t) for t in range(group)],
        out_specs=pl.BlockSpec((1, rows, HEAD_DIM), lambda b, p, pt: (b, 0, 0)),
        scratch_shapes=[pltpu.VMEM((n_tok, LANES), I32),
                        pltpu.VMEM((rows, LANES), F32),
                        pltpu.VMEM((rows, LANES), F32),
                        pltpu.VMEM((rows, HEAD_DIM), F32)],
    )
    return pl.pallas_call(
        body,
        grid_spec=grid_spec,
        out_shape=jax.ShapeDtypeStruct((db, rows, HEAD_DIM), F32),
        compiler_params=_cp(("arbitrary", "arbitrary")),
        name="attn_sample",
    )(page_table, scores, scores_new, q_rows, k_new, v_new, bias_last, bias_new,
      *([cache_k] * group), *([cache_v] * group))


def _idx_new_body(iq_ref, iw_ref, ik_ref, o_ref, *, n_tok):
    for b in range(iq_ref.shape[0]):
        s = lax.dot_general(iq_ref[b], ik_ref[b].astype(BF16), (((1,), (1,)), ((), ())),
                            preferred_element_type=F32)
        r = jnp.maximum(s, 0.0) * iw_ref[b]
        o_ref[b] = r.reshape(n_tok, IDX_HEADS, LANES).sum(axis=1)


def _idx_new(iq_rows, iw_rows, ik_new, n_tok):
    db = iq_rows.shape[0]
    return pl.pallas_call(
        functools.partial(_idx_new_body, n_tok=n_tok),
        out_shape=jax.ShapeDtypeStruct((db, n_tok, LANES), F32),
        name="idx_new",
    )(iq_rows, iw_rows, ik_new)


HALO = CONV_WIDTH - 1
HALO_PAD = 32


def _conv_prompt_body(glu_ref, cw_ref, cb_ref, g_ref, b_ref, o_ref, st_ref, hp_ref, c_ref, *, ts, ns):
    s_idx = pl.program_id(1)

    @pl.when(s_idx == 0)
    def _():
        hp_ref[0:HALO_PAD, :] = jnp.zeros((HALO_PAD, CONV_DIM), F32)

    a = glu_ref[:, :CONV_DIM].astype(F32)
    gt = glu_ref[:, CONV_DIM:].astype(F32)
    hp_ref[HALO_PAD:HALO_PAD + ts, :] = a * _sigmoid(gt)
    off = HALO_PAD - HALO
    for c in range(CONV_DIM // LANES):
        cs = slice(c * LANES, (c + 1) * LANES)
        acc = jnp.zeros((ts, LANES), F32)
        for w in range(CONV_WIDTH):
            acc = acc + hp_ref[off + w:off + w + ts, cs] * cw_ref[w:w + 1, cs]
        c_ref[:, cs] = acc + cb_ref[:, cs]
    y = _ln_rows(c_ref[...], g_ref[...], b_ref[...])
    o_ref[...] = (y * _sigmoid(y)).astype(o_ref.dtype)
    tail = hp_ref[ts + off:ts + HALO_PAD, :]

    @pl.when(s_idx == ns - 1)
    def _():
        st_ref[0] = tail

    hp_ref[off:HALO_PAD, :] = tail


def _conv_prompt(ha, conv_w, conv_b, ln_g, ln_b, batch, seq, ts):
    ns = seq // ts
    body = functools.partial(_conv_prompt_body, ts=ts, ns=ns)
    vec = pl.BlockSpec((1, CONV_DIM), lambda b, s: (0, 0))
    return pl.pallas_call(
        body,
        grid=(batch, ns),
        in_specs=[pl.BlockSpec((ts, 2 * CONV_DIM), lambda b, s: (b * ns + s, HA_GLU // (2 * CONV_DIM))),
                  pl.BlockSpec((CONV_WIDTH, CONV_DIM), lambda b, s: (0, 0)), vec, vec, vec],
        out_specs=[pl.BlockSpec((ts, CONV_DIM), lambda b, s: (b * ns + s, 0)),
                   pl.BlockSpec((1, HALO, CONV_DIM), lambda b, s: (b, 0, 0))],
        out_shape=[jax.ShapeDtypeStruct((batch * seq, CONV_DIM), BF16),
                   jax.ShapeDtypeStruct((batch, HALO, CONV_DIM), F32)],
        scratch_shapes=[pltpu.VMEM((HALO_PAD + ts, CONV_DIM), F32), pltpu.VMEM((ts, CONV_DIM), F32)],
        compiler_params=_cp(("arbitrary", "arbitrary")),
        name="conv_prompt",
    )(ha, conv_w, conv_b, ln_g, ln_b)


def _sgu_prompt_body(uv_ref, g_ref, b_ref, w_ref, bs_ref, o_ref, *, ts):
    u = uv_ref[:, :SGU_DIM].astype(F32)
    vn = _ln_rows(uv_ref[:, SGU_DIM:].astype(F32), g_ref[...], b_ref[...]).astype(BF16)
    r_i = lax.broadcasted_iota(I32, (CHUNK, CHUNK), 0)
    c_i = lax.broadcasted_iota(I32, (CHUNK, CHUNK), 1)
    gd = SGU_DIM // SGU_GROUPS
    for g in range(SGU_GROUPS):
        wm = jnp.where(c_i <= r_i, w_ref[g], 0.0).astype(BF16)
        bias = bs_ref[:, g:g + 1]
        for c in range(ts // CHUNK):
            rs = slice(c * CHUNK, (c + 1) * CHUNK)
            gs = slice(g * gd, (g + 1) * gd)
            mixed = jnp.dot(wm, vn[rs, gs], preferred_element_type=F32) + bias
            o_ref[rs, gs] = (u[rs, gs] * mixed).astype(o_ref.dtype)


def _sgu_prompt(ha, ln_g, ln_b, w_s, b_s_t, batch, seq, ts):
    ns = seq // ts
    vec = pl.BlockSpec((1, SGU_DIM), lambda b, s: (0, 0))
    return pl.pallas_call(
        functools.partial(_sgu_prompt_body, ts=ts),
        grid=(batch, ns),
        in_specs=[pl.BlockSpec((ts, 2 * SGU_DIM), lambda b, s: (b * ns + s, HA_UV // (2 * SGU_DIM))),
                  vec, vec,
                  pl.BlockSpec((SGU_GROUPS, CHUNK, CHUNK), lambda b, s: (0, 0, 0)),
                  pl.BlockSpec((CHUNK, SGU_GROUPS), lambda b, s: (0, 0))],
        out_specs=pl.BlockSpec((ts, SGU_DIM), lambda b, s: (b * ns + s, 0)),
        out_shape=jax.ShapeDtypeStruct((batch * seq, SGU_DIM), BF16),
        compiler_params=_cp(("arbitrary", "arbitrary")),
        name="sgu_prompt",
    )(ha, ln_g, ln_b, w_s, b_s_t)


def _mix_sample_body(glu_ref, uv_ref, st_ref, cw_ref, cb_ref, cg_ref, cbt_ref, sg_ref, sb_ref, wv_ref, bv_ref,
                     co_ref, nst_ref, so_ref, vn_ref, hp_ref, *, db, n_tok):
    cw = cw_ref[...]
    for b in range(db):
        rs = slice(b * n_tok, (b + 1) * n_tok)
        a = glu_ref[rs, :CONV_DIM].astype(F32)
        gt = glu_ref[rs, CONV_DIM:].astype(F32)
        hp_ref[0:HALO, :] = st_ref[b]
        hp_ref[HALO:HALO + n_tok, :] = a * _sigmoid(gt)
        rows = [jnp.sum(hp_ref[t:t + CONV_WIDTH, :] * cw, axis=0, keepdims=True) for t in range(n_tok)]
        c = jnp.concatenate(rows, axis=0) + cb_ref[...]
        y = _ln_rows(c, cg_ref[...], cbt_ref[...])
        co_ref[rs, :] = y * _sigmoid(y)
        nst_ref[b] = hp_ref[n_tok:n_tok + HALO, :]
        u = uv_ref[rs, :SGU_DIM].astype(F32)
        vn = _ln_rows(uv_ref[rs, SGU_DIM:].astype(F32), sg_ref[...], sb_ref[...])
        vn_ref[rs, :] = vn
        mixed = []
        for t in range(n_tok):
            acc = bv_ref[t:t + 1, :]
            for s in range(t + 1):
                acc = acc + wv_ref[t, s:s + 1, :] * vn[s:s + 1, :]
            mixed.append(acc)
        so_ref[rs, :] = u * jnp.concatenate(mixed, axis=0)


def _mix_sample(glu, uv, state, conv_w, conv_b, cln_g, cln_b, sln_g, sln_b, wv, bv, db, n_tok):
    rows = db * n_tok
    return pl.pallas_call(
        functools.partial(_mix_sample_body, db=db, n_tok=n_tok),
        out_shape=[jax.ShapeDtypeStruct((rows, CONV_DIM), F32),
                   jax.ShapeDtypeStruct((db, HALO, CONV_DIM), F32),
                   jax.ShapeDtypeStruct((rows, SGU_DIM), F32),
                   jax.ShapeDtypeStruct((rows, SGU_DIM), F32)],
        scratch_shapes=[pltpu.VMEM((_round_up(HALO + n_tok, 8), CONV_DIM), F32)],
        name="mix_sample",
    )(glu, uv, state, conv_w, conv_b, cln_g, cln_b, sln_g, sln_b, wv, bv)


def _router_body(x_ref, w_ref, b_ref, ti_ref, tg_ref, *, n_exp):
    logits = jnp.dot(x_ref[...], w_ref[...], preferred_element_type=F32,
                     precision=lax.Precision.HIGHEST) + b_ref[...]
    lane = lax.broadcasted_iota(I32, logits.shape, 1)
    lane_f = lane.astype(F32)
    neg = -jnp.inf
    lg = jnp.where(lane < n_exp, logits, neg)
    v1 = lg.max(axis=-1, keepdims=True)
    i1 = jnp.where(lg == v1, lane_f, float(LANES)).min(axis=-1, keepdims=True)
    lg2 = jnp.where(lane_f == i1, neg, lg)
    v2 = lg2.max(axis=-1, keepdims=True)
    i2 = jnp.where(lg2 == v2, lane_f, float(LANES)).min(axis=-1, keepdims=True)
    e = jnp.exp(v2 - v1)
    den = 1.0 + e
    ti_ref[...] = jnp.where(lane == 0, i1, jnp.where(lane == 1, i2, 0.0)).astype(I32)
    tg_ref[...] = jnp.where(lane == 0, 1.0 / den, jnp.where(lane == 1, e / den, 0.0))


def _router(x, w_pad, b_pad, n_exp, tm):
    m, d = x.shape
    return pl.pallas_call(
        functools.partial(_router_body, n_exp=n_exp),
        grid=(m // tm,),
        in_specs=[pl.BlockSpec((tm, d), lambda i: (i, 0)),
                  pl.BlockSpec((d, LANES), lambda i: (0, 0)),
                  pl.BlockSpec((1, LANES), lambda i: (0, 0))],
        out_specs=[pl.BlockSpec((tm, LANES), lambda i: (i, 0)),
                   pl.BlockSpec((tm, LANES), lambda i: (i, 0))],
        out_shape=[jax.ShapeDtypeStruct((m, LANES), I32), jax.ShapeDtypeStruct((m, LANES), F32)],
        compiler_params=_cp(("arbitrary",)),
        name="moe_router",
    )(x, w_pad, b_pad)


def _row_copy(src_hbm, dst, src_row, dst_row, sem):
    return pltpu.make_async_copy(src_hbm.at[pl.ds(src_row, 1)], dst.at[pl.ds(dst_row, 1)], sem)


def _dispatch_body(src_ref, x_hbm, o_ref, buf, sem, *, rows):
    base = pl.program_id(0) * rows

    def start(r, carry):
        _row_copy(x_hbm, buf, src_ref[base + r], r, sem).start()
        return carry

    lax.fori_loop(0, rows, start, 0)

    def wait(r, carry):
        _row_copy(x_hbm, buf, 0, r, sem).wait()
        return carry

    lax.fori_loop(0, rows, wait, 0)
    o_ref[...] = buf[...].astype(o_ref.dtype)


def _dispatch(src, x, n_slots, rows):
    d = x.shape[1]
    grid_spec = pltpu.PrefetchScalarGridSpec(
        num_scalar_prefetch=1,
        grid=(n_slots // rows,),
        in_specs=[pl.BlockSpec(memory_space=pl.ANY)],
        out_specs=pl.BlockSpec((rows, d), lambda i, s: (i, 0)),
        scratch_shapes=[pltpu.VMEM((rows, d), x.dtype), pltpu.SemaphoreType.DMA(())],
    )
    return pl.pallas_call(
        functools.partial(_dispatch_body, rows=rows),
        grid_spec=grid_spec,
        out_shape=jax.ShapeDtypeStruct((n_slots, d), BF16),
        compiler_params=_cp(("arbitrary",)),
        name="moe_dispatch",
    )(src, x)


def _tile_copy(src, dst, sem):
    return pltpu.make_async_copy(src, dst, sem)


def _moe_ffn_body(exp_ref, row0_ref, nsub_ref, jmap_ref, nzero_ref, xs_hbm, w1_ref, w3_ref, w2_ref, ys_hbm,
                  xbuf, acc, w1b, w3b, w2b, sem, *, nj, sub):
    w = pl.program_id(0)
    j = pl.program_id(1)
    nsub = nsub_ref[w]
    row0 = row0_ref[w]
    nzero = nzero_ref[w]

    def zero_copy(t):
        r = pl.multiple_of(row0 + t * sub, sub)
        return _tile_copy(acc.at[pl.ds(0, sub)], ys_hbm.at[pl.ds(r, sub)], sem)

    @pl.when(jnp.logical_and(j == 0, nzero > 0))
    def _():
        acc[0:sub, :] = jnp.zeros((sub, acc.shape[1]), F32)

        def zs(t, carry):
            zero_copy(t).start()
            return carry

        def zw(t, carry):
            zero_copy(t).wait()
            return carry

        lax.fori_loop(0, nzero, zs, 0)
        lax.fori_loop(0, nzero, zw, 0)

    def in_copy(t):
        r = pl.multiple_of(t * sub, sub)
        return _tile_copy(xs_hbm.at[pl.ds(pl.multiple_of(row0 + r, sub), sub)], xbuf.at[pl.ds(r, sub)], sem)

    def out_copy(t):
        r = pl.multiple_of(t * sub, sub)
        return _tile_copy(acc.at[pl.ds(r, sub)], ys_hbm.at[pl.ds(pl.multiple_of(row0 + r, sub), sub)], sem)

    def for_tiles(fn):
        def it(t, carry):
            fn(t)
            return carry
        lax.fori_loop(0, nsub, it, 0)

    @pl.when(j == 0)
    def _():
        for_tiles(lambda t: in_copy(t).start())
        for_tiles(lambda t: in_copy(t).wait())

    @pl.when(nsub > 0)
    def _():
        w1b[...] = w1_ref[...].astype(BF16)
        w3b[...] = w3_ref[...].astype(BF16)
        w2b[...] = w2_ref[...].astype(BF16)

    def tile(t):
        r = pl.multiple_of(t * sub, sub)
        x = xbuf[pl.ds(r, sub), :]
        a = jnp.dot(x, w1b[...], preferred_element_type=F32)
        c = jnp.dot(x, w3b[...], preferred_element_type=F32)
        h = (a * _sigmoid(a) * c).astype(BF16)
        y = jnp.dot(h, w2b[...], preferred_element_type=F32)

        @pl.when(j == 0)
        def _():
            acc[pl.ds(r, sub), :] = y

        @pl.when(j > 0)
        def _():
            acc[pl.ds(r, sub), :] += y

    for_tiles(tile)

    @pl.when(j == nj - 1)
    def _():
        for_tiles(lambda t: out_copy(t).start())
        for_tiles(lambda t: out_copy(t).wait())


def _moe_ffn(item_exp, item_row0, item_nsub, item_jlast, item_nzero, xs, w1, w3, w2, layer, n_items, super_rows,
             sub, tf):
    n_slots, d = xs.shape
    f = w1.shape[3]
    nj = f // tf

    def jj(w, j, jl):
        return jnp.where(jl[w] > 0, j, nj - 1)

    grid_spec = pltpu.PrefetchScalarGridSpec(
        num_scalar_prefetch=5,
        grid=(n_items, nj),
        in_specs=[pl.BlockSpec(memory_space=pl.ANY),
                  pl.BlockSpec((None, None, d, tf), lambda w, j, ex, r0, ns, jl, nz: (layer, ex[w], 0, jj(w, j, jl))),
                  pl.BlockSpec((None, None, d, tf), lambda w, j, ex, r0, ns, jl, nz: (layer, ex[w], 0, jj(w, j, jl))),
                  pl.BlockSpec((None, None, tf, d), lambda w, j, ex, r0, ns, jl, nz: (layer, ex[w], jj(w, j, jl), 0))],
        out_specs=pl.BlockSpec(memory_space=pl.ANY),
        scratch_shapes=[pltpu.VMEM((super_rows, d), BF16),
                        pltpu.VMEM((super_rows, d), F32),
                        pltpu.VMEM((d, tf), BF16),
                        pltpu.VMEM((d, tf), BF16),
                        pltpu.VMEM((tf, d), BF16),
                        pltpu.SemaphoreType.DMA(())],
    )
    return pl.pallas_call(
        functools.partial(_moe_ffn_body, nj=nj, sub=sub),
        grid_spec=grid_spec,
        out_shape=jax.ShapeDtypeStruct((n_slots, d), F32),
        compiler_params=_cp(("arbitrary", "arbitrary")),
        name="moe_ffn",
    )(item_exp, item_row0, item_nsub, item_jlast, item_nzero, xs, w1, w3, w2)


def _combine_body(pos_ref, ys_hbm, tg_ref, x_ref, g_ref, b_ref, of_ref, ob_ref, buf, sem, *, tb, alpha):
    base = pl.program_id(0) * tb

    def start(r, carry):
        for k in range(TOP_K_EXPERTS):
            _row_copy(ys_hbm, buf.at[k], pos_ref[TOP_K_EXPERTS * (base + r) + k], r, sem).start()
        return carry

    lax.fori_loop(0, tb, start, 0)

    def wait(r, carry):
        for k in range(TOP_K_EXPERTS):
            _row_copy(ys_hbm, buf.at[k], 0, r, sem).wait()
        return carry

    lax.fori_loop(0, tb, wait, 0)
    f = tg_ref[:, 0:1] * buf[0] + tg_ref[:, 1:2] * buf[1]
    y = _ln_rows(alpha * x_ref[...] + f, g_ref[...], b_ref[...])
    of_ref[...] = y
    ob_ref[...] = y.astype(BF16)


def _combine_ln(pos, ys, tg, x, g, b, alpha, tb):
    m, d = x.shape
    grid_spec = pltpu.PrefetchScalarGridSpec(
        num_scalar_prefetch=1,
        grid=(m // tb,),
        in_specs=[pl.BlockSpec(memory_space=pl.ANY),
                  pl.BlockSpec((tb, LANES), lambda i, p: (i, 0)),
                  pl.BlockSpec((tb, d), lambda i, p: (i, 0)),
                  pl.BlockSpec((1, d), lambda i, p: (0, 0)),
                  pl.BlockSpec((1, d), lambda i, p: (0, 0))],
        out_specs=[pl.BlockSpec((tb, d), lambda i, p: (i, 0)),
                   pl.BlockSpec((tb, d), lambda i, p: (i, 0))],
        scratch_shapes=[pltpu.VMEM((TOP_K_EXPERTS, tb, d), F32), pltpu.SemaphoreType.DMA(())],
    )
    return pl.pallas_call(
        functools.partial(_combine_body, tb=tb, alpha=alpha),
        grid_spec=grid_spec,
        out_shape=[jax.ShapeDtypeStruct((m, d), F32), jax.ShapeDtypeStruct((m, d), BF16)],
        compiler_params=_cp(("arbitrary",)),
        name="moe_combine_ln",
    )(pos, ys, tg, x, g, b)


def _moe_plan(top_i, n_exp, n_slots, n_items, super_rows, sub):
    a = top_i.shape[0] * TOP_K_EXPERTS
    e_flat = top_i.reshape(-1)
    onehot = (e_flat[:, None] == jnp.arange(n_exp, dtype=I32)[None, :]).astype(I32)
    csum = jnp.cumsum(onehot, axis=0)
    rank = jnp.take_along_axis(csum, e_flat[:, None], axis=1)[:, 0] - 1
    counts = csum[-1]
    padded = (counts + sub - 1) // sub * sub
    gstart = jnp.cumsum(padded) - padded
    pos = (gstart[e_flat] + rank).astype(I32)
    src = jnp.zeros((n_slots,), I32).at[pos].set(jnp.arange(a, dtype=I32) // TOP_K_EXPERTS)
    n_super = (padded + super_rows - 1) // super_rows
    iend = jnp.cumsum(n_super)
    istart = iend - n_super
    total = iend[-1]
    wi = jnp.arange(n_items, dtype=I32)
    valid = wi < total
    e_w = jnp.minimum(jnp.searchsorted(iend, wi, side="right"), n_exp - 1).astype(I32)
    last_e = e_w[jnp.maximum(total - 1, 0)]
    e_w = jnp.where(valid, e_w, last_e)
    k_w = wi - istart[e_w]
    nsub = jnp.where(valid, jnp.minimum(super_rows, padded[e_w] - k_w * super_rows) // sub, 0).astype(I32)
    z0 = jnp.sum(padded) + (wi - total) * super_rows
    nzero = jnp.where(valid, 0, jnp.clip((n_slots - z0) // sub, 0, super_rows // sub)).astype(I32)
    row0 = jnp.where(valid, gstart[e_w] + k_w * super_rows, jnp.where(nzero > 0, z0, 0)).astype(I32)
    return pos, src, e_w, row0, nsub, valid.astype(I32), nzero


def _rel_bucket(dist):
    n = jnp.maximum(dist, 0)
    max_exact = REL_BUCKETS // 2
    nf = jnp.maximum(n, 1).astype(F32)
    large = max_exact + (jnp.log(nf / max_exact) / math.log(REL_MAX_DIST / max_exact)
                         * (REL_BUCKETS - max_exact)).astype(I32)
    large = jnp.minimum(large, REL_BUCKETS - 1)
    return jnp.where(n < max_exact, n, large)


def _shifted_bias(rel_bias, dist):
    t = rel_bias[_rel_bucket(dist)] - rel_bias[REL_BUCKETS - 1]
    t = jnp.where((dist >= 0)[..., None], t, 0.0)
    return jnp.moveaxis(t, -1, 0)


def _row_tile(m_rows, n_tiles):
    return _round_up(-(-m_rows // n_tiles), 32)


def kernel(x_prompt, x_sample, cache_k, cache_v, cache_ik, state_conv, page_table, w_in, conv_w, conv_b,
           conv_ln_g, conv_ln_b, sgu_ln_g, sgu_ln_b, sgu_w, sgu_b, w_pa, w_pb, w_pc, w_out, ln1_g, ln1_b,
           ln2_g, ln2_b, rel_bias, ffn_w1, ffn_w3, ffn_w2, moe_router, moe_router_b, moe_w1, moe_w3, moe_w2):
    batch, seq, d_model = x_prompt.shape
    db, n_tok, _ = x_sample.shape
    depth = w_in.shape[0]
    n_pool = cache_k.shape[1]
    n_exp = moe_router.shape[2]
    mp, ms = batch * seq, db * n_tok
    tm = _row_tile(mp + ms, N_ROW_TILES)
    m_all = tm * N_ROW_TILES
    alpha = (2 * depth) ** 0.25
    tq = min(256, seq)

    xf = jnp.concatenate([x_prompt.reshape(mp, d_model), x_sample.reshape(ms, d_model),
                          jnp.zeros((m_all - mp - ms, d_model), F32)], axis=0)
    xb = xf.astype(BF16)

    ar = jnp.arange(tq, dtype=I32)
    d0 = ar[:, None] - ar[None, :]
    bias_tiles = jnp.stack([_shifted_bias(rel_bias, d0), _shifted_bias(rel_bias, d0 + tq)])
    tok = jnp.arange(n_tok, dtype=I32)
    lane = jnp.arange(LANES, dtype=I32)
    d_last = PAGE_SIZE + tok[:, None] - lane[None, :]
    bias_last = _shifted_bias(rel_bias, d_last).reshape(N_HEADS * n_tok, LANES)
    bias_new = _shifted_bias(rel_bias, tok[:, None] - lane[None, :]).reshape(N_HEADS * n_tok, LANES)

    ck = cache_k.reshape(depth, n_pool, PAGE_SIZE * N_KV_HEADS, HEAD_DIM)
    cv = cache_v.reshape(depth, n_pool, PAGE_SIZE * N_KV_HEADS, HEAD_DIM)
    cik_t = jnp.swapaxes(cache_ik, 2, 3)
    n_pages = page_table.shape[1]
    idx_group = math.gcd(n_pages, IDX_PAGE_GROUP)
    attn_group = math.gcd(n_pages, ATTN_PAGE_GROUP)
    sp = [int(v) for v in np.cumsum([QKV_W, KV_W, KV_W, IQ_W, IDX_DIM, IDX_HEADS, 2 * CONV_DIM, 2 * SGU_DIM])]

    n_slots = _round_up(TOP_K_EXPERTS * m_all + n_exp * (MOE_SUB - 1), MOE_SUB)
    n_items = -(-n_slots // MOE_SUPER) + n_exp

    outs = {k: [] for k in ("k", "v", "ik", "conv_p", "conv_s", "sgu_s")}
    for l in range(depth):
        wl = w_in[l]
        w_q, w_k, w_v, w_iq = wl[:, :sp[0]], wl[:, sp[0]:sp[1]], wl[:, sp[1]:sp[2]], wl[:, sp[2]:sp[3]]
        w_ik, w_iw = wl[:, sp[3]:sp[4]], wl[:, sp[4]:sp[5]]
        w_glu, w_uv, w_gate = wl[:, sp[5]:sp[6]], wl[:, sp[6]:sp[7]], wl[:, sp[7]:]
        wa = jnp.concatenate([w_q, w_glu, w_uv, w_gate, w_iq], axis=1).astype(BF16)
        wb = jnp.concatenate([w_k, w_v, w_ik, w_ik, w_iw,
                              jnp.zeros((d_model, LANES - IDX_HEADS), F32)], axis=1).astype(BF16)
        ha = _matmul(xb, wa, BF16, tm, 1024)
        hb = _matmul(xb, wb, F32, tm, HB_W)

        a_p = _attn_prompt(ha, hb, bias_tiles, batch, seq, tq)
        c_p, conv_state_p = _conv_prompt(ha, conv_w[l], conv_b[l][None], conv_ln_g[l][None], conv_ln_b[l][None],
                                         batch, seq, min(256, seq))
        s_p = _sgu_prompt(ha, sgu_ln_g[l][None], sgu_ln_b[l][None], sgu_w[l], sgu_b[l].T, batch, seq,
                          min(512, seq))

        has, hbs = ha[mp:mp + ms], hb[mp:mp + ms]
        iq_rows = has[:, HA_IQ:HA_IQ + IQ_W].reshape(db, n_tok * IDX_HEADS, IDX_DIM)
        iw_rows = (hbs[:, HB_IW:HB_IW + IDX_HEADS] * (IDX_DIM ** -0.5 * IDX_HEADS ** -0.5)
                   ).reshape(db, n_tok * IDX_HEADS, 1)
        iw_rows = jnp.broadcast_to(iw_rows, (db, n_tok * IDX_HEADS, LANES))
        scores = _idx_sample(page_table, iq_rows, iw_rows, cik_t, l, n_tok, idx_group)
        pad_rows = LANES - n_tok
        ik_new = jnp.pad(hbs[:, HB_IK:HB_IK + IDX_DIM].reshape(db, n_tok, IDX_DIM), ((0, 0), (0, pad_rows), (0, 0)))
        scores_new = _idx_new(iq_rows, iw_rows, ik_new, n_tok)
        q_rows = has[:, HA_Q:HA_Q + QKV_W].reshape(db, n_tok, N_HEADS, HEAD_DIM).transpose(0, 2, 1, 3)
        q_rows = q_rows.reshape(db, N_HEADS * n_tok, HEAD_DIM)
        k_new = jnp.pad(hbs[:, HB_K:HB_K + KV_W].reshape(db, n_tok, KV_W), ((0, 0), (0, pad_rows), (0, 0)))
        v_new = jnp.pad(hbs[:, HB_V:HB_V + KV_W].reshape(db, n_tok, KV_W), ((0, 0), (0, pad_rows), (0, 0)))
        a_s = _attn_sample(page_table, scores, scores_new, q_rows, k_new, v_new, bias_last, bias_new,
                           ck, cv, l, n_tok, attn_group)
        a_s = a_s.reshape(db, N_HEADS, n_tok, HEAD_DIM).transpose(0, 2, 1, 3).reshape(ms, QKV_W)
        gd = SGU_DIM // SGU_GROUPS
        wv = jnp.repeat(sgu_w[l][:, :n_tok, :n_tok].transpose(1, 2, 0), gd, axis=-1)
        bv = jnp.repeat(sgu_b[l][:, :n_tok].T, gd, axis=-1)
        c_s, conv_state_s, s_s, vn_s = _mix_sample(
            has[:, HA_GLU:HA_GLU + 2 * CONV_DIM], has[:, HA_UV:HA_UV + 2 * SGU_DIM], state_conv[l],
            conv_w[l], conv_b[l][None], conv_ln_g[l][None], conv_ln_b[l][None],
            sgu_ln_g[l][None], sgu_ln_b[l][None], wv, bv, db, n_tok)

        def join(p, s):
            return jnp.concatenate([p, s.astype(BF16), jnp.zeros((m_all - mp - ms, p.shape[1]), BF16)], axis=0)

        merged = _merge(join(a_p, a_s), join(c_p, c_s), join(s_p, s_s), ha, w_pa, w_pb, w_pc, l, tm, 512)
        x1f, x1b = _mm_ln(merged, w_out, l, xf, ln1_g[l][None], ln1_b[l][None], alpha, tm // 2, 512)

        j = l // 2
        if l % 2 == 0:
            h = _swiglu_up(x1b, ffn_w1, ffn_w3, j, tm, 512)
            xf, xb = _mm_ln(h, ffn_w2, j, x1f, ln2_g[l][None], ln2_b[l][None], alpha, tm // 2, 256)
        else:
            w_r = jnp.pad(moe_router[j], ((0, 0), (0, LANES - n_exp)))
            b_r = jnp.pad(moe_router_b[j], (0, LANES - n_exp))[None]
            ti, tg = _router(x1f, w_r, b_r, n_exp, tm // 2)
            pos, src, it_e, it_r0, it_ns, it_valid, it_nz = _moe_plan(ti[:, :TOP_K_EXPERTS], n_exp, n_slots,
                                                                      n_items, MOE_SUPER, MOE_SUB)
            xs = _dispatch(src, x1f, n_slots, MOE_SUB)
            ys = _moe_ffn(it_e, it_r0, it_ns, it_valid, it_nz, xs, moe_w1, moe_w3, moe_w2, j, n_items,
                          MOE_SUPER, MOE_SUB, MOE_TF)
            tb = COMBINE_TB if m_all % COMBINE_TB == 0 else 16
            xf, xb = _combine_ln(pos, ys, tg, x1f, ln2_g[l][None], ln2_b[l][None], alpha, tb)

        outs["k"].append(hb[:, HB_K:HB_K + KV_W])
        outs["v"].append(hb[:, HB_V:HB_V + KV_W])
        outs["ik"].append(hb[:, HB_IK:HB_IK + IDX_DIM])
        outs["conv_p"].append(conv_state_p)
        outs["conv_s"].append(conv_state_s)
        outs["sgu_s"].append(vn_s.reshape(db, n_tok, SGU_DIM))

    def split(name, width_shape):
        st = jnp.stack(outs[name])
        p = st[:, :mp].reshape((depth, batch, seq) + width_shape)
        s = st[:, mp:mp + ms].reshape((depth, db, n_tok) + width_shape)
        return p, s

    k_p, k_s = split("k", (N_KV_HEADS, HEAD_DIM))
    v_p, v_s = split("v", (N_KV_HEADS, HEAD_DIM))
    ik_p, ik_s = split("ik", (IDX_DIM,))
    y_prompt = xf[:mp].reshape(batch, seq, d_model)
    y_sample = xf[mp:mp + ms].reshape(db, n_tok, d_model)
    return (y_prompt, y_sample, k_p, v_p, ik_p, jnp.stack(outs["conv_p"]), k_s, v_s, ik_s,
            jnp.stack(outs["conv_s"]), jnp.stack(outs["sgu_s"]))
```

```python
import functools
import math

import jax
import jax.numpy as jnp
import numpy as np
from jax import lax
from jax.experimental import pallas as pl
from jax.experimental.pallas import tpu as pltpu

F32 = jnp.float32
BF16 = jnp.bfloat16
I32 = jnp.int32

N_HEADS = 16
HEAD_DIM = 128
N_KV_HEADS = 4
KV_REP = N_HEADS // N_KV_HEADS
IDX_HEADS = 16
IDX_DIM = 64
TOPK_MAX = 256
PAGE_SIZE = 128
CONV_DIM = 1024
CONV_WIDTH = 31
SGU_DIM = 1024
SGU_GROUPS = 8
CHUNK = 128
N_BRANCH = 3
TOP_K_EXPERTS = 2
REL_BUCKETS = 32
REL_MAX_DIST = 128
LN_EPS = 1e-5

V7X_VMEM_BYTES = 64 * 2**20
VMEM_LIMIT = V7X_VMEM_BYTES - 8 * 2**20
LANES = 128
F32_SUBLANES = 8
INT_MIN = -(2**31)
MASK_NEG = -1e30

QKV_W = N_HEADS * HEAD_DIM
KV_W = N_KV_HEADS * HEAD_DIM
IQ_W = IDX_HEADS * IDX_DIM
HA_Q, HA_GLU, HA_UV, HA_GATE, HA_IQ = 0, 2048, 4096, 6144, 12288
HA_TILE = 1024
W_Q_ROW0 = 0
W_KV_ROW0 = W_Q_ROW0 + QKV_W
W_IQ_ROW0 = W_KV_ROW0 + 2 * KV_W
W_IK_ROW0 = W_IQ_ROW0 + IQ_W
W_GLU_ROW0 = W_IK_ROW0 + IDX_DIM + IDX_HEADS
W_UV_ROW0 = W_GLU_ROW0 + 2 * CONV_DIM
W_GATE_ROW0 = W_UV_ROW0 + 2 * SGU_DIM
KV_K, KV_V = 0, KV_W
IKW_IW = IDX_DIM

N_ROW_TILES = 8
SAMPLE_TILE = 256
MOE_SUB = 256
MOE_SUPER = 2048
MOE_TF = 256
COMBINE_TB = 128
IDX_PAGE_GROUP = 32
ATTN_PAGE_GROUP = 16


def _cp(sem, vmem=VMEM_LIMIT):
    return pltpu.CompilerParams(dimension_semantics=sem, vmem_limit_bytes=vmem)


def _round_up(x, m):
    return (x + m - 1) // m * m


def _ln_rows(z, g, b):
    mu = jnp.mean(z, axis=-1, keepdims=True)
    d = z - mu
    var = jnp.mean(d * d, axis=-1, keepdims=True)
    return d * lax.rsqrt(var + LN_EPS) * g + b


def _sigmoid(x):
    return 1.0 / (1.0 + jnp.exp(-x))


def _in_proj_body(off_ref, x_ref, w_ref, o_ref, wt_ref):
    @pl.when(pl.program_id(1) == 0)
    def _():
        wt_ref[...] = w_ref[0].T.astype(BF16)

    o_ref[...] = jnp.dot(x_ref[...], wt_ref[...], preferred_element_type=F32).astype(o_ref.dtype)


def _in_proj(x, w_t, layer, row_offsets, tm, tn):
    m, k = x.shape
    n_tiles = len(row_offsets)
    grid_spec = pltpu.PrefetchScalarGridSpec(
        num_scalar_prefetch=1,
        grid=(n_tiles, m // tm),
        in_specs=[pl.BlockSpec((tm, k), lambda j, i, off: (i, 0)),
                  pl.BlockSpec((pl.Element(1), pl.Element(tn), pl.Element(k)),
                               lambda j, i, off: (layer, pl.multiple_of(off[j], F32_SUBLANES), 0))],
        out_specs=pl.BlockSpec((tm, tn), lambda j, i, off: (i, j)),
        scratch_shapes=[pltpu.VMEM((k, tn), BF16)],
    )
    return pl.pallas_call(
        _in_proj_body,
        grid_spec=grid_spec,
        out_shape=jax.ShapeDtypeStruct((m, n_tiles * tn), BF16),
        compiler_params=_cp(("arbitrary", "arbitrary")),
        name="in_proj",
    )(jnp.asarray(row_offsets, I32), x, w_t)


def _kv_proj_body(x_ref, wkv_ref, wik_ref, kv_ref, ik_ref, wkv_t, wik_t):
    @pl.when(pl.program_id(0) == 0)
    def _():
        wkv_t[...] = wkv_ref[...].T.astype(BF16)
        wik_t[...] = wik_ref[...].T.astype(BF16)

    x = x_ref[...]
    kv_ref[...] = jnp.dot(x, wkv_t[...], preferred_element_type=F32)
    ik_ref[...] = jnp.dot(x, wik_t[...], preferred_element_type=F32)


def _kv_proj(x, w_t, layer, tm):
    m, k = x.shape
    assert W_KV_ROW0 % (2 * KV_W) == 0 and W_IK_ROW0 % LANES == 0
    return pl.pallas_call(
        _kv_proj_body,
        grid=(m // tm,),
        in_specs=[pl.BlockSpec((tm, k), lambda i: (i, 0)),
                  pl.BlockSpec((None, 2 * KV_W, k), lambda i: (layer, W_KV_ROW0 // (2 * KV_W), 0)),
                  pl.BlockSpec((None, LANES, k), lambda i: (layer, W_IK_ROW0 // LANES, 0))],
        out_specs=[pl.BlockSpec((tm, 2 * KV_W), lambda i: (i, 0)),
                   pl.BlockSpec((tm, LANES), lambda i: (i, 0))],
        out_shape=[jax.ShapeDtypeStruct((m, 2 * KV_W), F32), jax.ShapeDtypeStruct((m, LANES), F32)],
        scratch_shapes=[pltpu.VMEM((k, 2 * KV_W), BF16), pltpu.VMEM((k, LANES), BF16)],
        compiler_params=_cp(("arbitrary",)),
        name="kv_proj",
    )(x, w_t, w_t)


def _mm_ln_body(x_ref, w_ref, r_ref, g_ref, b_ref, of_ref, ob_ref, *, alpha):
    z = jnp.dot(x_ref[...], w_ref[...], preferred_element_type=F32) + alpha * r_ref[...]
    y = _ln_rows(z, g_ref[...], b_ref[...])
    of_ref[...] = y
    ob_ref[...] = y.astype(BF16)


def _mm_ln(x, w, layer, resid, g, b, alpha, tm):
    m, k = x.shape
    n = w.shape[2]
    row = lambda i: (i, 0)
    return pl.pallas_call(
        functools.partial(_mm_ln_body, alpha=alpha),
        grid=(m // tm,),
        in_specs=[pl.BlockSpec((tm, k), row),
                  pl.BlockSpec((None, k, n), lambda i: (layer, 0, 0), pipeline_mode=pl.Buffered(1)),
                  pl.BlockSpec((tm, n), row),
                  pl.BlockSpec((1, n), lambda i: (0, 0)),
                  pl.BlockSpec((1, n), lambda i: (0, 0))],
        out_specs=[pl.BlockSpec((tm, n), row), pl.BlockSpec((tm, n), row)],
        out_shape=[jax.ShapeDtypeStruct((m, n), F32), jax.ShapeDtypeStruct((m, n), BF16)],
        compiler_params=_cp(("arbitrary",)),
        name="proj_ln",
    )(x, w, resid, g, b)


def _swiglu_up_body(x_ref, w1_ref, w3_ref, o_ref):
    x = x_ref[...]
    a = jnp.dot(x, w1_ref[...], preferred_element_type=F32)
    c = jnp.dot(x, w3_ref[...], preferred_element_type=F32)
    o_ref[...] = (a * _sigmoid(a) * c).astype(o_ref.dtype)


def _swiglu_up(x, w1, w3, layer, tm, tf):
    m, k = x.shape
    f = w1.shape[2]
    return pl.pallas_call(
        _swiglu_up_body,
        grid=(f // tf, m // tm),
        in_specs=[pl.BlockSpec((tm, k), lambda j, i: (i, 0)),
                  pl.BlockSpec((None, k, tf), lambda j, i: (layer, 0, j)),
                  pl.BlockSpec((None, k, tf), lambda j, i: (layer, 0, j))],
        out_specs=pl.BlockSpec((tm, tf), lambda j, i: (i, j)),
        out_shape=jax.ShapeDtypeStruct((m, f), BF16),
        compiler_params=_cp(("arbitrary", "arbitrary")),
        name="ffn_up",
    )(x, w1, w3)


def _merge_body(ap_ref, cp_ref, sp_ref, as_ref, cs_ref, ss_ref, g_ref, wa_ref, wb_ref, wc_ref, o_ref, *, n_prompt):
    i = pl.program_id(0)
    d = o_ref.shape[1]

    def merge(a_ref, c_ref, s_ref):
        pa = jnp.dot(a_ref[...], wa_ref[...], preferred_element_type=F32)
        pb = jnp.dot(c_ref[...], wb_ref[...], preferred_element_type=F32)
        pc = jnp.dot(s_ref[...], wc_ref[...], preferred_element_type=F32)
        o = (_sigmoid(g_ref[:, :d].astype(F32)) * pa + _sigmoid(g_ref[:, d:2 * d].astype(F32)) * pb
             + _sigmoid(g_ref[:, 2 * d:].astype(F32)) * pc)
        o_ref[...] = o.astype(o_ref.dtype)

    @pl.when(i < n_prompt)
    def _():
        merge(ap_ref, cp_ref, sp_ref)

    @pl.when(i >= n_prompt)
    def _():
        merge(as_ref, cs_ref, ss_ref)


def _merge(prompt, sample, ha, w_pa, w_pb, w_pc, layer, tm):
    m = ha.shape[0]
    d = w_pa.shape[2]
    n_prompt = prompt[0].shape[0] // tm
    assert prompt[0].shape[0] % tm == 0 and sample[0].shape[0] == tm and m == (n_prompt + 1) * tm
    assert HA_GATE % (N_BRANCH * d) == 0
    p_row = lambda i: (jnp.minimum(i, n_prompt - 1), 0)
    fixed = lambda i: (0, 0)
    return pl.pallas_call(
        functools.partial(_merge_body, n_prompt=n_prompt),
        grid=(n_prompt + 1,),
        in_specs=[pl.BlockSpec((tm, t.shape[1]), p_row) for t in prompt]
        + [pl.BlockSpec((tm, t.shape[1]), fixed) for t in sample]
        + [pl.BlockSpec((tm, N_BRANCH * d), lambda i: (i, HA_GATE // (N_BRANCH * d)))]
        + [pl.BlockSpec((None,) + w.shape[1:], lambda i: (layer, 0, 0), pipeline_mode=pl.Buffered(1))
           for w in (w_pa, w_pb, w_pc)],
        out_specs=pl.BlockSpec((tm, d), lambda i: (i, 0)),
        out_shape=jax.ShapeDtypeStruct((m, d), BF16),
        compiler_params=_cp(("arbitrary",)),
        name="branch_merge",
    )(*prompt, *sample, ha, w_pa, w_pb, w_pc)


def _sortable_key(score):
    bits = pltpu.bitcast(score, I32)
    return bits ^ ((bits >> 31) & jnp.int32(0x7FFFFFFF))


def _kth_largest_key(count_ge, rows, k):
    def step(it, t):
        cand = t + jnp.left_shift(jnp.int32(1), 31 - it)
        return jnp.where(count_ge(cand) >= k, cand, t)

    return lax.fori_loop(0, 32, step, jnp.full((rows, 1), INT_MIN, I32))


def _lane_tile(x, width):
    return x if width == LANES else jnp.concatenate([x] * (width // LANES), axis=1)


def _attn_prompt_body(q_ref, iq_ref, iw_ref, kv_ref, ik_ref, bias_ref, o_ref,
                      keys_ref, mb_ref, iwb_ref, m_ref, l_ref, acc_ref, *, tq, k_sel):
    i = pl.program_id(1)
    nkb = i + 1
    tk = tq
    row = lax.broadcasted_iota(I32, (tq, tk), 0)
    col = lax.broadcasted_iota(I32, (tq, tk), 1)
    low_half = lax.broadcasted_iota(I32, (tk, LANES), 1) < IDX_DIM
    iw = iw_ref[...] * (IDX_DIM ** -0.5 * IDX_HEADS ** -0.5)
    for h in range(IDX_HEADS):
        iwb_ref[h] = jnp.broadcast_to(iw[:, IKW_IW + h:IKW_IW + h + 1], (tq, LANES))
    iq_pairs = [iq_ref[:, p * LANES:(p + 1) * LANES] for p in range(IDX_HEADS // 2)]

    def idx_block(kb, carry):
        ik_lo = jnp.where(low_half, ik_ref[pl.ds(pl.multiple_of(kb * tk, tk), tk), :], 0.0)
        ik_sides = (ik_lo.astype(BF16), pltpu.roll(ik_lo, IDX_DIM, axis=1).astype(BF16))
        acc = jnp.zeros((tq, tk), F32)
        for h in range(IDX_HEADS):
            s = lax.dot_general(iq_pairs[h // 2], ik_sides[h % 2], (((1,), (1,)), ((), ())),
                                preferred_element_type=F32)
            acc = acc + jnp.maximum(s, 0.0) * _lane_tile(iwb_ref[h], tk)
        causal = (col + kb * tk) <= (row + i * tq)
        keys_ref[kb] = jnp.where(causal, _sortable_key(acc), jnp.int32(INT_MIN))
        return carry

    lax.fori_loop(0, nkb, idx_block, 0)

    def count_ge(cand):
        cand_b = jnp.broadcast_to(cand, (tq, LANES))

        def cb(kb, c):
            keys = keys_ref[kb]
            for t in range(tk // LANES):
                c = c + jnp.where(keys[:, t * LANES:(t + 1) * LANES] >= cand_b, 1.0, 0.0)
            return c

        cnt = lax.fori_loop(0, nkb, cb, jnp.zeros((tq, LANES), F32))
        return cnt.sum(axis=-1, keepdims=True)

    thr = _kth_largest_key(count_ge, tq, float(k_sel))
    thr = jnp.maximum(thr, jnp.int32(INT_MIN + 1))
    thr_b = _lane_tile(jnp.broadcast_to(thr, (tq, LANES)), tk)

    def mask_block(kb, carry):
        mb_ref[kb] = jnp.where(keys_ref[kb] >= thr_b, 0.0, MASK_NEG)
        return carry

    lax.fori_loop(0, nkb, mask_block, 0)

    scale = HEAD_DIM ** -0.5
    for g in range(N_KV_HEADS):
        qg = jnp.concatenate(
            [q_ref[:, (KV_REP * g + r) * HEAD_DIM:(KV_REP * g + r + 1) * HEAD_DIM] for r in range(KV_REP)], axis=0)
        m_ref[...] = jnp.full(m_ref.shape, MASK_NEG, F32)
        l_ref[...] = jnp.zeros(l_ref.shape, F32)
        acc_ref[...] = jnp.zeros(acc_ref.shape, F32)

        def block(kb, bias_idx, g=g, qg=qg):
            r0 = pl.multiple_of(kb * tk, tk)
            kblk = kv_ref[pl.ds(r0, tk), KV_K + g * HEAD_DIM:KV_K + (g + 1) * HEAD_DIM].astype(BF16)
            vblk = kv_ref[pl.ds(r0, tk), KV_V + g * HEAD_DIM:KV_V + (g + 1) * HEAD_DIM].astype(BF16)
            s = lax.dot_general(qg, kblk, (((1,), (1,)), ((), ())), preferred_element_type=F32) * scale
            s = s.reshape(KV_REP, tq, tk) + mb_ref[kb][None]
            if bias_idx is not None:
                s = s + bias_ref[bias_idx, KV_REP * g:KV_REP * (g + 1)]
            s = s.reshape(KV_REP * tq, tk)
            m_old = m_ref[...]
            m_new = jnp.maximum(m_old, s.max(axis=-1, keepdims=True))
            p = jnp.exp(s - _lane_tile(m_new, tk))
            alpha = jnp.exp(m_old - m_new)
            l_ref[...] = alpha * l_ref[...] + p.sum(axis=-1, keepdims=True)
            acc_ref[...] = alpha * acc_ref[...] + jnp.dot(p.astype(BF16), vblk, preferred_element_type=F32)
            m_ref[...] = m_new

        def far(kb, carry):
            block(kb, None)
            return carry

        lax.fori_loop(0, jnp.maximum(i - 1, 0), far, 0)

        @pl.when(i >= 1)
        def _():
            block(i - 1, 1)

        block(i, 0)
        out = acc_ref[...] / l_ref[...]
        for r in range(KV_REP):
            h = KV_REP * g + r
            o_ref[:, h * HEAD_DIM:(h + 1) * HEAD_DIM] = out[r * tq:(r + 1) * tq].astype(o_ref.dtype)


def _attn_prompt(ha, kv, ikw, bias_tiles, batch, seq, tq):
    nq = seq // tq
    k_sel = min(TOPK_MAX, seq // 4)
    body = functools.partial(_attn_prompt_body, tq=tq, k_sel=k_sel)
    return pl.pallas_call(
        body,
        grid=(batch, nq),
        in_specs=[pl.BlockSpec((tq, QKV_W), lambda b, i: (b * nq + i, HA_Q // QKV_W)),
                  pl.BlockSpec((tq, IQ_W), lambda b, i: (b * nq + i, HA_IQ // IQ_W)),
                  pl.BlockSpec((tq, LANES), lambda b, i: (b * nq + i, 0)),
                  pl.BlockSpec((seq, 2 * KV_W), lambda b, i: (b, 0), pipeline_mode=pl.Buffered(1)),
                  pl.BlockSpec((seq, LANES), lambda b, i: (b, 0), pipeline_mode=pl.Buffered(1)),
                  pl.BlockSpec((2, N_HEADS, tq, tq), lambda b, i: (0, 0, 0, 0), pipeline_mode=pl.Buffered(1))],
        out_specs=pl.BlockSpec((tq, QKV_W), lambda b, i: (b * nq + i, 0)),
        out_shape=jax.ShapeDtypeStruct((batch * seq, QKV_W), BF16),
        scratch_shapes=[pltpu.VMEM((nq, tq, tq), I32),
                        pltpu.VMEM((nq, tq, tq), F32),
                        pltpu.VMEM((IDX_HEADS, tq, LANES), F32),
                        pltpu.VMEM((KV_REP * tq, LANES), F32),
                        pltpu.VMEM((KV_REP * tq, LANES), F32),
                        pltpu.VMEM((KV_REP * tq, HEAD_DIM), F32)],
        compiler_params=_cp(("arbitrary", "arbitrary")),
        name="attn_prompt",
    )(ha, ha, ikw, kv, ikw, bias_tiles)


def _page_spec(block, layer, group, slot):
    return pl.BlockSpec((None, None) + block, lambda b, p, pt: (layer, pt[b, p * group + slot], 0, 0))


def _idx_sample_body(pt_ref, iq_ref, iw_ref, *refs, n_tok, group):
    pages, o_ref = refs[:group], refs[group]
    ikt = jnp.concatenate([r[...].astype(BF16) for r in pages], axis=1)
    s = jnp.dot(iq_ref[0], ikt, preferred_element_type=F32)
    r = jnp.maximum(s, 0.0) * _lane_tile(iw_ref[0], group * PAGE_SIZE)
    tok_scores = r.reshape(n_tok, IDX_HEADS, group * PAGE_SIZE).sum(axis=1)
    for t in range(group):
        o_ref[0, t] = tok_scores[:, t * PAGE_SIZE:(t + 1) * PAGE_SIZE]


def _idx_sample(page_table, iq_rows, iw_rows, cache_ik_t, layer, n_tok, group):
    db, n_pages = page_table.shape
    rows = n_tok * IDX_HEADS
    body = functools.partial(_idx_sample_body, n_tok=n_tok, group=group)
    grid_spec = pltpu.PrefetchScalarGridSpec(
        num_scalar_prefetch=1,
        grid=(db, n_pages // group),
        in_specs=[pl.BlockSpec((1, rows, IDX_DIM), lambda b, p, pt: (b, 0, 0)),
                  pl.BlockSpec((1, rows, LANES), lambda b, p, pt: (b, 0, 0))]
        + [_page_spec((IDX_DIM, PAGE_SIZE), layer, group, t) for t in range(group)],
        out_specs=pl.BlockSpec((1, group, n_tok, PAGE_SIZE), lambda b, p, pt: (b, p, 0, 0)),
    )
    return pl.pallas_call(
        body,
        grid_spec=grid_spec,
        out_shape=jax.ShapeDtypeStruct((db, n_pages, n_tok, PAGE_SIZE), F32),
        compiler_params=_cp(("arbitrary", "arbitrary")),
        name="idx_sample",
    )(page_table, iq_rows, iw_rows, *([cache_ik_t] * group))


def _attn_sample_body(pt_ref, sc_ref, scn_ref, q_ref, kn_ref, vn_ref, bl_ref, bn_ref, *refs,
                      n_tok, n_pages, k_sel, group):
    kp_refs, vp_refs = refs[:group], refs[group:2 * group]
    o_ref, thr_ref, m_ref, l_ref, acc_ref = refs[2 * group:]
    p = pl.program_id(1)
    n_steps = n_pages // group
    grp = KV_REP * n_tok
    width = group * PAGE_SIZE
    scale = HEAD_DIM ** -0.5
    r_i = lax.broadcasted_iota(I32, (n_tok, LANES), 0)
    c_i = lax.broadcasted_iota(I32, (n_tok, LANES), 1)
    keys_new = jnp.where(c_i <= r_i, _sortable_key(scn_ref[0]), jnp.int32(INT_MIN))

    @pl.when(p == 0)
    def _():
        keys = _sortable_key(sc_ref[0])

        def count_ge(cand):
            cand_b = jnp.broadcast_to(cand, (n_tok, LANES))
            c = jnp.where(keys >= cand_b[None], 1.0, 0.0).sum(axis=0) + jnp.where(keys_new >= cand_b, 1.0, 0.0)
            return c.sum(axis=-1, keepdims=True)

        thr = _kth_largest_key(count_ge, n_tok, float(k_sel))
        thr_ref[...] = jnp.broadcast_to(jnp.maximum(thr, jnp.int32(INT_MIN + 1)), (n_tok, LANES))
        m_ref[...] = jnp.full(m_ref.shape, MASK_NEG, F32)
        l_ref[...] = jnp.zeros(l_ref.shape, F32)
        acc_ref[...] = jnp.zeros(acc_ref.shape, F32)

    thr_b = thr_ref[...]

    def update(g, s, vblk):
        sl = slice(g * grp, (g + 1) * grp)
        m_old = m_ref[sl, :]
        m_new = jnp.maximum(m_old, s.max(axis=-1, keepdims=True))
        pr = jnp.exp(s - _lane_tile(m_new, s.shape[1]))
        alpha = jnp.exp(m_old - m_new)
        l_ref[sl, :] = alpha * l_ref[sl, :] + pr.sum(axis=-1, keepdims=True)
        acc_ref[sl, :] = alpha * acc_ref[sl, :] + jnp.dot(pr.astype(BF16), vblk, preferred_element_type=F32)
        m_ref[sl, :] = m_new

    def group_rows(mask_tok):
        return jnp.concatenate([mask_tok] * KV_REP, axis=0)

    base = p * group
    mb = jnp.concatenate(
        [jnp.where(_sortable_key(sc_ref[0, base + t]) >= thr_b, 0.0, MASK_NEG) for t in range(group)], axis=1)
    mb_g = group_rows(mb)
    is_last = p == n_steps - 1
    zeros_head = jnp.zeros((grp, width - PAGE_SIZE), F32)
    for g in range(N_KV_HEADS):
        qg = q_ref[0, g * grp:(g + 1) * grp, :]
        kg = jnp.concatenate([r[pl.ds(g, PAGE_SIZE, stride=N_KV_HEADS), :].astype(BF16) for r in kp_refs], axis=0)
        vg = jnp.concatenate([r[pl.ds(g, PAGE_SIZE, stride=N_KV_HEADS), :].astype(BF16) for r in vp_refs], axis=0)
        s = lax.dot_general(qg, kg, (((1,), (1,)), ((), ())), preferred_element_type=F32) * scale
        near = jnp.where(is_last, bl_ref[g * grp:(g + 1) * grp, :], 0.0)
        s = s + mb_g + (near if group == 1 else jnp.concatenate([zeros_head, near], axis=1))
        update(g, s, vg)

    @pl.when(is_last)
    def _():
        mbn_g = group_rows(jnp.where(keys_new >= thr_b, 0.0, MASK_NEG))
        for g in range(N_KV_HEADS):
            qg = q_ref[0, g * grp:(g + 1) * grp, :]
            kblk = kn_ref[0, :, g * HEAD_DIM:(g + 1) * HEAD_DIM].astype(BF16)
            vblk = vn_ref[0, :, g * HEAD_DIM:(g + 1) * HEAD_DIM].astype(BF16)
            s = lax.dot_general(qg, kblk, (((1,), (1,)), ((), ())), preferred_element_type=F32) * scale
            s = s + mbn_g + bn_ref[g * grp:(g + 1) * grp, :]
            update(g, s, vblk)
        o_ref[0] = acc_ref[...] / l_ref[...]


def _attn_sample(page_table, scores, scores_new, q_rows, k_new, v_new, bias_last, bias_new,
                 cache_k, cache_v, layer, n_tok, group):
    db, n_pages = page_table.shape
    past = n_pages * PAGE_SIZE
    rows = N_HEADS * n_tok
    k_sel = min(TOPK_MAX, (past + n_tok) // 4)
    body = functools.partial(_attn_sample_body, n_tok=n_tok, n_pages=n_pages, k_sel=k_sel, group=group)
    page = (PAGE_SIZE * N_KV_HEADS, HEAD_DIM)
    grid_spec = pltpu.PrefetchScalarGridSpec(
        num_scalar_prefetch=1,
        grid=(db, n_pages // group),
        in_specs=[pl.BlockSpec((1, n_pages, n_tok, PAGE_SIZE), lambda b, p, pt: (b, 0, 0, 0)),
                  pl.BlockSpec((1, n_tok, LANES), lambda b, p, pt: (b, 0, 0)),
                  pl.BlockSpec((1, rows, HEAD_DIM), lambda b, p, pt: (b, 0, 0)),
                  pl.BlockSpec((1, LANES, KV_W), lambda b, p, pt: (b, 0, 0)),
                  pl.BlockSpec((1, LANES, KV_W), lambda b, p, pt: (b, 0, 0)),
                  pl.BlockSpec((rows, LANES), lambda b, p, pt: (0, 0)),
                  pl.BlockSpec((rows, LANES), lambda b, p, pt: (0, 0))]
        + [_page_spec(page, layer, group, t) for t in range(group)]
        + [_page_spec(page, layer, group, t) for t in range(group)],
        out_specs=pl.BlockSpec((1, rows, HEAD_DIM), lambda b, p, pt: (b, 0, 0)),
        scratch_shapes=[pltpu.VMEM((n_tok, LANES), I32),
                        pltpu.VMEM((rows, LANES), F32),
                        pltpu.VMEM((rows, LANES), F32),
                        pltpu.VMEM((rows, HEAD_DIM), F32)],
    )
    return pl.pallas_call(
        body,
        grid_spec=grid_spec,
        out_shape=jax.ShapeDtypeStruct((db, rows, HEAD_DIM), F32),
        compiler_params=_cp(("arbitrary", "arbitrary")),
        name="attn_sample",
    )(page_table, scores, scores_new, q_rows, k_new, v_new, bias_last, bias_new,
      *([cache_k] * group), *([cache_v] * group))


def _idx_new_body(iq_ref, iw_ref, ik_ref, o_ref, *, n_tok):
    for b in range(iq_ref.shape[0]):
        s = lax.dot_general(iq_ref[b], ik_ref[b].astype(BF16), (((1,), (1,)), ((), ())),
                            preferred_element_type=F32)
        r = jnp.maximum(s, 0.0) * iw_ref[b]
        o_ref[b] = r.reshape(n_tok, IDX_HEADS, LANES).sum(axis=1)


def _idx_new(iq_rows, iw_rows, ik_new, n_tok):
    db = iq_rows.shape[0]
    return pl.pallas_call(
        functools.partial(_idx_new_body, n_tok=n_tok),
        out_shape=jax.ShapeDtypeStruct((db, n_tok, LANES), F32),
        name="idx_new",
    )(iq_rows, iw_rows, ik_new)


HALO = CONV_WIDTH - 1
HALO_PAD = 32


def _conv_prompt_body(glu_ref, cw_ref, cb_ref, g_ref, b_ref, o_ref, st_ref, hp_ref, c_ref, *, ts, ns):
    s_idx = pl.program_id(1)

    @pl.when(s_idx == 0)
    def _():
        hp_ref[0:HALO_PAD, :] = jnp.zeros((HALO_PAD, CONV_DIM), F32)

    a = glu_ref[:, :CONV_DIM].astype(F32)
    gt = glu_ref[:, CONV_DIM:].astype(F32)
    hp_ref[HALO_PAD:HALO_PAD + ts, :] = a * _sigmoid(gt)
    off = HALO_PAD - HALO
    for c in range(CONV_DIM // LANES):
        cs = slice(c * LANES, (c + 1) * LANES)
        acc = jnp.zeros((ts, LANES), F32)
        for w in range(CONV_WIDTH):
            acc = acc + hp_ref[off + w:off + w + ts, cs] * cw_ref[w:w + 1, cs]
        c_ref[:, cs] = acc + cb_ref[:, cs]
    y = _ln_rows(c_ref[...], g_ref[...], b_ref[...])
    o_ref[...] = (y * _sigmoid(y)).astype(o_ref.dtype)
    tail = hp_ref[ts + off:ts + HALO_PAD, :]

    @pl.when(s_idx == ns - 1)
    def _():
        st_ref[0] = tail

    hp_ref[off:HALO_PAD, :] = tail


def _conv_prompt(ha, conv_w, conv_b, ln_g, ln_b, batch, seq, ts):
    ns = seq // ts
    body = functools.partial(_conv_prompt_body, ts=ts, ns=ns)
    vec = pl.BlockSpec((1, CONV_DIM), lambda b, s: (0, 0))
    return pl.pallas_call(
        body,
        grid=(batch, ns),
        in_specs=[pl.BlockSpec((ts, 2 * CONV_DIM), lambda b, s: (b * ns + s, HA_GLU // (2 * CONV_DIM))),
                  pl.BlockSpec((CONV_WIDTH, CONV_DIM), lambda b, s: (0, 0)), vec, vec, vec],
        out_specs=[pl.BlockSpec((ts, CONV_DIM), lambda b, s: (b * ns + s, 0)),
                   pl.BlockSpec((1, HALO, CONV_DIM), lambda b, s: (b, 0, 0))],
        out_shape=[jax.ShapeDtypeStruct((batch * seq, CONV_DIM), BF16),
                   jax.ShapeDtypeStruct((batch, HALO, CONV_DIM), F32)],
        scratch_shapes=[pltpu.VMEM((HALO_PAD + ts, CONV_DIM), F32), pltpu.VMEM((ts, CONV_DIM), F32)],
        compiler_params=_cp(("arbitrary", "arbitrary")),
        name="conv_prompt",
    )(ha, conv_w, conv_b, ln_g, ln_b)


def _sgu_prompt_body(uv_ref, g_ref, b_ref, w_ref, bs_ref, o_ref, *, ts):
    u = uv_ref[:, :SGU_DIM].astype(F32)
    vn = _ln_rows(uv_ref[:, SGU_DIM:].astype(F32), g_ref[...], b_ref[...]).astype(BF16)
    r_i = lax.broadcasted_iota(I32, (CHUNK, CHUNK), 0)
    c_i = lax.broadcasted_iota(I32, (CHUNK, CHUNK), 1)
    gd = SGU_DIM // SGU_GROUPS
    for g in range(SGU_GROUPS):
        wm = jnp.where(c_i <= r_i, w_ref[g], 0.0).astype(BF16)
        bias = bs_ref[:, g:g + 1]
        for c in range(ts // CHUNK):
            rs = slice(c * CHUNK, (c + 1) * CHUNK)
            gs = slice(g * gd, (g + 1) * gd)
            mixed = jnp.dot(wm, vn[rs, gs], preferred_element_type=F32) + bias
            o_ref[rs, gs] = (u[rs, gs] * mixed).astype(o_ref.dtype)


def _sgu_prompt(ha, ln_g, ln_b, w_s, b_s_t, batch, seq, ts):
    ns = seq // ts
    vec = pl.BlockSpec((1, SGU_DIM), lambda b, s: (0, 0))
    return pl.pallas_call(
        functools.partial(_sgu_prompt_body, ts=ts),
        grid=(batch, ns),
        in_specs=[pl.BlockSpec((ts, 2 * SGU_DIM), lambda b, s: (b * ns + s, HA_UV // (2 * SGU_DIM))),
                  vec, vec,
                  pl.BlockSpec((SGU_GROUPS, CHUNK, CHUNK), lambda b, s: (0, 0, 0)),
                  pl.BlockSpec((CHUNK, SGU_GROUPS), lambda b, s: (0, 0))],
        out_specs=pl.BlockSpec((ts, SGU_DIM), lambda b, s: (b * ns + s, 0)),
        out_shape=jax.ShapeDtypeStruct((batch * seq, SGU_DIM), BF16),
        compiler_params=_cp(("arbitrary", "arbitrary")),
        name="sgu_prompt",
    )(ha, ln_g, ln_b, w_s, b_s_t)


def _mix_sample_body(glu_ref, uv_ref, st_ref, cw_ref, cb_ref, cg_ref, cbt_ref, sg_ref, sb_ref, wv_ref, bv_ref,
                     co_ref, nst_ref, so_ref, vn_ref, hp_ref, *, db, n_tok):
    cw = cw_ref[...]
    for b in range(db):
        rs = slice(b * n_tok, (b + 1) * n_tok)
        a = glu_ref[rs, :CONV_DIM].astype(F32)
        gt = glu_ref[rs, CONV_DIM:].astype(F32)
        hp_ref[0:HALO, :] = st_ref[b]
        hp_ref[HALO:HALO + n_tok, :] = a * _sigmoid(gt)
        rows = [jnp.sum(hp_ref[t:t + CONV_WIDTH, :] * cw, axis=0, keepdims=True) for t in range(n_tok)]
        c = jnp.concatenate(rows, axis=0) + cb_ref[...]
        y = _ln_rows(c, cg_ref[...], cbt_ref[...])
        co_ref[rs, :] = y * _sigmoid(y)
        nst_ref[b] = hp_ref[n_tok:n_tok + HALO, :]
        u = uv_ref[rs, :SGU_DIM].astype(F32)
        vn = _ln_rows(uv_ref[rs, SGU_DIM:].astype(F32), sg_ref[...], sb_ref[...])
        vn_ref[rs, :] = vn
        mixed = []
        for t in range(n_tok):
            acc = bv_ref[t:t + 1, :]
            for s in range(t + 1):
                acc = acc + wv_ref[t, s:s + 1, :] * vn[s:s + 1, :]
            mixed.append(acc)
        so_ref[rs, :] = u * jnp.concatenate(mixed, axis=0)


def _mix_sample(glu, uv, state, conv_w, conv_b, cln_g, cln_b, sln_g, sln_b, wv, bv, db, n_tok):
    rows = db * n_tok
    return pl.pallas_call(
        functools.partial(_mix_sample_body, db=db, n_tok=n_tok),
        out_shape=[jax.ShapeDtypeStruct((rows, CONV_DIM), F32),
                   jax.ShapeDtypeStruct((db, HALO, CONV_DIM), F32),
                   jax.ShapeDtypeStruct((rows, SGU_DIM), F32),
                   jax.ShapeDtypeStruct((rows, SGU_DIM), F32)],
        scratch_shapes=[pltpu.VMEM((_round_up(HALO + n_tok, 8), CONV_DIM), F32)],
        name="mix_sample",
    )(glu, uv, state, conv_w, conv_b, cln_g, cln_b, sln_g, sln_b, wv, bv)


def _router_body(x_ref, w_ref, b_ref, ti_ref, tg_ref, *, n_exp):
    logits = jnp.dot(x_ref[...], w_ref[...], preferred_element_type=F32,
                     precision=lax.Precision.HIGHEST) + b_ref[...]
    lane = lax.broadcasted_iota(I32, logits.shape, 1)
    lane_f = lane.astype(F32)
    neg = -jnp.inf
    lg = jnp.where(lane < n_exp, logits, neg)
    v1 = lg.max(axis=-1, keepdims=True)
    i1 = jnp.where(lg == v1, lane_f, float(LANES)).min(axis=-1, keepdims=True)
    lg2 = jnp.where(lane_f == i1, neg, lg)
    v2 = lg2.max(axis=-1, keepdims=True)
    i2 = jnp.where(lg2 == v2, lane_f, float(LANES)).min(axis=-1, keepdims=True)
    e = jnp.exp(v2 - v1)
    den = 1.0 + e
    ti_ref[...] = jnp.where(lane == 0, i1, jnp.where(lane == 1, i2, 0.0)).astype(I32)
    tg_ref[...] = jnp.where(lane == 0, 1.0 / den, jnp.where(lane == 1, e / den, 0.0))


def _router(x, w_pad, b_pad, n_exp, tm):
    m, d = x.shape
    return pl.pallas_call(
        functools.partial(_router_body, n_exp=n_exp),
        grid=(m // tm,),
        in_specs=[pl.BlockSpec((tm, d), lambda i: (i, 0)),
                  pl.BlockSpec((d, LANES), lambda i: (0, 0)),
                  pl.BlockSpec((1, LANES), lambda i: (0, 0))],
        out_specs=[pl.BlockSpec((tm, LANES), lambda i: (i, 0)),
                   pl.BlockSpec((tm, LANES), lambda i: (i, 0))],
        out_shape=[jax.ShapeDtypeStruct((m, LANES), I32), jax.ShapeDtypeStruct((m, LANES), F32)],
        compiler_params=_cp(("arbitrary",)),
        name="moe_router",
    )(x, w_pad, b_pad)


def _row_copy(src_hbm, dst, src_row, dst_row, sem):
    return pltpu.make_async_copy(src_hbm.at[pl.ds(src_row, 1)], dst.at[pl.ds(dst_row, 1)], sem)


def _dispatch_body(src_ref, x_hbm, o_ref, buf, sem, *, rows):
    base = pl.program_id(0) * rows

    def start(r, carry):
        _row_copy(x_hbm, buf, src_ref[base + r], r, sem).start()
        return carry

    lax.fori_loop(0, rows, start, 0)

    def wait(r, carry):
        _row_copy(x_hbm, buf, 0, r, sem).wait()
        return carry

    lax.fori_loop(0, rows, wait, 0)
    o_ref[...] = buf[...].astype(o_ref.dtype)


def _dispatch(src, x, n_slots, rows):
    d = x.shape[1]
    grid_spec = pltpu.PrefetchScalarGridSpec(
        num_scalar_prefetch=1,
        grid=(n_slots // rows,),
        in_specs=[pl.BlockSpec(memory_space=pl.ANY)],
        out_specs=pl.BlockSpec((rows, d), lambda i, s: (i, 0)),
        scratch_shapes=[pltpu.VMEM((rows, d), x.dtype), pltpu.SemaphoreType.DMA(())],
    )
    return pl.pallas_call(
        functools.partial(_dispatch_body, rows=rows),
        grid_spec=grid_spec,
        out_shape=jax.ShapeDtypeStruct((n_slots, d), BF16),
        compiler_params=_cp(("arbitrary",)),
        name="moe_dispatch",
    )(src, x)


def _tile_copy(src, dst, sem):
    return pltpu.make_async_copy(src, dst, sem)


def _moe_ffn_body(exp_ref, row0_ref, nsub_ref, jmap_ref, nzero_ref, xs_hbm, w1_ref, w3_ref, w2_ref, ys_hbm,
                  xbuf, acc, w1b, w3b, w2b, sem, *, nj, sub):
    w = pl.program_id(0)
    j = pl.program_id(1)
    nsub = nsub_ref[w]
    row0 = row0_ref[w]
    nzero = nzero_ref[w]

    def zero_copy(t):
        r = pl.multiple_of(row0 + t * sub, sub)
        return _tile_copy(acc.at[pl.ds(0, sub)], ys_hbm.at[pl.ds(r, sub)], sem)

    @pl.when(jnp.logical_and(j == 0, nzero > 0))
    def _():
        acc[0:sub, :] = jnp.zeros((sub, acc.shape[1]), F32)

        def zs(t, carry):
            zero_copy(t).start()
            return carry

        def zw(t, carry):
            zero_copy(t).wait()
            return carry

        lax.fori_loop(0, nzero, zs, 0)
        lax.fori_loop(0, nzero, zw, 0)

    def in_copy(t):
        r = pl.multiple_of(t * sub, sub)
        return _tile_copy(xs_hbm.at[pl.ds(pl.multiple_of(row0 + r, sub), sub)], xbuf.at[pl.ds(r, sub)], sem)

    def out_copy(t):
        r = pl.multiple_of(t * sub, sub)
        return _tile_copy(acc.at[pl.ds(r, sub)], ys_hbm.at[pl.ds(pl.multiple_of(row0 + r, sub), sub)], sem)

    def for_tiles(fn):
        def it(t, carry):
            fn(t)
            return carry
        lax.fori_loop(0, nsub, it, 0)

    def zero_tile(t):
        acc[pl.ds(pl.multiple_of(t * sub, sub), sub), :] = jnp.zeros((sub, acc.shape[1]), F32)

    @pl.when(j == 0)
    def _():
        for_tiles(lambda t: in_copy(t).start())
        for_tiles(zero_tile)
        for_tiles(lambda t: in_copy(t).wait())

    @pl.when(nsub > 0)
    def _():
        w1b[...] = w1_ref[...].astype(BF16)
        w3b[...] = w3_ref[...].astype(BF16)
        w2b[...] = w2_ref[...].astype(BF16)

    def tile(t):
        r = pl.multiple_of(t * sub, sub)
        x = xbuf[pl.ds(r, sub), :]
        a = jnp.dot(x, w1b[...], preferred_element_type=F32)
        c = jnp.dot(x, w3b[...], preferred_element_type=F32)
        h = (a * _sigmoid(a) * c).astype(BF16)
        acc[pl.ds(r, sub), :] += jnp.dot(h, w2b[...], preferred_element_type=F32)

    def pair(t, carry):
        tile(2 * t)
        tile(2 * t + 1)
        return carry

    lax.fori_loop(0, nsub // 2, pair, 0)

    @pl.when(nsub % 2 == 1)
    def _():
        tile(nsub - 1)

    @pl.when(j == nj - 1)
    def _():
        for_tiles(lambda t: out_copy(t).start())
        for_tiles(lambda t: out_copy(t).wait())


def _moe_ffn(item_exp, item_row0, item_nsub, item_jlast, item_nzero, xs, w1, w3, w2, layer, n_items, super_rows,
             sub, tf):
    n_slots, d = xs.shape
    f = w1.shape[3]
    nj = f // tf

    def jj(w, j, jl):
        return jnp.where(jl[w] > 0, j, nj - 1)

    grid_spec = pltpu.PrefetchScalarGridSpec(
        num_scalar_prefetch=5,
        grid=(n_items, nj),
        in_specs=[pl.BlockSpec(memory_space=pl.ANY),
                  pl.BlockSpec((None, None, d, tf), lambda w, j, ex, r0, ns, jl, nz: (layer, ex[w], 0, jj(w, j, jl))),
                  pl.BlockSpec((None, None, d, tf), lambda w, j, ex, r0, ns, jl, nz: (layer, ex[w], 0, jj(w, j, jl))),
                  pl.BlockSpec((None, None, tf, d), lambda w, j, ex, r0, ns, jl, nz: (layer, ex[w], jj(w, j, jl), 0))],
        out_specs=pl.BlockSpec(memory_space=pl.ANY),
        scratch_shapes=[pltpu.VMEM((super_rows, d), BF16),
                        pltpu.VMEM((super_rows, d), F32),
                        pltpu.VMEM((d, tf), BF16),
                        pltpu.VMEM((d, tf), BF16),
                        pltpu.VMEM((tf, d), BF16),
                        pltpu.SemaphoreType.DMA(())],
    )
    return pl.pallas_call(
        functools.partial(_moe_ffn_body, nj=nj, sub=sub),
        grid_spec=grid_spec,
        out_shape=jax.ShapeDtypeStruct((n_slots, d), F32),
        compiler_params=_cp(("arbitrary", "arbitrary")),
        name="moe_ffn",
    )(item_exp, item_row0, item_nsub, item_jlast, item_nzero, xs, w1, w3, w2)


def _combine_body(pos_ref, ys_hbm, tg_ref, x_ref, g_ref, b_ref, of_ref, ob_ref, buf, sem, *, tb, alpha):
    base = pl.program_id(0) * tb

    def start(r, carry):
        for k in range(TOP_K_EXPERTS):
            _row_copy(ys_hbm, buf.at[k], pos_ref[TOP_K_EXPERTS * (base + r) + k], r, sem).start()
        return carry

    lax.fori_loop(0, tb, start, 0)

    def wait(r, carry):
        for k in range(TOP_K_EXPERTS):
            _row_copy(ys_hbm, buf.at[k], 0, r, sem).wait()
        return carry

    lax.fori_loop(0, tb, wait, 0)
    f = tg_ref[:, 0:1] * buf[0] + tg_ref[:, 1:2] * buf[1]
    y = _ln_rows(alpha * x_ref[...] + f, g_ref[...], b_ref[...])
    of_ref[...] = y
    ob_ref[...] = y.astype(BF16)


def _combine_ln(pos, ys, tg, x, g, b, alpha, tb):
    m, d = x.shape
    grid_spec = pltpu.PrefetchScalarGridSpec(
        num_scalar_prefetch=1,
        grid=(m // tb,),
        in_specs=[pl.BlockSpec(memory_space=pl.ANY),
                  pl.BlockSpec((tb, LANES), lambda i, p: (i, 0)),
                  pl.BlockSpec((tb, d), lambda i, p: (i, 0)),
                  pl.BlockSpec((1, d), lambda i, p: (0, 0)),
                  pl.BlockSpec((1, d), lambda i, p: (0, 0))],
        out_specs=[pl.BlockSpec((tb, d), lambda i, p: (i, 0)),
                   pl.BlockSpec((tb, d), lambda i, p: (i, 0))],
        scratch_shapes=[pltpu.VMEM((TOP_K_EXPERTS, tb, d), F32), pltpu.SemaphoreType.DMA(())],
    )
    return pl.pallas_call(
        functools.partial(_combine_body, tb=tb, alpha=alpha),
        grid_spec=grid_spec,
        out_shape=[jax.ShapeDtypeStruct((m, d), F32), jax.ShapeDtypeStruct((m, d), BF16)],
        compiler_params=_cp(("arbitrary",)),
        name="moe_combine_ln",
    )(pos, ys, tg, x, g, b)


def _moe_plan(top_i, n_exp, n_slots, n_items, super_rows, sub):
    a = top_i.shape[0] * TOP_K_EXPERTS
    e_flat = top_i.reshape(-1)
    onehot = (e_flat[:, None] == jnp.arange(n_exp, dtype=I32)[None, :]).astype(I32)
    csum = jnp.cumsum(onehot, axis=0)
    rank = jnp.take_along_axis(csum, e_flat[:, None], axis=1)[:, 0] - 1
    counts = csum[-1]
    padded = (counts + sub - 1) // sub * sub
    gstart = jnp.cumsum(padded) - padded
    pos = (gstart[e_flat] + rank).astype(I32)
    src = jnp.zeros((n_slots,), I32).at[pos].set(jnp.arange(a, dtype=I32) // TOP_K_EXPERTS)
    n_super = (padded + super_rows - 1) // super_rows
    iend = jnp.cumsum(n_super)
    istart = iend - n_super
    total = iend[-1]
    wi = jnp.arange(n_items, dtype=I32)
    valid = wi < total
    e_w = jnp.minimum(jnp.searchsorted(iend, wi, side="right"), n_exp - 1).astype(I32)
    last_e = e_w[jnp.maximum(total - 1, 0)]
    e_w = jnp.where(valid, e_w, last_e)
    k_w = wi - istart[e_w]
    nsub = jnp.where(valid, jnp.minimum(super_rows, padded[e_w] - k_w * super_rows) // sub, 0).astype(I32)
    z0 = jnp.sum(padded) + (wi - total) * super_rows
    nzero = jnp.where(valid, 0, jnp.clip((n_slots - z0) // sub, 0, super_rows // sub)).astype(I32)
    row0 = jnp.where(valid, gstart[e_w] + k_w * super_rows, jnp.where(nzero > 0, z0, 0)).astype(I32)
    return pos, src, e_w, row0, nsub, valid.astype(I32), nzero


def _rel_bucket(dist):
    n = jnp.maximum(dist, 0)
    max_exact = REL_BUCKETS // 2
    nf = jnp.maximum(n, 1).astype(F32)
    large = max_exact + (jnp.log(nf / max_exact) / math.log(REL_MAX_DIST / max_exact)
                         * (REL_BUCKETS - max_exact)).astype(I32)
    large = jnp.minimum(large, REL_BUCKETS - 1)
    return jnp.where(n < max_exact, n, large)


def _shifted_bias(rel_bias, dist):
    t = rel_bias[_rel_bucket(dist)] - rel_bias[REL_BUCKETS - 1]
    t = jnp.where((dist >= 0)[..., None], t, 0.0)
    return jnp.moveaxis(t, -1, 0)


def _pick_tile(m_rows, target):
    return max(t for t in range(16, target + 1, 16) if m_rows % t == 0)


def _pad_rows(x, rows):
    return jnp.pad(x, ((0, rows - x.shape[0]), (0, 0)))


def kernel(x_prompt, x_sample, cache_k, cache_v, cache_ik, state_conv, page_table, w_in, conv_w, conv_b,
           conv_ln_g, conv_ln_b, sgu_ln_g, sgu_ln_b, sgu_w, sgu_b, w_pa, w_pb, w_pc, w_out, ln1_g, ln1_b,
           ln2_g, ln2_b, rel_bias, ffn_w1, ffn_w3, ffn_w2, moe_router, moe_router_b, moe_w1, moe_w3, moe_w2):
    batch, seq, d_model = x_prompt.shape
    db, n_tok, _ = x_sample.shape
    depth = w_in.shape[0]
    n_pool = cache_k.shape[1]
    n_exp = moe_router.shape[2]
    mp, ms = batch * seq, db * n_tok
    assert ms <= SAMPLE_TILE and mp % SAMPLE_TILE == 0
    m_all = mp + SAMPLE_TILE
    tm = m_all // N_ROW_TILES
    assert tm * N_ROW_TILES == m_all and tm % 16 == 0
    alpha = (2 * depth) ** 0.25
    tq = min(256, seq)

    xf = jnp.concatenate([x_prompt.reshape(mp, d_model), x_sample.reshape(ms, d_model),
                          jnp.zeros((m_all - mp - ms, d_model), F32)], axis=0)
    xb = xf.astype(BF16)
    w_in_t = jnp.swapaxes(w_in, 1, 2)
    ha_rows = ([W_Q_ROW0 + HA_TILE * t for t in range(QKV_W // HA_TILE)]
               + [W_GLU_ROW0 + HA_TILE * t for t in range(2 * CONV_DIM // HA_TILE)]
               + [W_UV_ROW0 + HA_TILE * t for t in range(2 * SGU_DIM // HA_TILE)]
               + [W_GATE_ROW0 + HA_TILE * t for t in range(N_BRANCH * d_model // HA_TILE)]
               + [W_IQ_ROW0 + HA_TILE * t for t in range(IQ_W // HA_TILE)])
    w_pa_b, w_pb_b, w_pc_b, w_out_b = (w.astype(BF16) for w in (w_pa, w_pb, w_pc, w_out))
    ffn_w1_b, ffn_w3_b, ffn_w2_b = (w.astype(BF16) for w in (ffn_w1, ffn_w3, ffn_w2))

    ar = jnp.arange(tq, dtype=I32)
    d0 = ar[:, None] - ar[None, :]
    bias_tiles = jnp.stack([_shifted_bias(rel_bias, d0), _shifted_bias(rel_bias, d0 + tq)])
    tok = jnp.arange(n_tok, dtype=I32)
    lane = jnp.arange(LANES, dtype=I32)
    d_last = PAGE_SIZE + tok[:, None] - lane[None, :]
    bias_last = _shifted_bias(rel_bias, d_last).reshape(N_HEADS * n_tok, LANES)
    bias_new = _shifted_bias(rel_bias, tok[:, None] - lane[None, :]).reshape(N_HEADS * n_tok, LANES)

    ck = cache_k.reshape(depth, n_pool, PAGE_SIZE * N_KV_HEADS, HEAD_DIM)
    cv = cache_v.reshape(depth, n_pool, PAGE_SIZE * N_KV_HEADS, HEAD_DIM)
    cik_t = jnp.swapaxes(cache_ik, 2, 3)
    n_pages = page_table.shape[1]
    idx_group = math.gcd(n_pages, IDX_PAGE_GROUP)
    attn_group = math.gcd(n_pages, ATTN_PAGE_GROUP)
    n_slots = _round_up(TOP_K_EXPERTS * m_all + n_exp * (MOE_SUB - 1), MOE_SUB)
    n_items = -(-n_slots // MOE_SUPER) + n_exp

    outs = {k: [] for k in ("k", "v", "ik", "conv_p", "conv_s", "sgu_s")}
    for l in range(depth):
        ha = _in_proj(xb, w_in_t, l, ha_rows, tm, HA_TILE)
        kv, ikw = _kv_proj(xb, w_in_t, l, tm)

        a_p = _attn_prompt(ha, kv, ikw, bias_tiles, batch, seq, tq)
        c_p, conv_state_p = _conv_prompt(ha, conv_w[l], conv_b[l][None], conv_ln_g[l][None], conv_ln_b[l][None],
                                         batch, seq, min(256, seq))
        s_p = _sgu_prompt(ha, sgu_ln_g[l][None], sgu_ln_b[l][None], sgu_w[l], sgu_b[l].T, batch, seq,
                          min(512, seq))

        has, kvs, ikws = ha[mp:mp + ms], kv[mp:mp + ms], ikw[mp:mp + ms]
        iq_rows = has[:, HA_IQ:HA_IQ + IQ_W].reshape(db, n_tok * IDX_HEADS, IDX_DIM)
        iw_rows = (ikws[:, IKW_IW:IKW_IW + IDX_HEADS] * (IDX_DIM ** -0.5 * IDX_HEADS ** -0.5)
                   ).reshape(db, n_tok * IDX_HEADS, 1)
        iw_rows = jnp.broadcast_to(iw_rows, (db, n_tok * IDX_HEADS, LANES))
        scores = _idx_sample(page_table, iq_rows, iw_rows, cik_t, l, n_tok, idx_group)
        pad_rows = LANES - n_tok
        ik_new = jnp.pad(ikws[:, :IDX_DIM].reshape(db, n_tok, IDX_DIM), ((0, 0), (0, pad_rows), (0, 0)))
        scores_new = _idx_new(iq_rows, iw_rows, ik_new, n_tok)
        q_rows = has[:, HA_Q:HA_Q + QKV_W].reshape(db, n_tok, N_HEADS, HEAD_DIM).transpose(0, 2, 1, 3)
        q_rows = q_rows.reshape(db, N_HEADS * n_tok, HEAD_DIM)
        k_new = jnp.pad(kvs[:, KV_K:KV_K + KV_W].reshape(db, n_tok, KV_W), ((0, 0), (0, pad_rows), (0, 0)))
        v_new = jnp.pad(kvs[:, KV_V:KV_V + KV_W].reshape(db, n_tok, KV_W), ((0, 0), (0, pad_rows), (0, 0)))
        a_s = _attn_sample(page_table, scores, scores_new, q_rows, k_new, v_new, bias_last, bias_new,
                           ck, cv, l, n_tok, attn_group)
        a_s = a_s.reshape(db, N_HEADS, n_tok, HEAD_DIM).transpose(0, 2, 1, 3).reshape(ms, QKV_W)
        gd = SGU_DIM // SGU_GROUPS
        wv = jnp.repeat(sgu_w[l][:, :n_tok, :n_tok].transpose(1, 2, 0), gd, axis=-1)
        bv = jnp.repeat(sgu_b[l][:, :n_tok].T, gd, axis=-1)
        c_s, conv_state_s, s_s, vn_s = _mix_sample(
            has[:, HA_GLU:HA_GLU + 2 * CONV_DIM], has[:, HA_UV:HA_UV + 2 * SGU_DIM], state_conv[l],
            conv_w[l], conv_b[l][None], conv_ln_g[l][None], conv_ln_b[l][None],
            sgu_ln_g[l][None], sgu_ln_b[l][None], wv, bv, db, n_tok)

        sample_rows = tuple(_pad_rows(t.astype(BF16), SAMPLE_TILE) for t in (a_s, c_s, s_s))
        merged = _merge((a_p, c_p, s_p), sample_rows, ha, w_pa_b, w_pb_b, w_pc_b, l, SAMPLE_TILE)
        x1f, x1b = _mm_ln(merged, w_out_b, l, xf, ln1_g[l][None], ln1_b[l][None], alpha, _pick_tile(m_all, 528))

        j = l // 2
        if l % 2 == 0:
            h = _swiglu_up(x1b, ffn_w1_b, ffn_w3_b, j, tm, 512)
            xf, xb = _mm_ln(h, ffn_w2_b, j, x1f, ln2_g[l][None], ln2_b[l][None], alpha, _pick_tile(m_all, 384))
        else:
            w_r = jnp.pad(moe_router[j], ((0, 0), (0, LANES - n_exp)))
            b_r = jnp.pad(moe_router_b[j], (0, LANES - n_exp))[None]
            ti, tg = _router(x1f, w_r, b_r, n_exp, _pick_tile(m_all, 528))
            pos, src, it_e, it_r0, it_ns, it_valid, it_nz = _moe_plan(ti[:, :TOP_K_EXPERTS], n_exp, n_slots,
                                                                      n_items, MOE_SUPER, MOE_SUB)
            xs = _dispatch(src, x1f, n_slots, MOE_SUB)
            ys = _moe_ffn(it_e, it_r0, it_ns, it_valid, it_nz, xs, moe_w1, moe_w3, moe_w2, j, n_items,
                          MOE_SUPER, MOE_SUB, MOE_TF)
            tb = COMBINE_TB if m_all % COMBINE_TB == 0 else 16
            xf, xb = _combine_ln(pos, ys, tg, x1f, ln2_g[l][None], ln2_b[l][None], alpha, tb)

        outs["k"].append(kv[:, KV_K:KV_K + KV_W])
        outs["v"].append(kv[:, KV_V:KV_V + KV_W])
        outs["ik"].append(ikw[:, :IDX_DIM])
        outs["conv_p"].append(conv_state_p)
        outs["conv_s"].append(conv_state_s)
        outs["sgu_s"].append(vn_s.reshape(db, n_tok, SGU_DIM))

    def split(name, width_shape):
        st = jnp.stack(outs[name])
        p = st[:, :mp].reshape((depth, batch, seq) + width_shape)
        s = st[:, mp:mp + ms].reshape((depth, db, n_tok) + width_shape)
        return p, s

    k_p, k_s = split("k", (N_KV_HEADS, HEAD_DIM))
    v_p, v_s = split("v", (N_KV_HEADS, HEAD_DIM))
    ik_p, ik_s = split("ik", (IDX_DIM,))
    y_prompt = xf[:mp].reshape(batch, seq, d_model)
    y_sample = xf[mp:mp + ms].reshape(db, n_tok, d_model)
    return (y_prompt, y_sample, k_p, v_p, ik_p, jnp.stack(outs["conv_p"]), k_s, v_s, ik_s,
            jnp.stack(outs["conv_s"]), jnp.stack(outs["sgu_s"]))
```

```python
import functools
import math

import jax
import jax.numpy as jnp
import numpy as np
from jax import lax
from jax.experimental import pallas as pl
from jax.experimental.pallas import tpu as pltpu

F32 = jnp.float32
BF16 = jnp.bfloat16
I32 = jnp.int32

N_HEADS = 16
HEAD_DIM = 128
N_KV_HEADS = 4
KV_REP = N_HEADS // N_KV_HEADS
IDX_HEADS = 16
IDX_DIM = 64
TOPK_MAX = 256
PAGE_SIZE = 128
CONV_DIM = 1024
CONV_WIDTH = 31
SGU_DIM = 1024
SGU_GROUPS = 8
CHUNK = 128
N_BRANCH = 3
TOP_K_EXPERTS = 2
REL_BUCKETS = 32
REL_MAX_DIST = 128
LN_EPS = 1e-5

V7X_VMEM_BYTES = 64 * 2**20
VMEM_LIMIT = V7X_VMEM_BYTES - 8 * 2**20
LANES = 128
F32_SUBLANES = 8
INT_MIN = -(2**31)
MASK_NEG = -1e30

QKV_W = N_HEADS * HEAD_DIM
KV_W = N_KV_HEADS * HEAD_DIM
IQ_W = IDX_HEADS * IDX_DIM
HA_Q, HA_GLU, HA_UV, HA_GATE, HA_IQ = 0, 2048, 4096, 6144, 12288
HA_TILE = 1024
W_Q_ROW0 = 0
W_KV_ROW0 = W_Q_ROW0 + QKV_W
W_IQ_ROW0 = W_KV_ROW0 + 2 * KV_W
W_IK_ROW0 = W_IQ_ROW0 + IQ_W
W_GLU_ROW0 = W_IK_ROW0 + IDX_DIM + IDX_HEADS
W_UV_ROW0 = W_GLU_ROW0 + 2 * CONV_DIM
W_GATE_ROW0 = W_UV_ROW0 + 2 * SGU_DIM
KV_K, KV_V = 0, KV_W
IKW_IW = IDX_DIM

N_ROW_TILES = 8
SAMPLE_TILE = 256
MOE_SUB = 256
MOE_SUPER = 2048
MOE_TF = 256
COMBINE_TB = 128
IDX_PAGE_GROUP = 32
ATTN_PAGE_GROUP = 16


def _cp(sem, vmem=VMEM_LIMIT):
    return pltpu.CompilerParams(dimension_semantics=sem, vmem_limit_bytes=vmem)


def _round_up(x, m):
    return (x + m - 1) // m * m


def _ln_rows(z, g, b):
    mu = jnp.mean(z, axis=-1, keepdims=True)
    d = z - mu
    var = jnp.mean(d * d, axis=-1, keepdims=True)
    return d * lax.rsqrt(var + LN_EPS) * g + b


def _sigmoid(x):
    return 1.0 / (1.0 + jnp.exp(-x))


def _in_proj_body(off_ref, x_ref, w_ref, o_ref, wt_ref):
    @pl.when(pl.program_id(1) == 0)
    def _():
        wt_ref[...] = w_ref[0].T.astype(BF16)

    o_ref[...] = jnp.dot(x_ref[...], wt_ref[...], preferred_element_type=F32).astype(o_ref.dtype)


def _in_proj(x, w_t, layer, row_offsets, tm, tn):
    m, k = x.shape
    n_tiles = len(row_offsets)
    grid_spec = pltpu.PrefetchScalarGridSpec(
        num_scalar_prefetch=1,
        grid=(n_tiles, m // tm),
        in_specs=[pl.BlockSpec((tm, k), lambda j, i, off: (i, 0)),
                  pl.BlockSpec((pl.Element(1), pl.Element(tn), pl.Element(k)),
                               lambda j, i, off: (layer, pl.multiple_of(off[j], F32_SUBLANES), 0))],
        out_specs=pl.BlockSpec((tm, tn), lambda j, i, off: (i, j)),
        scratch_shapes=[pltpu.VMEM((k, tn), BF16)],
    )
    return pl.pallas_call(
        _in_proj_body,
        grid_spec=grid_spec,
        out_shape=jax.ShapeDtypeStruct((m, n_tiles * tn), BF16),
        compiler_params=_cp(("arbitrary", "arbitrary")),
        name="in_proj",
    )(jnp.asarray(row_offsets, I32), x, w_t)


def _kv_proj_body(x_ref, wkv_ref, wik_ref, kv_ref, ik_ref, wkv_t, wik_t):
    @pl.when(pl.program_id(0) == 0)
    def _():
        wkv_t[...] = wkv_ref[...].T.astype(BF16)
        wik_t[...] = wik_ref[...].T.astype(BF16)

    x = x_ref[...]
    kv_ref[...] = jnp.dot(x, wkv_t[...], preferred_element_type=F32)
    ik_ref[...] = jnp.dot(x, wik_t[...], preferred_element_type=F32)


def _kv_proj(x, w_t, layer, tm):
    m, k = x.shape
    assert W_KV_ROW0 % (2 * KV_W) == 0 and W_IK_ROW0 % LANES == 0
    return pl.pallas_call(
        _kv_proj_body,
        grid=(m // tm,),
        in_specs=[pl.BlockSpec((tm, k), lambda i: (i, 0)),
                  pl.BlockSpec((None, 2 * KV_W, k), lambda i: (layer, W_KV_ROW0 // (2 * KV_W), 0)),
                  pl.BlockSpec((None, LANES, k), lambda i: (layer, W_IK_ROW0 // LANES, 0))],
        out_specs=[pl.BlockSpec((tm, 2 * KV_W), lambda i: (i, 0)),
                   pl.BlockSpec((tm, LANES), lambda i: (i, 0))],
        out_shape=[jax.ShapeDtypeStruct((m, 2 * KV_W), F32), jax.ShapeDtypeStruct((m, LANES), F32)],
        scratch_shapes=[pltpu.VMEM((k, 2 * KV_W), BF16), pltpu.VMEM((k, LANES), BF16)],
        compiler_params=_cp(("arbitrary",)),
        name="kv_proj",
    )(x, w_t, w_t)


def _mm_ln_body(x_ref, w_ref, r_ref, g_ref, b_ref, of_ref, ob_ref, *, alpha):
    z = jnp.dot(x_ref[...], w_ref[...], preferred_element_type=F32) + alpha * r_ref[...]
    y = _ln_rows(z, g_ref[...], b_ref[...])
    of_ref[...] = y
    ob_ref[...] = y.astype(BF16)


def _mm_ln(x, w, layer, resid, g, b, alpha, tm):
    m, k = x.shape
    n = w.shape[2]
    row = lambda i: (i, 0)
    return pl.pallas_call(
        functools.partial(_mm_ln_body, alpha=alpha),
        grid=(m // tm,),
        in_specs=[pl.BlockSpec((tm, k), row),
                  pl.BlockSpec((None, k, n), lambda i: (layer, 0, 0), pipeline_mode=pl.Buffered(1)),
                  pl.BlockSpec((tm, n), row),
                  pl.BlockSpec((1, n), lambda i: (0, 0)),
                  pl.BlockSpec((1, n), lambda i: (0, 0))],
        out_specs=[pl.BlockSpec((tm, n), row), pl.BlockSpec((tm, n), row)],
        out_shape=[jax.ShapeDtypeStruct((m, n), F32), jax.ShapeDtypeStruct((m, n), BF16)],
        compiler_params=_cp(("arbitrary",)),
        name="proj_ln",
    )(x, w, resid, g, b)


def _swiglu_up_body(x_ref, w1_ref, w3_ref, o_ref):
    x = x_ref[...]
    a = jnp.dot(x, w1_ref[...], preferred_element_type=F32)
    c = jnp.dot(x, w3_ref[...], preferred_element_type=F32)
    o_ref[...] = (a * _sigmoid(a) * c).astype(o_ref.dtype)


def _swiglu_up(x, w1, w3, layer, tm, tf):
    m, k = x.shape
    f = w1.shape[2]
    return pl.pallas_call(
        _swiglu_up_body,
        grid=(f // tf, m // tm),
        in_specs=[pl.BlockSpec((tm, k), lambda j, i: (i, 0)),
                  pl.BlockSpec((None, k, tf), lambda j, i: (layer, 0, j)),
                  pl.BlockSpec((None, k, tf), lambda j, i: (layer, 0, j))],
        out_specs=pl.BlockSpec((tm, tf), lambda j, i: (i, j)),
        out_shape=jax.ShapeDtypeStruct((m, f), BF16),
        compiler_params=_cp(("arbitrary", "arbitrary")),
        name="ffn_up",
    )(x, w1, w3)


def _merge_body(ap_ref, cp_ref, sp_ref, as_ref, cs_ref, ss_ref, g_ref, wa_ref, wb_ref, wc_ref, o_ref, *, n_prompt):
    i = pl.program_id(0)
    d = o_ref.shape[1]

    def merge(a_ref, c_ref, s_ref):
        pa = jnp.dot(a_ref[...], wa_ref[...], preferred_element_type=F32)
        pb = jnp.dot(c_ref[...], wb_ref[...], preferred_element_type=F32)
        pc = jnp.dot(s_ref[...], wc_ref[...], preferred_element_type=F32)
        o = (_sigmoid(g_ref[:, :d].astype(F32)) * pa + _sigmoid(g_ref[:, d:2 * d].astype(F32)) * pb
             + _sigmoid(g_ref[:, 2 * d:].astype(F32)) * pc)
        o_ref[...] = o.astype(o_ref.dtype)

    @pl.when(i < n_prompt)
    def _():
        merge(ap_ref, cp_ref, sp_ref)

    @pl.when(i >= n_prompt)
    def _():
        merge(as_ref, cs_ref, ss_ref)


def _merge(prompt, sample, ha, w_pa, w_pb, w_pc, layer, tm):
    m = ha.shape[0]
    d = w_pa.shape[2]
    n_prompt = prompt[0].shape[0] // tm
    assert prompt[0].shape[0] % tm == 0 and sample[0].shape[0] == tm and m == (n_prompt + 1) * tm
    assert HA_GATE % (N_BRANCH * d) == 0
    p_row = lambda i: (jnp.minimum(i, n_prompt - 1), 0)
    fixed = lambda i: (0, 0)
    return pl.pallas_call(
        functools.partial(_merge_body, n_prompt=n_prompt),
        grid=(n_prompt + 1,),
        in_specs=[pl.BlockSpec((tm, t.shape[1]), p_row) for t in prompt]
        + [pl.BlockSpec((tm, t.shape[1]), fixed) for t in sample]
        + [pl.BlockSpec((tm, N_BRANCH * d), lambda i: (i, HA_GATE // (N_BRANCH * d)))]
        + [pl.BlockSpec((None,) + w.shape[1:], lambda i: (layer, 0, 0), pipeline_mode=pl.Buffered(1))
           for w in (w_pa, w_pb, w_pc)],
        out_specs=pl.BlockSpec((tm, d), lambda i: (i, 0)),
        out_shape=jax.ShapeDtypeStruct((m, d), BF16),
        compiler_params=_cp(("arbitrary",)),
        name="branch_merge",
    )(*prompt, *sample, ha, w_pa, w_pb, w_pc)


def _sortable_key(score):
    bits = pltpu.bitcast(score, I32)
    return bits ^ ((bits >> 31) & jnp.int32(0x7FFFFFFF))


def _kth_largest_key(count_ge, rows, k):
    def step(it, t):
        cand = t + jnp.left_shift(jnp.int32(1), 31 - it)
        return jnp.where(count_ge(cand) >= k, cand, t)

    return lax.fori_loop(0, 32, step, jnp.full((rows, 1), INT_MIN, I32))


RADIX4_STEPS = 16
RADIX4_FIELD_BITS = 5
RADIX4_WIDE = 4096.0


def _kth_largest_key_radix4(count3, rows, k, n_valid):
    kf = float(k)

    def cond(state):
        it, _, _, pending = state
        return jnp.logical_and(it < RADIX4_STEPS, pending > 0.0)

    def body(state):
        it, t, cnt, _ = state
        one = jnp.left_shift(jnp.int32(1), 30 - 2 * it)
        c1 = t + one
        c2 = c1 + one
        c3 = c2 + one
        n1, n2, n3 = count3(c1, c2, c3)
        t = jnp.where(n3 >= kf, c3, jnp.where(n2 >= kf, c2, jnp.where(n1 >= kf, c1, t)))
        cnt = jnp.where(n3 >= kf, n3, jnp.where(n2 >= kf, n2, jnp.where(n1 >= kf, n1, cnt)))
        settled = jnp.logical_or(cnt == kf, n_valid <= kf)
        return it + 1, t, cnt, jnp.max(jnp.where(settled, 0.0, 1.0))

    init = (jnp.int32(0), jnp.full((rows, 1), INT_MIN, I32), jnp.full((rows, 1), -1.0, F32), jnp.float32(1.0))
    return lax.while_loop(cond, body, init)[1]


def _lane_tile(x, width):
    return x if width == LANES else jnp.concatenate([x] * (width // LANES), axis=1)


def _attn_prompt_body(q_ref, iq_ref, iw_ref, kv_ref, ik_ref, bias_ref, o_ref,
                      keys_ref, mb_ref, iwb_ref, m_ref, l_ref, acc_ref, *, tq, k_sel):
    i = pl.program_id(1)
    nkb = i + 1
    tk = tq
    row = lax.broadcasted_iota(I32, (tq, tk), 0)
    col = lax.broadcasted_iota(I32, (tq, tk), 1)
    low_half = lax.broadcasted_iota(I32, (tk, LANES), 1) < IDX_DIM
    iw = iw_ref[...] * (IDX_DIM ** -0.5 * IDX_HEADS ** -0.5)
    for h in range(IDX_HEADS):
        iwb_ref[h] = jnp.broadcast_to(iw[:, IKW_IW + h:IKW_IW + h + 1], (tq, LANES))
    iq_pairs = [iq_ref[:, p * LANES:(p + 1) * LANES] for p in range(IDX_HEADS // 2)]

    def idx_block(kb, carry):
        ik_lo = jnp.where(low_half, ik_ref[pl.ds(pl.multiple_of(kb * tk, tk), tk), :], 0.0)
        ik_sides = (ik_lo.astype(BF16), pltpu.roll(ik_lo, IDX_DIM, axis=1).astype(BF16))
        acc = jnp.zeros((tq, tk), F32)
        for h in range(IDX_HEADS):
            s = lax.dot_general(iq_pairs[h // 2], ik_sides[h % 2], (((1,), (1,)), ((), ())),
                                preferred_element_type=F32)
            acc = acc + jnp.maximum(s, 0.0) * _lane_tile(iwb_ref[h], tk)
        causal = (col + kb * tk) <= (row + i * tq)
        keys_ref[kb] = jnp.where(causal, _sortable_key(acc), jnp.int32(INT_MIN))
        return carry

    lax.fori_loop(0, nkb, idx_block, 0)

    fb = RADIX4_FIELD_BITS
    field_mask = (1 << fb) - 1
    inc1, inc2, inc3 = 1, 1 + (1 << fb), 1 + (1 << fb) + (1 << 2 * fb)

    def count3(c1, c2, c3):
        b1, b2, b3 = (jnp.broadcast_to(c, (tq, LANES)) for c in (c1, c2, c3))

        def cb(kb, acc):
            keys = keys_ref[kb]
            for t in range(tk // LANES):
                kk = keys[:, t * LANES:(t + 1) * LANES]
                acc = acc + jnp.where(kk >= b3, inc3, jnp.where(kk >= b2, inc2, jnp.where(kk >= b1, inc1, 0)))
            return acc

        acc = lax.fori_loop(0, nkb, cb, jnp.zeros((tq, LANES), I32))
        n3 = (acc >> (2 * fb)).astype(F32).sum(axis=-1, keepdims=True)
        low = ((acc & field_mask).astype(F32) + ((acc >> fb) & field_mask).astype(F32) * RADIX4_WIDE)
        low = low.sum(axis=-1, keepdims=True)
        n2 = jnp.floor(low * (1.0 / RADIX4_WIDE))
        return low - RADIX4_WIDE * n2, n2, n3

    n_valid = (lax.broadcasted_iota(I32, (tq, 1), 0) + (i * tq + 1)).astype(F32)
    thr = _kth_largest_key_radix4(count3, tq, k_sel, n_valid)
    thr = jnp.maximum(thr, jnp.int32(INT_MIN + 1))
    thr_b = _lane_tile(jnp.broadcast_to(thr, (tq, LANES)), tk)

    def mask_block(kb, carry):
        mb_ref[kb] = jnp.where(keys_ref[kb] >= thr_b, 0.0, MASK_NEG)
        return carry

    lax.fori_loop(0, nkb, mask_block, 0)

    scale = HEAD_DIM ** -0.5
    for g in range(N_KV_HEADS):
        qg = jnp.concatenate(
            [q_ref[:, (KV_REP * g + r) * HEAD_DIM:(KV_REP * g + r + 1) * HEAD_DIM] for r in range(KV_REP)], axis=0)
        m_ref[...] = jnp.full(m_ref.shape, MASK_NEG, F32)
        l_ref[...] = jnp.zeros(l_ref.shape, F32)
        acc_ref[...] = jnp.zeros(acc_ref.shape, F32)

        def block(kb, bias_idx, g=g, qg=qg):
            r0 = pl.multiple_of(kb * tk, tk)
            kblk = kv_ref[pl.ds(r0, tk), KV_K + g * HEAD_DIM:KV_K + (g + 1) * HEAD_DIM].astype(BF16)
            vblk = kv_ref[pl.ds(r0, tk), KV_V + g * HEAD_DIM:KV_V + (g + 1) * HEAD_DIM].astype(BF16)
            s = lax.dot_general(qg, kblk, (((1,), (1,)), ((), ())), preferred_element_type=F32) * scale
            s = s.reshape(KV_REP, tq, tk) + mb_ref[kb][None]
            if bias_idx is not None:
                s = s + bias_ref[bias_idx, KV_REP * g:KV_REP * (g + 1)]
            s = s.reshape(KV_REP * tq, tk)
            m_old = m_ref[...]
            m_new = jnp.maximum(m_old, s.max(axis=-1, keepdims=True))
            p = jnp.exp(s - _lane_tile(m_new, tk))
            alpha = jnp.exp(m_old - m_new)
            l_ref[...] = alpha * l_ref[...] + p.sum(axis=-1, keepdims=True)
            acc_ref[...] = alpha * acc_ref[...] + jnp.dot(p.astype(BF16), vblk, preferred_element_type=F32)
            m_ref[...] = m_new

        def far(kb, carry):
            block(kb, None)
            return carry

        lax.fori_loop(0, jnp.maximum(i - 1, 0), far, 0)

        @pl.when(i >= 1)
        def _():
            block(i - 1, 1)

        block(i, 0)
        out = acc_ref[...] / l_ref[...]
        for r in range(KV_REP):
            h = KV_REP * g + r
            o_ref[:, h * HEAD_DIM:(h + 1) * HEAD_DIM] = out[r * tq:(r + 1) * tq].astype(o_ref.dtype)


def _attn_prompt(ha, kv, ikw, bias_tiles, batch, seq, tq):
    nq = seq // tq
    k_sel = min(TOPK_MAX, seq // 4)
    assert seq // LANES < (1 << RADIX4_FIELD_BITS)
    body = functools.partial(_attn_prompt_body, tq=tq, k_sel=k_sel)
    return pl.pallas_call(
        body,
        grid=(batch, nq),
        in_specs=[pl.BlockSpec((tq, QKV_W), lambda b, i: (b * nq + i, HA_Q // QKV_W)),
                  pl.BlockSpec((tq, IQ_W), lambda b, i: (b * nq + i, HA_IQ // IQ_W)),
                  pl.BlockSpec((tq, LANES), lambda b, i: (b * nq + i, 0)),
                  pl.BlockSpec((seq, 2 * KV_W), lambda b, i: (b, 0), pipeline_mode=pl.Buffered(1)),
                  pl.BlockSpec((seq, LANES), lambda b, i: (b, 0), pipeline_mode=pl.Buffered(1)),
                  pl.BlockSpec((2, N_HEADS, tq, tq), lambda b, i: (0, 0, 0, 0), pipeline_mode=pl.Buffered(1))],
        out_specs=pl.BlockSpec((tq, QKV_W), lambda b, i: (b * nq + i, 0)),
        out_shape=jax.ShapeDtypeStruct((batch * seq, QKV_W), BF16),
        scratch_shapes=[pltpu.VMEM((nq, tq, tq), I32),
                        pltpu.VMEM((nq, tq, tq), F32),
                        pltpu.VMEM((IDX_HEADS, tq, LANES), F32),
                        pltpu.VMEM((KV_REP * tq, LANES), F32),
                        pltpu.VMEM((KV_REP * tq, LANES), F32),
                        pltpu.VMEM((KV_REP * tq, HEAD_DIM), F32)],
        compiler_params=_cp(("arbitrary", "arbitrary")),
        name="attn_prompt",
    )(ha, ha, ikw, kv, ikw, bias_tiles)


def _page_spec(block, layer, group, slot):
    return pl.BlockSpec((None, None) + block, lambda b, p, pt: (layer, pt[b, p * group + slot], 0, 0))


def _idx_sample_body(pt_ref, iq_ref, iw_ref, *refs, n_tok, group):
    pages, o_ref = refs[:group], refs[group]
    ikt = jnp.concatenate([r[...].astype(BF16) for r in pages], axis=1)
    s = jnp.dot(iq_ref[0], ikt, preferred_element_type=F32)
    r = jnp.maximum(s, 0.0) * _lane_tile(iw_ref[0], group * PAGE_SIZE)
    tok_scores = r.reshape(n_tok, IDX_HEADS, group * PAGE_SIZE).sum(axis=1)
    for t in range(group):
        o_ref[0, t] = tok_scores[:, t * PAGE_SIZE:(t + 1) * PAGE_SIZE]


def _idx_sample(page_table, iq_rows, iw_rows, cache_ik_t, layer, n_tok, group):
    db, n_pages = page_table.shape
    rows = n_tok * IDX_HEADS
    body = functools.partial(_idx_sample_body, n_tok=n_tok, group=group)
    grid_spec = pltpu.PrefetchScalarGridSpec(
        num_scalar_prefetch=1,
        grid=(db, n_pages // group),
        in_specs=[pl.BlockSpec((1, rows, IDX_DIM), lambda b, p, pt: (b, 0, 0)),
                  pl.BlockSpec((1, rows, LANES), lambda b, p, pt: (b, 0, 0))]
        + [_page_spec((IDX_DIM, PAGE_SIZE), layer, group, t) for t in range(group)],
        out_specs=pl.BlockSpec((1, group, n_tok, PAGE_SIZE), lambda b, p, pt: (b, p, 0, 0)),
    )
    return pl.pallas_call(
        body,
        grid_spec=grid_spec,
        out_shape=jax.ShapeDtypeStruct((db, n_pages, n_tok, PAGE_SIZE), F32),
        compiler_params=_cp(("arbitrary", "arbitrary")),
        name="idx_sample",
    )(page_table, iq_rows, iw_rows, *([cache_ik_t] * group))


def _attn_sample_body(pt_ref, sc_ref, scn_ref, q_ref, kn_ref, vn_ref, bl_ref, bn_ref, *refs,
                      n_tok, n_pages, k_sel, group):
    kp_refs, vp_refs = refs[:group], refs[group:2 * group]
    o_ref, thr_ref, m_ref, l_ref, acc_ref = refs[2 * group:]
    p = pl.program_id(1)
    n_steps = n_pages // group
    grp = KV_REP * n_tok
    width = group * PAGE_SIZE
    scale = HEAD_DIM ** -0.5
    r_i = lax.broadcasted_iota(I32, (n_tok, LANES), 0)
    c_i = lax.broadcasted_iota(I32, (n_tok, LANES), 1)
    keys_new = jnp.where(c_i <= r_i, _sortable_key(scn_ref[0]), jnp.int32(INT_MIN))

    @pl.when(p == 0)
    def _():
        keys = _sortable_key(sc_ref[0])

        def count_ge(cand):
            cand_b = jnp.broadcast_to(cand, (n_tok, LANES))
            c = jnp.where(keys >= cand_b[None], 1.0, 0.0).sum(axis=0) + jnp.where(keys_new >= cand_b, 1.0, 0.0)
            return c.sum(axis=-1, keepdims=True)

        thr = _kth_largest_key(count_ge, n_tok, float(k_sel))
        thr_ref[...] = jnp.broadcast_to(jnp.maximum(thr, jnp.int32(INT_MIN + 1)), (n_tok, LANES))
        m_ref[...] = jnp.full(m_ref.shape, MASK_NEG, F32)
        l_ref[...] = jnp.zeros(l_ref.shape, F32)
        acc_ref[...] = jnp.zeros(acc_ref.shape, F32)

    thr_b = thr_ref[...]

    def update(g, s, vblk):
        sl = slice(g * grp, (g + 1) * grp)
        m_old = m_ref[sl, :]
        m_new = jnp.maximum(m_old, s.max(axis=-1, keepdims=True))
        pr = jnp.exp(s - _lane_tile(m_new, s.shape[1]))
        alpha = jnp.exp(m_old - m_new)
        l_ref[sl, :] = alpha * l_ref[sl, :] + pr.sum(axis=-1, keepdims=True)
        acc_ref[sl, :] = alpha * acc_ref[sl, :] + jnp.dot(pr.astype(BF16), vblk, preferred_element_type=F32)
        m_ref[sl, :] = m_new

    def group_rows(mask_tok):
        return jnp.concatenate([mask_tok] * KV_REP, axis=0)

    base = p * group
    mb = jnp.concatenate(
        [jnp.where(_sortable_key(sc_ref[0, base + t]) >= thr_b, 0.0, MASK_NEG) for t in range(group)], axis=1)
    mb_g = group_rows(mb)
    is_last = p == n_steps - 1
    zeros_head = jnp.zeros((grp, width - PAGE_SIZE), F32)
    for g in range(N_KV_HEADS):
        qg = q_ref[0, g * grp:(g + 1) * grp, :]
        kg = jnp.concatenate([r[pl.ds(g, PAGE_SIZE, stride=N_KV_HEADS), :].astype(BF16) for r in kp_refs], axis=0)
        vg = jnp.concatenate([r[pl.ds(g, PAGE_SIZE, stride=N_KV_HEADS), :].astype(BF16) for r in vp_refs], axis=0)
        s = lax.dot_general(qg, kg, (((1,), (1,)), ((), ())), preferred_element_type=F32) * scale
        near = jnp.where(is_last, bl_ref[g * grp:(g + 1) * grp, :], 0.0)
        s = s + mb_g + (near if group == 1 else jnp.concatenate([zeros_head, near], axis=1))
        update(g, s, vg)

    @pl.when(is_last)
    def _():
        mbn_g = group_rows(jnp.where(keys_new >= thr_b, 0.0, MASK_NEG))
        for g in range(N_KV_HEADS):
            qg = q_ref[0, g * grp:(g + 1) * grp, :]
            kblk = kn_ref[0, :, g * HEAD_DIM:(g + 1) * HEAD_DIM].astype(BF16)
            vblk = vn_ref[0, :, g * HEAD_DIM:(g + 1) * HEAD_DIM].astype(BF16)
            s = lax.dot_general(qg, kblk, (((1,), (1,)), ((), ())), preferred_element_type=F32) * scale
            s = s + mbn_g + bn_ref[g * grp:(g + 1) * grp, :]
            update(g, s, vblk)
        o_ref[0] = acc_ref[...] / l_ref[...]


def _attn_sample(page_table, scores, scores_new, q_rows, k_new, v_new, bias_last, bias_new,
                 cache_k, cache_v, layer, n_tok, group):
    db, n_pages = page_table.shape
    past = n_pages * PAGE_SIZE
    rows = N_HEADS * n_tok
    k_sel = min(TOPK_MAX, (past + n_tok) // 4)
    body = functools.partial(_attn_sample_body, n_tok=n_tok, n_pages=n_pages, k_sel=k_sel, group=group)
    page = (PAGE_SIZE * N_KV_HEADS, HEAD_DIM)
    grid_spec = pltpu.PrefetchScalarGridSpec(
        num_scalar_prefetch=1,
        grid=(db, n_pages // group),
        in_specs=[pl.BlockSpec((1, n_pages, n_tok, PAGE_SIZE), lambda b, p, pt: (b, 0, 0, 0)),
                  pl.BlockSpec((1, n_tok, LANES), lambda b, p, pt: (b, 0, 0)),
                  pl.BlockSpec((1, rows, HEAD_DIM), lambda b, p, pt: (b, 0, 0)),
                  pl.BlockSpec((1, LANES, KV_W), lambda b, p, pt: (b, 0, 0)),
                  pl.BlockSpec((1, LANES, KV_W), lambda b, p, pt: (b, 0, 0)),
                  pl.BlockSpec((rows, LANES), lambda b, p, pt: (0, 0)),
                  pl.BlockSpec((rows, LANES), lambda b, p, pt: (0, 0))]
        + [_page_spec(page, layer, group, t) for t in range(group)]
        + [_page_spec(page, layer, group, t) for t in range(group)],
        out_specs=pl.BlockSpec((1, rows, HEAD_DIM), lambda b, p, pt: (b, 0, 0)),
        scratch_shapes=[pltpu.VMEM((n_tok, LANES), I32),
                        pltpu.VMEM((rows, LANES), F32),
                        pltpu.VMEM((rows, LANES), F32),
                        pltpu.VMEM((rows, HEAD_DIM), F32)],
    )
    return pl.pallas_call(
        body,
        grid_spec=grid_spec,
        out_shape=jax.ShapeDtypeStruct((db, rows, HEAD_DIM), F32),
        compiler_params=_cp(("arbitrary", "arbitrary")),
        name="attn_sample",
    )(page_table, scores, scores_new, q_rows, k_new, v_new, bias_last, bias_new,
      *([cache_k] * group), *([cache_v] * group))


def _idx_new_body(iq_ref, iw_ref, ik_ref, o_ref, *, n_tok):
    for b in range(iq_ref.shape[0]):
        s = lax.dot_general(iq_ref[b], ik_ref[b].astype(BF16), (((1,), (1,)), ((), ())),
                            preferred_element_type=F32)
        r = jnp.maximum(s, 0.0) * iw_ref[b]
        o_ref[b] = r.reshape(n_tok, IDX_HEADS, LANES).sum(axis=1)


def _idx_new(iq_rows, iw_rows, ik_new, n_tok):
    db = iq_rows.shape[0]
    return pl.pallas_call(
        functools.partial(_idx_new_body, n_tok=n_tok),
        out_shape=jax.ShapeDtypeStruct((db, n_tok, LANES), F32),
        name="idx_new",
    )(iq_rows, iw_rows, ik_new)


HALO = CONV_WIDTH - 1
HALO_PAD = 32


def _conv_prompt_body(glu_ref, cw_ref, cb_ref, g_ref, b_ref, o_ref, st_ref, hp_ref, c_ref, *, ts, ns):
    s_idx = pl.program_id(1)

    @pl.when(s_idx == 0)
    def _():
        hp_ref[0:HALO_PAD, :] = jnp.zeros((HALO_PAD, CONV_DIM), F32)

    a = glu_ref[:, :CONV_DIM].astype(F32)
    gt = glu_ref[:, CONV_DIM:].astype(F32)
    hp_ref[HALO_PAD:HALO_PAD + ts, :] = a * _sigmoid(gt)
    off = HALO_PAD - HALO
    for c in range(CONV_DIM // LANES):
        cs = slice(c * LANES, (c + 1) * LANES)
        acc = jnp.zeros((ts, LANES), F32)
        for w in range(CONV_WIDTH):
            acc = acc + hp_ref[off + w:off + w + ts, cs] * cw_ref[w:w + 1, cs]
        c_ref[:, cs] = acc + cb_ref[:, cs]
    y = _ln_rows(c_ref[...], g_ref[...], b_ref[...])
    o_ref[...] = (y * _sigmoid(y)).astype(o_ref.dtype)
    tail = hp_ref[ts + off:ts + HALO_PAD, :]

    @pl.when(s_idx == ns - 1)
    def _():
        st_ref[0] = tail

    hp_ref[off:HALO_PAD, :] = tail


def _conv_prompt(ha, conv_w, conv_b, ln_g, ln_b, batch, seq, ts):
    ns = seq // ts
    body = functools.partial(_conv_prompt_body, ts=ts, ns=ns)
    vec = pl.BlockSpec((1, CONV_DIM), lambda b, s: (0, 0))
    return pl.pallas_call(
        body,
        grid=(batch, ns),
        in_specs=[pl.BlockSpec((ts, 2 * CONV_DIM), lambda b, s: (b * ns + s, HA_GLU // (2 * CONV_DIM))),
                  pl.BlockSpec((CONV_WIDTH, CONV_DIM), lambda b, s: (0, 0)), vec, vec, vec],
        out_specs=[pl.BlockSpec((ts, CONV_DIM), lambda b, s: (b * ns + s, 0)),
                   pl.BlockSpec((1, HALO, CONV_DIM), lambda b, s: (b, 0, 0))],
        out_shape=[jax.ShapeDtypeStruct((batch * seq, CONV_DIM), BF16),
                   jax.ShapeDtypeStruct((batch, HALO, CONV_DIM), F32)],
        scratch_shapes=[pltpu.VMEM((HALO_PAD + ts, CONV_DIM), F32), pltpu.VMEM((ts, CONV_DIM), F32)],
        compiler_params=_cp(("arbitrary", "arbitrary")),
        name="conv_prompt",
    )(ha, conv_w, conv_b, ln_g, ln_b)


def _sgu_prompt_body(uv_ref, g_ref, b_ref, w_ref, bs_ref, o_ref, *, ts):
    u = uv_ref[:, :SGU_DIM].astype(F32)
    vn = _ln_rows(uv_ref[:, SGU_DIM:].astype(F32), g_ref[...], b_ref[...]).astype(BF16)
    r_i = lax.broadcasted_iota(I32, (CHUNK, CHUNK), 0)
    c_i = lax.broadcasted_iota(I32, (CHUNK, CHUNK), 1)
    gd = SGU_DIM // SGU_GROUPS
    for g in range(SGU_GROUPS):
        wm = jnp.where(c_i <= r_i, w_ref[g], 0.0).astype(BF16)
        bias = bs_ref[:, g:g + 1]
        for c in range(ts // CHUNK):
            rs = slice(c * CHUNK, (c + 1) * CHUNK)
            gs = slice(g * gd, (g + 1) * gd)
            mixed = jnp.dot(wm, vn[rs, gs], preferred_element_type=F32) + bias
            o_ref[rs, gs] = (u[rs, gs] * mixed).astype(o_ref.dtype)


def _sgu_prompt(ha, ln_g, ln_b, w_s, b_s_t, batch, seq, ts):
    ns = seq // ts
    vec = pl.BlockSpec((1, SGU_DIM), lambda b, s: (0, 0))
    return pl.pallas_call(
        functools.partial(_sgu_prompt_body, ts=ts),
        grid=(batch, ns),
        in_specs=[pl.BlockSpec((ts, 2 * SGU_DIM), lambda b, s: (b * ns + s, HA_UV // (2 * SGU_DIM))),
                  vec, vec,
                  pl.BlockSpec((SGU_GROUPS, CHUNK, CHUNK), lambda b, s: (0, 0, 0)),
                  pl.BlockSpec((CHUNK, SGU_GROUPS), lambda b, s: (0, 0))],
        out_specs=pl.BlockSpec((ts, SGU_DIM), lambda b, s: (b * ns + s, 0)),
        out_shape=jax.ShapeDtypeStruct((batch * seq, SGU_DIM), BF16),
        compiler_params=_cp(("arbitrary", "arbitrary")),
        name="sgu_prompt",
    )(ha, ln_g, ln_b, w_s, b_s_t)


def _mix_sample_body(glu_ref, uv_ref, st_ref, cw_ref, cb_ref, cg_ref, cbt_ref, sg_ref, sb_ref, wv_ref, bv_ref,
                     co_ref, nst_ref, so_ref, vn_ref, hp_ref, *, db, n_tok):
    cw = cw_ref[...]
    for b in range(db):
        rs = slice(b * n_tok, (b + 1) * n_tok)
        a = glu_ref[rs, :CONV_DIM].astype(F32)
        gt = glu_ref[rs, CONV_DIM:].astype(F32)
        hp_ref[0:HALO, :] = st_ref[b]
        hp_ref[HALO:HALO + n_tok, :] = a * _sigmoid(gt)
        rows = [jnp.sum(hp_ref[t:t + CONV_WIDTH, :] * cw, axis=0, keepdims=True) for t in range(n_tok)]
        c = jnp.concatenate(rows, axis=0) + cb_ref[...]
        y = _ln_rows(c, cg_ref[...], cbt_ref[...])
        co_ref[rs, :] = y * _sigmoid(y)
        nst_ref[b] = hp_ref[n_tok:n_tok + HALO, :]
        u = uv_ref[rs, :SGU_DIM].astype(F32)
        vn = _ln_rows(uv_ref[rs, SGU_DIM:].astype(F32), sg_ref[...], sb_ref[...])
        vn_ref[rs, :] = vn
        mixed = []
        for t in range(n_tok):
            acc = bv_ref[t:t + 1, :]
            for s in range(t + 1):
                acc = acc + wv_ref[t, s:s + 1, :] * vn[s:s + 1, :]
            mixed.append(acc)
        so_ref[rs, :] = u * jnp.concatenate(mixed, axis=0)


def _mix_sample(glu, uv, state, conv_w, conv_b, cln_g, cln_b, sln_g, sln_b, wv, bv, db, n_tok):
    rows = db * n_tok
    return pl.pallas_call(
        functools.partial(_mix_sample_body, db=db, n_tok=n_tok),
        out_shape=[jax.ShapeDtypeStruct((rows, CONV_DIM), F32),
                   jax.ShapeDtypeStruct((db, HALO, CONV_DIM), F32),
                   jax.ShapeDtypeStruct((rows, SGU_DIM), F32),
                   jax.ShapeDtypeStruct((rows, SGU_DIM), F32)],
        scratch_shapes=[pltpu.VMEM((_round_up(HALO + n_tok, 8), CONV_DIM), F32)],
        name="mix_sample",
    )(glu, uv, state, conv_w, conv_b, cln_g, cln_b, sln_g, sln_b, wv, bv)


def _router_body(x_ref, w_ref, b_ref, ti_ref, tg_ref, *, n_exp):
    logits = jnp.dot(x_ref[...], w_ref[...], preferred_element_type=F32,
                     precision=lax.Precision.HIGHEST) + b_ref[...]
    lane = lax.broadcasted_iota(I32, logits.shape, 1)
    lane_f = lane.astype(F32)
    neg = -jnp.inf
    lg = jnp.where(lane < n_exp, logits, neg)
    v1 = lg.max(axis=-1, keepdims=True)
    i1 = jnp.where(lg == v1, lane_f, float(LANES)).min(axis=-1, keepdims=True)
    lg2 = jnp.where(lane_f == i1, neg, lg)
    v2 = lg2.max(axis=-1, keepdims=True)
    i2 = jnp.where(lg2 == v2, lane_f, float(LANES)).min(axis=-1, keepdims=True)
    e = jnp.exp(v2 - v1)
    den = 1.0 + e
    ti_ref[...] = jnp.where(lane == 0, i1, jnp.where(lane == 1, i2, 0.0)).astype(I32)
    tg_ref[...] = jnp.where(lane == 0, 1.0 / den, jnp.where(lane == 1, e / den, 0.0))


def _router(x, w_pad, b_pad, n_exp, tm):
    m, d = x.shape
    return pl.pallas_call(
        functools.partial(_router_body, n_exp=n_exp),
        grid=(m // tm,),
        in_specs=[pl.BlockSpec((tm, d), lambda i: (i, 0)),
                  pl.BlockSpec((d, LANES), lambda i: (0, 0)),
                  pl.BlockSpec((1, LANES), lambda i: (0, 0))],
        out_specs=[pl.BlockSpec((tm, LANES), lambda i: (i, 0)),
                   pl.BlockSpec((tm, LANES), lambda i: (i, 0))],
        out_shape=[jax.ShapeDtypeStruct((m, LANES), I32), jax.ShapeDtypeStruct((m, LANES), F32)],
        compiler_params=_cp(("arbitrary",)),
        name="moe_router",
    )(x, w_pad, b_pad)


def _row_copy(src_hbm, dst, src_row, dst_row, sem):
    return pltpu.make_async_copy(src_hbm.at[pl.ds(src_row, 1)], dst.at[pl.ds(dst_row, 1)], sem)


ROW_DMA_UNROLL = 8


def _prefetch_chunks(issue, wait):
    c = pl.program_id(0)

    @pl.when(c == 0)
    def _():
        issue(c, 0)

    @pl.when(c + 1 < pl.num_programs(0))
    def _():
        issue(c + 1, (c + 1) % 2)

    slot = c % 2
    wait(slot)
    return slot


def _dispatch_body(src_ref, x_hbm, o_ref, buf, sem, *, rows):
    def issue(chunk, slot):
        def start(u, carry):
            for v in range(ROW_DMA_UNROLL):
                r = u * ROW_DMA_UNROLL + v
                _row_copy(x_hbm, buf.at[slot], src_ref[chunk * rows + r], r, sem.at[slot]).start()
            return carry

        lax.fori_loop(0, rows // ROW_DMA_UNROLL, start, 0)

    def wait(slot):
        def w(r, carry):
            _row_copy(x_hbm, buf.at[slot], 0, r, sem.at[slot]).wait()
            return carry

        lax.fori_loop(0, rows, w, 0)

    slot = _prefetch_chunks(issue, wait)
    o_ref[...] = buf[slot].astype(o_ref.dtype)


def _dispatch(src, x, n_slots, rows):
    d = x.shape[1]
    grid_spec = pltpu.PrefetchScalarGridSpec(
        num_scalar_prefetch=1,
        grid=(n_slots // rows,),
        in_specs=[pl.BlockSpec(memory_space=pl.ANY)],
        out_specs=pl.BlockSpec((rows, d), lambda i, s: (i, 0)),
        scratch_shapes=[pltpu.VMEM((2, rows, d), x.dtype), pltpu.SemaphoreType.DMA((2,))],
    )
    assert rows % ROW_DMA_UNROLL == 0
    return pl.pallas_call(
        functools.partial(_dispatch_body, rows=rows),
        grid_spec=grid_spec,
        out_shape=jax.ShapeDtypeStruct((n_slots, d), BF16),
        compiler_params=_cp(("arbitrary",)),
        name="moe_dispatch",
    )(src, x)


def _tile_copy(src, dst, sem):
    return pltpu.make_async_copy(src, dst, sem)


def _moe_ffn_body(exp_ref, row0_ref, nsub_ref, jmap_ref, nzero_ref, xs_hbm, w1_ref, w3_ref, w2_ref, ys_hbm,
                  xbuf, acc, w1b, w3b, w2b, sem, *, nj, sub):
    w = pl.program_id(0)
    j = pl.program_id(1)
    nsub = nsub_ref[w]
    row0 = row0_ref[w]
    nzero = nzero_ref[w]

    def zero_copy(t):
        r = pl.multiple_of(row0 + t * sub, sub)
        return _tile_copy(acc.at[pl.ds(0, sub)], ys_hbm.at[pl.ds(r, sub)], sem)

    @pl.when(jnp.logical_and(j == 0, nzero > 0))
    def _():
        acc[0:sub, :] = jnp.zeros((sub, acc.shape[1]), F32)

        def zs(t, carry):
            zero_copy(t).start()
            return carry

        def zw(t, carry):
            zero_copy(t).wait()
            return carry

        lax.fori_loop(0, nzero, zs, 0)
        lax.fori_loop(0, nzero, zw, 0)

    def in_copy(t):
        r = pl.multiple_of(t * sub, sub)
        return _tile_copy(xs_hbm.at[pl.ds(pl.multiple_of(row0 + r, sub), sub)], xbuf.at[pl.ds(r, sub)], sem)

    def out_copy(t):
        r = pl.multiple_of(t * sub, sub)
        return _tile_copy(acc.at[pl.ds(r, sub)], ys_hbm.at[pl.ds(pl.multiple_of(row0 + r, sub), sub)], sem)

    def for_tiles(fn):
        def it(t, carry):
            fn(t)
            return carry
        lax.fori_loop(0, nsub, it, 0)

    def zero_tile(t):
        acc[pl.ds(pl.multiple_of(t * sub, sub), sub), :] = jnp.zeros((sub, acc.shape[1]), F32)

    @pl.when(j == 0)
    def _():
        for_tiles(lambda t: in_copy(t).start())
        for_tiles(zero_tile)
        for_tiles(lambda t: in_copy(t).wait())

    @pl.when(nsub > 0)
    def _():
        w1b[...] = w1_ref[...].astype(BF16)
        w3b[...] = w3_ref[...].astype(BF16)
        w2b[...] = w2_ref[...].astype(BF16)

    def tile(t):
        r = pl.multiple_of(t * sub, sub)
        x = xbuf[pl.ds(r, sub), :]
        a = jnp.dot(x, w1b[...], preferred_element_type=F32)
        c = jnp.dot(x, w3b[...], preferred_element_type=F32)
        h = (a * _sigmoid(a) * c).astype(BF16)
        acc[pl.ds(r, sub), :] += jnp.dot(h, w2b[...], preferred_element_type=F32)

    def pair(t, carry):
        tile(2 * t)
        tile(2 * t + 1)
        return carry

    lax.fori_loop(0, nsub // 2, pair, 0)

    @pl.when(nsub % 2 == 1)
    def _():
        tile(nsub - 1)

    @pl.when(j == nj - 1)
    def _():
        for_tiles(lambda t: out_copy(t).start())
        for_tiles(lambda t: out_copy(t).wait())


def _moe_ffn(item_exp, item_row0, item_nsub, item_jlast, item_nzero, xs, w1, w3, w2, layer, n_items, super_rows,
             sub, tf):
    n_slots, d = xs.shape
    f = w1.shape[3]
    nj = f // tf

    def jj(w, j, jl):
        return jnp.where(jl[w] > 0, j, nj - 1)

    grid_spec = pltpu.PrefetchScalarGridSpec(
        num_scalar_prefetch=5,
        grid=(n_items, nj),
        in_specs=[pl.BlockSpec(memory_space=pl.ANY),
                  pl.BlockSpec((None, None, d, tf), lambda w, j, ex, r0, ns, jl, nz: (layer, ex[w], 0, jj(w, j, jl))),
                  pl.BlockSpec((None, None, d, tf), lambda w, j, ex, r0, ns, jl, nz: (layer, ex[w], 0, jj(w, j, jl))),
                  pl.BlockSpec((None, None, tf, d), lambda w, j, ex, r0, ns, jl, nz: (layer, ex[w], jj(w, j, jl), 0))],
        out_specs=pl.BlockSpec(memory_space=pl.ANY),
        scratch_shapes=[pltpu.VMEM((super_rows, d), BF16),
                        pltpu.VMEM((super_rows, d), F32),
                        pltpu.VMEM((d, tf), BF16),
                        pltpu.VMEM((d, tf), BF16),
                        pltpu.VMEM((tf, d), BF16),
                        pltpu.SemaphoreType.DMA(())],
    )
    return pl.pallas_call(
        functools.partial(_moe_ffn_body, nj=nj, sub=sub),
        grid_spec=grid_spec,
        out_shape=jax.ShapeDtypeStruct((n_slots, d), F32),
        compiler_params=_cp(("arbitrary", "arbitrary")),
        name="moe_ffn",
    )(item_exp, item_row0, item_nsub, item_jlast, item_nzero, xs, w1, w3, w2)


def _combine_body(pos_ref, ys_hbm, tg_ref, x_ref, g_ref, b_ref, of_ref, ob_ref, buf, sem, *, tb, alpha):
    def issue(chunk, slot):
        def start(u, carry):
            for v in range(ROW_DMA_UNROLL // TOP_K_EXPERTS):
                r = u * (ROW_DMA_UNROLL // TOP_K_EXPERTS) + v
                for k in range(TOP_K_EXPERTS):
                    p = pos_ref[TOP_K_EXPERTS * (chunk * tb + r) + k]
                    _row_copy(ys_hbm, buf.at[slot, k], p, r, sem.at[slot]).start()
            return carry

        lax.fori_loop(0, tb * TOP_K_EXPERTS // ROW_DMA_UNROLL, start, 0)

    def wait(slot):
        def w(r, carry):
            for k in range(TOP_K_EXPERTS):
                _row_copy(ys_hbm, buf.at[slot, k], 0, r, sem.at[slot]).wait()
            return carry

        lax.fori_loop(0, tb, w, 0)

    slot = _prefetch_chunks(issue, wait)
    f = tg_ref[:, 0:1] * buf[slot, 0] + tg_ref[:, 1:2] * buf[slot, 1]
    y = _ln_rows(alpha * x_ref[...] + f, g_ref[...], b_ref[...])
    of_ref[...] = y
    ob_ref[...] = y.astype(BF16)


def _combine_ln(pos, ys, tg, x, g, b, alpha, tb):
    m, d = x.shape
    grid_spec = pltpu.PrefetchScalarGridSpec(
        num_scalar_prefetch=1,
        grid=(m // tb,),
        in_specs=[pl.BlockSpec(memory_space=pl.ANY),
                  pl.BlockSpec((tb, LANES), lambda i, p: (i, 0)),
                  pl.BlockSpec((tb, d), lambda i, p: (i, 0)),
                  pl.BlockSpec((1, d), lambda i, p: (0, 0)),
                  pl.BlockSpec((1, d), lambda i, p: (0, 0))],
        out_specs=[pl.BlockSpec((tb, d), lambda i, p: (i, 0)),
                   pl.BlockSpec((tb, d), lambda i, p: (i, 0))],
        scratch_shapes=[pltpu.VMEM((2, TOP_K_EXPERTS, tb, d), F32), pltpu.SemaphoreType.DMA((2,))],
    )
    assert tb * TOP_K_EXPERTS % ROW_DMA_UNROLL == 0
    return pl.pallas_call(
        functools.partial(_combine_body, tb=tb, alpha=alpha),
        grid_spec=grid_spec,
        out_shape=[jax.ShapeDtypeStruct((m, d), F32), jax.ShapeDtypeStruct((m, d), BF16)],
        compiler_params=_cp(("arbitrary",)),
        name="moe_combine_ln",
    )(pos, ys, tg, x, g, b)


def _moe_plan(top_i, n_exp, n_slots, n_items, super_rows, sub):
    a = top_i.shape[0] * TOP_K_EXPERTS
    e_flat = top_i.reshape(-1)
    onehot = (e_flat[:, None] == jnp.arange(n_exp, dtype=I32)[None, :]).astype(I32)
    csum = jnp.cumsum(onehot, axis=0)
    rank = jnp.take_along_axis(csum, e_flat[:, None], axis=1)[:, 0] - 1
    counts = csum[-1]
    padded = (counts + sub - 1) // sub * sub
    gstart = jnp.cumsum(padded) - padded
    pos = (gstart[e_flat] + rank).astype(I32)
    src = jnp.zeros((n_slots,), I32).at[pos].set(jnp.arange(a, dtype=I32) // TOP_K_EXPERTS)
    n_super = (padded + super_rows - 1) // super_rows
    iend = jnp.cumsum(n_super)
    istart = iend - n_super
    total = iend[-1]
    wi = jnp.arange(n_items, dtype=I32)
    valid = wi < total
    e_w = jnp.minimum(jnp.searchsorted(iend, wi, side="right"), n_exp - 1).astype(I32)
    last_e = e_w[jnp.maximum(total - 1, 0)]
    e_w = jnp.where(valid, e_w, last_e)
    k_w = wi - istart[e_w]
    nsub = jnp.where(valid, jnp.minimum(super_rows, padded[e_w] - k_w * super_rows) // sub, 0).astype(I32)
    z0 = jnp.sum(padded) + (wi - total) * super_rows
    nzero = jnp.where(valid, 0, jnp.clip((n_slots - z0) // sub, 0, super_rows // sub)).astype(I32)
    row0 = jnp.where(valid, gstart[e_w] + k_w * super_rows, jnp.where(nzero > 0, z0, 0)).astype(I32)
    return pos, src, e_w, row0, nsub, valid.astype(I32), nzero


def _rel_bucket(dist):
    n = jnp.maximum(dist, 0)
    max_exact = REL_BUCKETS // 2
    nf = jnp.maximum(n, 1).astype(F32)
    large = max_exact + (jnp.log(nf / max_exact) / math.log(REL_MAX_DIST / max_exact)
                         * (REL_BUCKETS - max_exact)).astype(I32)
    large = jnp.minimum(large, REL_BUCKETS - 1)
    return jnp.where(n < max_exact, n, large)


def _shifted_bias(rel_bias, dist):
    t = rel_bias[_rel_bucket(dist)] - rel_bias[REL_BUCKETS - 1]
    t = jnp.where((dist >= 0)[..., None], t, 0.0)
    return jnp.moveaxis(t, -1, 0)


def _pick_tile(m_rows, target):
    return max(t for t in range(16, target + 1, 16) if m_rows % t == 0)


def _pad_rows(x, rows):
    return jnp.pad(x, ((0, rows - x.shape[0]), (0, 0)))


def kernel(x_prompt, x_sample, cache_k, cache_v, cache_ik, state_conv, page_table, w_in, conv_w, conv_b,
           conv_ln_g, conv_ln_b, sgu_ln_g, sgu_ln_b, sgu_w, sgu_b, w_pa, w_pb, w_pc, w_out, ln1_g, ln1_b,
           ln2_g, ln2_b, rel_bias, ffn_w1, ffn_w3, ffn_w2, moe_router, moe_router_b, moe_w1, moe_w3, moe_w2):
    batch, seq, d_model = x_prompt.shape
    db, n_tok, _ = x_sample.shape
    depth = w_in.shape[0]
    n_pool = cache_k.shape[1]
    n_exp = moe_router.shape[2]
    mp, ms = batch * seq, db * n_tok
    assert ms <= SAMPLE_TILE and mp % SAMPLE_TILE == 0
    m_all = mp + SAMPLE_TILE
    tm = m_all // N_ROW_TILES
    assert tm * N_ROW_TILES == m_all and tm % 16 == 0
    alpha = (2 * depth) ** 0.25
    tq = min(256, seq)

    xf = jnp.concatenate([x_prompt.reshape(mp, d_model), x_sample.reshape(ms, d_model),
                          jnp.zeros((m_all - mp - ms, d_model), F32)], axis=0)
    xb = xf.astype(BF16)
    w_in_t = jnp.swapaxes(w_in, 1, 2)
    ha_rows = ([W_Q_ROW0 + HA_TILE * t for t in range(QKV_W // HA_TILE)]
               + [W_GLU_ROW0 + HA_TILE * t for t in range(2 * CONV_DIM // HA_TILE)]
               + [W_UV_ROW0 + HA_TILE * t for t in range(2 * SGU_DIM // HA_TILE)]
               + [W_GATE_ROW0 + HA_TILE * t for t in range(N_BRANCH * d_model // HA_TILE)]
               + [W_IQ_ROW0 + HA_TILE * t for t in range(IQ_W // HA_TILE)])
    w_pa_b, w_pb_b, w_pc_b, w_out_b = (w.astype(BF16) for w in (w_pa, w_pb, w_pc, w_out))
    ffn_w1_b, ffn_w3_b, ffn_w2_b = (w.astype(BF16) for w in (ffn_w1, ffn_w3, ffn_w2))

    kk = jnp.arange(2 * tq, dtype=I32)
    d_wrap = jnp.where(kk < tq, -kk, 2 * tq - kk)

    def toeplitz(offset):
        v = _shifted_bias(rel_bias, d_wrap + offset)
        return jnp.tile(v, (1, tq))[:, :tq * (2 * tq - 1)].reshape(N_HEADS, tq, 2 * tq - 1)[:, :, :tq]

    bias_tiles = jnp.stack([toeplitz(0), toeplitz(tq)])
    tok = jnp.arange(n_tok, dtype=I32)
    lane = jnp.arange(LANES, dtype=I32)
    d_last = PAGE_SIZE + tok[:, None] - lane[None, :]
    bias_last = _shifted_bias(rel_bias, d_last).reshape(N_HEADS * n_tok, LANES)
    bias_new = _shifted_bias(rel_bias, tok[:, None] - lane[None, :]).reshape(N_HEADS * n_tok, LANES)

    ck = cache_k.reshape(depth, n_pool, PAGE_SIZE * N_KV_HEADS, HEAD_DIM)
    cv = cache_v.reshape(depth, n_pool, PAGE_SIZE * N_KV_HEADS, HEAD_DIM)
    cik_t = jnp.swapaxes(cache_ik, 2, 3)
    n_pages = page_table.shape[1]
    idx_group = math.gcd(n_pages, IDX_PAGE_GROUP)
    attn_group = math.gcd(n_pages, ATTN_PAGE_GROUP)
    n_slots = _round_up(TOP_K_EXPERTS * m_all + n_exp * (MOE_SUB - 1), MOE_SUB)
    n_items = -(-n_slots // MOE_SUPER) + n_exp

    outs = {k: [] for k in ("k", "v", "ik", "conv_p", "conv_s", "sgu_s")}
    for l in range(depth):
        ha = _in_proj(xb, w_in_t, l, ha_rows, tm, HA_TILE)
        kv, ikw = _kv_proj(xb, w_in_t, l, tm)

        a_p = _attn_prompt(ha, kv, ikw, bias_tiles, batch, seq, tq)
        c_p, conv_state_p = _conv_prompt(ha, conv_w[l], conv_b[l][None], conv_ln_g[l][None], conv_ln_b[l][None],
                                         batch, seq, min(256, seq))
        s_p = _sgu_prompt(ha, sgu_ln_g[l][None], sgu_ln_b[l][None], sgu_w[l], sgu_b[l].T, batch, seq,
                          min(512, seq))

        has, kvs, ikws = ha[mp:mp + ms], kv[mp:mp + ms], ikw[mp:mp + ms]
        iq_rows = has[:, HA_IQ:HA_IQ + IQ_W].reshape(db, n_tok * IDX_HEADS, IDX_DIM)
        iw_rows = (ikws[:, IKW_IW:IKW_IW + IDX_HEADS] * (IDX_DIM ** -0.5 * IDX_HEADS ** -0.5)
                   ).reshape(db, n_tok * IDX_HEADS, 1)
        iw_rows = jnp.broadcast_to(iw_rows, (db, n_tok * IDX_HEADS, LANES))
        scores = _idx_sample(page_table, iq_rows, iw_rows, cik_t, l, n_tok, idx_group)
        pad_rows = LANES - n_tok
        ik_new = jnp.pad(ikws[:, :IDX_DIM].reshape(db, n_tok, IDX_DIM), ((0, 0), (0, pad_rows), (0, 0)))
        scores_new = _idx_new(iq_rows, iw_rows, ik_new, n_tok)
        q_rows = has[:, HA_Q:HA_Q + QKV_W].reshape(db, n_tok, N_HEADS, HEAD_DIM).transpose(0, 2, 1, 3)
        q_rows = q_rows.reshape(db, N_HEADS * n_tok, HEAD_DIM)
        k_new = jnp.pad(kvs[:, KV_K:KV_K + KV_W].reshape(db, n_tok, KV_W), ((0, 0), (0, pad_rows), (0, 0)))
        v_new = jnp.pad(kvs[:, KV_V:KV_V + KV_W].reshape(db, n_tok, KV_W), ((0, 0), (0, pad_rows), (0, 0)))
        a_s = _attn_sample(page_table, scores, scores_new, q_rows, k_new, v_new, bias_last, bias_new,
                           ck, cv, l, n_tok, attn_group)
        a_s = a_s.reshape(db, N_HEADS, n_tok, HEAD_DIM).transpose(0, 2, 1, 3).reshape(ms, QKV_W)
        gd = SGU_DIM // SGU_GROUPS
        wv = jnp.repeat(sgu_w[l][:, :n_tok, :n_tok].transpose(1, 2, 0), gd, axis=-1)
        bv = jnp.repeat(sgu_b[l][:, :n_tok].T, gd, axis=-1)
        c_s, conv_state_s, s_s, vn_s = _mix_sample(
            has[:, HA_GLU:HA_GLU + 2 * CONV_DIM], has[:, HA_UV:HA_UV + 2 * SGU_DIM], state_conv[l],
            conv_w[l], conv_b[l][None], conv_ln_g[l][None], conv_ln_b[l][None],
            sgu_ln_g[l][None], sgu_ln_b[l][None], wv, bv, db, n_tok)

        sample_rows = tuple(_pad_rows(t.astype(BF16), SAMPLE_TILE) for t in (a_s, c_s, s_s))
        merged = _merge((a_p, c_p, s_p), sample_rows, ha, w_pa_b, w_pb_b, w_pc_b, l, SAMPLE_TILE)
        x1f, x1b = _mm_ln(merged, w_out_b, l, xf, ln1_g[l][None], ln1_b[l][None], alpha, _pick_tile(m_all, 528))

        j = l // 2
        if l % 2 == 0:
            h = _swiglu_up(x1b, ffn_w1_b, ffn_w3_b, j, tm, 512)
            xf, xb = _mm_ln(h, ffn_w2_b, j, x1f, ln2_g[l][None], ln2_b[l][None], alpha, _pick_tile(m_all, 384))
        else:
            w_r = jnp.pad(moe_router[j], ((0, 0), (0, LANES - n_exp)))
            b_r = jnp.pad(moe_router_b[j], (0, LANES - n_exp))[None]
            ti, tg = _router(x1f, w_r, b_r, n_exp, _pick_tile(m_all, 528))
            pos, src, it_e, it_r0, it_ns, it_valid, it_nz = _moe_plan(ti[:, :TOP_K_EXPERTS], n_exp, n_slots,
                                                                      n_items, MOE_SUPER, MOE_SUB)
            xs = _dispatch(src, x1f, n_slots, MOE_SUB)
            ys = _moe_ffn(it_e, it_r0, it_ns, it_valid, it_nz, xs, moe_w1, moe_w3, moe_w2, j, n_items,
                          MOE_SUPER, MOE_SUB, MOE_TF)
            tb = COMBINE_TB if m_all % COMBINE_TB == 0 else 16
            xf, xb = _combine_ln(pos, ys, tg, x1f, ln2_g[l][None], ln2_b[l][None], alpha, tb)

        outs["k"].append(kv[:, KV_K:KV_K + KV_W])
        outs["v"].append(kv[:, KV_V:KV_V + KV_W])
        outs["ik"].append(ikw[:, :IDX_DIM])
        outs["conv_p"].append(conv_state_p)
        outs["conv_s"].append(conv_state_s)
        outs["sgu_s"].append(vn_s.reshape(db, n_tok, SGU_DIM))

    def split(name, width_shape):
        st = jnp.stack(outs[name])
        p = st[:, :mp].reshape((depth, batch, seq) + width_shape)
        s = st[:, mp:mp + ms].reshape((depth, db, n_tok) + width_shape)
        return p, s

    k_p, k_s = split("k", (N_KV_HEADS, HEAD_DIM))
    v_p, v_s = split("v", (N_KV_HEADS, HEAD_DIM))
    ik_p, ik_s = split("ik", (IDX_DIM,))
    y_prompt = xf[:mp].reshape(batch, seq, d_model)
    y_sample = xf[mp:mp + ms].reshape(db, n_tok, d_model)
    return (y_prompt, y_sample, k_p, v_p, ik_p, jnp.stack(outs["conv_p"]), k_s, v_s, ik_s,
            jnp.stack(outs["conv_s"]), jnp.stack(outs["sgu_s"]))
```

```python
import functools
import math

import jax
import jax.numpy as jnp
import numpy as np
from jax import lax
from jax.experimental import pallas as pl
from jax.experimental.pallas import tpu as pltpu

F32 = jnp.float32
BF16 = jnp.bfloat16
I32 = jnp.int32

N_HEADS = 16
HEAD_DIM = 128
N_KV_HEADS = 4
KV_REP = N_HEADS // N_KV_HEADS
IDX_HEADS = 16
IDX_DIM = 64
TOPK_MAX = 256
PAGE_SIZE = 128
CONV_DIM = 1024
CONV_WIDTH = 31
SGU_DIM = 1024
SGU_GROUPS = 8
CHUNK = 128
N_BRANCH = 3
TOP_K_EXPERTS = 2
REL_BUCKETS = 32
REL_MAX_DIST = 128
LN_EPS = 1e-5

V7X_VMEM_BYTES = 64 * 2**20
VMEM_LIMIT = V7X_VMEM_BYTES - 8 * 2**20
LANES = 128
F32_SUBLANES = 8
INT_MIN = -(2**31)
MASK_NEG = -1e30

QKV_W = N_HEADS * HEAD_DIM
KV_W = N_KV_HEADS * HEAD_DIM
IQ_W = IDX_HEADS * IDX_DIM
HA_Q, HA_GLU, HA_UV, HA_GATE, HA_IQ = 0, 2048, 4096, 6144, 12288
HA_TILE = 1024
W_Q_ROW0 = 0
W_KV_ROW0 = W_Q_ROW0 + QKV_W
W_IQ_ROW0 = W_KV_ROW0 + 2 * KV_W
W_IK_ROW0 = W_IQ_ROW0 + IQ_W
W_GLU_ROW0 = W_IK_ROW0 + IDX_DIM + IDX_HEADS
W_UV_ROW0 = W_GLU_ROW0 + 2 * CONV_DIM
W_GATE_ROW0 = W_UV_ROW0 + 2 * SGU_DIM
KV_K, KV_V = 0, KV_W
IKW_IW = IDX_DIM

N_ROW_TILES = 8
SAMPLE_TILE = 256
MOE_SUB = 256
MOE_SUPER = 2048
MOE_TF = 256
COMBINE_TB = 128
IDX_PAGE_GROUP = 32
ATTN_PAGE_GROUP = 16


def _cp(sem, vmem=VMEM_LIMIT):
    return pltpu.CompilerParams(dimension_semantics=sem, vmem_limit_bytes=vmem)


def _round_up(x, m):
    return (x + m - 1) // m * m


def _ln_rows(z, g, b):
    mu = jnp.mean(z, axis=-1, keepdims=True)
    d = z - mu
    var = jnp.mean(d * d, axis=-1, keepdims=True)
    return d * lax.rsqrt(var + LN_EPS) * g + b


def _sigmoid(x):
    return 1.0 / (1.0 + jnp.exp(-x))


def _in_proj_body(off_ref, x_ref, w_ref, o_ref, wt_ref):
    @pl.when(pl.program_id(1) == 0)
    def _():
        wt_ref[...] = w_ref[0].T.astype(BF16)

    o_ref[...] = jnp.dot(x_ref[...], wt_ref[...], preferred_element_type=F32).astype(o_ref.dtype)


def _in_proj(x, w_t, layer, row_offsets, tm, tn):
    m, k = x.shape
    n_tiles = len(row_offsets)
    grid_spec = pltpu.PrefetchScalarGridSpec(
        num_scalar_prefetch=1,
        grid=(n_tiles, m // tm),
        in_specs=[pl.BlockSpec((tm, k), lambda j, i, off: (i, 0)),
                  pl.BlockSpec((pl.Element(1), pl.Element(tn), pl.Element(k)),
                               lambda j, i, off: (layer, pl.multiple_of(off[j], F32_SUBLANES), 0))],
        out_specs=pl.BlockSpec((tm, tn), lambda j, i, off: (i, j)),
        scratch_shapes=[pltpu.VMEM((k, tn), BF16)],
    )
    return pl.pallas_call(
        _in_proj_body,
        grid_spec=grid_spec,
        out_shape=jax.ShapeDtypeStruct((m, n_tiles * tn), BF16),
        compiler_params=_cp(("arbitrary", "arbitrary")),
        name="in_proj",
    )(jnp.asarray(row_offsets, I32), x, w_t)


def _kv_proj_body(x_ref, wkv_ref, wik_ref, kv_ref, ik_ref, wkv_t, wik_t):
    @pl.when(pl.program_id(0) == 0)
    def _():
        wkv_t[...] = wkv_ref[...].T.astype(BF16)
        wik_t[...] = wik_ref[...].T.astype(BF16)

    x = x_ref[...]
    kv_ref[...] = jnp.dot(x, wkv_t[...], preferred_element_type=F32)
    ik_ref[...] = jnp.dot(x, wik_t[...], preferred_element_type=F32)


def _kv_proj(x, w_t, layer, tm):
    m, k = x.shape
    assert W_KV_ROW0 % (2 * KV_W) == 0 and W_IK_ROW0 % LANES == 0
    return pl.pallas_call(
        _kv_proj_body,
        grid=(m // tm,),
        in_specs=[pl.BlockSpec((tm, k), lambda i: (i, 0)),
                  pl.BlockSpec((None, 2 * KV_W, k), lambda i: (layer, W_KV_ROW0 // (2 * KV_W), 0)),
                  pl.BlockSpec((None, LANES, k), lambda i: (layer, W_IK_ROW0 // LANES, 0))],
        out_specs=[pl.BlockSpec((tm, 2 * KV_W), lambda i: (i, 0)),
                   pl.BlockSpec((tm, LANES), lambda i: (i, 0))],
        out_shape=[jax.ShapeDtypeStruct((m, 2 * KV_W), F32), jax.ShapeDtypeStruct((m, LANES), F32)],
        scratch_shapes=[pltpu.VMEM((k, 2 * KV_W), BF16), pltpu.VMEM((k, LANES), BF16)],
        compiler_params=_cp(("arbitrary",)),
        name="kv_proj",
    )(x, w_t, w_t)


def _mm_ln_body(x_ref, w_ref, r_ref, g_ref, b_ref, of_ref, ob_ref, *, alpha):
    z = jnp.dot(x_ref[...], w_ref[...], preferred_element_type=F32) + alpha * r_ref[...]
    y = _ln_rows(z, g_ref[...], b_ref[...])
    of_ref[...] = y
    ob_ref[...] = y.astype(BF16)


def _mm_ln(x, w, layer, resid, g, b, alpha, tm):
    m, k = x.shape
    n = w.shape[2]
    row = lambda i: (i, 0)
    return pl.pallas_call(
        functools.partial(_mm_ln_body, alpha=alpha),
        grid=(m // tm,),
        in_specs=[pl.BlockSpec((tm, k), row),
                  pl.BlockSpec((None, k, n), lambda i: (layer, 0, 0), pipeline_mode=pl.Buffered(1)),
                  pl.BlockSpec((tm, n), row),
                  pl.BlockSpec((1, n), lambda i: (0, 0)),
                  pl.BlockSpec((1, n), lambda i: (0, 0))],
        out_specs=[pl.BlockSpec((tm, n), row), pl.BlockSpec((tm, n), row)],
        out_shape=[jax.ShapeDtypeStruct((m, n), F32), jax.ShapeDtypeStruct((m, n), BF16)],
        compiler_params=_cp(("arbitrary",)),
        name="proj_ln",
    )(x, w, resid, g, b)


def _swiglu_up_body(x_ref, w1_ref, w3_ref, o_ref):
    x = x_ref[...]
    a = jnp.dot(x, w1_ref[...], preferred_element_type=F32)
    c = jnp.dot(x, w3_ref[...], preferred_element_type=F32)
    o_ref[...] = (a * _sigmoid(a) * c).astype(o_ref.dtype)


def _swiglu_up(x, w1, w3, layer, tm, tf):
    m, k = x.shape
    f = w1.shape[2]
    return pl.pallas_call(
        _swiglu_up_body,
        grid=(f // tf, m // tm),
        in_specs=[pl.BlockSpec((tm, k), lambda j, i: (i, 0)),
                  pl.BlockSpec((None, k, tf), lambda j, i: (layer, 0, j)),
                  pl.BlockSpec((None, k, tf), lambda j, i: (layer, 0, j))],
        out_specs=pl.BlockSpec((tm, tf), lambda j, i: (i, j)),
        out_shape=jax.ShapeDtypeStruct((m, f), BF16),
        compiler_params=_cp(("arbitrary", "arbitrary")),
        name="ffn_up",
    )(x, w1, w3)


def _merge_body(ap_ref, cp_ref, sp_ref, as_ref, cs_ref, ss_ref, g_ref, wa_ref, wb_ref, wc_ref, o_ref, *, n_prompt):
    i = pl.program_id(0)
    d = o_ref.shape[1]

    def merge(a_ref, c_ref, s_ref):
        pa = jnp.dot(a_ref[...], wa_ref[...], preferred_element_type=F32)
        pb = jnp.dot(c_ref[...], wb_ref[...], preferred_element_type=F32)
        pc = jnp.dot(s_ref[...], wc_ref[...], preferred_element_type=F32)
        o = (_sigmoid(g_ref[:, :d].astype(F32)) * pa + _sigmoid(g_ref[:, d:2 * d].astype(F32)) * pb
             + _sigmoid(g_ref[:, 2 * d:].astype(F32)) * pc)
        o_ref[...] = o.astype(o_ref.dtype)

    @pl.when(i < n_prompt)
    def _():
        merge(ap_ref, cp_ref, sp_ref)

    @pl.when(i >= n_prompt)
    def _():
        merge(as_ref, cs_ref, ss_ref)


def _merge(prompt, sample, ha, w_pa, w_pb, w_pc, layer, tm):
    m = ha.shape[0]
    d = w_pa.shape[2]
    n_prompt = prompt[0].shape[0] // tm
    assert prompt[0].shape[0] % tm == 0 and sample[0].shape[0] == tm and m == (n_prompt + 1) * tm
    assert HA_GATE % (N_BRANCH * d) == 0
    p_row = lambda i: (jnp.minimum(i, n_prompt - 1), 0)
    fixed = lambda i: (0, 0)
    return pl.pallas_call(
        functools.partial(_merge_body, n_prompt=n_prompt),
        grid=(n_prompt + 1,),
        in_specs=[pl.BlockSpec((tm, t.shape[1]), p_row) for t in prompt]
        + [pl.BlockSpec((tm, t.shape[1]), fixed) for t in sample]
        + [pl.BlockSpec((tm, N_BRANCH * d), lambda i: (i, HA_GATE // (N_BRANCH * d)))]
        + [pl.BlockSpec((None,) + w.shape[1:], lambda i: (layer, 0, 0), pipeline_mode=pl.Buffered(1))
           for w in (w_pa, w_pb, w_pc)],
        out_specs=pl.BlockSpec((tm, d), lambda i: (i, 0)),
        out_shape=jax.ShapeDtypeStruct((m, d), BF16),
        compiler_params=_cp(("arbitrary",)),
        name="branch_merge",
    )(*prompt, *sample, ha, w_pa, w_pb, w_pc)


def _sortable_key(score):
    bits = pltpu.bitcast(score, I32)
    return bits ^ ((bits >> 31) & jnp.int32(0x7FFFFFFF))


def _kth_largest_key(count_ge, rows, k):
    def step(it, t):
        cand = t + jnp.left_shift(jnp.int32(1), 31 - it)
        return jnp.where(count_ge(cand) >= k, cand, t)

    return lax.fori_loop(0, 32, step, jnp.full((rows, 1), INT_MIN, I32))


RADIX4_STEPS = 16
RADIX4_FIELD_BITS = 5
RADIX4_WIDE = 4096.0


def _kth_largest_key_radix4(count3, rows, k, n_valid):
    kf = float(k)

    def cond(state):
        it, _, _, pending = state
        return jnp.logical_and(it < RADIX4_STEPS, pending > 0.0)

    def body(state):
        it, t, cnt, _ = state
        one = jnp.left_shift(jnp.int32(1), 30 - 2 * it)
        c1 = t + one
        c2 = c1 + one
        c3 = c2 + one
        n1, n2, n3 = count3(c1, c2, c3)
        t = jnp.where(n3 >= kf, c3, jnp.where(n2 >= kf, c2, jnp.where(n1 >= kf, c1, t)))
        cnt = jnp.where(n3 >= kf, n3, jnp.where(n2 >= kf, n2, jnp.where(n1 >= kf, n1, cnt)))
        settled = jnp.logical_or(cnt == kf, n_valid <= kf)
        return it + 1, t, cnt, jnp.max(jnp.where(settled, 0.0, 1.0))

    init = (jnp.int32(0), jnp.full((rows, 1), INT_MIN, I32), jnp.full((rows, 1), -1.0, F32), jnp.float32(1.0))
    return lax.while_loop(cond, body, init)[1]


def _lane_tile(x, width):
    return x if width == LANES else jnp.concatenate([x] * (width // LANES), axis=1)


def _attn_prompt_body(q_ref, iq_ref, iw_ref, kv_ref, ik_ref, bias_ref, o_ref,
                      keys_ref, mb_ref, iwb_ref, m_ref, l_ref, acc_ref, *, tq, k_sel):
    i = pl.program_id(1)
    nkb = i + 1
    tk = tq
    row = lax.broadcasted_iota(I32, (tq, tk), 0)
    col = lax.broadcasted_iota(I32, (tq, tk), 1)
    low_half = lax.broadcasted_iota(I32, (tk, LANES), 1) < IDX_DIM
    iw = iw_ref[...] * (IDX_DIM ** -0.5 * IDX_HEADS ** -0.5)
    for h in range(IDX_HEADS):
        iwb_ref[h] = jnp.broadcast_to(iw[:, IKW_IW + h:IKW_IW + h + 1], (tq, LANES))
    iq_pairs = [iq_ref[:, p * LANES:(p + 1) * LANES] for p in range(IDX_HEADS // 2)]

    def idx_block(kb, carry):
        ik_lo = jnp.where(low_half, ik_ref[pl.ds(pl.multiple_of(kb * tk, tk), tk), :], 0.0)
        ik_sides = (ik_lo.astype(BF16), pltpu.roll(ik_lo, IDX_DIM, axis=1).astype(BF16))
        acc = jnp.zeros((tq, tk), F32)
        for h in range(IDX_HEADS):
            s = lax.dot_general(iq_pairs[h // 2], ik_sides[h % 2], (((1,), (1,)), ((), ())),
                                preferred_element_type=F32)
            acc = acc + jnp.maximum(s, 0.0) * _lane_tile(iwb_ref[h], tk)
        causal = (col + kb * tk) <= (row + i * tq)
        keys_ref[kb] = jnp.where(causal, _sortable_key(acc), jnp.int32(INT_MIN))
        return carry

    lax.fori_loop(0, nkb, idx_block, 0)

    fb = RADIX4_FIELD_BITS
    field_mask = (1 << fb) - 1
    inc1, inc2, inc3 = 1, 1 + (1 << fb), 1 + (1 << fb) + (1 << 2 * fb)

    def count3(c1, c2, c3):
        b1, b2, b3 = (jnp.broadcast_to(c, (tq, LANES)) for c in (c1, c2, c3))

        def cb(kb, acc):
            keys = keys_ref[kb]
            for t in range(tk // LANES):
                kk = keys[:, t * LANES:(t + 1) * LANES]
                acc = acc + jnp.where(kk >= b3, inc3, jnp.where(kk >= b2, inc2, jnp.where(kk >= b1, inc1, 0)))
            return acc

        acc = lax.fori_loop(0, nkb, cb, jnp.zeros((tq, LANES), I32))
        n3 = (acc >> (2 * fb)).astype(F32).sum(axis=-1, keepdims=True)
        low = ((acc & field_mask).astype(F32) + ((acc >> fb) & field_mask).astype(F32) * RADIX4_WIDE)
        low = low.sum(axis=-1, keepdims=True)
        n2 = jnp.floor(low * (1.0 / RADIX4_WIDE))
        return low - RADIX4_WIDE * n2, n2, n3

    n_valid = (lax.broadcasted_iota(I32, (tq, 1), 0) + (i * tq + 1)).astype(F32)
    thr = _kth_largest_key_radix4(count3, tq, k_sel, n_valid)
    thr = jnp.maximum(thr, jnp.int32(INT_MIN + 1))
    thr_b = _lane_tile(jnp.broadcast_to(thr, (tq, LANES)), tk)

    def mask_block(kb, carry):
        mb_ref[kb] = jnp.where(keys_ref[kb] >= thr_b, 0.0, MASK_NEG)
        return carry

    lax.fori_loop(0, nkb, mask_block, 0)

    scale = HEAD_DIM ** -0.5
    for g in range(N_KV_HEADS):
        qg = jnp.concatenate(
            [q_ref[:, (KV_REP * g + r) * HEAD_DIM:(KV_REP * g + r + 1) * HEAD_DIM] for r in range(KV_REP)], axis=0)
        m_ref[...] = jnp.full(m_ref.shape, MASK_NEG, F32)
        l_ref[...] = jnp.zeros(l_ref.shape, F32)
        acc_ref[...] = jnp.zeros(acc_ref.shape, F32)

        def block(kb, bias_idx, g=g, qg=qg):
            r0 = pl.multiple_of(kb * tk, tk)
            kblk = kv_ref[pl.ds(r0, tk), KV_K + g * HEAD_DIM:KV_K + (g + 1) * HEAD_DIM].astype(BF16)
            vblk = kv_ref[pl.ds(r0, tk), KV_V + g * HEAD_DIM:KV_V + (g + 1) * HEAD_DIM].astype(BF16)
            s = lax.dot_general(qg, kblk, (((1,), (1,)), ((), ())), preferred_element_type=F32) * scale
            s = s.reshape(KV_REP, tq, tk) + mb_ref[kb][None]
            if bias_idx is not None:
                s = s + bias_ref[bias_idx, KV_REP * g:KV_REP * (g + 1)]
            s = s.reshape(KV_REP * tq, tk)
            m_old = m_ref[...]
            m_new = jnp.maximum(m_old, s.max(axis=-1, keepdims=True))
            p = jnp.exp(s - _lane_tile(m_new, tk))
            alpha = jnp.exp(m_old - m_new)
            l_ref[...] = alpha * l_ref[...] + p.sum(axis=-1, keepdims=True)
            acc_ref[...] = alpha * acc_ref[...] + jnp.dot(p.astype(BF16), vblk, preferred_element_type=F32)
            m_ref[...] = m_new

        n_far = jnp.maximum(i - 1, 0)

        def far_pair(t, carry):
            block(2 * t, None)
            block(2 * t + 1, None)
            return carry

        lax.fori_loop(0, n_far // 2, far_pair, 0)

        @pl.when(n_far % 2 == 1)
        def _():
            block(n_far - 1, None)

        @pl.when(i >= 1)
        def _():
            block(i - 1, 1)
            block(i, 0)

        @pl.when(i == 0)
        def _():
            block(i, 0)
        out = acc_ref[...] / l_ref[...]
        for r in range(KV_REP):
            h = KV_REP * g + r
            o_ref[:, h * HEAD_DIM:(h + 1) * HEAD_DIM] = out[r * tq:(r + 1) * tq].astype(o_ref.dtype)


def _attn_prompt(ha, kv, ikw, bias_tiles, batch, seq, tq):
    nq = seq // tq
    k_sel = min(TOPK_MAX, seq // 4)
    assert seq // LANES < (1 << RADIX4_FIELD_BITS)
    body = functools.partial(_attn_prompt_body, tq=tq, k_sel=k_sel)
    return pl.pallas_call(
        body,
        grid=(batch, nq),
        in_specs=[pl.BlockSpec((tq, QKV_W), lambda b, i: (b * nq + i, HA_Q // QKV_W)),
                  pl.BlockSpec((tq, IQ_W), lambda b, i: (b * nq + i, HA_IQ // IQ_W)),
                  pl.BlockSpec((tq, LANES), lambda b, i: (b * nq + i, 0)),
                  pl.BlockSpec((seq, 2 * KV_W), lambda b, i: (b, 0), pipeline_mode=pl.Buffered(1)),
                  pl.BlockSpec((seq, LANES), lambda b, i: (b, 0), pipeline_mode=pl.Buffered(1)),
                  pl.BlockSpec((2, N_HEADS, tq, tq), lambda b, i: (0, 0, 0, 0), pipeline_mode=pl.Buffered(1))],
        out_specs=pl.BlockSpec((tq, QKV_W), lambda b, i: (b * nq + i, 0)),
        out_shape=jax.ShapeDtypeStruct((batch * seq, QKV_W), BF16),
        scratch_shapes=[pltpu.VMEM((nq, tq, tq), I32),
                        pltpu.VMEM((nq, tq, tq), F32),
                        pltpu.VMEM((IDX_HEADS, tq, LANES), F32),
                        pltpu.VMEM((KV_REP * tq, LANES), F32),
                        pltpu.VMEM((KV_REP * tq, LANES), F32),
                        pltpu.VMEM((KV_REP * tq, HEAD_DIM), F32)],
        compiler_params=_cp(("arbitrary", "arbitrary")),
        name="attn_prompt",
    )(ha, ha, ikw, kv, ikw, bias_tiles)


def _page_spec(block, layer, group, slot):
    return pl.BlockSpec((None, None) + block, lambda b, p, pt: (layer, pt[b, p * group + slot], 0, 0))


def _idx_sample_body(pt_ref, iq_ref, iw_ref, *refs, n_tok, group):
    pages, o_ref = refs[:group], refs[group]
    ikt = jnp.concatenate([r[...].astype(BF16) for r in pages], axis=1)
    s = jnp.dot(iq_ref[0], ikt, preferred_element_type=F32)
    r = jnp.maximum(s, 0.0) * _lane_tile(iw_ref[0], group * PAGE_SIZE)
    tok_scores = r.reshape(n_tok, IDX_HEADS, group * PAGE_SIZE).sum(axis=1)
    for t in range(group):
        o_ref[0, t] = tok_scores[:, t * PAGE_SIZE:(t + 1) * PAGE_SIZE]


def _idx_sample(page_table, iq_rows, iw_rows, cache_ik_t, layer, n_tok, group):
    db, n_pages = page_table.shape
    rows = n_tok * IDX_HEADS
    body = functools.partial(_idx_sample_body, n_tok=n_tok, group=group)
    grid_spec = pltpu.PrefetchScalarGridSpec(
        num_scalar_prefetch=1,
        grid=(db, n_pages // group),
        in_specs=[pl.BlockSpec((1, rows, IDX_DIM), lambda b, p, pt: (b, 0, 0)),
                  pl.BlockSpec((1, rows, LANES), lambda b, p, pt: (b, 0, 0))]
        + [_page_spec((IDX_DIM, PAGE_SIZE), layer, group, t) for t in range(group)],
        out_specs=pl.BlockSpec((1, group, n_tok, PAGE_SIZE), lambda b, p, pt: (b, p, 0, 0)),
    )
    return pl.pallas_call(
        body,
        grid_spec=grid_spec,
        out_shape=jax.ShapeDtypeStruct((db, n_pages, n_tok, PAGE_SIZE), F32),
        compiler_params=_cp(("arbitrary", "arbitrary")),
        name="idx_sample",
    )(page_table, iq_rows, iw_rows, *([cache_ik_t] * group))


def _attn_sample_body(pt_ref, sc_ref, scn_ref, q_ref, kn_ref, vn_ref, bl_ref, bn_ref, *refs,
                      n_tok, n_pages, k_sel, group):
    kp_refs, vp_refs = refs[:group], refs[group:2 * group]
    o_ref, thr_ref, m_ref, l_ref, acc_ref = refs[2 * group:]
    p = pl.program_id(1)
    n_steps = n_pages // group
    grp = KV_REP * n_tok
    width = group * PAGE_SIZE
    scale = HEAD_DIM ** -0.5
    r_i = lax.broadcasted_iota(I32, (n_tok, LANES), 0)
    c_i = lax.broadcasted_iota(I32, (n_tok, LANES), 1)
    keys_new = jnp.where(c_i <= r_i, _sortable_key(scn_ref[0]), jnp.int32(INT_MIN))

    @pl.when(p == 0)
    def _():
        keys = _sortable_key(sc_ref[0])

        def count_ge(cand):
            cand_b = jnp.broadcast_to(cand, (n_tok, LANES))
            c = jnp.where(keys >= cand_b[None], 1.0, 0.0).sum(axis=0) + jnp.where(keys_new >= cand_b, 1.0, 0.0)
            return c.sum(axis=-1, keepdims=True)

        thr = _kth_largest_key(count_ge, n_tok, float(k_sel))
        thr_ref[...] = jnp.broadcast_to(jnp.maximum(thr, jnp.int32(INT_MIN + 1)), (n_tok, LANES))
        m_ref[...] = jnp.full(m_ref.shape, MASK_NEG, F32)
        l_ref[...] = jnp.zeros(l_ref.shape, F32)
        acc_ref[...] = jnp.zeros(acc_ref.shape, F32)

    thr_b = thr_ref[...]

    def update(g, s, vblk):
        sl = slice(g * grp, (g + 1) * grp)
        m_old = m_ref[sl, :]
        m_new = jnp.maximum(m_old, s.max(axis=-1, keepdims=True))
        pr = jnp.exp(s - _lane_tile(m_new, s.shape[1]))
        alpha = jnp.exp(m_old - m_new)
        l_ref[sl, :] = alpha * l_ref[sl, :] + pr.sum(axis=-1, keepdims=True)
        acc_ref[sl, :] = alpha * acc_ref[sl, :] + jnp.dot(pr.astype(BF16), vblk, preferred_element_type=F32)
        m_ref[sl, :] = m_new

    def group_rows(mask_tok):
        return jnp.concatenate([mask_tok] * KV_REP, axis=0)

    base = p * group
    mb = jnp.concatenate(
        [jnp.where(_sortable_key(sc_ref[0, base + t]) >= thr_b, 0.0, MASK_NEG) for t in range(group)], axis=1)
    mb_g = group_rows(mb)
    is_last = p == n_steps - 1
    zeros_head = jnp.zeros((grp, width - PAGE_SIZE), F32)
    for g in range(N_KV_HEADS):
        qg = q_ref[0, g * grp:(g + 1) * grp, :]
        kg = jnp.concatenate([r[pl.ds(g, PAGE_SIZE, stride=N_KV_HEADS), :].astype(BF16) for r in kp_refs], axis=0)
        vg = jnp.concatenate([r[pl.ds(g, PAGE_SIZE, stride=N_KV_HEADS), :].astype(BF16) for r in vp_refs], axis=0)
        s = lax.dot_general(qg, kg, (((1,), (1,)), ((), ())), preferred_element_type=F32) * scale
        near = jnp.where(is_last, bl_ref[g * grp:(g + 1) * grp, :], 0.0)
        s = s + mb_g + (near if group == 1 else jnp.concatenate([zeros_head, near], axis=1))
        update(g, s, vg)

    @pl.when(is_last)
    def _():
        mbn_g = group_rows(jnp.where(keys_new >= thr_b, 0.0, MASK_NEG))
        for g in range(N_KV_HEADS):
            qg = q_ref[0, g * grp:(g + 1) * grp, :]
            kblk = kn_ref[0, :, g * HEAD_DIM:(g + 1) * HEAD_DIM].astype(BF16)
            vblk = vn_ref[0, :, g * HEAD_DIM:(g + 1) * HEAD_DIM].astype(BF16)
            s = lax.dot_general(qg, kblk, (((1,), (1,)), ((), ())), preferred_element_type=F32) * scale
            s = s + mbn_g + bn_ref[g * grp:(g + 1) * grp, :]
            update(g, s, vblk)
        o_ref[0] = acc_ref[...] / l_ref[...]


def _attn_sample(page_table, scores, scores_new, q_rows, k_new, v_new, bias_last, bias_new,
                 cache_k, cache_v, layer, n_tok, group):
    db, n_pages = page_table.shape
    past = n_pages * PAGE_SIZE
    rows = N_HEADS * n_tok
    k_sel = min(TOPK_MAX, (past + n_tok) // 4)
    body = functools.partial(_attn_sample_body, n_tok=n_tok, n_pages=n_pages, k_sel=k_sel, group=group)
    page = (PAGE_SIZE * N_KV_HEADS, HEAD_DIM)
    grid_spec = pltpu.PrefetchScalarGridSpec(
        num_scalar_prefetch=1,
        grid=(db, n_pages // group),
        in_specs=[pl.BlockSpec((1, n_pages, n_tok, PAGE_SIZE), lambda b, p, pt: (b, 0, 0, 0)),
                  pl.BlockSpec((1, n_tok, LANES), lambda b, p, pt: (b, 0, 0)),
                  pl.BlockSpec((1, rows, HEAD_DIM), lambda b, p, pt: (b, 0, 0)),
                  pl.BlockSpec((1, LANES, KV_W), lambda b, p, pt: (b, 0, 0)),
                  pl.BlockSpec((1, LANES, KV_W), lambda b, p, pt: (b, 0, 0)),
                  pl.BlockSpec((rows, LANES), lambda b, p, pt: (0, 0)),
                  pl.BlockSpec((rows, LANES), lambda b, p, pt: (0, 0))]
        + [_page_spec(page, layer, group, t) for t in range(group)]
        + [_page_spec(page, layer, group, t) for t in range(group)],
        out_specs=pl.BlockSpec((1, rows, HEAD_DIM), lambda b, p, pt: (b, 0, 0)),
        scratch_shapes=[pltpu.VMEM((n_tok, LANES), I32),
                        pltpu.VMEM((rows, LANES), F32),
                        pltpu.VMEM((rows, LANES), F32),
                        pltpu.VMEM((rows, HEAD_DIM), F32)],
    )
    return pl.pallas_call(
        body,
        grid_spec=grid_spec,
        out_shape=jax.ShapeDtypeStruct((db, rows, HEAD_DIM), F32),
        compiler_params=_cp(("arbitrary", "arbitrary")),
        name="attn_sample",
    )(page_table, scores, scores_new, q_rows, k_new, v_new, bias_last, bias_new,
      *([cache_k] * group), *([cache_v] * group))


def _idx_new_body(iq_ref, iw_ref, ik_ref, o_ref, *, n_tok):
    for b in range(iq_ref.shape[0]):
        s = lax.dot_general(iq_ref[b], ik_ref[b].astype(BF16), (((1,), (1,)), ((), ())),
                            preferred_element_type=F32)
        r = jnp.maximum(s, 0.0) * iw_ref[b]
        o_ref[b] = r.reshape(n_tok, IDX_HEADS, LANES).sum(axis=1)


def _idx_new(iq_rows, iw_rows, ik_new, n_tok):
    db = iq_rows.shape[0]
    return pl.pallas_call(
        functools.partial(_idx_new_body, n_tok=n_tok),
        out_shape=jax.ShapeDtypeStruct((db, n_tok, LANES), F32),
        name="idx_new",
    )(iq_rows, iw_rows, ik_new)


HALO = CONV_WIDTH - 1
HALO_PAD = 32


def _conv_prompt_body(glu_ref, cw_ref, cb_ref, g_ref, b_ref, o_ref, st_ref, hp_ref, c_ref, *, ts, ns):
    s_idx = pl.program_id(1)

    @pl.when(s_idx == 0)
    def _():
        hp_ref[0:HALO_PAD, :] = jnp.zeros((HALO_PAD, CONV_DIM), F32)

    a = glu_ref[:, :CONV_DIM].astype(F32)
    gt = glu_ref[:, CONV_DIM:].astype(F32)
    hp_ref[HALO_PAD:HALO_PAD + ts, :] = a * _sigmoid(gt)
    off = HALO_PAD - HALO
    for c in range(CONV_DIM // LANES):
        cs = slice(c * LANES, (c + 1) * LANES)
        acc = jnp.zeros((ts, LANES), F32)
        for w in range(CONV_WIDTH):
            acc = acc + hp_ref[off + w:off + w + ts, cs] * cw_ref[w:w + 1, cs]
        c_ref[:, cs] = acc + cb_ref[:, cs]
    y = _ln_rows(c_ref[...], g_ref[...], b_ref[...])
    o_ref[...] = (y * _sigmoid(y)).astype(o_ref.dtype)
    tail = hp_ref[ts + off:ts + HALO_PAD, :]

    @pl.when(s_idx == ns - 1)
    def _():
        st_ref[0] = tail

    hp_ref[off:HALO_PAD, :] = tail


def _conv_prompt(ha, conv_w, conv_b, ln_g, ln_b, batch, seq, ts):
    ns = seq // ts
    body = functools.partial(_conv_prompt_body, ts=ts, ns=ns)
    vec = pl.BlockSpec((1, CONV_DIM), lambda b, s: (0, 0))
    return pl.pallas_call(
        body,
        grid=(batch, ns),
        in_specs=[pl.BlockSpec((ts, 2 * CONV_DIM), lambda b, s: (b * ns + s, HA_GLU // (2 * CONV_DIM))),
                  pl.BlockSpec((CONV_WIDTH, CONV_DIM), lambda b, s: (0, 0)), vec, vec, vec],
        out_specs=[pl.BlockSpec((ts, CONV_DIM), lambda b, s: (b * ns + s, 0)),
                   pl.BlockSpec((1, HALO, CONV_DIM), lambda b, s: (b, 0, 0))],
        out_shape=[jax.ShapeDtypeStruct((batch * seq, CONV_DIM), BF16),
                   jax.ShapeDtypeStruct((batch, HALO, CONV_DIM), F32)],
        scratch_shapes=[pltpu.VMEM((HALO_PAD + ts, CONV_DIM), F32), pltpu.VMEM((ts, CONV_DIM), F32)],
        compiler_params=_cp(("arbitrary", "arbitrary")),
        name="conv_prompt",
    )(ha, conv_w, conv_b, ln_g, ln_b)


def _sgu_prompt_body(uv_ref, g_ref, b_ref, w_ref, bs_ref, o_ref, *, ts):
    u = uv_ref[:, :SGU_DIM].astype(F32)
    vn = _ln_rows(uv_ref[:, SGU_DIM:].astype(F32), g_ref[...], b_ref[...]).astype(BF16)
    r_i = lax.broadcasted_iota(I32, (CHUNK, CHUNK), 0)
    c_i = lax.broadcasted_iota(I32, (CHUNK, CHUNK), 1)
    gd = SGU_DIM // SGU_GROUPS
    for g in range(SGU_GROUPS):
        wm = jnp.where(c_i <= r_i, w_ref[g], 0.0).astype(BF16)
        bias = bs_ref[:, g:g + 1]
        for c in range(ts // CHUNK):
            rs = slice(c * CHUNK, (c + 1) * CHUNK)
            gs = slice(g * gd, (g + 1) * gd)
            mixed = jnp.dot(wm, vn[rs, gs], preferred_element_type=F32) + bias
            o_ref[rs, gs] = (u[rs, gs] * mixed).astype(o_ref.dtype)


def _sgu_prompt(ha, ln_g, ln_b, w_s, b_s_t, batch, seq, ts):
    ns = seq // ts
    vec = pl.BlockSpec((1, SGU_DIM), lambda b, s: (0, 0))
    return pl.pallas_call(
        functools.partial(_sgu_prompt_body, ts=ts),
        grid=(batch, ns),
        in_specs=[pl.BlockSpec((ts, 2 * SGU_DIM), lambda b, s: (b * ns + s, HA_UV // (2 * SGU_DIM))),
                  vec, vec,
                  pl.BlockSpec((SGU_GROUPS, CHUNK, CHUNK), lambda b, s: (0, 0, 0)),
                  pl.BlockSpec((CHUNK, SGU_GROUPS), lambda b, s: (0, 0))],
        out_specs=pl.BlockSpec((ts, SGU_DIM), lambda b, s: (b * ns + s, 0)),
        out_shape=jax.ShapeDtypeStruct((batch * seq, SGU_DIM), BF16),
        compiler_params=_cp(("arbitrary", "arbitrary")),
        name="sgu_prompt",
    )(ha, ln_g, ln_b, w_s, b_s_t)


def _mix_sample_body(glu_ref, uv_ref, st_ref, cw_ref, cb_ref, cg_ref, cbt_ref, sg_ref, sb_ref, wv_ref, bv_ref,
                     co_ref, nst_ref, so_ref, vn_ref, hp_ref, *, db, n_tok):
    cw = cw_ref[...]
    for b in range(db):
        rs = slice(b * n_tok, (b + 1) * n_tok)
        a = glu_ref[rs, :CONV_DIM].astype(F32)
        gt = glu_ref[rs, CONV_DIM:].astype(F32)
        hp_ref[0:HALO, :] = st_ref[b]
        hp_ref[HALO:HALO + n_tok, :] = a * _sigmoid(gt)
        rows = [jnp.sum(hp_ref[t:t + CONV_WIDTH, :] * cw, axis=0, keepdims=True) for t in range(n_tok)]
        c = jnp.concatenate(rows, axis=0) + cb_ref[...]
        y = _ln_rows(c, cg_ref[...], cbt_ref[...])
        co_ref[rs, :] = y * _sigmoid(y)
        nst_ref[b] = hp_ref[n_tok:n_tok + HALO, :]
        u = uv_ref[rs, :SGU_DIM].astype(F32)
        vn = _ln_rows(uv_ref[rs, SGU_DIM:].astype(F32), sg_ref[...], sb_ref[...])
        vn_ref[rs, :] = vn
        mixed = []
        for t in range(n_tok):
            acc = bv_ref[t:t + 1, :]
            for s in range(t + 1):
                acc = acc + wv_ref[t, s:s + 1, :] * vn[s:s + 1, :]
            mixed.append(acc)
        so_ref[rs, :] = u * jnp.concatenate(mixed, axis=0)


def _mix_sample(glu, uv, state, conv_w, conv_b, cln_g, cln_b, sln_g, sln_b, wv, bv, db, n_tok):
    rows = db * n_tok
    return pl.pallas_call(
        functools.partial(_mix_sample_body, db=db, n_tok=n_tok),
        out_shape=[jax.ShapeDtypeStruct((rows, CONV_DIM), F32),
                   jax.ShapeDtypeStruct((db, HALO, CONV_DIM), F32),
                   jax.ShapeDtypeStruct((rows, SGU_DIM), F32),
                   jax.ShapeDtypeStruct((rows, SGU_DIM), F32)],
        scratch_shapes=[pltpu.VMEM((_round_up(HALO + n_tok, 8), CONV_DIM), F32)],
        name="mix_sample",
    )(glu, uv, state, conv_w, conv_b, cln_g, cln_b, sln_g, sln_b, wv, bv)


def _router_body(x_ref, w_ref, b_ref, ti_ref, tg_ref, *, n_exp):
    logits = jnp.dot(x_ref[...], w_ref[...], preferred_element_type=F32,
                     precision=lax.Precision.HIGHEST) + b_ref[...]
    lane = lax.broadcasted_iota(I32, logits.shape, 1)
    lane_f = lane.astype(F32)
    neg = -jnp.inf
    lg = jnp.where(lane < n_exp, logits, neg)
    v1 = lg.max(axis=-1, keepdims=True)
    i1 = jnp.where(lg == v1, lane_f, float(LANES)).min(axis=-1, keepdims=True)
    lg2 = jnp.where(lane_f == i1, neg, lg)
    v2 = lg2.max(axis=-1, keepdims=True)
    i2 = jnp.where(lg2 == v2, lane_f, float(LANES)).min(axis=-1, keepdims=True)
    e = jnp.exp(v2 - v1)
    den = 1.0 + e
    ti_ref[...] = jnp.where(lane == 0, i1, jnp.where(lane == 1, i2, 0.0)).astype(I32)
    tg_ref[...] = jnp.where(lane == 0, 1.0 / den, jnp.where(lane == 1, e / den, 0.0))


def _router(x, w_pad, b_pad, n_exp, tm):
    m, d = x.shape
    return pl.pallas_call(
        functools.partial(_router_body, n_exp=n_exp),
        grid=(m // tm,),
        in_specs=[pl.BlockSpec((tm, d), lambda i: (i, 0)),
                  pl.BlockSpec((d, LANES), lambda i: (0, 0)),
                  pl.BlockSpec((1, LANES), lambda i: (0, 0))],
        out_specs=[pl.BlockSpec((tm, LANES), lambda i: (i, 0)),
                   pl.BlockSpec((tm, LANES), lambda i: (i, 0))],
        out_shape=[jax.ShapeDtypeStruct((m, LANES), I32), jax.ShapeDtypeStruct((m, LANES), F32)],
        compiler_params=_cp(("arbitrary",)),
        name="moe_router",
    )(x, w_pad, b_pad)


def _row_copy(src_hbm, dst, src_row, dst_row, sem):
    return pltpu.make_async_copy(src_hbm.at[pl.ds(src_row, 1)], dst.at[pl.ds(dst_row, 1)], sem)


ROW_DMA_UNROLL = 8


def _prefetch_chunks(issue, wait):
    c = pl.program_id(0)

    @pl.when(c == 0)
    def _():
        issue(c, 0)

    @pl.when(c + 1 < pl.num_programs(0))
    def _():
        issue(c + 1, (c + 1) % 2)

    slot = c % 2
    wait(slot)
    return slot


def _dispatch_body(src_ref, x_hbm, o_ref, buf, sem, *, rows):
    def issue(chunk, slot):
        def start(u, carry):
            for v in range(ROW_DMA_UNROLL):
                r = u * ROW_DMA_UNROLL + v
                _row_copy(x_hbm, buf.at[slot], src_ref[chunk * rows + r], r, sem.at[slot]).start()
            return carry

        lax.fori_loop(0, rows // ROW_DMA_UNROLL, start, 0)

    def wait(slot):
        def w(r, carry):
            _row_copy(x_hbm, buf.at[slot], 0, r, sem.at[slot]).wait()
            return carry

        lax.fori_loop(0, rows, w, 0)

    slot = _prefetch_chunks(issue, wait)
    o_ref[...] = buf[slot].astype(o_ref.dtype)


def _dispatch(src, x, n_slots, rows):
    d = x.shape[1]
    grid_spec = pltpu.PrefetchScalarGridSpec(
        num_scalar_prefetch=1,
        grid=(n_slots // rows,),
        in_specs=[pl.BlockSpec(memory_space=pl.ANY)],
        out_specs=pl.BlockSpec((rows, d), lambda i, s: (i, 0)),
        scratch_shapes=[pltpu.VMEM((2, rows, d), x.dtype), pltpu.SemaphoreType.DMA((2,))],
    )
    assert rows % ROW_DMA_UNROLL == 0
    return pl.pallas_call(
        functools.partial(_dispatch_body, rows=rows),
        grid_spec=grid_spec,
        out_shape=jax.ShapeDtypeStruct((n_slots, d), BF16),
        compiler_params=_cp(("arbitrary",)),
        name="moe_dispatch",
    )(src, x)


def _tile_copy(src, dst, sem):
    return pltpu.make_async_copy(src, dst, sem)


def _moe_ffn_body(exp_ref, row0_ref, nsub_ref, jmap_ref, nzero_ref, xs_hbm, w1_ref, w3_ref, w2_ref, ys_hbm,
                  xbuf, acc, w1b, w3b, w2b, sem, *, nj, sub):
    w = pl.program_id(0)
    j = pl.program_id(1)
    nsub = nsub_ref[w]
    row0 = row0_ref[w]
    nzero = nzero_ref[w]

    def zero_copy(t):
        r = pl.multiple_of(row0 + t * sub, sub)
        return _tile_copy(acc.at[pl.ds(0, sub)], ys_hbm.at[pl.ds(r, sub)], sem)

    @pl.when(jnp.logical_and(j == 0, nzero > 0))
    def _():
        acc[0:sub, :] = jnp.zeros((sub, acc.shape[1]), F32)

        def zs(t, carry):
            zero_copy(t).start()
            return carry

        def zw(t, carry):
            zero_copy(t).wait()
            return carry

        lax.fori_loop(0, nzero, zs, 0)
        lax.fori_loop(0, nzero, zw, 0)

    def in_copy(t):
        r = pl.multiple_of(t * sub, sub)
        return _tile_copy(xs_hbm.at[pl.ds(pl.multiple_of(row0 + r, sub), sub)], xbuf.at[pl.ds(r, sub)], sem)

    def out_copy(t):
        r = pl.multiple_of(t * sub, sub)
        return _tile_copy(acc.at[pl.ds(r, sub)], ys_hbm.at[pl.ds(pl.multiple_of(row0 + r, sub), sub)], sem)

    def for_tiles(fn):
        def it(t, carry):
            fn(t)
            return carry
        lax.fori_loop(0, nsub, it, 0)

    def zero_tile(t):
        acc[pl.ds(pl.multiple_of(t * sub, sub), sub), :] = jnp.zeros((sub, acc.shape[1]), F32)

    @pl.when(j == 0)
    def _():
        for_tiles(lambda t: in_copy(t).start())
        for_tiles(zero_tile)
        for_tiles(lambda t: in_copy(t).wait())

    def tile(t, w1v, w3v, w2v):
        r = pl.multiple_of(t * sub, sub)
        x = xbuf[pl.ds(r, sub), :]
        a = jnp.dot(x, w1v, preferred_element_type=F32)
        c = jnp.dot(x, w3v, preferred_element_type=F32)
        h = (a * _sigmoid(a) * c).astype(BF16)
        acc[pl.ds(r, sub), :] += jnp.dot(h, w2v, preferred_element_type=F32)

    def cast_weights():
        return w1_ref[...].astype(BF16), w3_ref[...].astype(BF16), w2_ref[...].astype(BF16)

    @pl.when(nsub >= 2)
    def _():
        wv = cast_weights()
        w1b[...], w3b[...], w2b[...] = wv
        tile(0, *wv)
        tile(1, *wv)

    @pl.when(nsub == 1)
    def _():
        tile(0, *cast_weights())

    def pair(t, carry):
        tile(2 * t, w1b[...], w3b[...], w2b[...])
        tile(2 * t + 1, w1b[...], w3b[...], w2b[...])
        return carry

    lax.fori_loop(1, nsub // 2, pair, 0)

    @pl.when(jnp.logical_and(nsub % 2 == 1, nsub >= 3))
    def _():
        tile(nsub - 1, w1b[...], w3b[...], w2b[...])

    @pl.when(j == nj - 1)
    def _():
        for_tiles(lambda t: out_copy(t).start())
        for_tiles(lambda t: out_copy(t).wait())


def _moe_ffn(item_exp, item_row0, item_nsub, item_jlast, item_nzero, xs, w1, w3, w2, layer, n_items, super_rows,
             sub, tf):
    n_slots, d = xs.shape
    f = w1.shape[3]
    nj = f // tf

    def jj(w, j, jl):
        return jnp.where(jl[w] > 0, j, nj - 1)

    grid_spec = pltpu.PrefetchScalarGridSpec(
        num_scalar_prefetch=5,
        grid=(n_items, nj),
        in_specs=[pl.BlockSpec(memory_space=pl.ANY),
                  pl.BlockSpec((None, None, d, tf), lambda w, j, ex, r0, ns, jl, nz: (layer, ex[w], 0, jj(w, j, jl))),
                  pl.BlockSpec((None, None, d, tf), lambda w, j, ex, r0, ns, jl, nz: (layer, ex[w], 0, jj(w, j, jl))),
                  pl.BlockSpec((None, None, tf, d), lambda w, j, ex, r0, ns, jl, nz: (layer, ex[w], jj(w, j, jl), 0))],
        out_specs=pl.BlockSpec(memory_space=pl.ANY),
        scratch_shapes=[pltpu.VMEM((super_rows, d), BF16),
                        pltpu.VMEM((super_rows, d), F32),
                        pltpu.VMEM((d, tf), BF16),
                        pltpu.VMEM((d, tf), BF16),
                        pltpu.VMEM((tf, d), BF16),
                        pltpu.SemaphoreType.DMA(())],
    )
    return pl.pallas_call(
        functools.partial(_moe_ffn_body, nj=nj, sub=sub),
        grid_spec=grid_spec,
        out_shape=jax.ShapeDtypeStruct((n_slots, d), F32),
        compiler_params=_cp(("arbitrary", "arbitrary")),
        name="moe_ffn",
    )(item_exp, item_row0, item_nsub, item_jlast, item_nzero, xs, w1, w3, w2)


def _combine_body(pos_ref, ys_hbm, tg_ref, x_ref, g_ref, b_ref, of_ref, ob_ref, buf, sem, *, tb, alpha):
    def issue(chunk, slot):
        def start(u, carry):
            for v in range(ROW_DMA_UNROLL // TOP_K_EXPERTS):
                r = u * (ROW_DMA_UNROLL // TOP_K_EXPERTS) + v
                for k in range(TOP_K_EXPERTS):
                    p = pos_ref[TOP_K_EXPERTS * (chunk * tb + r) + k]
                    _row_copy(ys_hbm, buf.at[slot, k], p, r, sem.at[slot]).start()
            return carry

        lax.fori_loop(0, tb * TOP_K_EXPERTS // ROW_DMA_UNROLL, start, 0)

    def wait(slot):
        def w(r, carry):
            for k in range(TOP_K_EXPERTS):
                _row_copy(ys_hbm, buf.at[slot, k], 0, r, sem.at[slot]).wait()
            return carry

        lax.fori_loop(0, tb, w, 0)

    slot = _prefetch_chunks(issue, wait)
    f = tg_ref[:, 0:1] * buf[slot, 0] + tg_ref[:, 1:2] * buf[slot, 1]
    y = _ln_rows(alpha * x_ref[...] + f, g_ref[...], b_ref[...])
    of_ref[...] = y
    ob_ref[...] = y.astype(BF16)


def _combine_ln(pos, ys, tg, x, g, b, alpha, tb):
    m, d = x.shape
    grid_spec = pltpu.PrefetchScalarGridSpec(
        num_scalar_prefetch=1,
        grid=(m // tb,),
        in_specs=[pl.BlockSpec(memory_space=pl.ANY),
                  pl.BlockSpec((tb, LANES), lambda i, p: (i, 0)),
                  pl.BlockSpec((tb, d), lambda i, p: (i, 0)),
                  pl.BlockSpec((1, d), lambda i, p: (0, 0)),
                  pl.BlockSpec((1, d), lambda i, p: (0, 0))],
        out_specs=[pl.BlockSpec((tb, d), lambda i, p: (i, 0)),
                   pl.BlockSpec((tb, d), lambda i, p: (i, 0))],
        scratch_shapes=[pltpu.VMEM((2, TOP_K_EXPERTS, tb, d), F32), pltpu.SemaphoreType.DMA((2,))],
    )
    assert tb * TOP_K_EXPERTS % ROW_DMA_UNROLL == 0
    return pl.pallas_call(
        functools.partial(_combine_body, tb=tb, alpha=alpha),
        grid_spec=grid_spec,
        out_shape=[jax.ShapeDtypeStruct((m, d), F32), jax.ShapeDtypeStruct((m, d), BF16)],
        compiler_params=_cp(("arbitrary",)),
        name="moe_combine_ln",
    )(pos, ys, tg, x, g, b)


def _moe_plan(top_i, n_real, n_exp, n_slots, n_items, super_rows, sub):
    a = top_i.shape[0] * TOP_K_EXPERTS
    e_flat = top_i.reshape(-1)
    tok = jnp.arange(a, dtype=I32) // TOP_K_EXPERTS
    real = tok < n_real
    onehot = jnp.logical_and(e_flat[:, None] == jnp.arange(n_exp, dtype=I32)[None, :], real[:, None]).astype(I32)
    csum = jnp.cumsum(onehot, axis=0)
    rank = jnp.take_along_axis(csum, e_flat[:, None], axis=1)[:, 0] - 1
    counts = csum[-1]
    padded = (counts + sub - 1) // sub * sub
    gstart = jnp.cumsum(padded) - padded
    pos = jnp.where(real, gstart[e_flat] + rank, 0).astype(I32)
    src = jnp.zeros((n_slots,), I32).at[jnp.where(real, pos, n_slots)].set(tok, mode="drop")
    n_super = (padded + super_rows - 1) // super_rows
    iend = jnp.cumsum(n_super)
    istart = iend - n_super
    total = iend[-1]
    wi = jnp.arange(n_items, dtype=I32)
    valid = wi < total
    e_w = jnp.minimum(jnp.searchsorted(iend, wi, side="right"), n_exp - 1).astype(I32)
    last_e = e_w[jnp.maximum(total - 1, 0)]
    e_w = jnp.where(valid, e_w, last_e)
    k_w = wi - istart[e_w]
    nsub = jnp.where(valid, jnp.minimum(super_rows, padded[e_w] - k_w * super_rows) // sub, 0).astype(I32)
    z0 = jnp.sum(padded) + (wi - total) * super_rows
    nzero = jnp.where(valid, 0, jnp.clip((n_slots - z0) // sub, 0, super_rows // sub)).astype(I32)
    row0 = jnp.where(valid, gstart[e_w] + k_w * super_rows, jnp.where(nzero > 0, z0, 0)).astype(I32)
    return pos, src, e_w, row0, nsub, valid.astype(I32), nzero


def _rel_bucket(dist):
    n = jnp.maximum(dist, 0)
    max_exact = REL_BUCKETS // 2
    nf = jnp.maximum(n, 1).astype(F32)
    large = max_exact + (jnp.log(nf / max_exact) / math.log(REL_MAX_DIST / max_exact)
                         * (REL_BUCKETS - max_exact)).astype(I32)
    large = jnp.minimum(large, REL_BUCKETS - 1)
    return jnp.where(n < max_exact, n, large)


def _shifted_bias(rel_bias, dist):
    t = rel_bias[_rel_bucket(dist)] - rel_bias[REL_BUCKETS - 1]
    t = jnp.where((dist >= 0)[..., None], t, 0.0)
    return jnp.moveaxis(t, -1, 0)


def _pick_tile(m_rows, target):
    return max(t for t in range(16, target + 1, 16) if m_rows % t == 0)


def _pad_rows(x, rows):
    return jnp.pad(x, ((0, rows - x.shape[0]), (0, 0)))


def kernel(x_prompt, x_sample, cache_k, cache_v, cache_ik, state_conv, page_table, w_in, conv_w, conv_b,
           conv_ln_g, conv_ln_b, sgu_ln_g, sgu_ln_b, sgu_w, sgu_b, w_pa, w_pb, w_pc, w_out, ln1_g, ln1_b,
           ln2_g, ln2_b, rel_bias, ffn_w1, ffn_w3, ffn_w2, moe_router, moe_router_b, moe_w1, moe_w3, moe_w2):
    batch, seq, d_model = x_prompt.shape
    db, n_tok, _ = x_sample.shape
    depth = w_in.shape[0]
    n_pool = cache_k.shape[1]
    n_exp = moe_router.shape[2]
    mp, ms = batch * seq, db * n_tok
    assert ms <= SAMPLE_TILE and mp % SAMPLE_TILE == 0
    m_all = mp + SAMPLE_TILE
    tm = m_all // N_ROW_TILES
    assert tm * N_ROW_TILES == m_all and tm % 16 == 0
    alpha = (2 * depth) ** 0.25
    tq = min(256, seq)

    xf = jnp.concatenate([x_prompt.reshape(mp, d_model), x_sample.reshape(ms, d_model),
                          jnp.zeros((m_all - mp - ms, d_model), F32)], axis=0)
    xb = xf.astype(BF16)
    w_in_t = jnp.swapaxes(w_in, 1, 2)
    ha_rows = ([W_Q_ROW0 + HA_TILE * t for t in range(QKV_W // HA_TILE)]
               + [W_GLU_ROW0 + HA_TILE * t for t in range(2 * CONV_DIM // HA_TILE)]
               + [W_UV_ROW0 + HA_TILE * t for t in range(2 * SGU_DIM // HA_TILE)]
               + [W_GATE_ROW0 + HA_TILE * t for t in range(N_BRANCH * d_model // HA_TILE)]
               + [W_IQ_ROW0 + HA_TILE * t for t in range(IQ_W // HA_TILE)])
    w_pa_b, w_pb_b, w_pc_b, w_out_b = (w.astype(BF16) for w in (w_pa, w_pb, w_pc, w_out))
    ffn_w1_b, ffn_w3_b, ffn_w2_b = (w.astype(BF16) for w in (ffn_w1, ffn_w3, ffn_w2))

    kk = jnp.arange(2 * tq, dtype=I32)
    d_wrap = jnp.where(kk < tq, -kk, 2 * tq - kk)

    def toeplitz(offset):
        v = _shifted_bias(rel_bias, d_wrap + offset)
        return jnp.tile(v, (1, tq))[:, :tq * (2 * tq - 1)].reshape(N_HEADS, tq, 2 * tq - 1)[:, :, :tq]

    bias_tiles = jnp.stack([toeplitz(0), toeplitz(tq)])
    tok = jnp.arange(n_tok, dtype=I32)
    lane = jnp.arange(LANES, dtype=I32)
    d_last = PAGE_SIZE + tok[:, None] - lane[None, :]
    bias_last = _shifted_bias(rel_bias, d_last).reshape(N_HEADS * n_tok, LANES)
    bias_new = _shifted_bias(rel_bias, tok[:, None] - lane[None, :]).reshape(N_HEADS * n_tok, LANES)

    ck = cache_k.reshape(depth, n_pool, PAGE_SIZE * N_KV_HEADS, HEAD_DIM)
    cv = cache_v.reshape(depth, n_pool, PAGE_SIZE * N_KV_HEADS, HEAD_DIM)
    cik_t = jnp.swapaxes(cache_ik, 2, 3)
    n_pages = page_table.shape[1]
    idx_group = math.gcd(n_pages, IDX_PAGE_GROUP)
    attn_group = math.gcd(n_pages, ATTN_PAGE_GROUP)
    n_slots = _round_up(TOP_K_EXPERTS * (mp + ms) + n_exp * (MOE_SUB - 1), MOE_SUB)
    n_items = -(-n_slots // MOE_SUPER) + n_exp

    outs = {k: [] for k in ("k", "v", "ik", "conv_p", "conv_s", "sgu_s")}
    for l in range(depth):
        ha = _in_proj(xb, w_in_t, l, ha_rows, tm, HA_TILE)
        kv, ikw = _kv_proj(xb, w_in_t, l, tm)

        a_p = _attn_prompt(ha, kv, ikw, bias_tiles, batch, seq, tq)
        c_p, conv_state_p = _conv_prompt(ha, conv_w[l], conv_b[l][None], conv_ln_g[l][None], conv_ln_b[l][None],
                                         batch, seq, min(256, seq))
        s_p = _sgu_prompt(ha, sgu_ln_g[l][None], sgu_ln_b[l][None], sgu_w[l], sgu_b[l].T, batch, seq,
                          min(512, seq))

        has, kvs, ikws = ha[mp:mp + ms], kv[mp:mp + ms], ikw[mp:mp + ms]
        iq_rows = has[:, HA_IQ:HA_IQ + IQ_W].reshape(db, n_tok * IDX_HEADS, IDX_DIM)
        iw_rows = (ikws[:, IKW_IW:IKW_IW + IDX_HEADS] * (IDX_DIM ** -0.5 * IDX_HEADS ** -0.5)
                   ).reshape(db, n_tok * IDX_HEADS, 1)
        iw_rows = jnp.broadcast_to(iw_rows, (db, n_tok * IDX_HEADS, LANES))
        scores = _idx_sample(page_table, iq_rows, iw_rows, cik_t, l, n_tok, idx_group)
        pad_rows = LANES - n_tok
        ik_new = jnp.pad(ikws[:, :IDX_DIM].reshape(db, n_tok, IDX_DIM), ((0, 0), (0, pad_rows), (0, 0)))
        scores_new = _idx_new(iq_rows, iw_rows, ik_new, n_tok)
        q_rows = has[:, HA_Q:HA_Q + QKV_W].reshape(db, n_tok, N_HEADS, HEAD_DIM).transpose(0, 2, 1, 3)
        q_rows = q_rows.reshape(db, N_HEADS * n_tok, HEAD_DIM)
        k_new = jnp.pad(kvs[:, KV_K:KV_K + KV_W].reshape(db, n_tok, KV_W), ((0, 0), (0, pad_rows), (0, 0)))
        v_new = jnp.pad(kvs[:, KV_V:KV_V + KV_W].reshape(db, n_tok, KV_W), ((0, 0), (0, pad_rows), (0, 0)))
        a_s = _attn_sample(page_table, scores, scores_new, q_rows, k_new, v_new, bias_last, bias_new,
                           ck, cv, l, n_tok, attn_group)
        a_s = a_s.reshape(db, N_HEADS, n_tok, HEAD_DIM).transpose(0, 2, 1, 3).reshape(ms, QKV_W)
        gd = SGU_DIM // SGU_GROUPS
        wv = jnp.repeat(sgu_w[l][:, :n_tok, :n_tok].transpose(1, 2, 0), gd, axis=-1)
        bv = jnp.repeat(sgu_b[l][:, :n_tok].T, gd, axis=-1)
        c_s, conv_state_s, s_s, vn_s = _mix_sample(
            has[:, HA_GLU:HA_GLU + 2 * CONV_DIM], has[:, HA_UV:HA_UV + 2 * SGU_DIM], state_conv[l],
            conv_w[l], conv_b[l][None], conv_ln_g[l][None], conv_ln_b[l][None],
            sgu_ln_g[l][None], sgu_ln_b[l][None], wv, bv, db, n_tok)

        sample_rows = tuple(_pad_rows(t.astype(BF16), SAMPLE_TILE) for t in (a_s, c_s, s_s))
        merged = _merge((a_p, c_p, s_p), sample_rows, ha, w_pa_b, w_pb_b, w_pc_b, l, SAMPLE_TILE)
        x1f, x1b = _mm_ln(merged, w_out_b, l, xf, ln1_g[l][None], ln1_b[l][None], alpha, _pick_tile(m_all, 528))

        j = l // 2
        if l % 2 == 0:
            h = _swiglu_up(x1b, ffn_w1_b, ffn_w3_b, j, tm, 512)
            xf, xb = _mm_ln(h, ffn_w2_b, j, x1f, ln2_g[l][None], ln2_b[l][None], alpha, _pick_tile(m_all, 384))
        else:
            w_r = jnp.pad(moe_router[j], ((0, 0), (0, LANES - n_exp)))
            b_r = jnp.pad(moe_router_b[j], (0, LANES - n_exp))[None]
            ti, tg = _router(x1f, w_r, b_r, n_exp, _pick_tile(m_all, 528))
            pos, src, it_e, it_r0, it_ns, it_valid, it_nz = _moe_plan(ti[:, :TOP_K_EXPERTS], mp + ms, n_exp,
                                                                      n_slots, n_items, MOE_SUPER, MOE_SUB)
            xs = _dispatch(src, x1f, n_slots, MOE_SUB)
            ys = _moe_ffn(it_e, it_r0, it_ns, it_valid, it_nz, xs, moe_w1, moe_w3, moe_w2, j, n_items,
                          MOE_SUPER, MOE_SUB, MOE_TF)
            tb = COMBINE_TB if m_all % COMBINE_TB == 0 else 16
            xf, xb = _combine_ln(pos, ys, tg, x1f, ln2_g[l][None], ln2_b[l][None], alpha, tb)

        outs["k"].append(kv[:, KV_K:KV_K + KV_W])
        outs["v"].append(kv[:, KV_V:KV_V + KV_W])
        outs["ik"].append(ikw[:, :IDX_DIM])
        outs["conv_p"].append(conv_state_p)
        outs["conv_s"].append(conv_state_s)
        outs["sgu_s"].append(vn_s.reshape(db, n_tok, SGU_DIM))

    def split(name, width_shape):
        st = jnp.stack(outs[name])
        p = st[:, :mp].reshape((depth, batch, seq) + width_shape)
        s = st[:, mp:mp + ms].reshape((depth, db, n_tok) + width_shape)
        return p, s

    k_p, k_s = split("k", (N_KV_HEADS, HEAD_DIM))
    v_p, v_s = split("v", (N_KV_HEADS, HEAD_DIM))
    ik_p, ik_s = split("ik", (IDX_DIM,))
    y_prompt = xf[:mp].reshape(batch, seq, d_model)
    y_sample = xf[mp:mp + ms].reshape(db, n_tok, d_model)
    return (y_prompt, y_sample, k_p, v_p, ik_p, jnp.stack(outs["conv_p"]), k_s, v_s, ik_s,
            jnp.stack(outs["conv_s"]), jnp.stack(outs["sgu_s"]))
```

```python
import functools
import math

import jax
import jax.numpy as jnp
import numpy as np
from jax import lax
from jax.experimental import pallas as pl
from jax.experimental.pallas import tpu as pltpu

F32 = jnp.float32
BF16 = jnp.bfloat16
I32 = jnp.int32

N_HEADS = 16
HEAD_DIM = 128
N_KV_HEADS = 4
KV_REP = N_HEADS // N_KV_HEADS
IDX_HEADS = 16
IDX_DIM = 64
TOPK_MAX = 256
PAGE_SIZE = 128
CONV_DIM = 1024
CONV_WIDTH = 31
SGU_DIM = 1024
SGU_GROUPS = 8
CHUNK = 128
N_BRANCH = 3
TOP_K_EXPERTS = 2
REL_BUCKETS = 32
REL_MAX_DIST = 128
LN_EPS = 1e-5

V7X_VMEM_BYTES = 64 * 2**20
VMEM_LIMIT = V7X_VMEM_BYTES - 8 * 2**20
LANES = 128
F32_SUBLANES = 8
INT_MIN = -(2**31)
MASK_NEG = -1e30

QKV_W = N_HEADS * HEAD_DIM
KV_W = N_KV_HEADS * HEAD_DIM
IQ_W = IDX_HEADS * IDX_DIM
HA_Q, HA_GLU, HA_UV, HA_GATE, HA_IQ = 0, 2048, 4096, 6144, 12288
HA_TILE = 1024
W_Q_ROW0 = 0
W_KV_ROW0 = W_Q_ROW0 + QKV_W
W_IQ_ROW0 = W_KV_ROW0 + 2 * KV_W
W_IK_ROW0 = W_IQ_ROW0 + IQ_W
W_GLU_ROW0 = W_IK_ROW0 + IDX_DIM + IDX_HEADS
W_UV_ROW0 = W_GLU_ROW0 + 2 * CONV_DIM
W_GATE_ROW0 = W_UV_ROW0 + 2 * SGU_DIM
KV_K, KV_V = 0, KV_W
IKW_IW = IDX_DIM

N_ROW_TILES = 8
SAMPLE_TILE = 256
MOE_SUB = 256
MOE_SUPER = 2048
MOE_TF = 256
MOE_WIDE = 2
COMBINE_TB = 128
IDX_PAGE_GROUP = 32
ATTN_PAGE_GROUP = 32


def _cp(sem, vmem=VMEM_LIMIT):
    return pltpu.CompilerParams(dimension_semantics=sem, vmem_limit_bytes=vmem)


def _round_up(x, m):
    return (x + m - 1) // m * m


def _ln_rows(z, g, b):
    mu = jnp.mean(z, axis=-1, keepdims=True)
    d = z - mu
    var = jnp.mean(d * d, axis=-1, keepdims=True)
    return d * lax.rsqrt(var + LN_EPS) * g + b


def _sigmoid(x):
    return 1.0 / (1.0 + jnp.exp(-x))


def _in_proj_body(off_ref, x_ref, w_ref, o_ref, wt_ref):
    @pl.when(pl.program_id(1) == 0)
    def _():
        wt_ref[...] = w_ref[0].T.astype(BF16)

    o_ref[...] = jnp.dot(x_ref[...], wt_ref[...], preferred_element_type=F32).astype(o_ref.dtype)


def _in_proj(x, w_t, layer, row_offsets, tm, tn):
    m, k = x.shape
    n_tiles = len(row_offsets)
    grid_spec = pltpu.PrefetchScalarGridSpec(
        num_scalar_prefetch=1,
        grid=(n_tiles, m // tm),
        in_specs=[pl.BlockSpec((tm, k), lambda j, i, off: (i, 0)),
                  pl.BlockSpec((pl.Element(1), pl.Element(tn), pl.Element(k)),
                               lambda j, i, off: (layer, pl.multiple_of(off[j], F32_SUBLANES), 0))],
        out_specs=pl.BlockSpec((tm, tn), lambda j, i, off: (i, j)),
        scratch_shapes=[pltpu.VMEM((k, tn), BF16)],
    )
    return pl.pallas_call(
        _in_proj_body,
        grid_spec=grid_spec,
        out_shape=jax.ShapeDtypeStruct((m, n_tiles * tn), BF16),
        compiler_params=_cp(("arbitrary", "arbitrary")),
        name="in_proj",
    )(jnp.asarray(row_offsets, I32), x, w_t)


def _kv_proj_body(x_ref, wkv_ref, wik_ref, kv_ref, ik_ref, wkv_t, wik_t):
    @pl.when(pl.program_id(0) == 0)
    def _():
        wkv_t[...] = wkv_ref[...].T.astype(BF16)
        wik_t[...] = wik_ref[...].T.astype(BF16)

    x = x_ref[...]
    kv_ref[...] = jnp.dot(x, wkv_t[...], preferred_element_type=F32)
    ik_ref[...] = jnp.dot(x, wik_t[...], preferred_element_type=F32)


def _kv_proj(x, w_t, layer, tm):
    m, k = x.shape
    assert W_KV_ROW0 % (2 * KV_W) == 0 and W_IK_ROW0 % LANES == 0
    return pl.pallas_call(
        _kv_proj_body,
        grid=(m // tm,),
        in_specs=[pl.BlockSpec((tm, k), lambda i: (i, 0)),
                  pl.BlockSpec((None, 2 * KV_W, k), lambda i: (layer, W_KV_ROW0 // (2 * KV_W), 0)),
                  pl.BlockSpec((None, LANES, k), lambda i: (layer, W_IK_ROW0 // LANES, 0))],
        out_specs=[pl.BlockSpec((tm, 2 * KV_W), lambda i: (i, 0)),
                   pl.BlockSpec((tm, LANES), lambda i: (i, 0))],
        out_shape=[jax.ShapeDtypeStruct((m, 2 * KV_W), F32), jax.ShapeDtypeStruct((m, LANES), F32)],
        scratch_shapes=[pltpu.VMEM((k, 2 * KV_W), BF16), pltpu.VMEM((k, LANES), BF16)],
        compiler_params=_cp(("arbitrary",)),
        name="kv_proj",
    )(x, w_t, w_t)


def _row_halves(rows):
    half = rows // 2
    return (slice(0, half), slice(half, rows)) if half % 16 == 0 else (slice(0, rows),)


def _mm_ln_body(x_ref, w_ref, r_ref, g_ref, b_ref, of_ref, ob_ref, *, alpha):
    for rs in _row_halves(x_ref.shape[0]):
        z = jnp.dot(x_ref[rs, :], w_ref[...], preferred_element_type=F32) + alpha * r_ref[rs, :]
        y = _ln_rows(z, g_ref[...], b_ref[...])
        of_ref[rs, :] = y
        ob_ref[rs, :] = y.astype(BF16)


def _mm_ln(x, w, layer, resid, g, b, alpha, tm):
    m, k = x.shape
    n = w.shape[2]
    row = lambda i: (i, 0)
    return pl.pallas_call(
        functools.partial(_mm_ln_body, alpha=alpha),
        grid=(m // tm,),
        in_specs=[pl.BlockSpec((tm, k), row),
                  pl.BlockSpec((None, k, n), lambda i: (layer, 0, 0), pipeline_mode=pl.Buffered(1)),
                  pl.BlockSpec((tm, n), row),
                  pl.BlockSpec((1, n), lambda i: (0, 0)),
                  pl.BlockSpec((1, n), lambda i: (0, 0))],
        out_specs=[pl.BlockSpec((tm, n), row), pl.BlockSpec((tm, n), row)],
        out_shape=[jax.ShapeDtypeStruct((m, n), F32), jax.ShapeDtypeStruct((m, n), BF16)],
        compiler_params=_cp(("arbitrary",)),
        name="proj_ln",
    )(x, w, resid, g, b)


def _swiglu_up_body(x_ref, w1_ref, w3_ref, o_ref):
    for rs in _row_halves(x_ref.shape[0]):
        x = x_ref[rs, :]
        a = jnp.dot(x, w1_ref[...], preferred_element_type=F32)
        c = jnp.dot(x, w3_ref[...], preferred_element_type=F32)
        o_ref[rs, :] = (a * _sigmoid(a) * c).astype(o_ref.dtype)


def _swiglu_up(x, w1, w3, layer, tm, tf):
    m, k = x.shape
    f = w1.shape[2]
    return pl.pallas_call(
        _swiglu_up_body,
        grid=(f // tf, m // tm),
        in_specs=[pl.BlockSpec((tm, k), lambda j, i: (i, 0)),
                  pl.BlockSpec((None, k, tf), lambda j, i: (layer, 0, j)),
                  pl.BlockSpec((None, k, tf), lambda j, i: (layer, 0, j))],
        out_specs=pl.BlockSpec((tm, tf), lambda j, i: (i, j)),
        out_shape=jax.ShapeDtypeStruct((m, f), BF16),
        compiler_params=_cp(("arbitrary", "arbitrary")),
        name="ffn_up",
    )(x, w1, w3)


def _merge_body(ap_ref, cp_ref, sp_ref, as_ref, cs_ref, ss_ref, g_ref, wa_ref, wb_ref, wc_ref, o_ref, *, n_prompt):
    i = pl.program_id(0)
    d = o_ref.shape[1]

    def merge(a_ref, c_ref, s_ref):
        for rs in _row_halves(o_ref.shape[0]):
            pa = jnp.dot(a_ref[rs, :], wa_ref[...], preferred_element_type=F32)
            pb = jnp.dot(c_ref[rs, :], wb_ref[...], preferred_element_type=F32)
            pc = jnp.dot(s_ref[rs, :], wc_ref[...], preferred_element_type=F32)
            o = (_sigmoid(g_ref[rs, :d].astype(F32)) * pa + _sigmoid(g_ref[rs, d:2 * d].astype(F32)) * pb
                 + _sigmoid(g_ref[rs, 2 * d:].astype(F32)) * pc)
            o_ref[rs, :] = o.astype(o_ref.dtype)

    @pl.when(i < n_prompt)
    def _():
        merge(ap_ref, cp_ref, sp_ref)

    @pl.when(i >= n_prompt)
    def _():
        merge(as_ref, cs_ref, ss_ref)


def _merge(prompt, sample, ha, w_pa, w_pb, w_pc, layer, tm):
    m = ha.shape[0]
    d = w_pa.shape[2]
    n_prompt = prompt[0].shape[0] // tm
    assert prompt[0].shape[0] % tm == 0 and sample[0].shape[0] == tm and m == (n_prompt + 1) * tm
    assert HA_GATE % (N_BRANCH * d) == 0
    p_row = lambda i: (jnp.minimum(i, n_prompt - 1), 0)
    fixed = lambda i: (0, 0)
    return pl.pallas_call(
        functools.partial(_merge_body, n_prompt=n_prompt),
        grid=(n_prompt + 1,),
        in_specs=[pl.BlockSpec((tm, t.shape[1]), p_row) for t in prompt]
        + [pl.BlockSpec((tm, t.shape[1]), fixed) for t in sample]
        + [pl.BlockSpec((tm, N_BRANCH * d), lambda i: (i, HA_GATE // (N_BRANCH * d)))]
        + [pl.BlockSpec((None,) + w.shape[1:], lambda i: (layer, 0, 0), pipeline_mode=pl.Buffered(1))
           for w in (w_pa, w_pb, w_pc)],
        out_specs=pl.BlockSpec((tm, d), lambda i: (i, 0)),
        out_shape=jax.ShapeDtypeStruct((m, d), BF16),
        compiler_params=_cp(("arbitrary",)),
        name="branch_merge",
    )(*prompt, *sample, ha, w_pa, w_pb, w_pc)


def _sortable_key(score):
    bits = pltpu.bitcast(score, I32)
    return bits ^ ((bits >> 31) & jnp.int32(0x7FFFFFFF))


def _kth_largest_key(count_ge, rows, k):
    def step(it, t):
        cand = t + jnp.left_shift(jnp.int32(1), 31 - it)
        return jnp.where(count_ge(cand) >= k, cand, t)

    return lax.fori_loop(0, 32, step, jnp.full((rows, 1), INT_MIN, I32))


RADIX4_STEPS = 16
RADIX4_FIELD_BITS = 5
RADIX4_WIDE = 4096.0


def _kth_largest_key_radix4(count3, rows, k, n_valid):
    kf = float(k)

    def cond(state):
        it, _, _, pending = state
        return jnp.logical_and(it < RADIX4_STEPS, pending > 0.0)

    def body(state):
        it, t, cnt, _ = state
        one = jnp.left_shift(jnp.int32(1), 30 - 2 * it)
        c1 = t + one
        c2 = c1 + one
        c3 = c2 + one
        n1, n2, n3 = count3(c1, c2, c3)
        t = jnp.where(n3 >= kf, c3, jnp.where(n2 >= kf, c2, jnp.where(n1 >= kf, c1, t)))
        cnt = jnp.where(n3 >= kf, n3, jnp.where(n2 >= kf, n2, jnp.where(n1 >= kf, n1, cnt)))
        settled = jnp.logical_or(cnt == kf, n_valid <= kf)
        return it + 1, t, cnt, jnp.max(jnp.where(settled, 0.0, 1.0))

    init = (jnp.int32(0), jnp.full((rows, 1), INT_MIN, I32), jnp.full((rows, 1), -1.0, F32), jnp.float32(1.0))
    return lax.while_loop(cond, body, init)[1]


def _lane_tile(x, width):
    return x if width == LANES else jnp.concatenate([x] * (width // LANES), axis=1)


def _attn_prompt_body(q_ref, iq_ref, iw_ref, kv_ref, ik_ref, bias_ref, o_ref,
                      keys_ref, mb_ref, iwb_ref, m_ref, l_ref, acc_ref, *, tq, k_sel):
    i = pl.program_id(1)
    nkb = i + 1
    tk = tq
    row = lax.broadcasted_iota(I32, (tq, tk), 0)
    col = lax.broadcasted_iota(I32, (tq, tk), 1)
    low_half = lax.broadcasted_iota(I32, (tk, LANES), 1) < IDX_DIM
    iw = iw_ref[...] * (IDX_DIM ** -0.5 * IDX_HEADS ** -0.5)
    for h in range(IDX_HEADS):
        iwb_ref[h] = jnp.broadcast_to(iw[:, IKW_IW + h:IKW_IW + h + 1], (tq, LANES))
    iq_pairs = [iq_ref[:, p * LANES:(p + 1) * LANES] for p in range(IDX_HEADS // 2)]

    def idx_block(kb, carry):
        ik_lo = jnp.where(low_half, ik_ref[pl.ds(pl.multiple_of(kb * tk, tk), tk), :], 0.0)
        ik_sides = (ik_lo.astype(BF16), pltpu.roll(ik_lo, IDX_DIM, axis=1).astype(BF16))
        acc = jnp.zeros((tq, tk), F32)
        for h in range(IDX_HEADS):
            s = lax.dot_general(iq_pairs[h // 2], ik_sides[h % 2], (((1,), (1,)), ((), ())),
                                preferred_element_type=F32)
            acc = acc + jnp.maximum(s, 0.0) * _lane_tile(iwb_ref[h], tk)
        causal = (col + kb * tk) <= (row + i * tq)
        keys_ref[kb] = jnp.where(causal, _sortable_key(acc), jnp.int32(INT_MIN))
        return carry

    lax.fori_loop(0, nkb, idx_block, 0)

    fb = RADIX4_FIELD_BITS
    field_mask = (1 << fb) - 1
    inc1, inc2, inc3 = 1, 1 + (1 << fb), 1 + (1 << fb) + (1 << 2 * fb)

    def count3(c1, c2, c3):
        b1, b2, b3 = (jnp.broadcast_to(c, (tq, LANES)) for c in (c1, c2, c3))

        def cb(kb, acc):
            keys = keys_ref[kb]
            for t in range(tk // LANES):
                kk = keys[:, t * LANES:(t + 1) * LANES]
                acc = acc + jnp.where(kk >= b3, inc3, jnp.where(kk >= b2, inc2, jnp.where(kk >= b1, inc1, 0)))
            return acc

        acc = lax.fori_loop(0, nkb, cb, jnp.zeros((tq, LANES), I32))
        n3 = (acc >> (2 * fb)).astype(F32).sum(axis=-1, keepdims=True)
        low = ((acc & field_mask).astype(F32) + ((acc >> fb) & field_mask).astype(F32) * RADIX4_WIDE)
        low = low.sum(axis=-1, keepdims=True)
        n2 = jnp.floor(low * (1.0 / RADIX4_WIDE))
        return low - RADIX4_WIDE * n2, n2, n3

    n_valid = (lax.broadcasted_iota(I32, (tq, 1), 0) + (i * tq + 1)).astype(F32)
    thr = _kth_largest_key_radix4(count3, tq, k_sel, n_valid)
    thr = jnp.maximum(thr, jnp.int32(INT_MIN + 1))
    thr_b = _lane_tile(jnp.broadcast_to(thr, (tq, LANES)), tk)

    def mask_block(kb, carry):
        mb_ref[kb] = jnp.where(keys_ref[kb] >= thr_b, 0.0, MASK_NEG)
        return carry

    lax.fori_loop(0, nkb, mask_block, 0)

    scale = HEAD_DIM ** -0.5
    for g in range(N_KV_HEADS):
        qg = jnp.concatenate(
            [q_ref[:, (KV_REP * g + r) * HEAD_DIM:(KV_REP * g + r + 1) * HEAD_DIM] for r in range(KV_REP)], axis=0)
        m_ref[...] = jnp.full(m_ref.shape, MASK_NEG, F32)
        l_ref[...] = jnp.zeros(l_ref.shape, F32)
        acc_ref[...] = jnp.zeros(acc_ref.shape, F32)

        def block(kb, bias_idx, g=g, qg=qg):
            r0 = pl.multiple_of(kb * tk, tk)
            kblk = kv_ref[pl.ds(r0, tk), KV_K + g * HEAD_DIM:KV_K + (g + 1) * HEAD_DIM].astype(BF16)
            vblk = kv_ref[pl.ds(r0, tk), KV_V + g * HEAD_DIM:KV_V + (g + 1) * HEAD_DIM].astype(BF16)
            s = lax.dot_general(qg, kblk, (((1,), (1,)), ((), ())), preferred_element_type=F32) * scale
            s = s.reshape(KV_REP, tq, tk) + mb_ref[kb][None]
            if bias_idx is not None:
                s = s + bias_ref[bias_idx, KV_REP * g:KV_REP * (g + 1)]
            s = s.reshape(KV_REP * tq, tk)
            m_old = m_ref[...]
            m_new = jnp.maximum(m_old, s.max(axis=-1, keepdims=True))
            p = jnp.exp(s - _lane_tile(m_new, tk))
            alpha = jnp.exp(m_old - m_new)
            l_ref[...] = alpha * l_ref[...] + p.sum(axis=-1, keepdims=True)
            acc_ref[...] = alpha * acc_ref[...] + jnp.dot(p.astype(BF16), vblk, preferred_element_type=F32)
            m_ref[...] = m_new

        n_far = jnp.maximum(i - 1, 0)

        def far_pair(t, carry):
            block(2 * t, None)
            block(2 * t + 1, None)
            return carry

        lax.fori_loop(0, n_far // 2, far_pair, 0)

        @pl.when(n_far % 2 == 1)
        def _():
            block(n_far - 1, None)

        @pl.when(i >= 1)
        def _():
            block(i - 1, 1)
            block(i, 0)

        @pl.when(i == 0)
        def _():
            block(i, 0)
        out = acc_ref[...] / l_ref[...]
        for r in range(KV_REP):
            h = KV_REP * g + r
            o_ref[:, h * HEAD_DIM:(h + 1) * HEAD_DIM] = out[r * tq:(r + 1) * tq].astype(o_ref.dtype)


def _attn_prompt(ha, kv, ikw, bias_tiles, batch, seq, tq):
    nq = seq // tq
    k_sel = min(TOPK_MAX, seq // 4)
    assert seq // LANES < (1 << RADIX4_FIELD_BITS)
    body = functools.partial(_attn_prompt_body, tq=tq, k_sel=k_sel)
    return pl.pallas_call(
        body,
        grid=(batch, nq),
        in_specs=[pl.BlockSpec((tq, QKV_W), lambda b, i: (b * nq + i, HA_Q // QKV_W)),
                  pl.BlockSpec((tq, IQ_W), lambda b, i: (b * nq + i, HA_IQ // IQ_W)),
                  pl.BlockSpec((tq, LANES), lambda b, i: (b * nq + i, 0)),
                  pl.BlockSpec((seq, 2 * KV_W), lambda b, i: (b, 0), pipeline_mode=pl.Buffered(1)),
                  pl.BlockSpec((seq, LANES), lambda b, i: (b, 0), pipeline_mode=pl.Buffered(1)),
                  pl.BlockSpec((2, N_HEADS, tq, tq), lambda b, i: (0, 0, 0, 0), pipeline_mode=pl.Buffered(1))],
        out_specs=pl.BlockSpec((tq, QKV_W), lambda b, i: (b * nq + i, 0)),
        out_shape=jax.ShapeDtypeStruct((batch * seq, QKV_W), BF16),
        scratch_shapes=[pltpu.VMEM((nq, tq, tq), I32),
                        pltpu.VMEM((nq, tq, tq), F32),
                        pltpu.VMEM((IDX_HEADS, tq, LANES), F32),
                        pltpu.VMEM((KV_REP * tq, LANES), F32),
                        pltpu.VMEM((KV_REP * tq, LANES), F32),
                        pltpu.VMEM((KV_REP * tq, HEAD_DIM), F32)],
        compiler_params=_cp(("arbitrary", "arbitrary")),
        name="attn_prompt",
    )(ha, ha, ikw, kv, ikw, bias_tiles)


def _page_spec(block, layer, group, slot):
    return pl.BlockSpec((None, None) + block, lambda b, p, pt: (layer, pt[b, p * group + slot], 0, 0))


def _idx_sample_body(pt_ref, iq_ref, iw_ref, *refs, n_tok, group):
    pages, o_ref = refs[:group], refs[group]
    ikt = jnp.concatenate([r[...].astype(BF16) for r in pages], axis=1)
    s = jnp.dot(iq_ref[0], ikt, preferred_element_type=F32)
    r = jnp.maximum(s, 0.0) * _lane_tile(iw_ref[0], group * PAGE_SIZE)
    tok_scores = r.reshape(n_tok, IDX_HEADS, group * PAGE_SIZE).sum(axis=1)
    for t in range(group):
        o_ref[0, t] = tok_scores[:, t * PAGE_SIZE:(t + 1) * PAGE_SIZE]


def _idx_sample(page_table, iq_rows, iw_rows, cache_ik_t, layer, n_tok, group):
    db, n_pages = page_table.shape
    rows = n_tok * IDX_HEADS
    body = functools.partial(_idx_sample_body, n_tok=n_tok, group=group)
    grid_spec = pltpu.PrefetchScalarGridSpec(
        num_scalar_prefetch=1,
        grid=(db, n_pages // group),
        in_specs=[pl.BlockSpec((1, rows, IDX_DIM), lambda b, p, pt: (b, 0, 0)),
                  pl.BlockSpec((1, rows, LANES), lambda b, p, pt: (b, 0, 0))]
        + [_page_spec((IDX_DIM, PAGE_SIZE), layer, group, t) for t in range(group)],
        out_specs=pl.BlockSpec((1, group, n_tok, PAGE_SIZE), lambda b, p, pt: (b, p, 0, 0)),
    )
    return pl.pallas_call(
        body,
        grid_spec=grid_spec,
        out_shape=jax.ShapeDtypeStruct((db, n_pages, n_tok, PAGE_SIZE), F32),
        compiler_params=_cp(("arbitrary", "arbitrary")),
        name="idx_sample",
    )(page_table, iq_rows, iw_rows, *([cache_ik_t] * group))


def _attn_sample_body(pt_ref, sc_ref, scn_ref, q_ref, kn_ref, vn_ref, bl_ref, bn_ref, *refs,
                      n_tok, n_pages, k_sel, group):
    kp_refs, vp_refs = refs[:group], refs[group:2 * group]
    o_ref, thr_ref, m_ref, l_ref, acc_ref = refs[2 * group:]
    p = pl.program_id(1)
    n_steps = n_pages // group
    grp = KV_REP * n_tok
    width = group * PAGE_SIZE
    scale = HEAD_DIM ** -0.5
    r_i = lax.broadcasted_iota(I32, (n_tok, LANES), 0)
    c_i = lax.broadcasted_iota(I32, (n_tok, LANES), 1)
    keys_new = jnp.where(c_i <= r_i, _sortable_key(scn_ref[0]), jnp.int32(INT_MIN))

    @pl.when(p == 0)
    def _():
        keys = _sortable_key(sc_ref[0])

        def count_ge(cand):
            cand_b = jnp.broadcast_to(cand, (n_tok, LANES))
            c = jnp.where(keys >= cand_b[None], 1.0, 0.0).sum(axis=0) + jnp.where(keys_new >= cand_b, 1.0, 0.0)
            return c.sum(axis=-1, keepdims=True)

        thr = _kth_largest_key(count_ge, n_tok, float(k_sel))
        thr_ref[...] = jnp.broadcast_to(jnp.maximum(thr, jnp.int32(INT_MIN + 1)), (n_tok, LANES))
        m_ref[...] = jnp.full(m_ref.shape, MASK_NEG, F32)
        l_ref[...] = jnp.zeros(l_ref.shape, F32)
        acc_ref[...] = jnp.zeros(acc_ref.shape, F32)

    thr_b = thr_ref[...]

    def update(g, s, vblk):
        sl = slice(g * grp, (g + 1) * grp)
        m_old = m_ref[sl, :]
        m_new = jnp.maximum(m_old, s.max(axis=-1, keepdims=True))
        pr = jnp.exp(s - _lane_tile(m_new, s.shape[1]))
        alpha = jnp.exp(m_old - m_new)
        l_ref[sl, :] = alpha * l_ref[sl, :] + pr.sum(axis=-1, keepdims=True)
        acc_ref[sl, :] = alpha * acc_ref[sl, :] + jnp.dot(pr.astype(BF16), vblk, preferred_element_type=F32)
        m_ref[sl, :] = m_new

    def group_rows(mask_tok):
        return jnp.concatenate([mask_tok] * KV_REP, axis=0)

    base = p * group
    mb = jnp.concatenate(
        [jnp.where(_sortable_key(sc_ref[0, base + t]) >= thr_b, 0.0, MASK_NEG) for t in range(group)], axis=1)
    mb_g = group_rows(mb)
    is_last = p == n_steps - 1
    zeros_head = jnp.zeros((grp, width - PAGE_SIZE), F32)
    for g in range(N_KV_HEADS):
        qg = q_ref[0, g * grp:(g + 1) * grp, :]
        kg = jnp.concatenate([r[pl.ds(g, PAGE_SIZE, stride=N_KV_HEADS), :].astype(BF16) for r in kp_refs], axis=0)
        vg = jnp.concatenate([r[pl.ds(g, PAGE_SIZE, stride=N_KV_HEADS), :].astype(BF16) for r in vp_refs], axis=0)
        s = lax.dot_general(qg, kg, (((1,), (1,)), ((), ())), preferred_element_type=F32) * scale
        near = jnp.where(is_last, bl_ref[g * grp:(g + 1) * grp, :], 0.0)
        s = s + mb_g + (near if group == 1 else jnp.concatenate([zeros_head, near], axis=1))
        update(g, s, vg)

    @pl.when(is_last)
    def _():
        mbn_g = group_rows(jnp.where(keys_new >= thr_b, 0.0, MASK_NEG))
        for g in range(N_KV_HEADS):
            qg = q_ref[0, g * grp:(g + 1) * grp, :]
            kblk = kn_ref[0, :, g * HEAD_DIM:(g + 1) * HEAD_DIM].astype(BF16)
            vblk = vn_ref[0, :, g * HEAD_DIM:(g + 1) * HEAD_DIM].astype(BF16)
            s = lax.dot_general(qg, kblk, (((1,), (1,)), ((), ())), preferred_element_type=F32) * scale
            s = s + mbn_g + bn_ref[g * grp:(g + 1) * grp, :]
            update(g, s, vblk)
        o_ref[0] = acc_ref[...] / l_ref[...]


def _attn_sample(page_table, scores, scores_new, q_rows, k_new, v_new, bias_last, bias_new,
                 cache_k, cache_v, layer, n_tok, group):
    db, n_pages = page_table.shape
    past = n_pages * PAGE_SIZE
    rows = N_HEADS * n_tok
    k_sel = min(TOPK_MAX, (past + n_tok) // 4)
    body = functools.partial(_attn_sample_body, n_tok=n_tok, n_pages=n_pages, k_sel=k_sel, group=group)
    page = (PAGE_SIZE * N_KV_HEADS, HEAD_DIM)
    grid_spec = pltpu.PrefetchScalarGridSpec(
        num_scalar_prefetch=1,
        grid=(db, n_pages // group),
        in_specs=[pl.BlockSpec((1, n_pages, n_tok, PAGE_SIZE), lambda b, p, pt: (b, 0, 0, 0)),
                  pl.BlockSpec((1, n_tok, LANES), lambda b, p, pt: (b, 0, 0)),
                  pl.BlockSpec((1, rows, HEAD_DIM), lambda b, p, pt: (b, 0, 0)),
                  pl.BlockSpec((1, LANES, KV_W), lambda b, p, pt: (b, 0, 0)),
                  pl.BlockSpec((1, LANES, KV_W), lambda b, p, pt: (b, 0, 0)),
                  pl.BlockSpec((rows, LANES), lambda b, p, pt: (0, 0)),
                  pl.BlockSpec((rows, LANES), lambda b, p, pt: (0, 0))]
        + [_page_spec(page, layer, group, t) for t in range(group)]
        + [_page_spec(page, layer, group, t) for t in range(group)],
        out_specs=pl.BlockSpec((1, rows, HEAD_DIM), lambda b, p, pt: (b, 0, 0)),
        scratch_shapes=[pltpu.VMEM((n_tok, LANES), I32),
                        pltpu.VMEM((rows, LANES), F32),
                        pltpu.VMEM((rows, LANES), F32),
                        pltpu.VMEM((rows, HEAD_DIM), F32)],
    )
    return pl.pallas_call(
        body,
        grid_spec=grid_spec,
        out_shape=jax.ShapeDtypeStruct((db, rows, HEAD_DIM), F32),
        compiler_params=_cp(("arbitrary", "arbitrary")),
        name="attn_sample",
    )(page_table, scores, scores_new, q_rows, k_new, v_new, bias_last, bias_new,
      *([cache_k] * group), *([cache_v] * group))


def _idx_new_body(iq_ref, iw_ref, ik_ref, o_ref, *, n_tok):
    for b in range(iq_ref.shape[0]):
        s = lax.dot_general(iq_ref[b], ik_ref[b].astype(BF16), (((1,), (1,)), ((), ())),
                            preferred_element_type=F32)
        r = jnp.maximum(s, 0.0) * iw_ref[b]
        o_ref[b] = r.reshape(n_tok, IDX_HEADS, LANES).sum(axis=1)


def _idx_new(iq_rows, iw_rows, ik_new, n_tok):
    db = iq_rows.shape[0]
    return pl.pallas_call(
        functools.partial(_idx_new_body, n_tok=n_tok),
        out_shape=jax.ShapeDtypeStruct((db, n_tok, LANES), F32),
        name="idx_new",
    )(iq_rows, iw_rows, ik_new)


HALO = CONV_WIDTH - 1
HALO_PAD = 32
CONV_ROW_BLOCK = 128


CONV_CHUNKS = CONV_DIM // LANES


def _conv_prompt_body(glu_ref, cw_ref, cb_ref, g_ref, b_ref, o_ref, st_ref, hp_ref, c_ref, sh_ref, *, ts, ns):
    s_idx = pl.program_id(1)

    @pl.when(s_idx == 0)
    def _():
        hp_ref[:, 0:HALO_PAD, :] = jnp.zeros((CONV_CHUNKS, HALO_PAD, LANES), F32)

    for c in range(CONV_CHUNKS):
        cs = slice(c * LANES, (c + 1) * LANES)
        a = glu_ref[:, cs].astype(F32)
        gt = glu_ref[:, CONV_DIM + c * LANES:CONV_DIM + (c + 1) * LANES].astype(F32)
        hp_ref[c, HALO_PAD:HALO_PAD + ts, :] = a * _sigmoid(gt)
    off = HALO_PAD - HALO
    rows = min(ts, CONV_ROW_BLOCK)

    def chunk(c, carry):
        for r0 in range(0, ts, rows):
            acc = jnp.zeros((rows // F32_SUBLANES, F32_SUBLANES, LANES), F32)
            for shift in range(F32_SUBLANES):
                taps = range(shift, CONV_WIDTH, F32_SUBLANES)
                span = rows + taps[-1] - shift
                sh_ref[0:span, :] = hp_ref[c, off + r0 + shift:off + r0 + shift + span, :]
                for w in taps:
                    x = sh_ref[w - shift:w - shift + rows, :].reshape(rows // F32_SUBLANES, F32_SUBLANES, LANES)
                    acc = acc + x * cw_ref[c, w][None]
            c_ref[c, r0:r0 + rows, :] = acc.reshape(rows, LANES)
        return carry

    lax.fori_loop(0, CONV_CHUNKS, chunk, 0)

    conv = [c_ref[c] + cb_ref[:, c * LANES:(c + 1) * LANES] for c in range(CONV_CHUNKS)]
    mu = sum(v.sum(axis=-1, keepdims=True) for v in conv) / CONV_DIM
    var = sum(((v - mu) * (v - mu)).sum(axis=-1, keepdims=True) for v in conv) / CONV_DIM
    inv = lax.rsqrt(var + LN_EPS)
    for c in range(CONV_CHUNKS):
        cs = slice(c * LANES, (c + 1) * LANES)
        y = (conv[c] - mu) * inv * g_ref[:, cs] + b_ref[:, cs]
        o_ref[:, cs] = (y * _sigmoid(y)).astype(o_ref.dtype)

    @pl.when(s_idx == ns - 1)
    def _():
        for c in range(CONV_CHUNKS):
            st_ref[0, :, c * LANES:(c + 1) * LANES] = hp_ref[c, ts + off:ts + HALO_PAD, :]

    hp_ref[:, off:HALO_PAD, :] = hp_ref[:, ts + off:ts + HALO_PAD, :]


def _conv_prompt(ha, conv_w, conv_b, ln_g, ln_b, batch, seq, ts):
    ns = seq // ts
    body = functools.partial(_conv_prompt_body, ts=ts, ns=ns)
    vec = pl.BlockSpec((1, CONV_DIM), lambda b, s: (0, 0))
    return pl.pallas_call(
        body,
        grid=(batch, ns),
        in_specs=[pl.BlockSpec((ts, 2 * CONV_DIM), lambda b, s: (b * ns + s, HA_GLU // (2 * CONV_DIM))),
                  pl.BlockSpec((CONV_CHUNKS, CONV_WIDTH, F32_SUBLANES, LANES), lambda b, s: (0, 0, 0, 0)),
                  vec, vec, vec],
        out_specs=[pl.BlockSpec((ts, CONV_DIM), lambda b, s: (b * ns + s, 0)),
                   pl.BlockSpec((1, HALO, CONV_DIM), lambda b, s: (b, 0, 0))],
        out_shape=[jax.ShapeDtypeStruct((batch * seq, CONV_DIM), BF16),
                   jax.ShapeDtypeStruct((batch, HALO, CONV_DIM), F32)],
        scratch_shapes=[pltpu.VMEM((CONV_CHUNKS, HALO_PAD + ts, LANES), F32),
                        pltpu.VMEM((CONV_CHUNKS, ts, LANES), F32),
                        pltpu.VMEM((HALO_PAD + min(ts, CONV_ROW_BLOCK), LANES), F32)],
        compiler_params=_cp(("arbitrary", "arbitrary")),
        name="conv_prompt",
    )(ha, conv_w, conv_b, ln_g, ln_b)


def _sgu_prompt_body(uv_ref, g_ref, b_ref, w_ref, bs_ref, o_ref, *, ts):
    u = uv_ref[:, :SGU_DIM].astype(F32)
    vn = _ln_rows(uv_ref[:, SGU_DIM:].astype(F32), g_ref[...], b_ref[...]).astype(BF16)
    r_i = lax.broadcasted_iota(I32, (CHUNK, CHUNK), 0)
    c_i = lax.broadcasted_iota(I32, (CHUNK, CHUNK), 1)
    gd = SGU_DIM // SGU_GROUPS
    for g in range(SGU_GROUPS):
        wm = jnp.where(c_i <= r_i, w_ref[g], 0.0).astype(BF16)
        bias = bs_ref[:, g:g + 1]
        for c in range(ts // CHUNK):
            rs = slice(c * CHUNK, (c + 1) * CHUNK)
            gs = slice(g * gd, (g + 1) * gd)
            mixed = jnp.dot(wm, vn[rs, gs], preferred_element_type=F32) + bias
            o_ref[rs, gs] = (u[rs, gs] * mixed).astype(o_ref.dtype)


def _sgu_prompt(ha, ln_g, ln_b, w_s, b_s_t, batch, seq, ts):
    ns = seq // ts
    vec = pl.BlockSpec((1, SGU_DIM), lambda b, s: (0, 0))
    return pl.pallas_call(
        functools.partial(_sgu_prompt_body, ts=ts),
        grid=(batch, ns),
        in_specs=[pl.BlockSpec((ts, 2 * SGU_DIM), lambda b, s: (b * ns + s, HA_UV // (2 * SGU_DIM))),
                  vec, vec,
                  pl.BlockSpec((SGU_GROUPS, CHUNK, CHUNK), lambda b, s: (0, 0, 0)),
                  pl.BlockSpec((CHUNK, SGU_GROUPS), lambda b, s: (0, 0))],
        out_specs=pl.BlockSpec((ts, SGU_DIM), lambda b, s: (b * ns + s, 0)),
        out_shape=jax.ShapeDtypeStruct((batch * seq, SGU_DIM), BF16),
        compiler_params=_cp(("arbitrary", "arbitrary")),
        name="sgu_prompt",
    )(ha, ln_g, ln_b, w_s, b_s_t)


def _mix_sample_body(glu_ref, uv_ref, st_ref, cw_ref, cb_ref, cg_ref, cbt_ref, sg_ref, sb_ref, wv_ref, bv_ref,
                     co_ref, nst_ref, so_ref, vn_ref, hp_ref, *, db, n_tok):
    cw = cw_ref[...]
    for b in range(db):
        rs = slice(b * n_tok, (b + 1) * n_tok)
        a = glu_ref[rs, :CONV_DIM].astype(F32)
        gt = glu_ref[rs, CONV_DIM:].astype(F32)
        hp_ref[0:HALO, :] = st_ref[b]
        hp_ref[HALO:HALO + n_tok, :] = a * _sigmoid(gt)
        rows = [jnp.sum(hp_ref[t:t + CONV_WIDTH, :] * cw, axis=0, keepdims=True) for t in range(n_tok)]
        c = jnp.concatenate(rows, axis=0) + cb_ref[...]
        y = _ln_rows(c, cg_ref[...], cbt_ref[...])
        co_ref[rs, :] = y * _sigmoid(y)
        nst_ref[b] = hp_ref[n_tok:n_tok + HALO, :]
        u = uv_ref[rs, :SGU_DIM].astype(F32)
        vn = _ln_rows(uv_ref[rs, SGU_DIM:].astype(F32), sg_ref[...], sb_ref[...])
        vn_ref[rs, :] = vn
        mixed = []
        for t in range(n_tok):
            acc = bv_ref[t:t + 1, :]
            for s in range(t + 1):
                acc = acc + wv_ref[t, s:s + 1, :] * vn[s:s + 1, :]
            mixed.append(acc)
        so_ref[rs, :] = u * jnp.concatenate(mixed, axis=0)


def _mix_sample(glu, uv, state, conv_w, conv_b, cln_g, cln_b, sln_g, sln_b, wv, bv, db, n_tok):
    rows = db * n_tok
    return pl.pallas_call(
        functools.partial(_mix_sample_body, db=db, n_tok=n_tok),
        out_shape=[jax.ShapeDtypeStruct((rows, CONV_DIM), F32),
                   jax.ShapeDtypeStruct((db, HALO, CONV_DIM), F32),
                   jax.ShapeDtypeStruct((rows, SGU_DIM), F32),
                   jax.ShapeDtypeStruct((rows, SGU_DIM), F32)],
        scratch_shapes=[pltpu.VMEM((_round_up(HALO + n_tok, 8), CONV_DIM), F32)],
        name="mix_sample",
    )(glu, uv, state, conv_w, conv_b, cln_g, cln_b, sln_g, sln_b, wv, bv)


def _router_body(x_ref, w_ref, b_ref, ti_ref, tg_ref, *, n_exp):
    logits = jnp.dot(x_ref[...], w_ref[...], preferred_element_type=F32,
                     precision=lax.Precision.HIGHEST) + b_ref[...]
    lane = lax.broadcasted_iota(I32, logits.shape, 1)
    lane_f = lane.astype(F32)
    neg = -jnp.inf
    lg = jnp.where(lane < n_exp, logits, neg)
    v1 = lg.max(axis=-1, keepdims=True)
    i1 = jnp.where(lg == v1, lane_f, float(LANES)).min(axis=-1, keepdims=True)
    lg2 = jnp.where(lane_f == i1, neg, lg)
    v2 = lg2.max(axis=-1, keepdims=True)
    i2 = jnp.where(lg2 == v2, lane_f, float(LANES)).min(axis=-1, keepdims=True)
    e = jnp.exp(v2 - v1)
    den = 1.0 + e
    ti_ref[...] = jnp.where(lane == 0, i1, jnp.where(lane == 1, i2, 0.0)).astype(I32)
    tg_ref[...] = jnp.where(lane == 0, 1.0 / den, jnp.where(lane == 1, e / den, 0.0))


def _router(x, w_pad, b_pad, n_exp, tm):
    m, d = x.shape
    return pl.pallas_call(
        functools.partial(_router_body, n_exp=n_exp),
        grid=(m // tm,),
        in_specs=[pl.BlockSpec((tm, d), lambda i: (i, 0)),
                  pl.BlockSpec((d, LANES), lambda i: (0, 0)),
                  pl.BlockSpec((1, LANES), lambda i: (0, 0))],
        out_specs=[pl.BlockSpec((tm, LANES), lambda i: (i, 0)),
                   pl.BlockSpec((tm, LANES), lambda i: (i, 0))],
        out_shape=[jax.ShapeDtypeStruct((m, LANES), I32), jax.ShapeDtypeStruct((m, LANES), F32)],
        compiler_params=_cp(("arbitrary",)),
        name="moe_router",
    )(x, w_pad, b_pad)


def _row_copy(src_hbm, dst, src_row, dst_row, sem):
    return pltpu.make_async_copy(src_hbm.at[pl.ds(src_row, 1)], dst.at[pl.ds(dst_row, 1)], sem)


ROW_DMA_UNROLL = 8


def _prefetch_chunks(issue, wait):
    c = pl.program_id(0)

    @pl.when(c == 0)
    def _():
        issue(c, 0)

    @pl.when(c + 1 < pl.num_programs(0))
    def _():
        issue(c + 1, (c + 1) % 2)

    slot = c % 2
    wait(slot)
    return slot


def _dispatch_body(src_ref, x_hbm, o_ref, buf, sem, *, rows):
    def issue(chunk, slot):
        def start(u, carry):
            for v in range(ROW_DMA_UNROLL):
                r = u * ROW_DMA_UNROLL + v
                _row_copy(x_hbm, buf.at[slot], src_ref[chunk * rows + r], r, sem.at[slot]).start()
            return carry

        lax.fori_loop(0, rows // ROW_DMA_UNROLL, start, 0)

    def wait(slot):
        def w(r, carry):
            _row_copy(x_hbm, buf.at[slot], 0, r, sem.at[slot]).wait()
            return carry

        lax.fori_loop(0, rows, w, 0)

    slot = _prefetch_chunks(issue, wait)
    o_ref[...] = buf[slot].astype(o_ref.dtype)


def _dispatch(src, x, n_slots, rows):
    d = x.shape[1]
    grid_spec = pltpu.PrefetchScalarGridSpec(
        num_scalar_prefetch=1,
        grid=(n_slots // rows,),
        in_specs=[pl.BlockSpec(memory_space=pl.ANY)],
        out_specs=pl.BlockSpec((rows, d), lambda i, s: (i, 0)),
        scratch_shapes=[pltpu.VMEM((2, rows, d), x.dtype), pltpu.SemaphoreType.DMA((2,))],
    )
    assert rows % ROW_DMA_UNROLL == 0
    return pl.pallas_call(
        functools.partial(_dispatch_body, rows=rows),
        grid_spec=grid_spec,
        out_shape=jax.ShapeDtypeStruct((n_slots, d), BF16),
        compiler_params=_cp(("arbitrary",)),
        name="moe_dispatch",
    )(src, x)


def _tile_copy(src, dst, sem):
    return pltpu.make_async_copy(src, dst, sem)


def _moe_ffn_body(exp_ref, row0_ref, nsub_ref, jmap_ref, nzero_ref, xs_hbm, w1_ref, w3_ref, w2_ref, ys_hbm,
                  xbuf, acc, w1b, w3b, w2b, sem, *, nj, sub):
    w = pl.program_id(0)
    j = pl.program_id(1)
    nsub = nsub_ref[w]
    row0 = row0_ref[w]
    nzero = nzero_ref[w]

    def zero_copy(t):
        r = pl.multiple_of(row0 + t * sub, sub)
        return _tile_copy(acc.at[pl.ds(0, sub)], ys_hbm.at[pl.ds(r, sub)], sem)

    @pl.when(jnp.logical_and(j == 0, nzero > 0))
    def _():
        acc[0:sub, :] = jnp.zeros((sub, acc.shape[1]), F32)

        def zs(t, carry):
            zero_copy(t).start()
            return carry

        def zw(t, carry):
            zero_copy(t).wait()
            return carry

        lax.fori_loop(0, nzero, zs, 0)
        lax.fori_loop(0, nzero, zw, 0)

    def in_copy(t):
        r = pl.multiple_of(t * sub, sub)
        return _tile_copy(xs_hbm.at[pl.ds(pl.multiple_of(row0 + r, sub), sub)], xbuf.at[pl.ds(r, sub)], sem)

    def out_copy(t):
        r = pl.multiple_of(t * sub, sub)
        return _tile_copy(acc.at[pl.ds(r, sub)], ys_hbm.at[pl.ds(pl.multiple_of(row0 + r, sub), sub)], sem)

    def for_tiles(fn):
        def it(t, carry):
            fn(t)
            return carry
        lax.fori_loop(0, nsub, it, 0)

    def zero_tile(t):
        acc[pl.ds(pl.multiple_of(t * sub, sub), sub), :] = jnp.zeros((sub, acc.shape[1]), F32)

    @pl.when(j == 0)
    def _():
        for_tiles(lambda t: in_copy(t).start())
        for_tiles(zero_tile)
        for_tiles(lambda t: in_copy(t).wait())

    def tile(t, n, w1v, w3v, w2v):
        r = pl.multiple_of(t * sub, sub)
        x = xbuf[pl.ds(r, n * sub), :]
        a = jnp.dot(x, w1v, preferred_element_type=F32)
        c = jnp.dot(x, w3v, preferred_element_type=F32)
        h = (a * _sigmoid(a) * c).astype(BF16)
        acc[pl.ds(r, n * sub), :] += jnp.dot(h, w2v, preferred_element_type=F32)

    def cast_weights():
        return w1_ref[...].astype(BF16), w3_ref[...].astype(BF16), w2_ref[...].astype(BF16)

    per_trip = 2 * MOE_WIDE
    n_trips = nsub // per_trip

    def trip(t, wv):
        tile(per_trip * t, MOE_WIDE, *wv)
        tile(per_trip * t + MOE_WIDE, MOE_WIDE, *wv)

    @pl.when(n_trips >= 1)
    def _():
        wv = cast_weights()
        w1b[...], w3b[...], w2b[...] = wv
        trip(0, wv)

    @pl.when(jnp.logical_and(n_trips == 0, nsub > 0))
    def _():
        w1b[...], w3b[...], w2b[...] = cast_weights()

    def later_trip(t, carry):
        trip(t, (w1b[...], w3b[...], w2b[...]))
        return carry

    lax.fori_loop(1, n_trips, later_trip, 0)

    def rest(t, carry):
        tile(t, 1, w1b[...], w3b[...], w2b[...])
        return carry

    lax.fori_loop(n_trips * per_trip, nsub, rest, 0)

    @pl.when(j == nj - 1)
    def _():
        for_tiles(lambda t: out_copy(t).start())
        for_tiles(lambda t: out_copy(t).wait())


def _moe_ffn(item_exp, item_row0, item_nsub, item_jlast, item_nzero, xs, w1, w3, w2, layer, n_items, super_rows,
             sub, tf):
    n_slots, d = xs.shape
    f = w1.shape[3]
    nj = f // tf

    def jj(w, j, jl):
        return jnp.where(jl[w] > 0, j, nj - 1)

    grid_spec = pltpu.PrefetchScalarGridSpec(
        num_scalar_prefetch=5,
        grid=(n_items, nj),
        in_specs=[pl.BlockSpec(memory_space=pl.ANY),
                  pl.BlockSpec((None, None, d, tf), lambda w, j, ex, r0, ns, jl, nz: (layer, ex[w], 0, jj(w, j, jl))),
                  pl.BlockSpec((None, None, d, tf), lambda w, j, ex, r0, ns, jl, nz: (layer, ex[w], 0, jj(w, j, jl))),
                  pl.BlockSpec((None, None, tf, d), lambda w, j, ex, r0, ns, jl, nz: (layer, ex[w], jj(w, j, jl), 0))],
        out_specs=pl.BlockSpec(memory_space=pl.ANY),
        scratch_shapes=[pltpu.VMEM((super_rows, d), BF16),
                        pltpu.VMEM((super_rows, d), F32),
                        pltpu.VMEM((d, tf), BF16),
                        pltpu.VMEM((d, tf), BF16),
                        pltpu.VMEM((tf, d), BF16),
                        pltpu.SemaphoreType.DMA(())],
    )
    return pl.pallas_call(
        functools.partial(_moe_ffn_body, nj=nj, sub=sub),
        grid_spec=grid_spec,
        out_shape=jax.ShapeDtypeStruct((n_slots, d), F32),
        compiler_params=_cp(("arbitrary", "arbitrary")),
        name="moe_ffn",
    )(item_exp, item_row0, item_nsub, item_jlast, item_nzero, xs, w1, w3, w2)


def _combine_body(pos_ref, ys_hbm, tg_ref, x_ref, g_ref, b_ref, of_ref, ob_ref, buf, sem, *, tb, alpha):
    def issue(chunk, slot):
        def start(u, carry):
            for v in range(ROW_DMA_UNROLL // TOP_K_EXPERTS):
                r = u * (ROW_DMA_UNROLL // TOP_K_EXPERTS) + v
                for k in range(TOP_K_EXPERTS):
                    p = pos_ref[TOP_K_EXPERTS * (chunk * tb + r) + k]
                    _row_copy(ys_hbm, buf.at[slot, k], p, r, sem.at[slot]).start()
            return carry

        lax.fori_loop(0, tb * TOP_K_EXPERTS // ROW_DMA_UNROLL, start, 0)

    def wait(slot):
        def w(r, carry):
            for k in range(TOP_K_EXPERTS):
                _row_copy(ys_hbm, buf.at[slot, k], 0, r, sem.at[slot]).wait()
            return carry

        lax.fori_loop(0, tb, w, 0)

    slot = _prefetch_chunks(issue, wait)
    f = tg_ref[:, 0:1] * buf[slot, 0] + tg_ref[:, 1:2] * buf[slot, 1]
    y = _ln_rows(alpha * x_ref[...] + f, g_ref[...], b_ref[...])
    of_ref[...] = y
    ob_ref[...] = y.astype(BF16)


def _combine_ln(pos, ys, tg, x, g, b, alpha, tb):
    m, d = x.shape
    grid_spec = pltpu.PrefetchScalarGridSpec(
        num_scalar_prefetch=1,
        grid=(m // tb,),
        in_specs=[pl.BlockSpec(memory_space=pl.ANY),
                  pl.BlockSpec((tb, LANES), lambda i, p: (i, 0)),
                  pl.BlockSpec((tb, d), lambda i, p: (i, 0)),
                  pl.BlockSpec((1, d), lambda i, p: (0, 0)),
                  pl.BlockSpec((1, d), lambda i, p: (0, 0))],
        out_specs=[pl.BlockSpec((tb, d), lambda i, p: (i, 0)),
                   pl.BlockSpec((tb, d), lambda i, p: (i, 0))],
        scratch_shapes=[pltpu.VMEM((2, TOP_K_EXPERTS, tb, d), F32), pltpu.SemaphoreType.DMA((2,))],
    )
    assert tb * TOP_K_EXPERTS % ROW_DMA_UNROLL == 0
    return pl.pallas_call(
        functools.partial(_combine_body, tb=tb, alpha=alpha),
        grid_spec=grid_spec,
        out_shape=[jax.ShapeDtypeStruct((m, d), F32), jax.ShapeDtypeStruct((m, d), BF16)],
        compiler_params=_cp(("arbitrary",)),
        name="moe_combine_ln",
    )(pos, ys, tg, x, g, b)


def _moe_plan(top_i, n_real, n_exp, n_slots, n_items, super_rows, sub):
    a = top_i.shape[0] * TOP_K_EXPERTS
    e_flat = top_i.reshape(-1)
    tok = jnp.arange(a, dtype=I32) // TOP_K_EXPERTS
    real = tok < n_real
    onehot = jnp.logical_and(e_flat[:, None] == jnp.arange(n_exp, dtype=I32)[None, :], real[:, None]).astype(I32)
    csum = jnp.cumsum(onehot, axis=0)
    rank = jnp.take_along_axis(csum, e_flat[:, None], axis=1)[:, 0] - 1
    counts = csum[-1]
    padded = (counts + sub - 1) // sub * sub
    gstart = jnp.cumsum(padded) - padded
    pos = jnp.where(real, gstart[e_flat] + rank, 0).astype(I32)
    src = jnp.zeros((n_slots,), I32).at[jnp.where(real, pos, n_slots)].set(tok, mode="drop")
    n_super = (padded + super_rows - 1) // super_rows
    iend = jnp.cumsum(n_super)
    istart = iend - n_super
    total = iend[-1]
    wi = jnp.arange(n_items, dtype=I32)
    valid = wi < total
    e_w = jnp.minimum(jnp.searchsorted(iend, wi, side="right"), n_exp - 1).astype(I32)
    last_e = e_w[jnp.maximum(total - 1, 0)]
    e_w = jnp.where(valid, e_w, last_e)
    k_w = wi - istart[e_w]
    nsub = jnp.where(valid, jnp.minimum(super_rows, padded[e_w] - k_w * super_rows) // sub, 0).astype(I32)
    z0 = jnp.sum(padded) + (wi - total) * super_rows
    nzero = jnp.where(valid, 0, jnp.clip((n_slots - z0) // sub, 0, super_rows // sub)).astype(I32)
    row0 = jnp.where(valid, gstart[e_w] + k_w * super_rows, jnp.where(nzero > 0, z0, 0)).astype(I32)
    return pos, src, e_w, row0, nsub, valid.astype(I32), nzero


def _rel_bucket(dist):
    n = jnp.maximum(dist, 0)
    max_exact = REL_BUCKETS // 2
    nf = jnp.maximum(n, 1).astype(F32)
    large = max_exact + (jnp.log(nf / max_exact) / math.log(REL_MAX_DIST / max_exact)
                         * (REL_BUCKETS - max_exact)).astype(I32)
    large = jnp.minimum(large, REL_BUCKETS - 1)
    return jnp.where(n < max_exact, n, large)


def _shifted_bias(rel_bias, dist):
    t = rel_bias[_rel_bucket(dist)] - rel_bias[REL_BUCKETS - 1]
    t = jnp.where((dist >= 0)[..., None], t, 0.0)
    return jnp.moveaxis(t, -1, 0)


def _pick_tile(m_rows, target):
    return max(t for t in range(16, target + 1, 16) if m_rows % t == 0)


def _pad_rows(x, rows):
    return jnp.pad(x, ((0, rows - x.shape[0]), (0, 0)))


def kernel(x_prompt, x_sample, cache_k, cache_v, cache_ik, state_conv, page_table, w_in, conv_w, conv_b,
           conv_ln_g, conv_ln_b, sgu_ln_g, sgu_ln_b, sgu_w, sgu_b, w_pa, w_pb, w_pc, w_out, ln1_g, ln1_b,
           ln2_g, ln2_b, rel_bias, ffn_w1, ffn_w3, ffn_w2, moe_router, moe_router_b, moe_w1, moe_w3, moe_w2):
    batch, seq, d_model = x_prompt.shape
    db, n_tok, _ = x_sample.shape
    depth = w_in.shape[0]
    n_pool = cache_k.shape[1]
    n_exp = moe_router.shape[2]
    mp, ms = batch * seq, db * n_tok
    assert ms <= SAMPLE_TILE and mp % SAMPLE_TILE == 0
    m_all = mp + SAMPLE_TILE
    tm = m_all // N_ROW_TILES
    assert tm * N_ROW_TILES == m_all and tm % 16 == 0
    alpha = (2 * depth) ** 0.25
    tq = min(256, seq)

    xf = jnp.concatenate([x_prompt.reshape(mp, d_model), x_sample.reshape(ms, d_model),
                          jnp.zeros((m_all - mp - ms, d_model), F32)], axis=0)
    xb = xf.astype(BF16)
    w_in_t = jnp.swapaxes(w_in, 1, 2)
    ha_rows = ([W_Q_ROW0 + HA_TILE * t for t in range(QKV_W // HA_TILE)]
               + [W_GLU_ROW0 + HA_TILE * t for t in range(2 * CONV_DIM // HA_TILE)]
               + [W_UV_ROW0 + HA_TILE * t for t in range(2 * SGU_DIM // HA_TILE)]
               + [W_GATE_ROW0 + HA_TILE * t for t in range(N_BRANCH * d_model // HA_TILE)]
               + [W_IQ_ROW0 + HA_TILE * t for t in range(IQ_W // HA_TILE)])
    w_pa_b, w_pb_b, w_pc_b, w_out_b = (w.astype(BF16) for w in (w_pa, w_pb, w_pc, w_out))
    ffn_w1_b, ffn_w3_b, ffn_w2_b = (w.astype(BF16) for w in (ffn_w1, ffn_w3, ffn_w2))

    kk = jnp.arange(2 * tq, dtype=I32)
    d_wrap = jnp.where(kk < tq, -kk, 2 * tq - kk)

    def toeplitz(offset):
        v = _shifted_bias(rel_bias, d_wrap + offset)
        return jnp.tile(v, (1, tq))[:, :tq * (2 * tq - 1)].reshape(N_HEADS, tq, 2 * tq - 1)[:, :, :tq]

    bias_tiles = jnp.stack([toeplitz(0), toeplitz(tq)])
    tok = jnp.arange(n_tok, dtype=I32)
    lane = jnp.arange(LANES, dtype=I32)
    d_last = PAGE_SIZE + tok[:, None] - lane[None, :]
    bias_last = _shifted_bias(rel_bias, d_last).reshape(N_HEADS * n_tok, LANES)
    bias_new = _shifted_bias(rel_bias, tok[:, None] - lane[None, :]).reshape(N_HEADS * n_tok, LANES)

    ck = cache_k.reshape(depth, n_pool, PAGE_SIZE * N_KV_HEADS, HEAD_DIM)
    cv = cache_v.reshape(depth, n_pool, PAGE_SIZE * N_KV_HEADS, HEAD_DIM)
    cik_t = jnp.swapaxes(cache_ik, 2, 3)
    n_pages = page_table.shape[1]
    idx_group = math.gcd(n_pages, IDX_PAGE_GROUP)
    attn_group = math.gcd(n_pages, ATTN_PAGE_GROUP)
    n_slots = _round_up(TOP_K_EXPERTS * (mp + ms) + n_exp * (MOE_SUB - 1), MOE_SUB)
    n_items = -(-n_slots // MOE_SUPER) + n_exp

    outs = {k: [] for k in ("k", "v", "ik", "conv_p", "conv_s", "sgu_s")}
    for l in range(depth):
        ha = _in_proj(xb, w_in_t, l, ha_rows, tm, HA_TILE)
        kv, ikw = _kv_proj(xb, w_in_t, l, tm)

        a_p = _attn_prompt(ha, kv, ikw, bias_tiles, batch, seq, tq)
        conv_w_rep = jnp.broadcast_to(conv_w[l].reshape(CONV_WIDTH, CONV_CHUNKS, 1, LANES).transpose(1, 0, 2, 3),
                                      (CONV_CHUNKS, CONV_WIDTH, F32_SUBLANES, LANES))
        c_p, conv_state_p = _conv_prompt(ha, conv_w_rep, conv_b[l][None], conv_ln_g[l][None], conv_ln_b[l][None],
                                         batch, seq, min(256, seq))
        s_p = _sgu_prompt(ha, sgu_ln_g[l][None], sgu_ln_b[l][None], sgu_w[l], sgu_b[l].T, batch, seq,
                          min(512, seq))

        has, kvs, ikws = ha[mp:mp + ms], kv[mp:mp + ms], ikw[mp:mp + ms]
        iq_rows = has[:, HA_IQ:HA_IQ + IQ_W].reshape(db, n_tok * IDX_HEADS, IDX_DIM)
        iw_rows = (ikws[:, IKW_IW:IKW_IW + IDX_HEADS] * (IDX_DIM ** -0.5 * IDX_HEADS ** -0.5)
                   ).reshape(db, n_tok * IDX_HEADS, 1)
        iw_rows = jnp.broadcast_to(iw_rows, (db, n_tok * IDX_HEADS, LANES))
        scores = _idx_sample(page_table, iq_rows, iw_rows, cik_t, l, n_tok, idx_group)
        pad_rows = LANES - n_tok
        ik_new = jnp.pad(ikws[:, :IDX_DIM].reshape(db, n_tok, IDX_DIM), ((0, 0), (0, pad_rows), (0, 0)))
        scores_new = _idx_new(iq_rows, iw_rows, ik_new, n_tok)
        q_rows = has[:, HA_Q:HA_Q + QKV_W].reshape(db, n_tok, N_HEADS, HEAD_DIM).transpose(0, 2, 1, 3)
        q_rows = q_rows.reshape(db, N_HEADS * n_tok, HEAD_DIM)
        k_new = jnp.pad(kvs[:, KV_K:KV_K + KV_W].reshape(db, n_tok, KV_W), ((0, 0), (0, pad_rows), (0, 0)))
        v_new = jnp.pad(kvs[:, KV_V:KV_V + KV_W].reshape(db, n_tok, KV_W), ((0, 0), (0, pad_rows), (0, 0)))
        a_s = _attn_sample(page_table, scores, scores_new, q_rows, k_new, v_new, bias_last, bias_new,
                           ck, cv, l, n_tok, attn_group)
        a_s = a_s.reshape(db, N_HEADS, n_tok, HEAD_DIM).transpose(0, 2, 1, 3).reshape(ms, QKV_W)
        gd = SGU_DIM // SGU_GROUPS
        wv = jnp.repeat(sgu_w[l][:, :n_tok, :n_tok].transpose(1, 2, 0), gd, axis=-1)
        bv = jnp.repeat(sgu_b[l][:, :n_tok].T, gd, axis=-1)
        c_s, conv_state_s, s_s, vn_s = _mix_sample(
            has[:, HA_GLU:HA_GLU + 2 * CONV_DIM], has[:, HA_UV:HA_UV + 2 * SGU_DIM], state_conv[l],
            conv_w[l], conv_b[l][None], conv_ln_g[l][None], conv_ln_b[l][None],
            sgu_ln_g[l][None], sgu_ln_b[l][None], wv, bv, db, n_tok)

        sample_rows = tuple(_pad_rows(t.astype(BF16), SAMPLE_TILE) for t in (a_s, c_s, s_s))
        merged = _merge((a_p, c_p, s_p), sample_rows, ha, w_pa_b, w_pb_b, w_pc_b, l, SAMPLE_TILE)
        x1f, x1b = _mm_ln(merged, w_out_b, l, xf, ln1_g[l][None], ln1_b[l][None], alpha, _pick_tile(m_all, 384))

        j = l // 2
        if l % 2 == 0:
            h = _swiglu_up(x1b, ffn_w1_b, ffn_w3_b, j, tm, 512)
            xf, xb = _mm_ln(h, ffn_w2_b, j, x1f, ln2_g[l][None], ln2_b[l][None], alpha, _pick_tile(m_all, 384))
        else:
            w_r = jnp.pad(moe_router[j], ((0, 0), (0, LANES - n_exp)))
            b_r = jnp.pad(moe_router_b[j], (0, LANES - n_exp))[None]
            ti, tg = _router(x1f, w_r, b_r, n_exp, _pick_tile(m_all, 528))
            pos, src, it_e, it_r0, it_ns, it_valid, it_nz = _moe_plan(ti[:, :TOP_K_EXPERTS], mp + ms, n_exp,
                                                                      n_slots, n_items, MOE_SUPER, MOE_SUB)
            xs = _dispatch(src, x1f, n_slots, MOE_SUB)
            ys = _moe_ffn(it_e, it_r0, it_ns, it_valid, it_nz, xs, moe_w1, moe_w3, moe_w2, j, n_items,
                          MOE_SUPER, MOE_SUB, MOE_TF)
            tb = COMBINE_TB if m_all % COMBINE_TB == 0 else 16
            xf, xb = _combine_ln(pos, ys, tg, x1f, ln2_g[l][None], ln2_b[l][None], alpha, tb)

        outs["k"].append(kv[:, KV_K:KV_K + KV_W])
        outs["v"].append(kv[:, KV_V:KV_V + KV_W])
        outs["ik"].append(ikw[:, :IDX_DIM])
        outs["conv_p"].append(conv_state_p)
        outs["conv_s"].append(conv_state_s)
        outs["sgu_s"].append(vn_s.reshape(db, n_tok, SGU_DIM))

    def split(name, width_shape):
        st = jnp.stack(outs[name])
        p = st[:, :mp].reshape((depth, batch, seq) + width_shape)
        s = st[:, mp:mp + ms].reshape((depth, db, n_tok) + width_shape)
        return p, s

    k_p, k_s = split("k", (N_KV_HEADS, HEAD_DIM))
    v_p, v_s = split("v", (N_KV_HEADS, HEAD_DIM))
    ik_p, ik_s = split("ik", (IDX_DIM,))
    y_prompt = xf[:mp].reshape(batch, seq, d_model)
    y_sample = xf[mp:mp + ms].reshape(db, n_tok, d_model)
    return (y_prompt, y_sample, k_p, v_p, ik_p, jnp.stack(outs["conv_p"]), k_s, v_s, ik_s,
            jnp.stack(outs["conv_s"]), jnp.stack(outs["sgu_s"]))
```

```python
import functools
import math

import jax
import jax.numpy as jnp
import numpy as np
from jax import lax
from jax.experimental import pallas as pl
from jax.experimental.pallas import tpu as pltpu

F32 = jnp.float32
BF16 = jnp.bfloat16
I32 = jnp.int32

N_HEADS = 16
HEAD_DIM = 128
N_KV_HEADS = 4
KV_REP = N_HEADS // N_KV_HEADS
IDX_HEADS = 16
IDX_DIM = 64
TOPK_MAX = 256
PAGE_SIZE = 128
CONV_DIM = 1024
CONV_WIDTH = 31
SGU_DIM = 1024
SGU_GROUPS = 8
CHUNK = 128
N_BRANCH = 3
TOP_K_EXPERTS = 2
REL_BUCKETS = 32
REL_MAX_DIST = 128
LN_EPS = 1e-5

V7X_VMEM_BYTES = 64 * 2**20
VMEM_LIMIT = V7X_VMEM_BYTES - 8 * 2**20
LANES = 128
F32_SUBLANES = 8
INT_MIN = -(2**31)
MASK_NEG = -1e30

QKV_W = N_HEADS * HEAD_DIM
KV_W = N_KV_HEADS * HEAD_DIM
IQ_W = IDX_HEADS * IDX_DIM
HA_Q, HA_GLU, HA_UV, HA_GATE, HA_IQ = 0, 2048, 4096, 6144, 12288
HA_TILE = 1024
W_Q_ROW0 = 0
W_KV_ROW0 = W_Q_ROW0 + QKV_W
W_IQ_ROW0 = W_KV_ROW0 + 2 * KV_W
W_IK_ROW0 = W_IQ_ROW0 + IQ_W
W_GLU_ROW0 = W_IK_ROW0 + IDX_DIM + IDX_HEADS
W_UV_ROW0 = W_GLU_ROW0 + 2 * CONV_DIM
W_GATE_ROW0 = W_UV_ROW0 + 2 * SGU_DIM
KV_K, KV_V = 0, KV_W
IKW_IW = IDX_DIM

N_ROW_TILES = 8
SAMPLE_TILE = 256
MOE_SUB = 256
MOE_SUPER = 2048
MOE_TF = 256
MOE_WIDE = 2
COMBINE_TB = 128
IDX_PAGE_GROUP = 32
ATTN_PAGE_GROUP = 32


def _cp(sem, vmem=VMEM_LIMIT):
    return pltpu.CompilerParams(dimension_semantics=sem, vmem_limit_bytes=vmem)


def _round_up(x, m):
    return (x + m - 1) // m * m


def _ln_rows(z, g, b):
    mu = jnp.mean(z, axis=-1, keepdims=True)
    d = z - mu
    var = jnp.mean(d * d, axis=-1, keepdims=True)
    return d * lax.rsqrt(var + LN_EPS) * g + b


def _sigmoid(x):
    return 1.0 / (1.0 + jnp.exp(-x))


def _in_proj_body(off_ref, x_ref, w_ref, o_ref, wt_ref):
    @pl.when(pl.program_id(1) == 0)
    def _():
        wt_ref[...] = w_ref[0].T.astype(BF16)

    o_ref[...] = jnp.dot(x_ref[...], wt_ref[...], preferred_element_type=F32).astype(o_ref.dtype)


def _in_proj(x, w_t, layer, row_offsets, tm, tn):
    m, k = x.shape
    n_tiles = len(row_offsets)
    grid_spec = pltpu.PrefetchScalarGridSpec(
        num_scalar_prefetch=1,
        grid=(n_tiles, m // tm),
        in_specs=[pl.BlockSpec((tm, k), lambda j, i, off: (i, 0)),
                  pl.BlockSpec((pl.Element(1), pl.Element(tn), pl.Element(k)),
                               lambda j, i, off: (layer, pl.multiple_of(off[j], F32_SUBLANES), 0))],
        out_specs=pl.BlockSpec((tm, tn), lambda j, i, off: (i, j)),
        scratch_shapes=[pltpu.VMEM((k, tn), BF16)],
    )
    return pl.pallas_call(
        _in_proj_body,
        grid_spec=grid_spec,
        out_shape=jax.ShapeDtypeStruct((m, n_tiles * tn), BF16),
        compiler_params=_cp(("arbitrary", "arbitrary")),
        name="in_proj",
    )(jnp.asarray(row_offsets, I32), x, w_t)


def _kv_proj_body(x_ref, wkv_ref, wik_ref, kv_ref, ik_ref, wkv_t, wik_t):
    @pl.when(pl.program_id(0) == 0)
    def _():
        wkv_t[...] = wkv_ref[...].T.astype(BF16)
        wik_t[...] = wik_ref[...].T.astype(BF16)

    x = x_ref[...]
    kv_ref[...] = jnp.dot(x, wkv_t[...], preferred_element_type=F32)
    ik_ref[...] = jnp.dot(x, wik_t[...], preferred_element_type=F32)


def _kv_proj(x, w_t, layer, tm):
    m, k = x.shape
    assert W_KV_ROW0 % (2 * KV_W) == 0 and W_IK_ROW0 % LANES == 0
    return pl.pallas_call(
        _kv_proj_body,
        grid=(m // tm,),
        in_specs=[pl.BlockSpec((tm, k), lambda i: (i, 0)),
                  pl.BlockSpec((None, 2 * KV_W, k), lambda i: (layer, W_KV_ROW0 // (2 * KV_W), 0)),
                  pl.BlockSpec((None, LANES, k), lambda i: (layer, W_IK_ROW0 // LANES, 0))],
        out_specs=[pl.BlockSpec((tm, 2 * KV_W), lambda i: (i, 0)),
                   pl.BlockSpec((tm, LANES), lambda i: (i, 0))],
        out_shape=[jax.ShapeDtypeStruct((m, 2 * KV_W), F32), jax.ShapeDtypeStruct((m, LANES), F32)],
        scratch_shapes=[pltpu.VMEM((k, 2 * KV_W), BF16), pltpu.VMEM((k, LANES), BF16)],
        compiler_params=_cp(("arbitrary",)),
        name="kv_proj",
    )(x, w_t, w_t)


def _row_halves(rows):
    half = rows // 2
    return (slice(0, half), slice(half, rows)) if half % 16 == 0 else (slice(0, rows),)


def _mm_ln_body(x_ref, w_ref, r_ref, g_ref, b_ref, of_ref, ob_ref, *, alpha):
    for rs in _row_halves(x_ref.shape[0]):
        z = jnp.dot(x_ref[rs, :], w_ref[...], preferred_element_type=F32) + alpha * r_ref[rs, :]
        y = _ln_rows(z, g_ref[...], b_ref[...])
        of_ref[rs, :] = y
        ob_ref[rs, :] = y.astype(BF16)


def _mm_ln(x, w, layer, resid, g, b, alpha, tm):
    m, k = x.shape
    n = w.shape[2]
    row = lambda i: (i, 0)
    return pl.pallas_call(
        functools.partial(_mm_ln_body, alpha=alpha),
        grid=(m // tm,),
        in_specs=[pl.BlockSpec((tm, k), row),
                  pl.BlockSpec((None, k, n), lambda i: (layer, 0, 0), pipeline_mode=pl.Buffered(1)),
                  pl.BlockSpec((tm, n), row),
                  pl.BlockSpec((1, n), lambda i: (0, 0)),
                  pl.BlockSpec((1, n), lambda i: (0, 0))],
        out_specs=[pl.BlockSpec((tm, n), row), pl.BlockSpec((tm, n), row)],
        out_shape=[jax.ShapeDtypeStruct((m, n), F32), jax.ShapeDtypeStruct((m, n), BF16)],
        compiler_params=_cp(("arbitrary",)),
        name="proj_ln",
    )(x, w, resid, g, b)


def _swiglu_up_body(x_ref, w1_ref, w3_ref, o_ref, w1b, w3b):
    @pl.when(pl.program_id(1) == 0)
    def _():
        w1b[...] = w1_ref[...].astype(BF16)
        w3b[...] = w3_ref[...].astype(BF16)

    for rs in _row_halves(x_ref.shape[0]):
        x = x_ref[rs, :]
        a = jnp.dot(x, w1b[...], preferred_element_type=F32)
        c = jnp.dot(x, w3b[...], preferred_element_type=F32)
        o_ref[rs, :] = (a * _sigmoid(a) * c).astype(o_ref.dtype)


def _swiglu_up(x, w1, w3, layer, tm, tf):
    m, k = x.shape
    f = w1.shape[2]
    return pl.pallas_call(
        _swiglu_up_body,
        grid=(f // tf, m // tm),
        in_specs=[pl.BlockSpec((tm, k), lambda j, i: (i, 0)),
                  pl.BlockSpec((None, k, tf), lambda j, i: (layer, 0, j)),
                  pl.BlockSpec((None, k, tf), lambda j, i: (layer, 0, j))],
        out_specs=pl.BlockSpec((tm, tf), lambda j, i: (i, j)),
        out_shape=jax.ShapeDtypeStruct((m, f), BF16),
        scratch_shapes=[pltpu.VMEM((k, tf), BF16), pltpu.VMEM((k, tf), BF16)],
        compiler_params=_cp(("arbitrary", "arbitrary")),
        name="ffn_up",
    )(x, w1, w3)


def _merge_body(ap_ref, cp_ref, sp_ref, as_ref, cs_ref, ss_ref, g_ref, wa_ref, wb_ref, wc_ref, o_ref, *, n_prompt):
    i = pl.program_id(0)
    d = o_ref.shape[1]

    def merge(a_ref, c_ref, s_ref):
        for rs in _row_halves(o_ref.shape[0]):
            pa = jnp.dot(a_ref[rs, :], wa_ref[...], preferred_element_type=F32)
            pb = jnp.dot(c_ref[rs, :], wb_ref[...], preferred_element_type=F32)
            pc = jnp.dot(s_ref[rs, :], wc_ref[...], preferred_element_type=F32)
            o = (_sigmoid(g_ref[rs, :d].astype(F32)) * pa + _sigmoid(g_ref[rs, d:2 * d].astype(F32)) * pb
                 + _sigmoid(g_ref[rs, 2 * d:].astype(F32)) * pc)
            o_ref[rs, :] = o.astype(o_ref.dtype)

    @pl.when(i < n_prompt)
    def _():
        merge(ap_ref, cp_ref, sp_ref)

    @pl.when(i >= n_prompt)
    def _():
        merge(as_ref, cs_ref, ss_ref)


def _merge(prompt, sample, ha, w_pa, w_pb, w_pc, layer, tm):
    m = ha.shape[0]
    d = w_pa.shape[2]
    n_prompt = prompt[0].shape[0] // tm
    assert prompt[0].shape[0] % tm == 0 and sample[0].shape[0] == tm and m == (n_prompt + 1) * tm
    assert HA_GATE % (N_BRANCH * d) == 0
    p_row = lambda i: (jnp.minimum(i, n_prompt - 1), 0)
    fixed = lambda i: (0, 0)
    return pl.pallas_call(
        functools.partial(_merge_body, n_prompt=n_prompt),
        grid=(n_prompt + 1,),
        in_specs=[pl.BlockSpec((tm, t.shape[1]), p_row) for t in prompt]
        + [pl.BlockSpec((tm, t.shape[1]), fixed) for t in sample]
        + [pl.BlockSpec((tm, N_BRANCH * d), lambda i: (i, HA_GATE // (N_BRANCH * d)))]
        + [pl.BlockSpec((None,) + w.shape[1:], lambda i: (layer, 0, 0), pipeline_mode=pl.Buffered(1))
           for w in (w_pa, w_pb, w_pc)],
        out_specs=pl.BlockSpec((tm, d), lambda i: (i, 0)),
        out_shape=jax.ShapeDtypeStruct((m, d), BF16),
        compiler_params=_cp(("arbitrary",)),
        name="branch_merge",
    )(*prompt, *sample, ha, w_pa, w_pb, w_pc)


def _sortable_key(score):
    bits = pltpu.bitcast(score, I32)
    return bits ^ ((bits >> 31) & jnp.int32(0x7FFFFFFF))


def _kth_largest_key(count_ge, rows, k):
    def step(it, t):
        cand = t + jnp.left_shift(jnp.int32(1), 31 - it)
        return jnp.where(count_ge(cand) >= k, cand, t)

    return lax.fori_loop(0, 32, step, jnp.full((rows, 1), INT_MIN, I32))


RADIX4_STEPS = 16
RADIX4_FIELD_BITS = 5
RADIX4_WIDE = 4096.0


def _kth_largest_key_radix4(count3, rows, k, n_valid):
    kf = float(k)

    def cond(state):
        it, _, _, pending = state
        return jnp.logical_and(it < RADIX4_STEPS, pending > 0.0)

    def body(state):
        it, t, cnt, _ = state
        one = jnp.left_shift(jnp.int32(1), 30 - 2 * it)
        c1 = t + one
        c2 = c1 + one
        c3 = c2 + one
        n1, n2, n3 = count3(c1, c2, c3)
        t = jnp.where(n3 >= kf, c3, jnp.where(n2 >= kf, c2, jnp.where(n1 >= kf, c1, t)))
        cnt = jnp.where(n3 >= kf, n3, jnp.where(n2 >= kf, n2, jnp.where(n1 >= kf, n1, cnt)))
        settled = jnp.logical_or(cnt == kf, n_valid <= kf)
        return it + 1, t, cnt, jnp.max(jnp.where(settled, 0.0, 1.0))

    init = (jnp.int32(0), jnp.full((rows, 1), INT_MIN, I32), jnp.full((rows, 1), -1.0, F32), jnp.float32(1.0))
    return lax.while_loop(cond, body, init)[1]


def _lane_tile(x, width):
    return x if width == LANES else jnp.concatenate([x] * (width // LANES), axis=1)


def _attn_prompt_body(q_ref, iq_ref, iw_ref, kv_ref, ik_ref, bias_ref, o_ref,
                      keys_ref, mb_ref, iwb_ref, m_ref, acc_ref, *, tq, k_sel):
    i = pl.program_id(1)
    nkb = i + 1
    tk = tq
    row = lax.broadcasted_iota(I32, (tq, tk), 0)
    col = lax.broadcasted_iota(I32, (tq, tk), 1)
    low_half = lax.broadcasted_iota(I32, (tk, LANES), 1) < IDX_DIM
    iw = iw_ref[...] * (IDX_DIM ** -0.5 * IDX_HEADS ** -0.5)
    for h in range(IDX_HEADS):
        iwb_ref[h] = jnp.broadcast_to(iw[:, IKW_IW + h:IKW_IW + h + 1], (tq, LANES))
    iq_pairs = [iq_ref[:, p * LANES:(p + 1) * LANES] for p in range(IDX_HEADS // 2)]

    def idx_block(kb, carry):
        ik_lo = jnp.where(low_half, ik_ref[pl.ds(pl.multiple_of(kb * tk, tk), tk), :], 0.0)
        ik_sides = (ik_lo.astype(BF16), pltpu.roll(ik_lo, IDX_DIM, axis=1).astype(BF16))
        acc = jnp.zeros((tq, tk), F32)
        for h in range(IDX_HEADS):
            s = lax.dot_general(iq_pairs[h // 2], ik_sides[h % 2], (((1,), (1,)), ((), ())),
                                preferred_element_type=F32)
            acc = acc + jnp.maximum(s, 0.0) * _lane_tile(iwb_ref[h], tk)
        causal = (col + kb * tk) <= (row + i * tq)
        keys_ref[kb] = jnp.where(causal, _sortable_key(acc), jnp.int32(INT_MIN))
        return carry

    lax.fori_loop(0, nkb, idx_block, 0)

    fb = RADIX4_FIELD_BITS
    field_mask = (1 << fb) - 1
    inc1, inc2, inc3 = 1, 1 + (1 << fb), 1 + (1 << fb) + (1 << 2 * fb)

    def count3(c1, c2, c3):
        b1, b2, b3 = (jnp.broadcast_to(c, (tq, LANES)) for c in (c1, c2, c3))

        def cb(kb, acc):
            keys = keys_ref[kb]
            for t in range(tk // LANES):
                kk = keys[:, t * LANES:(t + 1) * LANES]
                acc = acc + jnp.where(kk >= b3, inc3, jnp.where(kk >= b2, inc2, jnp.where(kk >= b1, inc1, 0)))
            return acc

        acc = lax.fori_loop(0, nkb, cb, jnp.zeros((tq, LANES), I32))
        n3 = (acc >> (2 * fb)).astype(F32).sum(axis=-1, keepdims=True)
        low = ((acc & field_mask).astype(F32) + ((acc >> fb) & field_mask).astype(F32) * RADIX4_WIDE)
        low = low.sum(axis=-1, keepdims=True)
        n2 = jnp.floor(low * (1.0 / RADIX4_WIDE))
        return low - RADIX4_WIDE * n2, n2, n3

    n_valid = (lax.broadcasted_iota(I32, (tq, 1), 0) + (i * tq + 1)).astype(F32)
    thr = _kth_largest_key_radix4(count3, tq, k_sel, n_valid)
    thr = jnp.maximum(thr, jnp.int32(INT_MIN + 1))
    thr_b = _lane_tile(jnp.broadcast_to(thr, (tq, LANES)), tk)

    def mask_block(kb, carry):
        mb_ref[kb] = jnp.where(keys_ref[kb] >= thr_b, 0.0, MASK_NEG)
        return carry

    lax.fori_loop(0, nkb, mask_block, 0)

    scale = HEAD_DIM ** -0.5
    for g in range(N_KV_HEADS):
        qg = jnp.concatenate(
            [q_ref[:, (KV_REP * g + r) * HEAD_DIM:(KV_REP * g + r + 1) * HEAD_DIM] for r in range(KV_REP)], axis=0)
        m_ref[...] = jnp.full(m_ref.shape, MASK_NEG, F32)
        acc_ref[...] = jnp.zeros(acc_ref.shape, F32)

        def block(kb, bias_idx, g=g, qg=qg):
            r0 = pl.multiple_of(kb * tk, tk)
            kblk = kv_ref[pl.ds(r0, tk), KV_K + g * HEAD_DIM:KV_K + (g + 1) * HEAD_DIM].astype(BF16)
            vblk = kv_ref[pl.ds(r0, tk), KV_V + g * HEAD_DIM:KV_V + (g + 1) * HEAD_DIM].astype(BF16)
            s = lax.dot_general(qg, kblk, (((1,), (1,)), ((), ())), preferred_element_type=F32) * scale
            s = s.reshape(KV_REP, tq, tk) + mb_ref[kb][None]
            if bias_idx is not None:
                s = s + bias_ref[bias_idx, KV_REP * g:KV_REP * (g + 1)]
            s = s.reshape(KV_REP * tq, tk)
            m_old = m_ref[...]
            m_new = jnp.maximum(m_old, s.max(axis=-1, keepdims=True))
            p = jnp.exp(s - _lane_tile(m_new, tk))
            alpha = jnp.exp(m_old - m_new)
            v_ones = jnp.concatenate([vblk, jnp.ones((tk, LANES), BF16)], axis=1)
            acc_ref[...] = (_lane_tile(alpha, HEAD_DIM + LANES) * acc_ref[...]
                            + jnp.dot(p.astype(BF16), v_ones, preferred_element_type=F32))
            m_ref[...] = m_new

        n_far = jnp.maximum(i - 1, 0)

        def far_pair(t, carry):
            block(2 * t, None)
            block(2 * t + 1, None)
            return carry

        lax.fori_loop(0, n_far // 2, far_pair, 0)

        @pl.when(n_far % 2 == 1)
        def _():
            block(n_far - 1, None)

        @pl.when(i >= 1)
        def _():
            block(i - 1, 1)
            block(i, 0)

        @pl.when(i == 0)
        def _():
            block(i, 0)
        out = acc_ref[:, :HEAD_DIM] / acc_ref[:, HEAD_DIM:]
        for r in range(KV_REP):
            h = KV_REP * g + r
            o_ref[:, h * HEAD_DIM:(h + 1) * HEAD_DIM] = out[r * tq:(r + 1) * tq].astype(o_ref.dtype)


def _attn_prompt(ha, kv, ikw, bias_tiles, batch, seq, tq):
    nq = seq // tq
    k_sel = min(TOPK_MAX, seq // 4)
    assert seq // LANES < (1 << RADIX4_FIELD_BITS)
    body = functools.partial(_attn_prompt_body, tq=tq, k_sel=k_sel)
    return pl.pallas_call(
        body,
        grid=(batch, nq),
        in_specs=[pl.BlockSpec((tq, QKV_W), lambda b, i: (b * nq + i, HA_Q // QKV_W)),
                  pl.BlockSpec((tq, IQ_W), lambda b, i: (b * nq + i, HA_IQ // IQ_W)),
                  pl.BlockSpec((tq, LANES), lambda b, i: (b * nq + i, 0)),
                  pl.BlockSpec((seq, 2 * KV_W), lambda b, i: (b, 0), pipeline_mode=pl.Buffered(1)),
                  pl.BlockSpec((seq, LANES), lambda b, i: (b, 0), pipeline_mode=pl.Buffered(1)),
                  pl.BlockSpec((2, N_HEADS, tq, tq), lambda b, i: (0, 0, 0, 0), pipeline_mode=pl.Buffered(1))],
        out_specs=pl.BlockSpec((tq, QKV_W), lambda b, i: (b * nq + i, 0)),
        out_shape=jax.ShapeDtypeStruct((batch * seq, QKV_W), BF16),
        scratch_shapes=[pltpu.VMEM((nq, tq, tq), I32),
                        pltpu.VMEM((nq, tq, tq), F32),
                        pltpu.VMEM((IDX_HEADS, tq, LANES), F32),
                        pltpu.VMEM((KV_REP * tq, LANES), F32),
                        pltpu.VMEM((KV_REP * tq, HEAD_DIM + LANES), F32)],
        compiler_params=_cp(("arbitrary", "arbitrary")),
        name="attn_prompt",
    )(ha, ha, ikw, kv, ikw, bias_tiles)


def _page_spec(block, layer, group, slot):
    return pl.BlockSpec((None, None) + block, lambda b, p, pt: (layer, pt[b, p * group + slot], 0, 0))


def _idx_sample_body(pt_ref, iq_ref, iw_ref, *refs, n_tok, group):
    pages, o_ref = refs[:group], refs[group]
    ikt = jnp.concatenate([r[...].astype(BF16) for r in pages], axis=1)
    s = jnp.dot(iq_ref[0], ikt, preferred_element_type=F32)
    r = jnp.maximum(s, 0.0) * _lane_tile(iw_ref[0], group * PAGE_SIZE)
    tok_scores = r.reshape(n_tok, IDX_HEADS, group * PAGE_SIZE).sum(axis=1)
    for t in range(group):
        o_ref[0, t] = tok_scores[:, t * PAGE_SIZE:(t + 1) * PAGE_SIZE]


def _idx_sample(page_table, iq_rows, iw_rows, cache_ik_t, layer, n_tok, group):
    db, n_pages = page_table.shape
    rows = n_tok * IDX_HEADS
    body = functools.partial(_idx_sample_body, n_tok=n_tok, group=group)
    grid_spec = pltpu.PrefetchScalarGridSpec(
        num_scalar_prefetch=1,
        grid=(db, n_pages // group),
        in_specs=[pl.BlockSpec((1, rows, IDX_DIM), lambda b, p, pt: (b, 0, 0)),
                  pl.BlockSpec((1, rows, LANES), lambda b, p, pt: (b, 0, 0))]
        + [_page_spec((IDX_DIM, PAGE_SIZE), layer, group, t) for t in range(group)],
        out_specs=pl.BlockSpec((1, group, n_tok, PAGE_SIZE), lambda b, p, pt: (b, p, 0, 0)),
    )
    return pl.pallas_call(
        body,
        grid_spec=grid_spec,
        out_shape=jax.ShapeDtypeStruct((db, n_pages, n_tok, PAGE_SIZE), F32),
        compiler_params=_cp(("arbitrary", "arbitrary")),
        name="idx_sample",
    )(page_table, iq_rows, iw_rows, *([cache_ik_t] * group))


def _attn_sample_body(pt_ref, sc_ref, scn_ref, q_ref, kn_ref, vn_ref, bl_ref, bn_ref, *refs,
                      n_tok, n_pages, k_sel, group):
    kp_refs, vp_refs = refs[:group], refs[group:2 * group]
    o_ref, thr_ref, m_ref, l_ref, acc_ref = refs[2 * group:]
    p = pl.program_id(1)
    n_steps = n_pages // group
    grp = KV_REP * n_tok
    width = group * PAGE_SIZE
    scale = HEAD_DIM ** -0.5
    r_i = lax.broadcasted_iota(I32, (n_tok, LANES), 0)
    c_i = lax.broadcasted_iota(I32, (n_tok, LANES), 1)
    keys_new = jnp.where(c_i <= r_i, _sortable_key(scn_ref[0]), jnp.int32(INT_MIN))

    @pl.when(p == 0)
    def _():
        keys = _sortable_key(sc_ref[0])

        def count_ge(cand):
            cand_b = jnp.broadcast_to(cand, (n_tok, LANES))
            c = jnp.where(keys >= cand_b[None], 1.0, 0.0).sum(axis=0) + jnp.where(keys_new >= cand_b, 1.0, 0.0)
            return c.sum(axis=-1, keepdims=True)

        thr = _kth_largest_key(count_ge, n_tok, float(k_sel))
        thr_ref[...] = jnp.broadcast_to(jnp.maximum(thr, jnp.int32(INT_MIN + 1)), (n_tok, LANES))
        m_ref[...] = jnp.full(m_ref.shape, MASK_NEG, F32)
        l_ref[...] = jnp.zeros(l_ref.shape, F32)
        acc_ref[...] = jnp.zeros(acc_ref.shape, F32)

    thr_b = thr_ref[...]

    def update(g, s, vblk):
        sl = slice(g * grp, (g + 1) * grp)
        m_old = m_ref[sl, :]
        m_new = jnp.maximum(m_old, s.max(axis=-1, keepdims=True))
        pr = jnp.exp(s - _lane_tile(m_new, s.shape[1]))
        alpha = jnp.exp(m_old - m_new)
        l_ref[sl, :] = alpha * l_ref[sl, :] + pr.sum(axis=-1, keepdims=True)
        acc_ref[sl, :] = alpha * acc_ref[sl, :] + jnp.dot(pr.astype(BF16), vblk, preferred_element_type=F32)
        m_ref[sl, :] = m_new

    def group_rows(mask_tok):
        return jnp.concatenate([mask_tok] * KV_REP, axis=0)

    base = p * group
    mb = jnp.concatenate(
        [jnp.where(_sortable_key(sc_ref[0, base + t]) >= thr_b, 0.0, MASK_NEG) for t in range(group)], axis=1)
    mb_g = group_rows(mb)
    is_last = p == n_steps - 1
    zeros_head = jnp.zeros((grp, width - PAGE_SIZE), F32)
    for g in range(N_KV_HEADS):
        qg = q_ref[0, g * grp:(g + 1) * grp, :]
        kg = jnp.concatenate([r[pl.ds(g, PAGE_SIZE, stride=N_KV_HEADS), :].astype(BF16) for r in kp_refs], axis=0)
        vg = jnp.concatenate([r[pl.ds(g, PAGE_SIZE, stride=N_KV_HEADS), :].astype(BF16) for r in vp_refs], axis=0)
        s = lax.dot_general(qg, kg, (((1,), (1,)), ((), ())), preferred_element_type=F32) * scale
        near = jnp.where(is_last, bl_ref[g * grp:(g + 1) * grp, :], 0.0)
        s = s + mb_g + (near if group == 1 else jnp.concatenate([zeros_head, near], axis=1))
        update(g, s, vg)

    @pl.when(is_last)
    def _():
        mbn_g = group_rows(jnp.where(keys_new >= thr_b, 0.0, MASK_NEG))
        for g in range(N_KV_HEADS):
            qg = q_ref[0, g * grp:(g + 1) * grp, :]
            kblk = kn_ref[0, :, g * HEAD_DIM:(g + 1) * HEAD_DIM].astype(BF16)
            vblk = vn_ref[0, :, g * HEAD_DIM:(g + 1) * HEAD_DIM].astype(BF16)
            s = lax.dot_general(qg, kblk, (((1,), (1,)), ((), ())), preferred_element_type=F32) * scale
            s = s + mbn_g + bn_ref[g * grp:(g + 1) * grp, :]
            update(g, s, vblk)
        o_ref[0] = acc_ref[...] / l_ref[...]


def _attn_sample(page_table, scores, scores_new, q_rows, k_new, v_new, bias_last, bias_new,
                 cache_k, cache_v, layer, n_tok, group):
    db, n_pages = page_table.shape
    past = n_pages * PAGE_SIZE
    rows = N_HEADS * n_tok
    k_sel = min(TOPK_MAX, (past + n_tok) // 4)
    body = functools.partial(_attn_sample_body, n_tok=n_tok, n_pages=n_pages, k_sel=k_sel, group=group)
    page = (PAGE_SIZE * N_KV_HEADS, HEAD_DIM)
    grid_spec = pltpu.PrefetchScalarGridSpec(
        num_scalar_prefetch=1,
        grid=(db, n_pages // group),
        in_specs=[pl.BlockSpec((1, n_pages, n_tok, PAGE_SIZE), lambda b, p, pt: (b, 0, 0, 0)),
                  pl.BlockSpec((1, n_tok, LANES), lambda b, p, pt: (b, 0, 0)),
                  pl.BlockSpec((1, rows, HEAD_DIM), lambda b, p, pt: (b, 0, 0)),
                  pl.BlockSpec((1, LANES, KV_W), lambda b, p, pt: (b, 0, 0)),
                  pl.BlockSpec((1, LANES, KV_W), lambda b, p, pt: (b, 0, 0)),
                  pl.BlockSpec((rows, LANES), lambda b, p, pt: (0, 0)),
                  pl.BlockSpec((rows, LANES), lambda b, p, pt: (0, 0))]
        + [_page_spec(page, layer, group, t) for t in range(group)]
        + [_page_spec(page, layer, group, t) for t in range(group)],
        out_specs=pl.BlockSpec((1, rows, HEAD_DIM), lambda b, p, pt: (b, 0, 0)),
        scratch_shapes=[pltpu.VMEM((n_tok, LANES), I32),
                        pltpu.VMEM((rows, LANES), F32),
                        pltpu.VMEM((rows, LANES), F32),
                        pltpu.VMEM((rows, HEAD_DIM), F32)],
    )
    return pl.pallas_call(
        body,
        grid_spec=grid_spec,
        out_shape=jax.ShapeDtypeStruct((db, rows, HEAD_DIM), F32),
        compiler_params=_cp(("arbitrary", "arbitrary")),
        name="attn_sample",
    )(page_table, scores, scores_new, q_rows, k_new, v_new, bias_last, bias_new,
      *([cache_k] * group), *([cache_v] * group))


def _idx_new_body(iq_ref, iw_ref, ik_ref, o_ref, *, n_tok):
    for b in range(iq_ref.shape[0]):
        s = lax.dot_general(iq_ref[b], ik_ref[b].astype(BF16), (((1,), (1,)), ((), ())),
                            preferred_element_type=F32)
        r = jnp.maximum(s, 0.0) * iw_ref[b]
        o_ref[b] = r.reshape(n_tok, IDX_HEADS, LANES).sum(axis=1)


def _idx_new(iq_rows, iw_rows, ik_new, n_tok):
    db = iq_rows.shape[0]
    return pl.pallas_call(
        functools.partial(_idx_new_body, n_tok=n_tok),
        out_shape=jax.ShapeDtypeStruct((db, n_tok, LANES), F32),
        name="idx_new",
    )(iq_rows, iw_rows, ik_new)


HALO = CONV_WIDTH - 1
HALO_PAD = 32
CONV_ROW_BLOCK = 128


CONV_CHUNKS = CONV_DIM // LANES


def _conv_prompt_body(glu_ref, cw_ref, cb_ref, g_ref, b_ref, o_ref, st_ref, hp_ref, c_ref, sh_ref, *, ts, ns):
    s_idx = pl.program_id(1)

    @pl.when(s_idx == 0)
    def _():
        hp_ref[:, 0:HALO_PAD, :] = jnp.zeros((CONV_CHUNKS, HALO_PAD, LANES), F32)

    for c in range(CONV_CHUNKS):
        cs = slice(c * LANES, (c + 1) * LANES)
        a = glu_ref[:, cs].astype(F32)
        gt = glu_ref[:, CONV_DIM + c * LANES:CONV_DIM + (c + 1) * LANES].astype(F32)
        hp_ref[c, HALO_PAD:HALO_PAD + ts, :] = a * _sigmoid(gt)
    off = HALO_PAD - HALO
    rows = min(ts, CONV_ROW_BLOCK)

    def chunk(c, carry):
        for r0 in range(0, ts, rows):
            acc = jnp.zeros((rows // F32_SUBLANES, F32_SUBLANES, LANES), F32)
            for shift in range(F32_SUBLANES):
                taps = range(shift, CONV_WIDTH, F32_SUBLANES)
                span = rows + taps[-1] - shift
                sh_ref[0:span, :] = hp_ref[c, off + r0 + shift:off + r0 + shift + span, :]
                for w in taps:
                    x = sh_ref[w - shift:w - shift + rows, :].reshape(rows // F32_SUBLANES, F32_SUBLANES, LANES)
                    acc = acc + x * cw_ref[c, w][None]
            c_ref[c, r0:r0 + rows, :] = acc.reshape(rows, LANES)
        return carry

    lax.fori_loop(0, CONV_CHUNKS, chunk, 0)

    conv = [c_ref[c] + cb_ref[:, c * LANES:(c + 1) * LANES] for c in range(CONV_CHUNKS)]
    mu = sum(v.sum(axis=-1, keepdims=True) for v in conv) / CONV_DIM
    var = sum(((v - mu) * (v - mu)).sum(axis=-1, keepdims=True) for v in conv) / CONV_DIM
    inv = lax.rsqrt(var + LN_EPS)
    for c in range(CONV_CHUNKS):
        cs = slice(c * LANES, (c + 1) * LANES)
        y = (conv[c] - mu) * inv * g_ref[:, cs] + b_ref[:, cs]
        o_ref[:, cs] = (y * _sigmoid(y)).astype(o_ref.dtype)

    @pl.when(s_idx == ns - 1)
    def _():
        for c in range(CONV_CHUNKS):
            st_ref[0, :, c * LANES:(c + 1) * LANES] = hp_ref[c, ts + off:ts + HALO_PAD, :]

    hp_ref[:, off:HALO_PAD, :] = hp_ref[:, ts + off:ts + HALO_PAD, :]


def _conv_prompt(ha, conv_w, conv_b, ln_g, ln_b, batch, seq, ts):
    ns = seq // ts
    body = functools.partial(_conv_prompt_body, ts=ts, ns=ns)
    vec = pl.BlockSpec((1, CONV_DIM), lambda b, s: (0, 0))
    return pl.pallas_call(
        body,
        grid=(batch, ns),
        in_specs=[pl.BlockSpec((ts, 2 * CONV_DIM), lambda b, s: (b * ns + s, HA_GLU // (2 * CONV_DIM))),
                  pl.BlockSpec((CONV_CHUNKS, CONV_WIDTH, F32_SUBLANES, LANES), lambda b, s: (0, 0, 0, 0)),
                  vec, vec, vec],
        out_specs=[pl.BlockSpec((ts, CONV_DIM), lambda b, s: (b * ns + s, 0)),
                   pl.BlockSpec((1, HALO, CONV_DIM), lambda b, s: (b, 0, 0))],
        out_shape=[jax.ShapeDtypeStruct((batch * seq, CONV_DIM), BF16),
                   jax.ShapeDtypeStruct((batch, HALO, CONV_DIM), F32)],
        scratch_shapes=[pltpu.VMEM((CONV_CHUNKS, HALO_PAD + ts, LANES), F32),
                        pltpu.VMEM((CONV_CHUNKS, ts, LANES), F32),
                        pltpu.VMEM((HALO_PAD + min(ts, CONV_ROW_BLOCK), LANES), F32)],
        compiler_params=_cp(("arbitrary", "arbitrary")),
        name="conv_prompt",
    )(ha, conv_w, conv_b, ln_g, ln_b)


def _sgu_prompt_body(uv_ref, g_ref, b_ref, w_ref, bs_ref, o_ref, *, ts):
    u = uv_ref[:, :SGU_DIM].astype(F32)
    vn = _ln_rows(uv_ref[:, SGU_DIM:].astype(F32), g_ref[...], b_ref[...]).astype(BF16)
    r_i = lax.broadcasted_iota(I32, (CHUNK, CHUNK), 0)
    c_i = lax.broadcasted_iota(I32, (CHUNK, CHUNK), 1)
    gd = SGU_DIM // SGU_GROUPS
    for g in range(SGU_GROUPS):
        wm = jnp.where(c_i <= r_i, w_ref[g], 0.0).astype(BF16)
        bias = bs_ref[:, g:g + 1]
        for c in range(ts // CHUNK):
            rs = slice(c * CHUNK, (c + 1) * CHUNK)
            gs = slice(g * gd, (g + 1) * gd)
            mixed = jnp.dot(wm, vn[rs, gs], preferred_element_type=F32) + bias
            o_ref[rs, gs] = (u[rs, gs] * mixed).astype(o_ref.dtype)


def _sgu_prompt(ha, ln_g, ln_b, w_s, b_s_t, batch, seq, ts):
    ns = seq // ts
    vec = pl.BlockSpec((1, SGU_DIM), lambda b, s: (0, 0))
    return pl.pallas_call(
        functools.partial(_sgu_prompt_body, ts=ts),
        grid=(batch, ns),
        in_specs=[pl.BlockSpec((ts, 2 * SGU_DIM), lambda b, s: (b * ns + s, HA_UV // (2 * SGU_DIM))),
                  vec, vec,
                  pl.BlockSpec((SGU_GROUPS, CHUNK, CHUNK), lambda b, s: (0, 0, 0)),
                  pl.BlockSpec((CHUNK, SGU_GROUPS), lambda b, s: (0, 0))],
        out_specs=pl.BlockSpec((ts, SGU_DIM), lambda b, s: (b * ns + s, 0)),
        out_shape=jax.ShapeDtypeStruct((batch * seq, SGU_DIM), BF16),
        compiler_params=_cp(("arbitrary", "arbitrary")),
        name="sgu_prompt",
    )(ha, ln_g, ln_b, w_s, b_s_t)


def _mix_sample_body(glu_ref, uv_ref, st_ref, cw_ref, cb_ref, cg_ref, cbt_ref, sg_ref, sb_ref, wv_ref, bv_ref,
                     co_ref, nst_ref, so_ref, vn_ref, hp_ref, *, db, n_tok):
    cw = cw_ref[...]
    for b in range(db):
        rs = slice(b * n_tok, (b + 1) * n_tok)
        a = glu_ref[rs, :CONV_DIM].astype(F32)
        gt = glu_ref[rs, CONV_DIM:].astype(F32)
        hp_ref[0:HALO, :] = st_ref[b]
        hp_ref[HALO:HALO + n_tok, :] = a * _sigmoid(gt)
        rows = [jnp.sum(hp_ref[t:t + CONV_WIDTH, :] * cw, axis=0, keepdims=True) for t in range(n_tok)]
        c = jnp.concatenate(rows, axis=0) + cb_ref[...]
        y = _ln_rows(c, cg_ref[...], cbt_ref[...])
        co_ref[rs, :] = y * _sigmoid(y)
        nst_ref[b] = hp_ref[n_tok:n_tok + HALO, :]
        u = uv_ref[rs, :SGU_DIM].astype(F32)
        vn = _ln_rows(uv_ref[rs, SGU_DIM:].astype(F32), sg_ref[...], sb_ref[...])
        vn_ref[rs, :] = vn
        mixed = []
        for t in range(n_tok):
            acc = bv_ref[t:t + 1, :]
            for s in range(t + 1):
                acc = acc + wv_ref[t, s:s + 1, :] * vn[s:s + 1, :]
            mixed.append(acc)
        so_ref[rs, :] = u * jnp.concatenate(mixed, axis=0)


def _mix_sample(glu, uv, state, conv_w, conv_b, cln_g, cln_b, sln_g, sln_b, wv, bv, db, n_tok):
    rows = db * n_tok
    return pl.pallas_call(
        functools.partial(_mix_sample_body, db=db, n_tok=n_tok),
        out_shape=[jax.ShapeDtypeStruct((rows, CONV_DIM), F32),
                   jax.ShapeDtypeStruct((db, HALO, CONV_DIM), F32),
                   jax.ShapeDtypeStruct((rows, SGU_DIM), F32),
                   jax.ShapeDtypeStruct((rows, SGU_DIM), F32)],
        scratch_shapes=[pltpu.VMEM((_round_up(HALO + n_tok, 8), CONV_DIM), F32)],
        name="mix_sample",
    )(glu, uv, state, conv_w, conv_b, cln_g, cln_b, sln_g, sln_b, wv, bv)


def _router_body(x_ref, w_ref, b_ref, ti_ref, tg_ref, *, n_exp):
    logits = jnp.dot(x_ref[...], w_ref[...], preferred_element_type=F32,
                     precision=lax.Precision.HIGHEST) + b_ref[...]
    lane = lax.broadcasted_iota(I32, logits.shape, 1)
    lane_f = lane.astype(F32)
    neg = -jnp.inf
    lg = jnp.where(lane < n_exp, logits, neg)
    v1 = lg.max(axis=-1, keepdims=True)
    i1 = jnp.where(lg == v1, lane_f, float(LANES)).min(axis=-1, keepdims=True)
    lg2 = jnp.where(lane_f == i1, neg, lg)
    v2 = lg2.max(axis=-1, keepdims=True)
    i2 = jnp.where(lg2 == v2, lane_f, float(LANES)).min(axis=-1, keepdims=True)
    e = jnp.exp(v2 - v1)
    den = 1.0 + e
    ti_ref[...] = jnp.where(lane == 0, i1, jnp.where(lane == 1, i2, 0.0)).astype(I32)
    tg_ref[...] = jnp.where(lane == 0, 1.0 / den, jnp.where(lane == 1, e / den, 0.0))


def _router(x, w_pad, b_pad, n_exp, tm):
    m, d = x.shape
    return pl.pallas_call(
        functools.partial(_router_body, n_exp=n_exp),
        grid=(m // tm,),
        in_specs=[pl.BlockSpec((tm, d), lambda i: (i, 0)),
                  pl.BlockSpec((d, LANES), lambda i: (0, 0)),
                  pl.BlockSpec((1, LANES), lambda i: (0, 0))],
        out_specs=[pl.BlockSpec((tm, LANES), lambda i: (i, 0)),
                   pl.BlockSpec((tm, LANES), lambda i: (i, 0))],
        out_shape=[jax.ShapeDtypeStruct((m, LANES), I32), jax.ShapeDtypeStruct((m, LANES), F32)],
        compiler_params=_cp(("arbitrary",)),
        name="moe_router",
    )(x, w_pad, b_pad)


def _row_copy(src_hbm, dst, src_row, dst_row, sem):
    return pltpu.make_async_copy(src_hbm.at[pl.ds(src_row, 1)], dst.at[pl.ds(dst_row, 1)], sem)


ROW_DMA_UNROLL = 8


def _prefetch_chunks(issue, wait):
    c = pl.program_id(0)

    @pl.when(c == 0)
    def _():
        issue(c, 0)

    @pl.when(c + 1 < pl.num_programs(0))
    def _():
        issue(c + 1, (c + 1) % 2)

    slot = c % 2
    wait(slot)
    return slot


def _dispatch_body(src_ref, x_hbm, o_ref, buf, sem, *, rows):
    def issue(chunk, slot):
        def start(u, carry):
            for v in range(ROW_DMA_UNROLL):
                r = u * ROW_DMA_UNROLL + v
                _row_copy(x_hbm, buf.at[slot], src_ref[chunk * rows + r], r, sem.at[slot]).start()
            return carry

        lax.fori_loop(0, rows // ROW_DMA_UNROLL, start, 0)

    def wait(slot):
        def w(r, carry):
            _row_copy(x_hbm, buf.at[slot], 0, r, sem.at[slot]).wait()
            return carry

        lax.fori_loop(0, rows, w, 0)

    slot = _prefetch_chunks(issue, wait)
    o_ref[...] = buf[slot].astype(o_ref.dtype)


def _dispatch(src, x, n_slots, rows):
    d = x.shape[1]
    grid_spec = pltpu.PrefetchScalarGridSpec(
        num_scalar_prefetch=1,
        grid=(n_slots // rows,),
        in_specs=[pl.BlockSpec(memory_space=pl.ANY)],
        out_specs=pl.BlockSpec((rows, d), lambda i, s: (i, 0)),
        scratch_shapes=[pltpu.VMEM((2, rows, d), x.dtype), pltpu.SemaphoreType.DMA((2,))],
    )
    assert rows % ROW_DMA_UNROLL == 0
    return pl.pallas_call(
        functools.partial(_dispatch_body, rows=rows),
        grid_spec=grid_spec,
        out_shape=jax.ShapeDtypeStruct((n_slots, d), BF16),
        compiler_params=_cp(("arbitrary",)),
        name="moe_dispatch",
    )(src, x)


def _tile_copy(src, dst, sem):
    return pltpu.make_async_copy(src, dst, sem)


def _moe_ffn_body(exp_ref, row0_ref, nsub_ref, jmap_ref, nzero_ref, xs_hbm, w1_ref, w3_ref, w2_ref, ys_hbm,
                  xbuf, acc, w1b, w3b, w2b, sem, *, nj, sub):
    w = pl.program_id(0)
    j = pl.program_id(1)
    nsub = nsub_ref[w]
    row0 = row0_ref[w]
    nzero = nzero_ref[w]

    def zero_copy(t):
        r = pl.multiple_of(row0 + t * sub, sub)
        return _tile_copy(acc.at[pl.ds(0, sub)], ys_hbm.at[pl.ds(r, sub)], sem)

    @pl.when(jnp.logical_and(j == 0, nzero > 0))
    def _():
        acc[0:sub, :] = jnp.zeros((sub, acc.shape[1]), F32)

        def zs(t, carry):
            zero_copy(t).start()
            return carry

        def zw(t, carry):
            zero_copy(t).wait()
            return carry

        lax.fori_loop(0, nzero, zs, 0)
        lax.fori_loop(0, nzero, zw, 0)

    def in_copy(t):
        r = pl.multiple_of(t * sub, sub)
        return _tile_copy(xs_hbm.at[pl.ds(pl.multiple_of(row0 + r, sub), sub)], xbuf.at[pl.ds(r, sub)], sem)

    def out_copy(t):
        r = pl.multiple_of(t * sub, sub)
        return _tile_copy(acc.at[pl.ds(r, sub)], ys_hbm.at[pl.ds(pl.multiple_of(row0 + r, sub), sub)], sem)

    def for_tiles(fn):
        def it(t, carry):
            fn(t)
            return carry
        lax.fori_loop(0, nsub, it, 0)

    def zero_tile(t):
        acc[pl.ds(pl.multiple_of(t * sub, sub), sub), :] = jnp.zeros((sub, acc.shape[1]), F32)

    @pl.when(j == 0)
    def _():
        for_tiles(lambda t: in_copy(t).start())
        for_tiles(zero_tile)
        for_tiles(lambda t: in_copy(t).wait())

    def tile(t, n, w1v, w3v, w2v):
        r = pl.multiple_of(t * sub, sub)
        x = xbuf[pl.ds(r, n * sub), :]
        a = jnp.dot(x, w1v, preferred_element_type=F32)
        c = jnp.dot(x, w3v, preferred_element_type=F32)
        h = (a * _sigmoid(a) * c).astype(BF16)
        acc[pl.ds(r, n * sub), :] += jnp.dot(h, w2v, preferred_element_type=F32)

    def cast_weights():
        return w1_ref[...].astype(BF16), w3_ref[...].astype(BF16), w2_ref[...].astype(BF16)

    per_trip = 2 * MOE_WIDE
    n_trips = nsub // per_trip

    def trip(t, wv):
        tile(per_trip * t, MOE_WIDE, *wv)
        tile(per_trip * t + MOE_WIDE, MOE_WIDE, *wv)

    @pl.when(n_trips >= 1)
    def _():
        wv = cast_weights()
        w1b[...], w3b[...], w2b[...] = wv
        trip(0, wv)

    @pl.when(jnp.logical_and(n_trips == 0, nsub > 0))
    def _():
        w1b[...], w3b[...], w2b[...] = cast_weights()

    def later_trip(t, carry):
        trip(t, (w1b[...], w3b[...], w2b[...]))
        return carry

    lax.fori_loop(1, n_trips, later_trip, 0)

    def rest(t, carry):
        tile(t, 1, w1b[...], w3b[...], w2b[...])
        return carry

    lax.fori_loop(n_trips * per_trip, nsub, rest, 0)

    @pl.when(j == nj - 1)
    def _():
        for_tiles(lambda t: out_copy(t).start())
        for_tiles(lambda t: out_copy(t).wait())


def _moe_ffn(item_exp, item_row0, item_nsub, item_jlast, item_nzero, xs, w1, w3, w2, layer, n_items, super_rows,
             sub, tf):
    n_slots, d = xs.shape
    f = w1.shape[3]
    nj = f // tf

    def jj(w, j, jl):
        return jnp.where(jl[w] > 0, j, nj - 1)

    grid_spec = pltpu.PrefetchScalarGridSpec(
        num_scalar_prefetch=5,
        grid=(n_items, nj),
        in_specs=[pl.BlockSpec(memory_space=pl.ANY),
                  pl.BlockSpec((None, None, d, tf), lambda w, j, ex, r0, ns, jl, nz: (layer, ex[w], 0, jj(w, j, jl))),
                  pl.BlockSpec((None, None, d, tf), lambda w, j, ex, r0, ns, jl, nz: (layer, ex[w], 0, jj(w, j, jl))),
                  pl.BlockSpec((None, None, tf, d), lambda w, j, ex, r0, ns, jl, nz: (layer, ex[w], jj(w, j, jl), 0))],
        out_specs=pl.BlockSpec(memory_space=pl.ANY),
        scratch_shapes=[pltpu.VMEM((super_rows, d), BF16),
                        pltpu.VMEM((super_rows, d), F32),
                        pltpu.VMEM((d, tf), BF16),
                        pltpu.VMEM((d, tf), BF16),
                        pltpu.VMEM((tf, d), BF16),
                        pltpu.SemaphoreType.DMA(())],
    )
    return pl.pallas_call(
        functools.partial(_moe_ffn_body, nj=nj, sub=sub),
        grid_spec=grid_spec,
        out_shape=jax.ShapeDtypeStruct((n_slots, d), F32),
        compiler_params=_cp(("arbitrary", "arbitrary")),
        name="moe_ffn",
    )(item_exp, item_row0, item_nsub, item_jlast, item_nzero, xs, w1, w3, w2)


def _combine_body(pos_ref, ys_hbm, tg_ref, x_ref, g_ref, b_ref, of_ref, ob_ref, buf, sem, *, tb, alpha):
    def issue(chunk, slot):
        def start(u, carry):
            for v in range(ROW_DMA_UNROLL // TOP_K_EXPERTS):
                r = u * (ROW_DMA_UNROLL // TOP_K_EXPERTS) + v
                for k in range(TOP_K_EXPERTS):
                    p = pos_ref[TOP_K_EXPERTS * (chunk * tb + r) + k]
                    _row_copy(ys_hbm, buf.at[slot, k], p, r, sem.at[slot]).start()
            return carry

        lax.fori_loop(0, tb * TOP_K_EXPERTS // ROW_DMA_UNROLL, start, 0)

    def wait(slot):
        def w(r, carry):
            for k in range(TOP_K_EXPERTS):
                _row_copy(ys_hbm, buf.at[slot, k], 0, r, sem.at[slot]).wait()
            return carry

        lax.fori_loop(0, tb, w, 0)

    slot = _prefetch_chunks(issue, wait)
    f = tg_ref[:, 0:1] * buf[slot, 0] + tg_ref[:, 1:2] * buf[slot, 1]
    y = _ln_rows(alpha * x_ref[...] + f, g_ref[...], b_ref[...])
    of_ref[...] = y
    ob_ref[...] = y.astype(BF16)


def _combine_ln(pos, ys, tg, x, g, b, alpha, tb):
    m, d = x.shape
    grid_spec = pltpu.PrefetchScalarGridSpec(
        num_scalar_prefetch=1,
        grid=(m // tb,),
        in_specs=[pl.BlockSpec(memory_space=pl.ANY),
                  pl.BlockSpec((tb, LANES), lambda i, p: (i, 0)),
                  pl.BlockSpec((tb, d), lambda i, p: (i, 0)),
                  pl.BlockSpec((1, d), lambda i, p: (0, 0)),
                  pl.BlockSpec((1, d), lambda i, p: (0, 0))],
        out_specs=[pl.BlockSpec((tb, d), lambda i, p: (i, 0)),
                   pl.BlockSpec((tb, d), lambda i, p: (i, 0))],
        scratch_shapes=[pltpu.VMEM((2, TOP_K_EXPERTS, tb, d), F32), pltpu.SemaphoreType.DMA((2,))],
    )
    assert tb * TOP_K_EXPERTS % ROW_DMA_UNROLL == 0
    return pl.pallas_call(
        functools.partial(_combine_body, tb=tb, alpha=alpha),
        grid_spec=grid_spec,
        out_shape=[jax.ShapeDtypeStruct((m, d), F32), jax.ShapeDtypeStruct((m, d), BF16)],
        compiler_params=_cp(("arbitrary",)),
        name="moe_combine_ln",
    )(pos, ys, tg, x, g, b)


def _moe_plan(top_i, n_real, n_exp, n_slots, n_items, super_rows, sub):
    a = top_i.shape[0] * TOP_K_EXPERTS
    e_flat = top_i.reshape(-1)
    tok = jnp.arange(a, dtype=I32) // TOP_K_EXPERTS
    real = tok < n_real
    onehot = jnp.logical_and(e_flat[:, None] == jnp.arange(n_exp, dtype=I32)[None, :], real[:, None]).astype(I32)
    csum = jnp.cumsum(onehot, axis=0)
    rank = jnp.take_along_axis(csum, e_flat[:, None], axis=1)[:, 0] - 1
    counts = csum[-1]
    padded = (counts + sub - 1) // sub * sub
    gstart = jnp.cumsum(padded) - padded
    pos = jnp.where(real, gstart[e_flat] + rank, 0).astype(I32)
    src = jnp.zeros((n_slots,), I32).at[jnp.where(real, pos, n_slots)].set(tok, mode="drop")
    n_super = (padded + super_rows - 1) // super_rows
    iend = jnp.cumsum(n_super)
    istart = iend - n_super
    total = iend[-1]
    wi = jnp.arange(n_items, dtype=I32)
    valid = wi < total
    e_w = jnp.minimum(jnp.searchsorted(iend, wi, side="right"), n_exp - 1).astype(I32)
    last_e = e_w[jnp.maximum(total - 1, 0)]
    e_w = jnp.where(valid, e_w, last_e)
    k_w = wi - istart[e_w]
    subs_e = padded[e_w] // sub
    items_e = jnp.maximum(n_super[e_w], 1)
    base, extra = subs_e // items_e, subs_e % items_e
    nsub = jnp.where(valid, base + (k_w < extra).astype(I32), 0).astype(I32)
    first_sub = k_w * base + jnp.minimum(k_w, extra)
    z0 = jnp.sum(padded) + (wi - total) * super_rows
    nzero = jnp.where(valid, 0, jnp.clip((n_slots - z0) // sub, 0, super_rows // sub)).astype(I32)
    row0 = jnp.where(valid, gstart[e_w] + first_sub * sub, jnp.where(nzero > 0, z0, 0)).astype(I32)
    return pos, src, e_w, row0, nsub, valid.astype(I32), nzero


def _rel_bucket(dist):
    n = jnp.maximum(dist, 0)
    max_exact = REL_BUCKETS // 2
    nf = jnp.maximum(n, 1).astype(F32)
    large = max_exact + (jnp.log(nf / max_exact) / math.log(REL_MAX_DIST / max_exact)
                         * (REL_BUCKETS - max_exact)).astype(I32)
    large = jnp.minimum(large, REL_BUCKETS - 1)
    return jnp.where(n < max_exact, n, large)


def _shifted_bias(rel_bias, dist):
    t = rel_bias[_rel_bucket(dist)] - rel_bias[REL_BUCKETS - 1]
    t = jnp.where((dist >= 0)[..., None], t, 0.0)
    return jnp.moveaxis(t, -1, 0)


def _pick_tile(m_rows, target):
    return max(t for t in range(16, target + 1, 16) if m_rows % t == 0)


def _pad_rows(x, rows):
    return jnp.pad(x, ((0, rows - x.shape[0]), (0, 0)))


def kernel(x_prompt, x_sample, cache_k, cache_v, cache_ik, state_conv, page_table, w_in, conv_w, conv_b,
           conv_ln_g, conv_ln_b, sgu_ln_g, sgu_ln_b, sgu_w, sgu_b, w_pa, w_pb, w_pc, w_out, ln1_g, ln1_b,
           ln2_g, ln2_b, rel_bias, ffn_w1, ffn_w3, ffn_w2, moe_router, moe_router_b, moe_w1, moe_w3, moe_w2):
    batch, seq, d_model = x_prompt.shape
    db, n_tok, _ = x_sample.shape
    depth = w_in.shape[0]
    n_pool = cache_k.shape[1]
    n_exp = moe_router.shape[2]
    mp, ms = batch * seq, db * n_tok
    assert ms <= SAMPLE_TILE and mp % SAMPLE_TILE == 0
    m_all = mp + SAMPLE_TILE
    tm = m_all // N_ROW_TILES
    assert tm * N_ROW_TILES == m_all and tm % 16 == 0
    alpha = (2 * depth) ** 0.25
    tq = min(256, seq)

    xf = jnp.concatenate([x_prompt.reshape(mp, d_model), x_sample.reshape(ms, d_model),
                          jnp.zeros((m_all - mp - ms, d_model), F32)], axis=0)
    xb = xf.astype(BF16)
    w_in_t = jnp.swapaxes(w_in, 1, 2)
    ha_rows = ([W_Q_ROW0 + HA_TILE * t for t in range(QKV_W // HA_TILE)]
               + [W_GLU_ROW0 + HA_TILE * t for t in range(2 * CONV_DIM // HA_TILE)]
               + [W_UV_ROW0 + HA_TILE * t for t in range(2 * SGU_DIM // HA_TILE)]
               + [W_GATE_ROW0 + HA_TILE * t for t in range(N_BRANCH * d_model // HA_TILE)]
               + [W_IQ_ROW0 + HA_TILE * t for t in range(IQ_W // HA_TILE)])
    w_pa_b, w_pb_b, w_pc_b, w_out_b = (w.astype(BF16) for w in (w_pa, w_pb, w_pc, w_out))
    ffn_w2_b = ffn_w2.astype(BF16)

    kk = jnp.arange(2 * tq, dtype=I32)
    d_wrap = jnp.where(kk < tq, -kk, 2 * tq - kk)

    def toeplitz(offset):
        v = _shifted_bias(rel_bias, d_wrap + offset)
        return jnp.tile(v, (1, tq))[:, :tq * (2 * tq - 1)].reshape(N_HEADS, tq, 2 * tq - 1)[:, :, :tq]

    bias_tiles = jnp.stack([toeplitz(0), toeplitz(tq)])
    tok = jnp.arange(n_tok, dtype=I32)
    lane = jnp.arange(LANES, dtype=I32)
    d_last = PAGE_SIZE + tok[:, None] - lane[None, :]
    bias_last = _shifted_bias(rel_bias, d_last).reshape(N_HEADS * n_tok, LANES)
    bias_new = _shifted_bias(rel_bias, tok[:, None] - lane[None, :]).reshape(N_HEADS * n_tok, LANES)

    ck = cache_k.reshape(depth, n_pool, PAGE_SIZE * N_KV_HEADS, HEAD_DIM)
    cv = cache_v.reshape(depth, n_pool, PAGE_SIZE * N_KV_HEADS, HEAD_DIM)
    cik_t = jnp.swapaxes(cache_ik, 2, 3)
    n_pages = page_table.shape[1]
    idx_group = math.gcd(n_pages, IDX_PAGE_GROUP)
    attn_group = math.gcd(n_pages, ATTN_PAGE_GROUP)
    n_slots = _round_up(TOP_K_EXPERTS * (mp + ms) + n_exp * (MOE_SUB - 1), MOE_SUB)
    n_items = -(-n_slots // MOE_SUPER) + n_exp

    outs = {k: [] for k in ("k", "v", "ik", "conv_p", "conv_s", "sgu_s")}
    for l in range(depth):
        ha = _in_proj(xb, w_in_t, l, ha_rows, tm, HA_TILE)
        kv, ikw = _kv_proj(xb, w_in_t, l, tm)

        a_p = _attn_prompt(ha, kv, ikw, bias_tiles, batch, seq, tq)
        conv_w_rep = jnp.broadcast_to(conv_w[l].reshape(CONV_WIDTH, CONV_CHUNKS, 1, LANES).transpose(1, 0, 2, 3),
                                      (CONV_CHUNKS, CONV_WIDTH, F32_SUBLANES, LANES))
        c_p, conv_state_p = _conv_prompt(ha, conv_w_rep, conv_b[l][None], conv_ln_g[l][None], conv_ln_b[l][None],
                                         batch, seq, min(256, seq))
        s_p = _sgu_prompt(ha, sgu_ln_g[l][None], sgu_ln_b[l][None], sgu_w[l], sgu_b[l].T, batch, seq,
                          min(512, seq))

        has, kvs, ikws = ha[mp:mp + ms], kv[mp:mp + ms], ikw[mp:mp + ms]
        iq_rows = has[:, HA_IQ:HA_IQ + IQ_W].reshape(db, n_tok * IDX_HEADS, IDX_DIM)
        iw_rows = (ikws[:, IKW_IW:IKW_IW + IDX_HEADS] * (IDX_DIM ** -0.5 * IDX_HEADS ** -0.5)
                   ).reshape(db, n_tok * IDX_HEADS, 1)
        iw_rows = jnp.broadcast_to(iw_rows, (db, n_tok * IDX_HEADS, LANES))
        scores = _idx_sample(page_table, iq_rows, iw_rows, cik_t, l, n_tok, idx_group)
        pad_rows = LANES - n_tok
        ik_new = jnp.pad(ikws[:, :IDX_DIM].reshape(db, n_tok, IDX_DIM), ((0, 0), (0, pad_rows), (0, 0)))
        scores_new = _idx_new(iq_rows, iw_rows, ik_new, n_tok)
        q_rows = has[:, HA_Q:HA_Q + QKV_W].reshape(db, n_tok, N_HEADS, HEAD_DIM).transpose(0, 2, 1, 3)
        q_rows = q_rows.reshape(db, N_HEADS * n_tok, HEAD_DIM)
        k_new = jnp.pad(kvs[:, KV_K:KV_K + KV_W].reshape(db, n_tok, KV_W), ((0, 0), (0, pad_rows), (0, 0)))
        v_new = jnp.pad(kvs[:, KV_V:KV_V + KV_W].reshape(db, n_tok, KV_W), ((0, 0), (0, pad_rows), (0, 0)))
        a_s = _attn_sample(page_table, scores, scores_new, q_rows, k_new, v_new, bias_last, bias_new,
                           ck, cv, l, n_tok, attn_group)
        a_s = a_s.reshape(db, N_HEADS, n_tok, HEAD_DIM).transpose(0, 2, 1, 3).reshape(ms, QKV_W)
        gd = SGU_DIM // SGU_GROUPS
        wv = jnp.repeat(sgu_w[l][:, :n_tok, :n_tok].transpose(1, 2, 0), gd, axis=-1)
        bv = jnp.repeat(sgu_b[l][:, :n_tok].T, gd, axis=-1)
        c_s, conv_state_s, s_s, vn_s = _mix_sample(
            has[:, HA_GLU:HA_GLU + 2 * CONV_DIM], has[:, HA_UV:HA_UV + 2 * SGU_DIM], state_conv[l],
            conv_w[l], conv_b[l][None], conv_ln_g[l][None], conv_ln_b[l][None],
            sgu_ln_g[l][None], sgu_ln_b[l][None], wv, bv, db, n_tok)

        sample_rows = tuple(_pad_rows(t.astype(BF16), SAMPLE_TILE) for t in (a_s, c_s, s_s))
        merged = _merge((a_p, c_p, s_p), sample_rows, ha, w_pa_b, w_pb_b, w_pc_b, l, SAMPLE_TILE)
        x1f, x1b = _mm_ln(merged, w_out_b, l, xf, ln1_g[l][None], ln1_b[l][None], alpha, _pick_tile(m_all, 384))

        j = l // 2
        if l % 2 == 0:
            h = _swiglu_up(x1b, ffn_w1, ffn_w3, j, tm, 512)
            xf, xb = _mm_ln(h, ffn_w2_b, j, x1f, ln2_g[l][None], ln2_b[l][None], alpha, _pick_tile(m_all, 384))
        else:
            w_r = jnp.pad(moe_router[j], ((0, 0), (0, LANES - n_exp)))
            b_r = jnp.pad(moe_router_b[j], (0, LANES - n_exp))[None]
            ti, tg = _router(x1f, w_r, b_r, n_exp, _pick_tile(m_all, 528))
            pos, src, it_e, it_r0, it_ns, it_valid, it_nz = _moe_plan(ti[:, :TOP_K_EXPERTS], mp + ms, n_exp,
                                                                      n_slots, n_items, MOE_SUPER, MOE_SUB)
            xs = _dispatch(src, x1f, n_slots, MOE_SUB)
            ys = _moe_ffn(it_e, it_r0, it_ns, it_valid, it_nz, xs, moe_w1, moe_w3, moe_w2, j, n_items,
                          MOE_SUPER, MOE_SUB, MOE_TF)
            tb = COMBINE_TB if m_all % COMBINE_TB == 0 else 16
            xf, xb = _combine_ln(pos, ys, tg, x1f, ln2_g[l][None], ln2_b[l][None], alpha, tb)

        outs["k"].append(kv[:, KV_K:KV_K + KV_W])
        outs["v"].append(kv[:, KV_V:KV_V + KV_W])
        outs["ik"].append(ikw[:, :IDX_DIM])
        outs["conv_p"].append(conv_state_p)
        outs["conv_s"].append(conv_state_s)
        outs["sgu_s"].append(vn_s.reshape(db, n_tok, SGU_DIM))

    def split(name, width_shape):
        st = jnp.stack(outs[name])
        p = st[:, :mp].reshape((depth, batch, seq) + width_shape)
        s = st[:, mp:mp + ms].reshape((depth, db, n_tok) + width_shape)
        return p, s

    k_p, k_s = split("k", (N_KV_HEADS, HEAD_DIM))
    v_p, v_s = split("v", (N_KV_HEADS, HEAD_DIM))
    ik_p, ik_s = split("ik", (IDX_DIM,))
    y_prompt = xf[:mp].reshape(batch, seq, d_model)
    y_sample = xf[mp:mp + ms].reshape(db, n_tok, d_model)
    return (y_prompt, y_sample, k_p, v_p, ik_p, jnp.stack(outs["conv_p"]), k_s, v_s, ik_s,
            jnp.stack(outs["conv_s"]), jnp.stack(outs["sgu_s"]))
```

```python
import functools
import math

import jax
import jax.numpy as jnp
import numpy as np
from jax import lax
from jax.experimental import pallas as pl
from jax.experimental.pallas import tpu as pltpu

F32 = jnp.float32
BF16 = jnp.bfloat16
I32 = jnp.int32

N_HEADS = 16
HEAD_DIM = 128
N_KV_HEADS = 4
KV_REP = N_HEADS // N_KV_HEADS
IDX_HEADS = 16
IDX_DIM = 64
TOPK_MAX = 256
PAGE_SIZE = 128
CONV_DIM = 1024
CONV_WIDTH = 31
SGU_DIM = 1024
SGU_GROUPS = 8
CHUNK = 128
N_BRANCH = 3
TOP_K_EXPERTS = 2
REL_BUCKETS = 32
REL_MAX_DIST = 128
LN_EPS = 1e-5

V7X_VMEM_BYTES = 64 * 2**20
VMEM_LIMIT = V7X_VMEM_BYTES - 8 * 2**20
LANES = 128
F32_SUBLANES = 8
INT_MIN = -(2**31)
MASK_NEG = -1e30

QKV_W = N_HEADS * HEAD_DIM
KV_W = N_KV_HEADS * HEAD_DIM
IQ_W = IDX_HEADS * IDX_DIM
HA_Q, HA_GLU, HA_UV, HA_GATE, HA_IQ = 0, 2048, 4096, 6144, 12288
HA_TILE = 1024
W_Q_ROW0 = 0
W_KV_ROW0 = W_Q_ROW0 + QKV_W
W_IQ_ROW0 = W_KV_ROW0 + 2 * KV_W
W_IK_ROW0 = W_IQ_ROW0 + IQ_W
W_GLU_ROW0 = W_IK_ROW0 + IDX_DIM + IDX_HEADS
W_UV_ROW0 = W_GLU_ROW0 + 2 * CONV_DIM
W_GATE_ROW0 = W_UV_ROW0 + 2 * SGU_DIM
KV_K, KV_V = 0, KV_W
IKW_IW = IDX_DIM

N_ROW_TILES = 8
SAMPLE_TILE = 128
MOE_SUB = 256
MOE_SUPER = 2048
MOE_TF = 256
MOE_WIDE = 2
COMBINE_TB = 128
IDX_PAGE_GROUP = 32
ATTN_PAGE_GROUP = 32


def _cp(sem, vmem=VMEM_LIMIT):
    return pltpu.CompilerParams(dimension_semantics=sem, vmem_limit_bytes=vmem)


def _round_up(x, m):
    return (x + m - 1) // m * m


def _ln_rows(z, g, b):
    mu = jnp.mean(z, axis=-1, keepdims=True)
    d = z - mu
    var = jnp.mean(d * d, axis=-1, keepdims=True)
    return d * lax.rsqrt(var + LN_EPS) * g + b


def _sigmoid(x):
    return 1.0 / (1.0 + jnp.exp(-x))


def _in_proj_body(off_ref, x_ref, w_ref, o_ref, wt_ref):
    @pl.when(pl.program_id(1) == 0)
    def _():
        wt_ref[...] = w_ref[0].T.astype(BF16)

    o_ref[...] = jnp.dot(x_ref[...], wt_ref[...], preferred_element_type=F32).astype(o_ref.dtype)


def _in_proj(x, w_t, layer, row_offsets, tm, tn):
    m, k = x.shape
    n_tiles = len(row_offsets)
    grid_spec = pltpu.PrefetchScalarGridSpec(
        num_scalar_prefetch=1,
        grid=(n_tiles, m // tm),
        in_specs=[pl.BlockSpec((tm, k), lambda j, i, off: (i, 0)),
                  pl.BlockSpec((pl.Element(1), pl.Element(tn), pl.Element(k)),
                               lambda j, i, off: (layer, pl.multiple_of(off[j], F32_SUBLANES), 0))],
        out_specs=pl.BlockSpec((tm, tn), lambda j, i, off: (i, j)),
        scratch_shapes=[pltpu.VMEM((k, tn), BF16)],
    )
    return pl.pallas_call(
        _in_proj_body,
        grid_spec=grid_spec,
        out_shape=jax.ShapeDtypeStruct((m, n_tiles * tn), BF16),
        compiler_params=_cp(("arbitrary", "arbitrary")),
        name="in_proj",
    )(jnp.asarray(row_offsets, I32), x, w_t)


def _kv_proj_body(x_ref, wkv_ref, wik_ref, kv_ref, ik_ref, k4_ref, v4_ref, wkv_t, wik_t):
    @pl.when(pl.program_id(0) == 0)
    def _():
        wkv_t[...] = wkv_ref[...].T.astype(BF16)
        wik_t[...] = wik_ref[...].T.astype(BF16)

    x = x_ref[...]
    tm = x.shape[0]
    kv = jnp.dot(x, wkv_t[...], preferred_element_type=F32)
    kv_ref[...] = kv
    ik_ref[...] = jnp.dot(x, wik_t[...], preferred_element_type=F32)
    for g in range(N_KV_HEADS):
        k4_ref[pl.ds(g, tm, stride=N_KV_HEADS), :] = kv[:, KV_K + g * HEAD_DIM:KV_K + (g + 1) * HEAD_DIM]
        v4_ref[pl.ds(g, tm, stride=N_KV_HEADS), :] = kv[:, KV_V + g * HEAD_DIM:KV_V + (g + 1) * HEAD_DIM]


def _kv_proj(x, w_t, layer, tm):
    m, k = x.shape
    assert W_KV_ROW0 % (2 * KV_W) == 0 and W_IK_ROW0 % LANES == 0
    row = lambda i: (i, 0)
    per_head = jax.ShapeDtypeStruct((m * N_KV_HEADS, HEAD_DIM), F32)
    return pl.pallas_call(
        _kv_proj_body,
        grid=(m // tm,),
        in_specs=[pl.BlockSpec((tm, k), row),
                  pl.BlockSpec((None, 2 * KV_W, k), lambda i: (layer, W_KV_ROW0 // (2 * KV_W), 0)),
                  pl.BlockSpec((None, LANES, k), lambda i: (layer, W_IK_ROW0 // LANES, 0))],
        out_specs=[pl.BlockSpec((tm, 2 * KV_W), row), pl.BlockSpec((tm, LANES), row),
                   pl.BlockSpec((tm * N_KV_HEADS, HEAD_DIM), row), pl.BlockSpec((tm * N_KV_HEADS, HEAD_DIM), row)],
        out_shape=[jax.ShapeDtypeStruct((m, 2 * KV_W), F32), jax.ShapeDtypeStruct((m, LANES), F32),
                   per_head, per_head],
        scratch_shapes=[pltpu.VMEM((k, 2 * KV_W), BF16), pltpu.VMEM((k, LANES), BF16)],
        compiler_params=_cp(("arbitrary",)),
        name="kv_proj",
    )(x, w_t, w_t)


def _row_halves(rows):
    half = rows // 2 // 16 * 16
    return (slice(0, half), slice(half, rows)) if half > 0 else (slice(0, rows),)


def _mm_ln_body(x_ref, w_ref, r_ref, g_ref, b_ref, of_ref, ob_ref, *, alpha):
    for rs in _row_halves(x_ref.shape[0]):
        z = jnp.dot(x_ref[rs, :], w_ref[...], preferred_element_type=F32) + alpha * r_ref[rs, :]
        y = _ln_rows(z, g_ref[...], b_ref[...])
        of_ref[rs, :] = y
        ob_ref[rs, :] = y.astype(BF16)


def _mm_ln(x, w, layer, resid, g, b, alpha, tm):
    m, k = x.shape
    n = w.shape[2]
    row = lambda i: (i, 0)
    return pl.pallas_call(
        functools.partial(_mm_ln_body, alpha=alpha),
        grid=(m // tm,),
        in_specs=[pl.BlockSpec((tm, k), row),
                  pl.BlockSpec((None, k, n), lambda i: (layer, 0, 0), pipeline_mode=pl.Buffered(1)),
                  pl.BlockSpec((tm, n), row),
                  pl.BlockSpec((1, n), lambda i: (0, 0)),
                  pl.BlockSpec((1, n), lambda i: (0, 0))],
        out_specs=[pl.BlockSpec((tm, n), row), pl.BlockSpec((tm, n), row)],
        out_shape=[jax.ShapeDtypeStruct((m, n), F32), jax.ShapeDtypeStruct((m, n), BF16)],
        compiler_params=_cp(("arbitrary",)),
        name="proj_ln",
    )(x, w, resid, g, b)


def _swiglu_up_body(x_ref, w1_ref, w3_ref, o_ref, w1b, w3b):
    @pl.when(pl.program_id(1) == 0)
    def _():
        w1b[...] = w1_ref[...].astype(BF16)
        w3b[...] = w3_ref[...].astype(BF16)

    for rs in _row_halves(x_ref.shape[0]):
        x = x_ref[rs, :]
        a = jnp.dot(x, w1b[...], preferred_element_type=F32)
        c = jnp.dot(x, w3b[...], preferred_element_type=F32)
        o_ref[rs, :] = (a * _sigmoid(a) * c).astype(o_ref.dtype)


def _swiglu_up(x, w1, w3, layer, tm, tf):
    m, k = x.shape
    f = w1.shape[2]
    return pl.pallas_call(
        _swiglu_up_body,
        grid=(f // tf, m // tm),
        in_specs=[pl.BlockSpec((tm, k), lambda j, i: (i, 0)),
                  pl.BlockSpec((None, k, tf), lambda j, i: (layer, 0, j)),
                  pl.BlockSpec((None, k, tf), lambda j, i: (layer, 0, j))],
        out_specs=pl.BlockSpec((tm, tf), lambda j, i: (i, j)),
        out_shape=jax.ShapeDtypeStruct((m, f), BF16),
        scratch_shapes=[pltpu.VMEM((k, tf), BF16), pltpu.VMEM((k, tf), BF16)],
        compiler_params=_cp(("arbitrary", "arbitrary")),
        name="ffn_up",
    )(x, w1, w3)


def _merge_body(ap_ref, cp_ref, sp_ref, as_ref, cs_ref, ss_ref, g_ref, wa_ref, wb_ref, wc_ref, o_ref, *, n_prompt):
    i = pl.program_id(0)
    d = o_ref.shape[1]

    def merge(a_ref, c_ref, s_ref):
        for rs in _row_halves(o_ref.shape[0]):
            pa = jnp.dot(a_ref[rs, :], wa_ref[...], preferred_element_type=F32)
            pb = jnp.dot(c_ref[rs, :], wb_ref[...], preferred_element_type=F32)
            pc = jnp.dot(s_ref[rs, :], wc_ref[...], preferred_element_type=F32)
            o = (_sigmoid(g_ref[rs, :d].astype(F32)) * pa + _sigmoid(g_ref[rs, d:2 * d].astype(F32)) * pb
                 + _sigmoid(g_ref[rs, 2 * d:].astype(F32)) * pc)
            o_ref[rs, :] = o.astype(o_ref.dtype)

    @pl.when(i < n_prompt)
    def _():
        merge(ap_ref, cp_ref, sp_ref)

    @pl.when(i >= n_prompt)
    def _():
        merge(as_ref, cs_ref, ss_ref)


def _merge(prompt, sample, ha, w_pa, w_pb, w_pc, layer, tm):
    m = ha.shape[0]
    d = w_pa.shape[2]
    n_prompt = prompt[0].shape[0] // tm
    assert prompt[0].shape[0] % tm == 0 and sample[0].shape[0] == tm and m == (n_prompt + 1) * tm
    assert HA_GATE % (N_BRANCH * d) == 0
    p_row = lambda i: (jnp.minimum(i, n_prompt - 1), 0)
    fixed = lambda i: (0, 0)
    return pl.pallas_call(
        functools.partial(_merge_body, n_prompt=n_prompt),
        grid=(n_prompt + 1,),
        in_specs=[pl.BlockSpec((tm, t.shape[1]), p_row) for t in prompt]
        + [pl.BlockSpec((tm, t.shape[1]), fixed) for t in sample]
        + [pl.BlockSpec((tm, N_BRANCH * d), lambda i: (i, HA_GATE // (N_BRANCH * d)))]
        + [pl.BlockSpec((None,) + w.shape[1:], lambda i: (layer, 0, 0), pipeline_mode=pl.Buffered(1))
           for w in (w_pa, w_pb, w_pc)],
        out_specs=pl.BlockSpec((tm, d), lambda i: (i, 0)),
        out_shape=jax.ShapeDtypeStruct((m, d), BF16),
        compiler_params=_cp(("arbitrary",)),
        name="branch_merge",
    )(*prompt, *sample, ha, w_pa, w_pb, w_pc)


def _sortable_key(score):
    bits = pltpu.bitcast(score, I32)
    return bits ^ ((bits >> 31) & jnp.int32(0x7FFFFFFF))


def _kth_largest_key(count_ge, rows, k):
    def step(it, t):
        cand = t + jnp.left_shift(jnp.int32(1), 31 - it)
        return jnp.where(count_ge(cand) >= k, cand, t)

    return lax.fori_loop(0, 32, step, jnp.full((rows, 1), INT_MIN, I32))


RADIX4_STEPS = 16
RADIX4_FIELD_BITS = 5
RADIX4_WIDE = 4096.0


def _kth_largest_key_radix4(count3, rows, k, n_valid):
    kf = float(k)

    def cond(state):
        it, _, _, pending = state
        return jnp.logical_and(it < RADIX4_STEPS, pending > 0.0)

    def body(state):
        it, t, cnt, _ = state
        one = jnp.left_shift(jnp.int32(1), 30 - 2 * it)
        c1 = t + one
        c2 = c1 + one
        c3 = c2 + one
        n1, n2, n3 = count3(c1, c2, c3)
        t = jnp.where(n3 >= kf, c3, jnp.where(n2 >= kf, c2, jnp.where(n1 >= kf, c1, t)))
        cnt = jnp.where(n3 >= kf, n3, jnp.where(n2 >= kf, n2, jnp.where(n1 >= kf, n1, cnt)))
        settled = jnp.logical_or(cnt == kf, n_valid <= kf)
        return it + 1, t, cnt, jnp.max(jnp.where(settled, 0.0, 1.0))

    init = (jnp.int32(0), jnp.full((rows, 1), INT_MIN, I32), jnp.full((rows, 1), -1.0, F32), jnp.float32(1.0))
    return lax.while_loop(cond, body, init)[1]


def _lane_tile(x, width):
    return x if width == LANES else jnp.concatenate([x] * (width // LANES), axis=1)


def _attn_prompt_body(q_ref, iq_ref, iw_ref, kv_ref, ik_ref, bias_ref, o_ref,
                      keys_ref, mb_ref, iwb_ref, m_ref, acc_ref, *, tq, k_sel):
    i = pl.program_id(1)
    nkb = i + 1
    tk = tq
    row = lax.broadcasted_iota(I32, (tq, tk), 0)
    col = lax.broadcasted_iota(I32, (tq, tk), 1)
    low_half = lax.broadcasted_iota(I32, (tk, LANES), 1) < IDX_DIM
    iw = iw_ref[...] * (IDX_DIM ** -0.5 * IDX_HEADS ** -0.5)
    for h in range(IDX_HEADS):
        iwb_ref[h] = jnp.broadcast_to(iw[:, IKW_IW + h:IKW_IW + h + 1], (tq, LANES))
    iq_pairs = [iq_ref[:, p * LANES:(p + 1) * LANES] for p in range(IDX_HEADS // 2)]

    def idx_block(kb, carry):
        ik_lo = jnp.where(low_half, ik_ref[pl.ds(pl.multiple_of(kb * tk, tk), tk), :], 0.0)
        ik_sides = (ik_lo.astype(BF16), pltpu.roll(ik_lo, IDX_DIM, axis=1).astype(BF16))
        acc = jnp.zeros((tq, tk), F32)
        for h in range(IDX_HEADS):
            s = lax.dot_general(iq_pairs[h // 2], ik_sides[h % 2], (((1,), (1,)), ((), ())),
                                preferred_element_type=F32)
            acc = acc + jnp.maximum(s, 0.0) * _lane_tile(iwb_ref[h], tk)
        causal = (col + kb * tk) <= (row + i * tq)
        keys_ref[kb] = jnp.where(causal, _sortable_key(acc), jnp.int32(INT_MIN))
        return carry

    lax.fori_loop(0, nkb, idx_block, 0)

    fb = RADIX4_FIELD_BITS
    field_mask = (1 << fb) - 1
    inc1, inc2, inc3 = 1, 1 + (1 << fb), 1 + (1 << fb) + (1 << 2 * fb)

    def count3(c1, c2, c3):
        b1, b2, b3 = (jnp.broadcast_to(c, (tq, LANES)) for c in (c1, c2, c3))

        def cb(kb, acc):
            keys = keys_ref[kb]
            for t in range(tk // LANES):
                kk = keys[:, t * LANES:(t + 1) * LANES]
                acc = acc + jnp.where(kk >= b3, inc3, jnp.where(kk >= b2, inc2, jnp.where(kk >= b1, inc1, 0)))
            return acc

        acc = lax.fori_loop(0, nkb, cb, jnp.zeros((tq, LANES), I32))
        n3 = (acc >> (2 * fb)).astype(F32).sum(axis=-1, keepdims=True)
        low = ((acc & field_mask).astype(F32) + ((acc >> fb) & field_mask).astype(F32) * RADIX4_WIDE)
        low = low.sum(axis=-1, keepdims=True)
        n2 = jnp.floor(low * (1.0 / RADIX4_WIDE))
        return low - RADIX4_WIDE * n2, n2, n3

    n_valid = (lax.broadcasted_iota(I32, (tq, 1), 0) + (i * tq + 1)).astype(F32)
    thr = _kth_largest_key_radix4(count3, tq, k_sel, n_valid)
    thr = jnp.maximum(thr, jnp.int32(INT_MIN + 1))
    thr_b = _lane_tile(jnp.broadcast_to(thr, (tq, LANES)), tk)

    def mask_block(kb, carry):
        mb_ref[kb] = jnp.where(keys_ref[kb] >= thr_b, 0.0, MASK_NEG)
        return carry

    lax.fori_loop(0, nkb, mask_block, 0)

    scale = HEAD_DIM ** -0.5
    for g in range(N_KV_HEADS):
        qg = jnp.concatenate(
            [q_ref[:, (KV_REP * g + r) * HEAD_DIM:(KV_REP * g + r + 1) * HEAD_DIM] for r in range(KV_REP)], axis=0)
        m_ref[...] = jnp.full(m_ref.shape, MASK_NEG, F32)
        acc_ref[...] = jnp.zeros(acc_ref.shape, F32)

        def block(kb, bias_idx, g=g, qg=qg):
            r0 = pl.multiple_of(kb * tk, tk)
            kblk = kv_ref[pl.ds(r0, tk), KV_K + g * HEAD_DIM:KV_K + (g + 1) * HEAD_DIM].astype(BF16)
            vblk = kv_ref[pl.ds(r0, tk), KV_V + g * HEAD_DIM:KV_V + (g + 1) * HEAD_DIM].astype(BF16)
            s = lax.dot_general(qg, kblk, (((1,), (1,)), ((), ())), preferred_element_type=F32) * scale
            s = s.reshape(KV_REP, tq, tk) + mb_ref[kb][None]
            if bias_idx is not None:
                s = s + bias_ref[bias_idx, KV_REP * g:KV_REP * (g + 1)]
            s = s.reshape(KV_REP * tq, tk)
            m_old = m_ref[...]
            m_new = jnp.maximum(m_old, s.max(axis=-1, keepdims=True))
            p = jnp.exp(s - _lane_tile(m_new, tk))
            alpha = jnp.exp(m_old - m_new)
            v_ones = jnp.concatenate([vblk, jnp.ones((tk, LANES), BF16)], axis=1)
            acc_ref[...] = (_lane_tile(alpha, HEAD_DIM + LANES) * acc_ref[...]
                            + jnp.dot(p.astype(BF16), v_ones, preferred_element_type=F32))
            m_ref[...] = m_new

        n_far = jnp.maximum(i - 1, 0)

        def far_pair(t, carry):
            block(2 * t, None)
            block(2 * t + 1, None)
            return carry

        lax.fori_loop(0, n_far // 2, far_pair, 0)

        @pl.when(n_far % 2 == 1)
        def _():
            block(n_far - 1, None)

        @pl.when(i >= 1)
        def _():
            block(i - 1, 1)
            block(i, 0)

        @pl.when(i == 0)
        def _():
            block(i, 0)
        out = acc_ref[:, :HEAD_DIM] / acc_ref[:, HEAD_DIM:]
        for r in range(KV_REP):
            h = KV_REP * g + r
            o_ref[:, h * HEAD_DIM:(h + 1) * HEAD_DIM] = out[r * tq:(r + 1) * tq].astype(o_ref.dtype)


def _attn_prompt(ha, kv, ikw, bias_tiles, batch, seq, tq):
    nq = seq // tq
    k_sel = min(TOPK_MAX, seq // 4)
    assert seq // LANES < (1 << RADIX4_FIELD_BITS)
    body = functools.partial(_attn_prompt_body, tq=tq, k_sel=k_sel)
    return pl.pallas_call(
        body,
        grid=(batch, nq),
        in_specs=[pl.BlockSpec((tq, QKV_W), lambda b, i: (b * nq + i, HA_Q // QKV_W)),
                  pl.BlockSpec((tq, IQ_W), lambda b, i: (b * nq + i, HA_IQ // IQ_W)),
                  pl.BlockSpec((tq, LANES), lambda b, i: (b * nq + i, 0)),
                  pl.BlockSpec((seq, 2 * KV_W), lambda b, i: (b, 0), pipeline_mode=pl.Buffered(1)),
                  pl.BlockSpec((seq, LANES), lambda b, i: (b, 0), pipeline_mode=pl.Buffered(1)),
                  pl.BlockSpec((2, N_HEADS, tq, tq), lambda b, i: (0, 0, 0, 0), pipeline_mode=pl.Buffered(1))],
        out_specs=pl.BlockSpec((tq, QKV_W), lambda b, i: (b * nq + i, 0)),
        out_shape=jax.ShapeDtypeStruct((batch * seq, QKV_W), BF16),
        scratch_shapes=[pltpu.VMEM((nq, tq, tq), I32),
                        pltpu.VMEM((nq, tq, tq), F32),
                        pltpu.VMEM((IDX_HEADS, tq, LANES), F32),
                        pltpu.VMEM((KV_REP * tq, LANES), F32),
                        pltpu.VMEM((KV_REP * tq, HEAD_DIM + LANES), F32)],
        compiler_params=_cp(("arbitrary", "arbitrary")),
        name="attn_prompt",
    )(ha, ha, ikw, kv, ikw, bias_tiles)


def _page_spec(block, layer, group, slot):
    return pl.BlockSpec((None, None) + block, lambda b, p, pt: (layer, pt[b, p * group + slot], 0, 0))


def _idx_sample_body(pt_ref, iq_ref, iw_ref, *refs, n_tok, group):
    pages, o_ref = refs[:group], refs[group]
    ikt = jnp.concatenate([r[...].astype(BF16) for r in pages], axis=1)
    s = jnp.dot(iq_ref[0], ikt, preferred_element_type=F32)
    r = jnp.maximum(s, 0.0) * _lane_tile(iw_ref[0], group * PAGE_SIZE)
    tok_scores = r.reshape(n_tok, IDX_HEADS, group * PAGE_SIZE).sum(axis=1)
    for t in range(group):
        o_ref[0, t] = tok_scores[:, t * PAGE_SIZE:(t + 1) * PAGE_SIZE]


def _idx_sample(page_table, iq_rows, iw_rows, cache_ik_t, layer, n_tok, group):
    db, n_pages = page_table.shape
    rows = n_tok * IDX_HEADS
    body = functools.partial(_idx_sample_body, n_tok=n_tok, group=group)
    grid_spec = pltpu.PrefetchScalarGridSpec(
        num_scalar_prefetch=1,
        grid=(db, n_pages // group),
        in_specs=[pl.BlockSpec((1, rows, IDX_DIM), lambda b, p, pt: (b, 0, 0)),
                  pl.BlockSpec((1, rows, LANES), lambda b, p, pt: (b, 0, 0))]
        + [_page_spec((IDX_DIM, PAGE_SIZE), layer, group, t) for t in range(group)],
        out_specs=pl.BlockSpec((1, group, n_tok, PAGE_SIZE), lambda b, p, pt: (b, p, 0, 0)),
    )
    return pl.pallas_call(
        body,
        grid_spec=grid_spec,
        out_shape=jax.ShapeDtypeStruct((db, n_pages, n_tok, PAGE_SIZE), F32),
        compiler_params=_cp(("arbitrary", "arbitrary")),
        name="idx_sample",
    )(page_table, iq_rows, iw_rows, *([cache_ik_t] * group))


def _attn_sample_body(pt_ref, sc_ref, scn_ref, q_ref, kn_ref, vn_ref, bl_ref, bn_ref, *refs,
                      n_tok, n_pages, k_sel, group):
    kp_refs, vp_refs = refs[:group], refs[group:2 * group]
    o_ref, thr_ref, m_ref, l_ref, acc_ref = refs[2 * group:]
    p = pl.program_id(1)
    n_steps = n_pages // group
    grp = KV_REP * n_tok
    width = group * PAGE_SIZE
    scale = HEAD_DIM ** -0.5
    r_i = lax.broadcasted_iota(I32, (n_tok, LANES), 0)
    c_i = lax.broadcasted_iota(I32, (n_tok, LANES), 1)
    keys_new = jnp.where(c_i <= r_i, _sortable_key(scn_ref[0]), jnp.int32(INT_MIN))

    @pl.when(p == 0)
    def _():
        keys = _sortable_key(sc_ref[0])

        def count_ge(cand):
            cand_b = jnp.broadcast_to(cand, (n_tok, LANES))
            c = jnp.where(keys >= cand_b[None], 1.0, 0.0).sum(axis=0) + jnp.where(keys_new >= cand_b, 1.0, 0.0)
            return c.sum(axis=-1, keepdims=True)

        thr = _kth_largest_key(count_ge, n_tok, float(k_sel))
        thr_ref[...] = jnp.broadcast_to(jnp.maximum(thr, jnp.int32(INT_MIN + 1)), (n_tok, LANES))
        m_ref[...] = jnp.full(m_ref.shape, MASK_NEG, F32)
        l_ref[...] = jnp.zeros(l_ref.shape, F32)
        acc_ref[...] = jnp.zeros(acc_ref.shape, F32)

    thr_b = thr_ref[...]

    def update(g, s, vblk):
        sl = slice(g * grp, (g + 1) * grp)
        m_old = m_ref[sl, :]
        m_new = jnp.maximum(m_old, s.max(axis=-1, keepdims=True))
        pr = jnp.exp(s - _lane_tile(m_new, s.shape[1]))
        alpha = jnp.exp(m_old - m_new)
        l_ref[sl, :] = alpha * l_ref[sl, :] + pr.sum(axis=-1, keepdims=True)
        acc_ref[sl, :] = alpha * acc_ref[sl, :] + jnp.dot(pr.astype(BF16), vblk, preferred_element_type=F32)
        m_ref[sl, :] = m_new

    def group_rows(mask_tok):
        return jnp.concatenate([mask_tok] * KV_REP, axis=0)

    base = p * group
    mb = jnp.concatenate(
        [jnp.where(_sortable_key(sc_ref[0, base + t]) >= thr_b, 0.0, MASK_NEG) for t in range(group)], axis=1)
    mb_g = group_rows(mb)
    is_last = p == n_steps - 1
    zeros_head = jnp.zeros((grp, width - PAGE_SIZE), F32)
    for g in range(N_KV_HEADS):
        qg = q_ref[0, g * grp:(g + 1) * grp, :]
        kg = jnp.concatenate([r[pl.ds(g, PAGE_SIZE, stride=N_KV_HEADS), :].astype(BF16) for r in kp_refs], axis=0)
        vg = jnp.concatenate([r[pl.ds(g, PAGE_SIZE, stride=N_KV_HEADS), :].astype(BF16) for r in vp_refs], axis=0)
        s = lax.dot_general(qg, kg, (((1,), (1,)), ((), ())), preferred_element_type=F32) * scale
        near = jnp.where(is_last, bl_ref[g * grp:(g + 1) * grp, :], 0.0)
        s = s + mb_g + (near if group == 1 else jnp.concatenate([zeros_head, near], axis=1))
        update(g, s, vg)

    @pl.when(is_last)
    def _():
        mbn_g = group_rows(jnp.where(keys_new >= thr_b, 0.0, MASK_NEG))
        for g in range(N_KV_HEADS):
            qg = q_ref[0, g * grp:(g + 1) * grp, :]
            kblk = kn_ref[0, :, g * HEAD_DIM:(g + 1) * HEAD_DIM].astype(BF16)
            vblk = vn_ref[0, :, g * HEAD_DIM:(g + 1) * HEAD_DIM].astype(BF16)
            s = lax.dot_general(qg, kblk, (((1,), (1,)), ((), ())), preferred_element_type=F32) * scale
            s = s + mbn_g + bn_ref[g * grp:(g + 1) * grp, :]
            update(g, s, vblk)
        o_ref[0] = acc_ref[...] / l_ref[...]


def _attn_sample(page_table, scores, scores_new, q_rows, k_new, v_new, bias_last, bias_new,
                 cache_k, cache_v, layer, n_tok, group):
    db, n_pages = page_table.shape
    past = n_pages * PAGE_SIZE
    rows = N_HEADS * n_tok
    k_sel = min(TOPK_MAX, (past + n_tok) // 4)
    body = functools.partial(_attn_sample_body, n_tok=n_tok, n_pages=n_pages, k_sel=k_sel, group=group)
    page = (PAGE_SIZE * N_KV_HEADS, HEAD_DIM)
    grid_spec = pltpu.PrefetchScalarGridSpec(
        num_scalar_prefetch=1,
        grid=(db, n_pages // group),
        in_specs=[pl.BlockSpec((1, n_pages, n_tok, PAGE_SIZE), lambda b, p, pt: (b, 0, 0, 0)),
                  pl.BlockSpec((1, n_tok, LANES), lambda b, p, pt: (b, 0, 0)),
                  pl.BlockSpec((1, rows, HEAD_DIM), lambda b, p, pt: (b, 0, 0)),
                  pl.BlockSpec((1, LANES, KV_W), lambda b, p, pt: (b, 0, 0)),
                  pl.BlockSpec((1, LANES, KV_W), lambda b, p, pt: (b, 0, 0)),
                  pl.BlockSpec((rows, LANES), lambda b, p, pt: (0, 0)),
                  pl.BlockSpec((rows, LANES), lambda b, p, pt: (0, 0))]
        + [_page_spec(page, layer, group, t) for t in range(group)]
        + [_page_spec(page, layer, group, t) for t in range(group)],
        out_specs=pl.BlockSpec((1, rows, HEAD_DIM), lambda b, p, pt: (b, 0, 0)),
        scratch_shapes=[pltpu.VMEM((n_tok, LANES), I32),
                        pltpu.VMEM((rows, LANES), F32),
                        pltpu.VMEM((rows, LANES), F32),
                        pltpu.VMEM((rows, HEAD_DIM), F32)],
    )
    return pl.pallas_call(
        body,
        grid_spec=grid_spec,
        out_shape=jax.ShapeDtypeStruct((db, rows, HEAD_DIM), F32),
        compiler_params=_cp(("arbitrary", "arbitrary")),
        name="attn_sample",
    )(page_table, scores, scores_new, q_rows, k_new, v_new, bias_last, bias_new,
      *([cache_k] * group), *([cache_v] * group))


def _idx_new_body(iq_ref, iw_ref, ik_ref, o_ref, *, n_tok):
    for b in range(iq_ref.shape[0]):
        s = lax.dot_general(iq_ref[b], ik_ref[b].astype(BF16), (((1,), (1,)), ((), ())),
                            preferred_element_type=F32)
        r = jnp.maximum(s, 0.0) * iw_ref[b]
        o_ref[b] = r.reshape(n_tok, IDX_HEADS, LANES).sum(axis=1)


def _idx_new(iq_rows, iw_rows, ik_new, n_tok):
    db = iq_rows.shape[0]
    return pl.pallas_call(
        functools.partial(_idx_new_body, n_tok=n_tok),
        out_shape=jax.ShapeDtypeStruct((db, n_tok, LANES), F32),
        name="idx_new",
    )(iq_rows, iw_rows, ik_new)


HALO = CONV_WIDTH - 1
HALO_PAD = 32
CONV_ROW_BLOCK = 128


CONV_CHUNKS = CONV_DIM // LANES


def _conv_prompt_body(glu_ref, cw_ref, cb_ref, g_ref, b_ref, o_ref, st_ref, hp_ref, c_ref, sh_ref, *, ts, ns):
    s_idx = pl.program_id(1)

    @pl.when(s_idx == 0)
    def _():
        hp_ref[:, 0:HALO_PAD, :] = jnp.zeros((CONV_CHUNKS, HALO_PAD, LANES), F32)

    for c in range(CONV_CHUNKS):
        cs = slice(c * LANES, (c + 1) * LANES)
        a = glu_ref[:, cs].astype(F32)
        gt = glu_ref[:, CONV_DIM + c * LANES:CONV_DIM + (c + 1) * LANES].astype(F32)
        hp_ref[c, HALO_PAD:HALO_PAD + ts, :] = a * _sigmoid(gt)
    off = HALO_PAD - HALO
    rows = min(ts, CONV_ROW_BLOCK)

    def chunk(c, carry):
        for r0 in range(0, ts, rows):
            acc = jnp.zeros((rows // F32_SUBLANES, F32_SUBLANES, LANES), F32)
            for shift in range(F32_SUBLANES):
                taps = range(shift, CONV_WIDTH, F32_SUBLANES)
                span = rows + taps[-1] - shift
                sh_ref[0:span, :] = hp_ref[c, off + r0 + shift:off + r0 + shift + span, :]
                for w in taps:
                    x = sh_ref[w - shift:w - shift + rows, :].reshape(rows // F32_SUBLANES, F32_SUBLANES, LANES)
                    acc = acc + x * cw_ref[c, w][None]
            c_ref[c, r0:r0 + rows, :] = acc.reshape(rows, LANES)
        return carry

    lax.fori_loop(0, CONV_CHUNKS, chunk, 0)

    conv = [c_ref[c] + cb_ref[:, c * LANES:(c + 1) * LANES] for c in range(CONV_CHUNKS)]
    mu = sum(v.sum(axis=-1, keepdims=True) for v in conv) / CONV_DIM
    var = sum(((v - mu) * (v - mu)).sum(axis=-1, keepdims=True) for v in conv) / CONV_DIM
    inv = lax.rsqrt(var + LN_EPS)
    for c in range(CONV_CHUNKS):
        cs = slice(c * LANES, (c + 1) * LANES)
        y = (conv[c] - mu) * inv * g_ref[:, cs] + b_ref[:, cs]
        o_ref[:, cs] = (y * _sigmoid(y)).astype(o_ref.dtype)

    @pl.when(s_idx == ns - 1)
    def _():
        for c in range(CONV_CHUNKS):
            st_ref[0, :, c * LANES:(c + 1) * LANES] = hp_ref[c, ts + off:ts + HALO_PAD, :]

    hp_ref[:, off:HALO_PAD, :] = hp_ref[:, ts + off:ts + HALO_PAD, :]


def _conv_prompt(ha, conv_w, conv_b, ln_g, ln_b, batch, seq, ts):
    ns = seq // ts
    body = functools.partial(_conv_prompt_body, ts=ts, ns=ns)
    vec = pl.BlockSpec((1, CONV_DIM), lambda b, s: (0, 0))
    return pl.pallas_call(
        body,
        grid=(batch, ns),
        in_specs=[pl.BlockSpec((ts, 2 * CONV_DIM), lambda b, s: (b * ns + s, HA_GLU // (2 * CONV_DIM))),
                  pl.BlockSpec((CONV_CHUNKS, CONV_WIDTH, F32_SUBLANES, LANES), lambda b, s: (0, 0, 0, 0)),
                  vec, vec, vec],
        out_specs=[pl.BlockSpec((ts, CONV_DIM), lambda b, s: (b * ns + s, 0)),
                   pl.BlockSpec((1, HALO, CONV_DIM), lambda b, s: (b, 0, 0))],
        out_shape=[jax.ShapeDtypeStruct((batch * seq, CONV_DIM), BF16),
                   jax.ShapeDtypeStruct((batch, HALO, CONV_DIM), F32)],
        scratch_shapes=[pltpu.VMEM((CONV_CHUNKS, HALO_PAD + ts, LANES), F32),
                        pltpu.VMEM((CONV_CHUNKS, ts, LANES), F32),
                        pltpu.VMEM((HALO_PAD + min(ts, CONV_ROW_BLOCK), LANES), F32)],
        compiler_params=_cp(("arbitrary", "arbitrary")),
        name="conv_prompt",
    )(ha, conv_w, conv_b, ln_g, ln_b)


def _sgu_prompt_body(uv_ref, g_ref, b_ref, w_ref, bs_ref, o_ref, *, ts):
    u = uv_ref[:, :SGU_DIM].astype(F32)
    vn = _ln_rows(uv_ref[:, SGU_DIM:].astype(F32), g_ref[...], b_ref[...]).astype(BF16)
    r_i = lax.broadcasted_iota(I32, (CHUNK, CHUNK), 0)
    c_i = lax.broadcasted_iota(I32, (CHUNK, CHUNK), 1)
    gd = SGU_DIM // SGU_GROUPS
    for g in range(SGU_GROUPS):
        wm = jnp.where(c_i <= r_i, w_ref[g], 0.0).astype(BF16)
        bias = bs_ref[:, g:g + 1]
        for c in range(ts // CHUNK):
            rs = slice(c * CHUNK, (c + 1) * CHUNK)
            gs = slice(g * gd, (g + 1) * gd)
            mixed = jnp.dot(wm, vn[rs, gs], preferred_element_type=F32) + bias
            o_ref[rs, gs] = (u[rs, gs] * mixed).astype(o_ref.dtype)


def _sgu_prompt(ha, ln_g, ln_b, w_s, b_s_t, batch, seq, ts):
    ns = seq // ts
    vec = pl.BlockSpec((1, SGU_DIM), lambda b, s: (0, 0))
    return pl.pallas_call(
        functools.partial(_sgu_prompt_body, ts=ts),
        grid=(batch, ns),
        in_specs=[pl.BlockSpec((ts, 2 * SGU_DIM), lambda b, s: (b * ns + s, HA_UV // (2 * SGU_DIM))),
                  vec, vec,
                  pl.BlockSpec((SGU_GROUPS, CHUNK, CHUNK), lambda b, s: (0, 0, 0)),
                  pl.BlockSpec((CHUNK, SGU_GROUPS), lambda b, s: (0, 0))],
        out_specs=pl.BlockSpec((ts, SGU_DIM), lambda b, s: (b * ns + s, 0)),
        out_shape=jax.ShapeDtypeStruct((batch * seq, SGU_DIM), BF16),
        compiler_params=_cp(("arbitrary", "arbitrary")),
        name="sgu_prompt",
    )(ha, ln_g, ln_b, w_s, b_s_t)


def _mix_sample_body(glu_ref, uv_ref, st_ref, cw_ref, cb_ref, cg_ref, cbt_ref, sg_ref, sb_ref, wv_ref, bv_ref,
                     co_ref, nst_ref, so_ref, vn_ref, hp_ref, *, db, n_tok):
    cw = cw_ref[...]
    for b in range(db):
        rs = slice(b * n_tok, (b + 1) * n_tok)
        a = glu_ref[rs, :CONV_DIM].astype(F32)
        gt = glu_ref[rs, CONV_DIM:].astype(F32)
        hp_ref[0:HALO, :] = st_ref[b]
        hp_ref[HALO:HALO + n_tok, :] = a * _sigmoid(gt)
        rows = [jnp.sum(hp_ref[t:t + CONV_WIDTH, :] * cw, axis=0, keepdims=True) for t in range(n_tok)]
        c = jnp.concatenate(rows, axis=0) + cb_ref[...]
        y = _ln_rows(c, cg_ref[...], cbt_ref[...])
        co_ref[rs, :] = y * _sigmoid(y)
        nst_ref[b] = hp_ref[n_tok:n_tok + HALO, :]
        u = uv_ref[rs, :SGU_DIM].astype(F32)
        vn = _ln_rows(uv_ref[rs, SGU_DIM:].astype(F32), sg_ref[...], sb_ref[...])
        vn_ref[rs, :] = vn
        mixed = []
        for t in range(n_tok):
            acc = bv_ref[t:t + 1, :]
            for s in range(t + 1):
                acc = acc + wv_ref[t, s:s + 1, :] * vn[s:s + 1, :]
            mixed.append(acc)
        so_ref[rs, :] = u * jnp.concatenate(mixed, axis=0)


def _mix_sample(glu, uv, state, conv_w, conv_b, cln_g, cln_b, sln_g, sln_b, wv, bv, db, n_tok):
    rows = db * n_tok
    return pl.pallas_call(
        functools.partial(_mix_sample_body, db=db, n_tok=n_tok),
        out_shape=[jax.ShapeDtypeStruct((rows, CONV_DIM), F32),
                   jax.ShapeDtypeStruct((db, HALO, CONV_DIM), F32),
                   jax.ShapeDtypeStruct((rows, SGU_DIM), F32),
                   jax.ShapeDtypeStruct((rows, SGU_DIM), F32)],
        scratch_shapes=[pltpu.VMEM((_round_up(HALO + n_tok, 8), CONV_DIM), F32)],
        name="mix_sample",
    )(glu, uv, state, conv_w, conv_b, cln_g, cln_b, sln_g, sln_b, wv, bv)


def _router_body(x_ref, w_ref, b_ref, ti_ref, tg_ref, *, n_exp):
    logits = jnp.dot(x_ref[...], w_ref[...], preferred_element_type=F32,
                     precision=lax.Precision.HIGHEST) + b_ref[...]
    lane = lax.broadcasted_iota(I32, logits.shape, 1)
    lane_f = lane.astype(F32)
    neg = -jnp.inf
    lg = jnp.where(lane < n_exp, logits, neg)
    v1 = lg.max(axis=-1, keepdims=True)
    i1 = jnp.where(lg == v1, lane_f, float(LANES)).min(axis=-1, keepdims=True)
    lg2 = jnp.where(lane_f == i1, neg, lg)
    v2 = lg2.max(axis=-1, keepdims=True)
    i2 = jnp.where(lg2 == v2, lane_f, float(LANES)).min(axis=-1, keepdims=True)
    e = jnp.exp(v2 - v1)
    den = 1.0 + e
    ti_ref[...] = jnp.where(lane == 0, i1, jnp.where(lane == 1, i2, 0.0)).astype(I32)
    tg_ref[...] = jnp.where(lane == 0, 1.0 / den, jnp.where(lane == 1, e / den, 0.0))


def _router(x, w_pad, b_pad, n_exp, tm):
    m, d = x.shape
    return pl.pallas_call(
        functools.partial(_router_body, n_exp=n_exp),
        grid=(m // tm,),
        in_specs=[pl.BlockSpec((tm, d), lambda i: (i, 0)),
                  pl.BlockSpec((d, LANES), lambda i: (0, 0)),
                  pl.BlockSpec((1, LANES), lambda i: (0, 0))],
        out_specs=[pl.BlockSpec((tm, LANES), lambda i: (i, 0)),
                   pl.BlockSpec((tm, LANES), lambda i: (i, 0))],
        out_shape=[jax.ShapeDtypeStruct((m, LANES), I32), jax.ShapeDtypeStruct((m, LANES), F32)],
        compiler_params=_cp(("arbitrary",)),
        name="moe_router",
    )(x, w_pad, b_pad)


def _row_copy(src_hbm, dst, src_row, dst_row, sem):
    return pltpu.make_async_copy(src_hbm.at[pl.ds(src_row, 1)], dst.at[pl.ds(dst_row, 1)], sem)


ROW_DMA_UNROLL = 8
ROW_DMA_SLOTS = 3


def _prefetch_chunks(issue, wait):
    c = pl.program_id(0)
    ahead = ROW_DMA_SLOTS - 1

    @pl.when(c == 0)
    def _():
        for first in range(ahead):
            @pl.when(first < pl.num_programs(0))
            def _():
                issue(first, first)

    @pl.when(c + ahead < pl.num_programs(0))
    def _():
        issue(c + ahead, (c + ahead) % ROW_DMA_SLOTS)

    slot = c % ROW_DMA_SLOTS
    wait(slot)
    return slot


def _dispatch_body(src_ref, x_hbm, o_ref, buf, sem, *, rows):
    def issue(chunk, slot):
        def start(u, carry):
            for v in range(ROW_DMA_UNROLL):
                r = u * ROW_DMA_UNROLL + v
                _row_copy(x_hbm, buf.at[slot], src_ref[chunk * rows + r], r, sem.at[slot]).start()
            return carry

        lax.fori_loop(0, rows // ROW_DMA_UNROLL, start, 0)

    def wait(slot):
        def w(r, carry):
            _row_copy(x_hbm, buf.at[slot], 0, r, sem.at[slot]).wait()
            return carry

        lax.fori_loop(0, rows, w, 0)

    slot = _prefetch_chunks(issue, wait)
    o_ref[...] = buf[slot].astype(o_ref.dtype)


def _dispatch(src, x, n_slots, rows):
    d = x.shape[1]
    grid_spec = pltpu.PrefetchScalarGridSpec(
        num_scalar_prefetch=1,
        grid=(n_slots // rows,),
        in_specs=[pl.BlockSpec(memory_space=pl.ANY)],
        out_specs=pl.BlockSpec((rows, d), lambda i, s: (i, 0)),
        scratch_shapes=[pltpu.VMEM((ROW_DMA_SLOTS, rows, d), x.dtype), pltpu.SemaphoreType.DMA((ROW_DMA_SLOTS,))],
    )
    assert rows % ROW_DMA_UNROLL == 0
    return pl.pallas_call(
        functools.partial(_dispatch_body, rows=rows),
        grid_spec=grid_spec,
        out_shape=jax.ShapeDtypeStruct((n_slots, d), BF16),
        compiler_params=_cp(("arbitrary",)),
        name="moe_dispatch",
    )(src, x)


def _tile_copy(src, dst, sem):
    return pltpu.make_async_copy(src, dst, sem)


def _moe_ffn_body(exp_ref, row0_ref, nsub_ref, jmap_ref, nzero_ref, xs_hbm, w1_ref, w3_ref, w2_ref, ys_hbm,
                  xbuf, acc, w1b, w3b, w2b, sem, *, nj, sub):
    w = pl.program_id(0)
    j = pl.program_id(1)
    nsub = nsub_ref[w]
    row0 = row0_ref[w]
    nzero = nzero_ref[w]

    def zero_copy(t):
        r = pl.multiple_of(row0 + t * sub, sub)
        return _tile_copy(acc.at[pl.ds(0, sub)], ys_hbm.at[pl.ds(r, sub)], sem)

    @pl.when(jnp.logical_and(j == 0, nzero > 0))
    def _():
        acc[0:sub, :] = jnp.zeros((sub, acc.shape[1]), F32)

        def zs(t, carry):
            zero_copy(t).start()
            return carry

        def zw(t, carry):
            zero_copy(t).wait()
            return carry

        lax.fori_loop(0, nzero, zs, 0)
        lax.fori_loop(0, nzero, zw, 0)

    def in_copy(t):
        r = pl.multiple_of(t * sub, sub)
        return _tile_copy(xs_hbm.at[pl.ds(pl.multiple_of(row0 + r, sub), sub)], xbuf.at[pl.ds(r, sub)], sem)

    def out_copy(t):
        r = pl.multiple_of(t * sub, sub)
        return _tile_copy(acc.at[pl.ds(r, sub)], ys_hbm.at[pl.ds(pl.multiple_of(row0 + r, sub), sub)], sem)

    def for_tiles(fn):
        def it(t, carry):
            fn(t)
            return carry
        lax.fori_loop(0, nsub, it, 0)

    def zero_tile(t):
        acc[pl.ds(pl.multiple_of(t * sub, sub), sub), :] = jnp.zeros((sub, acc.shape[1]), F32)

    @pl.when(j == 0)
    def _():
        for_tiles(lambda t: in_copy(t).start())
        for_tiles(zero_tile)
        for_tiles(lambda t: in_copy(t).wait())

    def tile(t, n, w1v, w3v, w2v):
        r = pl.multiple_of(t * sub, sub)
        x = xbuf[pl.ds(r, n * sub), :]
        a = jnp.dot(x, w1v, preferred_element_type=F32)
        c = jnp.dot(x, w3v, preferred_element_type=F32)
        h = (a * _sigmoid(a) * c).astype(BF16)
        acc[pl.ds(r, n * sub), :] += jnp.dot(h, w2v, preferred_element_type=F32)

    def cast_weights():
        return w1_ref[...].astype(BF16), w3_ref[...].astype(BF16), w2_ref[...].astype(BF16)

    per_trip = 2 * MOE_WIDE
    n_trips = nsub // per_trip

    def trip(t, wv):
        tile(per_trip * t, MOE_WIDE, *wv)
        tile(per_trip * t + MOE_WIDE, MOE_WIDE, *wv)

    @pl.when(n_trips >= 1)
    def _():
        wv = cast_weights()
        w1b[...], w3b[...], w2b[...] = wv
        trip(0, wv)

    @pl.when(jnp.logical_and(n_trips == 0, nsub > 0))
    def _():
        w1b[...], w3b[...], w2b[...] = cast_weights()

    def later_trip(t, carry):
        trip(t, (w1b[...], w3b[...], w2b[...]))
        return carry

    lax.fori_loop(1, n_trips, later_trip, 0)

    def rest(t, carry):
        tile(t, 1, w1b[...], w3b[...], w2b[...])
        return carry

    lax.fori_loop(n_trips * per_trip, nsub, rest, 0)

    @pl.when(j == nj - 1)
    def _():
        for_tiles(lambda t: out_copy(t).start())
        for_tiles(lambda t: out_copy(t).wait())


def _moe_ffn(item_exp, item_row0, item_nsub, item_jlast, item_nzero, xs, w1, w3, w2, layer, n_items, super_rows,
             sub, tf):
    n_slots, d = xs.shape
    f = w1.shape[3]
    nj = f // tf

    def jj(w, j, jl):
        return jnp.where(jl[w] > 0, j, nj - 1)

    grid_spec = pltpu.PrefetchScalarGridSpec(
        num_scalar_prefetch=5,
        grid=(n_items, nj),
        in_specs=[pl.BlockSpec(memory_space=pl.ANY),
                  pl.BlockSpec((None, None, d, tf), lambda w, j, ex, r0, ns, jl, nz: (layer, ex[w], 0, jj(w, j, jl))),
                  pl.BlockSpec((None, None, d, tf), lambda w, j, ex, r0, ns, jl, nz: (layer, ex[w], 0, jj(w, j, jl))),
                  pl.BlockSpec((None, None, tf, d), lambda w, j, ex, r0, ns, jl, nz: (layer, ex[w], jj(w, j, jl), 0))],
        out_specs=pl.BlockSpec(memory_space=pl.ANY),
        scratch_shapes=[pltpu.VMEM((super_rows, d), BF16),
                        pltpu.VMEM((super_rows, d), F32),
                        pltpu.VMEM((d, tf), BF16),
                        pltpu.VMEM((d, tf), BF16),
                        pltpu.VMEM((tf, d), BF16),
                        pltpu.SemaphoreType.DMA(())],
    )
    return pl.pallas_call(
        functools.partial(_moe_ffn_body, nj=nj, sub=sub),
        grid_spec=grid_spec,
        out_shape=jax.ShapeDtypeStruct((n_slots, d), F32),
        compiler_params=_cp(("arbitrary", "arbitrary")),
        name="moe_ffn",
    )(item_exp, item_row0, item_nsub, item_jlast, item_nzero, xs, w1, w3, w2)


def _combine_body(pos_ref, ys_hbm, tg_ref, x_ref, g_ref, b_ref, of_ref, ob_ref, buf, sem, *, tb, alpha):
    def issue(chunk, slot):
        def start(u, carry):
            for v in range(ROW_DMA_UNROLL // TOP_K_EXPERTS):
                r = u * (ROW_DMA_UNROLL // TOP_K_EXPERTS) + v
                for k in range(TOP_K_EXPERTS):
                    p = pos_ref[TOP_K_EXPERTS * (chunk * tb + r) + k]
                    _row_copy(ys_hbm, buf.at[slot, k], p, r, sem.at[slot]).start()
            return carry

        lax.fori_loop(0, tb * TOP_K_EXPERTS // ROW_DMA_UNROLL, start, 0)

    def wait(slot):
        def w(r, carry):
            for k in range(TOP_K_EXPERTS):
                _row_copy(ys_hbm, buf.at[slot, k], 0, r, sem.at[slot]).wait()
            return carry

        lax.fori_loop(0, tb, w, 0)

    slot = _prefetch_chunks(issue, wait)
    f = tg_ref[:, 0:1] * buf[slot, 0] + tg_ref[:, 1:2] * buf[slot, 1]
    y = _ln_rows(alpha * x_ref[...] + f, g_ref[...], b_ref[...])
    of_ref[...] = y
    ob_ref[...] = y.astype(BF16)


def _combine_ln(pos, ys, tg, x, g, b, alpha, tb):
    m, d = x.shape
    grid_spec = pltpu.PrefetchScalarGridSpec(
        num_scalar_prefetch=1,
        grid=(m // tb,),
        in_specs=[pl.BlockSpec(memory_space=pl.ANY),
                  pl.BlockSpec((tb, LANES), lambda i, p: (i, 0)),
                  pl.BlockSpec((tb, d), lambda i, p: (i, 0)),
                  pl.BlockSpec((1, d), lambda i, p: (0, 0)),
                  pl.BlockSpec((1, d), lambda i, p: (0, 0))],
        out_specs=[pl.BlockSpec((tb, d), lambda i, p: (i, 0)),
                   pl.BlockSpec((tb, d), lambda i, p: (i, 0))],
        scratch_shapes=[pltpu.VMEM((ROW_DMA_SLOTS, TOP_K_EXPERTS, tb, d), F32),
                        pltpu.SemaphoreType.DMA((ROW_DMA_SLOTS,))],
    )
    assert tb * TOP_K_EXPERTS % ROW_DMA_UNROLL == 0
    return pl.pallas_call(
        functools.partial(_combine_body, tb=tb, alpha=alpha),
        grid_spec=grid_spec,
        out_shape=[jax.ShapeDtypeStruct((m, d), F32), jax.ShapeDtypeStruct((m, d), BF16)],
        compiler_params=_cp(("arbitrary",)),
        name="moe_combine_ln",
    )(pos, ys, tg, x, g, b)


def _moe_plan(top_i, n_real, n_exp, n_slots, n_items, super_rows, sub):
    a = top_i.shape[0] * TOP_K_EXPERTS
    e_flat = top_i.reshape(-1)
    tok = jnp.arange(a, dtype=I32) // TOP_K_EXPERTS
    real = tok < n_real
    onehot = jnp.logical_and(e_flat[:, None] == jnp.arange(n_exp, dtype=I32)[None, :], real[:, None]).astype(I32)
    csum = jnp.cumsum(onehot, axis=0)
    rank = jnp.take_along_axis(csum, e_flat[:, None], axis=1)[:, 0] - 1
    counts = csum[-1]
    padded = (counts + sub - 1) // sub * sub
    gstart = jnp.cumsum(padded) - padded
    pos = jnp.where(real, gstart[e_flat] + rank, 0).astype(I32)
    src = jnp.zeros((n_slots,), I32).at[jnp.where(real, pos, n_slots)].set(tok, mode="drop")
    n_super = (padded + super_rows - 1) // super_rows
    iend = jnp.cumsum(n_super)
    istart = iend - n_super
    total = iend[-1]
    wi = jnp.arange(n_items, dtype=I32)
    valid = wi < total
    e_w = jnp.minimum(jnp.searchsorted(iend, wi, side="right"), n_exp - 1).astype(I32)
    last_e = e_w[jnp.maximum(total - 1, 0)]
    e_w = jnp.where(valid, e_w, last_e)
    k_w = wi - istart[e_w]
    subs_e = padded[e_w] // sub
    items_e = jnp.maximum(n_super[e_w], 1)
    base, extra = subs_e // items_e, subs_e % items_e
    nsub = jnp.where(valid, base + (k_w < extra).astype(I32), 0).astype(I32)
    first_sub = k_w * base + jnp.minimum(k_w, extra)
    z0 = jnp.sum(padded) + (wi - total) * super_rows
    nzero = jnp.where(valid, 0, jnp.clip((n_slots - z0) // sub, 0, super_rows // sub)).astype(I32)
    row0 = jnp.where(valid, gstart[e_w] + first_sub * sub, jnp.where(nzero > 0, z0, 0)).astype(I32)
    return pos, src, e_w, row0, nsub, valid.astype(I32), nzero


def _rel_bucket(dist):
    n = jnp.maximum(dist, 0)
    max_exact = REL_BUCKETS // 2
    nf = jnp.maximum(n, 1).astype(F32)
    large = max_exact + (jnp.log(nf / max_exact) / math.log(REL_MAX_DIST / max_exact)
                         * (REL_BUCKETS - max_exact)).astype(I32)
    large = jnp.minimum(large, REL_BUCKETS - 1)
    return jnp.where(n < max_exact, n, large)


def _shifted_bias(rel_bias, dist):
    t = rel_bias[_rel_bucket(dist)] - rel_bias[REL_BUCKETS - 1]
    t = jnp.where((dist >= 0)[..., None], t, 0.0)
    return jnp.moveaxis(t, -1, 0)


def _pick_tile(m_rows, target):
    return max(t for t in range(16, target + 1, 16) if m_rows % t == 0)


def _pad_rows(x, rows):
    return jnp.pad(x, ((0, rows - x.shape[0]), (0, 0)))


def kernel(x_prompt, x_sample, cache_k, cache_v, cache_ik, state_conv, page_table, w_in, conv_w, conv_b,
           conv_ln_g, conv_ln_b, sgu_ln_g, sgu_ln_b, sgu_w, sgu_b, w_pa, w_pb, w_pc, w_out, ln1_g, ln1_b,
           ln2_g, ln2_b, rel_bias, ffn_w1, ffn_w3, ffn_w2, moe_router, moe_router_b, moe_w1, moe_w3, moe_w2):
    batch, seq, d_model = x_prompt.shape
    db, n_tok, _ = x_sample.shape
    depth = w_in.shape[0]
    n_pool = cache_k.shape[1]
    n_exp = moe_router.shape[2]
    mp, ms = batch * seq, db * n_tok
    assert ms <= SAMPLE_TILE and mp % SAMPLE_TILE == 0
    m_all = mp + SAMPLE_TILE
    tm = m_all // N_ROW_TILES
    assert tm * N_ROW_TILES == m_all and tm % 16 == 0
    alpha = (2 * depth) ** 0.25
    tq = min(256, seq)

    xf = jnp.concatenate([x_prompt.reshape(mp, d_model), x_sample.reshape(ms, d_model),
                          jnp.zeros((m_all - mp - ms, d_model), F32)], axis=0)
    xb = xf.astype(BF16)
    w_in_t = jnp.swapaxes(w_in, 1, 2)
    ha_rows = ([W_Q_ROW0 + HA_TILE * t for t in range(QKV_W // HA_TILE)]
               + [W_GLU_ROW0 + HA_TILE * t for t in range(2 * CONV_DIM // HA_TILE)]
               + [W_UV_ROW0 + HA_TILE * t for t in range(2 * SGU_DIM // HA_TILE)]
               + [W_GATE_ROW0 + HA_TILE * t for t in range(N_BRANCH * d_model // HA_TILE)]
               + [W_IQ_ROW0 + HA_TILE * t for t in range(IQ_W // HA_TILE)])
    w_pa_b, w_pb_b, w_pc_b, w_out_b = (w.astype(BF16) for w in (w_pa, w_pb, w_pc, w_out))
    ffn_w2_b = ffn_w2.astype(BF16)

    kk = jnp.arange(2 * tq, dtype=I32)
    d_wrap = jnp.where(kk < tq, -kk, 2 * tq - kk)

    def toeplitz(offset):
        v = _shifted_bias(rel_bias, d_wrap + offset)
        return jnp.tile(v, (1, tq))[:, :tq * (2 * tq - 1)].reshape(N_HEADS, tq, 2 * tq - 1)[:, :, :tq]

    bias_tiles = jnp.stack([toeplitz(0), toeplitz(tq)])
    tok = jnp.arange(n_tok, dtype=I32)
    lane = jnp.arange(LANES, dtype=I32)
    d_last = PAGE_SIZE + tok[:, None] - lane[None, :]
    bias_last = _shifted_bias(rel_bias, d_last).reshape(N_HEADS * n_tok, LANES)
    bias_new = _shifted_bias(rel_bias, tok[:, None] - lane[None, :]).reshape(N_HEADS * n_tok, LANES)

    ck = cache_k.reshape(depth, n_pool, PAGE_SIZE * N_KV_HEADS, HEAD_DIM)
    cv = cache_v.reshape(depth, n_pool, PAGE_SIZE * N_KV_HEADS, HEAD_DIM)
    cik_t = jnp.swapaxes(cache_ik, 2, 3)
    n_pages = page_table.shape[1]
    idx_group = math.gcd(n_pages, IDX_PAGE_GROUP)
    attn_group = math.gcd(n_pages, ATTN_PAGE_GROUP)
    n_slots = _round_up(TOP_K_EXPERTS * (mp + ms) + n_exp * (MOE_SUB - 1), MOE_SUB)
    n_items = -(-n_slots // MOE_SUPER) + n_exp

    outs = {k: [] for k in ("k", "v", "ik", "conv_p", "conv_s", "sgu_s")}
    for l in range(depth):
        ha = _in_proj(xb, w_in_t, l, ha_rows, tm, HA_TILE)
        kv, ikw, k4, v4 = _kv_proj(xb, w_in_t, l, tm)

        a_p = _attn_prompt(ha, kv, ikw, bias_tiles, batch, seq, tq)
        conv_w_rep = jnp.broadcast_to(conv_w[l].reshape(CONV_WIDTH, CONV_CHUNKS, 1, LANES).transpose(1, 0, 2, 3),
                                      (CONV_CHUNKS, CONV_WIDTH, F32_SUBLANES, LANES))
        c_p, conv_state_p = _conv_prompt(ha, conv_w_rep, conv_b[l][None], conv_ln_g[l][None], conv_ln_b[l][None],
                                         batch, seq, min(256, seq))
        s_p = _sgu_prompt(ha, sgu_ln_g[l][None], sgu_ln_b[l][None], sgu_w[l], sgu_b[l].T, batch, seq,
                          min(512, seq))

        has, kvs, ikws = ha[mp:mp + ms], kv[mp:mp + ms], ikw[mp:mp + ms]
        iq_rows = has[:, HA_IQ:HA_IQ + IQ_W].reshape(db, n_tok * IDX_HEADS, IDX_DIM)
        iw_rows = (ikws[:, IKW_IW:IKW_IW + IDX_HEADS] * (IDX_DIM ** -0.5 * IDX_HEADS ** -0.5)
                   ).reshape(db, n_tok * IDX_HEADS, 1)
        iw_rows = jnp.broadcast_to(iw_rows, (db, n_tok * IDX_HEADS, LANES))
        scores = _idx_sample(page_table, iq_rows, iw_rows, cik_t, l, n_tok, idx_group)
        pad_rows = LANES - n_tok
        ik_new = jnp.pad(ikws[:, :IDX_DIM].reshape(db, n_tok, IDX_DIM), ((0, 0), (0, pad_rows), (0, 0)))
        scores_new = _idx_new(iq_rows, iw_rows, ik_new, n_tok)
        q_rows = has[:, HA_Q:HA_Q + QKV_W].reshape(db, n_tok, N_HEADS, HEAD_DIM).transpose(0, 2, 1, 3)
        q_rows = q_rows.reshape(db, N_HEADS * n_tok, HEAD_DIM)
        k_new = jnp.pad(kvs[:, KV_K:KV_K + KV_W].reshape(db, n_tok, KV_W), ((0, 0), (0, pad_rows), (0, 0)))
        v_new = jnp.pad(kvs[:, KV_V:KV_V + KV_W].reshape(db, n_tok, KV_W), ((0, 0), (0, pad_rows), (0, 0)))
        a_s = _attn_sample(page_table, scores, scores_new, q_rows, k_new, v_new, bias_last, bias_new,
                           ck, cv, l, n_tok, attn_group)
        a_s = a_s.reshape(db, N_HEADS, n_tok, HEAD_DIM).transpose(0, 2, 1, 3).reshape(ms, QKV_W)
        gd = SGU_DIM // SGU_GROUPS
        wv = jnp.repeat(sgu_w[l][:, :n_tok, :n_tok].transpose(1, 2, 0), gd, axis=-1)
        bv = jnp.repeat(sgu_b[l][:, :n_tok].T, gd, axis=-1)
        c_s, conv_state_s, s_s, vn_s = _mix_sample(
            has[:, HA_GLU:HA_GLU + 2 * CONV_DIM], has[:, HA_UV:HA_UV + 2 * SGU_DIM], state_conv[l],
            conv_w[l], conv_b[l][None], conv_ln_g[l][None], conv_ln_b[l][None],
            sgu_ln_g[l][None], sgu_ln_b[l][None], wv, bv, db, n_tok)

        sample_rows = tuple(_pad_rows(t.astype(BF16), SAMPLE_TILE) for t in (a_s, c_s, s_s))
        merged = _merge((a_p, c_p, s_p), sample_rows, ha, w_pa_b, w_pb_b, w_pc_b, l, SAMPLE_TILE)
        x1f, x1b = _mm_ln(merged, w_out_b, l, xf, ln1_g[l][None], ln1_b[l][None], alpha, _pick_tile(m_all, 384))

        j = l // 2
        if l % 2 == 0:
            h = _swiglu_up(x1b, ffn_w1, ffn_w3, j, tm, 512)
            xf, xb = _mm_ln(h, ffn_w2_b, j, x1f, ln2_g[l][None], ln2_b[l][None], alpha, _pick_tile(m_all, 384))
        else:
            w_r = jnp.pad(moe_router[j], ((0, 0), (0, LANES - n_exp)))
            b_r = jnp.pad(moe_router_b[j], (0, LANES - n_exp))[None]
            ti, tg = _router(x1f, w_r, b_r, n_exp, _pick_tile(m_all, 528))
            pos, src, it_e, it_r0, it_ns, it_valid, it_nz = _moe_plan(ti[:, :TOP_K_EXPERTS], mp + ms, n_exp,
                                                                      n_slots, n_items, MOE_SUPER, MOE_SUB)
            xs = _dispatch(src, x1f, n_slots, MOE_SUB)
            ys = _moe_ffn(it_e, it_r0, it_ns, it_valid, it_nz, xs, moe_w1, moe_w3, moe_w2, j, n_items,
                          MOE_SUPER, MOE_SUB, MOE_TF)
            tb = COMBINE_TB if m_all % COMBINE_TB == 0 else 16
            xf, xb = _combine_ln(pos, ys, tg, x1f, ln2_g[l][None], ln2_b[l][None], alpha, tb)

        outs["k"].append(k4)
        outs["v"].append(v4)
        outs["ik"].append(ikw[:, :IDX_DIM])
        outs["conv_p"].append(conv_state_p)
        outs["conv_s"].append(conv_state_s)
        outs["sgu_s"].append(vn_s.reshape(db, n_tok, SGU_DIM))

    def split(name, rows_per_token, width_shape):
        st = jnp.stack(outs[name])
        p = st[:, :mp * rows_per_token].reshape((depth, batch, seq) + width_shape)
        s = st[:, mp * rows_per_token:(mp + ms) * rows_per_token].reshape((depth, db, n_tok) + width_shape)
        return p, s

    k_p, k_s = split("k", N_KV_HEADS, (N_KV_HEADS, HEAD_DIM))
    v_p, v_s = split("v", N_KV_HEADS, (N_KV_HEADS, HEAD_DIM))
    ik_p, ik_s = split("ik", 1, (IDX_DIM,))
    y_prompt = xf[:mp].reshape(batch, seq, d_model)
    y_sample = xf[mp:mp + ms].reshape(db, n_tok, d_model)
    return (y_prompt, y_sample, k_p, v_p, ik_p, jnp.stack(outs["conv_p"]), k_s, v_s, ik_s,
            jnp.stack(outs["conv_s"]), jnp.stack(outs["sgu_s"]))
```

```python
import functools
import math

import jax
import jax.numpy as jnp
import numpy as np
from jax import lax
from jax.experimental import pallas as pl
from jax.experimental.pallas import tpu as pltpu

F32 = jnp.float32
BF16 = jnp.bfloat16
I32 = jnp.int32

N_HEADS = 16
HEAD_DIM = 128
N_KV_HEADS = 4
KV_REP = N_HEADS // N_KV_HEADS
IDX_HEADS = 16
IDX_DIM = 64
TOPK_MAX = 256
PAGE_SIZE = 128
CONV_DIM = 1024
CONV_WIDTH = 31
SGU_DIM = 1024
SGU_GROUPS = 8
CHUNK = 128
N_BRANCH = 3
TOP_K_EXPERTS = 2
REL_BUCKETS = 32
REL_MAX_DIST = 128
LN_EPS = 1e-5

V7X_VMEM_BYTES = 64 * 2**20
VMEM_LIMIT = V7X_VMEM_BYTES - 8 * 2**20
LANES = 128
F32_SUBLANES = 8
INT_MIN = -(2**31)
MASK_NEG = -1e30

QKV_W = N_HEADS * HEAD_DIM
KV_W = N_KV_HEADS * HEAD_DIM
IQ_W = IDX_HEADS * IDX_DIM
HA_Q, HA_GLU, HA_UV, HA_GATE, HA_IQ = 0, 2048, 4096, 6144, 12288
HA_TILE = 1024
W_Q_ROW0 = 0
W_KV_ROW0 = W_Q_ROW0 + QKV_W
W_IQ_ROW0 = W_KV_ROW0 + 2 * KV_W
W_IK_ROW0 = W_IQ_ROW0 + IQ_W
W_GLU_ROW0 = W_IK_ROW0 + IDX_DIM + IDX_HEADS
W_UV_ROW0 = W_GLU_ROW0 + 2 * CONV_DIM
W_GATE_ROW0 = W_UV_ROW0 + 2 * SGU_DIM
KV_K, KV_V = 0, KV_W
IKW_IW = IDX_DIM

N_ROW_TILES = 8
SAMPLE_TILE = 256
MOE_SUB = 256
MOE_SUPER = 2048
MOE_TF = 256
MOE_WIDE = 2
COMBINE_TB = 128
IDX_PAGE_GROUP = 32
ATTN_PAGE_GROUP = 32


def _cp(sem, vmem=VMEM_LIMIT):
    return pltpu.CompilerParams(dimension_semantics=sem, vmem_limit_bytes=vmem)


def _round_up(x, m):
    return (x + m - 1) // m * m


def _ln_rows(z, g, b):
    mu = jnp.mean(z, axis=-1, keepdims=True)
    d = z - mu
    var = jnp.mean(d * d, axis=-1, keepdims=True)
    return d * lax.rsqrt(var + LN_EPS) * g + b


def _sigmoid(x):
    return 1.0 / (1.0 + jnp.exp(-x))


def _in_proj_body(off_ref, x_ref, w_ref, o_ref, wt_ref):
    @pl.when(pl.program_id(1) == 0)
    def _():
        wt_ref[...] = w_ref[0].T.astype(BF16)

    o_ref[...] = jnp.dot(x_ref[...], wt_ref[...], preferred_element_type=F32).astype(o_ref.dtype)


def _in_proj(x, w_t, layer, row_offsets, tm, tn):
    m, k = x.shape
    n_tiles = len(row_offsets)
    grid_spec = pltpu.PrefetchScalarGridSpec(
        num_scalar_prefetch=1,
        grid=(n_tiles, m // tm),
        in_specs=[pl.BlockSpec((tm, k), lambda j, i, off: (i, 0)),
                  pl.BlockSpec((pl.Element(1), pl.Element(tn), pl.Element(k)),
                               lambda j, i, off: (layer, pl.multiple_of(off[j], F32_SUBLANES), 0))],
        out_specs=pl.BlockSpec((tm, tn), lambda j, i, off: (i, j)),
        scratch_shapes=[pltpu.VMEM((k, tn), BF16)],
    )
    return pl.pallas_call(
        _in_proj_body,
        grid_spec=grid_spec,
        out_shape=jax.ShapeDtypeStruct((m, n_tiles * tn), BF16),
        compiler_params=_cp(("arbitrary", "arbitrary")),
        name="in_proj",
    )(jnp.asarray(row_offsets, I32), x, w_t)


def _kv_proj_body(x_ref, wkv_ref, wik_ref, kv_ref, ik_ref, k4_ref, v4_ref, wkv_t, wik_t):
    @pl.when(pl.program_id(0) == 0)
    def _():
        wkv_t[...] = wkv_ref[...].T.astype(BF16)
        wik_t[...] = wik_ref[...].T.astype(BF16)

    x = x_ref[...]
    tm = x.shape[0]
    kv = jnp.dot(x, wkv_t[...], preferred_element_type=F32)
    kv_ref[...] = kv
    ik_ref[...] = jnp.dot(x, wik_t[...], preferred_element_type=F32)
    for g in range(N_KV_HEADS):
        k4_ref[pl.ds(g, tm, stride=N_KV_HEADS), :] = kv[:, KV_K + g * HEAD_DIM:KV_K + (g + 1) * HEAD_DIM]
        v4_ref[pl.ds(g, tm, stride=N_KV_HEADS), :] = kv[:, KV_V + g * HEAD_DIM:KV_V + (g + 1) * HEAD_DIM]


def _kv_proj(x, w_t, layer, tm):
    m, k = x.shape
    assert W_KV_ROW0 % (2 * KV_W) == 0 and W_IK_ROW0 % LANES == 0
    row = lambda i: (i, 0)
    per_head = jax.ShapeDtypeStruct((m * N_KV_HEADS, HEAD_DIM), F32)
    return pl.pallas_call(
        _kv_proj_body,
        grid=(m // tm,),
        in_specs=[pl.BlockSpec((tm, k), row),
                  pl.BlockSpec((None, 2 * KV_W, k), lambda i: (layer, W_KV_ROW0 // (2 * KV_W), 0)),
                  pl.BlockSpec((None, LANES, k), lambda i: (layer, W_IK_ROW0 // LANES, 0))],
        out_specs=[pl.BlockSpec((tm, 2 * KV_W), row), pl.BlockSpec((tm, LANES), row),
                   pl.BlockSpec((tm * N_KV_HEADS, HEAD_DIM), row), pl.BlockSpec((tm * N_KV_HEADS, HEAD_DIM), row)],
        out_shape=[jax.ShapeDtypeStruct((m, 2 * KV_W), F32), jax.ShapeDtypeStruct((m, LANES), F32),
                   per_head, per_head],
        scratch_shapes=[pltpu.VMEM((k, 2 * KV_W), BF16), pltpu.VMEM((k, LANES), BF16)],
        compiler_params=_cp(("arbitrary",)),
        name="kv_proj",
    )(x, w_t, w_t)


def _row_halves(rows):
    half = rows // 2 // 16 * 16
    return (slice(0, half), slice(half, rows)) if half > 0 else (slice(0, rows),)


def _mm_ln_body(x_ref, w_ref, r_ref, g_ref, b_ref, of_ref, ob_ref, *, alpha):
    for rs in _row_halves(x_ref.shape[0]):
        z = jnp.dot(x_ref[rs, :], w_ref[...], preferred_element_type=F32) + alpha * r_ref[rs, :]
        y = _ln_rows(z, g_ref[...], b_ref[...])
        of_ref[rs, :] = y
        ob_ref[rs, :] = y.astype(BF16)


def _mm_ln(x, w, layer, resid, g, b, alpha, tm):
    m, k = x.shape
    n = w.shape[2]
    row = lambda i: (i, 0)
    return pl.pallas_call(
        functools.partial(_mm_ln_body, alpha=alpha),
        grid=(m // tm,),
        in_specs=[pl.BlockSpec((tm, k), row),
                  pl.BlockSpec((None, k, n), lambda i: (layer, 0, 0), pipeline_mode=pl.Buffered(1)),
                  pl.BlockSpec((tm, n), row),
                  pl.BlockSpec((1, n), lambda i: (0, 0)),
                  pl.BlockSpec((1, n), lambda i: (0, 0))],
        out_specs=[pl.BlockSpec((tm, n), row), pl.BlockSpec((tm, n), row)],
        out_shape=[jax.ShapeDtypeStruct((m, n), F32), jax.ShapeDtypeStruct((m, n), BF16)],
        compiler_params=_cp(("arbitrary",)),
        name="proj_ln",
    )(x, w, resid, g, b)


def _swiglu_up_body(x_ref, w1_ref, w3_ref, o_ref, w1b, w3b):
    @pl.when(pl.program_id(1) == 0)
    def _():
        w1b[...] = w1_ref[...].astype(BF16)
        w3b[...] = w3_ref[...].astype(BF16)

    for rs in _row_halves(x_ref.shape[0]):
        x = x_ref[rs, :]
        a = jnp.dot(x, w1b[...], preferred_element_type=F32)
        c = jnp.dot(x, w3b[...], preferred_element_type=F32)
        o_ref[rs, :] = (a * _sigmoid(a) * c).astype(o_ref.dtype)


def _swiglu_up(x, w1, w3, layer, tm, tf):
    m, k = x.shape
    f = w1.shape[2]
    return pl.pallas_call(
        _swiglu_up_body,
        grid=(f // tf, m // tm),
        in_specs=[pl.BlockSpec((tm, k), lambda j, i: (i, 0)),
                  pl.BlockSpec((None, k, tf), lambda j, i: (layer, 0, j)),
                  pl.BlockSpec((None, k, tf), lambda j, i: (layer, 0, j))],
        out_specs=pl.BlockSpec((tm, tf), lambda j, i: (i, j)),
        out_shape=jax.ShapeDtypeStruct((m, f), BF16),
        scratch_shapes=[pltpu.VMEM((k, tf), BF16), pltpu.VMEM((k, tf), BF16)],
        compiler_params=_cp(("arbitrary", "arbitrary")),
        name="ffn_up",
    )(x, w1, w3)


def _merge_body(ap_ref, cp_ref, sp_ref, as_ref, cs_ref, ss_ref, g_ref, wa_ref, wb_ref, wc_ref, o_ref, *, n_prompt):
    i = pl.program_id(0)
    d = o_ref.shape[1]

    def merge(a_ref, c_ref, s_ref):
        for rs in _row_halves(o_ref.shape[0]):
            pa = jnp.dot(a_ref[rs, :], wa_ref[...], preferred_element_type=F32)
            pb = jnp.dot(c_ref[rs, :], wb_ref[...], preferred_element_type=F32)
            pc = jnp.dot(s_ref[rs, :], wc_ref[...], preferred_element_type=F32)
            o = (_sigmoid(g_ref[rs, :d].astype(F32)) * pa + _sigmoid(g_ref[rs, d:2 * d].astype(F32)) * pb
                 + _sigmoid(g_ref[rs, 2 * d:].astype(F32)) * pc)
            o_ref[rs, :] = o.astype(o_ref.dtype)

    @pl.when(i < n_prompt)
    def _():
        merge(ap_ref, cp_ref, sp_ref)

    @pl.when(i >= n_prompt)
    def _():
        merge(as_ref, cs_ref, ss_ref)


def _merge(prompt, sample, ha, w_pa, w_pb, w_pc, layer, tm):
    m = ha.shape[0]
    d = w_pa.shape[2]
    n_prompt = prompt[0].shape[0] // tm
    assert prompt[0].shape[0] % tm == 0 and sample[0].shape[0] == tm and m == (n_prompt + 1) * tm
    assert HA_GATE % (N_BRANCH * d) == 0
    p_row = lambda i: (jnp.minimum(i, n_prompt - 1), 0)
    fixed = lambda i: (0, 0)
    return pl.pallas_call(
        functools.partial(_merge_body, n_prompt=n_prompt),
        grid=(n_prompt + 1,),
        in_specs=[pl.BlockSpec((tm, t.shape[1]), p_row) for t in prompt]
        + [pl.BlockSpec((tm, t.shape[1]), fixed) for t in sample]
        + [pl.BlockSpec((tm, N_BRANCH * d), lambda i: (i, HA_GATE // (N_BRANCH * d)))]
        + [pl.BlockSpec((None,) + w.shape[1:], lambda i: (layer, 0, 0), pipeline_mode=pl.Buffered(1))
           for w in (w_pa, w_pb, w_pc)],
        out_specs=pl.BlockSpec((tm, d), lambda i: (i, 0)),
        out_shape=jax.ShapeDtypeStruct((m, d), BF16),
        compiler_params=_cp(("arbitrary",)),
        name="branch_merge",
    )(*prompt, *sample, ha, w_pa, w_pb, w_pc)


def _sortable_key(score):
    bits = pltpu.bitcast(score, I32)
    return bits ^ ((bits >> 31) & jnp.int32(0x7FFFFFFF))


def _kth_largest_key(count_ge, rows, k):
    def step(it, t):
        cand = t + jnp.left_shift(jnp.int32(1), 31 - it)
        return jnp.where(count_ge(cand) >= k, cand, t)

    return lax.fori_loop(0, 32, step, jnp.full((rows, 1), INT_MIN, I32))


RADIX4_STEPS = 16
RADIX4_FIELD_BITS = 5
RADIX4_WIDE = 4096.0


def _kth_largest_key_radix4(count3, rows, k, n_valid):
    kf = float(k)

    def cond(state):
        it, _, _, pending = state
        return jnp.logical_and(it < RADIX4_STEPS, pending > 0.0)

    def body(state):
        it, t, cnt, _ = state
        one = jnp.left_shift(jnp.int32(1), 30 - 2 * it)
        c1 = t + one
        c2 = c1 + one
        c3 = c2 + one
        n1, n2, n3 = count3(c1, c2, c3)
        t = jnp.where(n3 >= kf, c3, jnp.where(n2 >= kf, c2, jnp.where(n1 >= kf, c1, t)))
        cnt = jnp.where(n3 >= kf, n3, jnp.where(n2 >= kf, n2, jnp.where(n1 >= kf, n1, cnt)))
        settled = jnp.logical_or(cnt == kf, n_valid <= kf)
        return it + 1, t, cnt, jnp.max(jnp.where(settled, 0.0, 1.0))

    init = (jnp.int32(0), jnp.full((rows, 1), INT_MIN, I32), jnp.full((rows, 1), -1.0, F32), jnp.float32(1.0))
    return lax.while_loop(cond, body, init)[1]


def _lane_tile(x, width):
    return x if width == LANES else jnp.concatenate([x] * (width // LANES), axis=1)


def _attn_prompt_body(q_ref, iq_ref, iw_ref, kv_ref, ik_ref, bias_ref, o_ref,
                      keys_ref, mb_ref, iwb_ref, m_ref, acc_ref, *, tq, k_sel):
    i = pl.program_id(1)
    nkb = i + 1
    tk = tq
    row = lax.broadcasted_iota(I32, (tq, tk), 0)
    col = lax.broadcasted_iota(I32, (tq, tk), 1)
    low_half = lax.broadcasted_iota(I32, (tk, LANES), 1) < IDX_DIM
    iw = iw_ref[...] * (IDX_DIM ** -0.5 * IDX_HEADS ** -0.5)
    for h in range(IDX_HEADS):
        iwb_ref[h] = jnp.broadcast_to(iw[:, IKW_IW + h:IKW_IW + h + 1], (tq, LANES))
    iq_pairs = [iq_ref[:, p * LANES:(p + 1) * LANES] for p in range(IDX_HEADS // 2)]

    def idx_block(kb, carry):
        ik_lo = jnp.where(low_half, ik_ref[pl.ds(pl.multiple_of(kb * tk, tk), tk), :], 0.0)
        ik_sides = (ik_lo.astype(BF16), pltpu.roll(ik_lo, IDX_DIM, axis=1).astype(BF16))
        acc = jnp.zeros((tq, tk), F32)
        for h in range(IDX_HEADS):
            s = lax.dot_general(iq_pairs[h // 2], ik_sides[h % 2], (((1,), (1,)), ((), ())),
                                preferred_element_type=F32)
            acc = acc + jnp.maximum(s, 0.0) * _lane_tile(iwb_ref[h], tk)
        causal = (col + kb * tk) <= (row + i * tq)
        keys_ref[kb] = jnp.where(causal, _sortable_key(acc), jnp.int32(INT_MIN))
        return carry

    lax.fori_loop(0, nkb, idx_block, 0)

    fb = RADIX4_FIELD_BITS
    field_mask = (1 << fb) - 1
    inc1, inc2, inc3 = 1, 1 + (1 << fb), 1 + (1 << fb) + (1 << 2 * fb)

    def count3(c1, c2, c3):
        b1, b2, b3 = (jnp.broadcast_to(c, (tq, LANES)) for c in (c1, c2, c3))

        def cb(kb, acc):
            keys = keys_ref[kb]
            for t in range(tk // LANES):
                kk = keys[:, t * LANES:(t + 1) * LANES]
                acc = acc + jnp.where(kk >= b3, inc3, jnp.where(kk >= b2, inc2, jnp.where(kk >= b1, inc1, 0)))
            return acc

        acc = lax.fori_loop(0, nkb, cb, jnp.zeros((tq, LANES), I32))
        n3 = (acc >> (2 * fb)).astype(F32).sum(axis=-1, keepdims=True)
        low = ((acc & field_mask).astype(F32) + ((acc >> fb) & field_mask).astype(F32) * RADIX4_WIDE)
        low = low.sum(axis=-1, keepdims=True)
        n2 = jnp.floor(low * (1.0 / RADIX4_WIDE))
        return low - RADIX4_WIDE * n2, n2, n3

    n_valid = (lax.broadcasted_iota(I32, (tq, 1), 0) + (i * tq + 1)).astype(F32)
    thr = _kth_largest_key_radix4(count3, tq, k_sel, n_valid)
    thr = jnp.maximum(thr, jnp.int32(INT_MIN + 1))
    thr_b = _lane_tile(jnp.broadcast_to(thr, (tq, LANES)), tk)

    def mask_block(kb, carry):
        mb_ref[kb] = jnp.where(keys_ref[kb] >= thr_b, 0.0, MASK_NEG)
        return carry

    lax.fori_loop(0, nkb, mask_block, 0)

    scale = HEAD_DIM ** -0.5
    for g in range(N_KV_HEADS):
        qg = jnp.concatenate(
            [q_ref[:, (KV_REP * g + r) * HEAD_DIM:(KV_REP * g + r + 1) * HEAD_DIM] for r in range(KV_REP)], axis=0)
        m_ref[...] = jnp.full(m_ref.shape, MASK_NEG, F32)
        acc_ref[...] = jnp.zeros(acc_ref.shape, F32)

        def block(kb, bias_idx, g=g, qg=qg):
            r0 = pl.multiple_of(kb * tk, tk)
            kblk = kv_ref[pl.ds(r0, tk), KV_K + g * HEAD_DIM:KV_K + (g + 1) * HEAD_DIM].astype(BF16)
            vblk = kv_ref[pl.ds(r0, tk), KV_V + g * HEAD_DIM:KV_V + (g + 1) * HEAD_DIM].astype(BF16)
            s = lax.dot_general(qg, kblk, (((1,), (1,)), ((), ())), preferred_element_type=F32) * scale
            s = s.reshape(KV_REP, tq, tk) + mb_ref[kb][None]
            if bias_idx is not None:
                s = s + bias_ref[bias_idx, KV_REP * g:KV_REP * (g + 1)]
            s = s.reshape(KV_REP * tq, tk)
            m_old = m_ref[...]
            m_new = jnp.maximum(m_old, s.max(axis=-1, keepdims=True))
            p = jnp.exp(s - _lane_tile(m_new, tk))
            alpha = jnp.exp(m_old - m_new)
            v_ones = jnp.concatenate([vblk, jnp.ones((tk, LANES), BF16)], axis=1)
            acc_ref[...] = (_lane_tile(alpha, HEAD_DIM + LANES) * acc_ref[...]
                            + jnp.dot(p.astype(BF16), v_ones, preferred_element_type=F32))
            m_ref[...] = m_new

        n_far = jnp.maximum(i - 1, 0)

        def far_pair(t, carry):
            block(2 * t, None)
            block(2 * t + 1, None)
            return carry

        lax.fori_loop(0, n_far // 2, far_pair, 0)

        @pl.when(n_far % 2 == 1)
        def _():
            block(n_far - 1, None)

        @pl.when(i >= 1)
        def _():
            block(i - 1, 1)
            block(i, 0)

        @pl.when(i == 0)
        def _():
            block(i, 0)
        out = acc_ref[:, :HEAD_DIM] / acc_ref[:, HEAD_DIM:]
        for r in range(KV_REP):
            h = KV_REP * g + r
            o_ref[:, h * HEAD_DIM:(h + 1) * HEAD_DIM] = out[r * tq:(r + 1) * tq].astype(o_ref.dtype)


def _attn_prompt(ha, kv, ikw, bias_tiles, batch, seq, tq):
    nq = seq // tq
    k_sel = min(TOPK_MAX, seq // 4)
    assert seq // LANES < (1 << RADIX4_FIELD_BITS)
    body = functools.partial(_attn_prompt_body, tq=tq, k_sel=k_sel)
    return pl.pallas_call(
        body,
        grid=(batch, nq),
        in_specs=[pl.BlockSpec((tq, QKV_W), lambda b, i: (b * nq + i, HA_Q // QKV_W)),
                  pl.BlockSpec((tq, IQ_W), lambda b, i: (b * nq + i, HA_IQ // IQ_W)),
                  pl.BlockSpec((tq, LANES), lambda b, i: (b * nq + i, 0)),
                  pl.BlockSpec((seq, 2 * KV_W), lambda b, i: (b, 0), pipeline_mode=pl.Buffered(1)),
                  pl.BlockSpec((seq, LANES), lambda b, i: (b, 0), pipeline_mode=pl.Buffered(1)),
                  pl.BlockSpec((2, N_HEADS, tq, tq), lambda b, i: (0, 0, 0, 0), pipeline_mode=pl.Buffered(1))],
        out_specs=pl.BlockSpec((tq, QKV_W), lambda b, i: (b * nq + i, 0)),
        out_shape=jax.ShapeDtypeStruct((batch * seq, QKV_W), BF16),
        scratch_shapes=[pltpu.VMEM((nq, tq, tq), I32),
                        pltpu.VMEM((nq, tq, tq), F32),
                        pltpu.VMEM((IDX_HEADS, tq, LANES), F32),
                        pltpu.VMEM((KV_REP * tq, LANES), F32),
                        pltpu.VMEM((KV_REP * tq, HEAD_DIM + LANES), F32)],
        compiler_params=_cp(("arbitrary", "arbitrary")),
        name="attn_prompt",
    )(ha, ha, ikw, kv, ikw, bias_tiles)


def _page_spec(block, layer, group, slot):
    return pl.BlockSpec((None, None) + block, lambda b, p, pt: (layer, pt[b, p * group + slot], 0, 0))


def _idx_sample_body(pt_ref, iq_ref, iw_ref, *refs, n_tok, group):
    pages, o_ref = refs[:group], refs[group]
    ikt = jnp.concatenate([r[...].astype(BF16) for r in pages], axis=1)
    s = jnp.dot(iq_ref[0], ikt, preferred_element_type=F32)
    r = jnp.maximum(s, 0.0) * _lane_tile(iw_ref[0], group * PAGE_SIZE)
    tok_scores = r.reshape(n_tok, IDX_HEADS, group * PAGE_SIZE).sum(axis=1)
    for t in range(group):
        o_ref[0, t] = tok_scores[:, t * PAGE_SIZE:(t + 1) * PAGE_SIZE]


def _idx_sample(page_table, iq_rows, iw_rows, cache_ik_t, layer, n_tok, group):
    db, n_pages = page_table.shape
    rows = n_tok * IDX_HEADS
    body = functools.partial(_idx_sample_body, n_tok=n_tok, group=group)
    grid_spec = pltpu.PrefetchScalarGridSpec(
        num_scalar_prefetch=1,
        grid=(db, n_pages // group),
        in_specs=[pl.BlockSpec((1, rows, IDX_DIM), lambda b, p, pt: (b, 0, 0)),
                  pl.BlockSpec((1, rows, LANES), lambda b, p, pt: (b, 0, 0))]
        + [_page_spec((IDX_DIM, PAGE_SIZE), layer, group, t) for t in range(group)],
        out_specs=pl.BlockSpec((1, group, n_tok, PAGE_SIZE), lambda b, p, pt: (b, p, 0, 0)),
    )
    return pl.pallas_call(
        body,
        grid_spec=grid_spec,
        out_shape=jax.ShapeDtypeStruct((db, n_pages, n_tok, PAGE_SIZE), F32),
        compiler_params=_cp(("arbitrary", "arbitrary")),
        name="idx_sample",
    )(page_table, iq_rows, iw_rows, *([cache_ik_t] * group))


def _attn_sample_body(pt_ref, sc_ref, scn_ref, q_ref, kn_ref, vn_ref, bl_ref, bn_ref, *refs,
                      n_tok, n_pages, k_sel, group):
    kp_refs, vp_refs = refs[:group], refs[group:2 * group]
    o_ref, thr_ref, m_ref, l_ref, acc_ref = refs[2 * group:]
    p = pl.program_id(1)
    n_steps = n_pages // group
    grp = KV_REP * n_tok
    width = group * PAGE_SIZE
    scale = HEAD_DIM ** -0.5
    r_i = lax.broadcasted_iota(I32, (n_tok, LANES), 0)
    c_i = lax.broadcasted_iota(I32, (n_tok, LANES), 1)
    keys_new = jnp.where(c_i <= r_i, _sortable_key(scn_ref[0]), jnp.int32(INT_MIN))

    @pl.when(p == 0)
    def _():
        keys = _sortable_key(sc_ref[0])

        def count_ge(cand):
            cand_b = jnp.broadcast_to(cand, (n_tok, LANES))
            c = jnp.where(keys >= cand_b[None], 1.0, 0.0).sum(axis=0) + jnp.where(keys_new >= cand_b, 1.0, 0.0)
            return c.sum(axis=-1, keepdims=True)

        thr = _kth_largest_key(count_ge, n_tok, float(k_sel))
        thr_ref[...] = jnp.broadcast_to(jnp.maximum(thr, jnp.int32(INT_MIN + 1)), (n_tok, LANES))
        m_ref[...] = jnp.full(m_ref.shape, MASK_NEG, F32)
        l_ref[...] = jnp.zeros(l_ref.shape, F32)
        acc_ref[...] = jnp.zeros(acc_ref.shape, F32)

    thr_b = thr_ref[...]

    def update(g, s, vblk):
        sl = slice(g * grp, (g + 1) * grp)
        m_old = m_ref[sl, :]
        m_new = jnp.maximum(m_old, s.max(axis=-1, keepdims=True))
        pr = jnp.exp(s - _lane_tile(m_new, s.shape[1]))
        alpha = jnp.exp(m_old - m_new)
        l_ref[sl, :] = alpha * l_ref[sl, :] + pr.sum(axis=-1, keepdims=True)
        acc_ref[sl, :] = alpha * acc_ref[sl, :] + jnp.dot(pr.astype(BF16), vblk, preferred_element_type=F32)
        m_ref[sl, :] = m_new

    def group_rows(mask_tok):
        return jnp.concatenate([mask_tok] * KV_REP, axis=0)

    base = p * group
    mb = jnp.concatenate(
        [jnp.where(_sortable_key(sc_ref[0, base + t]) >= thr_b, 0.0, MASK_NEG) for t in range(group)], axis=1)
    mb_g = group_rows(mb)
    is_last = p == n_steps - 1
    zeros_head = jnp.zeros((grp, width - PAGE_SIZE), F32)
    for g in range(N_KV_HEADS):
        qg = q_ref[0, g * grp:(g + 1) * grp, :]
        kg = jnp.concatenate([r[pl.ds(g, PAGE_SIZE, stride=N_KV_HEADS), :].astype(BF16) for r in kp_refs], axis=0)
        vg = jnp.concatenate([r[pl.ds(g, PAGE_SIZE, stride=N_KV_HEADS), :].astype(BF16) for r in vp_refs], axis=0)
        s = lax.dot_general(qg, kg, (((1,), (1,)), ((), ())), preferred_element_type=F32) * scale
        near = jnp.where(is_last, bl_ref[g * grp:(g + 1) * grp, :], 0.0)
        s = s + mb_g + (near if group == 1 else jnp.concatenate([zeros_head, near], axis=1))
        update(g, s, vg)

    @pl.when(is_last)
    def _():
        mbn_g = group_rows(jnp.where(keys_new >= thr_b, 0.0, MASK_NEG))
        for g in range(N_KV_HEADS):
            qg = q_ref[0, g * grp:(g + 1) * grp, :]
            kblk = kn_ref[0, :, g * HEAD_DIM:(g + 1) * HEAD_DIM].astype(BF16)
            vblk = vn_ref[0, :, g * HEAD_DIM:(g + 1) * HEAD_DIM].astype(BF16)
            s = lax.dot_general(qg, kblk, (((1,), (1,)), ((), ())), preferred_element_type=F32) * scale
            s = s + mbn_g + bn_ref[g * grp:(g + 1) * grp, :]
            update(g, s, vblk)
        o_ref[0] = acc_ref[...] / l_ref[...]


def _attn_sample(page_table, scores, scores_new, q_rows, k_new, v_new, bias_last, bias_new,
                 cache_k, cache_v, layer, n_tok, group):
    db, n_pages = page_table.shape
    past = n_pages * PAGE_SIZE
    rows = N_HEADS * n_tok
    k_sel = min(TOPK_MAX, (past + n_tok) // 4)
    body = functools.partial(_attn_sample_body, n_tok=n_tok, n_pages=n_pages, k_sel=k_sel, group=group)
    page = (PAGE_SIZE * N_KV_HEADS, HEAD_DIM)
    grid_spec = pltpu.PrefetchScalarGridSpec(
        num_scalar_prefetch=1,
        grid=(db, n_pages // group),
        in_specs=[pl.BlockSpec((1, n_pages, n_tok, PAGE_SIZE), lambda b, p, pt: (b, 0, 0, 0)),
                  pl.BlockSpec((1, n_tok, LANES), lambda b, p, pt: (b, 0, 0)),
                  pl.BlockSpec((1, rows, HEAD_DIM), lambda b, p, pt: (b, 0, 0)),
                  pl.BlockSpec((1, LANES, KV_W), lambda b, p, pt: (b, 0, 0)),
                  pl.BlockSpec((1, LANES, KV_W), lambda b, p, pt: (b, 0, 0)),
                  pl.BlockSpec((rows, LANES), lambda b, p, pt: (0, 0)),
                  pl.BlockSpec((rows, LANES), lambda b, p, pt: (0, 0))]
        + [_page_spec(page, layer, group, t) for t in range(group)]
        + [_page_spec(page, layer, group, t) for t in range(group)],
        out_specs=pl.BlockSpec((1, rows, HEAD_DIM), lambda b, p, pt: (b, 0, 0)),
        scratch_shapes=[pltpu.VMEM((n_tok, LANES), I32),
                        pltpu.VMEM((rows, LANES), F32),
                        pltpu.VMEM((rows, LANES), F32),
                        pltpu.VMEM((rows, HEAD_DIM), F32)],
    )
    return pl.pallas_call(
        body,
        grid_spec=grid_spec,
        out_shape=jax.ShapeDtypeStruct((db, rows, HEAD_DIM), F32),
        compiler_params=_cp(("arbitrary", "arbitrary")),
        name="attn_sample",
    )(page_table, scores, scores_new, q_rows, k_new, v_new, bias_last, bias_new,
      *([cache_k] * group), *([cache_v] * group))


def _idx_new_body(iq_ref, iw_ref, ik_ref, o_ref, *, n_tok):
    for b in range(iq_ref.shape[0]):
        s = lax.dot_general(iq_ref[b], ik_ref[b].astype(BF16), (((1,), (1,)), ((), ())),
                            preferred_element_type=F32)
        r = jnp.maximum(s, 0.0) * iw_ref[b]
        o_ref[b] = r.reshape(n_tok, IDX_HEADS, LANES).sum(axis=1)


def _idx_new(iq_rows, iw_rows, ik_new, n_tok):
    db = iq_rows.shape[0]
    return pl.pallas_call(
        functools.partial(_idx_new_body, n_tok=n_tok),
        out_shape=jax.ShapeDtypeStruct((db, n_tok, LANES), F32),
        name="idx_new",
    )(iq_rows, iw_rows, ik_new)


HALO = CONV_WIDTH - 1
HALO_PAD = 32
CONV_ROW_BLOCK = 128


CONV_CHUNKS = CONV_DIM // LANES


def _conv_prompt_body(glu_ref, cw_ref, cb_ref, g_ref, b_ref, o_ref, st_ref, hp_ref, c_ref, sh_ref, *, ts, ns):
    s_idx = pl.program_id(1)

    @pl.when(s_idx == 0)
    def _():
        hp_ref[:, 0:HALO_PAD, :] = jnp.zeros((CONV_CHUNKS, HALO_PAD, LANES), F32)

    for c in range(CONV_CHUNKS):
        cs = slice(c * LANES, (c + 1) * LANES)
        a = glu_ref[:, cs].astype(F32)
        gt = glu_ref[:, CONV_DIM + c * LANES:CONV_DIM + (c + 1) * LANES].astype(F32)
        hp_ref[c, HALO_PAD:HALO_PAD + ts, :] = a * _sigmoid(gt)
    off = HALO_PAD - HALO
    rows = min(ts, CONV_ROW_BLOCK)

    def chunk(c, carry):
        for r0 in range(0, ts, rows):
            acc = jnp.zeros((rows // F32_SUBLANES, F32_SUBLANES, LANES), F32)
            for shift in range(F32_SUBLANES):
                taps = range(shift, CONV_WIDTH, F32_SUBLANES)
                span = rows + taps[-1] - shift
                sh_ref[0:span, :] = hp_ref[c, off + r0 + shift:off + r0 + shift + span, :]
                for w in taps:
                    x = sh_ref[w - shift:w - shift + rows, :].reshape(rows // F32_SUBLANES, F32_SUBLANES, LANES)
                    acc = acc + x * cw_ref[c, w][None]
            c_ref[c, r0:r0 + rows, :] = acc.reshape(rows, LANES)
        return carry

    lax.fori_loop(0, CONV_CHUNKS, chunk, 0)

    conv = [c_ref[c] + cb_ref[:, c * LANES:(c + 1) * LANES] for c in range(CONV_CHUNKS)]
    mu = sum(v.sum(axis=-1, keepdims=True) for v in conv) / CONV_DIM
    var = sum(((v - mu) * (v - mu)).sum(axis=-1, keepdims=True) for v in conv) / CONV_DIM
    inv = lax.rsqrt(var + LN_EPS)
    for c in range(CONV_CHUNKS):
        cs = slice(c * LANES, (c + 1) * LANES)
        y = (conv[c] - mu) * inv * g_ref[:, cs] + b_ref[:, cs]
        o_ref[:, cs] = (y * _sigmoid(y)).astype(o_ref.dtype)

    @pl.when(s_idx == ns - 1)
    def _():
        for c in range(CONV_CHUNKS):
            st_ref[0, :, c * LANES:(c + 1) * LANES] = hp_ref[c, ts + off:ts + HALO_PAD, :]

    hp_ref[:, off:HALO_PAD, :] = hp_ref[:, ts + off:ts + HALO_PAD, :]


def _conv_prompt(ha, conv_w, conv_b, ln_g, ln_b, batch, seq, ts):
    ns = seq // ts
    body = functools.partial(_conv_prompt_body, ts=ts, ns=ns)
    vec = pl.BlockSpec((1, CONV_DIM), lambda b, s: (0, 0))
    return pl.pallas_call(
        body,
        grid=(batch, ns),
        in_specs=[pl.BlockSpec((ts, 2 * CONV_DIM), lambda b, s: (b * ns + s, HA_GLU // (2 * CONV_DIM))),
                  pl.BlockSpec((CONV_CHUNKS, CONV_WIDTH, F32_SUBLANES, LANES), lambda b, s: (0, 0, 0, 0)),
                  vec, vec, vec],
        out_specs=[pl.BlockSpec((ts, CONV_DIM), lambda b, s: (b * ns + s, 0)),
                   pl.BlockSpec((1, HALO, CONV_DIM), lambda b, s: (b, 0, 0))],
        out_shape=[jax.ShapeDtypeStruct((batch * seq, CONV_DIM), BF16),
                   jax.ShapeDtypeStruct((batch, HALO, CONV_DIM), F32)],
        scratch_shapes=[pltpu.VMEM((CONV_CHUNKS, HALO_PAD + ts, LANES), F32),
                        pltpu.VMEM((CONV_CHUNKS, ts, LANES), F32),
                        pltpu.VMEM((HALO_PAD + min(ts, CONV_ROW_BLOCK), LANES), F32)],
        compiler_params=_cp(("arbitrary", "arbitrary")),
        name="conv_prompt",
    )(ha, conv_w, conv_b, ln_g, ln_b)


def _sgu_prompt_body(uv_ref, g_ref, b_ref, w_ref, bs_ref, o_ref, *, ts):
    u = uv_ref[:, :SGU_DIM].astype(F32)
    vn = _ln_rows(uv_ref[:, SGU_DIM:].astype(F32), g_ref[...], b_ref[...]).astype(BF16)
    r_i = lax.broadcasted_iota(I32, (CHUNK, CHUNK), 0)
    c_i = lax.broadcasted_iota(I32, (CHUNK, CHUNK), 1)
    gd = SGU_DIM // SGU_GROUPS
    for g in range(SGU_GROUPS):
        wm = jnp.where(c_i <= r_i, w_ref[g], 0.0).astype(BF16)
        bias = bs_ref[:, g:g + 1]
        for c in range(ts // CHUNK):
            rs = slice(c * CHUNK, (c + 1) * CHUNK)
            gs = slice(g * gd, (g + 1) * gd)
            mixed = jnp.dot(wm, vn[rs, gs], preferred_element_type=F32) + bias
            o_ref[rs, gs] = (u[rs, gs] * mixed).astype(o_ref.dtype)


def _sgu_prompt(ha, ln_g, ln_b, w_s, b_s_t, batch, seq, ts):
    ns = seq // ts
    vec = pl.BlockSpec((1, SGU_DIM), lambda b, s: (0, 0))
    return pl.pallas_call(
        functools.partial(_sgu_prompt_body, ts=ts),
        grid=(batch, ns),
        in_specs=[pl.BlockSpec((ts, 2 * SGU_DIM), lambda b, s: (b * ns + s, HA_UV // (2 * SGU_DIM))),
                  vec, vec,
                  pl.BlockSpec((SGU_GROUPS, CHUNK, CHUNK), lambda b, s: (0, 0, 0)),
                  pl.BlockSpec((CHUNK, SGU_GROUPS), lambda b, s: (0, 0))],
        out_specs=pl.BlockSpec((ts, SGU_DIM), lambda b, s: (b * ns + s, 0)),
        out_shape=jax.ShapeDtypeStruct((batch * seq, SGU_DIM), BF16),
        compiler_params=_cp(("arbitrary", "arbitrary")),
        name="sgu_prompt",
    )(ha, ln_g, ln_b, w_s, b_s_t)


def _mix_sample_body(glu_ref, uv_ref, st_ref, cw_ref, cb_ref, cg_ref, cbt_ref, sg_ref, sb_ref, wv_ref, bv_ref,
                     co_ref, nst_ref, so_ref, vn_ref, hp_ref, *, db, n_tok):
    cw = cw_ref[...]
    for b in range(db):
        rs = slice(b * n_tok, (b + 1) * n_tok)
        a = glu_ref[rs, :CONV_DIM].astype(F32)
        gt = glu_ref[rs, CONV_DIM:].astype(F32)
        hp_ref[0:HALO, :] = st_ref[b]
        hp_ref[HALO:HALO + n_tok, :] = a * _sigmoid(gt)
        rows = [jnp.sum(hp_ref[t:t + CONV_WIDTH, :] * cw, axis=0, keepdims=True) for t in range(n_tok)]
        c = jnp.concatenate(rows, axis=0) + cb_ref[...]
        y = _ln_rows(c, cg_ref[...], cbt_ref[...])
        co_ref[rs, :] = y * _sigmoid(y)
        nst_ref[b] = hp_ref[n_tok:n_tok + HALO, :]
        u = uv_ref[rs, :SGU_DIM].astype(F32)
        vn = _ln_rows(uv_ref[rs, SGU_DIM:].astype(F32), sg_ref[...], sb_ref[...])
        vn_ref[rs, :] = vn
        mixed = []
        for t in range(n_tok):
            acc = bv_ref[t:t + 1, :]
            for s in range(t + 1):
                acc = acc + wv_ref[t, s:s + 1, :] * vn[s:s + 1, :]
            mixed.append(acc)
        so_ref[rs, :] = u * jnp.concatenate(mixed, axis=0)


def _mix_sample(glu, uv, state, conv_w, conv_b, cln_g, cln_b, sln_g, sln_b, wv, bv, db, n_tok):
    rows = db * n_tok
    return pl.pallas_call(
        functools.partial(_mix_sample_body, db=db, n_tok=n_tok),
        out_shape=[jax.ShapeDtypeStruct((rows, CONV_DIM), F32),
                   jax.ShapeDtypeStruct((db, HALO, CONV_DIM), F32),
                   jax.ShapeDtypeStruct((rows, SGU_DIM), F32),
                   jax.ShapeDtypeStruct((rows, SGU_DIM), F32)],
        scratch_shapes=[pltpu.VMEM((_round_up(HALO + n_tok, 8), CONV_DIM), F32)],
        name="mix_sample",
    )(glu, uv, state, conv_w, conv_b, cln_g, cln_b, sln_g, sln_b, wv, bv)


def _router_body(x_ref, w_ref, b_ref, ti_ref, tg_ref, *, n_exp):
    logits = jnp.dot(x_ref[...], w_ref[...], preferred_element_type=F32,
                     precision=lax.Precision.HIGHEST) + b_ref[...]
    lane = lax.broadcasted_iota(I32, logits.shape, 1)
    lane_f = lane.astype(F32)
    neg = -jnp.inf
    lg = jnp.where(lane < n_exp, logits, neg)
    v1 = lg.max(axis=-1, keepdims=True)
    i1 = jnp.where(lg == v1, lane_f, float(LANES)).min(axis=-1, keepdims=True)
    lg2 = jnp.where(lane_f == i1, neg, lg)
    v2 = lg2.max(axis=-1, keepdims=True)
    i2 = jnp.where(lg2 == v2, lane_f, float(LANES)).min(axis=-1, keepdims=True)
    e = jnp.exp(v2 - v1)
    den = 1.0 + e
    ti_ref[...] = jnp.where(lane == 0, i1, jnp.where(lane == 1, i2, 0.0)).astype(I32)
    tg_ref[...] = jnp.where(lane == 0, 1.0 / den, jnp.where(lane == 1, e / den, 0.0))


def _router(x, w_pad, b_pad, n_exp, tm):
    m, d = x.shape
    return pl.pallas_call(
        functools.partial(_router_body, n_exp=n_exp),
        grid=(m // tm,),
        in_specs=[pl.BlockSpec((tm, d), lambda i: (i, 0)),
                  pl.BlockSpec((d, LANES), lambda i: (0, 0)),
                  pl.BlockSpec((1, LANES), lambda i: (0, 0))],
        out_specs=[pl.BlockSpec((tm, LANES), lambda i: (i, 0)),
                   pl.BlockSpec((tm, LANES), lambda i: (i, 0))],
        out_shape=[jax.ShapeDtypeStruct((m, LANES), I32), jax.ShapeDtypeStruct((m, LANES), F32)],
        compiler_params=_cp(("arbitrary",)),
        name="moe_router",
    )(x, w_pad, b_pad)


def _row_copy(src_hbm, dst, src_row, dst_row, sem):
    return pltpu.make_async_copy(src_hbm.at[pl.ds(src_row, 1)], dst.at[pl.ds(dst_row, 1)], sem)


ROW_DMA_UNROLL = 8
ROW_DMA_SLOTS = 2


def _prefetch_chunks(issue, wait):
    c = pl.program_id(0)
    ahead = ROW_DMA_SLOTS - 1

    @pl.when(c == 0)
    def _():
        for first in range(ahead):
            @pl.when(first < pl.num_programs(0))
            def _():
                issue(first, first)

    @pl.when(c + ahead < pl.num_programs(0))
    def _():
        issue(c + ahead, (c + ahead) % ROW_DMA_SLOTS)

    slot = c % ROW_DMA_SLOTS
    wait(slot)
    return slot


def _dispatch_body(src_ref, x_hbm, o_ref, buf, sem, *, rows):
    def issue(chunk, slot):
        def start(u, carry):
            for v in range(ROW_DMA_UNROLL):
                r = u * ROW_DMA_UNROLL + v
                _row_copy(x_hbm, buf.at[slot], src_ref[chunk * rows + r], r, sem.at[slot]).start()
            return carry

        lax.fori_loop(0, rows // ROW_DMA_UNROLL, start, 0)

    def wait(slot):
        def w(r, carry):
            _row_copy(x_hbm, buf.at[slot], 0, r, sem.at[slot]).wait()
            return carry

        lax.fori_loop(0, rows, w, 0)

    slot = _prefetch_chunks(issue, wait)
    o_ref[...] = buf[slot].astype(o_ref.dtype)


def _dispatch(src, x, n_slots, rows):
    d = x.shape[1]
    grid_spec = pltpu.PrefetchScalarGridSpec(
        num_scalar_prefetch=1,
        grid=(n_slots // rows,),
        in_specs=[pl.BlockSpec(memory_space=pl.ANY)],
        out_specs=pl.BlockSpec((rows, d), lambda i, s: (i, 0)),
        scratch_shapes=[pltpu.VMEM((ROW_DMA_SLOTS, rows, d), x.dtype), pltpu.SemaphoreType.DMA((ROW_DMA_SLOTS,))],
    )
    assert rows % ROW_DMA_UNROLL == 0
    return pl.pallas_call(
        functools.partial(_dispatch_body, rows=rows),
        grid_spec=grid_spec,
        out_shape=jax.ShapeDtypeStruct((n_slots, d), BF16),
        compiler_params=_cp(("arbitrary",)),
        name="moe_dispatch",
    )(src, x)


def _tile_copy(src, dst, sem):
    return pltpu.make_async_copy(src, dst, sem)


def _moe_ffn_body(exp_ref, row0_ref, nsub_ref, jmap_ref, nzero_ref, xs_hbm, w1_ref, w3_ref, w2_ref, ys_hbm,
                  xbuf, acc, w1b, w3b, w2b, sem, *, nj, sub):
    w = pl.program_id(0)
    j = pl.program_id(1)
    nsub = nsub_ref[w]
    row0 = row0_ref[w]
    nzero = nzero_ref[w]

    def zero_copy(t):
        r = pl.multiple_of(row0 + t * sub, sub)
        return _tile_copy(acc.at[pl.ds(0, sub)], ys_hbm.at[pl.ds(r, sub)], sem)

    @pl.when(jnp.logical_and(j == 0, nzero > 0))
    def _():
        acc[0:sub, :] = jnp.zeros((sub, acc.shape[1]), F32)

        def zs(t, carry):
            zero_copy(t).start()
            return carry

        def zw(t, carry):
            zero_copy(t).wait()
            return carry

        lax.fori_loop(0, nzero, zs, 0)
        lax.fori_loop(0, nzero, zw, 0)

    def in_copy(t):
        r = pl.multiple_of(t * sub, sub)
        return _tile_copy(xs_hbm.at[pl.ds(pl.multiple_of(row0 + r, sub), sub)], xbuf.at[pl.ds(r, sub)], sem)

    def out_copy(t):
        r = pl.multiple_of(t * sub, sub)
        return _tile_copy(acc.at[pl.ds(r, sub)], ys_hbm.at[pl.ds(pl.multiple_of(row0 + r, sub), sub)], sem)

    def for_tiles(fn):
        def it(t, carry):
            fn(t)
            return carry
        lax.fori_loop(0, nsub, it, 0)

    def zero_tile(t):
        acc[pl.ds(pl.multiple_of(t * sub, sub), sub), :] = jnp.zeros((sub, acc.shape[1]), F32)

    @pl.when(j == 0)
    def _():
        for_tiles(lambda t: in_copy(t).start())
        for_tiles(zero_tile)
        for_tiles(lambda t: in_copy(t).wait())

    def tile(t, n, w1v, w3v, w2v):
        r = pl.multiple_of(t * sub, sub)
        x = xbuf[pl.ds(r, n * sub), :]
        a = jnp.dot(x, w1v, preferred_element_type=F32)
        c = jnp.dot(x, w3v, preferred_element_type=F32)
        h = (a * _sigmoid(a) * c).astype(BF16)
        acc[pl.ds(r, n * sub), :] += jnp.dot(h, w2v, preferred_element_type=F32)

    def cast_weights():
        return w1_ref[...].astype(BF16), w3_ref[...].astype(BF16), w2_ref[...].astype(BF16)

    per_trip = 2 * MOE_WIDE
    n_trips = nsub // per_trip

    def trip(t, wv):
        tile(per_trip * t, MOE_WIDE, *wv)
        tile(per_trip * t + MOE_WIDE, MOE_WIDE, *wv)

    @pl.when(n_trips >= 1)
    def _():
        wv = cast_weights()
        w1b[...], w3b[...], w2b[...] = wv
        trip(0, wv)

    @pl.when(jnp.logical_and(n_trips == 0, nsub > 0))
    def _():
        w1b[...], w3b[...], w2b[...] = cast_weights()

    def later_trip(t, carry):
        trip(t, (w1b[...], w3b[...], w2b[...]))
        return carry

    lax.fori_loop(1, n_trips, later_trip, 0)

    def rest(t, carry):
        tile(t, 1, w1b[...], w3b[...], w2b[...])
        return carry

    lax.fori_loop(n_trips * per_trip, nsub, rest, 0)

    @pl.when(j == nj - 1)
    def _():
        for_tiles(lambda t: out_copy(t).start())
        for_tiles(lambda t: out_copy(t).wait())


def _moe_ffn(item_exp, item_row0, item_nsub, item_jlast, item_nzero, xs, w1, w3, w2, layer, n_items, super_rows,
             sub, tf):
    n_slots, d = xs.shape
    f = w1.shape[3]
    nj = f // tf

    def jj(w, j, jl):
        return jnp.where(jl[w] > 0, j, nj - 1)

    grid_spec = pltpu.PrefetchScalarGridSpec(
        num_scalar_prefetch=5,
        grid=(n_items, nj),
        in_specs=[pl.BlockSpec(memory_space=pl.ANY),
                  pl.BlockSpec((None, None, d, tf), lambda w, j, ex, r0, ns, jl, nz: (layer, ex[w], 0, jj(w, j, jl))),
                  pl.BlockSpec((None, None, d, tf), lambda w, j, ex, r0, ns, jl, nz: (layer, ex[w], 0, jj(w, j, jl))),
                  pl.BlockSpec((None, None, tf, d), lambda w, j, ex, r0, ns, jl, nz: (layer, ex[w], jj(w, j, jl), 0))],
        out_specs=pl.BlockSpec(memory_space=pl.ANY),
        scratch_shapes=[pltpu.VMEM((super_rows, d), BF16),
                        pltpu.VMEM((super_rows, d), F32),
                        pltpu.VMEM((d, tf), BF16),
                        pltpu.VMEM((d, tf), BF16),
                        pltpu.VMEM((tf, d), BF16),
                        pltpu.SemaphoreType.DMA(())],
    )
    return pl.pallas_call(
        functools.partial(_moe_ffn_body, nj=nj, sub=sub),
        grid_spec=grid_spec,
        out_shape=jax.ShapeDtypeStruct((n_slots, d), F32),
        compiler_params=_cp(("arbitrary", "arbitrary")),
        name="moe_ffn",
    )(item_exp, item_row0, item_nsub, item_jlast, item_nzero, xs, w1, w3, w2)


def _combine_body(pos_ref, ys_hbm, tg_ref, x_ref, g_ref, b_ref, of_ref, ob_ref, buf, sem, *, tb, alpha):
    def issue(chunk, slot):
        def start(u, carry):
            for v in range(ROW_DMA_UNROLL // TOP_K_EXPERTS):
                r = u * (ROW_DMA_UNROLL // TOP_K_EXPERTS) + v
                for k in range(TOP_K_EXPERTS):
                    p = pos_ref[TOP_K_EXPERTS * (chunk * tb + r) + k]
                    _row_copy(ys_hbm, buf.at[slot, k], p, r, sem.at[slot]).start()
            return carry

        lax.fori_loop(0, tb * TOP_K_EXPERTS // ROW_DMA_UNROLL, start, 0)

    def wait(slot):
        def w(r, carry):
            for k in range(TOP_K_EXPERTS):
                _row_copy(ys_hbm, buf.at[slot, k], 0, r, sem.at[slot]).wait()
            return carry

        lax.fori_loop(0, tb, w, 0)

    slot = _prefetch_chunks(issue, wait)
    f = tg_ref[:, 0:1] * buf[slot, 0] + tg_ref[:, 1:2] * buf[slot, 1]
    y = _ln_rows(alpha * x_ref[...] + f, g_ref[...], b_ref[...])
    of_ref[...] = y
    ob_ref[...] = y.astype(BF16)


def _combine_ln(pos, ys, tg, x, g, b, alpha, tb):
    m, d = x.shape
    grid_spec = pltpu.PrefetchScalarGridSpec(
        num_scalar_prefetch=1,
        grid=(m // tb,),
        in_specs=[pl.BlockSpec(memory_space=pl.ANY),
                  pl.BlockSpec((tb, LANES), lambda i, p: (i, 0)),
                  pl.BlockSpec((tb, d), lambda i, p: (i, 0)),
                  pl.BlockSpec((1, d), lambda i, p: (0, 0)),
                  pl.BlockSpec((1, d), lambda i, p: (0, 0))],
        out_specs=[pl.BlockSpec((tb, d), lambda i, p: (i, 0)),
                   pl.BlockSpec((tb, d), lambda i, p: (i, 0))],
        scratch_shapes=[pltpu.VMEM((ROW_DMA_SLOTS, TOP_K_EXPERTS, tb, d), F32),
                        pltpu.SemaphoreType.DMA((ROW_DMA_SLOTS,))],
    )
    assert tb * TOP_K_EXPERTS % ROW_DMA_UNROLL == 0
    return pl.pallas_call(
        functools.partial(_combine_body, tb=tb, alpha=alpha),
        grid_spec=grid_spec,
        out_shape=[jax.ShapeDtypeStruct((m, d), F32), jax.ShapeDtypeStruct((m, d), BF16)],
        compiler_params=_cp(("arbitrary",)),
        name="moe_combine_ln",
    )(pos, ys, tg, x, g, b)


def _moe_plan(top_i, n_real, n_exp, n_slots, n_items, super_rows, sub):
    a = top_i.shape[0] * TOP_K_EXPERTS
    e_flat = top_i.reshape(-1)
    tok = jnp.arange(a, dtype=I32) // TOP_K_EXPERTS
    real = tok < n_real
    onehot = jnp.logical_and(e_flat[:, None] == jnp.arange(n_exp, dtype=I32)[None, :], real[:, None]).astype(I32)
    csum = jnp.cumsum(onehot, axis=0)
    rank = jnp.take_along_axis(csum, e_flat[:, None], axis=1)[:, 0] - 1
    counts = csum[-1]
    padded = (counts + sub - 1) // sub * sub
    gstart = jnp.cumsum(padded) - padded
    pos = jnp.where(real, gstart[e_flat] + rank, 0).astype(I32)
    src = jnp.zeros((n_slots,), I32).at[jnp.where(real, pos, n_slots)].set(tok, mode="drop")
    n_super = (padded + super_rows - 1) // super_rows
    iend = jnp.cumsum(n_super)
    istart = iend - n_super
    total = iend[-1]
    wi = jnp.arange(n_items, dtype=I32)
    valid = wi < total
    e_w = jnp.minimum(jnp.searchsorted(iend, wi, side="right"), n_exp - 1).astype(I32)
    last_e = e_w[jnp.maximum(total - 1, 0)]
    e_w = jnp.where(valid, e_w, last_e)
    k_w = wi - istart[e_w]
    subs_e = padded[e_w] // sub
    items_e = jnp.maximum(n_super[e_w], 1)
    base, extra = subs_e // items_e, subs_e % items_e
    nsub = jnp.where(valid, base + (k_w < extra).astype(I32), 0).astype(I32)
    first_sub = k_w * base + jnp.minimum(k_w, extra)
    z0 = jnp.sum(padded) + (wi - total) * super_rows
    nzero = jnp.where(valid, 0, jnp.clip((n_slots - z0) // sub, 0, super_rows // sub)).astype(I32)
    row0 = jnp.where(valid, gstart[e_w] + first_sub * sub, jnp.where(nzero > 0, z0, 0)).astype(I32)
    return pos, src, e_w, row0, nsub, valid.astype(I32), nzero


def _rel_bucket(dist):
    n = jnp.maximum(dist, 0)
    max_exact = REL_BUCKETS // 2
    nf = jnp.maximum(n, 1).astype(F32)
    large = max_exact + (jnp.log(nf / max_exact) / math.log(REL_MAX_DIST / max_exact)
                         * (REL_BUCKETS - max_exact)).astype(I32)
    large = jnp.minimum(large, REL_BUCKETS - 1)
    return jnp.where(n < max_exact, n, large)


def _shifted_bias(rel_bias, dist):
    t = rel_bias[_rel_bucket(dist)] - rel_bias[REL_BUCKETS - 1]
    t = jnp.where((dist >= 0)[..., None], t, 0.0)
    return jnp.moveaxis(t, -1, 0)


def _pick_tile(m_rows, target):
    return max(t for t in range(16, target + 1, 16) if m_rows % t == 0)


def _pad_rows(x, rows):
    return jnp.pad(x, ((0, rows - x.shape[0]), (0, 0)))


def kernel(x_prompt, x_sample, cache_k, cache_v, cache_ik, state_conv, page_table, w_in, conv_w, conv_b,
           conv_ln_g, conv_ln_b, sgu_ln_g, sgu_ln_b, sgu_w, sgu_b, w_pa, w_pb, w_pc, w_out, ln1_g, ln1_b,
           ln2_g, ln2_b, rel_bias, ffn_w1, ffn_w3, ffn_w2, moe_router, moe_router_b, moe_w1, moe_w3, moe_w2):
    batch, seq, d_model = x_prompt.shape
    db, n_tok, _ = x_sample.shape
    depth = w_in.shape[0]
    n_pool = cache_k.shape[1]
    n_exp = moe_router.shape[2]
    mp, ms = batch * seq, db * n_tok
    assert ms <= SAMPLE_TILE and mp % SAMPLE_TILE == 0
    m_all = mp + SAMPLE_TILE
    tm = m_all // N_ROW_TILES
    assert tm * N_ROW_TILES == m_all and tm % 16 == 0
    alpha = (2 * depth) ** 0.25
    tq = min(256, seq)

    xf = jnp.concatenate([x_prompt.reshape(mp, d_model), x_sample.reshape(ms, d_model),
                          jnp.zeros((m_all - mp - ms, d_model), F32)], axis=0)
    xb = xf.astype(BF16)
    w_in_t = jnp.swapaxes(w_in, 1, 2)
    ha_rows = ([W_Q_ROW0 + HA_TILE * t for t in range(QKV_W // HA_TILE)]
               + [W_GLU_ROW0 + HA_TILE * t for t in range(2 * CONV_DIM // HA_TILE)]
               + [W_UV_ROW0 + HA_TILE * t for t in range(2 * SGU_DIM // HA_TILE)]
               + [W_GATE_ROW0 + HA_TILE * t for t in range(N_BRANCH * d_model // HA_TILE)]
               + [W_IQ_ROW0 + HA_TILE * t for t in range(IQ_W // HA_TILE)])
    w_pa_b, w_pb_b, w_pc_b, w_out_b = (w.astype(BF16) for w in (w_pa, w_pb, w_pc, w_out))
    ffn_w2_b = ffn_w2.astype(BF16)

    kk = jnp.arange(2 * tq, dtype=I32)
    d_wrap = jnp.where(kk < tq, -kk, 2 * tq - kk)

    def toeplitz(offset):
        v = _shifted_bias(rel_bias, d_wrap + offset)
        return jnp.tile(v, (1, tq))[:, :tq * (2 * tq - 1)].reshape(N_HEADS, tq, 2 * tq - 1)[:, :, :tq]

    bias_tiles = jnp.stack([toeplitz(0), toeplitz(tq)])
    tok = jnp.arange(n_tok, dtype=I32)
    lane = jnp.arange(LANES, dtype=I32)
    d_last = PAGE_SIZE + tok[:, None] - lane[None, :]
    bias_last = _shifted_bias(rel_bias, d_last).reshape(N_HEADS * n_tok, LANES)
    bias_new = _shifted_bias(rel_bias, tok[:, None] - lane[None, :]).reshape(N_HEADS * n_tok, LANES)

    ck = cache_k.reshape(depth, n_pool, PAGE_SIZE * N_KV_HEADS, HEAD_DIM)
    cv = cache_v.reshape(depth, n_pool, PAGE_SIZE * N_KV_HEADS, HEAD_DIM)
    cik_t = jnp.swapaxes(cache_ik, 2, 3)
    n_pages = page_table.shape[1]
    idx_group = math.gcd(n_pages, IDX_PAGE_GROUP)
    attn_group = math.gcd(n_pages, ATTN_PAGE_GROUP)
    n_slots = _round_up(TOP_K_EXPERTS * (mp + ms) + n_exp * (MOE_SUB - 1), MOE_SUB)
    n_items = -(-n_slots // MOE_SUPER) + n_exp

    outs = {k: [] for k in ("k", "v", "ik", "conv_p", "conv_s", "sgu_s")}
    for l in range(depth):
        ha = _in_proj(xb, w_in_t, l, ha_rows, tm, HA_TILE)
        kv, ikw, k4, v4 = _kv_proj(xb, w_in_t, l, tm)

        a_p = _attn_prompt(ha, kv, ikw, bias_tiles, batch, seq, tq)
        conv_w_rep = jnp.broadcast_to(conv_w[l].reshape(CONV_WIDTH, CONV_CHUNKS, 1, LANES).transpose(1, 0, 2, 3),
                                      (CONV_CHUNKS, CONV_WIDTH, F32_SUBLANES, LANES))
        c_p, conv_state_p = _conv_prompt(ha, conv_w_rep, conv_b[l][None], conv_ln_g[l][None], conv_ln_b[l][None],
                                         batch, seq, min(256, seq))
        s_p = _sgu_prompt(ha, sgu_ln_g[l][None], sgu_ln_b[l][None], sgu_w[l], sgu_b[l].T, batch, seq,
                          min(512, seq))

        has, kvs, ikws = ha[mp:mp + ms], kv[mp:mp + ms], ikw[mp:mp + ms]
        iq_rows = has[:, HA_IQ:HA_IQ + IQ_W].reshape(db, n_tok * IDX_HEADS, IDX_DIM)
        iw_rows = (ikws[:, IKW_IW:IKW_IW + IDX_HEADS] * (IDX_DIM ** -0.5 * IDX_HEADS ** -0.5)
                   ).reshape(db, n_tok * IDX_HEADS, 1)
        iw_rows = jnp.broadcast_to(iw_rows, (db, n_tok * IDX_HEADS, LANES))
        scores = _idx_sample(page_table, iq_rows, iw_rows, cik_t, l, n_tok, idx_group)
        pad_rows = LANES - n_tok
        ik_new = jnp.pad(ikws[:, :IDX_DIM].reshape(db, n_tok, IDX_DIM), ((0, 0), (0, pad_rows), (0, 0)))
        scores_new = _idx_new(iq_rows, iw_rows, ik_new, n_tok)
        q_rows = has[:, HA_Q:HA_Q + QKV_W].reshape(db, n_tok, N_HEADS, HEAD_DIM).transpose(0, 2, 1, 3)
        q_rows = q_rows.reshape(db, N_HEADS * n_tok, HEAD_DIM)
        k_new = jnp.pad(kvs[:, KV_K:KV_K + KV_W].reshape(db, n_tok, KV_W), ((0, 0), (0, pad_rows), (0, 0)))
        v_new = jnp.pad(kvs[:, KV_V:KV_V + KV_W].reshape(db, n_tok, KV_W), ((0, 0), (0, pad_rows), (0, 0)))
        a_s = _attn_sample(page_table, scores, scores_new, q_rows, k_new, v_new, bias_last, bias_new,
                           ck, cv, l, n_tok, attn_group)
        a_s = a_s.reshape(db, N_HEADS, n_tok, HEAD_DIM).transpose(0, 2, 1, 3).reshape(ms, QKV_W)
        gd = SGU_DIM // SGU_GROUPS
        wv = jnp.repeat(sgu_w[l][:, :n_tok, :n_tok].transpose(1, 2, 0), gd, axis=-1)
        bv = jnp.repeat(sgu_b[l][:, :n_tok].T, gd, axis=-1)
        c_s, conv_state_s, s_s, vn_s = _mix_sample(
            has[:, HA_GLU:HA_GLU + 2 * CONV_DIM], has[:, HA_UV:HA_UV + 2 * SGU_DIM], state_conv[l],
            conv_w[l], conv_b[l][None], conv_ln_g[l][None], conv_ln_b[l][None],
            sgu_ln_g[l][None], sgu_ln_b[l][None], wv, bv, db, n_tok)

        sample_rows = tuple(_pad_rows(t.astype(BF16), SAMPLE_TILE) for t in (a_s, c_s, s_s))
        merged = _merge((a_p, c_p, s_p), sample_rows, ha, w_pa_b, w_pb_b, w_pc_b, l, SAMPLE_TILE)
        x1f, x1b = _mm_ln(merged, w_out_b, l, xf, ln1_g[l][None], ln1_b[l][None], alpha, _pick_tile(m_all, 384))

        j = l // 2
        if l % 2 == 0:
            h = _swiglu_up(x1b, ffn_w1, ffn_w3, j, tm, 512)
            xf, xb = _mm_ln(h, ffn_w2_b, j, x1f, ln2_g[l][None], ln2_b[l][None], alpha, _pick_tile(m_all, 384))
        else:
            w_r = jnp.pad(moe_router[j], ((0, 0), (0, LANES - n_exp)))
            b_r = jnp.pad(moe_router_b[j], (0, LANES - n_exp))[None]
            ti, tg = _router(x1f, w_r, b_r, n_exp, _pick_tile(m_all, 528))
            pos, src, it_e, it_r0, it_ns, it_valid, it_nz = _moe_plan(ti[:, :TOP_K_EXPERTS], mp + ms, n_exp,
                                                                      n_slots, n_items, MOE_SUPER, MOE_SUB)
            xs = _dispatch(src, x1f, n_slots, MOE_SUB)
            ys = _moe_ffn(it_e, it_r0, it_ns, it_valid, it_nz, xs, moe_w1, moe_w3, moe_w2, j, n_items,
                          MOE_SUPER, MOE_SUB, MOE_TF)
            tb = COMBINE_TB if m_all % COMBINE_TB == 0 else 16
            xf, xb = _combine_ln(pos, ys, tg, x1f, ln2_g[l][None], ln2_b[l][None], alpha, tb)

        outs["k"].append(k4)
        outs["v"].append(v4)
        outs["ik"].append(ikw[:, :IDX_DIM])
        outs["conv_p"].append(conv_state_p)
        outs["conv_s"].append(conv_state_s)
        outs["sgu_s"].append(vn_s.reshape(db, n_tok, SGU_DIM))

    def split(name, rows_per_token, width_shape):
        st = jnp.stack(outs[name])
        p = st[:, :mp * rows_per_token].reshape((depth, batch, seq) + width_shape)
        s = st[:, mp * rows_per_token:(mp + ms) * rows_per_token].reshape((depth, db, n_tok) + width_shape)
        return p, s

    k_p, k_s = split("k", N_KV_HEADS, (N_KV_HEADS, HEAD_DIM))
    v_p, v_s = split("v", N_KV_HEADS, (N_KV_HEADS, HEAD_DIM))
    ik_p, ik_s = split("ik", 1, (IDX_DIM,))
    y_prompt = xf[:mp].reshape(batch, seq, d_model)
    y_sample = xf[mp:mp + ms].reshape(db, n_tok, d_model)
    return (y_prompt, y_sample, k_p, v_p, ik_p, jnp.stack(outs["conv_p"]), k_s, v_s, ik_s,
            jnp.stack(outs["conv_s"]), jnp.stack(outs["sgu_s"]))
```

```python
import functools
import math

import jax
import jax.numpy as jnp
import numpy as np
from jax import lax
from jax.experimental import pallas as pl
from jax.experimental.pallas import tpu as pltpu

F32 = jnp.float32
BF16 = jnp.bfloat16
I32 = jnp.int32

N_HEADS = 16
HEAD_DIM = 128
N_KV_HEADS = 4
KV_REP = N_HEADS // N_KV_HEADS
IDX_HEADS = 16
IDX_DIM = 64
TOPK_MAX = 256
PAGE_SIZE = 128
CONV_DIM = 1024
CONV_WIDTH = 31
SGU_DIM = 1024
SGU_GROUPS = 8
CHUNK = 128
N_BRANCH = 3
TOP_K_EXPERTS = 2
REL_BUCKETS = 32
REL_MAX_DIST = 128
LN_EPS = 1e-5

V7X_VMEM_BYTES = 64 * 2**20
VMEM_LIMIT = V7X_VMEM_BYTES - 8 * 2**20
LANES = 128
F32_SUBLANES = 8
INT_MIN = -(2**31)
MASK_NEG = -1e30

QKV_W = N_HEADS * HEAD_DIM
KV_W = N_KV_HEADS * HEAD_DIM
IQ_W = IDX_HEADS * IDX_DIM
HA_Q, HA_GLU, HA_UV, HA_GATE, HA_IQ = 0, 2048, 4096, 6144, 12288
HA_TILE = 1024
W_Q_ROW0 = 0
W_KV_ROW0 = W_Q_ROW0 + QKV_W
W_IQ_ROW0 = W_KV_ROW0 + 2 * KV_W
W_IK_ROW0 = W_IQ_ROW0 + IQ_W
W_GLU_ROW0 = W_IK_ROW0 + IDX_DIM + IDX_HEADS
W_UV_ROW0 = W_GLU_ROW0 + 2 * CONV_DIM
W_GATE_ROW0 = W_UV_ROW0 + 2 * SGU_DIM
KV_K, KV_V = 0, KV_W
IKW_IW = IDX_DIM

N_ROW_TILES = 8
SAMPLE_TILE = 256
MOE_SUB = 256
MOE_SUPER = 2048
MOE_TF = 256
MOE_WIDE = 2
COMBINE_TB = 128
IDX_PAGE_GROUP = 32
ATTN_PAGE_GROUP = 32


def _cp(sem, vmem=VMEM_LIMIT):
    return pltpu.CompilerParams(dimension_semantics=sem, vmem_limit_bytes=vmem)


def _round_up(x, m):
    return (x + m - 1) // m * m


def _ln_rows(z, g, b):
    mu = jnp.mean(z, axis=-1, keepdims=True)
    d = z - mu
    var = jnp.mean(d * d, axis=-1, keepdims=True)
    return d * lax.rsqrt(var + LN_EPS) * g + b


def _sigmoid(x):
    return 1.0 / (1.0 + jnp.exp(-x))


def _in_proj_body(off_ref, x_ref, w_ref, o_ref, wt_ref):
    @pl.when(pl.program_id(1) == 0)
    def _():
        wt_ref[...] = w_ref[0].T.astype(BF16)

    o_ref[...] = jnp.dot(x_ref[...], wt_ref[...], preferred_element_type=F32).astype(o_ref.dtype)


def _in_proj(x, w_t, layer, row_offsets, tm, tn):
    m, k = x.shape
    n_tiles = len(row_offsets)
    grid_spec = pltpu.PrefetchScalarGridSpec(
        num_scalar_prefetch=1,
        grid=(n_tiles, m // tm),
        in_specs=[pl.BlockSpec((tm, k), lambda j, i, off: (i, 0)),
                  pl.BlockSpec((pl.Element(1), pl.Element(tn), pl.Element(k)),
                               lambda j, i, off: (layer, pl.multiple_of(off[j], F32_SUBLANES), 0))],
        out_specs=pl.BlockSpec((tm, tn), lambda j, i, off: (i, j)),
        scratch_shapes=[pltpu.VMEM((k, tn), BF16)],
    )
    return pl.pallas_call(
        _in_proj_body,
        grid_spec=grid_spec,
        out_shape=jax.ShapeDtypeStruct((m, n_tiles * tn), BF16),
        compiler_params=_cp(("arbitrary", "arbitrary")),
        name="in_proj",
    )(jnp.asarray(row_offsets, I32), x, w_t)


def _kv_proj_body(x_ref, wkv_ref, wik_ref, kv_ref, ik_ref, k4_ref, v4_ref, wkv_t, wik_t):
    @pl.when(pl.program_id(0) == 0)
    def _():
        wkv_t[...] = wkv_ref[...].T.astype(BF16)
        wik_t[...] = wik_ref[...].T.astype(BF16)

    x = x_ref[...]
    tm = x.shape[0]
    kv = jnp.dot(x, wkv_t[...], preferred_element_type=F32)
    kv_ref[...] = kv
    ik_ref[...] = jnp.dot(x, wik_t[...], preferred_element_type=F32)
    for g in range(N_KV_HEADS):
        k4_ref[pl.ds(g, tm, stride=N_KV_HEADS), :] = kv[:, KV_K + g * HEAD_DIM:KV_K + (g + 1) * HEAD_DIM]
        v4_ref[pl.ds(g, tm, stride=N_KV_HEADS), :] = kv[:, KV_V + g * HEAD_DIM:KV_V + (g + 1) * HEAD_DIM]


def _kv_proj(x, w_t, layer, tm):
    m, k = x.shape
    assert W_KV_ROW0 % (2 * KV_W) == 0 and W_IK_ROW0 % LANES == 0
    row = lambda i: (i, 0)
    per_head = jax.ShapeDtypeStruct((m * N_KV_HEADS, HEAD_DIM), F32)
    return pl.pallas_call(
        _kv_proj_body,
        grid=(m // tm,),
        in_specs=[pl.BlockSpec((tm, k), row),
                  pl.BlockSpec((None, 2 * KV_W, k), lambda i: (layer, W_KV_ROW0 // (2 * KV_W), 0)),
                  pl.BlockSpec((None, LANES, k), lambda i: (layer, W_IK_ROW0 // LANES, 0))],
        out_specs=[pl.BlockSpec((tm, 2 * KV_W), row), pl.BlockSpec((tm, LANES), row),
                   pl.BlockSpec((tm * N_KV_HEADS, HEAD_DIM), row), pl.BlockSpec((tm * N_KV_HEADS, HEAD_DIM), row)],
        out_shape=[jax.ShapeDtypeStruct((m, 2 * KV_W), F32), jax.ShapeDtypeStruct((m, LANES), F32),
                   per_head, per_head],
        scratch_shapes=[pltpu.VMEM((k, 2 * KV_W), BF16), pltpu.VMEM((k, LANES), BF16)],
        compiler_params=_cp(("arbitrary",)),
        name="kv_proj",
    )(x, w_t, w_t)


def _row_halves(rows):
    half = rows // 2 // 16 * 16
    return (slice(0, half), slice(half, rows)) if half > 0 else (slice(0, rows),)


def _mm_ln_body(x_ref, w_ref, r_ref, g_ref, b_ref, of_ref, ob_ref, *, alpha):
    for rs in _row_halves(x_ref.shape[0]):
        z = jnp.dot(x_ref[rs, :], w_ref[...], preferred_element_type=F32) + alpha * r_ref[rs, :]
        y = _ln_rows(z, g_ref[...], b_ref[...])
        of_ref[rs, :] = y
        ob_ref[rs, :] = y.astype(BF16)


def _mm_ln(x, w, layer, resid, g, b, alpha, tm):
    m, k = x.shape
    n = w.shape[2]
    row = lambda i: (i, 0)
    return pl.pallas_call(
        functools.partial(_mm_ln_body, alpha=alpha),
        grid=(m // tm,),
        in_specs=[pl.BlockSpec((tm, k), row),
                  pl.BlockSpec((None, k, n), lambda i: (layer, 0, 0), pipeline_mode=pl.Buffered(1)),
                  pl.BlockSpec((tm, n), row),
                  pl.BlockSpec((1, n), lambda i: (0, 0)),
                  pl.BlockSpec((1, n), lambda i: (0, 0))],
        out_specs=[pl.BlockSpec((tm, n), row), pl.BlockSpec((tm, n), row)],
        out_shape=[jax.ShapeDtypeStruct((m, n), F32), jax.ShapeDtypeStruct((m, n), BF16)],
        compiler_params=_cp(("arbitrary",)),
        name="proj_ln",
    )(x, w, resid, g, b)


def _swiglu_up_body(x_ref, w1_ref, w3_ref, o_ref, w1b, w3b):
    @pl.when(pl.program_id(1) == 0)
    def _():
        w1b[...] = w1_ref[...].astype(BF16)
        w3b[...] = w3_ref[...].astype(BF16)

    for rs in _row_halves(x_ref.shape[0]):
        x = x_ref[rs, :]
        a = jnp.dot(x, w1b[...], preferred_element_type=F32)
        c = jnp.dot(x, w3b[...], preferred_element_type=F32)
        o_ref[rs, :] = (a * _sigmoid(a) * c).astype(o_ref.dtype)


def _swiglu_up(x, w1, w3, layer, tm, tf):
    m, k = x.shape
    f = w1.shape[2]
    return pl.pallas_call(
        _swiglu_up_body,
        grid=(f // tf, m // tm),
        in_specs=[pl.BlockSpec((tm, k), lambda j, i: (i, 0)),
                  pl.BlockSpec((None, k, tf), lambda j, i: (layer, 0, j)),
                  pl.BlockSpec((None, k, tf), lambda j, i: (layer, 0, j))],
        out_specs=pl.BlockSpec((tm, tf), lambda j, i: (i, j)),
        out_shape=jax.ShapeDtypeStruct((m, f), BF16),
        scratch_shapes=[pltpu.VMEM((k, tf), BF16), pltpu.VMEM((k, tf), BF16)],
        compiler_params=_cp(("arbitrary", "arbitrary")),
        name="ffn_up",
    )(x, w1, w3)


def _merge_body(ap_ref, cp_ref, sp_ref, as_ref, cs_ref, ss_ref, g_ref, wa_ref, wb_ref, wc_ref, o_ref, *, n_prompt):
    i = pl.program_id(0)
    d = o_ref.shape[1]

    def merge(a_ref, c_ref, s_ref):
        for rs in _row_halves(o_ref.shape[0]):
            pa = jnp.dot(a_ref[rs, :], wa_ref[...], preferred_element_type=F32)
            pb = jnp.dot(c_ref[rs, :], wb_ref[...], preferred_element_type=F32)
            pc = jnp.dot(s_ref[rs, :], wc_ref[...], preferred_element_type=F32)
            o = (_sigmoid(g_ref[rs, :d].astype(F32)) * pa + _sigmoid(g_ref[rs, d:2 * d].astype(F32)) * pb
                 + _sigmoid(g_ref[rs, 2 * d:].astype(F32)) * pc)
            o_ref[rs, :] = o.astype(o_ref.dtype)

    @pl.when(i < n_prompt)
    def _():
        merge(ap_ref, cp_ref, sp_ref)

    @pl.when(i >= n_prompt)
    def _():
        merge(as_ref, cs_ref, ss_ref)


def _merge(prompt, sample, ha, w_pa, w_pb, w_pc, layer, tm):
    m = ha.shape[0]
    d = w_pa.shape[2]
    n_prompt = prompt[0].shape[0] // tm
    assert prompt[0].shape[0] % tm == 0 and sample[0].shape[0] == tm and m == (n_prompt + 1) * tm
    assert HA_GATE % (N_BRANCH * d) == 0
    p_row = lambda i: (jnp.minimum(i, n_prompt - 1), 0)
    fixed = lambda i: (0, 0)
    return pl.pallas_call(
        functools.partial(_merge_body, n_prompt=n_prompt),
        grid=(n_prompt + 1,),
        in_specs=[pl.BlockSpec((tm, t.shape[1]), p_row) for t in prompt]
        + [pl.BlockSpec((tm, t.shape[1]), fixed) for t in sample]
        + [pl.BlockSpec((tm, N_BRANCH * d), lambda i: (i, HA_GATE // (N_BRANCH * d)))]
        + [pl.BlockSpec((None,) + w.shape[1:], lambda i: (layer, 0, 0), pipeline_mode=pl.Buffered(1))
           for w in (w_pa, w_pb, w_pc)],
        out_specs=pl.BlockSpec((tm, d), lambda i: (i, 0)),
        out_shape=jax.ShapeDtypeStruct((m, d), BF16),
        compiler_params=_cp(("arbitrary",)),
        name="branch_merge",
    )(*prompt, *sample, ha, w_pa, w_pb, w_pc)


def _sortable_key(score):
    bits = pltpu.bitcast(score, I32)
    return bits ^ ((bits >> 31) & jnp.int32(0x7FFFFFFF))


def _kth_largest_key(count_ge, rows, k):
    def step(it, t):
        cand = t + jnp.left_shift(jnp.int32(1), 31 - it)
        return jnp.where(count_ge(cand) >= k, cand, t)

    return lax.fori_loop(0, 32, step, jnp.full((rows, 1), INT_MIN, I32))


RADIX4_STEPS = 16
RADIX4_FIELD_BITS = 5
RADIX4_WIDE = 4096.0


def _kth_largest_key_radix4(count3, rows, k, n_valid):
    kf = float(k)

    def cond(state):
        it, _, _, pending = state
        return jnp.logical_and(it < RADIX4_STEPS, pending > 0.0)

    def body(state):
        it, t, cnt, _ = state
        one = jnp.left_shift(jnp.int32(1), 30 - 2 * it)
        c1 = t + one
        c2 = c1 + one
        c3 = c2 + one
        n1, n2, n3 = count3(c1, c2, c3)
        t = jnp.where(n3 >= kf, c3, jnp.where(n2 >= kf, c2, jnp.where(n1 >= kf, c1, t)))
        cnt = jnp.where(n3 >= kf, n3, jnp.where(n2 >= kf, n2, jnp.where(n1 >= kf, n1, cnt)))
        settled = jnp.logical_or(cnt == kf, n_valid <= kf)
        return it + 1, t, cnt, jnp.max(jnp.where(settled, 0.0, 1.0))

    init = (jnp.int32(0), jnp.full((rows, 1), INT_MIN, I32), jnp.full((rows, 1), -1.0, F32), jnp.float32(1.0))
    return lax.while_loop(cond, body, init)[1]


def _lane_tile(x, width):
    return x if width == LANES else jnp.concatenate([x] * (width // LANES), axis=1)


def _attn_prompt_body(q_ref, iq_ref, iw_ref, kv_ref, ik_ref, bias_ref, o_ref,
                      keys_ref, mb_ref, iwb_ref, m_ref, acc_ref, *, tq, k_sel):
    i = pl.program_id(1)
    nkb = i + 1
    tk = tq
    row = lax.broadcasted_iota(I32, (tq, tk), 0)
    col = lax.broadcasted_iota(I32, (tq, tk), 1)
    low_half = lax.broadcasted_iota(I32, (tk, LANES), 1) < IDX_DIM
    iw = iw_ref[...] * (IDX_DIM ** -0.5 * IDX_HEADS ** -0.5)
    for h in range(IDX_HEADS):
        iwb_ref[h] = jnp.broadcast_to(iw[:, IKW_IW + h:IKW_IW + h + 1], (tq, LANES))
    iq_pairs = [iq_ref[:, p * LANES:(p + 1) * LANES] for p in range(IDX_HEADS // 2)]

    def idx_block(kb, carry):
        ik_lo = jnp.where(low_half, ik_ref[pl.ds(pl.multiple_of(kb * tk, tk), tk), :], 0.0)
        ik_sides = (ik_lo.astype(BF16), pltpu.roll(ik_lo, IDX_DIM, axis=1).astype(BF16))
        acc = jnp.zeros((tq, tk), F32)
        for h in range(IDX_HEADS):
            s = lax.dot_general(iq_pairs[h // 2], ik_sides[h % 2], (((1,), (1,)), ((), ())),
                                preferred_element_type=F32)
            acc = acc + jnp.maximum(s, 0.0) * _lane_tile(iwb_ref[h], tk)
        causal = (col + kb * tk) <= (row + i * tq)
        keys_ref[kb] = jnp.where(causal, _sortable_key(acc), jnp.int32(INT_MIN))
        return carry

    lax.fori_loop(0, nkb, idx_block, 0)

    fb = RADIX4_FIELD_BITS
    field_mask = (1 << fb) - 1
    inc1, inc2, inc3 = 1, 1 + (1 << fb), 1 + (1 << fb) + (1 << 2 * fb)

    def count3(c1, c2, c3):
        b1, b2, b3 = (jnp.broadcast_to(c, (tq, LANES)) for c in (c1, c2, c3))

        def cb(kb, acc):
            keys = keys_ref[kb]
            for t in range(tk // LANES):
                kk = keys[:, t * LANES:(t + 1) * LANES]
                acc = acc + jnp.where(kk >= b3, inc3, jnp.where(kk >= b2, inc2, jnp.where(kk >= b1, inc1, 0)))
            return acc

        acc = lax.fori_loop(0, nkb, cb, jnp.zeros((tq, LANES), I32))
        n3 = (acc >> (2 * fb)).astype(F32).sum(axis=-1, keepdims=True)
        low = ((acc & field_mask).astype(F32) + ((acc >> fb) & field_mask).astype(F32) * RADIX4_WIDE)
        low = low.sum(axis=-1, keepdims=True)
        n2 = jnp.floor(low * (1.0 / RADIX4_WIDE))
        return low - RADIX4_WIDE * n2, n2, n3

    n_valid = (lax.broadcasted_iota(I32, (tq, 1), 0) + (i * tq + 1)).astype(F32)
    thr = _kth_largest_key_radix4(count3, tq, k_sel, n_valid)
    thr = jnp.maximum(thr, jnp.int32(INT_MIN + 1))
    thr_b = _lane_tile(jnp.broadcast_to(thr, (tq, LANES)), tk)

    def mask_block(kb, carry):
        mb_ref[kb] = jnp.where(keys_ref[kb] >= thr_b, 0.0, MASK_NEG)
        return carry

    lax.fori_loop(0, nkb, mask_block, 0)

    scale = HEAD_DIM ** -0.5
    for g in range(N_KV_HEADS):
        qg = jnp.concatenate(
            [q_ref[:, (KV_REP * g + r) * HEAD_DIM:(KV_REP * g + r + 1) * HEAD_DIM] for r in range(KV_REP)], axis=0)
        m_ref[...] = jnp.full(m_ref.shape, MASK_NEG, F32)
        acc_ref[...] = jnp.zeros(acc_ref.shape, F32)

        def block(kb, bias_idx, g=g, qg=qg):
            r0 = pl.multiple_of(kb * tk, tk)
            kblk = kv_ref[pl.ds(r0, tk), KV_K + g * HEAD_DIM:KV_K + (g + 1) * HEAD_DIM].astype(BF16)
            vblk = kv_ref[pl.ds(r0, tk), KV_V + g * HEAD_DIM:KV_V + (g + 1) * HEAD_DIM].astype(BF16)
            s = lax.dot_general(qg, kblk, (((1,), (1,)), ((), ())), preferred_element_type=F32) * scale
            s = s.reshape(KV_REP, tq, tk) + mb_ref[kb][None]
            if bias_idx is not None:
                s = s + bias_ref[bias_idx, KV_REP * g:KV_REP * (g + 1)]
            s = s.reshape(KV_REP * tq, tk)
            m_old = m_ref[...]
            m_new = jnp.maximum(m_old, s.max(axis=-1, keepdims=True))
            p = jnp.exp(s - _lane_tile(m_new, tk))
            alpha = jnp.exp(m_old - m_new)
            v_ones = jnp.concatenate([vblk, jnp.ones((tk, LANES), BF16)], axis=1)
            acc_ref[...] = (_lane_tile(alpha, HEAD_DIM + LANES) * acc_ref[...]
                            + jnp.dot(p.astype(BF16), v_ones, preferred_element_type=F32))
            m_ref[...] = m_new

        n_far = jnp.maximum(i - 1, 0)

        def far_pair(t, carry):
            block(2 * t, None)
            block(2 * t + 1, None)
            return carry

        lax.fori_loop(0, n_far // 2, far_pair, 0)

        @pl.when(n_far % 2 == 1)
        def _():
            block(n_far - 1, None)

        @pl.when(i >= 1)
        def _():
            block(i - 1, 1)
            block(i, 0)

        @pl.when(i == 0)
        def _():
            block(i, 0)
        out = acc_ref[:, :HEAD_DIM] / acc_ref[:, HEAD_DIM:]
        for r in range(KV_REP):
            h = KV_REP * g + r
            o_ref[:, h * HEAD_DIM:(h + 1) * HEAD_DIM] = out[r * tq:(r + 1) * tq].astype(o_ref.dtype)


def _attn_prompt(ha, kv, ikw, bias_tiles, batch, seq, tq):
    nq = seq // tq
    k_sel = min(TOPK_MAX, seq // 4)
    assert seq // LANES < (1 << RADIX4_FIELD_BITS)
    body = functools.partial(_attn_prompt_body, tq=tq, k_sel=k_sel)
    return pl.pallas_call(
        body,
        grid=(batch, nq),
        in_specs=[pl.BlockSpec((tq, QKV_W), lambda b, i: (b * nq + i, HA_Q // QKV_W)),
                  pl.BlockSpec((tq, IQ_W), lambda b, i: (b * nq + i, HA_IQ // IQ_W)),
                  pl.BlockSpec((tq, LANES), lambda b, i: (b * nq + i, 0)),
                  pl.BlockSpec((seq, 2 * KV_W), lambda b, i: (b, 0), pipeline_mode=pl.Buffered(1)),
                  pl.BlockSpec((seq, LANES), lambda b, i: (b, 0), pipeline_mode=pl.Buffered(1)),
                  pl.BlockSpec((2, N_HEADS, tq, tq), lambda b, i: (0, 0, 0, 0), pipeline_mode=pl.Buffered(1))],
        out_specs=pl.BlockSpec((tq, QKV_W), lambda b, i: (b * nq + i, 0)),
        out_shape=jax.ShapeDtypeStruct((batch * seq, QKV_W), BF16),
        scratch_shapes=[pltpu.VMEM((nq, tq, tq), I32),
                        pltpu.VMEM((nq, tq, tq), F32),
                        pltpu.VMEM((IDX_HEADS, tq, LANES), F32),
                        pltpu.VMEM((KV_REP * tq, LANES), F32),
                        pltpu.VMEM((KV_REP * tq, HEAD_DIM + LANES), F32)],
        compiler_params=_cp(("arbitrary", "arbitrary")),
        name="attn_prompt",
    )(ha, ha, ikw, kv, ikw, bias_tiles)


def _page_spec(block, layer, group, slot):
    return pl.BlockSpec((None, None) + block, lambda b, p, pt: (layer, pt[b, p * group + slot], 0, 0))


def _idx_sample_body(pt_ref, iq_ref, iw_ref, *refs, n_tok, group):
    pages, o_ref = refs[:group], refs[group]
    ikt = jnp.concatenate([r[...].astype(BF16) for r in pages], axis=1)
    s = jnp.dot(iq_ref[0], ikt, preferred_element_type=F32)
    r = jnp.maximum(s, 0.0) * _lane_tile(iw_ref[0], group * PAGE_SIZE)
    tok_scores = r.reshape(n_tok, IDX_HEADS, group * PAGE_SIZE).sum(axis=1)
    for t in range(group):
        o_ref[0, t] = tok_scores[:, t * PAGE_SIZE:(t + 1) * PAGE_SIZE]


def _idx_sample(page_table, iq_rows, iw_rows, cache_ik_t, layer, n_tok, group):
    db, n_pages = page_table.shape
    rows = n_tok * IDX_HEADS
    body = functools.partial(_idx_sample_body, n_tok=n_tok, group=group)
    grid_spec = pltpu.PrefetchScalarGridSpec(
        num_scalar_prefetch=1,
        grid=(db, n_pages // group),
        in_specs=[pl.BlockSpec((1, rows, IDX_DIM), lambda b, p, pt: (b, 0, 0)),
                  pl.BlockSpec((1, rows, LANES), lambda b, p, pt: (b, 0, 0))]
        + [_page_spec((IDX_DIM, PAGE_SIZE), layer, group, t) for t in range(group)],
        out_specs=pl.BlockSpec((1, group, n_tok, PAGE_SIZE), lambda b, p, pt: (b, p, 0, 0)),
    )
    return pl.pallas_call(
        body,
        grid_spec=grid_spec,
        out_shape=jax.ShapeDtypeStruct((db, n_pages, n_tok, PAGE_SIZE), F32),
        compiler_params=_cp(("arbitrary", "arbitrary")),
        name="idx_sample",
    )(page_table, iq_rows, iw_rows, *([cache_ik_t] * group))


def _attn_sample_body(pt_ref, sc_ref, scn_ref, q_ref, kn_ref, vn_ref, bl_ref, bn_ref, *refs,
                      n_tok, n_pages, k_sel, group):
    kp_refs, vp_refs = refs[:group], refs[group:2 * group]
    o_ref, thr_ref, m_ref, l_ref, acc_ref = refs[2 * group:]
    p = pl.program_id(1)
    n_steps = n_pages // group
    grp = KV_REP * n_tok
    width = group * PAGE_SIZE
    scale = HEAD_DIM ** -0.5
    r_i = lax.broadcasted_iota(I32, (n_tok, LANES), 0)
    c_i = lax.broadcasted_iota(I32, (n_tok, LANES), 1)
    keys_new = jnp.where(c_i <= r_i, _sortable_key(scn_ref[0]), jnp.int32(INT_MIN))

    @pl.when(p == 0)
    def _():
        keys = _sortable_key(sc_ref[0])

        def count_ge(cand):
            cand_b = jnp.broadcast_to(cand, (n_tok, LANES))
            c = jnp.where(keys >= cand_b[None], 1.0, 0.0).sum(axis=0) + jnp.where(keys_new >= cand_b, 1.0, 0.0)
            return c.sum(axis=-1, keepdims=True)

        thr = _kth_largest_key(count_ge, n_tok, float(k_sel))
        thr_ref[...] = jnp.broadcast_to(jnp.maximum(thr, jnp.int32(INT_MIN + 1)), (n_tok, LANES))
        m_ref[...] = jnp.full(m_ref.shape, MASK_NEG, F32)
        l_ref[...] = jnp.zeros(l_ref.shape, F32)
        acc_ref[...] = jnp.zeros(acc_ref.shape, F32)

    thr_b = thr_ref[...]

    def update(g, s, vblk):
        sl = slice(g * grp, (g + 1) * grp)
        m_old = m_ref[sl, :]
        m_new = jnp.maximum(m_old, s.max(axis=-1, keepdims=True))
        pr = jnp.exp(s - _lane_tile(m_new, s.shape[1]))
        alpha = jnp.exp(m_old - m_new)
        l_ref[sl, :] = alpha * l_ref[sl, :] + pr.sum(axis=-1, keepdims=True)
        acc_ref[sl, :] = alpha * acc_ref[sl, :] + jnp.dot(pr.astype(BF16), vblk, preferred_element_type=F32)
        m_ref[sl, :] = m_new

    def group_rows(mask_tok):
        return jnp.concatenate([mask_tok] * KV_REP, axis=0)

    base = p * group
    mb = jnp.concatenate(
        [jnp.where(_sortable_key(sc_ref[0, base + t]) >= thr_b, 0.0, MASK_NEG) for t in range(group)], axis=1)
    mb_g = group_rows(mb)
    is_last = p == n_steps - 1
    zeros_head = jnp.zeros((grp, width - PAGE_SIZE), F32)
    for g in range(N_KV_HEADS):
        qg = q_ref[0, g * grp:(g + 1) * grp, :]
        kg = jnp.concatenate([r[pl.ds(g, PAGE_SIZE, stride=N_KV_HEADS), :].astype(BF16) for r in kp_refs], axis=0)
        vg = jnp.concatenate([r[pl.ds(g, PAGE_SIZE, stride=N_KV_HEADS), :].astype(BF16) for r in vp_refs], axis=0)
        s = lax.dot_general(qg, kg, (((1,), (1,)), ((), ())), preferred_element_type=F32) * scale
        near = jnp.where(is_last, bl_ref[g * grp:(g + 1) * grp, :], 0.0)
        s = s + mb_g + (near if group == 1 else jnp.concatenate([zeros_head, near], axis=1))
        update(g, s, vg)

    @pl.when(is_last)
    def _():
        mbn_g = group_rows(jnp.where(keys_new >= thr_b, 0.0, MASK_NEG))
        for g in range(N_KV_HEADS):
            qg = q_ref[0, g * grp:(g + 1) * grp, :]
            kblk = kn_ref[0, :, g * HEAD_DIM:(g + 1) * HEAD_DIM].astype(BF16)
            vblk = vn_ref[0, :, g * HEAD_DIM:(g + 1) * HEAD_DIM].astype(BF16)
            s = lax.dot_general(qg, kblk, (((1,), (1,)), ((), ())), preferred_element_type=F32) * scale
            s = s + mbn_g + bn_ref[g * grp:(g + 1) * grp, :]
            update(g, s, vblk)
        o_ref[0] = acc_ref[...] / l_ref[...]


def _attn_sample(page_table, scores, scores_new, q_rows, k_new, v_new, bias_last, bias_new,
                 cache_k, cache_v, layer, n_tok, group):
    db, n_pages = page_table.shape
    past = n_pages * PAGE_SIZE
    rows = N_HEADS * n_tok
    k_sel = min(TOPK_MAX, (past + n_tok) // 4)
    body = functools.partial(_attn_sample_body, n_tok=n_tok, n_pages=n_pages, k_sel=k_sel, group=group)
    page = (PAGE_SIZE * N_KV_HEADS, HEAD_DIM)
    grid_spec = pltpu.PrefetchScalarGridSpec(
        num_scalar_prefetch=1,
        grid=(db, n_pages // group),
        in_specs=[pl.BlockSpec((1, n_pages, n_tok, PAGE_SIZE), lambda b, p, pt: (b, 0, 0, 0)),
                  pl.BlockSpec((1, n_tok, LANES), lambda b, p, pt: (b, 0, 0)),
                  pl.BlockSpec((1, rows, HEAD_DIM), lambda b, p, pt: (b, 0, 0)),
                  pl.BlockSpec((1, LANES, KV_W), lambda b, p, pt: (b, 0, 0)),
                  pl.BlockSpec((1, LANES, KV_W), lambda b, p, pt: (b, 0, 0)),
                  pl.BlockSpec((rows, LANES), lambda b, p, pt: (0, 0)),
                  pl.BlockSpec((rows, LANES), lambda b, p, pt: (0, 0))]
        + [_page_spec(page, layer, group, t) for t in range(group)]
        + [_page_spec(page, layer, group, t) for t in range(group)],
        out_specs=pl.BlockSpec((1, rows, HEAD_DIM), lambda b, p, pt: (b, 0, 0)),
        scratch_shapes=[pltpu.VMEM((n_tok, LANES), I32),
                        pltpu.VMEM((rows, LANES), F32),
                        pltpu.VMEM((rows, LANES), F32),
                        pltpu.VMEM((rows, HEAD_DIM), F32)],
    )
    return pl.pallas_call(
        body,
        grid_spec=grid_spec,
        out_shape=jax.ShapeDtypeStruct((db, rows, HEAD_DIM), F32),
        compiler_params=_cp(("arbitrary", "arbitrary")),
        name="attn_sample",
    )(page_table, scores, scores_new, q_rows, k_new, v_new, bias_last, bias_new,
      *([cache_k] * group), *([cache_v] * group))


def _idx_new_body(iq_ref, iw_ref, ik_ref, o_ref, *, n_tok):
    for b in range(iq_ref.shape[0]):
        s = lax.dot_general(iq_ref[b], ik_ref[b].astype(BF16), (((1,), (1,)), ((), ())),
                            preferred_element_type=F32)
        r = jnp.maximum(s, 0.0) * iw_ref[b]
        o_ref[b] = r.reshape(n_tok, IDX_HEADS, LANES).sum(axis=1)


def _idx_new(iq_rows, iw_rows, ik_new, n_tok):
    db = iq_rows.shape[0]
    return pl.pallas_call(
        functools.partial(_idx_new_body, n_tok=n_tok),
        out_shape=jax.ShapeDtypeStruct((db, n_tok, LANES), F32),
        name="idx_new",
    )(iq_rows, iw_rows, ik_new)


HALO = CONV_WIDTH - 1
HALO_PAD = 32
CONV_ROW_BLOCK = 128


CONV_CHUNKS = CONV_DIM // LANES


def _conv_prompt_body(glu_ref, cw_ref, cb_ref, g_ref, b_ref, o_ref, st_ref, hp_ref, c_ref, sh_ref, *, ts, ns):
    s_idx = pl.program_id(1)

    @pl.when(s_idx == 0)
    def _():
        hp_ref[:, 0:HALO_PAD, :] = jnp.zeros((CONV_CHUNKS, HALO_PAD, LANES), F32)

    for c in range(CONV_CHUNKS):
        cs = slice(c * LANES, (c + 1) * LANES)
        a = glu_ref[:, cs].astype(F32)
        gt = glu_ref[:, CONV_DIM + c * LANES:CONV_DIM + (c + 1) * LANES].astype(F32)
        hp_ref[c, HALO_PAD:HALO_PAD + ts, :] = a * _sigmoid(gt)
    off = HALO_PAD - HALO
    rows = min(ts, CONV_ROW_BLOCK)

    def chunk(c, carry):
        for r0 in range(0, ts, rows):
            acc = jnp.zeros((rows // F32_SUBLANES, F32_SUBLANES, LANES), F32)
            for shift in range(F32_SUBLANES):
                taps = range(shift, CONV_WIDTH, F32_SUBLANES)
                span = rows + taps[-1] - shift
                sh_ref[0:span, :] = hp_ref[c, off + r0 + shift:off + r0 + shift + span, :]
                for w in taps:
                    x = sh_ref[w - shift:w - shift + rows, :].reshape(rows // F32_SUBLANES, F32_SUBLANES, LANES)
                    acc = acc + x * cw_ref[c, w][None]
            c_ref[c, r0:r0 + rows, :] = acc.reshape(rows, LANES)
        return carry

    lax.fori_loop(0, CONV_CHUNKS, chunk, 0)

    conv = [c_ref[c] + cb_ref[:, c * LANES:(c + 1) * LANES] for c in range(CONV_CHUNKS)]
    mu = sum(v.sum(axis=-1, keepdims=True) for v in conv) / CONV_DIM
    var = sum(((v - mu) * (v - mu)).sum(axis=-1, keepdims=True) for v in conv) / CONV_DIM
    inv = lax.rsqrt(var + LN_EPS)
    for c in range(CONV_CHUNKS):
        cs = slice(c * LANES, (c + 1) * LANES)
        y = (conv[c] - mu) * inv * g_ref[:, cs] + b_ref[:, cs]
        o_ref[:, cs] = (y * _sigmoid(y)).astype(o_ref.dtype)

    @pl.when(s_idx == ns - 1)
    def _():
        for c in range(CONV_CHUNKS):
            st_ref[0, :, c * LANES:(c + 1) * LANES] = hp_ref[c, ts + off:ts + HALO_PAD, :]

    hp_ref[:, off:HALO_PAD, :] = hp_ref[:, ts + off:ts + HALO_PAD, :]


def _conv_prompt(ha, conv_w, conv_b, ln_g, ln_b, batch, seq, ts):
    ns = seq // ts
    body = functools.partial(_conv_prompt_body, ts=ts, ns=ns)
    vec = pl.BlockSpec((1, CONV_DIM), lambda b, s: (0, 0))
    return pl.pallas_call(
        body,
        grid=(batch, ns),
        in_specs=[pl.BlockSpec((ts, 2 * CONV_DIM), lambda b, s: (b * ns + s, HA_GLU // (2 * CONV_DIM))),
                  pl.BlockSpec((CONV_CHUNKS, CONV_WIDTH, F32_SUBLANES, LANES), lambda b, s: (0, 0, 0, 0)),
                  vec, vec, vec],
        out_specs=[pl.BlockSpec((ts, CONV_DIM), lambda b, s: (b * ns + s, 0)),
                   pl.BlockSpec((1, HALO, CONV_DIM), lambda b, s: (b, 0, 0))],
        out_shape=[jax.ShapeDtypeStruct((batch * seq, CONV_DIM), BF16),
                   jax.ShapeDtypeStruct((batch, HALO, CONV_DIM), F32)],
        scratch_shapes=[pltpu.VMEM((CONV_CHUNKS, HALO_PAD + ts, LANES), F32),
                        pltpu.VMEM((CONV_CHUNKS, ts, LANES), F32),
                        pltpu.VMEM((HALO_PAD + min(ts, CONV_ROW_BLOCK), LANES), F32)],
        compiler_params=_cp(("arbitrary", "arbitrary")),
        name="conv_prompt",
    )(ha, conv_w, conv_b, ln_g, ln_b)


def _sgu_prompt_body(uv_ref, g_ref, b_ref, w_ref, bs_ref, o_ref, *, ts):
    u = uv_ref[:, :SGU_DIM].astype(F32)
    vn = _ln_rows(uv_ref[:, SGU_DIM:].astype(F32), g_ref[...], b_ref[...]).astype(BF16)
    r_i = lax.broadcasted_iota(I32, (CHUNK, CHUNK), 0)
    c_i = lax.broadcasted_iota(I32, (CHUNK, CHUNK), 1)
    gd = SGU_DIM // SGU_GROUPS
    for g in range(SGU_GROUPS):
        wm = jnp.where(c_i <= r_i, w_ref[g], 0.0).astype(BF16)
        bias = bs_ref[:, g:g + 1]
        for c in range(ts // CHUNK):
            rs = slice(c * CHUNK, (c + 1) * CHUNK)
            gs = slice(g * gd, (g + 1) * gd)
            mixed = jnp.dot(wm, vn[rs, gs], preferred_element_type=F32) + bias
            o_ref[rs, gs] = (u[rs, gs] * mixed).astype(o_ref.dtype)


def _sgu_prompt(ha, ln_g, ln_b, w_s, b_s_t, batch, seq, ts):
    ns = seq // ts
    vec = pl.BlockSpec((1, SGU_DIM), lambda b, s: (0, 0))
    return pl.pallas_call(
        functools.partial(_sgu_prompt_body, ts=ts),
        grid=(batch, ns),
        in_specs=[pl.BlockSpec((ts, 2 * SGU_DIM), lambda b, s: (b * ns + s, HA_UV // (2 * SGU_DIM))),
                  vec, vec,
                  pl.BlockSpec((SGU_GROUPS, CHUNK, CHUNK), lambda b, s: (0, 0, 0)),
                  pl.BlockSpec((CHUNK, SGU_GROUPS), lambda b, s: (0, 0))],
        out_specs=pl.BlockSpec((ts, SGU_DIM), lambda b, s: (b * ns + s, 0)),
        out_shape=jax.ShapeDtypeStruct((batch * seq, SGU_DIM), BF16),
        compiler_params=_cp(("arbitrary", "arbitrary")),
        name="sgu_prompt",
    )(ha, ln_g, ln_b, w_s, b_s_t)


def _mix_sample_body(glu_ref, uv_ref, st_ref, cw_ref, cb_ref, cg_ref, cbt_ref, sg_ref, sb_ref, wv_ref, bv_ref,
                     co_ref, nst_ref, so_ref, vn_ref, hp_ref, *, db, n_tok):
    cw = cw_ref[...]
    for b in range(db):
        rs = slice(b * n_tok, (b + 1) * n_tok)
        a = glu_ref[rs, :CONV_DIM].astype(F32)
        gt = glu_ref[rs, CONV_DIM:].astype(F32)
        hp_ref[0:HALO, :] = st_ref[b]
        hp_ref[HALO:HALO + n_tok, :] = a * _sigmoid(gt)
        rows = [jnp.sum(hp_ref[t:t + CONV_WIDTH, :] * cw, axis=0, keepdims=True) for t in range(n_tok)]
        c = jnp.concatenate(rows, axis=0) + cb_ref[...]
        y = _ln_rows(c, cg_ref[...], cbt_ref[...])
        co_ref[rs, :] = y * _sigmoid(y)
        nst_ref[b] = hp_ref[n_tok:n_tok + HALO, :]
        u = uv_ref[rs, :SGU_DIM].astype(F32)
        vn = _ln_rows(uv_ref[rs, SGU_DIM:].astype(F32), sg_ref[...], sb_ref[...])
        vn_ref[rs, :] = vn
        mixed = []
        for t in range(n_tok):
            acc = bv_ref[t:t + 1, :]
            for s in range(t + 1):
                acc = acc + wv_ref[t, s:s + 1, :] * vn[s:s + 1, :]
            mixed.append(acc)
        so_ref[rs, :] = u * jnp.concatenate(mixed, axis=0)


def _mix_sample(glu, uv, state, conv_w, conv_b, cln_g, cln_b, sln_g, sln_b, wv, bv, db, n_tok):
    rows = db * n_tok
    return pl.pallas_call(
        functools.partial(_mix_sample_body, db=db, n_tok=n_tok),
        out_shape=[jax.ShapeDtypeStruct((rows, CONV_DIM), F32),
                   jax.ShapeDtypeStruct((db, HALO, CONV_DIM), F32),
                   jax.ShapeDtypeStruct((rows, SGU_DIM), F32),
                   jax.ShapeDtypeStruct((rows, SGU_DIM), F32)],
        scratch_shapes=[pltpu.VMEM((_round_up(HALO + n_tok, 8), CONV_DIM), F32)],
        name="mix_sample",
    )(glu, uv, state, conv_w, conv_b, cln_g, cln_b, sln_g, sln_b, wv, bv)


def _router_body(x_ref, w_ref, b_ref, ti_ref, tg_ref, *, n_exp):
    logits = jnp.dot(x_ref[...], w_ref[...], preferred_element_type=F32,
                     precision=lax.Precision.HIGHEST) + b_ref[...]
    lane = lax.broadcasted_iota(I32, logits.shape, 1)
    lane_f = lane.astype(F32)
    neg = -jnp.inf
    lg = jnp.where(lane < n_exp, logits, neg)
    v1 = lg.max(axis=-1, keepdims=True)
    i1 = jnp.where(lg == v1, lane_f, float(LANES)).min(axis=-1, keepdims=True)
    lg2 = jnp.where(lane_f == i1, neg, lg)
    v2 = lg2.max(axis=-1, keepdims=True)
    i2 = jnp.where(lg2 == v2, lane_f, float(LANES)).min(axis=-1, keepdims=True)
    e = jnp.exp(v2 - v1)
    den = 1.0 + e
    ti_ref[...] = jnp.where(lane == 0, i1, jnp.where(lane == 1, i2, 0.0)).astype(I32)
    tg_ref[...] = jnp.where(lane == 0, 1.0 / den, jnp.where(lane == 1, e / den, 0.0))


def _router(x, w_pad, b_pad, n_exp, tm):
    m, d = x.shape
    return pl.pallas_call(
        functools.partial(_router_body, n_exp=n_exp),
        grid=(m // tm,),
        in_specs=[pl.BlockSpec((tm, d), lambda i: (i, 0)),
                  pl.BlockSpec((d, LANES), lambda i: (0, 0)),
                  pl.BlockSpec((1, LANES), lambda i: (0, 0))],
        out_specs=[pl.BlockSpec((tm, LANES), lambda i: (i, 0)),
                   pl.BlockSpec((tm, LANES), lambda i: (i, 0))],
        out_shape=[jax.ShapeDtypeStruct((m, LANES), I32), jax.ShapeDtypeStruct((m, LANES), F32)],
        compiler_params=_cp(("arbitrary",)),
        name="moe_router",
    )(x, w_pad, b_pad)


def _row_copy(src_hbm, dst, src_row, dst_row, sem):
    return pltpu.make_async_copy(src_hbm.at[pl.ds(src_row, 1)], dst.at[pl.ds(dst_row, 1)], sem)


ROW_DMA_UNROLL = 8
ROW_DMA_SLOTS = 2


def _prefetch_chunks(issue, wait):
    c = pl.program_id(0)
    ahead = ROW_DMA_SLOTS - 1

    @pl.when(c == 0)
    def _():
        for first in range(ahead):
            @pl.when(first < pl.num_programs(0))
            def _():
                issue(first, first)

    @pl.when(c + ahead < pl.num_programs(0))
    def _():
        issue(c + ahead, (c + ahead) % ROW_DMA_SLOTS)

    slot = c % ROW_DMA_SLOTS
    wait(slot)
    return slot


def _dispatch_body(src_ref, x_hbm, o_ref, buf, sem, *, rows):
    def issue(chunk, slot):
        def start(u, carry):
            for v in range(ROW_DMA_UNROLL):
                r = v * (rows // ROW_DMA_UNROLL) + u
                _row_copy(x_hbm, buf.at[slot], src_ref[chunk * rows + r], r, sem.at[slot]).start()
            return carry

        lax.fori_loop(0, rows // ROW_DMA_UNROLL, start, 0)

    def wait(slot):
        def w(r, carry):
            _row_copy(x_hbm, buf.at[slot], 0, r, sem.at[slot]).wait()
            return carry

        lax.fori_loop(0, rows, w, 0)

    slot = _prefetch_chunks(issue, wait)
    o_ref[...] = buf[slot].astype(o_ref.dtype)


def _dispatch(src, x, n_slots, rows):
    d = x.shape[1]
    grid_spec = pltpu.PrefetchScalarGridSpec(
        num_scalar_prefetch=1,
        grid=(n_slots // rows,),
        in_specs=[pl.BlockSpec(memory_space=pl.ANY)],
        out_specs=pl.BlockSpec((rows, d), lambda i, s: (i, 0)),
        scratch_shapes=[pltpu.VMEM((ROW_DMA_SLOTS, rows, d), x.dtype), pltpu.SemaphoreType.DMA((ROW_DMA_SLOTS,))],
    )
    assert rows % ROW_DMA_UNROLL == 0
    return pl.pallas_call(
        functools.partial(_dispatch_body, rows=rows),
        grid_spec=grid_spec,
        out_shape=jax.ShapeDtypeStruct((n_slots, d), BF16),
        compiler_params=_cp(("arbitrary",)),
        name="moe_dispatch",
    )(src, x)


def _tile_copy(src, dst, sem):
    return pltpu.make_async_copy(src, dst, sem)


def _moe_ffn_body(exp_ref, row0_ref, nsub_ref, jmap_ref, nzero_ref, xs_hbm, w1_ref, w3_ref, w2_ref, ys_hbm,
                  xbuf, acc, w1b, w3b, w2b, sem, *, nj, sub):
    w = pl.program_id(0)
    j = pl.program_id(1)
    nsub = nsub_ref[w]
    row0 = row0_ref[w]
    nzero = nzero_ref[w]

    def zero_copy(t):
        r = pl.multiple_of(row0 + t * sub, sub)
        return _tile_copy(acc.at[pl.ds(0, sub)], ys_hbm.at[pl.ds(r, sub)], sem)

    @pl.when(jnp.logical_and(j == 0, nzero > 0))
    def _():
        acc[0:sub, :] = jnp.zeros((sub, acc.shape[1]), F32)

        def zs(t, carry):
            zero_copy(t).start()
            return carry

        def zw(t, carry):
            zero_copy(t).wait()
            return carry

        lax.fori_loop(0, nzero, zs, 0)
        lax.fori_loop(0, nzero, zw, 0)

    def in_copy(t):
        r = pl.multiple_of(t * sub, sub)
        return _tile_copy(xs_hbm.at[pl.ds(pl.multiple_of(row0 + r, sub), sub)], xbuf.at[pl.ds(r, sub)], sem)

    def out_copy(t):
        r = pl.multiple_of(t * sub, sub)
        return _tile_copy(acc.at[pl.ds(r, sub)], ys_hbm.at[pl.ds(pl.multiple_of(row0 + r, sub), sub)], sem)

    def for_tiles(fn):
        def it(t, carry):
            fn(t)
            return carry
        lax.fori_loop(0, nsub, it, 0)

    def zero_tile(t):
        acc[pl.ds(pl.multiple_of(t * sub, sub), sub), :] = jnp.zeros((sub, acc.shape[1]), F32)

    @pl.when(j == 0)
    def _():
        for_tiles(lambda t: in_copy(t).start())
        for_tiles(zero_tile)
        for_tiles(lambda t: in_copy(t).wait())

    def tile(t, n, w1v, w3v, w2v):
        r = pl.multiple_of(t * sub, sub)
        x = xbuf[pl.ds(r, n * sub), :]
        a = jnp.dot(x, w1v, preferred_element_type=F32)
        c = jnp.dot(x, w3v, preferred_element_type=F32)
        h = (a * _sigmoid(a) * c).astype(BF16)
        acc[pl.ds(r, n * sub), :] += jnp.dot(h, w2v, preferred_element_type=F32)

    def cast_weights():
        return w1_ref[...].astype(BF16), w3_ref[...].astype(BF16), w2_ref[...].astype(BF16)

    per_trip = 2 * MOE_WIDE
    n_trips = nsub // per_trip

    def trip(t, wv):
        tile(per_trip * t, MOE_WIDE, *wv)
        tile(per_trip * t + MOE_WIDE, MOE_WIDE, *wv)

    @pl.when(n_trips >= 1)
    def _():
        wv = cast_weights()
        w1b[...], w3b[...], w2b[...] = wv
        trip(0, wv)

    @pl.when(jnp.logical_and(n_trips == 0, nsub > 0))
    def _():
        w1b[...], w3b[...], w2b[...] = cast_weights()

    def later_trip(t, carry):
        trip(t, (w1b[...], w3b[...], w2b[...]))
        return carry

    lax.fori_loop(1, n_trips, later_trip, 0)

    def rest(t, carry):
        tile(t, 1, w1b[...], w3b[...], w2b[...])
        return carry

    lax.fori_loop(n_trips * per_trip, nsub, rest, 0)

    @pl.when(j == nj - 1)
    def _():
        for_tiles(lambda t: out_copy(t).start())
        for_tiles(lambda t: out_copy(t).wait())


def _moe_ffn(item_exp, item_row0, item_nsub, item_jlast, item_nzero, xs, w1, w3, w2, layer, n_items, super_rows,
             sub, tf):
    n_slots, d = xs.shape
    f = w1.shape[3]
    nj = f // tf

    def jj(w, j, jl):
        return jnp.where(jl[w] > 0, j, nj - 1)

    grid_spec = pltpu.PrefetchScalarGridSpec(
        num_scalar_prefetch=5,
        grid=(n_items, nj),
        in_specs=[pl.BlockSpec(memory_space=pl.ANY),
                  pl.BlockSpec((None, None, d, tf), lambda w, j, ex, r0, ns, jl, nz: (layer, ex[w], 0, jj(w, j, jl))),
                  pl.BlockSpec((None, None, d, tf), lambda w, j, ex, r0, ns, jl, nz: (layer, ex[w], 0, jj(w, j, jl))),
                  pl.BlockSpec((None, None, tf, d), lambda w, j, ex, r0, ns, jl, nz: (layer, ex[w], jj(w, j, jl), 0))],
        out_specs=pl.BlockSpec(memory_space=pl.ANY),
        scratch_shapes=[pltpu.VMEM((super_rows, d), BF16),
                        pltpu.VMEM((super_rows, d), F32),
                        pltpu.VMEM((d, tf), BF16),
                        pltpu.VMEM((d, tf), BF16),
                        pltpu.VMEM((tf, d), BF16),
                        pltpu.SemaphoreType.DMA(())],
    )
    return pl.pallas_call(
        functools.partial(_moe_ffn_body, nj=nj, sub=sub),
        grid_spec=grid_spec,
        out_shape=jax.ShapeDtypeStruct((n_slots, d), F32),
        compiler_params=_cp(("arbitrary", "arbitrary")),
        name="moe_ffn",
    )(item_exp, item_row0, item_nsub, item_jlast, item_nzero, xs, w1, w3, w2)


def _combine_body(pos_ref, ys_hbm, tg_ref, x_ref, g_ref, b_ref, of_ref, ob_ref, buf, sem, *, tb, alpha):
    def issue(chunk, slot):
        def start(u, carry):
            for v in range(ROW_DMA_UNROLL // TOP_K_EXPERTS):
                r = u * (ROW_DMA_UNROLL // TOP_K_EXPERTS) + v
                for k in range(TOP_K_EXPERTS):
                    p = pos_ref[TOP_K_EXPERTS * (chunk * tb + r) + k]
                    _row_copy(ys_hbm, buf.at[slot, k], p, r, sem.at[slot]).start()
            return carry

        lax.fori_loop(0, tb * TOP_K_EXPERTS // ROW_DMA_UNROLL, start, 0)

    def wait(slot):
        def w(r, carry):
            for k in range(TOP_K_EXPERTS):
                _row_copy(ys_hbm, buf.at[slot, k], 0, r, sem.at[slot]).wait()
            return carry

        lax.fori_loop(0, tb, w, 0)

    slot = _prefetch_chunks(issue, wait)
    f = tg_ref[:, 0:1] * buf[slot, 0] + tg_ref[:, 1:2] * buf[slot, 1]
    y = _ln_rows(alpha * x_ref[...] + f, g_ref[...], b_ref[...])
    of_ref[...] = y
    ob_ref[...] = y.astype(BF16)


def _combine_ln(pos, ys, tg, x, g, b, alpha, tb):
    m, d = x.shape
    grid_spec = pltpu.PrefetchScalarGridSpec(
        num_scalar_prefetch=1,
        grid=(m // tb,),
        in_specs=[pl.BlockSpec(memory_space=pl.ANY),
                  pl.BlockSpec((tb, LANES), lambda i, p: (i, 0)),
                  pl.BlockSpec((tb, d), lambda i, p: (i, 0)),
                  pl.BlockSpec((1, d), lambda i, p: (0, 0)),
                  pl.BlockSpec((1, d), lambda i, p: (0, 0))],
        out_specs=[pl.BlockSpec((tb, d), lambda i, p: (i, 0)),
                   pl.BlockSpec((tb, d), lambda i, p: (i, 0))],
        scratch_shapes=[pltpu.VMEM((ROW_DMA_SLOTS, TOP_K_EXPERTS, tb, d), F32),
                        pltpu.SemaphoreType.DMA((ROW_DMA_SLOTS,))],
    )
    assert tb * TOP_K_EXPERTS % ROW_DMA_UNROLL == 0
    return pl.pallas_call(
        functools.partial(_combine_body, tb=tb, alpha=alpha),
        grid_spec=grid_spec,
        out_shape=[jax.ShapeDtypeStruct((m, d), F32), jax.ShapeDtypeStruct((m, d), BF16)],
        compiler_params=_cp(("arbitrary",)),
        name="moe_combine_ln",
    )(pos, ys, tg, x, g, b)


def _moe_plan(top_i, n_real, n_exp, n_slots, n_items, super_rows, sub):
    a = top_i.shape[0] * TOP_K_EXPERTS
    e_flat = top_i.reshape(-1)
    tok = jnp.arange(a, dtype=I32) // TOP_K_EXPERTS
    real = tok < n_real
    onehot = jnp.logical_and(e_flat[:, None] == jnp.arange(n_exp, dtype=I32)[None, :], real[:, None]).astype(I32)
    csum = jnp.cumsum(onehot, axis=0)
    rank = jnp.take_along_axis(csum, e_flat[:, None], axis=1)[:, 0] - 1
    counts = csum[-1]
    padded = (counts + sub - 1) // sub * sub
    gstart = jnp.cumsum(padded) - padded
    pos = jnp.where(real, gstart[e_flat] + rank, 0).astype(I32)
    src = jnp.zeros((n_slots,), I32).at[jnp.where(real, pos, n_slots)].set(tok, mode="drop")
    n_super = (padded + super_rows - 1) // super_rows
    iend = jnp.cumsum(n_super)
    istart = iend - n_super
    total = iend[-1]
    wi = jnp.arange(n_items, dtype=I32)
    valid = wi < total
    e_w = jnp.minimum(jnp.searchsorted(iend, wi, side="right"), n_exp - 1).astype(I32)
    last_e = e_w[jnp.maximum(total - 1, 0)]
    e_w = jnp.where(valid, e_w, last_e)
    k_w = wi - istart[e_w]
    subs_e = padded[e_w] // sub
    items_e = jnp.maximum(n_super[e_w], 1)
    base, extra = subs_e // items_e, subs_e % items_e
    nsub = jnp.where(valid, base + (k_w < extra).astype(I32), 0).astype(I32)
    first_sub = k_w * base + jnp.minimum(k_w, extra)
    z0 = jnp.sum(padded) + (wi - total) * super_rows
    nzero = jnp.where(valid, 0, jnp.clip((n_slots - z0) // sub, 0, super_rows // sub)).astype(I32)
    row0 = jnp.where(valid, gstart[e_w] + first_sub * sub, jnp.where(nzero > 0, z0, 0)).astype(I32)
    return pos, src, e_w, row0, nsub, valid.astype(I32), nzero


def _rel_bucket(dist):
    n = jnp.maximum(dist, 0)
    max_exact = REL_BUCKETS // 2
    nf = jnp.maximum(n, 1).astype(F32)
    large = max_exact + (jnp.log(nf / max_exact) / math.log(REL_MAX_DIST / max_exact)
                         * (REL_BUCKETS - max_exact)).astype(I32)
    large = jnp.minimum(large, REL_BUCKETS - 1)
    return jnp.where(n < max_exact, n, large)


def _shifted_bias(rel_bias, dist):
    t = rel_bias[_rel_bucket(dist)] - rel_bias[REL_BUCKETS - 1]
    t = jnp.where((dist >= 0)[..., None], t, 0.0)
    return jnp.moveaxis(t, -1, 0)


def _pick_tile(m_rows, target):
    return max(t for t in range(16, target + 1, 16) if m_rows % t == 0)


def _pad_rows(x, rows):
    return jnp.pad(x, ((0, rows - x.shape[0]), (0, 0)))


def kernel(x_prompt, x_sample, cache_k, cache_v, cache_ik, state_conv, page_table, w_in, conv_w, conv_b,
           conv_ln_g, conv_ln_b, sgu_ln_g, sgu_ln_b, sgu_w, sgu_b, w_pa, w_pb, w_pc, w_out, ln1_g, ln1_b,
           ln2_g, ln2_b, rel_bias, ffn_w1, ffn_w3, ffn_w2, moe_router, moe_router_b, moe_w1, moe_w3, moe_w2):
    batch, seq, d_model = x_prompt.shape
    db, n_tok, _ = x_sample.shape
    depth = w_in.shape[0]
    n_pool = cache_k.shape[1]
    n_exp = moe_router.shape[2]
    mp, ms = batch * seq, db * n_tok
    assert ms <= SAMPLE_TILE and mp % SAMPLE_TILE == 0
    m_all = mp + SAMPLE_TILE
    tm = m_all // N_ROW_TILES
    assert tm * N_ROW_TILES == m_all and tm % 16 == 0
    alpha = (2 * depth) ** 0.25
    tq = min(256, seq)

    xf = jnp.concatenate([x_prompt.reshape(mp, d_model), x_sample.reshape(ms, d_model),
                          jnp.zeros((m_all - mp - ms, d_model), F32)], axis=0)
    xb = xf.astype(BF16)
    w_in_t = jnp.swapaxes(w_in, 1, 2)
    ha_rows = ([W_Q_ROW0 + HA_TILE * t for t in range(QKV_W // HA_TILE)]
               + [W_GLU_ROW0 + HA_TILE * t for t in range(2 * CONV_DIM // HA_TILE)]
               + [W_UV_ROW0 + HA_TILE * t for t in range(2 * SGU_DIM // HA_TILE)]
               + [W_GATE_ROW0 + HA_TILE * t for t in range(N_BRANCH * d_model // HA_TILE)]
               + [W_IQ_ROW0 + HA_TILE * t for t in range(IQ_W // HA_TILE)])
    w_pa_b, w_pb_b, w_pc_b, w_out_b = (w.astype(BF16) for w in (w_pa, w_pb, w_pc, w_out))
    ffn_w2_b = ffn_w2.astype(BF16)

    kk = jnp.arange(2 * tq, dtype=I32)
    d_wrap = jnp.where(kk < tq, -kk, 2 * tq - kk)

    def toeplitz(offset):
        v = _shifted_bias(rel_bias, d_wrap + offset)
        return jnp.tile(v, (1, tq))[:, :tq * (2 * tq - 1)].reshape(N_HEADS, tq, 2 * tq - 1)[:, :, :tq]

    bias_tiles = jnp.stack([toeplitz(0), toeplitz(tq)])
    tok = jnp.arange(n_tok, dtype=I32)
    lane = jnp.arange(LANES, dtype=I32)
    d_last = PAGE_SIZE + tok[:, None] - lane[None, :]
    bias_last = _shifted_bias(rel_bias, d_last).reshape(N_HEADS * n_tok, LANES)
    bias_new = _shifted_bias(rel_bias, tok[:, None] - lane[None, :]).reshape(N_HEADS * n_tok, LANES)

    ck = cache_k.reshape(depth, n_pool, PAGE_SIZE * N_KV_HEADS, HEAD_DIM)
    cv = cache_v.reshape(depth, n_pool, PAGE_SIZE * N_KV_HEADS, HEAD_DIM)
    cik_t = jnp.swapaxes(cache_ik, 2, 3)
    n_pages = page_table.shape[1]
    idx_group = math.gcd(n_pages, IDX_PAGE_GROUP)
    attn_group = math.gcd(n_pages, ATTN_PAGE_GROUP)
    n_slots = _round_up(TOP_K_EXPERTS * (mp + ms) + n_exp * (MOE_SUB - 1), MOE_SUB)
    n_items = -(-n_slots // MOE_SUPER) + n_exp

    outs = {k: [] for k in ("k", "v", "ik", "conv_p", "conv_s", "sgu_s")}
    for l in range(depth):
        ha = _in_proj(xb, w_in_t, l, ha_rows, tm, HA_TILE)
        kv, ikw, k4, v4 = _kv_proj(xb, w_in_t, l, tm)

        a_p = _attn_prompt(ha, kv, ikw, bias_tiles, batch, seq, tq)
        conv_w_rep = jnp.broadcast_to(conv_w[l].reshape(CONV_WIDTH, CONV_CHUNKS, 1, LANES).transpose(1, 0, 2, 3),
                                      (CONV_CHUNKS, CONV_WIDTH, F32_SUBLANES, LANES))
        c_p, conv_state_p = _conv_prompt(ha, conv_w_rep, conv_b[l][None], conv_ln_g[l][None], conv_ln_b[l][None],
                                         batch, seq, min(256, seq))
        s_p = _sgu_prompt(ha, sgu_ln_g[l][None], sgu_ln_b[l][None], sgu_w[l], sgu_b[l].T, batch, seq,
                          min(512, seq))

        has, kvs, ikws = ha[mp:mp + ms], kv[mp:mp + ms], ikw[mp:mp + ms]
        iq_rows = has[:, HA_IQ:HA_IQ + IQ_W].reshape(db, n_tok * IDX_HEADS, IDX_DIM)
        iw_rows = (ikws[:, IKW_IW:IKW_IW + IDX_HEADS] * (IDX_DIM ** -0.5 * IDX_HEADS ** -0.5)
                   ).reshape(db, n_tok * IDX_HEADS, 1)
        iw_rows = jnp.broadcast_to(iw_rows, (db, n_tok * IDX_HEADS, LANES))
        scores = _idx_sample(page_table, iq_rows, iw_rows, cik_t, l, n_tok, idx_group)
        pad_rows = LANES - n_tok
        ik_new = jnp.pad(ikws[:, :IDX_DIM].reshape(db, n_tok, IDX_DIM), ((0, 0), (0, pad_rows), (0, 0)))
        scores_new = _idx_new(iq_rows, iw_rows, ik_new, n_tok)
        q_rows = has[:, HA_Q:HA_Q + QKV_W].reshape(db, n_tok, N_HEADS, HEAD_DIM).transpose(0, 2, 1, 3)
        q_rows = q_rows.reshape(db, N_HEADS * n_tok, HEAD_DIM)
        k_new = jnp.pad(kvs[:, KV_K:KV_K + KV_W].reshape(db, n_tok, KV_W), ((0, 0), (0, pad_rows), (0, 0)))
        v_new = jnp.pad(kvs[:, KV_V:KV_V + KV_W].reshape(db, n_tok, KV_W), ((0, 0), (0, pad_rows), (0, 0)))
        a_s = _attn_sample(page_table, scores, scores_new, q_rows, k_new, v_new, bias_last, bias_new,
                           ck, cv, l, n_tok, attn_group)
        a_s = a_s.reshape(db, N_HEADS, n_tok, HEAD_DIM).transpose(0, 2, 1, 3).reshape(ms, QKV_W)
        gd = SGU_DIM // SGU_GROUPS
        wv = jnp.repeat(sgu_w[l][:, :n_tok, :n_tok].transpose(1, 2, 0), gd, axis=-1)
        bv = jnp.repeat(sgu_b[l][:, :n_tok].T, gd, axis=-1)
        c_s, conv_state_s, s_s, vn_s = _mix_sample(
            has[:, HA_GLU:HA_GLU + 2 * CONV_DIM], has[:, HA_UV:HA_UV + 2 * SGU_DIM], state_conv[l],
            conv_w[l], conv_b[l][None], conv_ln_g[l][None], conv_ln_b[l][None],
            sgu_ln_g[l][None], sgu_ln_b[l][None], wv, bv, db, n_tok)

        sample_rows = tuple(_pad_rows(t.astype(BF16), SAMPLE_TILE) for t in (a_s, c_s, s_s))
        merged = _merge((a_p, c_p, s_p), sample_rows, ha, w_pa_b, w_pb_b, w_pc_b, l, SAMPLE_TILE)
        x1f, x1b = _mm_ln(merged, w_out_b, l, xf, ln1_g[l][None], ln1_b[l][None], alpha, _pick_tile(m_all, 384))

        j = l // 2
        if l % 2 == 0:
            h = _swiglu_up(x1b, ffn_w1, ffn_w3, j, tm, 512)
            xf, xb = _mm_ln(h, ffn_w2_b, j, x1f, ln2_g[l][None], ln2_b[l][None], alpha, _pick_tile(m_all, 384))
        else:
            w_r = jnp.pad(moe_router[j], ((0, 0), (0, LANES - n_exp)))
            b_r = jnp.pad(moe_router_b[j], (0, LANES - n_exp))[None]
            ti, tg = _router(x1f, w_r, b_r, n_exp, _pick_tile(m_all, 528))
            pos, src, it_e, it_r0, it_ns, it_valid, it_nz = _moe_plan(ti[:, :TOP_K_EXPERTS], mp + ms, n_exp,
                                                                      n_slots, n_items, MOE_SUPER, MOE_SUB)
            xs = _dispatch(src, x1f, n_slots, MOE_SUB)
            ys = _moe_ffn(it_e, it_r0, it_ns, it_valid, it_nz, xs, moe_w1, moe_w3, moe_w2, j, n_items,
                          MOE_SUPER, MOE_SUB, MOE_TF)
            tb = COMBINE_TB if m_all % COMBINE_TB == 0 else 16
            xf, xb = _combine_ln(pos, ys, tg, x1f, ln2_g[l][None], ln2_b[l][None], alpha, tb)

        outs["k"].append(k4)
        outs["v"].append(v4)
        outs["ik"].append(ikw[:, :IDX_DIM])
        outs["conv_p"].append(conv_state_p)
        outs["conv_s"].append(conv_state_s)
        outs["sgu_s"].append(vn_s.reshape(db, n_tok, SGU_DIM))

    def split(name, rows_per_token, width_shape):
        st = jnp.stack(outs[name])
        p = st[:, :mp * rows_per_token].reshape((depth, batch, seq) + width_shape)
        s = st[:, mp * rows_per_token:(mp + ms) * rows_per_token].reshape((depth, db, n_tok) + width_shape)
        return p, s

    k_p, k_s = split("k", N_KV_HEADS, (N_KV_HEADS, HEAD_DIM))
    v_p, v_s = split("v", N_KV_HEADS, (N_KV_HEADS, HEAD_DIM))
    ik_p, ik_s = split("ik", 1, (IDX_DIM,))
    y_prompt = xf[:mp].reshape(batch, seq, d_model)
    y_sample = xf[mp:mp + ms].reshape(db, n_tok, d_model)
    return (y_prompt, y_sample, k_p, v_p, ik_p, jnp.stack(outs["conv_p"]), k_s, v_s, ik_s,
            jnp.stack(outs["conv_s"]), jnp.stack(outs["sgu_s"]))
```

```python
import functools
import math

import jax
import jax.numpy as jnp
import numpy as np
from jax import lax
from jax.experimental import pallas as pl
from jax.experimental.pallas import tpu as pltpu

F32 = jnp.float32
BF16 = jnp.bfloat16
I32 = jnp.int32

N_HEADS = 16
HEAD_DIM = 128
N_KV_HEADS = 4
KV_REP = N_HEADS // N_KV_HEADS
IDX_HEADS = 16
IDX_DIM = 64
TOPK_MAX = 256
PAGE_SIZE = 128
CONV_DIM = 1024
CONV_WIDTH = 31
SGU_DIM = 1024
SGU_GROUPS = 8
CHUNK = 128
N_BRANCH = 3
TOP_K_EXPERTS = 2
REL_BUCKETS = 32
REL_MAX_DIST = 128
LN_EPS = 1e-5

V7X_VMEM_BYTES = 64 * 2**20
VMEM_LIMIT = V7X_VMEM_BYTES - 8 * 2**20
LANES = 128
F32_SUBLANES = 8
INT_MIN = -(2**31)
MASK_NEG = -1e30

QKV_W = N_HEADS * HEAD_DIM
KV_W = N_KV_HEADS * HEAD_DIM
IQ_W = IDX_HEADS * IDX_DIM
HA_Q, HA_GLU, HA_UV, HA_GATE, HA_IQ = 0, 2048, 4096, 6144, 12288
HA_TILE = 1024
W_Q_ROW0 = 0
W_KV_ROW0 = W_Q_ROW0 + QKV_W
W_IQ_ROW0 = W_KV_ROW0 + 2 * KV_W
W_IK_ROW0 = W_IQ_ROW0 + IQ_W
W_GLU_ROW0 = W_IK_ROW0 + IDX_DIM + IDX_HEADS
W_UV_ROW0 = W_GLU_ROW0 + 2 * CONV_DIM
W_GATE_ROW0 = W_UV_ROW0 + 2 * SGU_DIM
KV_K, KV_V = 0, KV_W
IKW_IW = IDX_DIM

N_ROW_TILES = 8
SAMPLE_TILE = 256
MOE_SUB = 256
MOE_SUPER = 2048
MOE_TF = 256
MOE_WIDE = 2
COMBINE_TB = 128
IDX_PAGE_GROUP = 32
ATTN_PAGE_GROUP = 32


def _cp(sem, vmem=VMEM_LIMIT):
    return pltpu.CompilerParams(dimension_semantics=sem, vmem_limit_bytes=vmem)


def _round_up(x, m):
    return (x + m - 1) // m * m


def _ln_rows(z, g, b):
    mu = jnp.mean(z, axis=-1, keepdims=True)
    d = z - mu
    var = jnp.mean(d * d, axis=-1, keepdims=True)
    return d * lax.rsqrt(var + LN_EPS) * g + b


def _sigmoid(x):
    return 1.0 / (1.0 + jnp.exp(-x))


def _in_proj_body(off_ref, x_ref, w_ref, o_ref, wt_ref):
    @pl.when(pl.program_id(1) == 0)
    def _():
        wt_ref[...] = w_ref[0].T.astype(BF16)

    o_ref[...] = jnp.dot(x_ref[...], wt_ref[...], preferred_element_type=F32).astype(o_ref.dtype)


def _in_proj(x, w_t, layer, row_offsets, tm, tn):
    m, k = x.shape
    n_tiles = len(row_offsets)
    grid_spec = pltpu.PrefetchScalarGridSpec(
        num_scalar_prefetch=1,
        grid=(n_tiles, m // tm),
        in_specs=[pl.BlockSpec((tm, k), lambda j, i, off: (i, 0)),
                  pl.BlockSpec((pl.Element(1), pl.Element(tn), pl.Element(k)),
                               lambda j, i, off: (layer, pl.multiple_of(off[j], F32_SUBLANES), 0))],
        out_specs=pl.BlockSpec((tm, tn), lambda j, i, off: (i, j)),
        scratch_shapes=[pltpu.VMEM((k, tn), BF16)],
    )
    return pl.pallas_call(
        _in_proj_body,
        grid_spec=grid_spec,
        out_shape=jax.ShapeDtypeStruct((m, n_tiles * tn), BF16),
        compiler_params=_cp(("arbitrary", "arbitrary")),
        name="in_proj",
    )(jnp.asarray(row_offsets, I32), x, w_t)


def _kv_proj_body(x_ref, wkv_ref, wik_ref, kv_ref, ik_ref, k4_ref, v4_ref, wkv_t, wik_t):
    @pl.when(pl.program_id(0) == 0)
    def _():
        wkv_t[...] = wkv_ref[...].T.astype(BF16)
        wik_t[...] = wik_ref[...].T.astype(BF16)

    x = x_ref[...]
    tm = x.shape[0]
    kv = jnp.dot(x, wkv_t[...], preferred_element_type=F32)
    kv_ref[...] = kv
    ik_ref[...] = jnp.dot(x, wik_t[...], preferred_element_type=F32)
    for g in range(N_KV_HEADS):
        k4_ref[pl.ds(g, tm, stride=N_KV_HEADS), :] = kv[:, KV_K + g * HEAD_DIM:KV_K + (g + 1) * HEAD_DIM]
        v4_ref[pl.ds(g, tm, stride=N_KV_HEADS), :] = kv[:, KV_V + g * HEAD_DIM:KV_V + (g + 1) * HEAD_DIM]


def _kv_proj(x, w_t, layer, tm):
    m, k = x.shape
    assert W_KV_ROW0 % (2 * KV_W) == 0 and W_IK_ROW0 % LANES == 0
    row = lambda i: (i, 0)
    per_head = jax.ShapeDtypeStruct((m * N_KV_HEADS, HEAD_DIM), F32)
    return pl.pallas_call(
        _kv_proj_body,
        grid=(m // tm,),
        in_specs=[pl.BlockSpec((tm, k), row),
                  pl.BlockSpec((None, 2 * KV_W, k), lambda i: (layer, W_KV_ROW0 // (2 * KV_W), 0)),
                  pl.BlockSpec((None, LANES, k), lambda i: (layer, W_IK_ROW0 // LANES, 0))],
        out_specs=[pl.BlockSpec((tm, 2 * KV_W), row), pl.BlockSpec((tm, LANES), row),
                   pl.BlockSpec((tm * N_KV_HEADS, HEAD_DIM), row), pl.BlockSpec((tm * N_KV_HEADS, HEAD_DIM), row)],
        out_shape=[jax.ShapeDtypeStruct((m, 2 * KV_W), F32), jax.ShapeDtypeStruct((m, LANES), F32),
                   per_head, per_head],
        scratch_shapes=[pltpu.VMEM((k, 2 * KV_W), BF16), pltpu.VMEM((k, LANES), BF16)],
        compiler_params=_cp(("arbitrary",)),
        name="kv_proj",
    )(x, w_t, w_t)


def _row_halves(rows):
    half = rows // 2 // 16 * 16
    return (slice(0, half), slice(half, rows)) if half > 0 else (slice(0, rows),)


def _mm_ln_body(x_ref, w_ref, r_ref, g_ref, b_ref, of_ref, ob_ref, *, alpha):
    for rs in _row_halves(x_ref.shape[0]):
        z = jnp.dot(x_ref[rs, :], w_ref[...], preferred_element_type=F32) + alpha * r_ref[rs, :]
        y = _ln_rows(z, g_ref[...], b_ref[...])
        of_ref[rs, :] = y
        ob_ref[rs, :] = y.astype(BF16)


def _mm_ln(x, w, layer, resid, g, b, alpha, tm):
    m, k = x.shape
    n = w.shape[2]
    row = lambda i: (i, 0)
    return pl.pallas_call(
        functools.partial(_mm_ln_body, alpha=alpha),
        grid=(m // tm,),
        in_specs=[pl.BlockSpec((tm, k), row),
                  pl.BlockSpec((None, k, n), lambda i: (layer, 0, 0), pipeline_mode=pl.Buffered(1)),
                  pl.BlockSpec((tm, n), row),
                  pl.BlockSpec((1, n), lambda i: (0, 0)),
                  pl.BlockSpec((1, n), lambda i: (0, 0))],
        out_specs=[pl.BlockSpec((tm, n), row), pl.BlockSpec((tm, n), row)],
        out_shape=[jax.ShapeDtypeStruct((m, n), F32), jax.ShapeDtypeStruct((m, n), BF16)],
        compiler_params=_cp(("arbitrary",)),
        name="proj_ln",
    )(x, w, resid, g, b)


def _swiglu_up_body(x_ref, w1_ref, w3_ref, o_ref, w1b, w3b):
    @pl.when(pl.program_id(1) == 0)
    def _():
        w1b[...] = w1_ref[...].astype(BF16)
        w3b[...] = w3_ref[...].astype(BF16)

    for rs in _row_halves(x_ref.shape[0]):
        x = x_ref[rs, :]
        a = jnp.dot(x, w1b[...], preferred_element_type=F32)
        c = jnp.dot(x, w3b[...], preferred_element_type=F32)
        o_ref[rs, :] = (a * _sigmoid(a) * c).astype(o_ref.dtype)


def _swiglu_up(x, w1, w3, layer, tm, tf):
    m, k = x.shape
    f = w1.shape[2]
    return pl.pallas_call(
        _swiglu_up_body,
        grid=(f // tf, m // tm),
        in_specs=[pl.BlockSpec((tm, k), lambda j, i: (i, 0)),
                  pl.BlockSpec((None, k, tf), lambda j, i: (layer, 0, j)),
                  pl.BlockSpec((None, k, tf), lambda j, i: (layer, 0, j))],
        out_specs=pl.BlockSpec((tm, tf), lambda j, i: (i, j)),
        out_shape=jax.ShapeDtypeStruct((m, f), BF16),
        scratch_shapes=[pltpu.VMEM((k, tf), BF16), pltpu.VMEM((k, tf), BF16)],
        compiler_params=_cp(("arbitrary", "arbitrary")),
        name="ffn_up",
    )(x, w1, w3)


def _merge_body(ap_ref, cp_ref, sp_ref, as_ref, cs_ref, ss_ref, g_ref, wa_ref, wb_ref, wc_ref, o_ref, *, n_prompt):
    i = pl.program_id(0)
    d = o_ref.shape[1]

    def merge(a_ref, c_ref, s_ref):
        for rs in _row_halves(o_ref.shape[0]):
            pa = jnp.dot(a_ref[rs, :], wa_ref[...], preferred_element_type=F32)
            pb = jnp.dot(c_ref[rs, :], wb_ref[...], preferred_element_type=F32)
            pc = jnp.dot(s_ref[rs, :], wc_ref[...], preferred_element_type=F32)
            o = (_sigmoid(g_ref[rs, :d].astype(F32)) * pa + _sigmoid(g_ref[rs, d:2 * d].astype(F32)) * pb
                 + _sigmoid(g_ref[rs, 2 * d:].astype(F32)) * pc)
            o_ref[rs, :] = o.astype(o_ref.dtype)

    @pl.when(i < n_prompt)
    def _():
        merge(ap_ref, cp_ref, sp_ref)

    @pl.when(i >= n_prompt)
    def _():
        merge(as_ref, cs_ref, ss_ref)


def _merge(prompt, sample, ha, w_pa, w_pb, w_pc, layer, tm):
    m = ha.shape[0]
    d = w_pa.shape[2]
    n_prompt = prompt[0].shape[0] // tm
    assert prompt[0].shape[0] % tm == 0 and sample[0].shape[0] == tm and m == (n_prompt + 1) * tm
    assert HA_GATE % (N_BRANCH * d) == 0
    p_row = lambda i: (jnp.minimum(i, n_prompt - 1), 0)
    fixed = lambda i: (0, 0)
    return pl.pallas_call(
        functools.partial(_merge_body, n_prompt=n_prompt),
        grid=(n_prompt + 1,),
        in_specs=[pl.BlockSpec((tm, t.shape[1]), p_row) for t in prompt]
        + [pl.BlockSpec((tm, t.shape[1]), fixed) for t in sample]
        + [pl.BlockSpec((tm, N_BRANCH * d), lambda i: (i, HA_GATE // (N_BRANCH * d)))]
        + [pl.BlockSpec((None,) + w.shape[1:], lambda i: (layer, 0, 0), pipeline_mode=pl.Buffered(1))
           for w in (w_pa, w_pb, w_pc)],
        out_specs=pl.BlockSpec((tm, d), lambda i: (i, 0)),
        out_shape=jax.ShapeDtypeStruct((m, d), BF16),
        compiler_params=_cp(("arbitrary",)),
        name="branch_merge",
    )(*prompt, *sample, ha, w_pa, w_pb, w_pc)


def _sortable_key(score):
    bits = pltpu.bitcast(score, I32)
    return bits ^ ((bits >> 31) & jnp.int32(0x7FFFFFFF))


def _kth_largest_key(count_ge, rows, k):
    def step(it, t):
        cand = t + jnp.left_shift(jnp.int32(1), 31 - it)
        return jnp.where(count_ge(cand) >= k, cand, t)

    return lax.fori_loop(0, 32, step, jnp.full((rows, 1), INT_MIN, I32))


RADIX4_STEPS = 16
RADIX4_FIELD_BITS = 5
RADIX4_WIDE = 4096.0


def _kth_largest_key_radix4(count3, rows, k, n_valid):
    kf = float(k)

    def cond(state):
        it, _, _, pending = state
        return jnp.logical_and(it < RADIX4_STEPS, pending > 0.0)

    def body(state):
        it, t, cnt, _ = state
        one = jnp.left_shift(jnp.int32(1), 30 - 2 * it)
        c1 = t + one
        c2 = c1 + one
        c3 = c2 + one
        n1, n2, n3 = count3(c1, c2, c3)
        t = jnp.where(n3 >= kf, c3, jnp.where(n2 >= kf, c2, jnp.where(n1 >= kf, c1, t)))
        cnt = jnp.where(n3 >= kf, n3, jnp.where(n2 >= kf, n2, jnp.where(n1 >= kf, n1, cnt)))
        settled = jnp.logical_or(cnt == kf, n_valid <= kf)
        return it + 1, t, cnt, jnp.max(jnp.where(settled, 0.0, 1.0))

    init = (jnp.int32(0), jnp.full((rows, 1), INT_MIN, I32), jnp.full((rows, 1), -1.0, F32), jnp.float32(1.0))
    return lax.while_loop(cond, body, init)[1]


def _lane_tile(x, width):
    return x if width == LANES else jnp.concatenate([x] * (width // LANES), axis=1)


def _attn_prompt_body(q_ref, iq_ref, iw_ref, kv_ref, ik_ref, bias_ref, o_ref,
                      keys_ref, mb_ref, iwb_ref, m_ref, acc_ref, *, tq, k_sel):
    i = pl.program_id(1)
    nkb = i + 1
    tk = tq
    row = lax.broadcasted_iota(I32, (tq, tk), 0)
    col = lax.broadcasted_iota(I32, (tq, tk), 1)
    low_half = lax.broadcasted_iota(I32, (tk, LANES), 1) < IDX_DIM
    iw = iw_ref[...] * (IDX_DIM ** -0.5 * IDX_HEADS ** -0.5)
    for h in range(IDX_HEADS):
        iwb_ref[h] = jnp.broadcast_to(iw[:, IKW_IW + h:IKW_IW + h + 1], (tq, LANES))
    iq_pairs = [iq_ref[:, p * LANES:(p + 1) * LANES] for p in range(IDX_HEADS // 2)]

    def idx_block(kb, carry):
        ik_lo = jnp.where(low_half, ik_ref[pl.ds(pl.multiple_of(kb * tk, tk), tk), :], 0.0)
        ik_sides = (ik_lo.astype(BF16), pltpu.roll(ik_lo, IDX_DIM, axis=1).astype(BF16))
        acc = jnp.zeros((tq, tk), F32)
        for h in range(IDX_HEADS):
            s = lax.dot_general(iq_pairs[h // 2], ik_sides[h % 2], (((1,), (1,)), ((), ())),
                                preferred_element_type=F32)
            acc = acc + jnp.maximum(s, 0.0) * _lane_tile(iwb_ref[h], tk)
        causal = (col + kb * tk) <= (row + i * tq)
        keys_ref[kb] = jnp.where(causal, _sortable_key(acc), jnp.int32(INT_MIN))
        return carry

    lax.fori_loop(0, nkb, idx_block, 0)

    fb = RADIX4_FIELD_BITS
    field_mask = (1 << fb) - 1
    inc1, inc2, inc3 = 1, 1 + (1 << fb), 1 + (1 << fb) + (1 << 2 * fb)

    def count3(c1, c2, c3):
        b1, b2, b3 = (jnp.broadcast_to(c, (tq, LANES)) for c in (c1, c2, c3))

        def cb(kb, acc):
            keys = keys_ref[kb]
            for t in range(tk // LANES):
                kk = keys[:, t * LANES:(t + 1) * LANES]
                acc = acc + jnp.where(kk >= b3, inc3, jnp.where(kk >= b2, inc2, jnp.where(kk >= b1, inc1, 0)))
            return acc

        acc = lax.fori_loop(0, nkb, cb, jnp.zeros((tq, LANES), I32))
        n3 = (acc >> (2 * fb)).astype(F32).sum(axis=-1, keepdims=True)
        low = ((acc & field_mask).astype(F32) + ((acc >> fb) & field_mask).astype(F32) * RADIX4_WIDE)
        low = low.sum(axis=-1, keepdims=True)
        n2 = jnp.floor(low * (1.0 / RADIX4_WIDE))
        return low - RADIX4_WIDE * n2, n2, n3

    n_valid = (lax.broadcasted_iota(I32, (tq, 1), 0) + (i * tq + 1)).astype(F32)
    thr = _kth_largest_key_radix4(count3, tq, k_sel, n_valid)
    thr = jnp.maximum(thr, jnp.int32(INT_MIN + 1))
    thr_b = _lane_tile(jnp.broadcast_to(thr, (tq, LANES)), tk)

    def mask_block(kb, carry):
        mb_ref[kb] = jnp.where(keys_ref[kb] >= thr_b, 0.0, MASK_NEG)
        return carry

    lax.fori_loop(0, nkb, mask_block, 0)

    scale = HEAD_DIM ** -0.5
    for g in range(N_KV_HEADS):
        qg = jnp.concatenate(
            [q_ref[:, (KV_REP * g + r) * HEAD_DIM:(KV_REP * g + r + 1) * HEAD_DIM] for r in range(KV_REP)], axis=0)
        m_ref[...] = jnp.full(m_ref.shape, MASK_NEG, F32)
        acc_ref[...] = jnp.zeros(acc_ref.shape, F32)

        def block(kb, bias_idx, g=g, qg=qg):
            r0 = pl.multiple_of(kb * tk, tk)
            kblk = kv_ref[pl.ds(r0, tk), KV_K + g * HEAD_DIM:KV_K + (g + 1) * HEAD_DIM].astype(BF16)
            vblk = kv_ref[pl.ds(r0, tk), KV_V + g * HEAD_DIM:KV_V + (g + 1) * HEAD_DIM].astype(BF16)
            s = lax.dot_general(qg, kblk, (((1,), (1,)), ((), ())), preferred_element_type=F32) * scale
            s = s.reshape(KV_REP, tq, tk) + mb_ref[kb][None]
            if bias_idx is not None:
                s = s + bias_ref[bias_idx, KV_REP * g:KV_REP * (g + 1)]
            s = s.reshape(KV_REP * tq, tk)
            m_old = m_ref[...]
            m_new = jnp.maximum(m_old, s.max(axis=-1, keepdims=True))
            p = jnp.exp(s - _lane_tile(m_new, tk))
            alpha = jnp.exp(m_old - m_new)
            v_ones = jnp.concatenate([vblk, jnp.ones((tk, LANES), BF16)], axis=1)
            acc_ref[...] = (_lane_tile(alpha, HEAD_DIM + LANES) * acc_ref[...]
                            + jnp.dot(p.astype(BF16), v_ones, preferred_element_type=F32))
            m_ref[...] = m_new

        n_far = jnp.maximum(i - 1, 0)

        def far_pair(t, carry):
            block(2 * t, None)
            block(2 * t + 1, None)
            return carry

        lax.fori_loop(0, n_far // 2, far_pair, 0)

        @pl.when(n_far % 2 == 1)
        def _():
            block(n_far - 1, None)

        @pl.when(i >= 1)
        def _():
            block(i - 1, 1)
            block(i, 0)

        @pl.when(i == 0)
        def _():
            block(i, 0)
        out = acc_ref[:, :HEAD_DIM] / acc_ref[:, HEAD_DIM:]
        for r in range(KV_REP):
            h = KV_REP * g + r
            o_ref[:, h * HEAD_DIM:(h + 1) * HEAD_DIM] = out[r * tq:(r + 1) * tq].astype(o_ref.dtype)


def _attn_prompt(ha, kv, ikw, bias_tiles, batch, seq, tq):
    nq = seq // tq
    k_sel = min(TOPK_MAX, seq // 4)
    assert seq // LANES < (1 << RADIX4_FIELD_BITS)
    body = functools.partial(_attn_prompt_body, tq=tq, k_sel=k_sel)
    return pl.pallas_call(
        body,
        grid=(batch, nq),
        in_specs=[pl.BlockSpec((tq, QKV_W), lambda b, i: (b * nq + i, HA_Q // QKV_W)),
                  pl.BlockSpec((tq, IQ_W), lambda b, i: (b * nq + i, HA_IQ // IQ_W)),
                  pl.BlockSpec((tq, LANES), lambda b, i: (b * nq + i, 0)),
                  pl.BlockSpec((seq, 2 * KV_W), lambda b, i: (b, 0), pipeline_mode=pl.Buffered(1)),
                  pl.BlockSpec((seq, LANES), lambda b, i: (b, 0), pipeline_mode=pl.Buffered(1)),
                  pl.BlockSpec((2, N_HEADS, tq, tq), lambda b, i: (0, 0, 0, 0), pipeline_mode=pl.Buffered(1))],
        out_specs=pl.BlockSpec((tq, QKV_W), lambda b, i: (b * nq + i, 0)),
        out_shape=jax.ShapeDtypeStruct((batch * seq, QKV_W), BF16),
        scratch_shapes=[pltpu.VMEM((nq, tq, tq), I32),
                        pltpu.VMEM((nq, tq, tq), F32),
                        pltpu.VMEM((IDX_HEADS, tq, LANES), F32),
                        pltpu.VMEM((KV_REP * tq, LANES), F32),
                        pltpu.VMEM((KV_REP * tq, HEAD_DIM + LANES), F32)],
        compiler_params=_cp(("arbitrary", "arbitrary")),
        name="attn_prompt",
    )(ha, ha, ikw, kv, ikw, bias_tiles)


def _page_spec(block, layer, group, slot):
    return pl.BlockSpec((None, None) + block, lambda b, p, pt: (layer, pt[b, p * group + slot], 0, 0))


def _idx_sample_body(pt_ref, iq_ref, iw_ref, *refs, n_tok, group):
    pages, o_ref = refs[:group], refs[group]
    ikt = jnp.concatenate([r[...].astype(BF16) for r in pages], axis=1)
    s = jnp.dot(iq_ref[0], ikt, preferred_element_type=F32)
    r = jnp.maximum(s, 0.0) * _lane_tile(iw_ref[0], group * PAGE_SIZE)
    tok_scores = r.reshape(n_tok, IDX_HEADS, group * PAGE_SIZE).sum(axis=1)
    for t in range(group):
        o_ref[0, t] = tok_scores[:, t * PAGE_SIZE:(t + 1) * PAGE_SIZE]


def _idx_sample(page_table, iq_rows, iw_rows, cache_ik_t, layer, n_tok, group):
    db, n_pages = page_table.shape
    rows = n_tok * IDX_HEADS
    body = functools.partial(_idx_sample_body, n_tok=n_tok, group=group)
    grid_spec = pltpu.PrefetchScalarGridSpec(
        num_scalar_prefetch=1,
        grid=(db, n_pages // group),
        in_specs=[pl.BlockSpec((1, rows, IDX_DIM), lambda b, p, pt: (b, 0, 0)),
                  pl.BlockSpec((1, rows, LANES), lambda b, p, pt: (b, 0, 0))]
        + [_page_spec((IDX_DIM, PAGE_SIZE), layer, group, t) for t in range(group)],
        out_specs=pl.BlockSpec((1, group, n_tok, PAGE_SIZE), lambda b, p, pt: (b, p, 0, 0)),
    )
    return pl.pallas_call(
        body,
        grid_spec=grid_spec,
        out_shape=jax.ShapeDtypeStruct((db, n_pages, n_tok, PAGE_SIZE), F32),
        compiler_params=_cp(("arbitrary", "arbitrary")),
        name="idx_sample",
    )(page_table, iq_rows, iw_rows, *([cache_ik_t] * group))


def _attn_sample_body(pt_ref, sc_ref, scn_ref, q_ref, kn_ref, vn_ref, bl_ref, bn_ref, *refs,
                      n_tok, n_pages, k_sel, group):
    kp_refs, vp_refs = refs[:group], refs[group:2 * group]
    o_ref, thr_ref, m_ref, l_ref, acc_ref = refs[2 * group:]
    p = pl.program_id(1)
    n_steps = n_pages // group
    grp = KV_REP * n_tok
    width = group * PAGE_SIZE
    scale = HEAD_DIM ** -0.5
    r_i = lax.broadcasted_iota(I32, (n_tok, LANES), 0)
    c_i = lax.broadcasted_iota(I32, (n_tok, LANES), 1)
    keys_new = jnp.where(c_i <= r_i, _sortable_key(scn_ref[0]), jnp.int32(INT_MIN))

    @pl.when(p == 0)
    def _():
        keys = _sortable_key(sc_ref[0])

        def count_ge(cand):
            cand_b = jnp.broadcast_to(cand, (n_tok, LANES))
            c = jnp.where(keys >= cand_b[None], 1.0, 0.0).sum(axis=0) + jnp.where(keys_new >= cand_b, 1.0, 0.0)
            return c.sum(axis=-1, keepdims=True)

        thr = _kth_largest_key(count_ge, n_tok, float(k_sel))
        thr_ref[...] = jnp.broadcast_to(jnp.maximum(thr, jnp.int32(INT_MIN + 1)), (n_tok, LANES))
        m_ref[...] = jnp.full(m_ref.shape, MASK_NEG, F32)
        l_ref[...] = jnp.zeros(l_ref.shape, F32)
        acc_ref[...] = jnp.zeros(acc_ref.shape, F32)

    thr_b = thr_ref[...]

    def update(g, s, vblk):
        sl = slice(g * grp, (g + 1) * grp)
        m_old = m_ref[sl, :]
        m_new = jnp.maximum(m_old, s.max(axis=-1, keepdims=True))
        pr = jnp.exp(s - _lane_tile(m_new, s.shape[1]))
        alpha = jnp.exp(m_old - m_new)
        l_ref[sl, :] = alpha * l_ref[sl, :] + pr.sum(axis=-1, keepdims=True)
        acc_ref[sl, :] = alpha * acc_ref[sl, :] + jnp.dot(pr.astype(BF16), vblk, preferred_element_type=F32)
        m_ref[sl, :] = m_new

    def group_rows(mask_tok):
        return jnp.concatenate([mask_tok] * KV_REP, axis=0)

    base = p * group
    mb = jnp.concatenate(
        [jnp.where(_sortable_key(sc_ref[0, base + t]) >= thr_b, 0.0, MASK_NEG) for t in range(group)], axis=1)
    mb_g = group_rows(mb)
    is_last = p == n_steps - 1
    zeros_head = jnp.zeros((grp, width - PAGE_SIZE), F32)
    for g in range(N_KV_HEADS):
        qg = q_ref[0, g * grp:(g + 1) * grp, :]
        kg = jnp.concatenate([r[pl.ds(g, PAGE_SIZE, stride=N_KV_HEADS), :].astype(BF16) for r in kp_refs], axis=0)
        vg = jnp.concatenate([r[pl.ds(g, PAGE_SIZE, stride=N_KV_HEADS), :].astype(BF16) for r in vp_refs], axis=0)
        s = lax.dot_general(qg, kg, (((1,), (1,)), ((), ())), preferred_element_type=F32) * scale
        near = jnp.where(is_last, bl_ref[g * grp:(g + 1) * grp, :], 0.0)
        s = s + mb_g + (near if group == 1 else jnp.concatenate([zeros_head, near], axis=1))
        update(g, s, vg)

    @pl.when(is_last)
    def _():
        mbn_g = group_rows(jnp.where(keys_new >= thr_b, 0.0, MASK_NEG))
        for g in range(N_KV_HEADS):
            qg = q_ref[0, g * grp:(g + 1) * grp, :]
            kblk = kn_ref[0, :, g * HEAD_DIM:(g + 1) * HEAD_DIM].astype(BF16)
            vblk = vn_ref[0, :, g * HEAD_DIM:(g + 1) * HEAD_DIM].astype(BF16)
            s = lax.dot_general(qg, kblk, (((1,), (1,)), ((), ())), preferred_element_type=F32) * scale
            s = s + mbn_g + bn_ref[g * grp:(g + 1) * grp, :]
            update(g, s, vblk)
        o_ref[0] = acc_ref[...] / l_ref[...]


def _attn_sample(page_table, scores, scores_new, q_rows, k_new, v_new, bias_last, bias_new,
                 cache_k, cache_v, layer, n_tok, group):
    db, n_pages = page_table.shape
    past = n_pages * PAGE_SIZE
    rows = N_HEADS * n_tok
    k_sel = min(TOPK_MAX, (past + n_tok) // 4)
    body = functools.partial(_attn_sample_body, n_tok=n_tok, n_pages=n_pages, k_sel=k_sel, group=group)
    page = (PAGE_SIZE * N_KV_HEADS, HEAD_DIM)
    grid_spec = pltpu.PrefetchScalarGridSpec(
        num_scalar_prefetch=1,
        grid=(db, n_pages // group),
        in_specs=[pl.BlockSpec((1, n_pages, n_tok, PAGE_SIZE), lambda b, p, pt: (b, 0, 0, 0)),
                  pl.BlockSpec((1, n_tok, LANES), lambda b, p, pt: (b, 0, 0)),
                  pl.BlockSpec((1, rows, HEAD_DIM), lambda b, p, pt: (b, 0, 0)),
                  pl.BlockSpec((1, LANES, KV_W), lambda b, p, pt: (b, 0, 0)),
                  pl.BlockSpec((1, LANES, KV_W), lambda b, p, pt: (b, 0, 0)),
                  pl.BlockSpec((rows, LANES), lambda b, p, pt: (0, 0)),
                  pl.BlockSpec((rows, LANES), lambda b, p, pt: (0, 0))]
        + [_page_spec(page, layer, group, t) for t in range(group)]
        + [_page_spec(page, layer, group, t) for t in range(group)],
        out_specs=pl.BlockSpec((1, rows, HEAD_DIM), lambda b, p, pt: (b, 0, 0)),
        scratch_shapes=[pltpu.VMEM((n_tok, LANES), I32),
                        pltpu.VMEM((rows, LANES), F32),
                        pltpu.VMEM((rows, LANES), F32),
                        pltpu.VMEM((rows, HEAD_DIM), F32)],
    )
    return pl.pallas_call(
        body,
        grid_spec=grid_spec,
        out_shape=jax.ShapeDtypeStruct((db, rows, HEAD_DIM), F32),
        compiler_params=_cp(("arbitrary", "arbitrary")),
        name="attn_sample",
    )(page_table, scores, scores_new, q_rows, k_new, v_new, bias_last, bias_new,
      *([cache_k] * group), *([cache_v] * group))


def _idx_new_body(iq_ref, iw_ref, ik_ref, o_ref, *, n_tok):
    for b in range(iq_ref.shape[0]):
        s = lax.dot_general(iq_ref[b], ik_ref[b].astype(BF16), (((1,), (1,)), ((), ())),
                            preferred_element_type=F32)
        r = jnp.maximum(s, 0.0) * iw_ref[b]
        o_ref[b] = r.reshape(n_tok, IDX_HEADS, LANES).sum(axis=1)


def _idx_new(iq_rows, iw_rows, ik_new, n_tok):
    db = iq_rows.shape[0]
    return pl.pallas_call(
        functools.partial(_idx_new_body, n_tok=n_tok),
        out_shape=jax.ShapeDtypeStruct((db, n_tok, LANES), F32),
        name="idx_new",
    )(iq_rows, iw_rows, ik_new)


HALO = CONV_WIDTH - 1
HALO_PAD = 32
CONV_ROW_BLOCK = 128


CONV_CHUNKS = CONV_DIM // LANES


def _conv_prompt_body(glu_ref, cw_ref, cb_ref, g_ref, b_ref, o_ref, st_ref, hp_ref, c_ref, sh_ref, *, ts, ns):
    s_idx = pl.program_id(1)

    @pl.when(s_idx == 0)
    def _():
        hp_ref[:, 0:HALO_PAD, :] = jnp.zeros((CONV_CHUNKS, HALO_PAD, LANES), F32)

    for c in range(CONV_CHUNKS):
        cs = slice(c * LANES, (c + 1) * LANES)
        a = glu_ref[:, cs].astype(F32)
        gt = glu_ref[:, CONV_DIM + c * LANES:CONV_DIM + (c + 1) * LANES].astype(F32)
        hp_ref[c, HALO_PAD:HALO_PAD + ts, :] = a * _sigmoid(gt)
    off = HALO_PAD - HALO
    rows = min(ts, CONV_ROW_BLOCK)

    def chunk(c, carry):
        for r0 in range(0, ts, rows):
            acc = jnp.zeros((rows // F32_SUBLANES, F32_SUBLANES, LANES), F32)
            for shift in range(F32_SUBLANES):
                taps = range(shift, CONV_WIDTH, F32_SUBLANES)
                span = rows + taps[-1] - shift
                sh_ref[0:span, :] = hp_ref[c, off + r0 + shift:off + r0 + shift + span, :]
                for w in taps:
                    x = sh_ref[w - shift:w - shift + rows, :].reshape(rows // F32_SUBLANES, F32_SUBLANES, LANES)
                    acc = acc + x * cw_ref[c, w][None]
            c_ref[c, r0:r0 + rows, :] = acc.reshape(rows, LANES)
        return carry

    lax.fori_loop(0, CONV_CHUNKS, chunk, 0)

    conv = [c_ref[c] + cb_ref[:, c * LANES:(c + 1) * LANES] for c in range(CONV_CHUNKS)]
    mu = sum(v.sum(axis=-1, keepdims=True) for v in conv) / CONV_DIM
    var = sum(((v - mu) * (v - mu)).sum(axis=-1, keepdims=True) for v in conv) / CONV_DIM
    inv = lax.rsqrt(var + LN_EPS)
    for c in range(CONV_CHUNKS):
        cs = slice(c * LANES, (c + 1) * LANES)
        y = (conv[c] - mu) * inv * g_ref[:, cs] + b_ref[:, cs]
        o_ref[:, cs] = (y * _sigmoid(y)).astype(o_ref.dtype)

    @pl.when(s_idx == ns - 1)
    def _():
        for c in range(CONV_CHUNKS):
            st_ref[0, :, c * LANES:(c + 1) * LANES] = hp_ref[c, ts + off:ts + HALO_PAD, :]

    hp_ref[:, off:HALO_PAD, :] = hp_ref[:, ts + off:ts + HALO_PAD, :]


def _conv_prompt(ha, conv_w, conv_b, ln_g, ln_b, batch, seq, ts):
    ns = seq // ts
    body = functools.partial(_conv_prompt_body, ts=ts, ns=ns)
    vec = pl.BlockSpec((1, CONV_DIM), lambda b, s: (0, 0))
    return pl.pallas_call(
        body,
        grid=(batch, ns),
        in_specs=[pl.BlockSpec((ts, 2 * CONV_DIM), lambda b, s: (b * ns + s, HA_GLU // (2 * CONV_DIM))),
                  pl.BlockSpec((CONV_CHUNKS, CONV_WIDTH, F32_SUBLANES, LANES), lambda b, s: (0, 0, 0, 0)),
                  vec, vec, vec],
        out_specs=[pl.BlockSpec((ts, CONV_DIM), lambda b, s: (b * ns + s, 0)),
                   pl.BlockSpec((1, HALO, CONV_DIM), lambda b, s: (b, 0, 0))],
        out_shape=[jax.ShapeDtypeStruct((batch * seq, CONV_DIM), BF16),
                   jax.ShapeDtypeStruct((batch, HALO, CONV_DIM), F32)],
        scratch_shapes=[pltpu.VMEM((CONV_CHUNKS, HALO_PAD + ts, LANES), F32),
                        pltpu.VMEM((CONV_CHUNKS, ts, LANES), F32),
                        pltpu.VMEM((HALO_PAD + min(ts, CONV_ROW_BLOCK), LANES), F32)],
        compiler_params=_cp(("arbitrary", "arbitrary")),
        name="conv_prompt",
    )(ha, conv_w, conv_b, ln_g, ln_b)


def _sgu_prompt_body(uv_ref, g_ref, b_ref, w_ref, bs_ref, o_ref, *, ts):
    u = uv_ref[:, :SGU_DIM].astype(F32)
    vn = _ln_rows(uv_ref[:, SGU_DIM:].astype(F32), g_ref[...], b_ref[...]).astype(BF16)
    r_i = lax.broadcasted_iota(I32, (CHUNK, CHUNK), 0)
    c_i = lax.broadcasted_iota(I32, (CHUNK, CHUNK), 1)
    gd = SGU_DIM // SGU_GROUPS
    for g in range(SGU_GROUPS):
        wm = jnp.where(c_i <= r_i, w_ref[g], 0.0).astype(BF16)
        bias = bs_ref[:, g:g + 1]
        for c in range(ts // CHUNK):
            rs = slice(c * CHUNK, (c + 1) * CHUNK)
            gs = slice(g * gd, (g + 1) * gd)
            mixed = jnp.dot(wm, vn[rs, gs], preferred_element_type=F32) + bias
            o_ref[rs, gs] = (u[rs, gs] * mixed).astype(o_ref.dtype)


def _sgu_prompt(ha, ln_g, ln_b, w_s, b_s_t, batch, seq, ts):
    ns = seq // ts
    vec = pl.BlockSpec((1, SGU_DIM), lambda b, s: (0, 0))
    return pl.pallas_call(
        functools.partial(_sgu_prompt_body, ts=ts),
        grid=(batch, ns),
        in_specs=[pl.BlockSpec((ts, 2 * SGU_DIM), lambda b, s: (b * ns + s, HA_UV // (2 * SGU_DIM))),
                  vec, vec,
                  pl.BlockSpec((SGU_GROUPS, CHUNK, CHUNK), lambda b, s: (0, 0, 0)),
                  pl.BlockSpec((CHUNK, SGU_GROUPS), lambda b, s: (0, 0))],
        out_specs=pl.BlockSpec((ts, SGU_DIM), lambda b, s: (b * ns + s, 0)),
        out_shape=jax.ShapeDtypeStruct((batch * seq, SGU_DIM), BF16),
        compiler_params=_cp(("arbitrary", "arbitrary")),
        name="sgu_prompt",
    )(ha, ln_g, ln_b, w_s, b_s_t)


def _mix_sample_body(glu_ref, uv_ref, st_ref, cw_ref, cb_ref, cg_ref, cbt_ref, sg_ref, sb_ref, wv_ref, bv_ref,
                     co_ref, nst_ref, so_ref, vn_ref, hp_ref, *, db, n_tok):
    cw = cw_ref[...]
    for b in range(db):
        rs = slice(b * n_tok, (b + 1) * n_tok)
        a = glu_ref[rs, :CONV_DIM].astype(F32)
        gt = glu_ref[rs, CONV_DIM:].astype(F32)
        hp_ref[0:HALO, :] = st_ref[b]
        hp_ref[HALO:HALO + n_tok, :] = a * _sigmoid(gt)
        rows = [jnp.sum(hp_ref[t:t + CONV_WIDTH, :] * cw, axis=0, keepdims=True) for t in range(n_tok)]
        c = jnp.concatenate(rows, axis=0) + cb_ref[...]
        y = _ln_rows(c, cg_ref[...], cbt_ref[...])
        co_ref[rs, :] = y * _sigmoid(y)
        nst_ref[b] = hp_ref[n_tok:n_tok + HALO, :]
        u = uv_ref[rs, :SGU_DIM].astype(F32)
        vn = _ln_rows(uv_ref[rs, SGU_DIM:].astype(F32), sg_ref[...], sb_ref[...])
        vn_ref[rs, :] = vn
        mixed = []
        for t in range(n_tok):
            acc = bv_ref[t:t + 1, :]
            for s in range(t + 1):
                acc = acc + wv_ref[t, s:s + 1, :] * vn[s:s + 1, :]
            mixed.append(acc)
        so_ref[rs, :] = u * jnp.concatenate(mixed, axis=0)


def _mix_sample(glu, uv, state, conv_w, conv_b, cln_g, cln_b, sln_g, sln_b, wv, bv, db, n_tok):
    rows = db * n_tok
    return pl.pallas_call(
        functools.partial(_mix_sample_body, db=db, n_tok=n_tok),
        out_shape=[jax.ShapeDtypeStruct((rows, CONV_DIM), F32),
                   jax.ShapeDtypeStruct((db, HALO, CONV_DIM), F32),
                   jax.ShapeDtypeStruct((rows, SGU_DIM), F32),
                   jax.ShapeDtypeStruct((rows, SGU_DIM), F32)],
        scratch_shapes=[pltpu.VMEM((_round_up(HALO + n_tok, 8), CONV_DIM), F32)],
        name="mix_sample",
    )(glu, uv, state, conv_w, conv_b, cln_g, cln_b, sln_g, sln_b, wv, bv)


def _router_body(x_ref, w_ref, b_ref, ti_ref, tg_ref, *, n_exp):
    logits = jnp.dot(x_ref[...], w_ref[...], preferred_element_type=F32,
                     precision=lax.Precision.HIGHEST) + b_ref[...]
    lane = lax.broadcasted_iota(I32, logits.shape, 1)
    lane_f = lane.astype(F32)
    neg = -jnp.inf
    lg = jnp.where(lane < n_exp, logits, neg)
    v1 = lg.max(axis=-1, keepdims=True)
    i1 = jnp.where(lg == v1, lane_f, float(LANES)).min(axis=-1, keepdims=True)
    lg2 = jnp.where(lane_f == i1, neg, lg)
    v2 = lg2.max(axis=-1, keepdims=True)
    i2 = jnp.where(lg2 == v2, lane_f, float(LANES)).min(axis=-1, keepdims=True)
    e = jnp.exp(v2 - v1)
    den = 1.0 + e
    ti_ref[...] = jnp.where(lane == 0, i1, jnp.where(lane == 1, i2, 0.0)).astype(I32)
    tg_ref[...] = jnp.where(lane == 0, 1.0 / den, jnp.where(lane == 1, e / den, 0.0))


def _router(x, w_pad, b_pad, n_exp, tm):
    m, d = x.shape
    return pl.pallas_call(
        functools.partial(_router_body, n_exp=n_exp),
        grid=(m // tm,),
        in_specs=[pl.BlockSpec((tm, d), lambda i: (i, 0)),
                  pl.BlockSpec((d, LANES), lambda i: (0, 0)),
                  pl.BlockSpec((1, LANES), lambda i: (0, 0))],
        out_specs=[pl.BlockSpec((tm, LANES), lambda i: (i, 0)),
                   pl.BlockSpec((tm, LANES), lambda i: (i, 0))],
        out_shape=[jax.ShapeDtypeStruct((m, LANES), I32), jax.ShapeDtypeStruct((m, LANES), F32)],
        compiler_params=_cp(("arbitrary",)),
        name="moe_router",
    )(x, w_pad, b_pad)


def _row_copy(src_hbm, dst, src_row, dst_row, sem):
    return pltpu.make_async_copy(src_hbm.at[pl.ds(src_row, 1)], dst.at[pl.ds(dst_row, 1)], sem)


ROW_DMA_UNROLL = 8
ROW_DMA_SLOTS = 2


def _prefetch_chunks(issue, wait):
    c = pl.program_id(0)
    ahead = ROW_DMA_SLOTS - 1

    @pl.when(c == 0)
    def _():
        for first in range(ahead):
            @pl.when(first < pl.num_programs(0))
            def _():
                issue(first, first)

    @pl.when(c + ahead < pl.num_programs(0))
    def _():
        issue(c + ahead, (c + ahead) % ROW_DMA_SLOTS)

    slot = c % ROW_DMA_SLOTS
    wait(slot)
    return slot


def _dispatch_body(src_ref, x_hbm, o_ref, buf, sem, *, rows):
    def issue(chunk, slot):
        def start(u, carry):
            for v in range(ROW_DMA_UNROLL):
                r = v * (rows // ROW_DMA_UNROLL) + u
                _row_copy(x_hbm, buf.at[slot], src_ref[chunk * rows + r], r, sem.at[slot]).start()
            return carry

        lax.fori_loop(0, rows // ROW_DMA_UNROLL, start, 0)

    def wait(slot):
        def w(r, carry):
            _row_copy(x_hbm, buf.at[slot], 0, r, sem.at[slot]).wait()
            return carry

        lax.fori_loop(0, rows, w, 0)

    slot = _prefetch_chunks(issue, wait)
    o_ref[...] = buf[slot].astype(o_ref.dtype)


def _dispatch(src, x, n_slots, rows):
    d = x.shape[1]
    grid_spec = pltpu.PrefetchScalarGridSpec(
        num_scalar_prefetch=1,
        grid=(n_slots // rows,),
        in_specs=[pl.BlockSpec(memory_space=pl.ANY)],
        out_specs=pl.BlockSpec((rows, d), lambda i, s: (i, 0)),
        scratch_shapes=[pltpu.VMEM((ROW_DMA_SLOTS, rows, d), x.dtype), pltpu.SemaphoreType.DMA((ROW_DMA_SLOTS,))],
    )
    assert rows % ROW_DMA_UNROLL == 0
    return pl.pallas_call(
        functools.partial(_dispatch_body, rows=rows),
        grid_spec=grid_spec,
        out_shape=jax.ShapeDtypeStruct((n_slots, d), BF16),
        compiler_params=_cp(("arbitrary",)),
        name="moe_dispatch",
    )(src, x)


def _tile_copy(src, dst, sem):
    return pltpu.make_async_copy(src, dst, sem)


def _moe_ffn_body(exp_ref, row0_ref, nsub_ref, jmap_ref, nzero_ref, xs_hbm, w1_ref, w3_ref, w2_ref, ys_hbm,
                  xbuf, acc, w1b, w3b, w2b, sem, *, nj, sub):
    w = pl.program_id(0)
    j = pl.program_id(1)
    nsub = nsub_ref[w]
    row0 = row0_ref[w]
    nzero = nzero_ref[w]

    def zero_copy(t):
        r = pl.multiple_of(row0 + t * sub, sub)
        return _tile_copy(acc.at[pl.ds(0, sub)], ys_hbm.at[pl.ds(r, sub)], sem)

    @pl.when(jnp.logical_and(j == 0, nzero > 0))
    def _():
        acc[0:sub, :] = jnp.zeros((sub, acc.shape[1]), F32)

        def zs(t, carry):
            zero_copy(t).start()
            return carry

        def zw(t, carry):
            zero_copy(t).wait()
            return carry

        lax.fori_loop(0, nzero, zs, 0)
        lax.fori_loop(0, nzero, zw, 0)

    def in_copy(t):
        r = pl.multiple_of(t * sub, sub)
        return _tile_copy(xs_hbm.at[pl.ds(pl.multiple_of(row0 + r, sub), sub)], xbuf.at[pl.ds(r, sub)], sem)

    def out_copy(t):
        r = pl.multiple_of(t * sub, sub)
        return _tile_copy(acc.at[pl.ds(r, sub)], ys_hbm.at[pl.ds(pl.multiple_of(row0 + r, sub), sub)], sem)

    def for_tiles(fn):
        def it(t, carry):
            fn(t)
            return carry
        lax.fori_loop(0, nsub, it, 0)

    def zero_tile(t):
        acc[pl.ds(pl.multiple_of(t * sub, sub), sub), :] = jnp.zeros((sub, acc.shape[1]), F32)

    @pl.when(j == 0)
    def _():
        for_tiles(lambda t: in_copy(t).start())
        for_tiles(zero_tile)
        for_tiles(lambda t: in_copy(t).wait())

    def tile(t, n, w1v, w3v, w2v):
        r = pl.multiple_of(t * sub, sub)
        x = xbuf[pl.ds(r, n * sub), :]
        a = jnp.dot(x, w1v, preferred_element_type=F32)
        c = jnp.dot(x, w3v, preferred_element_type=F32)
        h = (a * _sigmoid(a) * c).astype(BF16)
        acc[pl.ds(r, n * sub), :] += jnp.dot(h, w2v, preferred_element_type=F32)

    def cast_weights():
        return w1_ref[...].astype(BF16), w3_ref[...].astype(BF16), w2_ref[...].astype(BF16)

    per_trip = 2 * MOE_WIDE
    n_trips = nsub // per_trip

    def trip(t, wv):
        tile(per_trip * t, MOE_WIDE, *wv)
        tile(per_trip * t + MOE_WIDE, MOE_WIDE, *wv)

    @pl.when(n_trips >= 1)
    def _():
        wv = cast_weights()
        w1b[...], w3b[...], w2b[...] = wv
        trip(0, wv)

    @pl.when(jnp.logical_and(n_trips == 0, nsub > 0))
    def _():
        w1b[...], w3b[...], w2b[...] = cast_weights()

    def later_trip(t, carry):
        trip(t, (w1b[...], w3b[...], w2b[...]))
        return carry

    lax.fori_loop(1, n_trips, later_trip, 0)

    def rest(t, carry):
        tile(t, 1, w1b[...], w3b[...], w2b[...])
        return carry

    lax.fori_loop(n_trips * per_trip, nsub, rest, 0)

    @pl.when(j == nj - 1)
    def _():
        for_tiles(lambda t: out_copy(t).start())
        for_tiles(lambda t: out_copy(t).wait())


def _moe_ffn(item_exp, item_row0, item_nsub, item_jlast, item_nzero, xs, w1, w3, w2, layer, n_items, super_rows,
             sub, tf):
    n_slots, d = xs.shape
    f = w1.shape[3]
    nj = f // tf

    def jj(w, j, jl):
        return jnp.where(jl[w] > 0, j, nj - 1)

    grid_spec = pltpu.PrefetchScalarGridSpec(
        num_scalar_prefetch=5,
        grid=(n_items, nj),
        in_specs=[pl.BlockSpec(memory_space=pl.ANY),
                  pl.BlockSpec((None, None, d, tf), lambda w, j, ex, r0, ns, jl, nz: (layer, ex[w], 0, jj(w, j, jl))),
                  pl.BlockSpec((None, None, d, tf), lambda w, j, ex, r0, ns, jl, nz: (layer, ex[w], 0, jj(w, j, jl))),
                  pl.BlockSpec((None, None, tf, d), lambda w, j, ex, r0, ns, jl, nz: (layer, ex[w], jj(w, j, jl), 0))],
        out_specs=pl.BlockSpec(memory_space=pl.ANY),
        scratch_shapes=[pltpu.VMEM((super_rows, d), BF16),
                        pltpu.VMEM((super_rows, d), F32),
                        pltpu.VMEM((d, tf), BF16),
                        pltpu.VMEM((d, tf), BF16),
                        pltpu.VMEM((tf, d), BF16),
                        pltpu.SemaphoreType.DMA(())],
    )
    return pl.pallas_call(
        functools.partial(_moe_ffn_body, nj=nj, sub=sub),
        grid_spec=grid_spec,
        out_shape=jax.ShapeDtypeStruct((n_slots, d), F32),
        compiler_params=_cp(("arbitrary", "arbitrary")),
        name="moe_ffn",
    )(item_exp, item_row0, item_nsub, item_jlast, item_nzero, xs, w1, w3, w2)


def _combine_body(pos_ref, ys_hbm, tg_ref, x_ref, g_ref, b_ref, of_ref, ob_ref, buf, sem, *, tb, alpha):
    def issue(chunk, slot):
        def start(u, carry):
            for v in range(ROW_DMA_UNROLL // TOP_K_EXPERTS):
                r = u * (ROW_DMA_UNROLL // TOP_K_EXPERTS) + v
                for k in range(TOP_K_EXPERTS):
                    p = pos_ref[TOP_K_EXPERTS * (chunk * tb + r) + k]
                    _row_copy(ys_hbm, buf.at[slot, k], p, r, sem.at[slot]).start()
            return carry

        lax.fori_loop(0, tb * TOP_K_EXPERTS // ROW_DMA_UNROLL, start, 0)

    def wait(slot):
        def w(r, carry):
            for k in range(TOP_K_EXPERTS):
                _row_copy(ys_hbm, buf.at[slot, k], 0, r, sem.at[slot]).wait()
            return carry

        lax.fori_loop(0, tb, w, 0)

    slot = _prefetch_chunks(issue, wait)
    f = tg_ref[:, 0:1] * buf[slot, 0] + tg_ref[:, 1:2] * buf[slot, 1]
    y = _ln_rows(alpha * x_ref[...] + f, g_ref[...], b_ref[...])
    of_ref[...] = y
    ob_ref[...] = y.astype(BF16)


def _combine_ln(pos, ys, tg, x, g, b, alpha, tb):
    m, d = x.shape
    grid_spec = pltpu.PrefetchScalarGridSpec(
        num_scalar_prefetch=1,
        grid=(m // tb,),
        in_specs=[pl.BlockSpec(memory_space=pl.ANY),
                  pl.BlockSpec((tb, LANES), lambda i, p: (i, 0)),
                  pl.BlockSpec((tb, d), lambda i, p: (i, 0)),
                  pl.BlockSpec((1, d), lambda i, p: (0, 0)),
                  pl.BlockSpec((1, d), lambda i, p: (0, 0))],
        out_specs=[pl.BlockSpec((tb, d), lambda i, p: (i, 0)),
                   pl.BlockSpec((tb, d), lambda i, p: (i, 0))],
        scratch_shapes=[pltpu.VMEM((ROW_DMA_SLOTS, TOP_K_EXPERTS, tb, d), F32),
                        pltpu.SemaphoreType.DMA((ROW_DMA_SLOTS,))],
    )
    assert tb * TOP_K_EXPERTS % ROW_DMA_UNROLL == 0
    return pl.pallas_call(
        functools.partial(_combine_body, tb=tb, alpha=alpha),
        grid_spec=grid_spec,
        out_shape=[jax.ShapeDtypeStruct((m, d), F32), jax.ShapeDtypeStruct((m, d), BF16)],
        compiler_params=_cp(("arbitrary",)),
        name="moe_combine_ln",
    )(pos, ys, tg, x, g, b)


def _moe_plan(top_i, n_real, n_exp, n_slots, n_items, super_rows, sub):
    a = top_i.shape[0] * TOP_K_EXPERTS
    e_flat = top_i.reshape(-1)
    tok = jnp.arange(a, dtype=I32) // TOP_K_EXPERTS
    real = tok < n_real
    onehot = jnp.logical_and(e_flat[:, None] == jnp.arange(n_exp, dtype=I32)[None, :], real[:, None]).astype(I32)
    csum = jnp.cumsum(onehot, axis=0)
    rank = jnp.take_along_axis(csum, e_flat[:, None], axis=1)[:, 0] - 1
    counts = csum[-1]
    padded = (counts + sub - 1) // sub * sub
    gstart = jnp.cumsum(padded) - padded
    pos = jnp.where(real, gstart[e_flat] + rank, jnp.arange(a, dtype=I32) % n_slots).astype(I32)
    src = (jnp.arange(n_slots, dtype=I32) % n_real).at[jnp.where(real, pos, n_slots)].set(tok, mode="drop")
    n_super = (padded + super_rows - 1) // super_rows
    iend = jnp.cumsum(n_super)
    istart = iend - n_super
    total = iend[-1]
    wi = jnp.arange(n_items, dtype=I32)
    valid = wi < total
    e_w = jnp.minimum(jnp.searchsorted(iend, wi, side="right"), n_exp - 1).astype(I32)
    last_e = e_w[jnp.maximum(total - 1, 0)]
    e_w = jnp.where(valid, e_w, last_e)
    k_w = wi - istart[e_w]
    subs_e = padded[e_w] // sub
    items_e = jnp.maximum(n_super[e_w], 1)
    base, extra = subs_e // items_e, subs_e % items_e
    nsub = jnp.where(valid, base + (k_w < extra).astype(I32), 0).astype(I32)
    first_sub = k_w * base + jnp.minimum(k_w, extra)
    z0 = jnp.sum(padded) + (wi - total) * super_rows
    nzero = jnp.where(valid, 0, jnp.clip((n_slots - z0) // sub, 0, super_rows // sub)).astype(I32)
    row0 = jnp.where(valid, gstart[e_w] + first_sub * sub, jnp.where(nzero > 0, z0, 0)).astype(I32)
    return pos, src, e_w, row0, nsub, valid.astype(I32), nzero


def _rel_bucket(dist):
    n = jnp.maximum(dist, 0)
    max_exact = REL_BUCKETS // 2
    nf = jnp.maximum(n, 1).astype(F32)
    large = max_exact + (jnp.log(nf / max_exact) / math.log(REL_MAX_DIST / max_exact)
                         * (REL_BUCKETS - max_exact)).astype(I32)
    large = jnp.minimum(large, REL_BUCKETS - 1)
    return jnp.where(n < max_exact, n, large)


def _shifted_bias(rel_bias, dist):
    t = rel_bias[_rel_bucket(dist)] - rel_bias[REL_BUCKETS - 1]
    t = jnp.where((dist >= 0)[..., None], t, 0.0)
    return jnp.moveaxis(t, -1, 0)


def _pick_tile(m_rows, target):
    return max(t for t in range(16, target + 1, 16) if m_rows % t == 0)


def _pad_rows(x, rows):
    return jnp.pad(x, ((0, rows - x.shape[0]), (0, 0)))


def kernel(x_prompt, x_sample, cache_k, cache_v, cache_ik, state_conv, page_table, w_in, conv_w, conv_b,
           conv_ln_g, conv_ln_b, sgu_ln_g, sgu_ln_b, sgu_w, sgu_b, w_pa, w_pb, w_pc, w_out, ln1_g, ln1_b,
           ln2_g, ln2_b, rel_bias, ffn_w1, ffn_w3, ffn_w2, moe_router, moe_router_b, moe_w1, moe_w3, moe_w2):
    batch, seq, d_model = x_prompt.shape
    db, n_tok, _ = x_sample.shape
    depth = w_in.shape[0]
    n_pool = cache_k.shape[1]
    n_exp = moe_router.shape[2]
    mp, ms = batch * seq, db * n_tok
    assert ms <= SAMPLE_TILE and mp % SAMPLE_TILE == 0
    m_all = mp + SAMPLE_TILE
    tm = m_all // N_ROW_TILES
    assert tm * N_ROW_TILES == m_all and tm % 16 == 0
    alpha = (2 * depth) ** 0.25
    tq = min(256, seq)

    xf = jnp.concatenate([x_prompt.reshape(mp, d_model), x_sample.reshape(ms, d_model),
                          jnp.zeros((m_all - mp - ms, d_model), F32)], axis=0)
    xb = xf.astype(BF16)
    w_in_t = jnp.swapaxes(w_in, 1, 2)
    ha_rows = ([W_Q_ROW0 + HA_TILE * t for t in range(QKV_W // HA_TILE)]
               + [W_GLU_ROW0 + HA_TILE * t for t in range(2 * CONV_DIM // HA_TILE)]
               + [W_UV_ROW0 + HA_TILE * t for t in range(2 * SGU_DIM // HA_TILE)]
               + [W_GATE_ROW0 + HA_TILE * t for t in range(N_BRANCH * d_model // HA_TILE)]
               + [W_IQ_ROW0 + HA_TILE * t for t in range(IQ_W // HA_TILE)])
    w_pa_b, w_pb_b, w_pc_b, w_out_b = (w.astype(BF16) for w in (w_pa, w_pb, w_pc, w_out))
    ffn_w2_b = ffn_w2.astype(BF16)

    kk = jnp.arange(2 * tq, dtype=I32)
    d_wrap = jnp.where(kk < tq, -kk, 2 * tq - kk)

    def toeplitz(offset):
        v = _shifted_bias(rel_bias, d_wrap + offset)
        return jnp.tile(v, (1, tq))[:, :tq * (2 * tq - 1)].reshape(N_HEADS, tq, 2 * tq - 1)[:, :, :tq]

    bias_tiles = jnp.stack([toeplitz(0), toeplitz(tq)])
    tok = jnp.arange(n_tok, dtype=I32)
    lane = jnp.arange(LANES, dtype=I32)
    d_last = PAGE_SIZE + tok[:, None] - lane[None, :]
    bias_last = _shifted_bias(rel_bias, d_last).reshape(N_HEADS * n_tok, LANES)
    bias_new = _shifted_bias(rel_bias, tok[:, None] - lane[None, :]).reshape(N_HEADS * n_tok, LANES)

    ck = cache_k.reshape(depth, n_pool, PAGE_SIZE * N_KV_HEADS, HEAD_DIM)
    cv = cache_v.reshape(depth, n_pool, PAGE_SIZE * N_KV_HEADS, HEAD_DIM)
    cik_t = jnp.swapaxes(cache_ik, 2, 3)
    n_pages = page_table.shape[1]
    idx_group = math.gcd(n_pages, IDX_PAGE_GROUP)
    attn_group = math.gcd(n_pages, ATTN_PAGE_GROUP)
    n_slots = _round_up(TOP_K_EXPERTS * (mp + ms) + n_exp * (MOE_SUB - 1), MOE_SUB)
    n_items = -(-n_slots // MOE_SUPER) + n_exp

    outs = {k: [] for k in ("k", "v", "ik", "conv_p", "conv_s", "sgu_s")}
    for l in range(depth):
        ha = _in_proj(xb, w_in_t, l, ha_rows, tm, HA_TILE)
        kv, ikw, k4, v4 = _kv_proj(xb, w_in_t, l, tm)

        a_p = _attn_prompt(ha, kv, ikw, bias_tiles, batch, seq, tq)
        conv_w_rep = jnp.broadcast_to(conv_w[l].reshape(CONV_WIDTH, CONV_CHUNKS, 1, LANES).transpose(1, 0, 2, 3),
                                      (CONV_CHUNKS, CONV_WIDTH, F32_SUBLANES, LANES))
        c_p, conv_state_p = _conv_prompt(ha, conv_w_rep, conv_b[l][None], conv_ln_g[l][None], conv_ln_b[l][None],
                                         batch, seq, min(256, seq))
        s_p = _sgu_prompt(ha, sgu_ln_g[l][None], sgu_ln_b[l][None], sgu_w[l], sgu_b[l].T, batch, seq,
                          min(512, seq))

        has, kvs, ikws = ha[mp:mp + ms], kv[mp:mp + ms], ikw[mp:mp + ms]
        iq_rows = has[:, HA_IQ:HA_IQ + IQ_W].reshape(db, n_tok * IDX_HEADS, IDX_DIM)
        iw_rows = (ikws[:, IKW_IW:IKW_IW + IDX_HEADS] * (IDX_DIM ** -0.5 * IDX_HEADS ** -0.5)
                   ).reshape(db, n_tok * IDX_HEADS, 1)
        iw_rows = jnp.broadcast_to(iw_rows, (db, n_tok * IDX_HEADS, LANES))
        scores = _idx_sample(page_table, iq_rows, iw_rows, cik_t, l, n_tok, idx_group)
        pad_rows = LANES - n_tok
        ik_new = jnp.pad(ikws[:, :IDX_DIM].reshape(db, n_tok, IDX_DIM), ((0, 0), (0, pad_rows), (0, 0)))
        scores_new = _idx_new(iq_rows, iw_rows, ik_new, n_tok)
        q_rows = has[:, HA_Q:HA_Q + QKV_W].reshape(db, n_tok, N_HEADS, HEAD_DIM).transpose(0, 2, 1, 3)
        q_rows = q_rows.reshape(db, N_HEADS * n_tok, HEAD_DIM)
        k_new = jnp.pad(kvs[:, KV_K:KV_K + KV_W].reshape(db, n_tok, KV_W), ((0, 0), (0, pad_rows), (0, 0)))
        v_new = jnp.pad(kvs[:, KV_V:KV_V + KV_W].reshape(db, n_tok, KV_W), ((0, 0), (0, pad_rows), (0, 0)))
        a_s = _attn_sample(page_table, scores, scores_new, q_rows, k_new, v_new, bias_last, bias_new,
                           ck, cv, l, n_tok, attn_group)
        a_s = a_s.reshape(db, N_HEADS, n_tok, HEAD_DIM).transpose(0, 2, 1, 3).reshape(ms, QKV_W)
        gd = SGU_DIM // SGU_GROUPS
        wv = jnp.repeat(sgu_w[l][:, :n_tok, :n_tok].transpose(1, 2, 0), gd, axis=-1)
        bv = jnp.repeat(sgu_b[l][:, :n_tok].T, gd, axis=-1)
        c_s, conv_state_s, s_s, vn_s = _mix_sample(
            has[:, HA_GLU:HA_GLU + 2 * CONV_DIM], has[:, HA_UV:HA_UV + 2 * SGU_DIM], state_conv[l],
            conv_w[l], conv_b[l][None], conv_ln_g[l][None], conv_ln_b[l][None],
            sgu_ln_g[l][None], sgu_ln_b[l][None], wv, bv, db, n_tok)

        sample_rows = tuple(_pad_rows(t.astype(BF16), SAMPLE_TILE) for t in (a_s, c_s, s_s))
        merged = _merge((a_p, c_p, s_p), sample_rows, ha, w_pa_b, w_pb_b, w_pc_b, l, SAMPLE_TILE)
        x1f, x1b = _mm_ln(merged, w_out_b, l, xf, ln1_g[l][None], ln1_b[l][None], alpha, _pick_tile(m_all, 384))

        j = l // 2
        if l % 2 == 0:
            h = _swiglu_up(x1b, ffn_w1, ffn_w3, j, tm, 512)
            xf, xb = _mm_ln(h, ffn_w2_b, j, x1f, ln2_g[l][None], ln2_b[l][None], alpha, _pick_tile(m_all, 384))
        else:
            w_r = jnp.pad(moe_router[j], ((0, 0), (0, LANES - n_exp)))
            b_r = jnp.pad(moe_router_b[j], (0, LANES - n_exp))[None]
            ti, tg = _router(x1f, w_r, b_r, n_exp, _pick_tile(m_all, 528))
            pos, src, it_e, it_r0, it_ns, it_valid, it_nz = _moe_plan(ti[:, :TOP_K_EXPERTS], mp + ms, n_exp,
                                                                      n_slots, n_items, MOE_SUPER, MOE_SUB)
            xs = _dispatch(src, x1f, n_slots, MOE_SUB)
            ys = _moe_ffn(it_e, it_r0, it_ns, it_valid, it_nz, xs, moe_w1, moe_w3, moe_w2, j, n_items,
                          MOE_SUPER, MOE_SUB, MOE_TF)
            tb = COMBINE_TB if m_all % COMBINE_TB == 0 else 16
            xf, xb = _combine_ln(pos, ys, tg, x1f, ln2_g[l][None], ln2_b[l][None], alpha, tb)

        outs["k"].append(k4)
        outs["v"].append(v4)
        outs["ik"].append(ikw[:, :IDX_DIM])
        outs["conv_p"].append(conv_state_p)
        outs["conv_s"].append(conv_state_s)
        outs["sgu_s"].append(vn_s.reshape(db, n_tok, SGU_DIM))

    def split(name, rows_per_token, width_shape):
        st = jnp.stack(outs[name])
        p = st[:, :mp * rows_per_token].reshape((depth, batch, seq) + width_shape)
        s = st[:, mp * rows_per_token:(mp + ms) * rows_per_token].reshape((depth, db, n_tok) + width_shape)
        return p, s

    k_p, k_s = split("k", N_KV_HEADS, (N_KV_HEADS, HEAD_DIM))
    v_p, v_s = split("v", N_KV_HEADS, (N_KV_HEADS, HEAD_DIM))
    ik_p, ik_s = split("ik", 1, (IDX_DIM,))
    y_prompt = xf[:mp].reshape(batch, seq, d_model)
    y_sample = xf[mp:mp + ms].reshape(db, n_tok, d_model)
    return (y_prompt, y_sample, k_p, v_p, ik_p, jnp.stack(outs["conv_p"]), k_s, v_s, ik_s,
            jnp.stack(outs["conv_s"]), jnp.stack(outs["sgu_s"]))
```

```python
import functools
import math

import jax
import jax.numpy as jnp
import numpy as np
from jax import lax
from jax.experimental import pallas as pl
from jax.experimental.pallas import tpu as pltpu

F32 = jnp.float32
BF16 = jnp.bfloat16
I32 = jnp.int32

N_HEADS = 16
HEAD_DIM = 128
N_KV_HEADS = 4
KV_REP = N_HEADS // N_KV_HEADS
IDX_HEADS = 16
IDX_DIM = 64
TOPK_MAX = 256
PAGE_SIZE = 128
CONV_DIM = 1024
CONV_WIDTH = 31
SGU_DIM = 1024
SGU_GROUPS = 8
CHUNK = 128
N_BRANCH = 3
TOP_K_EXPERTS = 2
REL_BUCKETS = 32
REL_MAX_DIST = 128
LN_EPS = 1e-5

V7X_VMEM_BYTES = 64 * 2**20
VMEM_LIMIT = V7X_VMEM_BYTES - 8 * 2**20
LANES = 128
F32_SUBLANES = 8
INT_MIN = -(2**31)
MASK_NEG = -1e30

QKV_W = N_HEADS * HEAD_DIM
KV_W = N_KV_HEADS * HEAD_DIM
IQ_W = IDX_HEADS * IDX_DIM
HA_Q, HA_GLU, HA_UV, HA_GATE, HA_IQ = 0, 2048, 4096, 6144, 12288
HA_TILE = 1024
W_Q_ROW0 = 0
W_KV_ROW0 = W_Q_ROW0 + QKV_W
W_IQ_ROW0 = W_KV_ROW0 + 2 * KV_W
W_IK_ROW0 = W_IQ_ROW0 + IQ_W
W_GLU_ROW0 = W_IK_ROW0 + IDX_DIM + IDX_HEADS
W_UV_ROW0 = W_GLU_ROW0 + 2 * CONV_DIM
W_GATE_ROW0 = W_UV_ROW0 + 2 * SGU_DIM
KV_K, KV_V = 0, KV_W
IKW_IW = IDX_DIM

N_ROW_TILES = 8
SAMPLE_TILE = 256
MOE_SUB = 256
MOE_SUPER = 2048
MOE_TF = 256
MOE_WIDE = 2
COMBINE_TB = 128
IDX_PAGE_GROUP = 32
ATTN_PAGE_GROUP = 32


def _cp(sem, vmem=VMEM_LIMIT):
    return pltpu.CompilerParams(dimension_semantics=sem, vmem_limit_bytes=vmem)


def _round_up(x, m):
    return (x + m - 1) // m * m


def _ln_rows(z, g, b):
    mu = jnp.mean(z, axis=-1, keepdims=True)
    d = z - mu
    var = jnp.mean(d * d, axis=-1, keepdims=True)
    return d * lax.rsqrt(var + LN_EPS) * g + b


def _sigmoid(x):
    return 1.0 / (1.0 + jnp.exp(-x))


def _in_proj_body(off_ref, x_ref, w_ref, o_ref, wt_ref):
    @pl.when(pl.program_id(1) == 0)
    def _():
        wt_ref[...] = w_ref[0].T.astype(BF16)

    o_ref[...] = jnp.dot(x_ref[...], wt_ref[...], preferred_element_type=F32).astype(o_ref.dtype)


def _in_proj(x, w_t, layer, row_offsets, tm, tn):
    m, k = x.shape
    n_tiles = len(row_offsets)
    grid_spec = pltpu.PrefetchScalarGridSpec(
        num_scalar_prefetch=1,
        grid=(n_tiles, m // tm),
        in_specs=[pl.BlockSpec((tm, k), lambda j, i, off: (i, 0)),
                  pl.BlockSpec((pl.Element(1), pl.Element(tn), pl.Element(k)),
                               lambda j, i, off: (layer, pl.multiple_of(off[j], F32_SUBLANES), 0))],
        out_specs=pl.BlockSpec((tm, tn), lambda j, i, off: (i, j)),
        scratch_shapes=[pltpu.VMEM((k, tn), BF16)],
    )
    return pl.pallas_call(
        _in_proj_body,
        grid_spec=grid_spec,
        out_shape=jax.ShapeDtypeStruct((m, n_tiles * tn), BF16),
        compiler_params=_cp(("arbitrary", "arbitrary")),
        name="in_proj",
    )(jnp.asarray(row_offsets, I32), x, w_t)


def _kv_proj_body(x_ref, wkv_ref, wik_ref, kv_ref, ik_ref, k4_ref, v4_ref, wkv_t, wik_t):
    @pl.when(pl.program_id(0) == 0)
    def _():
        wkv_t[...] = wkv_ref[...].T.astype(BF16)
        wik_t[...] = wik_ref[...].T.astype(BF16)

    x = x_ref[...]
    tm = x.shape[0]
    kv = jnp.dot(x, wkv_t[...], preferred_element_type=F32)
    kv_ref[...] = kv
    ik_ref[...] = jnp.dot(x, wik_t[...], preferred_element_type=F32)
    for g in range(N_KV_HEADS):
        k4_ref[pl.ds(g, tm, stride=N_KV_HEADS), :] = kv[:, KV_K + g * HEAD_DIM:KV_K + (g + 1) * HEAD_DIM]
        v4_ref[pl.ds(g, tm, stride=N_KV_HEADS), :] = kv[:, KV_V + g * HEAD_DIM:KV_V + (g + 1) * HEAD_DIM]


def _kv_proj(x, w_t, layer, tm):
    m, k = x.shape
    assert W_KV_ROW0 % (2 * KV_W) == 0 and W_IK_ROW0 % LANES == 0
    row = lambda i: (i, 0)
    per_head = jax.ShapeDtypeStruct((m * N_KV_HEADS, HEAD_DIM), F32)
    return pl.pallas_call(
        _kv_proj_body,
        grid=(m // tm,),
        in_specs=[pl.BlockSpec((tm, k), row),
                  pl.BlockSpec((None, 2 * KV_W, k), lambda i: (layer, W_KV_ROW0 // (2 * KV_W), 0)),
                  pl.BlockSpec((None, LANES, k), lambda i: (layer, W_IK_ROW0 // LANES, 0))],
        out_specs=[pl.BlockSpec((tm, 2 * KV_W), row), pl.BlockSpec((tm, LANES), row),
                   pl.BlockSpec((tm * N_KV_HEADS, HEAD_DIM), row), pl.BlockSpec((tm * N_KV_HEADS, HEAD_DIM), row)],
        out_shape=[jax.ShapeDtypeStruct((m, 2 * KV_W), F32), jax.ShapeDtypeStruct((m, LANES), F32),
                   per_head, per_head],
        scratch_shapes=[pltpu.VMEM((k, 2 * KV_W), BF16), pltpu.VMEM((k, LANES), BF16)],
        compiler_params=_cp(("arbitrary",)),
        name="kv_proj",
    )(x, w_t, w_t)


def _row_halves(rows):
    half = rows // 2 // 16 * 16
    return (slice(0, half), slice(half, rows)) if half > 0 else (slice(0, rows),)


def _mm_ln_body(x_ref, w_ref, r_ref, g_ref, b_ref, of_ref, ob_ref, *, alpha):
    for rs in _row_halves(x_ref.shape[0]):
        z = jnp.dot(x_ref[rs, :], w_ref[...], preferred_element_type=F32) + alpha * r_ref[rs, :]
        y = _ln_rows(z, g_ref[...], b_ref[...])
        of_ref[rs, :] = y
        ob_ref[rs, :] = y.astype(BF16)


def _mm_ln(x, w, layer, resid, g, b, alpha, tm):
    m, k = x.shape
    n = w.shape[2]
    row = lambda i: (i, 0)
    return pl.pallas_call(
        functools.partial(_mm_ln_body, alpha=alpha),
        grid=(m // tm,),
        in_specs=[pl.BlockSpec((tm, k), row),
                  pl.BlockSpec((None, k, n), lambda i: (layer, 0, 0), pipeline_mode=pl.Buffered(1)),
                  pl.BlockSpec((tm, n), row),
                  pl.BlockSpec((1, n), lambda i: (0, 0)),
                  pl.BlockSpec((1, n), lambda i: (0, 0))],
        out_specs=[pl.BlockSpec((tm, n), row), pl.BlockSpec((tm, n), row)],
        out_shape=[jax.ShapeDtypeStruct((m, n), F32), jax.ShapeDtypeStruct((m, n), BF16)],
        compiler_params=_cp(("arbitrary",)),
        name="proj_ln",
    )(x, w, resid, g, b)


def _swiglu_up_body(x_ref, w1_ref, w3_ref, o_ref, w1b, w3b):
    @pl.when(pl.program_id(1) == 0)
    def _():
        w1b[...] = w1_ref[...].astype(BF16)
        w3b[...] = w3_ref[...].astype(BF16)

    for rs in _row_halves(x_ref.shape[0]):
        x = x_ref[rs, :]
        a = jnp.dot(x, w1b[...], preferred_element_type=F32)
        c = jnp.dot(x, w3b[...], preferred_element_type=F32)
        o_ref[rs, :] = (a * _sigmoid(a) * c).astype(o_ref.dtype)


def _swiglu_up(x, w1, w3, layer, tm, tf):
    m, k = x.shape
    f = w1.shape[2]
    return pl.pallas_call(
        _swiglu_up_body,
        grid=(f // tf, m // tm),
        in_specs=[pl.BlockSpec((tm, k), lambda j, i: (i, 0)),
                  pl.BlockSpec((None, k, tf), lambda j, i: (layer, 0, j)),
                  pl.BlockSpec((None, k, tf), lambda j, i: (layer, 0, j))],
        out_specs=pl.BlockSpec((tm, tf), lambda j, i: (i, j)),
        out_shape=jax.ShapeDtypeStruct((m, f), BF16),
        scratch_shapes=[pltpu.VMEM((k, tf), BF16), pltpu.VMEM((k, tf), BF16)],
        compiler_params=_cp(("arbitrary", "arbitrary")),
        name="ffn_up",
    )(x, w1, w3)


def _merge_body(ap_ref, cp_ref, sp_ref, as_ref, cs_ref, ss_ref, g_ref, wa_ref, wb_ref, wc_ref, o_ref, *, n_prompt):
    i = pl.program_id(0)
    d = o_ref.shape[1]

    def merge(a_ref, c_ref, s_ref):
        for rs in _row_halves(o_ref.shape[0]):
            pa = jnp.dot(a_ref[rs, :], wa_ref[...], preferred_element_type=F32)
            pb = jnp.dot(c_ref[rs, :], wb_ref[...], preferred_element_type=F32)
            pc = jnp.dot(s_ref[rs, :], wc_ref[...], preferred_element_type=F32)
            o = (_sigmoid(g_ref[rs, :d].astype(F32)) * pa + _sigmoid(g_ref[rs, d:2 * d].astype(F32)) * pb
                 + _sigmoid(g_ref[rs, 2 * d:].astype(F32)) * pc)
            o_ref[rs, :] = o.astype(o_ref.dtype)

    @pl.when(i < n_prompt)
    def _():
        merge(ap_ref, cp_ref, sp_ref)

    @pl.when(i >= n_prompt)
    def _():
        merge(as_ref, cs_ref, ss_ref)


def _merge(prompt, sample, ha, w_pa, w_pb, w_pc, layer, tm):
    m = ha.shape[0]
    d = w_pa.shape[2]
    n_prompt = prompt[0].shape[0] // tm
    assert prompt[0].shape[0] % tm == 0 and sample[0].shape[0] == tm and m == (n_prompt + 1) * tm
    assert HA_GATE % (N_BRANCH * d) == 0
    p_row = lambda i: (jnp.minimum(i, n_prompt - 1), 0)
    fixed = lambda i: (0, 0)
    return pl.pallas_call(
        functools.partial(_merge_body, n_prompt=n_prompt),
        grid=(n_prompt + 1,),
        in_specs=[pl.BlockSpec((tm, t.shape[1]), p_row) for t in prompt]
        + [pl.BlockSpec((tm, t.shape[1]), fixed) for t in sample]
        + [pl.BlockSpec((tm, N_BRANCH * d), lambda i: (i, HA_GATE // (N_BRANCH * d)))]
        + [pl.BlockSpec((None,) + w.shape[1:], lambda i: (layer, 0, 0), pipeline_mode=pl.Buffered(1))
           for w in (w_pa, w_pb, w_pc)],
        out_specs=pl.BlockSpec((tm, d), lambda i: (i, 0)),
        out_shape=jax.ShapeDtypeStruct((m, d), BF16),
        compiler_params=_cp(("arbitrary",)),
        name="branch_merge",
    )(*prompt, *sample, ha, w_pa, w_pb, w_pc)


def _sortable_key(score):
    bits = pltpu.bitcast(score, I32)
    return bits ^ ((bits >> 31) & jnp.int32(0x7FFFFFFF))


def _kth_largest_key(count_ge, rows, k):
    def step(it, t):
        cand = t + jnp.left_shift(jnp.int32(1), 31 - it)
        return jnp.where(count_ge(cand) >= k, cand, t)

    return lax.fori_loop(0, 32, step, jnp.full((rows, 1), INT_MIN, I32))


RADIX4_STEPS = 16
RADIX4_FIELD_BITS = 5
RADIX4_WIDE = 4096.0


def _kth_largest_key_radix4(count3, rows, k, n_valid):
    kf = float(k)

    def cond(state):
        it, _, _, pending = state
        return jnp.logical_and(it < RADIX4_STEPS, pending > 0.0)

    def body(state):
        it, t, cnt, _ = state
        one = jnp.left_shift(jnp.int32(1), 30 - 2 * it)
        c1 = t + one
        c2 = c1 + one
        c3 = c2 + one
        n1, n2, n3 = count3(c1, c2, c3)
        t = jnp.where(n3 >= kf, c3, jnp.where(n2 >= kf, c2, jnp.where(n1 >= kf, c1, t)))
        cnt = jnp.where(n3 >= kf, n3, jnp.where(n2 >= kf, n2, jnp.where(n1 >= kf, n1, cnt)))
        settled = jnp.logical_or(cnt == kf, n_valid <= kf)
        return it + 1, t, cnt, jnp.max(jnp.where(settled, 0.0, 1.0))

    init = (jnp.int32(0), jnp.full((rows, 1), INT_MIN, I32), jnp.full((rows, 1), -1.0, F32), jnp.float32(1.0))
    return lax.while_loop(cond, body, init)[1]


def _lane_tile(x, width):
    return x if width == LANES else jnp.concatenate([x] * (width // LANES), axis=1)


def _attn_prompt_body(q_ref, iq_ref, iw_ref, kv_ref, ik_ref, bias_ref, o_ref,
                      keys_ref, mb_ref, iwb_ref, m_ref, acc_ref, *, tq, k_sel):
    i = pl.program_id(1)
    nkb = i + 1
    tk = tq
    row = lax.broadcasted_iota(I32, (tq, tk), 0)
    col = lax.broadcasted_iota(I32, (tq, tk), 1)
    low_half = lax.broadcasted_iota(I32, (tk, LANES), 1) < IDX_DIM
    iw = iw_ref[...] * (IDX_DIM ** -0.5 * IDX_HEADS ** -0.5)
    for h in range(IDX_HEADS):
        iwb_ref[h] = jnp.broadcast_to(iw[:, IKW_IW + h:IKW_IW + h + 1], (tq, LANES))
    iq_pairs = [iq_ref[:, p * LANES:(p + 1) * LANES] for p in range(IDX_HEADS // 2)]

    def idx_block(kb, carry):
        ik_lo = jnp.where(low_half, ik_ref[pl.ds(pl.multiple_of(kb * tk, tk), tk), :], 0.0)
        ik_sides = (ik_lo.astype(BF16), pltpu.roll(ik_lo, IDX_DIM, axis=1).astype(BF16))
        acc = jnp.zeros((tq, tk), F32)
        for h in range(IDX_HEADS):
            s = lax.dot_general(iq_pairs[h // 2], ik_sides[h % 2], (((1,), (1,)), ((), ())),
                                preferred_element_type=F32)
            acc = acc + jnp.maximum(s, 0.0) * _lane_tile(iwb_ref[h], tk)
        causal = (col + kb * tk) <= (row + i * tq)
        keys_ref[kb] = jnp.where(causal, _sortable_key(acc), jnp.int32(INT_MIN))
        return carry

    def idx_pair(t, carry):
        idx_block(2 * t, carry)
        return idx_block(2 * t + 1, carry)

    lax.fori_loop(0, nkb // 2, idx_pair, 0)

    @pl.when(nkb % 2 == 1)
    def _():
        idx_block(nkb - 1, 0)
        keys_ref[nkb] = jnp.full((tq, tk), INT_MIN, I32)

    fb = RADIX4_FIELD_BITS
    field_mask = (1 << fb) - 1
    inc1, inc2, inc3 = 1, 1 + (1 << fb), 1 + (1 << fb) + (1 << 2 * fb)

    def count3(c1, c2, c3):
        b1, b2, b3 = (jnp.broadcast_to(c, (tq, LANES)) for c in (c1, c2, c3))

        def cb(pair, acc):
            for kb in (2 * pair, 2 * pair + 1):
                keys = keys_ref[kb]
                for t in range(tk // LANES):
                    kk = keys[:, t * LANES:(t + 1) * LANES]
                    acc = acc + jnp.where(kk >= b3, inc3, jnp.where(kk >= b2, inc2, jnp.where(kk >= b1, inc1, 0)))
            return acc

        acc = lax.fori_loop(0, (nkb + 1) // 2, cb, jnp.zeros((tq, LANES), I32))
        n3 = (acc >> (2 * fb)).astype(F32).sum(axis=-1, keepdims=True)
        low = ((acc & field_mask).astype(F32) + ((acc >> fb) & field_mask).astype(F32) * RADIX4_WIDE)
        low = low.sum(axis=-1, keepdims=True)
        n2 = jnp.floor(low * (1.0 / RADIX4_WIDE))
        return low - RADIX4_WIDE * n2, n2, n3

    n_valid = (lax.broadcasted_iota(I32, (tq, 1), 0) + (i * tq + 1)).astype(F32)
    thr = _kth_largest_key_radix4(count3, tq, k_sel, n_valid)
    thr = jnp.maximum(thr, jnp.int32(INT_MIN + 1))
    thr_b = _lane_tile(jnp.broadcast_to(thr, (tq, LANES)), tk)

    def mask_block(kb, carry):
        mb_ref[kb] = jnp.where(keys_ref[kb] >= thr_b, 0.0, MASK_NEG)
        return carry

    lax.fori_loop(0, nkb, mask_block, 0)

    scale = HEAD_DIM ** -0.5
    for g in range(N_KV_HEADS):
        qg = jnp.concatenate(
            [q_ref[:, (KV_REP * g + r) * HEAD_DIM:(KV_REP * g + r + 1) * HEAD_DIM] for r in range(KV_REP)], axis=0)
        m_ref[...] = jnp.full(m_ref.shape, MASK_NEG, F32)
        acc_ref[...] = jnp.zeros(acc_ref.shape, F32)

        def block(kb, bias_idx, g=g, qg=qg):
            r0 = pl.multiple_of(kb * tk, tk)
            kblk = kv_ref[pl.ds(r0, tk), KV_K + g * HEAD_DIM:KV_K + (g + 1) * HEAD_DIM].astype(BF16)
            vblk = kv_ref[pl.ds(r0, tk), KV_V + g * HEAD_DIM:KV_V + (g + 1) * HEAD_DIM].astype(BF16)
            s = lax.dot_general(qg, kblk, (((1,), (1,)), ((), ())), preferred_element_type=F32) * scale
            s = s.reshape(KV_REP, tq, tk) + mb_ref[kb][None]
            if bias_idx is not None:
                s = s + bias_ref[bias_idx, KV_REP * g:KV_REP * (g + 1)]
            s = s.reshape(KV_REP * tq, tk)
            m_old = m_ref[...]
            m_new = jnp.maximum(m_old, s.max(axis=-1, keepdims=True))
            p = jnp.exp(s - _lane_tile(m_new, tk))
            alpha = jnp.exp(m_old - m_new)
            v_ones = jnp.concatenate([vblk, jnp.ones((tk, LANES), BF16)], axis=1)
            acc_ref[...] = (_lane_tile(alpha, HEAD_DIM + LANES) * acc_ref[...]
                            + jnp.dot(p.astype(BF16), v_ones, preferred_element_type=F32))
            m_ref[...] = m_new

        n_far = jnp.maximum(i - 1, 0)

        def far_pair(t, carry):
            block(2 * t, None)
            block(2 * t + 1, None)
            return carry

        lax.fori_loop(0, n_far // 2, far_pair, 0)

        @pl.when(n_far % 2 == 1)
        def _():
            block(n_far - 1, None)

        @pl.when(i >= 1)
        def _():
            block(i - 1, 1)
            block(i, 0)

        @pl.when(i == 0)
        def _():
            block(i, 0)
        out = acc_ref[:, :HEAD_DIM] / acc_ref[:, HEAD_DIM:]
        for r in range(KV_REP):
            h = KV_REP * g + r
            o_ref[:, h * HEAD_DIM:(h + 1) * HEAD_DIM] = out[r * tq:(r + 1) * tq].astype(o_ref.dtype)


def _attn_prompt(ha, kv, ikw, bias_tiles, batch, seq, tq):
    nq = seq // tq
    k_sel = min(TOPK_MAX, seq // 4)
    assert seq // LANES < (1 << RADIX4_FIELD_BITS)
    body = functools.partial(_attn_prompt_body, tq=tq, k_sel=k_sel)
    return pl.pallas_call(
        body,
        grid=(batch, nq),
        in_specs=[pl.BlockSpec((tq, QKV_W), lambda b, i: (b * nq + i, HA_Q // QKV_W)),
                  pl.BlockSpec((tq, IQ_W), lambda b, i: (b * nq + i, HA_IQ // IQ_W)),
                  pl.BlockSpec((tq, LANES), lambda b, i: (b * nq + i, 0)),
                  pl.BlockSpec((seq, 2 * KV_W), lambda b, i: (b, 0), pipeline_mode=pl.Buffered(1)),
                  pl.BlockSpec((seq, LANES), lambda b, i: (b, 0), pipeline_mode=pl.Buffered(1)),
                  pl.BlockSpec((2, N_HEADS, tq, tq), lambda b, i: (0, 0, 0, 0), pipeline_mode=pl.Buffered(1))],
        out_specs=pl.BlockSpec((tq, QKV_W), lambda b, i: (b * nq + i, 0)),
        out_shape=jax.ShapeDtypeStruct((batch * seq, QKV_W), BF16),
        scratch_shapes=[pltpu.VMEM((nq + nq % 2, tq, tq), I32),
                        pltpu.VMEM((nq, tq, tq), F32),
                        pltpu.VMEM((IDX_HEADS, tq, LANES), F32),
                        pltpu.VMEM((KV_REP * tq, LANES), F32),
                        pltpu.VMEM((KV_REP * tq, HEAD_DIM + LANES), F32)],
        compiler_params=_cp(("arbitrary", "arbitrary")),
        name="attn_prompt",
    )(ha, ha, ikw, kv, ikw, bias_tiles)


def _page_spec(block, layer, group, slot):
    return pl.BlockSpec((None, None) + block, lambda b, p, pt: (layer, pt[b, p * group + slot], 0, 0))


def _idx_sample_body(pt_ref, iq_ref, iw_ref, *refs, n_tok, group):
    pages, o_ref = refs[:group], refs[group]
    ikt = jnp.concatenate([r[...].astype(BF16) for r in pages], axis=1)
    s = jnp.dot(iq_ref[0], ikt, preferred_element_type=F32)
    r = jnp.maximum(s, 0.0) * _lane_tile(iw_ref[0], group * PAGE_SIZE)
    tok_scores = r.reshape(n_tok, IDX_HEADS, group * PAGE_SIZE).sum(axis=1)
    for t in range(group):
        o_ref[0, t] = tok_scores[:, t * PAGE_SIZE:(t + 1) * PAGE_SIZE]


def _idx_sample(page_table, iq_rows, iw_rows, cache_ik_t, layer, n_tok, group):
    db, n_pages = page_table.shape
    rows = n_tok * IDX_HEADS
    body = functools.partial(_idx_sample_body, n_tok=n_tok, group=group)
    grid_spec = pltpu.PrefetchScalarGridSpec(
        num_scalar_prefetch=1,
        grid=(db, n_pages // group),
        in_specs=[pl.BlockSpec((1, rows, IDX_DIM), lambda b, p, pt: (b, 0, 0)),
                  pl.BlockSpec((1, rows, LANES), lambda b, p, pt: (b, 0, 0))]
        + [_page_spec((IDX_DIM, PAGE_SIZE), layer, group, t) for t in range(group)],
        out_specs=pl.BlockSpec((1, group, n_tok, PAGE_SIZE), lambda b, p, pt: (b, p, 0, 0)),
    )
    return pl.pallas_call(
        body,
        grid_spec=grid_spec,
        out_shape=jax.ShapeDtypeStruct((db, n_pages, n_tok, PAGE_SIZE), F32),
        compiler_params=_cp(("arbitrary", "arbitrary")),
        name="idx_sample",
    )(page_table, iq_rows, iw_rows, *([cache_ik_t] * group))


def _attn_sample_body(pt_ref, sc_ref, scn_ref, q_ref, kn_ref, vn_ref, bl_ref, bn_ref, *refs,
                      n_tok, n_pages, k_sel, group):
    kp_refs, vp_refs = refs[:group], refs[group:2 * group]
    o_ref, thr_ref, m_ref, l_ref, acc_ref = refs[2 * group:]
    p = pl.program_id(1)
    n_steps = n_pages // group
    grp = KV_REP * n_tok
    width = group * PAGE_SIZE
    scale = HEAD_DIM ** -0.5
    r_i = lax.broadcasted_iota(I32, (n_tok, LANES), 0)
    c_i = lax.broadcasted_iota(I32, (n_tok, LANES), 1)
    keys_new = jnp.where(c_i <= r_i, _sortable_key(scn_ref[0]), jnp.int32(INT_MIN))

    @pl.when(p == 0)
    def _():
        keys = _sortable_key(sc_ref[0])

        def count_ge(cand):
            cand_b = jnp.broadcast_to(cand, (n_tok, LANES))
            c = jnp.where(keys >= cand_b[None], 1.0, 0.0).sum(axis=0) + jnp.where(keys_new >= cand_b, 1.0, 0.0)
            return c.sum(axis=-1, keepdims=True)

        thr = _kth_largest_key(count_ge, n_tok, float(k_sel))
        thr_ref[...] = jnp.broadcast_to(jnp.maximum(thr, jnp.int32(INT_MIN + 1)), (n_tok, LANES))
        m_ref[...] = jnp.full(m_ref.shape, MASK_NEG, F32)
        l_ref[...] = jnp.zeros(l_ref.shape, F32)
        acc_ref[...] = jnp.zeros(acc_ref.shape, F32)

    thr_b = thr_ref[...]

    def update(g, s, vblk):
        sl = slice(g * grp, (g + 1) * grp)
        m_old = m_ref[sl, :]
        m_new = jnp.maximum(m_old, s.max(axis=-1, keepdims=True))
        pr = jnp.exp(s - _lane_tile(m_new, s.shape[1]))
        alpha = jnp.exp(m_old - m_new)
        l_ref[sl, :] = alpha * l_ref[sl, :] + pr.sum(axis=-1, keepdims=True)
        acc_ref[sl, :] = alpha * acc_ref[sl, :] + jnp.dot(pr.astype(BF16), vblk, preferred_element_type=F32)
        m_ref[sl, :] = m_new

    def group_rows(mask_tok):
        return jnp.concatenate([mask_tok] * KV_REP, axis=0)

    base = p * group
    mb = jnp.concatenate(
        [jnp.where(_sortable_key(sc_ref[0, base + t]) >= thr_b, 0.0, MASK_NEG) for t in range(group)], axis=1)
    mb_g = group_rows(mb)
    is_last = p == n_steps - 1
    zeros_head = jnp.zeros((grp, width - PAGE_SIZE), F32)
    for g in range(N_KV_HEADS):
        qg = q_ref[0, g * grp:(g + 1) * grp, :]
        kg = jnp.concatenate([r[pl.ds(g, PAGE_SIZE, stride=N_KV_HEADS), :].astype(BF16) for r in kp_refs], axis=0)
        vg = jnp.concatenate([r[pl.ds(g, PAGE_SIZE, stride=N_KV_HEADS), :].astype(BF16) for r in vp_refs], axis=0)
        s = lax.dot_general(qg, kg, (((1,), (1,)), ((), ())), preferred_element_type=F32) * scale
        near = jnp.where(is_last, bl_ref[g * grp:(g + 1) * grp, :], 0.0)
        s = s + mb_g + (near if group == 1 else jnp.concatenate([zeros_head, near], axis=1))
        update(g, s, vg)

    @pl.when(is_last)
    def _():
        mbn_g = group_rows(jnp.where(keys_new >= thr_b, 0.0, MASK_NEG))
        for g in range(N_KV_HEADS):
            qg = q_ref[0, g * grp:(g + 1) * grp, :]
            kblk = kn_ref[0, :, g * HEAD_DIM:(g + 1) * HEAD_DIM].astype(BF16)
            vblk = vn_ref[0, :, g * HEAD_DIM:(g + 1) * HEAD_DIM].astype(BF16)
            s = lax.dot_general(qg, kblk, (((1,), (1,)), ((), ())), preferred_element_type=F32) * scale
            s = s + mbn_g + bn_ref[g * grp:(g + 1) * grp, :]
            update(g, s, vblk)
        o_ref[0] = acc_ref[...] / l_ref[...]


def _attn_sample(page_table, scores, scores_new, q_rows, k_new, v_new, bias_last, bias_new,
                 cache_k, cache_v, layer, n_tok, group):
    db, n_pages = page_table.shape
    past = n_pages * PAGE_SIZE
    rows = N_HEADS * n_tok
    k_sel = min(TOPK_MAX, (past + n_tok) // 4)
    body = functools.partial(_attn_sample_body, n_tok=n_tok, n_pages=n_pages, k_sel=k_sel, group=group)
    page = (PAGE_SIZE * N_KV_HEADS, HEAD_DIM)
    grid_spec = pltpu.PrefetchScalarGridSpec(
        num_scalar_prefetch=1,
        grid=(db, n_pages // group),
        in_specs=[pl.BlockSpec((1, n_pages, n_tok, PAGE_SIZE), lambda b, p, pt: (b, 0, 0, 0)),
                  pl.BlockSpec((1, n_tok, LANES), lambda b, p, pt: (b, 0, 0)),
                  pl.BlockSpec((1, rows, HEAD_DIM), lambda b, p, pt: (b, 0, 0)),
                  pl.BlockSpec((1, LANES, KV_W), lambda b, p, pt: (b, 0, 0)),
                  pl.BlockSpec((1, LANES, KV_W), lambda b, p, pt: (b, 0, 0)),
                  pl.BlockSpec((rows, LANES), lambda b, p, pt: (0, 0)),
                  pl.BlockSpec((rows, LANES), lambda b, p, pt: (0, 0))]
        + [_page_spec(page, layer, group, t) for t in range(group)]
        + [_page_spec(page, layer, group, t) for t in range(group)],
        out_specs=pl.BlockSpec((1, rows, HEAD_DIM), lambda b, p, pt: (b, 0, 0)),
        scratch_shapes=[pltpu.VMEM((n_tok, LANES), I32),
                        pltpu.VMEM((rows, LANES), F32),
                        pltpu.VMEM((rows, LANES), F32),
                        pltpu.VMEM((rows, HEAD_DIM), F32)],
    )
    return pl.pallas_call(
        body,
        grid_spec=grid_spec,
        out_shape=jax.ShapeDtypeStruct((db, rows, HEAD_DIM), F32),
        compiler_params=_cp(("arbitrary", "arbitrary")),
        name="attn_sample",
    )(page_table, scores, scores_new, q_rows, k_new, v_new, bias_last, bias_new,
      *([cache_k] * group), *([cache_v] * group))


def _idx_new_body(iq_ref, iw_ref, ik_ref, o_ref, *, n_tok):
    for b in range(iq_ref.shape[0]):
        s = lax.dot_general(iq_ref[b], ik_ref[b].astype(BF16), (((1,), (1,)), ((), ())),
                            preferred_element_type=F32)
        r = jnp.maximum(s, 0.0) * iw_ref[b]
        o_ref[b] = r.reshape(n_tok, IDX_HEADS, LANES).sum(axis=1)


def _idx_new(iq_rows, iw_rows, ik_new, n_tok):
    db = iq_rows.shape[0]
    return pl.pallas_call(
        functools.partial(_idx_new_body, n_tok=n_tok),
        out_shape=jax.ShapeDtypeStruct((db, n_tok, LANES), F32),
        name="idx_new",
    )(iq_rows, iw_rows, ik_new)


HALO = CONV_WIDTH - 1
HALO_PAD = 32
CONV_ROW_BLOCK = 128


CONV_CHUNKS = CONV_DIM // LANES


def _conv_prompt_body(glu_ref, cw_ref, cb_ref, g_ref, b_ref, o_ref, st_ref, hp_ref, c_ref, sh_ref, *, ts, ns):
    s_idx = pl.program_id(1)

    @pl.when(s_idx == 0)
    def _():
        hp_ref[:, 0:HALO_PAD, :] = jnp.zeros((CONV_CHUNKS, HALO_PAD, LANES), F32)

    for c in range(CONV_CHUNKS):
        cs = slice(c * LANES, (c + 1) * LANES)
        a = glu_ref[:, cs].astype(F32)
        gt = glu_ref[:, CONV_DIM + c * LANES:CONV_DIM + (c + 1) * LANES].astype(F32)
        hp_ref[c, HALO_PAD:HALO_PAD + ts, :] = a * _sigmoid(gt)
    off = HALO_PAD - HALO
    rows = min(ts, CONV_ROW_BLOCK)

    def chunk(c, carry):
        for r0 in range(0, ts, rows):
            acc = jnp.zeros((rows // F32_SUBLANES, F32_SUBLANES, LANES), F32)
            for shift in range(F32_SUBLANES):
                taps = range(shift, CONV_WIDTH, F32_SUBLANES)
                span = rows + taps[-1] - shift
                sh_ref[0:span, :] = hp_ref[c, off + r0 + shift:off + r0 + shift + span, :]
                for w in taps:
                    x = sh_ref[w - shift:w - shift + rows, :].reshape(rows // F32_SUBLANES, F32_SUBLANES, LANES)
                    acc = acc + x * cw_ref[c, w][None]
            c_ref[c, r0:r0 + rows, :] = acc.reshape(rows, LANES)
        return carry

    lax.fori_loop(0, CONV_CHUNKS, chunk, 0)

    conv = [c_ref[c] + cb_ref[:, c * LANES:(c + 1) * LANES] for c in range(CONV_CHUNKS)]
    mu = sum(v.sum(axis=-1, keepdims=True) for v in conv) / CONV_DIM
    var = sum(((v - mu) * (v - mu)).sum(axis=-1, keepdims=True) for v in conv) / CONV_DIM
    inv = lax.rsqrt(var + LN_EPS)
    for c in range(CONV_CHUNKS):
        cs = slice(c * LANES, (c + 1) * LANES)
        y = (conv[c] - mu) * inv * g_ref[:, cs] + b_ref[:, cs]
        o_ref[:, cs] = (y * _sigmoid(y)).astype(o_ref.dtype)

    @pl.when(s_idx == ns - 1)
    def _():
        for c in range(CONV_CHUNKS):
            st_ref[0, :, c * LANES:(c + 1) * LANES] = hp_ref[c, ts + off:ts + HALO_PAD, :]

    hp_ref[:, off:HALO_PAD, :] = hp_ref[:, ts + off:ts + HALO_PAD, :]


def _conv_prompt(ha, conv_w, conv_b, ln_g, ln_b, batch, seq, ts):
    ns = seq // ts
    body = functools.partial(_conv_prompt_body, ts=ts, ns=ns)
    vec = pl.BlockSpec((1, CONV_DIM), lambda b, s: (0, 0))
    return pl.pallas_call(
        body,
        grid=(batch, ns),
        in_specs=[pl.BlockSpec((ts, 2 * CONV_DIM), lambda b, s: (b * ns + s, HA_GLU // (2 * CONV_DIM))),
                  pl.BlockSpec((CONV_CHUNKS, CONV_WIDTH, F32_SUBLANES, LANES), lambda b, s: (0, 0, 0, 0)),
                  vec, vec, vec],
        out_specs=[pl.BlockSpec((ts, CONV_DIM), lambda b, s: (b * ns + s, 0)),
                   pl.BlockSpec((1, HALO, CONV_DIM), lambda b, s: (b, 0, 0))],
        out_shape=[jax.ShapeDtypeStruct((batch * seq, CONV_DIM), BF16),
                   jax.ShapeDtypeStruct((batch, HALO, CONV_DIM), F32)],
        scratch_shapes=[pltpu.VMEM((CONV_CHUNKS, HALO_PAD + ts, LANES), F32),
                        pltpu.VMEM((CONV_CHUNKS, ts, LANES), F32),
                        pltpu.VMEM((HALO_PAD + min(ts, CONV_ROW_BLOCK), LANES), F32)],
        compiler_params=_cp(("arbitrary", "arbitrary")),
        name="conv_prompt",
    )(ha, conv_w, conv_b, ln_g, ln_b)


def _sgu_prompt_body(uv_ref, g_ref, b_ref, w_ref, bs_ref, o_ref, *, ts):
    u = uv_ref[:, :SGU_DIM].astype(F32)
    vn = _ln_rows(uv_ref[:, SGU_DIM:].astype(F32), g_ref[...], b_ref[...]).astype(BF16)
    r_i = lax.broadcasted_iota(I32, (CHUNK, CHUNK), 0)
    c_i = lax.broadcasted_iota(I32, (CHUNK, CHUNK), 1)
    gd = SGU_DIM // SGU_GROUPS
    for g in range(SGU_GROUPS):
        wm = jnp.where(c_i <= r_i, w_ref[g], 0.0).astype(BF16)
        bias = bs_ref[:, g:g + 1]
        for c in range(ts // CHUNK):
            rs = slice(c * CHUNK, (c + 1) * CHUNK)
            gs = slice(g * gd, (g + 1) * gd)
            mixed = jnp.dot(wm, vn[rs, gs], preferred_element_type=F32) + bias
            o_ref[rs, gs] = (u[rs, gs] * mixed).astype(o_ref.dtype)


def _sgu_prompt(ha, ln_g, ln_b, w_s, b_s_t, batch, seq, ts):
    ns = seq // ts
    vec = pl.BlockSpec((1, SGU_DIM), lambda b, s: (0, 0))
    return pl.pallas_call(
        functools.partial(_sgu_prompt_body, ts=ts),
        grid=(batch, ns),
        in_specs=[pl.BlockSpec((ts, 2 * SGU_DIM), lambda b, s: (b * ns + s, HA_UV // (2 * SGU_DIM))),
                  vec, vec,
                  pl.BlockSpec((SGU_GROUPS, CHUNK, CHUNK), lambda b, s: (0, 0, 0)),
                  pl.BlockSpec((CHUNK, SGU_GROUPS), lambda b, s: (0, 0))],
        out_specs=pl.BlockSpec((ts, SGU_DIM), lambda b, s: (b * ns + s, 0)),
        out_shape=jax.ShapeDtypeStruct((batch * seq, SGU_DIM), BF16),
        compiler_params=_cp(("arbitrary", "arbitrary")),
        name="sgu_prompt",
    )(ha, ln_g, ln_b, w_s, b_s_t)


def _mix_sample_body(glu_ref, uv_ref, st_ref, cw_ref, cb_ref, cg_ref, cbt_ref, sg_ref, sb_ref, wv_ref, bv_ref,
                     co_ref, nst_ref, so_ref, vn_ref, hp_ref, *, db, n_tok):
    cw = cw_ref[...]
    for b in range(db):
        rs = slice(b * n_tok, (b + 1) * n_tok)
        a = glu_ref[rs, :CONV_DIM].astype(F32)
        gt = glu_ref[rs, CONV_DIM:].astype(F32)
        hp_ref[0:HALO, :] = st_ref[b]
        hp_ref[HALO:HALO + n_tok, :] = a * _sigmoid(gt)
        rows = [jnp.sum(hp_ref[t:t + CONV_WIDTH, :] * cw, axis=0, keepdims=True) for t in range(n_tok)]
        c = jnp.concatenate(rows, axis=0) + cb_ref[...]
        y = _ln_rows(c, cg_ref[...], cbt_ref[...])
        co_ref[rs, :] = y * _sigmoid(y)
        nst_ref[b] = hp_ref[n_tok:n_tok + HALO, :]
        u = uv_ref[rs, :SGU_DIM].astype(F32)
        vn = _ln_rows(uv_ref[rs, SGU_DIM:].astype(F32), sg_ref[...], sb_ref[...])
        vn_ref[rs, :] = vn
        mixed = []
        for t in range(n_tok):
            acc = bv_ref[t:t + 1, :]
            for s in range(t + 1):
                acc = acc + wv_ref[t, s:s + 1, :] * vn[s:s + 1, :]
            mixed.append(acc)
        so_ref[rs, :] = u * jnp.concatenate(mixed, axis=0)


def _mix_sample(glu, uv, state, conv_w, conv_b, cln_g, cln_b, sln_g, sln_b, wv, bv, db, n_tok):
    rows = db * n_tok
    return pl.pallas_call(
        functools.partial(_mix_sample_body, db=db, n_tok=n_tok),
        out_shape=[jax.ShapeDtypeStruct((rows, CONV_DIM), F32),
                   jax.ShapeDtypeStruct((db, HALO, CONV_DIM), F32),
                   jax.ShapeDtypeStruct((rows, SGU_DIM), F32),
                   jax.ShapeDtypeStruct((rows, SGU_DIM), F32)],
        scratch_shapes=[pltpu.VMEM((_round_up(HALO + n_tok, 8), CONV_DIM), F32)],
        name="mix_sample",
    )(glu, uv, state, conv_w, conv_b, cln_g, cln_b, sln_g, sln_b, wv, bv)


def _router_body(x_ref, w_ref, b_ref, ti_ref, tg_ref, *, n_exp):
    logits = jnp.dot(x_ref[...], w_ref[...], preferred_element_type=F32,
                     precision=lax.Precision.HIGHEST) + b_ref[...]
    lane = lax.broadcasted_iota(I32, logits.shape, 1)
    lane_f = lane.astype(F32)
    neg = -jnp.inf
    lg = jnp.where(lane < n_exp, logits, neg)
    v1 = lg.max(axis=-1, keepdims=True)
    i1 = jnp.where(lg == v1, lane_f, float(LANES)).min(axis=-1, keepdims=True)
    lg2 = jnp.where(lane_f == i1, neg, lg)
    v2 = lg2.max(axis=-1, keepdims=True)
    i2 = jnp.where(lg2 == v2, lane_f, float(LANES)).min(axis=-1, keepdims=True)
    e = jnp.exp(v2 - v1)
    den = 1.0 + e
    ti_ref[...] = jnp.where(lane == 0, i1, jnp.where(lane == 1, i2, 0.0)).astype(I32)
    tg_ref[...] = jnp.where(lane == 0, 1.0 / den, jnp.where(lane == 1, e / den, 0.0))


def _router(x, w_pad, b_pad, n_exp, tm):
    m, d = x.shape
    return pl.pallas_call(
        functools.partial(_router_body, n_exp=n_exp),
        grid=(m // tm,),
        in_specs=[pl.BlockSpec((tm, d), lambda i: (i, 0)),
                  pl.BlockSpec((d, LANES), lambda i: (0, 0)),
                  pl.BlockSpec((1, LANES), lambda i: (0, 0))],
        out_specs=[pl.BlockSpec((tm, LANES), lambda i: (i, 0)),
                   pl.BlockSpec((tm, LANES), lambda i: (i, 0))],
        out_shape=[jax.ShapeDtypeStruct((m, LANES), I32), jax.ShapeDtypeStruct((m, LANES), F32)],
        compiler_params=_cp(("arbitrary",)),
        name="moe_router",
    )(x, w_pad, b_pad)


def _row_copy(src_hbm, dst, src_row, dst_row, sem):
    return pltpu.make_async_copy(src_hbm.at[pl.ds(src_row, 1)], dst.at[pl.ds(dst_row, 1)], sem)


ROW_DMA_UNROLL = 8
ROW_DMA_SLOTS = 2


def _prefetch_chunks(issue, wait):
    c = pl.program_id(0)
    ahead = ROW_DMA_SLOTS - 1

    @pl.when(c == 0)
    def _():
        for first in range(ahead):
            @pl.when(first < pl.num_programs(0))
            def _():
                issue(first, first)

    @pl.when(c + ahead < pl.num_programs(0))
    def _():
        issue(c + ahead, (c + ahead) % ROW_DMA_SLOTS)

    slot = c % ROW_DMA_SLOTS
    wait(slot)
    return slot


def _dispatch_body(src_ref, x_hbm, o_ref, buf, sem, *, rows):
    def issue(chunk, slot):
        def start(u, carry):
            for v in range(ROW_DMA_UNROLL):
                r = v * (rows // ROW_DMA_UNROLL) + u
                _row_copy(x_hbm, buf.at[slot], src_ref[chunk * rows + r], r, sem.at[slot]).start()
            return carry

        lax.fori_loop(0, rows // ROW_DMA_UNROLL, start, 0)

    def wait(slot):
        def w(r, carry):
            _row_copy(x_hbm, buf.at[slot], 0, r, sem.at[slot]).wait()
            return carry

        lax.fori_loop(0, rows, w, 0)

    slot = _prefetch_chunks(issue, wait)
    o_ref[...] = buf[slot].astype(o_ref.dtype)


def _dispatch(src, x, n_slots, rows):
    d = x.shape[1]
    grid_spec = pltpu.PrefetchScalarGridSpec(
        num_scalar_prefetch=1,
        grid=(n_slots // rows,),
        in_specs=[pl.BlockSpec(memory_space=pl.ANY)],
        out_specs=pl.BlockSpec((rows, d), lambda i, s: (i, 0)),
        scratch_shapes=[pltpu.VMEM((ROW_DMA_SLOTS, rows, d), x.dtype), pltpu.SemaphoreType.DMA((ROW_DMA_SLOTS,))],
    )
    assert rows % ROW_DMA_UNROLL == 0
    return pl.pallas_call(
        functools.partial(_dispatch_body, rows=rows),
        grid_spec=grid_spec,
        out_shape=jax.ShapeDtypeStruct((n_slots, d), BF16),
        compiler_params=_cp(("arbitrary",)),
        name="moe_dispatch",
    )(src, x)


def _tile_copy(src, dst, sem):
    return pltpu.make_async_copy(src, dst, sem)


def _moe_ffn_body(exp_ref, row0_ref, nsub_ref, jmap_ref, nzero_ref, xs_hbm, w1_ref, w3_ref, w2_ref, ys_hbm,
                  xbuf, acc, w1b, w3b, w2b, sem, *, nj, sub):
    w = pl.program_id(0)
    j = pl.program_id(1)
    nsub = nsub_ref[w]
    row0 = row0_ref[w]
    nzero = nzero_ref[w]

    def zero_copy(t):
        r = pl.multiple_of(row0 + t * sub, sub)
        return _tile_copy(acc.at[pl.ds(0, sub)], ys_hbm.at[pl.ds(r, sub)], sem)

    @pl.when(jnp.logical_and(j == 0, nzero > 0))
    def _():
        acc[0:sub, :] = jnp.zeros((sub, acc.shape[1]), F32)

        def zs(t, carry):
            zero_copy(t).start()
            return carry

        def zw(t, carry):
            zero_copy(t).wait()
            return carry

        lax.fori_loop(0, nzero, zs, 0)
        lax.fori_loop(0, nzero, zw, 0)

    def in_copy(t):
        r = pl.multiple_of(t * sub, sub)
        return _tile_copy(xs_hbm.at[pl.ds(pl.multiple_of(row0 + r, sub), sub)], xbuf.at[pl.ds(r, sub)], sem)

    def out_copy(t):
        r = pl.multiple_of(t * sub, sub)
        return _tile_copy(acc.at[pl.ds(r, sub)], ys_hbm.at[pl.ds(pl.multiple_of(row0 + r, sub), sub)], sem)

    def for_tiles(fn):
        def it(t, carry):
            fn(t)
            return carry
        lax.fori_loop(0, nsub, it, 0)

    def zero_tile(t):
        acc[pl.ds(pl.multiple_of(t * sub, sub), sub), :] = jnp.zeros((sub, acc.shape[1]), F32)

    @pl.when(j == 0)
    def _():
        for_tiles(lambda t: in_copy(t).start())
        for_tiles(zero_tile)
        for_tiles(lambda t: in_copy(t).wait())

    def tile(t, n, w1v, w3v, w2v):
        r = pl.multiple_of(t * sub, sub)
        x = xbuf[pl.ds(r, n * sub), :]
        a = jnp.dot(x, w1v, preferred_element_type=F32)
        c = jnp.dot(x, w3v, preferred_element_type=F32)
        h = (a * _sigmoid(a) * c).astype(BF16)
        acc[pl.ds(r, n * sub), :] += jnp.dot(h, w2v, preferred_element_type=F32)

    def cast_weights():
        return w1_ref[...].astype(BF16), w3_ref[...].astype(BF16), w2_ref[...].astype(BF16)

    per_trip = 2 * MOE_WIDE
    n_trips = nsub // per_trip

    def trip(t, wv):
        tile(per_trip * t, MOE_WIDE, *wv)
        tile(per_trip * t + MOE_WIDE, MOE_WIDE, *wv)

    @pl.when(n_trips >= 1)
    def _():
        wv = cast_weights()
        w1b[...], w3b[...], w2b[...] = wv
        trip(0, wv)

    @pl.when(jnp.logical_and(n_trips == 0, nsub > 0))
    def _():
        w1b[...], w3b[...], w2b[...] = cast_weights()

    def later_trip(t, carry):
        trip(t, (w1b[...], w3b[...], w2b[...]))
        return carry

    lax.fori_loop(1, n_trips, later_trip, 0)

    def rest(t, carry):
        tile(t, 1, w1b[...], w3b[...], w2b[...])
        return carry

    lax.fori_loop(n_trips * per_trip, nsub, rest, 0)

    @pl.when(j == nj - 1)
    def _():
        for_tiles(lambda t: out_copy(t).start())
        for_tiles(lambda t: out_copy(t).wait())


def _moe_ffn(item_exp, item_row0, item_nsub, item_jlast, item_nzero, xs, w1, w3, w2, layer, n_items, super_rows,
             sub, tf):
    n_slots, d = xs.shape
    f = w1.shape[3]
    nj = f // tf

    def jj(w, j, jl):
        return jnp.where(jl[w] > 0, j, nj - 1)

    grid_spec = pltpu.PrefetchScalarGridSpec(
        num_scalar_prefetch=5,
        grid=(n_items, nj),
        in_specs=[pl.BlockSpec(memory_space=pl.ANY),
                  pl.BlockSpec((None, None, d, tf), lambda w, j, ex, r0, ns, jl, nz: (layer, ex[w], 0, jj(w, j, jl))),
                  pl.BlockSpec((None, None, d, tf), lambda w, j, ex, r0, ns, jl, nz: (layer, ex[w], 0, jj(w, j, jl))),
                  pl.BlockSpec((None, None, tf, d), lambda w, j, ex, r0, ns, jl, nz: (layer, ex[w], jj(w, j, jl), 0))],
        out_specs=pl.BlockSpec(memory_space=pl.ANY),
        scratch_shapes=[pltpu.VMEM((super_rows, d), BF16),
                        pltpu.VMEM((super_rows, d), F32),
                        pltpu.VMEM((d, tf), BF16),
                        pltpu.VMEM((d, tf), BF16),
                        pltpu.VMEM((tf, d), BF16),
                        pltpu.SemaphoreType.DMA(())],
    )
    return pl.pallas_call(
        functools.partial(_moe_ffn_body, nj=nj, sub=sub),
        grid_spec=grid_spec,
        out_shape=jax.ShapeDtypeStruct((n_slots, d), F32),
        compiler_params=_cp(("arbitrary", "arbitrary")),
        name="moe_ffn",
    )(item_exp, item_row0, item_nsub, item_jlast, item_nzero, xs, w1, w3, w2)


def _combine_body(pos_ref, ys_hbm, tg_ref, x_ref, g_ref, b_ref, of_ref, ob_ref, buf, sem, *, tb, alpha):
    def issue(chunk, slot):
        def start(u, carry):
            for v in range(ROW_DMA_UNROLL // TOP_K_EXPERTS):
                r = u * (ROW_DMA_UNROLL // TOP_K_EXPERTS) + v
                for k in range(TOP_K_EXPERTS):
                    p = pos_ref[TOP_K_EXPERTS * (chunk * tb + r) + k]
                    _row_copy(ys_hbm, buf.at[slot, k], p, r, sem.at[slot]).start()
            return carry

        lax.fori_loop(0, tb * TOP_K_EXPERTS // ROW_DMA_UNROLL, start, 0)

    def wait(slot):
        def w(r, carry):
            for k in range(TOP_K_EXPERTS):
                _row_copy(ys_hbm, buf.at[slot, k], 0, r, sem.at[slot]).wait()
            return carry

        lax.fori_loop(0, tb, w, 0)

    slot = _prefetch_chunks(issue, wait)
    f = tg_ref[:, 0:1] * buf[slot, 0] + tg_ref[:, 1:2] * buf[slot, 1]
    y = _ln_rows(alpha * x_ref[...] + f, g_ref[...], b_ref[...])
    of_ref[...] = y
    ob_ref[...] = y.astype(BF16)


def _combine_ln(pos, ys, tg, x, g, b, alpha, tb):
    m, d = x.shape
    grid_spec = pltpu.PrefetchScalarGridSpec(
        num_scalar_prefetch=1,
        grid=(m // tb,),
        in_specs=[pl.BlockSpec(memory_space=pl.ANY),
                  pl.BlockSpec((tb, LANES), lambda i, p: (i, 0)),
                  pl.BlockSpec((tb, d), lambda i, p: (i, 0)),
                  pl.BlockSpec((1, d), lambda i, p: (0, 0)),
                  pl.BlockSpec((1, d), lambda i, p: (0, 0))],
        out_specs=[pl.BlockSpec((tb, d), lambda i, p: (i, 0)),
                   pl.BlockSpec((tb, d), lambda i, p: (i, 0))],
        scratch_shapes=[pltpu.VMEM((ROW_DMA_SLOTS, TOP_K_EXPERTS, tb, d), F32),
                        pltpu.SemaphoreType.DMA((ROW_DMA_SLOTS,))],
    )
    assert tb * TOP_K_EXPERTS % ROW_DMA_UNROLL == 0
    return pl.pallas_call(
        functools.partial(_combine_body, tb=tb, alpha=alpha),
        grid_spec=grid_spec,
        out_shape=[jax.ShapeDtypeStruct((m, d), F32), jax.ShapeDtypeStruct((m, d), BF16)],
        compiler_params=_cp(("arbitrary",)),
        name="moe_combine_ln",
    )(pos, ys, tg, x, g, b)


def _moe_plan(top_i, n_real, n_exp, n_slots, n_items, super_rows, sub):
    a = top_i.shape[0] * TOP_K_EXPERTS
    e_flat = top_i.reshape(-1)
    tok = jnp.arange(a, dtype=I32) // TOP_K_EXPERTS
    real = tok < n_real
    onehot = jnp.logical_and(e_flat[:, None] == jnp.arange(n_exp, dtype=I32)[None, :], real[:, None]).astype(I32)
    csum = jnp.cumsum(onehot, axis=0)
    rank = jnp.take_along_axis(csum, e_flat[:, None], axis=1)[:, 0] - 1
    counts = csum[-1]
    padded = (counts + sub - 1) // sub * sub
    gstart = jnp.cumsum(padded) - padded
    pos = jnp.where(real, gstart[e_flat] + rank, jnp.arange(a, dtype=I32) % n_slots).astype(I32)
    src = (jnp.arange(n_slots, dtype=I32) % n_real).at[jnp.where(real, pos, n_slots)].set(tok, mode="drop")
    n_super = (padded + super_rows - 1) // super_rows
    iend = jnp.cumsum(n_super)
    istart = iend - n_super
    total = iend[-1]
    wi = jnp.arange(n_items, dtype=I32)
    valid = wi < total
    e_w = jnp.minimum(jnp.searchsorted(iend, wi, side="right"), n_exp - 1).astype(I32)
    last_e = e_w[jnp.maximum(total - 1, 0)]
    e_w = jnp.where(valid, e_w, last_e)
    k_w = wi - istart[e_w]
    subs_e = padded[e_w] // sub
    items_e = jnp.maximum(n_super[e_w], 1)
    base, extra = subs_e // items_e, subs_e % items_e
    nsub = jnp.where(valid, base + (k_w < extra).astype(I32), 0).astype(I32)
    first_sub = k_w * base + jnp.minimum(k_w, extra)
    z0 = jnp.sum(padded) + (wi - total) * super_rows
    nzero = jnp.where(valid, 0, jnp.clip((n_slots - z0) // sub, 0, super_rows // sub)).astype(I32)
    row0 = jnp.where(valid, gstart[e_w] + first_sub * sub, jnp.where(nzero > 0, z0, 0)).astype(I32)
    return pos, src, e_w, row0, nsub, valid.astype(I32), nzero


def _rel_bucket(dist):
    n = jnp.maximum(dist, 0)
    max_exact = REL_BUCKETS // 2
    nf = jnp.maximum(n, 1).astype(F32)
    large = max_exact + (jnp.log(nf / max_exact) / math.log(REL_MAX_DIST / max_exact)
                         * (REL_BUCKETS - max_exact)).astype(I32)
    large = jnp.minimum(large, REL_BUCKETS - 1)
    return jnp.where(n < max_exact, n, large)


def _shifted_bias(rel_bias, dist):
    t = rel_bias[_rel_bucket(dist)] - rel_bias[REL_BUCKETS - 1]
    t = jnp.where((dist >= 0)[..., None], t, 0.0)
    return jnp.moveaxis(t, -1, 0)


def _pick_tile(m_rows, target):
    return max(t for t in range(16, target + 1, 16) if m_rows % t == 0)


def _pad_rows(x, rows):
    return jnp.pad(x, ((0, rows - x.shape[0]), (0, 0)))


def kernel(x_prompt, x_sample, cache_k, cache_v, cache_ik, state_conv, page_table, w_in, conv_w, conv_b,
           conv_ln_g, conv_ln_b, sgu_ln_g, sgu_ln_b, sgu_w, sgu_b, w_pa, w_pb, w_pc, w_out, ln1_g, ln1_b,
           ln2_g, ln2_b, rel_bias, ffn_w1, ffn_w3, ffn_w2, moe_router, moe_router_b, moe_w1, moe_w3, moe_w2):
    batch, seq, d_model = x_prompt.shape
    db, n_tok, _ = x_sample.shape
    depth = w_in.shape[0]
    n_pool = cache_k.shape[1]
    n_exp = moe_router.shape[2]
    mp, ms = batch * seq, db * n_tok
    assert ms <= SAMPLE_TILE and mp % SAMPLE_TILE == 0
    m_all = mp + SAMPLE_TILE
    tm = m_all // N_ROW_TILES
    assert tm * N_ROW_TILES == m_all and tm % 16 == 0
    alpha = (2 * depth) ** 0.25
    tq = min(256, seq)

    xf = jnp.concatenate([x_prompt.reshape(mp, d_model), x_sample.reshape(ms, d_model),
                          jnp.zeros((m_all - mp - ms, d_model), F32)], axis=0)
    xb = xf.astype(BF16)
    w_in_t = jnp.swapaxes(w_in, 1, 2)
    ha_rows = ([W_Q_ROW0 + HA_TILE * t for t in range(QKV_W // HA_TILE)]
               + [W_GLU_ROW0 + HA_TILE * t for t in range(2 * CONV_DIM // HA_TILE)]
               + [W_UV_ROW0 + HA_TILE * t for t in range(2 * SGU_DIM // HA_TILE)]
               + [W_GATE_ROW0 + HA_TILE * t for t in range(N_BRANCH * d_model // HA_TILE)]
               + [W_IQ_ROW0 + HA_TILE * t for t in range(IQ_W // HA_TILE)])
    w_pa_b, w_pb_b, w_pc_b, w_out_b = (w.astype(BF16) for w in (w_pa, w_pb, w_pc, w_out))
    ffn_w2_b = ffn_w2.astype(BF16)

    kk = jnp.arange(2 * tq, dtype=I32)
    d_wrap = jnp.where(kk < tq, -kk, 2 * tq - kk)

    def toeplitz(offset):
        v = _shifted_bias(rel_bias, d_wrap + offset)
        return jnp.tile(v, (1, tq))[:, :tq * (2 * tq - 1)].reshape(N_HEADS, tq, 2 * tq - 1)[:, :, :tq]

    bias_tiles = jnp.stack([toeplitz(0), toeplitz(tq)])
    tok = jnp.arange(n_tok, dtype=I32)
    lane = jnp.arange(LANES, dtype=I32)
    d_last = PAGE_SIZE + tok[:, None] - lane[None, :]
    bias_last = _shifted_bias(rel_bias, d_last).reshape(N_HEADS * n_tok, LANES)
    bias_new = _shifted_bias(rel_bias, tok[:, None] - lane[None, :]).reshape(N_HEADS * n_tok, LANES)

    ck = cache_k.reshape(depth, n_pool, PAGE_SIZE * N_KV_HEADS, HEAD_DIM)
    cv = cache_v.reshape(depth, n_pool, PAGE_SIZE * N_KV_HEADS, HEAD_DIM)
    cik_t = jnp.swapaxes(cache_ik, 2, 3)
    n_pages = page_table.shape[1]
    idx_group = math.gcd(n_pages, IDX_PAGE_GROUP)
    attn_group = math.gcd(n_pages, ATTN_PAGE_GROUP)
    n_slots = _round_up(TOP_K_EXPERTS * (mp + ms) + n_exp * (MOE_SUB - 1), MOE_SUB)
    n_items = -(-n_slots // MOE_SUPER) + n_exp

    outs = {k: [] for k in ("k", "v", "ik", "conv_p", "conv_s", "sgu_s")}
    for l in range(depth):
        ha = _in_proj(xb, w_in_t, l, ha_rows, tm, HA_TILE)
        kv, ikw, k4, v4 = _kv_proj(xb, w_in_t, l, tm)

        a_p = _attn_prompt(ha, kv, ikw, bias_tiles, batch, seq, tq)
        conv_w_rep = jnp.broadcast_to(conv_w[l].reshape(CONV_WIDTH, CONV_CHUNKS, 1, LANES).transpose(1, 0, 2, 3),
                                      (CONV_CHUNKS, CONV_WIDTH, F32_SUBLANES, LANES))
        c_p, conv_state_p = _conv_prompt(ha, conv_w_rep, conv_b[l][None], conv_ln_g[l][None], conv_ln_b[l][None],
                                         batch, seq, min(256, seq))
        s_p = _sgu_prompt(ha, sgu_ln_g[l][None], sgu_ln_b[l][None], sgu_w[l], sgu_b[l].T, batch, seq,
                          min(512, seq))

        has, kvs, ikws = ha[mp:mp + ms], kv[mp:mp + ms], ikw[mp:mp + ms]
        iq_rows = has[:, HA_IQ:HA_IQ + IQ_W].reshape(db, n_tok * IDX_HEADS, IDX_DIM)
        iw_rows = (ikws[:, IKW_IW:IKW_IW + IDX_HEADS] * (IDX_DIM ** -0.5 * IDX_HEADS ** -0.5)
                   ).reshape(db, n_tok * IDX_HEADS, 1)
        iw_rows = jnp.broadcast_to(iw_rows, (db, n_tok * IDX_HEADS, LANES))
        scores = _idx_sample(page_table, iq_rows, iw_rows, cik_t, l, n_tok, idx_group)
        pad_rows = LANES - n_tok
        ik_new = jnp.pad(ikws[:, :IDX_DIM].reshape(db, n_tok, IDX_DIM), ((0, 0), (0, pad_rows), (0, 0)))
        scores_new = _idx_new(iq_rows, iw_rows, ik_new, n_tok)
        q_rows = has[:, HA_Q:HA_Q + QKV_W].reshape(db, n_tok, N_HEADS, HEAD_DIM).transpose(0, 2, 1, 3)
        q_rows = q_rows.reshape(db, N_HEADS * n_tok, HEAD_DIM)
        k_new = jnp.pad(kvs[:, KV_K:KV_K + KV_W].reshape(db, n_tok, KV_W), ((0, 0), (0, pad_rows), (0, 0)))
        v_new = jnp.pad(kvs[:, KV_V:KV_V + KV_W].reshape(db, n_tok, KV_W), ((0, 0), (0, pad_rows), (0, 0)))
        a_s = _attn_sample(page_table, scores, scores_new, q_rows, k_new, v_new, bias_last, bias_new,
                           ck, cv, l, n_tok, attn_group)
        a_s = a_s.reshape(db, N_HEADS, n_tok, HEAD_DIM).transpose(0, 2, 1, 3).reshape(ms, QKV_W)
        gd = SGU_DIM // SGU_GROUPS
        wv = jnp.repeat(sgu_w[l][:, :n_tok, :n_tok].transpose(1, 2, 0), gd, axis=-1)
        bv = jnp.repeat(sgu_b[l][:, :n_tok].T, gd, axis=-1)
        c_s, conv_state_s, s_s, vn_s = _mix_sample(
            has[:, HA_GLU:HA_GLU + 2 * CONV_DIM], has[:, HA_UV:HA_UV + 2 * SGU_DIM], state_conv[l],
            conv_w[l], conv_b[l][None], conv_ln_g[l][None], conv_ln_b[l][None],
            sgu_ln_g[l][None], sgu_ln_b[l][None], wv, bv, db, n_tok)

        sample_rows = tuple(_pad_rows(t.astype(BF16), SAMPLE_TILE) for t in (a_s, c_s, s_s))
        merged = _merge((a_p, c_p, s_p), sample_rows, ha, w_pa_b, w_pb_b, w_pc_b, l, SAMPLE_TILE)
        x1f, x1b = _mm_ln(merged, w_out_b, l, xf, ln1_g[l][None], ln1_b[l][None], alpha, _pick_tile(m_all, 384))

        j = l // 2
        if l % 2 == 0:
            h = _swiglu_up(x1b, ffn_w1, ffn_w3, j, tm, 512)
            xf, xb = _mm_ln(h, ffn_w2_b, j, x1f, ln2_g[l][None], ln2_b[l][None], alpha, _pick_tile(m_all, 384))
        else:
            w_r = jnp.pad(moe_router[j], ((0, 0), (0, LANES - n_exp)))
            b_r = jnp.pad(moe_router_b[j], (0, LANES - n_exp))[None]
            ti, tg = _router(x1f, w_r, b_r, n_exp, _pick_tile(m_all, 528))
            pos, src, it_e, it_r0, it_ns, it_valid, it_nz = _moe_plan(ti[:, :TOP_K_EXPERTS], mp + ms, n_exp,
                                                                      n_slots, n_items, MOE_SUPER, MOE_SUB)
            xs = _dispatch(src, x1f, n_slots, MOE_SUB)
            ys = _moe_ffn(it_e, it_r0, it_ns, it_valid, it_nz, xs, moe_w1, moe_w3, moe_w2, j, n_items,
                          MOE_SUPER, MOE_SUB, MOE_TF)
            tb = COMBINE_TB if m_all % COMBINE_TB == 0 else 16
            xf, xb = _combine_ln(pos, ys, tg, x1f, ln2_g[l][None], ln2_b[l][None], alpha, tb)

        outs["k"].append(k4)
        outs["v"].append(v4)
        outs["ik"].append(ikw[:, :IDX_DIM])
        outs["conv_p"].append(conv_state_p)
        outs["conv_s"].append(conv_state_s)
        outs["sgu_s"].append(vn_s.reshape(db, n_tok, SGU_DIM))

    def split(name, rows_per_token, width_shape):
        st = jnp.stack(outs[name])
        p = st[:, :mp * rows_per_token].reshape((depth, batch, seq) + width_shape)
        s = st[:, mp * rows_per_token:(mp + ms) * rows_per_token].reshape((depth, db, n_tok) + width_shape)
        return p, s

    k_p, k_s = split("k", N_KV_HEADS, (N_KV_HEADS, HEAD_DIM))
    v_p, v_s = split("v", N_KV_HEADS, (N_KV_HEADS, HEAD_DIM))
    ik_p, ik_s = split("ik", 1, (IDX_DIM,))
    y_prompt = xf[:mp].reshape(batch, seq, d_model)
    y_sample = xf[mp:mp + ms].reshape(db, n_tok, d_model)
    return (y_prompt, y_sample, k_p, v_p, ik_p, jnp.stack(outs["conv_p"]), k_s, v_s, ik_s,
            jnp.stack(outs["conv_s"]), jnp.stack(outs["sgu_s"]))
```

```python
import functools
import math

import jax
import jax.numpy as jnp
from jax import lax
from jax.experimental import pallas as pl
from jax.experimental.pallas import tpu as pltpu

F32 = jnp.float32
BF16 = jnp.bfloat16
I32 = jnp.int32

N_HEADS = 16
HEAD_DIM = 128
N_KV_HEADS = 4
KV_REP = N_HEADS // N_KV_HEADS
IDX_HEADS = 16
IDX_DIM = 64
TOPK_MAX = 256
PAGE_SIZE = 128
CONV_DIM = 1024
CONV_WIDTH = 31
SGU_DIM = 1024
SGU_GROUPS = 8
CHUNK = 128
N_BRANCH = 3
TOP_K_EXPERTS = 2
REL_BUCKETS = 32
REL_MAX_DIST = 128
LN_EPS = 1e-5

V7X_VMEM_BYTES = 64 * 2**20
VMEM_LIMIT = V7X_VMEM_BYTES - 8 * 2**20
LANES = 128
F32_SUBLANES = 8
INT_MIN = -(2**31)
MASK_NEG = -1e30

QKV_W = N_HEADS * HEAD_DIM
KV_W = N_KV_HEADS * HEAD_DIM
IQ_W = IDX_HEADS * IDX_DIM
HA_Q, HA_GLU, HA_UV, HA_GATE, HA_IQ = 0, 2048, 4096, 6144, 12288
HA_TILE = 1024
W_Q_ROW0 = 0
W_KV_ROW0 = W_Q_ROW0 + QKV_W
W_IQ_ROW0 = W_KV_ROW0 + 2 * KV_W
W_IK_ROW0 = W_IQ_ROW0 + IQ_W
W_GLU_ROW0 = W_IK_ROW0 + IDX_DIM + IDX_HEADS
W_UV_ROW0 = W_GLU_ROW0 + 2 * CONV_DIM
W_GATE_ROW0 = W_UV_ROW0 + 2 * SGU_DIM
KV_K, KV_V = 0, KV_W
IKW_IW = IDX_DIM

N_ROW_TILES = 8
SAMPLE_TILE = 256
MOE_SUB = 256
MOE_SUPER = 2048
MOE_TF = 256
MOE_WIDE = 2
COMBINE_TB = 128
IDX_PAGE_GROUP = 32
ATTN_PAGE_GROUP = 32


def _cp(sem, vmem=VMEM_LIMIT):
    return pltpu.CompilerParams(dimension_semantics=sem, vmem_limit_bytes=vmem)


def _round_up(x, m):
    return (x + m - 1) // m * m


def _ln_rows(z, g, b):
    mu = jnp.mean(z, axis=-1, keepdims=True)
    d = z - mu
    var = jnp.mean(d * d, axis=-1, keepdims=True)
    return d * lax.rsqrt(var + LN_EPS) * g + b


def _sigmoid(x):
    return 1.0 / (1.0 + jnp.exp(-x))


def _in_proj_body(off_ref, x_ref, w_ref, o_ref, wt_ref):
    @pl.when(pl.program_id(1) == 0)
    def _():
        wt_ref[...] = w_ref[0].T.astype(BF16)

    o_ref[...] = jnp.dot(x_ref[...], wt_ref[...], preferred_element_type=F32).astype(o_ref.dtype)


def _in_proj(x, w_t, layer, row_offsets, tm, tn):
    m, k = x.shape
    n_tiles = len(row_offsets)
    grid_spec = pltpu.PrefetchScalarGridSpec(
        num_scalar_prefetch=1,
        grid=(n_tiles, m // tm),
        in_specs=[pl.BlockSpec((tm, k), lambda j, i, off: (i, 0)),
                  pl.BlockSpec((pl.Element(1), pl.Element(tn), pl.Element(k)),
                               lambda j, i, off: (layer, pl.multiple_of(off[j], F32_SUBLANES), 0))],
        out_specs=pl.BlockSpec((tm, tn), lambda j, i, off: (i, j)),
        scratch_shapes=[pltpu.VMEM((k, tn), BF16)],
    )
    return pl.pallas_call(
        _in_proj_body,
        grid_spec=grid_spec,
        out_shape=jax.ShapeDtypeStruct((m, n_tiles * tn), BF16),
        compiler_params=_cp(("arbitrary", "arbitrary")),
        name="in_proj",
    )(jnp.asarray(row_offsets, I32), x, w_t)


def _kv_proj_body(x_ref, wkv_ref, wik_ref, kv_ref, ik_ref, k4_ref, v4_ref, wkv_t, wik_t):
    @pl.when(pl.program_id(0) == 0)
    def _():
        wkv_t[...] = wkv_ref[...].T.astype(BF16)
        wik_t[...] = wik_ref[...].T.astype(BF16)

    x = x_ref[...]
    tm = x.shape[0]
    kv = jnp.dot(x, wkv_t[...], preferred_element_type=F32)
    kv_ref[...] = kv
    ik_ref[...] = jnp.dot(x, wik_t[...], preferred_element_type=F32)
    for g in range(N_KV_HEADS):
        k4_ref[pl.ds(g, tm, stride=N_KV_HEADS), :] = kv[:, KV_K + g * HEAD_DIM:KV_K + (g + 1) * HEAD_DIM]
        v4_ref[pl.ds(g, tm, stride=N_KV_HEADS), :] = kv[:, KV_V + g * HEAD_DIM:KV_V + (g + 1) * HEAD_DIM]


def _kv_proj(x, w_t, layer, tm):
    m, k = x.shape
    assert W_KV_ROW0 % (2 * KV_W) == 0 and W_IK_ROW0 % LANES == 0
    row = lambda i: (i, 0)
    per_head = jax.ShapeDtypeStruct((m * N_KV_HEADS, HEAD_DIM), F32)
    return pl.pallas_call(
        _kv_proj_body,
        grid=(m // tm,),
        in_specs=[pl.BlockSpec((tm, k), row),
                  pl.BlockSpec((None, 2 * KV_W, k), lambda i: (layer, W_KV_ROW0 // (2 * KV_W), 0)),
                  pl.BlockSpec((None, LANES, k), lambda i: (layer, W_IK_ROW0 // LANES, 0))],
        out_specs=[pl.BlockSpec((tm, 2 * KV_W), row), pl.BlockSpec((tm, LANES), row),
                   pl.BlockSpec((tm * N_KV_HEADS, HEAD_DIM), row), pl.BlockSpec((tm * N_KV_HEADS, HEAD_DIM), row)],
        out_shape=[jax.ShapeDtypeStruct((m, 2 * KV_W), F32), jax.ShapeDtypeStruct((m, LANES), F32),
                   per_head, per_head],
        scratch_shapes=[pltpu.VMEM((k, 2 * KV_W), BF16), pltpu.VMEM((k, LANES), BF16)],
        compiler_params=_cp(("arbitrary",)),
        name="kv_proj",
    )(x, w_t, w_t)


def _row_halves(rows):
    half = rows // 2 // 16 * 16
    return (slice(0, half), slice(half, rows)) if half > 0 else (slice(0, rows),)


def _mm_ln_body(x_ref, w_ref, r_ref, g_ref, b_ref, of_ref, ob_ref, *, alpha):
    for rs in _row_halves(x_ref.shape[0]):
        z = jnp.dot(x_ref[rs, :], w_ref[...], preferred_element_type=F32) + alpha * r_ref[rs, :]
        y = _ln_rows(z, g_ref[...], b_ref[...])
        of_ref[rs, :] = y
        ob_ref[rs, :] = y.astype(BF16)


def _mm_ln(x, w, layer, resid, g, b, alpha, tm):
    m, k = x.shape
    n = w.shape[2]
    row = lambda i: (i, 0)
    return pl.pallas_call(
        functools.partial(_mm_ln_body, alpha=alpha),
        grid=(m // tm,),
        in_specs=[pl.BlockSpec((tm, k), row),
                  pl.BlockSpec((None, k, n), lambda i: (layer, 0, 0), pipeline_mode=pl.Buffered(1)),
                  pl.BlockSpec((tm, n), row),
                  pl.BlockSpec((1, n), lambda i: (0, 0)),
                  pl.BlockSpec((1, n), lambda i: (0, 0))],
        out_specs=[pl.BlockSpec((tm, n), row), pl.BlockSpec((tm, n), row)],
        out_shape=[jax.ShapeDtypeStruct((m, n), F32), jax.ShapeDtypeStruct((m, n), BF16)],
        compiler_params=_cp(("arbitrary",)),
        name="proj_ln",
    )(x, w, resid, g, b)


def _swiglu_up_body(x_ref, w1_ref, w3_ref, o_ref, w1b, w3b):
    @pl.when(pl.program_id(1) == 0)
    def _():
        w1b[...] = w1_ref[...].astype(BF16)
        w3b[...] = w3_ref[...].astype(BF16)

    for rs in _row_halves(x_ref.shape[0]):
        x = x_ref[rs, :]
        a = jnp.dot(x, w1b[...], preferred_element_type=F32)
        c = jnp.dot(x, w3b[...], preferred_element_type=F32)
        o_ref[rs, :] = (a * _sigmoid(a) * c).astype(o_ref.dtype)


def _swiglu_up(x, w1, w3, layer, tm, tf):
    m, k = x.shape
    f = w1.shape[2]
    return pl.pallas_call(
        _swiglu_up_body,
        grid=(f // tf, m // tm),
        in_specs=[pl.BlockSpec((tm, k), lambda j, i: (i, 0)),
                  pl.BlockSpec((None, k, tf), lambda j, i: (layer, 0, j)),
                  pl.BlockSpec((None, k, tf), lambda j, i: (layer, 0, j))],
        out_specs=pl.BlockSpec((tm, tf), lambda j, i: (i, j)),
        out_shape=jax.ShapeDtypeStruct((m, f), BF16),
        scratch_shapes=[pltpu.VMEM((k, tf), BF16), pltpu.VMEM((k, tf), BF16)],
        compiler_params=_cp(("arbitrary", "arbitrary")),
        name="ffn_up",
    )(x, w1, w3)


def _merge_body(ap_ref, cp_ref, sp_ref, as_ref, cs_ref, ss_ref, g_ref, wa_ref, wb_ref, wc_ref, o_ref, *, n_prompt):
    i = pl.program_id(0)
    d = o_ref.shape[1]

    def merge(a_ref, c_ref, s_ref):
        for rs in _row_halves(o_ref.shape[0]):
            pa = jnp.dot(a_ref[rs, :], wa_ref[...], preferred_element_type=F32)
            pb = jnp.dot(c_ref[rs, :], wb_ref[...], preferred_element_type=F32)
            pc = jnp.dot(s_ref[rs, :], wc_ref[...], preferred_element_type=F32)
            o = (_sigmoid(g_ref[rs, :d].astype(F32)) * pa + _sigmoid(g_ref[rs, d:2 * d].astype(F32)) * pb
                 + _sigmoid(g_ref[rs, 2 * d:].astype(F32)) * pc)
            o_ref[rs, :] = o.astype(o_ref.dtype)

    @pl.when(i < n_prompt)
    def _():
        merge(ap_ref, cp_ref, sp_ref)

    @pl.when(i >= n_prompt)
    def _():
        merge(as_ref, cs_ref, ss_ref)


def _merge(prompt, sample, ha, w_pa, w_pb, w_pc, layer, tm):
    m = ha.shape[0]
    d = w_pa.shape[2]
    n_prompt = prompt[0].shape[0] // tm
    assert prompt[0].shape[0] % tm == 0 and sample[0].shape[0] == tm and m == (n_prompt + 1) * tm
    assert HA_GATE % (N_BRANCH * d) == 0
    p_row = lambda i: (jnp.minimum(i, n_prompt - 1), 0)
    fixed = lambda i: (0, 0)
    return pl.pallas_call(
        functools.partial(_merge_body, n_prompt=n_prompt),
        grid=(n_prompt + 1,),
        in_specs=[pl.BlockSpec((tm, t.shape[1]), p_row) for t in prompt]
        + [pl.BlockSpec((tm, t.shape[1]), fixed) for t in sample]
        + [pl.BlockSpec((tm, N_BRANCH * d), lambda i: (i, HA_GATE // (N_BRANCH * d)))]
        + [pl.BlockSpec((None,) + w.shape[1:], lambda i: (layer, 0, 0), pipeline_mode=pl.Buffered(1))
           for w in (w_pa, w_pb, w_pc)],
        out_specs=pl.BlockSpec((tm, d), lambda i: (i, 0)),
        out_shape=jax.ShapeDtypeStruct((m, d), BF16),
        compiler_params=_cp(("arbitrary",)),
        name="branch_merge",
    )(*prompt, *sample, ha, w_pa, w_pb, w_pc)


def _sortable_key(score):
    bits = pltpu.bitcast(score, I32)
    return bits ^ ((bits >> 31) & jnp.int32(0x7FFFFFFF))


def _kth_largest_key(count_ge, rows, k):
    def step(it, t):
        cand = t + jnp.left_shift(jnp.int32(1), 31 - it)
        return jnp.where(count_ge(cand) >= k, cand, t)

    return lax.fori_loop(0, 32, step, jnp.full((rows, 1), INT_MIN, I32))


RADIX4_STEPS = 16
RADIX4_FIELD_BITS = 5
RADIX4_WIDE = 4096.0


def _kth_largest_key_radix4(count3, rows, k, n_valid):
    kf = float(k)

    def cond(state):
        it, _, _, pending = state
        return jnp.logical_and(it < RADIX4_STEPS, pending > 0.0)

    def body(state):
        it, t, cnt, _ = state
        one = jnp.left_shift(jnp.int32(1), 30 - 2 * it)
        c1 = t + one
        c2 = c1 + one
        c3 = c2 + one
        n1, n2, n3 = count3(c1, c2, c3)
        t = jnp.where(n3 >= kf, c3, jnp.where(n2 >= kf, c2, jnp.where(n1 >= kf, c1, t)))
        cnt = jnp.where(n3 >= kf, n3, jnp.where(n2 >= kf, n2, jnp.where(n1 >= kf, n1, cnt)))
        settled = jnp.logical_or(cnt == kf, n_valid <= kf)
        return it + 1, t, cnt, jnp.max(jnp.where(settled, 0.0, 1.0))

    init = (jnp.int32(0), jnp.full((rows, 1), INT_MIN, I32), jnp.full((rows, 1), -1.0, F32), jnp.float32(1.0))
    return lax.while_loop(cond, body, init)[1]


def _lane_tile(x, width):
    return x if width == LANES else jnp.concatenate([x] * (width // LANES), axis=1)


def _attn_prompt_body(q_ref, iq_ref, iw_ref, kv_ref, ik_ref, bias_ref, o_ref,
                      keys_ref, mb_ref, iwb_ref, m_ref, acc_ref, *, tq, k_sel):
    i = pl.program_id(1)
    nkb = i + 1
    tk = tq
    row = lax.broadcasted_iota(I32, (tq, tk), 0)
    col = lax.broadcasted_iota(I32, (tq, tk), 1)
    low_half = lax.broadcasted_iota(I32, (tk, LANES), 1) < IDX_DIM
    iw = iw_ref[...] * (IDX_DIM ** -0.5 * IDX_HEADS ** -0.5)
    for h in range(IDX_HEADS):
        iwb_ref[h] = jnp.broadcast_to(iw[:, IKW_IW + h:IKW_IW + h + 1], (tq, LANES))
    iq_pairs = [iq_ref[:, p * LANES:(p + 1) * LANES] for p in range(IDX_HEADS // 2)]

    def idx_block(kb, carry):
        ik_lo = jnp.where(low_half, ik_ref[pl.ds(pl.multiple_of(kb * tk, tk), tk), :], 0.0)
        ik_sides = (ik_lo.astype(BF16), pltpu.roll(ik_lo, IDX_DIM, axis=1).astype(BF16))
        acc = jnp.zeros((tq, tk), F32)
        for h in range(IDX_HEADS):
            s = lax.dot_general(iq_pairs[h // 2], ik_sides[h % 2], (((1,), (1,)), ((), ())),
                                preferred_element_type=F32)
            acc = acc + jnp.maximum(s, 0.0) * _lane_tile(iwb_ref[h], tk)
        causal = (col + kb * tk) <= (row + i * tq)
        keys_ref[kb] = jnp.where(causal, _sortable_key(acc), jnp.int32(INT_MIN))
        return carry

    def idx_pair(t, carry):
        idx_block(2 * t, carry)
        return idx_block(2 * t + 1, carry)

    lax.fori_loop(0, nkb // 2, idx_pair, 0)

    @pl.when(nkb % 2 == 1)
    def _():
        idx_block(nkb - 1, 0)
        keys_ref[nkb] = jnp.full((tq, tk), INT_MIN, I32)

    fb = RADIX4_FIELD_BITS
    field_mask = (1 << fb) - 1
    inc1, inc2, inc3 = 1, 1 + (1 << fb), 1 + (1 << fb) + (1 << 2 * fb)

    def count3(c1, c2, c3):
        b1, b2, b3 = (jnp.broadcast_to(c, (tq, LANES)) for c in (c1, c2, c3))

        def cb(pair, acc):
            for kb in (2 * pair, 2 * pair + 1):
                keys = keys_ref[kb]
                for t in range(tk // LANES):
                    kk = keys[:, t * LANES:(t + 1) * LANES]
                    acc = acc + jnp.where(kk >= b3, inc3, jnp.where(kk >= b2, inc2, jnp.where(kk >= b1, inc1, 0)))
            return acc

        acc = lax.fori_loop(0, (nkb + 1) // 2, cb, jnp.zeros((tq, LANES), I32))
        n3 = (acc >> (2 * fb)).astype(F32).sum(axis=-1, keepdims=True)
        low = ((acc & field_mask).astype(F32) + ((acc >> fb) & field_mask).astype(F32) * RADIX4_WIDE)
        low = low.sum(axis=-1, keepdims=True)
        n2 = jnp.floor(low * (1.0 / RADIX4_WIDE))
        return low - RADIX4_WIDE * n2, n2, n3

    n_valid = (lax.broadcasted_iota(I32, (tq, 1), 0) + (i * tq + 1)).astype(F32)
    thr = _kth_largest_key_radix4(count3, tq, k_sel, n_valid)
    thr = jnp.maximum(thr, jnp.int32(INT_MIN + 1))
    thr_b = _lane_tile(jnp.broadcast_to(thr, (tq, LANES)), tk)

    def mask_block(kb, carry):
        mb_ref[kb] = jnp.where(keys_ref[kb] >= thr_b, 0.0, MASK_NEG)
        return carry

    lax.fori_loop(0, nkb, mask_block, 0)

    scale = HEAD_DIM ** -0.5
    for g in range(N_KV_HEADS):
        qg = jnp.concatenate(
            [q_ref[:, (KV_REP * g + r) * HEAD_DIM:(KV_REP * g + r + 1) * HEAD_DIM] for r in range(KV_REP)], axis=0)
        m_ref[...] = jnp.full(m_ref.shape, MASK_NEG, F32)
        acc_ref[...] = jnp.zeros(acc_ref.shape, F32)

        def block(kb, bias_idx, g=g, qg=qg):
            r0 = pl.multiple_of(kb * tk, tk)
            kblk = kv_ref[pl.ds(r0, tk), KV_K + g * HEAD_DIM:KV_K + (g + 1) * HEAD_DIM].astype(BF16)
            vblk = kv_ref[pl.ds(r0, tk), KV_V + g * HEAD_DIM:KV_V + (g + 1) * HEAD_DIM].astype(BF16)
            s = lax.dot_general(qg, kblk, (((1,), (1,)), ((), ())), preferred_element_type=F32) * scale
            s = s.reshape(KV_REP, tq, tk) + mb_ref[kb][None]
            if bias_idx is not None:
                s = s + bias_ref[bias_idx, KV_REP * g:KV_REP * (g + 1)]
            s = s.reshape(KV_REP * tq, tk)
            m_old = m_ref[...]
            m_new = jnp.maximum(m_old, s.max(axis=-1, keepdims=True))
            p = jnp.exp(s - _lane_tile(m_new, tk))
            alpha = jnp.exp(m_old - m_new)
            v_ones = jnp.concatenate([vblk, jnp.ones((tk, LANES), BF16)], axis=1)
            acc_ref[...] = (_lane_tile(alpha, HEAD_DIM + LANES) * acc_ref[...]
                            + jnp.dot(p.astype(BF16), v_ones, preferred_element_type=F32))
            m_ref[...] = m_new

        n_far = jnp.maximum(i - 1, 0)

        def far_pair(t, carry):
            block(2 * t, None)
            block(2 * t + 1, None)
            return carry

        lax.fori_loop(0, n_far // 2, far_pair, 0)

        @pl.when(n_far % 2 == 1)
        def _():
            block(n_far - 1, None)

        @pl.when(i >= 1)
        def _():
            block(i - 1, 1)
            block(i, 0)

        @pl.when(i == 0)
        def _():
            block(i, 0)
        out = acc_ref[:, :HEAD_DIM] / acc_ref[:, HEAD_DIM:]
        for r in range(KV_REP):
            h = KV_REP * g + r
            o_ref[:, h * HEAD_DIM:(h + 1) * HEAD_DIM] = out[r * tq:(r + 1) * tq].astype(o_ref.dtype)


def _attn_prompt(ha, kv, ikw, bias_tiles, batch, seq, tq):
    nq = seq // tq
    k_sel = min(TOPK_MAX, seq // 4)
    assert seq // LANES < (1 << RADIX4_FIELD_BITS)
    body = functools.partial(_attn_prompt_body, tq=tq, k_sel=k_sel)
    return pl.pallas_call(
        body,
        grid=(batch, nq),
        in_specs=[pl.BlockSpec((tq, QKV_W), lambda b, i: (b * nq + i, HA_Q // QKV_W)),
                  pl.BlockSpec((tq, IQ_W), lambda b, i: (b * nq + i, HA_IQ // IQ_W)),
                  pl.BlockSpec((tq, LANES), lambda b, i: (b * nq + i, 0)),
                  pl.BlockSpec((seq, 2 * KV_W), lambda b, i: (b, 0), pipeline_mode=pl.Buffered(1)),
                  pl.BlockSpec((seq, LANES), lambda b, i: (b, 0), pipeline_mode=pl.Buffered(1)),
                  pl.BlockSpec((2, N_HEADS, tq, tq), lambda b, i: (0, 0, 0, 0), pipeline_mode=pl.Buffered(1))],
        out_specs=pl.BlockSpec((tq, QKV_W), lambda b, i: (b * nq + i, 0)),
        out_shape=jax.ShapeDtypeStruct((batch * seq, QKV_W), BF16),
        scratch_shapes=[pltpu.VMEM((nq + nq % 2, tq, tq), I32),
                        pltpu.VMEM((nq, tq, tq), F32),
                        pltpu.VMEM((IDX_HEADS, tq, LANES), F32),
                        pltpu.VMEM((KV_REP * tq, LANES), F32),
                        pltpu.VMEM((KV_REP * tq, HEAD_DIM + LANES), F32)],
        compiler_params=_cp(("arbitrary", "arbitrary")),
        name="attn_prompt",
    )(ha, ha, ikw, kv, ikw, bias_tiles)


def _page_spec(block, layer, group, slot):
    return pl.BlockSpec((None, None) + block, lambda b, p, pt: (layer, pt[b, p * group + slot], 0, 0))


def _idx_sample_body(pt_ref, iq_ref, iw_ref, *refs, n_tok, group):
    pages, o_ref = refs[:group], refs[group]
    ikt = jnp.concatenate([r[...].astype(BF16) for r in pages], axis=1)
    s = jnp.dot(iq_ref[0], ikt, preferred_element_type=F32)
    r = jnp.maximum(s, 0.0) * _lane_tile(iw_ref[0], group * PAGE_SIZE)
    tok_scores = r.reshape(n_tok, IDX_HEADS, group * PAGE_SIZE).sum(axis=1)
    for t in range(group):
        o_ref[0, t] = tok_scores[:, t * PAGE_SIZE:(t + 1) * PAGE_SIZE]


def _idx_sample(page_table, iq_rows, iw_rows, cache_ik_t, layer, n_tok, group):
    db, n_pages = page_table.shape
    rows = n_tok * IDX_HEADS
    body = functools.partial(_idx_sample_body, n_tok=n_tok, group=group)
    grid_spec = pltpu.PrefetchScalarGridSpec(
        num_scalar_prefetch=1,
        grid=(db, n_pages // group),
        in_specs=[pl.BlockSpec((1, rows, IDX_DIM), lambda b, p, pt: (b, 0, 0)),
                  pl.BlockSpec((1, rows, LANES), lambda b, p, pt: (b, 0, 0))]
        + [_page_spec((IDX_DIM, PAGE_SIZE), layer, group, t) for t in range(group)],
        out_specs=pl.BlockSpec((1, group, n_tok, PAGE_SIZE), lambda b, p, pt: (b, p, 0, 0)),
    )
    return pl.pallas_call(
        body,
        grid_spec=grid_spec,
        out_shape=jax.ShapeDtypeStruct((db, n_pages, n_tok, PAGE_SIZE), F32),
        compiler_params=_cp(("arbitrary", "arbitrary")),
        name="idx_sample",
    )(page_table, iq_rows, iw_rows, *([cache_ik_t] * group))


def _attn_sample_body(pt_ref, sc_ref, scn_ref, q_ref, kn_ref, vn_ref, bl_ref, bn_ref, *refs,
                      n_tok, n_pages, k_sel, group):
    kp_refs, vp_refs = refs[:group], refs[group:2 * group]
    o_ref, thr_ref, m_ref, l_ref, acc_ref = refs[2 * group:]
    p = pl.program_id(1)
    n_steps = n_pages // group
    grp = KV_REP * n_tok
    width = group * PAGE_SIZE
    scale = HEAD_DIM ** -0.5
    r_i = lax.broadcasted_iota(I32, (n_tok, LANES), 0)
    c_i = lax.broadcasted_iota(I32, (n_tok, LANES), 1)
    keys_new = jnp.where(c_i <= r_i, _sortable_key(scn_ref[0]), jnp.int32(INT_MIN))

    @pl.when(p == 0)
    def _():
        keys = _sortable_key(sc_ref[0])

        def count_ge(cand):
            cand_b = jnp.broadcast_to(cand, (n_tok, LANES))
            c = jnp.where(keys >= cand_b[None], 1.0, 0.0).sum(axis=0) + jnp.where(keys_new >= cand_b, 1.0, 0.0)
            return c.sum(axis=-1, keepdims=True)

        thr = _kth_largest_key(count_ge, n_tok, float(k_sel))
        thr_ref[...] = jnp.broadcast_to(jnp.maximum(thr, jnp.int32(INT_MIN + 1)), (n_tok, LANES))
        m_ref[...] = jnp.full(m_ref.shape, MASK_NEG, F32)
        l_ref[...] = jnp.zeros(l_ref.shape, F32)
        acc_ref[...] = jnp.zeros(acc_ref.shape, F32)

    thr_b = thr_ref[...]

    def update(g, s, vblk):
        sl = slice(g * grp, (g + 1) * grp)
        m_old = m_ref[sl, :]
        m_new = jnp.maximum(m_old, s.max(axis=-1, keepdims=True))
        pr = jnp.exp(s - _lane_tile(m_new, s.shape[1]))
        alpha = jnp.exp(m_old - m_new)
        l_ref[sl, :] = alpha * l_ref[sl, :] + pr.sum(axis=-1, keepdims=True)
        acc_ref[sl, :] = alpha * acc_ref[sl, :] + jnp.dot(pr.astype(BF16), vblk, preferred_element_type=F32)
        m_ref[sl, :] = m_new

    def group_rows(mask_tok):
        return jnp.concatenate([mask_tok] * KV_REP, axis=0)

    base = p * group
    mb = jnp.concatenate(
        [jnp.where(_sortable_key(sc_ref[0, base + t]) >= thr_b, 0.0, MASK_NEG) for t in range(group)], axis=1)
    mb_g = group_rows(mb)
    is_last = p == n_steps - 1
    zeros_head = jnp.zeros((grp, width - PAGE_SIZE), F32)
    for g in range(N_KV_HEADS):
        qg = q_ref[0, g * grp:(g + 1) * grp, :]
        kg = jnp.concatenate([r[pl.ds(g, PAGE_SIZE, stride=N_KV_HEADS), :].astype(BF16) for r in kp_refs], axis=0)
        vg = jnp.concatenate([r[pl.ds(g, PAGE_SIZE, stride=N_KV_HEADS), :].astype(BF16) for r in vp_refs], axis=0)
        s = lax.dot_general(qg, kg, (((1,), (1,)), ((), ())), preferred_element_type=F32) * scale
        near = jnp.where(is_last, bl_ref[g * grp:(g + 1) * grp, :], 0.0)
        s = s + mb_g + (near if group == 1 else jnp.concatenate([zeros_head, near], axis=1))
        update(g, s, vg)

    @pl.when(is_last)
    def _():
        mbn_g = group_rows(jnp.where(keys_new >= thr_b, 0.0, MASK_NEG))
        for g in range(N_KV_HEADS):
            qg = q_ref[0, g * grp:(g + 1) * grp, :]
            kblk = kn_ref[0, :, g * HEAD_DIM:(g + 1) * HEAD_DIM].astype(BF16)
            vblk = vn_ref[0, :, g * HEAD_DIM:(g + 1) * HEAD_DIM].astype(BF16)
            s = lax.dot_general(qg, kblk, (((1,), (1,)), ((), ())), preferred_element_type=F32) * scale
            s = s + mbn_g + bn_ref[g * grp:(g + 1) * grp, :]
            update(g, s, vblk)
        o_ref[0] = acc_ref[...] / l_ref[...]


def _attn_sample(page_table, scores, scores_new, q_rows, k_new, v_new, bias_last, bias_new,
                 cache_k, cache_v, layer, n_tok, group):
    db, n_pages = page_table.shape
    past = n_pages * PAGE_SIZE
    rows = N_HEADS * n_tok
    k_sel = min(TOPK_MAX, (past + n_tok) // 4)
    body = functools.partial(_attn_sample_body, n_tok=n_tok, n_pages=n_pages, k_sel=k_sel, group=group)
    page = (PAGE_SIZE * N_KV_HEADS, HEAD_DIM)
    grid_spec = pltpu.PrefetchScalarGridSpec(
        num_scalar_prefetch=1,
        grid=(db, n_pages // group),
        in_specs=[pl.BlockSpec((1, n_pages, n_tok, PAGE_SIZE), lambda b, p, pt: (b, 0, 0, 0)),
                  pl.BlockSpec((1, n_tok, LANES), lambda b, p, pt: (b, 0, 0)),
                  pl.BlockSpec((1, rows, HEAD_DIM), lambda b, p, pt: (b, 0, 0)),
                  pl.BlockSpec((1, LANES, KV_W), lambda b, p, pt: (b, 0, 0)),
                  pl.BlockSpec((1, LANES, KV_W), lambda b, p, pt: (b, 0, 0)),
                  pl.BlockSpec((rows, LANES), lambda b, p, pt: (0, 0)),
                  pl.BlockSpec((rows, LANES), lambda b, p, pt: (0, 0))]
        + [_page_spec(page, layer, group, t) for t in range(group)]
        + [_page_spec(page, layer, group, t) for t in range(group)],
        out_specs=pl.BlockSpec((1, rows, HEAD_DIM), lambda b, p, pt: (b, 0, 0)),
        scratch_shapes=[pltpu.VMEM((n_tok, LANES), I32),
                        pltpu.VMEM((rows, LANES), F32),
                        pltpu.VMEM((rows, LANES), F32),
                        pltpu.VMEM((rows, HEAD_DIM), F32)],
    )
    return pl.pallas_call(
        body,
        grid_spec=grid_spec,
        out_shape=jax.ShapeDtypeStruct((db, rows, HEAD_DIM), F32),
        compiler_params=_cp(("arbitrary", "arbitrary")),
        name="attn_sample",
    )(page_table, scores, scores_new, q_rows, k_new, v_new, bias_last, bias_new,
      *([cache_k] * group), *([cache_v] * group))


def _idx_new_body(iq_ref, iw_ref, ik_ref, o_ref, *, n_tok):
    for b in range(iq_ref.shape[0]):
        s = lax.dot_general(iq_ref[b], ik_ref[b].astype(BF16), (((1,), (1,)), ((), ())),
                            preferred_element_type=F32)
        r = jnp.maximum(s, 0.0) * iw_ref[b]
        o_ref[b] = r.reshape(n_tok, IDX_HEADS, LANES).sum(axis=1)


def _idx_new(iq_rows, iw_rows, ik_new, n_tok):
    db = iq_rows.shape[0]
    return pl.pallas_call(
        functools.partial(_idx_new_body, n_tok=n_tok),
        out_shape=jax.ShapeDtypeStruct((db, n_tok, LANES), F32),
        name="idx_new",
    )(iq_rows, iw_rows, ik_new)


HALO = CONV_WIDTH - 1
HALO_PAD = 32
CONV_ROW_BLOCK = 128


CONV_CHUNKS = CONV_DIM // LANES


def _conv_prompt_body(glu_ref, cw_ref, cb_ref, g_ref, b_ref, o_ref, st_ref, hp_ref, c_ref, sh_ref, *, ts, ns):
    s_idx = pl.program_id(1)

    @pl.when(s_idx == 0)
    def _():
        hp_ref[:, 0:HALO_PAD, :] = jnp.zeros((CONV_CHUNKS, HALO_PAD, LANES), F32)

    for c in range(CONV_CHUNKS):
        cs = slice(c * LANES, (c + 1) * LANES)
        a = glu_ref[:, cs].astype(F32)
        gt = glu_ref[:, CONV_DIM + c * LANES:CONV_DIM + (c + 1) * LANES].astype(F32)
        hp_ref[c, HALO_PAD:HALO_PAD + ts, :] = a * _sigmoid(gt)
    off = HALO_PAD - HALO
    rows = min(ts, CONV_ROW_BLOCK)

    def chunk(c, carry):
        for r0 in range(0, ts, rows):
            acc = jnp.zeros((rows // F32_SUBLANES, F32_SUBLANES, LANES), F32)
            for shift in range(F32_SUBLANES):
                taps = range(shift, CONV_WIDTH, F32_SUBLANES)
                span = rows + taps[-1] - shift
                sh_ref[0:span, :] = hp_ref[c, off + r0 + shift:off + r0 + shift + span, :]
                for w in taps:
                    x = sh_ref[w - shift:w - shift + rows, :].reshape(rows // F32_SUBLANES, F32_SUBLANES, LANES)
                    acc = acc + x * cw_ref[c, w][None]
            c_ref[c, r0:r0 + rows, :] = acc.reshape(rows, LANES)
        return carry

    lax.fori_loop(0, CONV_CHUNKS, chunk, 0)

    conv = [c_ref[c] + cb_ref[:, c * LANES:(c + 1) * LANES] for c in range(CONV_CHUNKS)]
    mu = sum(v.sum(axis=-1, keepdims=True) for v in conv) / CONV_DIM
    var = sum(((v - mu) * (v - mu)).sum(axis=-1, keepdims=True) for v in conv) / CONV_DIM
    inv = lax.rsqrt(var + LN_EPS)
    for c in range(CONV_CHUNKS):
        cs = slice(c * LANES, (c + 1) * LANES)
        y = (conv[c] - mu) * inv * g_ref[:, cs] + b_ref[:, cs]
        o_ref[:, cs] = (y * _sigmoid(y)).astype(o_ref.dtype)

    @pl.when(s_idx == ns - 1)
    def _():
        for c in range(CONV_CHUNKS):
            st_ref[0, :, c * LANES:(c + 1) * LANES] = hp_ref[c, ts + off:ts + HALO_PAD, :]

    hp_ref[:, off:HALO_PAD, :] = hp_ref[:, ts + off:ts + HALO_PAD, :]


def _conv_prompt(ha, conv_w, conv_b, ln_g, ln_b, batch, seq, ts):
    ns = seq // ts
    body = functools.partial(_conv_prompt_body, ts=ts, ns=ns)
    vec = pl.BlockSpec((1, CONV_DIM), lambda b, s: (0, 0))
    return pl.pallas_call(
        body,
        grid=(batch, ns),
        in_specs=[pl.BlockSpec((ts, 2 * CONV_DIM), lambda b, s: (b * ns + s, HA_GLU // (2 * CONV_DIM))),
                  pl.BlockSpec((CONV_CHUNKS, CONV_WIDTH, F32_SUBLANES, LANES), lambda b, s: (0, 0, 0, 0)),
                  vec, vec, vec],
        out_specs=[pl.BlockSpec((ts, CONV_DIM), lambda b, s: (b * ns + s, 0)),
                   pl.BlockSpec((1, HALO, CONV_DIM), lambda b, s: (b, 0, 0))],
        out_shape=[jax.ShapeDtypeStruct((batch * seq, CONV_DIM), BF16),
                   jax.ShapeDtypeStruct((batch, HALO, CONV_DIM), F32)],
        scratch_shapes=[pltpu.VMEM((CONV_CHUNKS, HALO_PAD + ts, LANES), F32),
                        pltpu.VMEM((CONV_CHUNKS, ts, LANES), F32),
                        pltpu.VMEM((HALO_PAD + min(ts, CONV_ROW_BLOCK), LANES), F32)],
        compiler_params=_cp(("arbitrary", "arbitrary")),
        name="conv_prompt",
    )(ha, conv_w, conv_b, ln_g, ln_b)


def _sgu_prompt_body(uv_ref, g_ref, b_ref, w_ref, bs_ref, o_ref, *, ts):
    u = uv_ref[:, :SGU_DIM].astype(F32)
    vn = _ln_rows(uv_ref[:, SGU_DIM:].astype(F32), g_ref[...], b_ref[...]).astype(BF16)
    r_i = lax.broadcasted_iota(I32, (CHUNK, CHUNK), 0)
    c_i = lax.broadcasted_iota(I32, (CHUNK, CHUNK), 1)
    gd = SGU_DIM // SGU_GROUPS
    for g in range(SGU_GROUPS):
        wm = jnp.where(c_i <= r_i, w_ref[g], 0.0).astype(BF16)
        bias = bs_ref[:, g:g + 1]
        for c in range(ts // CHUNK):
            rs = slice(c * CHUNK, (c + 1) * CHUNK)
            gs = slice(g * gd, (g + 1) * gd)
            mixed = jnp.dot(wm, vn[rs, gs], preferred_element_type=F32) + bias
            o_ref[rs, gs] = (u[rs, gs] * mixed).astype(o_ref.dtype)


def _sgu_prompt(ha, ln_g, ln_b, w_s, b_s_t, batch, seq, ts):
    ns = seq // ts
    vec = pl.BlockSpec((1, SGU_DIM), lambda b, s: (0, 0))
    return pl.pallas_call(
        functools.partial(_sgu_prompt_body, ts=ts),
        grid=(batch, ns),
        in_specs=[pl.BlockSpec((ts, 2 * SGU_DIM), lambda b, s: (b * ns + s, HA_UV // (2 * SGU_DIM))),
                  vec, vec,
                  pl.BlockSpec((SGU_GROUPS, CHUNK, CHUNK), lambda b, s: (0, 0, 0)),
                  pl.BlockSpec((CHUNK, SGU_GROUPS), lambda b, s: (0, 0))],
        out_specs=pl.BlockSpec((ts, SGU_DIM), lambda b, s: (b * ns + s, 0)),
        out_shape=jax.ShapeDtypeStruct((batch * seq, SGU_DIM), BF16),
        compiler_params=_cp(("arbitrary", "arbitrary")),
        name="sgu_prompt",
    )(ha, ln_g, ln_b, w_s, b_s_t)


def _mix_sample_body(glu_ref, uv_ref, st_ref, cw_ref, cb_ref, cg_ref, cbt_ref, sg_ref, sb_ref, wv_ref, bv_ref,
                     co_ref, nst_ref, so_ref, vn_ref, hp_ref, *, db, n_tok):
    cw = cw_ref[...]
    for b in range(db):
        rs = slice(b * n_tok, (b + 1) * n_tok)
        a = glu_ref[rs, :CONV_DIM].astype(F32)
        gt = glu_ref[rs, CONV_DIM:].astype(F32)
        hp_ref[0:HALO, :] = st_ref[b]
        hp_ref[HALO:HALO + n_tok, :] = a * _sigmoid(gt)
        rows = [jnp.sum(hp_ref[t:t + CONV_WIDTH, :] * cw, axis=0, keepdims=True) for t in range(n_tok)]
        c = jnp.concatenate(rows, axis=0) + cb_ref[...]
        y = _ln_rows(c, cg_ref[...], cbt_ref[...])
        co_ref[rs, :] = y * _sigmoid(y)
        nst_ref[b] = hp_ref[n_tok:n_tok + HALO, :]
        u = uv_ref[rs, :SGU_DIM].astype(F32)
        vn = _ln_rows(uv_ref[rs, SGU_DIM:].astype(F32), sg_ref[...], sb_ref[...])
        vn_ref[rs, :] = vn
        mixed = []
        for t in range(n_tok):
            acc = bv_ref[t:t + 1, :]
            for s in range(t + 1):
                acc = acc + wv_ref[t, s:s + 1, :] * vn[s:s + 1, :]
            mixed.append(acc)
        so_ref[rs, :] = u * jnp.concatenate(mixed, axis=0)


def _mix_sample(glu, uv, state, conv_w, conv_b, cln_g, cln_b, sln_g, sln_b, wv, bv, db, n_tok):
    rows = db * n_tok
    return pl.pallas_call(
        functools.partial(_mix_sample_body, db=db, n_tok=n_tok),
        out_shape=[jax.ShapeDtypeStruct((rows, CONV_DIM), F32),
                   jax.ShapeDtypeStruct((db, HALO, CONV_DIM), F32),
                   jax.ShapeDtypeStruct((rows, SGU_DIM), F32),
                   jax.ShapeDtypeStruct((rows, SGU_DIM), F32)],
        scratch_shapes=[pltpu.VMEM((_round_up(HALO + n_tok, 8), CONV_DIM), F32)],
        name="mix_sample",
    )(glu, uv, state, conv_w, conv_b, cln_g, cln_b, sln_g, sln_b, wv, bv)


def _router_body(x_ref, w_ref, b_ref, ti_ref, tg_ref, *, n_exp):
    logits = jnp.dot(x_ref[...], w_ref[...], preferred_element_type=F32,
                     precision=lax.Precision.HIGHEST) + b_ref[...]
    lane = lax.broadcasted_iota(I32, logits.shape, 1)
    lane_f = lane.astype(F32)
    neg = -jnp.inf
    lg = jnp.where(lane < n_exp, logits, neg)
    v1 = lg.max(axis=-1, keepdims=True)
    i1 = jnp.where(lg == v1, lane_f, float(LANES)).min(axis=-1, keepdims=True)
    lg2 = jnp.where(lane_f == i1, neg, lg)
    v2 = lg2.max(axis=-1, keepdims=True)
    i2 = jnp.where(lg2 == v2, lane_f, float(LANES)).min(axis=-1, keepdims=True)
    e = jnp.exp(v2 - v1)
    den = 1.0 + e
    ti_ref[...] = jnp.where(lane == 0, i1, jnp.where(lane == 1, i2, 0.0)).astype(I32)
    tg_ref[...] = jnp.where(lane == 0, 1.0 / den, jnp.where(lane == 1, e / den, 0.0))


def _router(x, w_pad, b_pad, n_exp, tm):
    m, d = x.shape
    return pl.pallas_call(
        functools.partial(_router_body, n_exp=n_exp),
        grid=(m // tm,),
        in_specs=[pl.BlockSpec((tm, d), lambda i: (i, 0)),
                  pl.BlockSpec((d, LANES), lambda i: (0, 0)),
                  pl.BlockSpec((1, LANES), lambda i: (0, 0))],
        out_specs=[pl.BlockSpec((tm, LANES), lambda i: (i, 0)),
                   pl.BlockSpec((tm, LANES), lambda i: (i, 0))],
        out_shape=[jax.ShapeDtypeStruct((m, LANES), I32), jax.ShapeDtypeStruct((m, LANES), F32)],
        compiler_params=_cp(("arbitrary",)),
        name="moe_router",
    )(x, w_pad, b_pad)


def _row_copy(src_hbm, dst, src_row, dst_row, sem):
    return pltpu.make_async_copy(src_hbm.at[pl.ds(src_row, 1)], dst.at[pl.ds(dst_row, 1)], sem)


ROW_DMA_UNROLL = 8
ROW_DMA_SLOTS = 2


def _prefetch_chunks(issue, wait):
    c = pl.program_id(0)
    ahead = ROW_DMA_SLOTS - 1

    @pl.when(c == 0)
    def _():
        for first in range(ahead):
            @pl.when(first < pl.num_programs(0))
            def _():
                issue(first, first)

    @pl.when(c + ahead < pl.num_programs(0))
    def _():
        issue(c + ahead, (c + ahead) % ROW_DMA_SLOTS)

    slot = c % ROW_DMA_SLOTS
    wait(slot)
    return slot


def _dispatch_body(src_ref, x_hbm, o_ref, buf, sem, *, rows):
    def issue(chunk, slot):
        def start(u, carry):
            for v in range(ROW_DMA_UNROLL):
                r = v * (rows // ROW_DMA_UNROLL) + u
                _row_copy(x_hbm, buf.at[slot], src_ref[chunk * rows + r], r, sem.at[slot]).start()
            return carry

        lax.fori_loop(0, rows // ROW_DMA_UNROLL, start, 0)

    def wait(slot):
        def w(r, carry):
            _row_copy(x_hbm, buf.at[slot], 0, r, sem.at[slot]).wait()
            return carry

        lax.fori_loop(0, rows, w, 0)

    slot = _prefetch_chunks(issue, wait)
    o_ref[...] = buf[slot].astype(o_ref.dtype)


def _dispatch(src, x, n_slots, rows):
    d = x.shape[1]
    grid_spec = pltpu.PrefetchScalarGridSpec(
        num_scalar_prefetch=1,
        grid=(n_slots // rows,),
        in_specs=[pl.BlockSpec(memory_space=pl.ANY)],
        out_specs=pl.BlockSpec((rows, d), lambda i, s: (i, 0)),
        scratch_shapes=[pltpu.VMEM((ROW_DMA_SLOTS, rows, d), x.dtype), pltpu.SemaphoreType.DMA((ROW_DMA_SLOTS,))],
    )
    assert rows % ROW_DMA_UNROLL == 0
    return pl.pallas_call(
        functools.partial(_dispatch_body, rows=rows),
        grid_spec=grid_spec,
        out_shape=jax.ShapeDtypeStruct((n_slots, d), BF16),
        compiler_params=_cp(("arbitrary",)),
        name="moe_dispatch",
    )(src, x)


def _tile_copy(src, dst, sem):
    return pltpu.make_async_copy(src, dst, sem)


def _moe_ffn_body(exp_ref, row0_ref, nsub_ref, jmap_ref, nzero_ref, xs_hbm, w1_ref, w3_ref, w2_ref, ys_hbm,
                  xbuf, acc, w1b, w3b, w2b, sem, *, nj, sub):
    w = pl.program_id(0)
    j = pl.program_id(1)
    nsub = nsub_ref[w]
    row0 = row0_ref[w]
    nzero = nzero_ref[w]

    def zero_copy(t):
        r = pl.multiple_of(row0 + t * sub, sub)
        return _tile_copy(acc.at[pl.ds(0, sub)], ys_hbm.at[pl.ds(r, sub)], sem)

    @pl.when(jnp.logical_and(j == 0, nzero > 0))
    def _():
        acc[0:sub, :] = jnp.zeros((sub, acc.shape[1]), F32)

        def zs(t, carry):
            zero_copy(t).start()
            return carry

        def zw(t, carry):
            zero_copy(t).wait()
            return carry

        lax.fori_loop(0, nzero, zs, 0)
        lax.fori_loop(0, nzero, zw, 0)

    def in_copy(t):
        r = pl.multiple_of(t * sub, sub)
        return _tile_copy(xs_hbm.at[pl.ds(pl.multiple_of(row0 + r, sub), sub)], xbuf.at[pl.ds(r, sub)], sem)

    def out_copy(t):
        r = pl.multiple_of(t * sub, sub)
        return _tile_copy(acc.at[pl.ds(r, sub)], ys_hbm.at[pl.ds(pl.multiple_of(row0 + r, sub), sub)], sem)

    def for_tiles(fn):
        def it(t, carry):
            fn(t)
            return carry
        lax.fori_loop(0, nsub, it, 0)

    def zero_tile(t):
        acc[pl.ds(pl.multiple_of(t * sub, sub), sub), :] = jnp.zeros((sub, acc.shape[1]), F32)

    @pl.when(j == 0)
    def _():
        for_tiles(lambda t: in_copy(t).start())
        for_tiles(zero_tile)
        for_tiles(lambda t: in_copy(t).wait())

    def tile(t, n, w1v, w3v, w2v):
        r = pl.multiple_of(t * sub, sub)
        x = xbuf[pl.ds(r, n * sub), :]
        a = jnp.dot(x, w1v, preferred_element_type=F32)
        c = jnp.dot(x, w3v, preferred_element_type=F32)
        h = (a * _sigmoid(a) * c).astype(BF16)
        acc[pl.ds(r, n * sub), :] += jnp.dot(h, w2v, preferred_element_type=F32)

    def cast_weights():
        return w1_ref[...].astype(BF16), w3_ref[...].astype(BF16), w2_ref[...].astype(BF16)

    per_trip = 2 * MOE_WIDE
    n_trips = nsub // per_trip

    def trip(t, wv):
        tile(per_trip * t, MOE_WIDE, *wv)
        tile(per_trip * t + MOE_WIDE, MOE_WIDE, *wv)

    @pl.when(n_trips >= 1)
    def _():
        wv = cast_weights()
        w1b[...], w3b[...], w2b[...] = wv
        trip(0, wv)

    @pl.when(jnp.logical_and(n_trips == 0, nsub > 0))
    def _():
        w1b[...], w3b[...], w2b[...] = cast_weights()

    def later_trip(t, carry):
        trip(t, (w1b[...], w3b[...], w2b[...]))
        return carry

    lax.fori_loop(1, n_trips, later_trip, 0)

    def rest(t, carry):
        tile(t, 1, w1b[...], w3b[...], w2b[...])
        return carry

    lax.fori_loop(n_trips * per_trip, nsub, rest, 0)

    @pl.when(j == nj - 1)
    def _():
        for_tiles(lambda t: out_copy(t).start())
        for_tiles(lambda t: out_copy(t).wait())


def _moe_ffn(item_exp, item_row0, item_nsub, item_jlast, item_nzero, xs, w1, w3, w2, layer, n_items, super_rows,
             sub, tf):
    n_slots, d = xs.shape
    f = w1.shape[3]
    nj = f // tf

    def jj(w, j, jl):
        return jnp.where(jl[w] > 0, j, nj - 1)

    grid_spec = pltpu.PrefetchScalarGridSpec(
        num_scalar_prefetch=5,
        grid=(n_items, nj),
        in_specs=[pl.BlockSpec(memory_space=pl.ANY),
                  pl.BlockSpec((None, None, d, tf), lambda w, j, ex, r0, ns, jl, nz: (layer, ex[w], 0, jj(w, j, jl))),
                  pl.BlockSpec((None, None, d, tf), lambda w, j, ex, r0, ns, jl, nz: (layer, ex[w], 0, jj(w, j, jl))),
                  pl.BlockSpec((None, None, tf, d), lambda w, j, ex, r0, ns, jl, nz: (layer, ex[w], jj(w, j, jl), 0))],
        out_specs=pl.BlockSpec(memory_space=pl.ANY),
        scratch_shapes=[pltpu.VMEM((super_rows, d), BF16),
                        pltpu.VMEM((super_rows, d), F32),
                        pltpu.VMEM((d, tf), BF16),
                        pltpu.VMEM((d, tf), BF16),
                        pltpu.VMEM((tf, d), BF16),
                        pltpu.SemaphoreType.DMA(())],
    )
    return pl.pallas_call(
        functools.partial(_moe_ffn_body, nj=nj, sub=sub),
        grid_spec=grid_spec,
        out_shape=jax.ShapeDtypeStruct((n_slots, d), F32),
        compiler_params=_cp(("arbitrary", "arbitrary")),
        name="moe_ffn",
    )(item_exp, item_row0, item_nsub, item_jlast, item_nzero, xs, w1, w3, w2)


def _combine_body(pos_ref, ys_hbm, tg_ref, x_ref, g_ref, b_ref, *refs, tb, alpha, head_chunks):
    out_refs, (buf, sem) = refs[:-2], refs[-2:]

    def issue(chunk, slot):
        def start(u, carry):
            for v in range(ROW_DMA_UNROLL // TOP_K_EXPERTS):
                r = u * (ROW_DMA_UNROLL // TOP_K_EXPERTS) + v
                for k in range(TOP_K_EXPERTS):
                    p = pos_ref[TOP_K_EXPERTS * (chunk * tb + r) + k]
                    _row_copy(ys_hbm, buf.at[slot, k], p, r, sem.at[slot]).start()
            return carry

        lax.fori_loop(0, tb * TOP_K_EXPERTS // ROW_DMA_UNROLL, start, 0)

    def wait(slot):
        def w(r, carry):
            for k in range(TOP_K_EXPERTS):
                _row_copy(ys_hbm, buf.at[slot, k], 0, r, sem.at[slot]).wait()
            return carry

        lax.fori_loop(0, tb, w, 0)

    slot = _prefetch_chunks(issue, wait)
    f = tg_ref[:, 0:1] * buf[slot, 0] + tg_ref[:, 1:2] * buf[slot, 1]
    y = _ln_rows(alpha * x_ref[...] + f, g_ref[...], b_ref[...])
    if head_chunks is None:
        of_ref, ob_ref = out_refs
        of_ref[...] = y
        ob_ref[...] = y.astype(BF16)
    else:
        head_ref, tail_ref = out_refs
        c = pl.program_id(0)

        @pl.when(c < head_chunks)
        def _():
            head_ref[...] = y

        @pl.when(c >= head_chunks)
        def _():
            tail_ref[...] = y


def _combine_ln(pos, ys, tg, x, g, b, alpha, tb, split_rows=None):
    m, d = x.shape
    row = lambda i, p: (i, 0)
    if split_rows is None:
        head_chunks = None
        out_specs = [pl.BlockSpec((tb, d), row), pl.BlockSpec((tb, d), row)]
        out_shape = [jax.ShapeDtypeStruct((m, d), F32), jax.ShapeDtypeStruct((m, d), BF16)]
    else:
        assert split_rows % tb == 0 and 0 < split_rows < m
        head_chunks = split_rows // tb
        out_specs = [pl.BlockSpec((tb, d), lambda i, p: (jnp.minimum(i, head_chunks - 1), 0)),
                     pl.BlockSpec((tb, d), lambda i, p: (jnp.maximum(i - head_chunks, 0), 0))]
        out_shape = [jax.ShapeDtypeStruct((split_rows, d), F32), jax.ShapeDtypeStruct((m - split_rows, d), F32)]
    grid_spec = pltpu.PrefetchScalarGridSpec(
        num_scalar_prefetch=1,
        grid=(m // tb,),
        in_specs=[pl.BlockSpec(memory_space=pl.ANY),
                  pl.BlockSpec((tb, LANES), row),
                  pl.BlockSpec((tb, d), row),
                  pl.BlockSpec((1, d), lambda i, p: (0, 0)),
                  pl.BlockSpec((1, d), lambda i, p: (0, 0))],
        out_specs=out_specs,
        scratch_shapes=[pltpu.VMEM((ROW_DMA_SLOTS, TOP_K_EXPERTS, tb, d), F32),
                        pltpu.SemaphoreType.DMA((ROW_DMA_SLOTS,))],
    )
    assert tb * TOP_K_EXPERTS % ROW_DMA_UNROLL == 0
    return pl.pallas_call(
        functools.partial(_combine_body, tb=tb, alpha=alpha, head_chunks=head_chunks),
        grid_spec=grid_spec,
        out_shape=out_shape,
        compiler_params=_cp(("arbitrary",)),
        name="moe_combine_ln",
    )(pos, ys, tg, x, g, b)


def _moe_plan(top_i, n_real, n_exp, n_slots, n_items, super_rows, sub):
    a = top_i.shape[0] * TOP_K_EXPERTS
    e_flat = top_i.reshape(-1)
    tok = jnp.arange(a, dtype=I32) // TOP_K_EXPERTS
    real = tok < n_real
    onehot = jnp.logical_and(e_flat[:, None] == jnp.arange(n_exp, dtype=I32)[None, :], real[:, None]).astype(I32)
    csum = jnp.cumsum(onehot, axis=0)
    rank = jnp.take_along_axis(csum, e_flat[:, None], axis=1)[:, 0] - 1
    counts = csum[-1]
    padded = (counts + sub - 1) // sub * sub
    gstart = jnp.cumsum(padded) - padded
    pos = jnp.where(real, gstart[e_flat] + rank, jnp.arange(a, dtype=I32) % n_slots).astype(I32)
    src = (jnp.arange(n_slots, dtype=I32) % n_real).at[jnp.where(real, pos, n_slots)].set(tok, mode="drop")
    n_super = (padded + super_rows - 1) // super_rows
    iend = jnp.cumsum(n_super)
    istart = iend - n_super
    total = iend[-1]
    wi = jnp.arange(n_items, dtype=I32)
    valid = wi < total
    e_w = jnp.minimum(jnp.searchsorted(iend, wi, side="right"), n_exp - 1).astype(I32)
    last_e = e_w[jnp.maximum(total - 1, 0)]
    e_w = jnp.where(valid, e_w, last_e)
    k_w = wi - istart[e_w]
    subs_e = padded[e_w] // sub
    items_e = jnp.maximum(n_super[e_w], 1)
    base, extra = subs_e // items_e, subs_e % items_e
    nsub = jnp.where(valid, base + (k_w < extra).astype(I32), 0).astype(I32)
    first_sub = k_w * base + jnp.minimum(k_w, extra)
    z0 = jnp.sum(padded) + (wi - total) * super_rows
    nzero = jnp.where(valid, 0, jnp.clip((n_slots - z0) // sub, 0, super_rows // sub)).astype(I32)
    row0 = jnp.where(valid, gstart[e_w] + first_sub * sub, jnp.where(nzero > 0, z0, 0)).astype(I32)
    return pos, src, e_w, row0, nsub, valid.astype(I32), nzero


def _rel_bucket(dist):
    n = jnp.maximum(dist, 0)
    max_exact = REL_BUCKETS // 2
    nf = jnp.maximum(n, 1).astype(F32)
    large = max_exact + (jnp.log(nf / max_exact) / math.log(REL_MAX_DIST / max_exact)
                         * (REL_BUCKETS - max_exact)).astype(I32)
    large = jnp.minimum(large, REL_BUCKETS - 1)
    return jnp.where(n < max_exact, n, large)


def _shifted_bias(rel_bias, dist):
    t = rel_bias[_rel_bucket(dist)] - rel_bias[REL_BUCKETS - 1]
    t = jnp.where((dist >= 0)[..., None], t, 0.0)
    return jnp.moveaxis(t, -1, 0)


def _pick_tile(m_rows, target):
    return max(t for t in range(16, target + 1, 16) if m_rows % t == 0)


def _pad_rows(x, rows):
    return jnp.pad(x, ((0, rows - x.shape[0]), (0, 0)))


def kernel(x_prompt, x_sample, cache_k, cache_v, cache_ik, state_conv, page_table, w_in, conv_w, conv_b,
           conv_ln_g, conv_ln_b, sgu_ln_g, sgu_ln_b, sgu_w, sgu_b, w_pa, w_pb, w_pc, w_out, ln1_g, ln1_b,
           ln2_g, ln2_b, rel_bias, ffn_w1, ffn_w3, ffn_w2, moe_router, moe_router_b, moe_w1, moe_w3, moe_w2):
    batch, seq, d_model = x_prompt.shape
    db, n_tok, _ = x_sample.shape
    depth = w_in.shape[0]
    n_pool = cache_k.shape[1]
    n_exp = moe_router.shape[2]
    mp, ms = batch * seq, db * n_tok
    assert ms <= SAMPLE_TILE and mp % SAMPLE_TILE == 0
    m_all = mp + SAMPLE_TILE
    tm = m_all // N_ROW_TILES
    assert tm * N_ROW_TILES == m_all and tm % 16 == 0
    alpha = (2 * depth) ** 0.25
    tq = min(256, seq)

    xf = jnp.concatenate([x_prompt.reshape(mp, d_model), x_sample.reshape(ms, d_model),
                          jnp.zeros((m_all - mp - ms, d_model), F32)], axis=0)
    xb = xf.astype(BF16)
    w_in_t = jnp.swapaxes(w_in, 1, 2)
    ha_rows = ([W_Q_ROW0 + HA_TILE * t for t in range(QKV_W // HA_TILE)]
               + [W_GLU_ROW0 + HA_TILE * t for t in range(2 * CONV_DIM // HA_TILE)]
               + [W_UV_ROW0 + HA_TILE * t for t in range(2 * SGU_DIM // HA_TILE)]
               + [W_GATE_ROW0 + HA_TILE * t for t in range(N_BRANCH * d_model // HA_TILE)]
               + [W_IQ_ROW0 + HA_TILE * t for t in range(IQ_W // HA_TILE)])
    w_pa_b, w_pb_b, w_pc_b, w_out_b = (w.astype(BF16) for w in (w_pa, w_pb, w_pc, w_out))
    ffn_w2_b = ffn_w2.astype(BF16)

    kk = jnp.arange(2 * tq, dtype=I32)
    d_wrap = jnp.where(kk < tq, -kk, 2 * tq - kk)

    def toeplitz(offset):
        v = _shifted_bias(rel_bias, d_wrap + offset)
        return jnp.tile(v, (1, tq))[:, :tq * (2 * tq - 1)].reshape(N_HEADS, tq, 2 * tq - 1)[:, :, :tq]

    bias_tiles = jnp.stack([toeplitz(0), toeplitz(tq)])
    tok = jnp.arange(n_tok, dtype=I32)
    lane = jnp.arange(LANES, dtype=I32)
    d_last = PAGE_SIZE + tok[:, None] - lane[None, :]
    bias_last = _shifted_bias(rel_bias, d_last).reshape(N_HEADS * n_tok, LANES)
    bias_new = _shifted_bias(rel_bias, tok[:, None] - lane[None, :]).reshape(N_HEADS * n_tok, LANES)

    ck = cache_k.reshape(depth, n_pool, PAGE_SIZE * N_KV_HEADS, HEAD_DIM)
    cv = cache_v.reshape(depth, n_pool, PAGE_SIZE * N_KV_HEADS, HEAD_DIM)
    cik_t = jnp.swapaxes(cache_ik, 2, 3)
    n_pages = page_table.shape[1]
    idx_group = math.gcd(n_pages, IDX_PAGE_GROUP)
    attn_group = math.gcd(n_pages, ATTN_PAGE_GROUP)
    n_slots = _round_up(TOP_K_EXPERTS * (mp + ms) + n_exp * (MOE_SUB - 1), MOE_SUB)
    n_items = -(-n_slots // MOE_SUPER) + n_exp

    outs = {k: [] for k in ("k", "v", "ik", "conv_p", "conv_s", "sgu_s")}
    for l in range(depth):
        ha = _in_proj(xb, w_in_t, l, ha_rows, tm, HA_TILE)
        kv, ikw, k4, v4 = _kv_proj(xb, w_in_t, l, tm)

        a_p = _attn_prompt(ha, kv, ikw, bias_tiles, batch, seq, tq)
        conv_w_rep = jnp.broadcast_to(conv_w[l].reshape(CONV_WIDTH, CONV_CHUNKS, 1, LANES).transpose(1, 0, 2, 3),
                                      (CONV_CHUNKS, CONV_WIDTH, F32_SUBLANES, LANES))
        c_p, conv_state_p = _conv_prompt(ha, conv_w_rep, conv_b[l][None], conv_ln_g[l][None], conv_ln_b[l][None],
                                         batch, seq, min(256, seq))
        s_p = _sgu_prompt(ha, sgu_ln_g[l][None], sgu_ln_b[l][None], sgu_w[l], sgu_b[l].T, batch, seq,
                          min(512, seq))

        has, kvs, ikws = ha[mp:mp + ms], kv[mp:mp + ms], ikw[mp:mp + ms]
        iq_rows = has[:, HA_IQ:HA_IQ + IQ_W].reshape(db, n_tok * IDX_HEADS, IDX_DIM)
        iw_rows = (ikws[:, IKW_IW:IKW_IW + IDX_HEADS] * (IDX_DIM ** -0.5 * IDX_HEADS ** -0.5)
                   ).reshape(db, n_tok * IDX_HEADS, 1)
        iw_rows = jnp.broadcast_to(iw_rows, (db, n_tok * IDX_HEADS, LANES))
        scores = _idx_sample(page_table, iq_rows, iw_rows, cik_t, l, n_tok, idx_group)
        pad_rows = LANES - n_tok
        ik_new = jnp.pad(ikws[:, :IDX_DIM].reshape(db, n_tok, IDX_DIM), ((0, 0), (0, pad_rows), (0, 0)))
        scores_new = _idx_new(iq_rows, iw_rows, ik_new, n_tok)
        q_rows = has[:, HA_Q:HA_Q + QKV_W].reshape(db, n_tok, N_HEADS, HEAD_DIM).transpose(0, 2, 1, 3)
        q_rows = q_rows.reshape(db, N_HEADS * n_tok, HEAD_DIM)
        k_new = jnp.pad(kvs[:, KV_K:KV_K + KV_W].reshape(db, n_tok, KV_W), ((0, 0), (0, pad_rows), (0, 0)))
        v_new = jnp.pad(kvs[:, KV_V:KV_V + KV_W].reshape(db, n_tok, KV_W), ((0, 0), (0, pad_rows), (0, 0)))
        a_s = _attn_sample(page_table, scores, scores_new, q_rows, k_new, v_new, bias_last, bias_new,
                           ck, cv, l, n_tok, attn_group)
        a_s = a_s.reshape(db, N_HEADS, n_tok, HEAD_DIM).transpose(0, 2, 1, 3).reshape(ms, QKV_W)
        gd = SGU_DIM // SGU_GROUPS
        wv = jnp.repeat(sgu_w[l][:, :n_tok, :n_tok].transpose(1, 2, 0), gd, axis=-1)
        bv = jnp.repeat(sgu_b[l][:, :n_tok].T, gd, axis=-1)
        c_s, conv_state_s, s_s, vn_s = _mix_sample(
            has[:, HA_GLU:HA_GLU + 2 * CONV_DIM], has[:, HA_UV:HA_UV + 2 * SGU_DIM], state_conv[l],
            conv_w[l], conv_b[l][None], conv_ln_g[l][None], conv_ln_b[l][None],
            sgu_ln_g[l][None], sgu_ln_b[l][None], wv, bv, db, n_tok)

        sample_rows = tuple(_pad_rows(t.astype(BF16), SAMPLE_TILE) for t in (a_s, c_s, s_s))
        merged = _merge((a_p, c_p, s_p), sample_rows, ha, w_pa_b, w_pb_b, w_pc_b, l, SAMPLE_TILE)
        x1f, x1b = _mm_ln(merged, w_out_b, l, xf, ln1_g[l][None], ln1_b[l][None], alpha, _pick_tile(m_all, 384))

        j = l // 2
        if l % 2 == 0:
            h = _swiglu_up(x1b, ffn_w1, ffn_w3, j, tm, 512)
            xf, xb = _mm_ln(h, ffn_w2_b, j, x1f, ln2_g[l][None], ln2_b[l][None], alpha, _pick_tile(m_all, 384))
        else:
            w_r = jnp.pad(moe_router[j], ((0, 0), (0, LANES - n_exp)))
            b_r = jnp.pad(moe_router_b[j], (0, LANES - n_exp))[None]
            ti, tg = _router(x1f, w_r, b_r, n_exp, _pick_tile(m_all, 528))
            pos, src, it_e, it_r0, it_ns, it_valid, it_nz = _moe_plan(ti[:, :TOP_K_EXPERTS], mp + ms, n_exp,
                                                                      n_slots, n_items, MOE_SUPER, MOE_SUB)
            xs = _dispatch(src, x1f, n_slots, MOE_SUB)
            ys = _moe_ffn(it_e, it_r0, it_ns, it_valid, it_nz, xs, moe_w1, moe_w3, moe_w2, j, n_items,
                          MOE_SUPER, MOE_SUB, MOE_TF)
            tb = COMBINE_TB if m_all % COMBINE_TB == 0 and mp % COMBINE_TB == 0 else 16
            if l == depth - 1:
                y_head, y_tail = _combine_ln(pos, ys, tg, x1f, ln2_g[l][None], ln2_b[l][None], alpha, tb,
                                             split_rows=mp)
            else:
                xf, xb = _combine_ln(pos, ys, tg, x1f, ln2_g[l][None], ln2_b[l][None], alpha, tb)

        outs["k"].append(k4)
        outs["v"].append(v4)
        outs["ik"].append(ikw[:, :IDX_DIM])
        outs["conv_p"].append(conv_state_p)
        outs["conv_s"].append(conv_state_s)
        outs["sgu_s"].append(vn_s.reshape(db, n_tok, SGU_DIM))

    def split(name, rows_per_token, width_shape):
        st = jnp.stack(outs[name])
        p = st[:, :mp * rows_per_token].reshape((depth, batch, seq) + width_shape)
        s = st[:, mp * rows_per_token:(mp + ms) * rows_per_token].reshape((depth, db, n_tok) + width_shape)
        return p, s

    k_p, k_s = split("k", N_KV_HEADS, (N_KV_HEADS, HEAD_DIM))
    v_p, v_s = split("v", N_KV_HEADS, (N_KV_HEADS, HEAD_DIM))
    ik_p, ik_s = split("ik", 1, (IDX_DIM,))
    if depth % 2 == 0:
        y_prompt = y_head.reshape(batch, seq, d_model)
        y_sample = y_tail[:ms].reshape(db, n_tok, d_model)
    else:
        y_prompt = xf[:mp].reshape(batch, seq, d_model)
        y_sample = xf[mp:mp + ms].reshape(db, n_tok, d_model)
    return (y_prompt, y_sample, k_p, v_p, ik_p, jnp.stack(outs["conv_p"]), k_s, v_s, ik_s,
            jnp.stack(outs["conv_s"]), jnp.stack(outs["sgu_s"]))
```

```python
import functools
import math

import jax
import jax.numpy as jnp
from jax import lax
from jax.experimental import pallas as pl
from jax.experimental.pallas import tpu as pltpu

F32 = jnp.float32
BF16 = jnp.bfloat16
I32 = jnp.int32

N_HEADS = 16
HEAD_DIM = 128
N_KV_HEADS = 4
KV_REP = N_HEADS // N_KV_HEADS
IDX_HEADS = 16
IDX_DIM = 64
TOPK_MAX = 256
PAGE_SIZE = 128
CONV_DIM = 1024
CONV_WIDTH = 31
SGU_DIM = 1024
SGU_GROUPS = 8
CHUNK = 128
N_BRANCH = 3
TOP_K_EXPERTS = 2
REL_BUCKETS = 32
REL_MAX_DIST = 128
LN_EPS = 1e-5

V7X_VMEM_BYTES = 64 * 2**20
VMEM_LIMIT = V7X_VMEM_BYTES - 8 * 2**20
LANES = 128
F32_SUBLANES = 8
INT_MIN = -(2**31)
MASK_NEG = -1e30

QKV_W = N_HEADS * HEAD_DIM
KV_W = N_KV_HEADS * HEAD_DIM
IQ_W = IDX_HEADS * IDX_DIM
HA_Q, HA_GLU, HA_UV, HA_GATE, HA_IQ = 0, 2048, 4096, 6144, 12288
HA_TILE = 1024
W_Q_ROW0 = 0
W_KV_ROW0 = W_Q_ROW0 + QKV_W
W_IQ_ROW0 = W_KV_ROW0 + 2 * KV_W
W_IK_ROW0 = W_IQ_ROW0 + IQ_W
W_GLU_ROW0 = W_IK_ROW0 + IDX_DIM + IDX_HEADS
W_UV_ROW0 = W_GLU_ROW0 + 2 * CONV_DIM
W_GATE_ROW0 = W_UV_ROW0 + 2 * SGU_DIM
KV_K, KV_V = 0, KV_W
IKW_IW = IDX_DIM

N_ROW_TILES = 8
SAMPLE_TILE = 256
MOE_SUB = 128
DISPATCH_ROWS = 256
MOE_SUPER = 2048
MOE_TF = 256
MOE_WIDE = 4
COMBINE_TB = 128
IDX_PAGE_GROUP = 32
ATTN_PAGE_GROUP = 32


def _cp(sem, vmem=VMEM_LIMIT):
    return pltpu.CompilerParams(dimension_semantics=sem, vmem_limit_bytes=vmem)


def _round_up(x, m):
    return (x + m - 1) // m * m


def _ln_rows(z, g, b):
    mu = jnp.mean(z, axis=-1, keepdims=True)
    d = z - mu
    var = jnp.mean(d * d, axis=-1, keepdims=True)
    return d * lax.rsqrt(var + LN_EPS) * g + b


def _sigmoid(x):
    return 1.0 / (1.0 + jnp.exp(-x))


def _in_proj_body(off_ref, x_ref, w_ref, o_ref, wt_ref):
    @pl.when(pl.program_id(1) == 0)
    def _():
        wt_ref[...] = w_ref[0].T.astype(BF16)

    o_ref[...] = jnp.dot(x_ref[...], wt_ref[...], preferred_element_type=F32).astype(o_ref.dtype)


def _in_proj(x, w_t, layer, row_offsets, tm, tn):
    m, k = x.shape
    n_tiles = len(row_offsets)
    grid_spec = pltpu.PrefetchScalarGridSpec(
        num_scalar_prefetch=1,
        grid=(n_tiles, m // tm),
        in_specs=[pl.BlockSpec((tm, k), lambda j, i, off: (i, 0)),
                  pl.BlockSpec((pl.Element(1), pl.Element(tn), pl.Element(k)),
                               lambda j, i, off: (layer, pl.multiple_of(off[j], F32_SUBLANES), 0))],
        out_specs=pl.BlockSpec((tm, tn), lambda j, i, off: (i, j)),
        scratch_shapes=[pltpu.VMEM((k, tn), BF16)],
    )
    return pl.pallas_call(
        _in_proj_body,
        grid_spec=grid_spec,
        out_shape=jax.ShapeDtypeStruct((m, n_tiles * tn), BF16),
        compiler_params=_cp(("arbitrary", "arbitrary")),
        name="in_proj",
    )(jnp.asarray(row_offsets, I32), x, w_t)


def _kv_proj_body(x_ref, wkv_ref, wik_ref, kv_ref, ik_ref, k4_ref, v4_ref, wkv_t, wik_t):
    @pl.when(pl.program_id(0) == 0)
    def _():
        wkv_t[...] = wkv_ref[...].T.astype(BF16)
        wik_t[...] = wik_ref[...].T.astype(BF16)

    x = x_ref[...]
    tm = x.shape[0]
    kv = jnp.dot(x, wkv_t[...], preferred_element_type=F32)
    kv_ref[...] = kv
    ik_ref[...] = jnp.dot(x, wik_t[...], preferred_element_type=F32)
    for g in range(N_KV_HEADS):
        k4_ref[pl.ds(g, tm, stride=N_KV_HEADS), :] = kv[:, KV_K + g * HEAD_DIM:KV_K + (g + 1) * HEAD_DIM]
        v4_ref[pl.ds(g, tm, stride=N_KV_HEADS), :] = kv[:, KV_V + g * HEAD_DIM:KV_V + (g + 1) * HEAD_DIM]


def _kv_proj(x, w_t, layer, tm):
    m, k = x.shape
    assert W_KV_ROW0 % (2 * KV_W) == 0 and W_IK_ROW0 % LANES == 0
    row = lambda i: (i, 0)
    per_head = jax.ShapeDtypeStruct((m * N_KV_HEADS, HEAD_DIM), F32)
    return pl.pallas_call(
        _kv_proj_body,
        grid=(m // tm,),
        in_specs=[pl.BlockSpec((tm, k), row),
                  pl.BlockSpec((None, 2 * KV_W, k), lambda i: (layer, W_KV_ROW0 // (2 * KV_W), 0)),
                  pl.BlockSpec((None, LANES, k), lambda i: (layer, W_IK_ROW0 // LANES, 0))],
        out_specs=[pl.BlockSpec((tm, 2 * KV_W), row), pl.BlockSpec((tm, LANES), row),
                   pl.BlockSpec((tm * N_KV_HEADS, HEAD_DIM), row), pl.BlockSpec((tm * N_KV_HEADS, HEAD_DIM), row)],
        out_shape=[jax.ShapeDtypeStruct((m, 2 * KV_W), F32), jax.ShapeDtypeStruct((m, LANES), F32),
                   per_head, per_head],
        scratch_shapes=[pltpu.VMEM((k, 2 * KV_W), BF16), pltpu.VMEM((k, LANES), BF16)],
        compiler_params=_cp(("arbitrary",)),
        name="kv_proj",
    )(x, w_t, w_t)


def _row_halves(rows):
    half = rows // 2 // 16 * 16
    return (slice(0, half), slice(half, rows)) if half > 0 else (slice(0, rows),)


def _mm_ln_body(x_ref, w_ref, r_ref, g_ref, b_ref, of_ref, ob_ref, *, alpha):
    for rs in _row_halves(x_ref.shape[0]):
        z = jnp.dot(x_ref[rs, :], w_ref[...], preferred_element_type=F32) + alpha * r_ref[rs, :]
        y = _ln_rows(z, g_ref[...], b_ref[...])
        of_ref[rs, :] = y
        ob_ref[rs, :] = y.astype(BF16)


def _mm_ln(x, w, layer, resid, g, b, alpha, tm):
    m, k = x.shape
    n = w.shape[2]
    row = lambda i: (i, 0)
    return pl.pallas_call(
        functools.partial(_mm_ln_body, alpha=alpha),
        grid=(m // tm,),
        in_specs=[pl.BlockSpec((tm, k), row),
                  pl.BlockSpec((None, k, n), lambda i: (layer, 0, 0), pipeline_mode=pl.Buffered(1)),
                  pl.BlockSpec((tm, n), row),
                  pl.BlockSpec((1, n), lambda i: (0, 0)),
                  pl.BlockSpec((1, n), lambda i: (0, 0))],
        out_specs=[pl.BlockSpec((tm, n), row), pl.BlockSpec((tm, n), row)],
        out_shape=[jax.ShapeDtypeStruct((m, n), F32), jax.ShapeDtypeStruct((m, n), BF16)],
        compiler_params=_cp(("arbitrary",)),
        name="proj_ln",
    )(x, w, resid, g, b)


def _swiglu_up_body(x_ref, w1_ref, w3_ref, o_ref, w1b, w3b):
    @pl.when(pl.program_id(1) == 0)
    def _():
        w1b[...] = w1_ref[...].astype(BF16)
        w3b[...] = w3_ref[...].astype(BF16)

    for rs in _row_halves(x_ref.shape[0]):
        x = x_ref[rs, :]
        a = jnp.dot(x, w1b[...], preferred_element_type=F32)
        c = jnp.dot(x, w3b[...], preferred_element_type=F32)
        o_ref[rs, :] = (a * _sigmoid(a) * c).astype(o_ref.dtype)


def _swiglu_up(x, w1, w3, layer, tm, tf):
    m, k = x.shape
    f = w1.shape[2]
    return pl.pallas_call(
        _swiglu_up_body,
        grid=(f // tf, m // tm),
        in_specs=[pl.BlockSpec((tm, k), lambda j, i: (i, 0)),
                  pl.BlockSpec((None, k, tf), lambda j, i: (layer, 0, j)),
                  pl.BlockSpec((None, k, tf), lambda j, i: (layer, 0, j))],
        out_specs=pl.BlockSpec((tm, tf), lambda j, i: (i, j)),
        out_shape=jax.ShapeDtypeStruct((m, f), BF16),
        scratch_shapes=[pltpu.VMEM((k, tf), BF16), pltpu.VMEM((k, tf), BF16)],
        compiler_params=_cp(("arbitrary", "arbitrary")),
        name="ffn_up",
    )(x, w1, w3)


def _merge_body(ap_ref, cp_ref, sp_ref, as_ref, cs_ref, ss_ref, g_ref, wa_ref, wb_ref, wc_ref, o_ref, *, n_prompt):
    i = pl.program_id(0)
    d = o_ref.shape[1]

    def merge(a_ref, c_ref, s_ref):
        for rs in _row_halves(o_ref.shape[0]):
            pa = jnp.dot(a_ref[rs, :], wa_ref[...], preferred_element_type=F32)
            pb = jnp.dot(c_ref[rs, :], wb_ref[...], preferred_element_type=F32)
            pc = jnp.dot(s_ref[rs, :], wc_ref[...], preferred_element_type=F32)
            o = (_sigmoid(g_ref[rs, :d].astype(F32)) * pa + _sigmoid(g_ref[rs, d:2 * d].astype(F32)) * pb
                 + _sigmoid(g_ref[rs, 2 * d:].astype(F32)) * pc)
            o_ref[rs, :] = o.astype(o_ref.dtype)

    @pl.when(i < n_prompt)
    def _():
        merge(ap_ref, cp_ref, sp_ref)

    @pl.when(i >= n_prompt)
    def _():
        merge(as_ref, cs_ref, ss_ref)


def _merge(prompt, sample, ha, w_pa, w_pb, w_pc, layer, tm):
    m = ha.shape[0]
    d = w_pa.shape[2]
    n_prompt = prompt[0].shape[0] // tm
    assert prompt[0].shape[0] % tm == 0 and sample[0].shape[0] == tm and m == (n_prompt + 1) * tm
    assert HA_GATE % (N_BRANCH * d) == 0
    p_row = lambda i: (jnp.minimum(i, n_prompt - 1), 0)
    fixed = lambda i: (0, 0)
    return pl.pallas_call(
        functools.partial(_merge_body, n_prompt=n_prompt),
        grid=(n_prompt + 1,),
        in_specs=[pl.BlockSpec((tm, t.shape[1]), p_row) for t in prompt]
        + [pl.BlockSpec((tm, t.shape[1]), fixed) for t in sample]
        + [pl.BlockSpec((tm, N_BRANCH * d), lambda i: (i, HA_GATE // (N_BRANCH * d)))]
        + [pl.BlockSpec((None,) + w.shape[1:], lambda i: (layer, 0, 0), pipeline_mode=pl.Buffered(1))
           for w in (w_pa, w_pb, w_pc)],
        out_specs=pl.BlockSpec((tm, d), lambda i: (i, 0)),
        out_shape=jax.ShapeDtypeStruct((m, d), BF16),
        compiler_params=_cp(("arbitrary",)),
        name="branch_merge",
    )(*prompt, *sample, ha, w_pa, w_pb, w_pc)


def _sortable_key(score):
    bits = pltpu.bitcast(score, I32)
    return bits ^ ((bits >> 31) & jnp.int32(0x7FFFFFFF))


def _kth_largest_key(count_ge, rows, k):
    def step(it, t):
        cand = t + jnp.left_shift(jnp.int32(1), 31 - it)
        return jnp.where(count_ge(cand) >= k, cand, t)

    return lax.fori_loop(0, 32, step, jnp.full((rows, 1), INT_MIN, I32))


RADIX4_STEPS = 16
RADIX4_FIELD_BITS = 5
RADIX4_WIDE = 4096.0


def _kth_largest_key_radix4(count3, rows, k, n_valid):
    kf = float(k)

    def cond(state):
        it, _, _, pending = state
        return jnp.logical_and(it < RADIX4_STEPS, pending > 0.0)

    def body(state):
        it, t, cnt, _ = state
        one = jnp.left_shift(jnp.int32(1), 30 - 2 * it)
        c1 = t + one
        c2 = c1 + one
        c3 = c2 + one
        n1, n2, n3 = count3(c1, c2, c3)
        t = jnp.where(n3 >= kf, c3, jnp.where(n2 >= kf, c2, jnp.where(n1 >= kf, c1, t)))
        cnt = jnp.where(n3 >= kf, n3, jnp.where(n2 >= kf, n2, jnp.where(n1 >= kf, n1, cnt)))
        settled = jnp.logical_or(cnt == kf, n_valid <= kf)
        return it + 1, t, cnt, jnp.max(jnp.where(settled, 0.0, 1.0))

    init = (jnp.int32(0), jnp.full((rows, 1), INT_MIN, I32), jnp.full((rows, 1), -1.0, F32), jnp.float32(1.0))
    return lax.while_loop(cond, body, init)[1]


def _lane_tile(x, width):
    return x if width == LANES else jnp.concatenate([x] * (width // LANES), axis=1)


def _attn_prompt_body(q_ref, iq_ref, iw_ref, kv_ref, ik_ref, bias_ref, o_ref,
                      keys_ref, mb_ref, iwb_ref, m_ref, acc_ref, *, tq, k_sel):
    i = pl.program_id(1)
    nkb = i + 1
    tk = tq
    row = lax.broadcasted_iota(I32, (tq, tk), 0)
    col = lax.broadcasted_iota(I32, (tq, tk), 1)
    low_half = lax.broadcasted_iota(I32, (tk, LANES), 1) < IDX_DIM
    iw = iw_ref[...] * (IDX_DIM ** -0.5 * IDX_HEADS ** -0.5)
    for h in range(IDX_HEADS):
        iwb_ref[h] = jnp.broadcast_to(iw[:, IKW_IW + h:IKW_IW + h + 1], (tq, LANES))
    iq_pairs = [iq_ref[:, p * LANES:(p + 1) * LANES] for p in range(IDX_HEADS // 2)]

    def idx_block(kb, carry):
        ik_lo = jnp.where(low_half, ik_ref[pl.ds(pl.multiple_of(kb * tk, tk), tk), :], 0.0)
        ik_sides = (ik_lo.astype(BF16), pltpu.roll(ik_lo, IDX_DIM, axis=1).astype(BF16))
        acc = jnp.zeros((tq, tk), F32)
        for h in range(IDX_HEADS):
            s = lax.dot_general(iq_pairs[h // 2], ik_sides[h % 2], (((1,), (1,)), ((), ())),
                                preferred_element_type=F32)
            acc = acc + jnp.maximum(s, 0.0) * _lane_tile(iwb_ref[h], tk)
        causal = (col + kb * tk) <= (row + i * tq)
        keys_ref[kb] = jnp.where(causal, _sortable_key(acc), jnp.int32(INT_MIN))
        return carry

    def idx_pair(t, carry):
        idx_block(2 * t, carry)
        return idx_block(2 * t + 1, carry)

    lax.fori_loop(0, nkb // 2, idx_pair, 0)

    @pl.when(nkb % 2 == 1)
    def _():
        idx_block(nkb - 1, 0)
        keys_ref[nkb] = jnp.full((tq, tk), INT_MIN, I32)

    fb = RADIX4_FIELD_BITS
    field_mask = (1 << fb) - 1
    inc1, inc2, inc3 = 1, 1 + (1 << fb), 1 + (1 << fb) + (1 << 2 * fb)

    def count3(c1, c2, c3):
        b1, b2, b3 = (jnp.broadcast_to(c, (tq, LANES)) for c in (c1, c2, c3))

        def cb(pair, acc):
            for kb in (2 * pair, 2 * pair + 1):
                keys = keys_ref[kb]
                for t in range(tk // LANES):
                    kk = keys[:, t * LANES:(t + 1) * LANES]
                    acc = acc + jnp.where(kk >= b3, inc3, jnp.where(kk >= b2, inc2, jnp.where(kk >= b1, inc1, 0)))
            return acc

        acc = lax.fori_loop(0, (nkb + 1) // 2, cb, jnp.zeros((tq, LANES), I32))
        n3 = (acc >> (2 * fb)).astype(F32).sum(axis=-1, keepdims=True)
        low = ((acc & field_mask).astype(F32) + ((acc >> fb) & field_mask).astype(F32) * RADIX4_WIDE)
        low = low.sum(axis=-1, keepdims=True)
        n2 = jnp.floor(low * (1.0 / RADIX4_WIDE))
        return low - RADIX4_WIDE * n2, n2, n3

    n_valid = (lax.broadcasted_iota(I32, (tq, 1), 0) + (i * tq + 1)).astype(F32)
    thr = _kth_largest_key_radix4(count3, tq, k_sel, n_valid)
    thr = jnp.maximum(thr, jnp.int32(INT_MIN + 1))
    thr_b = _lane_tile(jnp.broadcast_to(thr, (tq, LANES)), tk)

    def mask_block(kb, carry):
        mb_ref[kb] = jnp.where(keys_ref[kb] >= thr_b, 0.0, MASK_NEG)
        return carry

    lax.fori_loop(0, nkb, mask_block, 0)

    scale = HEAD_DIM ** -0.5
    for g in range(N_KV_HEADS):
        qg = jnp.concatenate(
            [q_ref[:, (KV_REP * g + r) * HEAD_DIM:(KV_REP * g + r + 1) * HEAD_DIM] for r in range(KV_REP)], axis=0)
        m_ref[...] = jnp.full(m_ref.shape, MASK_NEG, F32)
        acc_ref[...] = jnp.zeros(acc_ref.shape, F32)

        def block(kb, bias_idx, g=g, qg=qg):
            r0 = pl.multiple_of(kb * tk, tk)
            kblk = kv_ref[pl.ds(r0, tk), KV_K + g * HEAD_DIM:KV_K + (g + 1) * HEAD_DIM].astype(BF16)
            vblk = kv_ref[pl.ds(r0, tk), KV_V + g * HEAD_DIM:KV_V + (g + 1) * HEAD_DIM].astype(BF16)
            s = lax.dot_general(qg, kblk, (((1,), (1,)), ((), ())), preferred_element_type=F32) * scale
            s = s.reshape(KV_REP, tq, tk) + mb_ref[kb][None]
            if bias_idx is not None:
                s = s + bias_ref[bias_idx, KV_REP * g:KV_REP * (g + 1)]
            s = s.reshape(KV_REP * tq, tk)
            m_old = m_ref[...]
            m_new = jnp.maximum(m_old, s.max(axis=-1, keepdims=True))
            p = jnp.exp(s - _lane_tile(m_new, tk))
            alpha = jnp.exp(m_old - m_new)
            v_ones = jnp.concatenate([vblk, jnp.ones((tk, LANES), BF16)], axis=1)
            acc_ref[...] = (_lane_tile(alpha, HEAD_DIM + LANES) * acc_ref[...]
                            + jnp.dot(p.astype(BF16), v_ones, preferred_element_type=F32))
            m_ref[...] = m_new

        n_far = jnp.maximum(i - 1, 0)

        def far_pair(t, carry):
            block(2 * t, None)
            block(2 * t + 1, None)
            return carry

        lax.fori_loop(0, n_far // 2, far_pair, 0)

        @pl.when(n_far % 2 == 1)
        def _():
            block(n_far - 1, None)

        @pl.when(i >= 1)
        def _():
            block(i - 1, 1)
            block(i, 0)

        @pl.when(i == 0)
        def _():
            block(i, 0)
        out = acc_ref[:, :HEAD_DIM] / acc_ref[:, HEAD_DIM:]
        for r in range(KV_REP):
            h = KV_REP * g + r
            o_ref[:, h * HEAD_DIM:(h + 1) * HEAD_DIM] = out[r * tq:(r + 1) * tq].astype(o_ref.dtype)


def _attn_prompt(ha, kv, ikw, bias_tiles, batch, seq, tq):
    nq = seq // tq
    k_sel = min(TOPK_MAX, seq // 4)
    assert seq // LANES < (1 << RADIX4_FIELD_BITS)
    body = functools.partial(_attn_prompt_body, tq=tq, k_sel=k_sel)
    return pl.pallas_call(
        body,
        grid=(batch, nq),
        in_specs=[pl.BlockSpec((tq, QKV_W), lambda b, i: (b * nq + i, HA_Q // QKV_W)),
                  pl.BlockSpec((tq, IQ_W), lambda b, i: (b * nq + i, HA_IQ // IQ_W)),
                  pl.BlockSpec((tq, LANES), lambda b, i: (b * nq + i, 0)),
                  pl.BlockSpec((seq, 2 * KV_W), lambda b, i: (b, 0), pipeline_mode=pl.Buffered(1)),
                  pl.BlockSpec((seq, LANES), lambda b, i: (b, 0), pipeline_mode=pl.Buffered(1)),
                  pl.BlockSpec((2, N_HEADS, tq, tq), lambda b, i: (0, 0, 0, 0), pipeline_mode=pl.Buffered(1))],
        out_specs=pl.BlockSpec((tq, QKV_W), lambda b, i: (b * nq + i, 0)),
        out_shape=jax.ShapeDtypeStruct((batch * seq, QKV_W), BF16),
        scratch_shapes=[pltpu.VMEM((nq + nq % 2, tq, tq), I32),
                        pltpu.VMEM((nq, tq, tq), F32),
                        pltpu.VMEM((IDX_HEADS, tq, LANES), F32),
                        pltpu.VMEM((KV_REP * tq, LANES), F32),
                        pltpu.VMEM((KV_REP * tq, HEAD_DIM + LANES), F32)],
        compiler_params=_cp(("arbitrary", "arbitrary")),
        name="attn_prompt",
    )(ha, ha, ikw, kv, ikw, bias_tiles)


def _page_spec(block, layer, group, slot):
    return pl.BlockSpec((None, None) + block, lambda b, p, pt: (layer, pt[b, p * group + slot], 0, 0))


def _idx_sample_body(pt_ref, iq_ref, iw_ref, *refs, n_tok, group):
    pages, o_ref = refs[:group], refs[group]
    ikt = jnp.concatenate([r[...].astype(BF16) for r in pages], axis=1)
    s = jnp.dot(iq_ref[0], ikt, preferred_element_type=F32)
    r = jnp.maximum(s, 0.0) * _lane_tile(iw_ref[0], group * PAGE_SIZE)
    tok_scores = r.reshape(n_tok, IDX_HEADS, group * PAGE_SIZE).sum(axis=1)
    for t in range(group):
        o_ref[0, t] = tok_scores[:, t * PAGE_SIZE:(t + 1) * PAGE_SIZE]


def _idx_sample(page_table, iq_rows, iw_rows, cache_ik_t, layer, n_tok, group):
    db, n_pages = page_table.shape
    rows = n_tok * IDX_HEADS
    body = functools.partial(_idx_sample_body, n_tok=n_tok, group=group)
    grid_spec = pltpu.PrefetchScalarGridSpec(
        num_scalar_prefetch=1,
        grid=(db, n_pages // group),
        in_specs=[pl.BlockSpec((1, rows, IDX_DIM), lambda b, p, pt: (b, 0, 0)),
                  pl.BlockSpec((1, rows, LANES), lambda b, p, pt: (b, 0, 0))]
        + [_page_spec((IDX_DIM, PAGE_SIZE), layer, group, t) for t in range(group)],
        out_specs=pl.BlockSpec((1, group, n_tok, PAGE_SIZE), lambda b, p, pt: (b, p, 0, 0)),
    )
    return pl.pallas_call(
        body,
        grid_spec=grid_spec,
        out_shape=jax.ShapeDtypeStruct((db, n_pages, n_tok, PAGE_SIZE), F32),
        compiler_params=_cp(("arbitrary", "arbitrary")),
        name="idx_sample",
    )(page_table, iq_rows, iw_rows, *([cache_ik_t] * group))


def _attn_sample_body(pt_ref, sc_ref, scn_ref, q_ref, kn_ref, vn_ref, bl_ref, bn_ref, *refs,
                      n_tok, n_pages, k_sel, group):
    kp_refs, vp_refs = refs[:group], refs[group:2 * group]
    o_ref, thr_ref, m_ref, l_ref, acc_ref = refs[2 * group:]
    p = pl.program_id(1)
    n_steps = n_pages // group
    grp = KV_REP * n_tok
    width = group * PAGE_SIZE
    scale = HEAD_DIM ** -0.5
    r_i = lax.broadcasted_iota(I32, (n_tok, LANES), 0)
    c_i = lax.broadcasted_iota(I32, (n_tok, LANES), 1)
    keys_new = jnp.where(c_i <= r_i, _sortable_key(scn_ref[0]), jnp.int32(INT_MIN))

    @pl.when(p == 0)
    def _():
        keys = _sortable_key(sc_ref[0])

        def count_ge(cand):
            cand_b = jnp.broadcast_to(cand, (n_tok, LANES))
            c = jnp.where(keys >= cand_b[None], 1.0, 0.0).sum(axis=0) + jnp.where(keys_new >= cand_b, 1.0, 0.0)
            return c.sum(axis=-1, keepdims=True)

        thr = _kth_largest_key(count_ge, n_tok, float(k_sel))
        thr_ref[...] = jnp.broadcast_to(jnp.maximum(thr, jnp.int32(INT_MIN + 1)), (n_tok, LANES))
        m_ref[...] = jnp.full(m_ref.shape, MASK_NEG, F32)
        l_ref[...] = jnp.zeros(l_ref.shape, F32)
        acc_ref[...] = jnp.zeros(acc_ref.shape, F32)

    thr_b = thr_ref[...]

    def update(g, s, vblk):
        sl = slice(g * grp, (g + 1) * grp)
        m_old = m_ref[sl, :]
        m_new = jnp.maximum(m_old, s.max(axis=-1, keepdims=True))
        pr = jnp.exp(s - _lane_tile(m_new, s.shape[1]))
        alpha = jnp.exp(m_old - m_new)
        l_ref[sl, :] = alpha * l_ref[sl, :] + pr.sum(axis=-1, keepdims=True)
        acc_ref[sl, :] = alpha * acc_ref[sl, :] + jnp.dot(pr.astype(BF16), vblk, preferred_element_type=F32)
        m_ref[sl, :] = m_new

    def group_rows(mask_tok):
        return jnp.concatenate([mask_tok] * KV_REP, axis=0)

    base = p * group
    mb = jnp.concatenate(
        [jnp.where(_sortable_key(sc_ref[0, base + t]) >= thr_b, 0.0, MASK_NEG) for t in range(group)], axis=1)
    mb_g = group_rows(mb)
    is_last = p == n_steps - 1
    zeros_head = jnp.zeros((grp, width - PAGE_SIZE), F32)
    for g in range(N_KV_HEADS):
        qg = q_ref[0, g * grp:(g + 1) * grp, :]
        kg = jnp.concatenate([r[pl.ds(g, PAGE_SIZE, stride=N_KV_HEADS), :].astype(BF16) for r in kp_refs], axis=0)
        vg = jnp.concatenate([r[pl.ds(g, PAGE_SIZE, stride=N_KV_HEADS), :].astype(BF16) for r in vp_refs], axis=0)
        s = lax.dot_general(qg, kg, (((1,), (1,)), ((), ())), preferred_element_type=F32) * scale
        near = jnp.where(is_last, bl_ref[g * grp:(g + 1) * grp, :], 0.0)
        s = s + mb_g + (near if group == 1 else jnp.concatenate([zeros_head, near], axis=1))
        update(g, s, vg)

    @pl.when(is_last)
    def _():
        mbn_g = group_rows(jnp.where(keys_new >= thr_b, 0.0, MASK_NEG))
        for g in range(N_KV_HEADS):
            qg = q_ref[0, g * grp:(g + 1) * grp, :]
            kblk = kn_ref[0, :, g * HEAD_DIM:(g + 1) * HEAD_DIM].astype(BF16)
            vblk = vn_ref[0, :, g * HEAD_DIM:(g + 1) * HEAD_DIM].astype(BF16)
            s = lax.dot_general(qg, kblk, (((1,), (1,)), ((), ())), preferred_element_type=F32) * scale
            s = s + mbn_g + bn_ref[g * grp:(g + 1) * grp, :]
            update(g, s, vblk)
        o_ref[0] = acc_ref[...] / l_ref[...]


def _attn_sample(page_table, scores, scores_new, q_rows, k_new, v_new, bias_last, bias_new,
                 cache_k, cache_v, layer, n_tok, group):
    db, n_pages = page_table.shape
    past = n_pages * PAGE_SIZE
    rows = N_HEADS * n_tok
    k_sel = min(TOPK_MAX, (past + n_tok) // 4)
    body = functools.partial(_attn_sample_body, n_tok=n_tok, n_pages=n_pages, k_sel=k_sel, group=group)
    page = (PAGE_SIZE * N_KV_HEADS, HEAD_DIM)
    grid_spec = pltpu.PrefetchScalarGridSpec(
        num_scalar_prefetch=1,
        grid=(db, n_pages // group),
        in_specs=[pl.BlockSpec((1, n_pages, n_tok, PAGE_SIZE), lambda b, p, pt: (b, 0, 0, 0)),
                  pl.BlockSpec((1, n_tok, LANES), lambda b, p, pt: (b, 0, 0)),
                  pl.BlockSpec((1, rows, HEAD_DIM), lambda b, p, pt: (b, 0, 0)),
                  pl.BlockSpec((1, LANES, KV_W), lambda b, p, pt: (b, 0, 0)),
                  pl.BlockSpec((1, LANES, KV_W), lambda b, p, pt: (b, 0, 0)),
                  pl.BlockSpec((rows, LANES), lambda b, p, pt: (0, 0)),
                  pl.BlockSpec((rows, LANES), lambda b, p, pt: (0, 0))]
        + [_page_spec(page, layer, group, t) for t in range(group)]
        + [_page_spec(page, layer, group, t) for t in range(group)],
        out_specs=pl.BlockSpec((1, rows, HEAD_DIM), lambda b, p, pt: (b, 0, 0)),
        scratch_shapes=[pltpu.VMEM((n_tok, LANES), I32),
                        pltpu.VMEM((rows, LANES), F32),
                        pltpu.VMEM((rows, LANES), F32),
                        pltpu.VMEM((rows, HEAD_DIM), F32)],
    )
    return pl.pallas_call(
        body,
        grid_spec=grid_spec,
        out_shape=jax.ShapeDtypeStruct((db, rows, HEAD_DIM), F32),
        compiler_params=_cp(("arbitrary", "arbitrary")),
        name="attn_sample",
    )(page_table, scores, scores_new, q_rows, k_new, v_new, bias_last, bias_new,
      *([cache_k] * group), *([cache_v] * group))


def _idx_new_body(iq_ref, iw_ref, ik_ref, o_ref, *, n_tok):
    for b in range(iq_ref.shape[0]):
        s = lax.dot_general(iq_ref[b], ik_ref[b].astype(BF16), (((1,), (1,)), ((), ())),
                            preferred_element_type=F32)
        r = jnp.maximum(s, 0.0) * iw_ref[b]
        o_ref[b] = r.reshape(n_tok, IDX_HEADS, LANES).sum(axis=1)


def _idx_new(iq_rows, iw_rows, ik_new, n_tok):
    db = iq_rows.shape[0]
    return pl.pallas_call(
        functools.partial(_idx_new_body, n_tok=n_tok),
        out_shape=jax.ShapeDtypeStruct((db, n_tok, LANES), F32),
        name="idx_new",
    )(iq_rows, iw_rows, ik_new)


HALO = CONV_WIDTH - 1
HALO_PAD = 32
CONV_ROW_BLOCK = 128


CONV_CHUNKS = CONV_DIM // LANES


def _conv_prompt_body(glu_ref, cw_ref, cb_ref, g_ref, b_ref, o_ref, st_ref, hp_ref, c_ref, sh_ref, *, ts, ns):
    s_idx = pl.program_id(1)

    @pl.when(s_idx == 0)
    def _():
        hp_ref[:, 0:HALO_PAD, :] = jnp.zeros((CONV_CHUNKS, HALO_PAD, LANES), F32)

    for c in range(CONV_CHUNKS):
        cs = slice(c * LANES, (c + 1) * LANES)
        a = glu_ref[:, cs].astype(F32)
        gt = glu_ref[:, CONV_DIM + c * LANES:CONV_DIM + (c + 1) * LANES].astype(F32)
        hp_ref[c, HALO_PAD:HALO_PAD + ts, :] = a * _sigmoid(gt)
    off = HALO_PAD - HALO
    rows = min(ts, CONV_ROW_BLOCK)

    def chunk(c, carry):
        for r0 in range(0, ts, rows):
            acc = jnp.zeros((rows // F32_SUBLANES, F32_SUBLANES, LANES), F32)
            for shift in range(F32_SUBLANES):
                taps = range(shift, CONV_WIDTH, F32_SUBLANES)
                span = rows + taps[-1] - shift
                sh_ref[0:span, :] = hp_ref[c, off + r0 + shift:off + r0 + shift + span, :]
                for w in taps:
                    x = sh_ref[w - shift:w - shift + rows, :].reshape(rows // F32_SUBLANES, F32_SUBLANES, LANES)
                    acc = acc + x * cw_ref[c, w][None]
            c_ref[c, r0:r0 + rows, :] = acc.reshape(rows, LANES)
        return carry

    lax.fori_loop(0, CONV_CHUNKS, chunk, 0)

    conv = [c_ref[c] + cb_ref[:, c * LANES:(c + 1) * LANES] for c in range(CONV_CHUNKS)]
    mu = sum(v.sum(axis=-1, keepdims=True) for v in conv) / CONV_DIM
    var = sum(((v - mu) * (v - mu)).sum(axis=-1, keepdims=True) for v in conv) / CONV_DIM
    inv = lax.rsqrt(var + LN_EPS)
    for c in range(CONV_CHUNKS):
        cs = slice(c * LANES, (c + 1) * LANES)
        y = (conv[c] - mu) * inv * g_ref[:, cs] + b_ref[:, cs]
        o_ref[:, cs] = (y * _sigmoid(y)).astype(o_ref.dtype)

    @pl.when(s_idx == ns - 1)
    def _():
        for c in range(CONV_CHUNKS):
            st_ref[0, :, c * LANES:(c + 1) * LANES] = hp_ref[c, ts + off:ts + HALO_PAD, :]

    hp_ref[:, off:HALO_PAD, :] = hp_ref[:, ts + off:ts + HALO_PAD, :]


def _conv_prompt(ha, conv_w, conv_b, ln_g, ln_b, batch, seq, ts):
    ns = seq // ts
    body = functools.partial(_conv_prompt_body, ts=ts, ns=ns)
    vec = pl.BlockSpec((1, CONV_DIM), lambda b, s: (0, 0))
    return pl.pallas_call(
        body,
        grid=(batch, ns),
        in_specs=[pl.BlockSpec((ts, 2 * CONV_DIM), lambda b, s: (b * ns + s, HA_GLU // (2 * CONV_DIM))),
                  pl.BlockSpec((CONV_CHUNKS, CONV_WIDTH, F32_SUBLANES, LANES), lambda b, s: (0, 0, 0, 0)),
                  vec, vec, vec],
        out_specs=[pl.BlockSpec((ts, CONV_DIM), lambda b, s: (b * ns + s, 0)),
                   pl.BlockSpec((1, HALO, CONV_DIM), lambda b, s: (b, 0, 0))],
        out_shape=[jax.ShapeDtypeStruct((batch * seq, CONV_DIM), BF16),
                   jax.ShapeDtypeStruct((batch, HALO, CONV_DIM), F32)],
        scratch_shapes=[pltpu.VMEM((CONV_CHUNKS, HALO_PAD + ts, LANES), F32),
                        pltpu.VMEM((CONV_CHUNKS, ts, LANES), F32),
                        pltpu.VMEM((HALO_PAD + min(ts, CONV_ROW_BLOCK), LANES), F32)],
        compiler_params=_cp(("arbitrary", "arbitrary")),
        name="conv_prompt",
    )(ha, conv_w, conv_b, ln_g, ln_b)


def _sgu_prompt_body(uv_ref, g_ref, b_ref, w_ref, bs_ref, o_ref, *, ts):
    u = uv_ref[:, :SGU_DIM].astype(F32)
    vn = _ln_rows(uv_ref[:, SGU_DIM:].astype(F32), g_ref[...], b_ref[...]).astype(BF16)
    r_i = lax.broadcasted_iota(I32, (CHUNK, CHUNK), 0)
    c_i = lax.broadcasted_iota(I32, (CHUNK, CHUNK), 1)
    gd = SGU_DIM // SGU_GROUPS
    for g in range(SGU_GROUPS):
        wm = jnp.where(c_i <= r_i, w_ref[g], 0.0).astype(BF16)
        bias = bs_ref[:, g:g + 1]
        for c in range(ts // CHUNK):
            rs = slice(c * CHUNK, (c + 1) * CHUNK)
            gs = slice(g * gd, (g + 1) * gd)
            mixed = jnp.dot(wm, vn[rs, gs], preferred_element_type=F32) + bias
            o_ref[rs, gs] = (u[rs, gs] * mixed).astype(o_ref.dtype)


def _sgu_prompt(ha, ln_g, ln_b, w_s, b_s_t, batch, seq, ts):
    ns = seq // ts
    vec = pl.BlockSpec((1, SGU_DIM), lambda b, s: (0, 0))
    return pl.pallas_call(
        functools.partial(_sgu_prompt_body, ts=ts),
        grid=(batch, ns),
        in_specs=[pl.BlockSpec((ts, 2 * SGU_DIM), lambda b, s: (b * ns + s, HA_UV // (2 * SGU_DIM))),
                  vec, vec,
                  pl.BlockSpec((SGU_GROUPS, CHUNK, CHUNK), lambda b, s: (0, 0, 0)),
                  pl.BlockSpec((CHUNK, SGU_GROUPS), lambda b, s: (0, 0))],
        out_specs=pl.BlockSpec((ts, SGU_DIM), lambda b, s: (b * ns + s, 0)),
        out_shape=jax.ShapeDtypeStruct((batch * seq, SGU_DIM), BF16),
        compiler_params=_cp(("arbitrary", "arbitrary")),
        name="sgu_prompt",
    )(ha, ln_g, ln_b, w_s, b_s_t)


def _mix_sample_body(glu_ref, uv_ref, st_ref, cw_ref, cb_ref, cg_ref, cbt_ref, sg_ref, sb_ref, wv_ref, bv_ref,
                     co_ref, nst_ref, so_ref, vn_ref, hp_ref, *, db, n_tok):
    cw = cw_ref[...]
    for b in range(db):
        rs = slice(b * n_tok, (b + 1) * n_tok)
        a = glu_ref[rs, :CONV_DIM].astype(F32)
        gt = glu_ref[rs, CONV_DIM:].astype(F32)
        hp_ref[0:HALO, :] = st_ref[b]
        hp_ref[HALO:HALO + n_tok, :] = a * _sigmoid(gt)
        rows = [jnp.sum(hp_ref[t:t + CONV_WIDTH, :] * cw, axis=0, keepdims=True) for t in range(n_tok)]
        c = jnp.concatenate(rows, axis=0) + cb_ref[...]
        y = _ln_rows(c, cg_ref[...], cbt_ref[...])
        co_ref[rs, :] = y * _sigmoid(y)
        nst_ref[b] = hp_ref[n_tok:n_tok + HALO, :]
        u = uv_ref[rs, :SGU_DIM].astype(F32)
        vn = _ln_rows(uv_ref[rs, SGU_DIM:].astype(F32), sg_ref[...], sb_ref[...])
        vn_ref[rs, :] = vn
        mixed = []
        for t in range(n_tok):
            acc = bv_ref[t:t + 1, :]
            for s in range(t + 1):
                acc = acc + wv_ref[t, s:s + 1, :] * vn[s:s + 1, :]
            mixed.append(acc)
        so_ref[rs, :] = u * jnp.concatenate(mixed, axis=0)


def _mix_sample(glu, uv, state, conv_w, conv_b, cln_g, cln_b, sln_g, sln_b, wv, bv, db, n_tok):
    rows = db * n_tok
    return pl.pallas_call(
        functools.partial(_mix_sample_body, db=db, n_tok=n_tok),
        out_shape=[jax.ShapeDtypeStruct((rows, CONV_DIM), F32),
                   jax.ShapeDtypeStruct((db, HALO, CONV_DIM), F32),
                   jax.ShapeDtypeStruct((rows, SGU_DIM), F32),
                   jax.ShapeDtypeStruct((rows, SGU_DIM), F32)],
        scratch_shapes=[pltpu.VMEM((_round_up(HALO + n_tok, 8), CONV_DIM), F32)],
        name="mix_sample",
    )(glu, uv, state, conv_w, conv_b, cln_g, cln_b, sln_g, sln_b, wv, bv)


def _router_body(x_ref, w_ref, b_ref, ti_ref, tg_ref, *, n_exp):
    logits = jnp.dot(x_ref[...], w_ref[...], preferred_element_type=F32,
                     precision=lax.Precision.HIGHEST) + b_ref[...]
    lane = lax.broadcasted_iota(I32, logits.shape, 1)
    lane_f = lane.astype(F32)
    neg = -jnp.inf
    lg = jnp.where(lane < n_exp, logits, neg)
    v1 = lg.max(axis=-1, keepdims=True)
    i1 = jnp.where(lg == v1, lane_f, float(LANES)).min(axis=-1, keepdims=True)
    lg2 = jnp.where(lane_f == i1, neg, lg)
    v2 = lg2.max(axis=-1, keepdims=True)
    i2 = jnp.where(lg2 == v2, lane_f, float(LANES)).min(axis=-1, keepdims=True)
    e = jnp.exp(v2 - v1)
    den = 1.0 + e
    ti_ref[...] = jnp.where(lane == 0, i1, jnp.where(lane == 1, i2, 0.0)).astype(I32)
    tg_ref[...] = jnp.where(lane == 0, 1.0 / den, jnp.where(lane == 1, e / den, 0.0))


def _router(x, w_pad, b_pad, n_exp, tm):
    m, d = x.shape
    return pl.pallas_call(
        functools.partial(_router_body, n_exp=n_exp),
        grid=(m // tm,),
        in_specs=[pl.BlockSpec((tm, d), lambda i: (i, 0)),
                  pl.BlockSpec((d, LANES), lambda i: (0, 0)),
                  pl.BlockSpec((1, LANES), lambda i: (0, 0))],
        out_specs=[pl.BlockSpec((tm, LANES), lambda i: (i, 0)),
                   pl.BlockSpec((tm, LANES), lambda i: (i, 0))],
        out_shape=[jax.ShapeDtypeStruct((m, LANES), I32), jax.ShapeDtypeStruct((m, LANES), F32)],
        compiler_params=_cp(("arbitrary",)),
        name="moe_router",
    )(x, w_pad, b_pad)


def _row_copy(src_hbm, dst, src_row, dst_row, sem):
    return pltpu.make_async_copy(src_hbm.at[pl.ds(src_row, 1)], dst.at[pl.ds(dst_row, 1)], sem)


ROW_DMA_UNROLL = 8
ROW_DMA_SLOTS = 2


def _prefetch_chunks(issue, wait):
    c = pl.program_id(0)
    ahead = ROW_DMA_SLOTS - 1

    @pl.when(c == 0)
    def _():
        for first in range(ahead):
            @pl.when(first < pl.num_programs(0))
            def _():
                issue(first, first)

    @pl.when(c + ahead < pl.num_programs(0))
    def _():
        issue(c + ahead, (c + ahead) % ROW_DMA_SLOTS)

    slot = c % ROW_DMA_SLOTS
    wait(slot)
    return slot


def _dispatch_body(src_ref, x_hbm, o_ref, buf, sem, *, rows):
    def issue(chunk, slot):
        def start(u, carry):
            for v in range(ROW_DMA_UNROLL):
                r = v * (rows // ROW_DMA_UNROLL) + u
                _row_copy(x_hbm, buf.at[slot], src_ref[chunk * rows + r], r, sem.at[slot]).start()
            return carry

        lax.fori_loop(0, rows // ROW_DMA_UNROLL, start, 0)

    def wait(slot):
        def w(r, carry):
            _row_copy(x_hbm, buf.at[slot], 0, r, sem.at[slot]).wait()
            return carry

        lax.fori_loop(0, rows, w, 0)

    slot = _prefetch_chunks(issue, wait)
    o_ref[...] = buf[slot].astype(o_ref.dtype)


def _dispatch(src, x, n_slots, rows):
    d = x.shape[1]
    grid_spec = pltpu.PrefetchScalarGridSpec(
        num_scalar_prefetch=1,
        grid=(n_slots // rows,),
        in_specs=[pl.BlockSpec(memory_space=pl.ANY)],
        out_specs=pl.BlockSpec((rows, d), lambda i, s: (i, 0)),
        scratch_shapes=[pltpu.VMEM((ROW_DMA_SLOTS, rows, d), x.dtype), pltpu.SemaphoreType.DMA((ROW_DMA_SLOTS,))],
    )
    assert rows % ROW_DMA_UNROLL == 0
    return pl.pallas_call(
        functools.partial(_dispatch_body, rows=rows),
        grid_spec=grid_spec,
        out_shape=jax.ShapeDtypeStruct((n_slots, d), BF16),
        compiler_params=_cp(("arbitrary",)),
        name="moe_dispatch",
    )(src, x)


def _tile_copy(src, dst, sem):
    return pltpu.make_async_copy(src, dst, sem)


def _moe_ffn_body(exp_ref, row0_ref, nsub_ref, jmap_ref, nzero_ref, xs_hbm, w1_ref, w3_ref, w2_ref, ys_hbm,
                  xbuf, acc, w1b, w3b, w2b, sem, *, nj, sub):
    w = pl.program_id(0)
    j = pl.program_id(1)
    nsub = nsub_ref[w]
    row0 = row0_ref[w]
    nzero = nzero_ref[w]

    def zero_copy(t):
        r = pl.multiple_of(row0 + t * sub, sub)
        return _tile_copy(acc.at[pl.ds(0, sub)], ys_hbm.at[pl.ds(r, sub)], sem)

    @pl.when(jnp.logical_and(j == 0, nzero > 0))
    def _():
        acc[0:sub, :] = jnp.zeros((sub, acc.shape[1]), F32)

        def zs(t, carry):
            zero_copy(t).start()
            return carry

        def zw(t, carry):
            zero_copy(t).wait()
            return carry

        lax.fori_loop(0, nzero, zs, 0)
        lax.fori_loop(0, nzero, zw, 0)

    def in_copy(t):
        r = pl.multiple_of(t * sub, sub)
        return _tile_copy(xs_hbm.at[pl.ds(pl.multiple_of(row0 + r, sub), sub)], xbuf.at[pl.ds(r, sub)], sem)

    def out_copy(t):
        r = pl.multiple_of(t * sub, sub)
        return _tile_copy(acc.at[pl.ds(r, sub)], ys_hbm.at[pl.ds(pl.multiple_of(row0 + r, sub), sub)], sem)

    def for_tiles(fn):
        def it(t, carry):
            fn(t)
            return carry
        lax.fori_loop(0, nsub, it, 0)

    def zero_tile(t):
        acc[pl.ds(pl.multiple_of(t * sub, sub), sub), :] = jnp.zeros((sub, acc.shape[1]), F32)

    @pl.when(j == 0)
    def _():
        for_tiles(lambda t: in_copy(t).start())
        for_tiles(zero_tile)
        for_tiles(lambda t: in_copy(t).wait())

    def tile(t, n, w1v, w3v, w2v):
        r = pl.multiple_of(t * sub, sub)
        x = xbuf[pl.ds(r, n * sub), :]
        a = jnp.dot(x, w1v, preferred_element_type=F32)
        c = jnp.dot(x, w3v, preferred_element_type=F32)
        h = (a * _sigmoid(a) * c).astype(BF16)
        acc[pl.ds(r, n * sub), :] += jnp.dot(h, w2v, preferred_element_type=F32)

    def cast_weights():
        return w1_ref[...].astype(BF16), w3_ref[...].astype(BF16), w2_ref[...].astype(BF16)

    per_trip = 2 * MOE_WIDE
    n_trips = nsub // per_trip

    def trip(t, wv):
        tile(per_trip * t, MOE_WIDE, *wv)
        tile(per_trip * t + MOE_WIDE, MOE_WIDE, *wv)

    @pl.when(n_trips >= 1)
    def _():
        wv = cast_weights()
        w1b[...], w3b[...], w2b[...] = wv
        trip(0, wv)

    @pl.when(jnp.logical_and(n_trips == 0, nsub > 0))
    def _():
        w1b[...], w3b[...], w2b[...] = cast_weights()

    def later_trip(t, carry):
        trip(t, (w1b[...], w3b[...], w2b[...]))
        return carry

    lax.fori_loop(1, n_trips, later_trip, 0)

    def rest(t, carry):
        tile(t, 1, w1b[...], w3b[...], w2b[...])
        return carry

    lax.fori_loop(n_trips * per_trip, nsub, rest, 0)

    @pl.when(j == nj - 1)
    def _():
        for_tiles(lambda t: out_copy(t).start())
        for_tiles(lambda t: out_copy(t).wait())


def _moe_ffn(item_exp, item_row0, item_nsub, item_jlast, item_nzero, xs, w1, w3, w2, layer, n_items, super_rows,
             sub, tf):
    n_slots, d = xs.shape
    f = w1.shape[3]
    nj = f // tf

    def jj(w, j, jl):
        return jnp.where(jl[w] > 0, j, nj - 1)

    grid_spec = pltpu.PrefetchScalarGridSpec(
        num_scalar_prefetch=5,
        grid=(n_items, nj),
        in_specs=[pl.BlockSpec(memory_space=pl.ANY),
                  pl.BlockSpec((None, None, d, tf), lambda w, j, ex, r0, ns, jl, nz: (layer, ex[w], 0, jj(w, j, jl))),
                  pl.BlockSpec((None, None, d, tf), lambda w, j, ex, r0, ns, jl, nz: (layer, ex[w], 0, jj(w, j, jl))),
                  pl.BlockSpec((None, None, tf, d), lambda w, j, ex, r0, ns, jl, nz: (layer, ex[w], jj(w, j, jl), 0))],
        out_specs=pl.BlockSpec(memory_space=pl.ANY),
        scratch_shapes=[pltpu.VMEM((super_rows, d), BF16),
                        pltpu.VMEM((super_rows, d), F32),
                        pltpu.VMEM((d, tf), BF16),
                        pltpu.VMEM((d, tf), BF16),
                        pltpu.VMEM((tf, d), BF16),
                        pltpu.SemaphoreType.DMA(())],
    )
    return pl.pallas_call(
        functools.partial(_moe_ffn_body, nj=nj, sub=sub),
        grid_spec=grid_spec,
        out_shape=jax.ShapeDtypeStruct((n_slots, d), F32),
        compiler_params=_cp(("arbitrary", "arbitrary")),
        name="moe_ffn",
    )(item_exp, item_row0, item_nsub, item_jlast, item_nzero, xs, w1, w3, w2)


def _combine_body(pos_ref, ys_hbm, tg_ref, x_ref, g_ref, b_ref, *refs, tb, alpha, head_chunks):
    out_refs, (buf, sem) = refs[:-2], refs[-2:]

    def issue(chunk, slot):
        def start(u, carry):
            for v in range(ROW_DMA_UNROLL // TOP_K_EXPERTS):
                r = u * (ROW_DMA_UNROLL // TOP_K_EXPERTS) + v
                for k in range(TOP_K_EXPERTS):
                    p = pos_ref[TOP_K_EXPERTS * (chunk * tb + r) + k]
                    _row_copy(ys_hbm, buf.at[slot, k], p, r, sem.at[slot]).start()
            return carry

        lax.fori_loop(0, tb * TOP_K_EXPERTS // ROW_DMA_UNROLL, start, 0)

    def wait(slot):
        def w(r, carry):
            for k in range(TOP_K_EXPERTS):
                _row_copy(ys_hbm, buf.at[slot, k], 0, r, sem.at[slot]).wait()
            return carry

        lax.fori_loop(0, tb, w, 0)

    slot = _prefetch_chunks(issue, wait)
    f = tg_ref[:, 0:1] * buf[slot, 0] + tg_ref[:, 1:2] * buf[slot, 1]
    y = _ln_rows(alpha * x_ref[...] + f, g_ref[...], b_ref[...])
    if head_chunks is None:
        of_ref, ob_ref = out_refs
        of_ref[...] = y
        ob_ref[...] = y.astype(BF16)
    else:
        head_ref, tail_ref = out_refs
        c = pl.program_id(0)

        @pl.when(c < head_chunks)
        def _():
            head_ref[...] = y

        @pl.when(c >= head_chunks)
        def _():
            tail_ref[...] = y


def _combine_ln(pos, ys, tg, x, g, b, alpha, tb, split_rows=None):
    m, d = x.shape
    row = lambda i, p: (i, 0)
    if split_rows is None:
        head_chunks = None
        out_specs = [pl.BlockSpec((tb, d), row), pl.BlockSpec((tb, d), row)]
        out_shape = [jax.ShapeDtypeStruct((m, d), F32), jax.ShapeDtypeStruct((m, d), BF16)]
    else:
        assert split_rows % tb == 0 and 0 < split_rows < m
        head_chunks = split_rows // tb
        out_specs = [pl.BlockSpec((tb, d), lambda i, p: (jnp.minimum(i, head_chunks - 1), 0)),
                     pl.BlockSpec((tb, d), lambda i, p: (jnp.maximum(i - head_chunks, 0), 0))]
        out_shape = [jax.ShapeDtypeStruct((split_rows, d), F32), jax.ShapeDtypeStruct((m - split_rows, d), F32)]
    grid_spec = pltpu.PrefetchScalarGridSpec(
        num_scalar_prefetch=1,
        grid=(m // tb,),
        in_specs=[pl.BlockSpec(memory_space=pl.ANY),
                  pl.BlockSpec((tb, LANES), row),
                  pl.BlockSpec((tb, d), row),
                  pl.BlockSpec((1, d), lambda i, p: (0, 0)),
                  pl.BlockSpec((1, d), lambda i, p: (0, 0))],
        out_specs=out_specs,
        scratch_shapes=[pltpu.VMEM((ROW_DMA_SLOTS, TOP_K_EXPERTS, tb, d), F32),
                        pltpu.SemaphoreType.DMA((ROW_DMA_SLOTS,))],
    )
    assert tb * TOP_K_EXPERTS % ROW_DMA_UNROLL == 0
    return pl.pallas_call(
        functools.partial(_combine_body, tb=tb, alpha=alpha, head_chunks=head_chunks),
        grid_spec=grid_spec,
        out_shape=out_shape,
        compiler_params=_cp(("arbitrary",)),
        name="moe_combine_ln",
    )(pos, ys, tg, x, g, b)


def _moe_plan(top_i, n_real, n_exp, n_slots, n_items, super_rows, sub):
    a = top_i.shape[0] * TOP_K_EXPERTS
    e_flat = top_i.reshape(-1)
    tok = jnp.arange(a, dtype=I32) // TOP_K_EXPERTS
    real = tok < n_real
    onehot = jnp.logical_and(e_flat[:, None] == jnp.arange(n_exp, dtype=I32)[None, :], real[:, None]).astype(I32)
    csum = jnp.cumsum(onehot, axis=0)
    rank = jnp.take_along_axis(csum, e_flat[:, None], axis=1)[:, 0] - 1
    counts = csum[-1]
    padded = (counts + sub - 1) // sub * sub
    gstart = jnp.cumsum(padded) - padded
    pos = jnp.where(real, gstart[e_flat] + rank, jnp.arange(a, dtype=I32) % n_slots).astype(I32)
    src = (jnp.arange(n_slots, dtype=I32) % n_real).at[jnp.where(real, pos, n_slots)].set(tok, mode="drop")
    n_super = (padded + super_rows - 1) // super_rows
    iend = jnp.cumsum(n_super)
    istart = iend - n_super
    total = iend[-1]
    wi = jnp.arange(n_items, dtype=I32)
    valid = wi < total
    e_w = jnp.minimum(jnp.searchsorted(iend, wi, side="right"), n_exp - 1).astype(I32)
    last_e = e_w[jnp.maximum(total - 1, 0)]
    e_w = jnp.where(valid, e_w, last_e)
    k_w = wi - istart[e_w]
    subs_e = padded[e_w] // sub
    items_e = jnp.maximum(n_super[e_w], 1)
    base, extra = subs_e // items_e, subs_e % items_e
    nsub = jnp.where(valid, base + (k_w < extra).astype(I32), 0).astype(I32)
    first_sub = k_w * base + jnp.minimum(k_w, extra)
    z0 = jnp.sum(padded) + (wi - total) * super_rows
    nzero = jnp.where(valid, 0, jnp.clip((n_slots - z0) // sub, 0, super_rows // sub)).astype(I32)
    row0 = jnp.where(valid, gstart[e_w] + first_sub * sub, jnp.where(nzero > 0, z0, 0)).astype(I32)
    return pos, src, e_w, row0, nsub, valid.astype(I32), nzero


def _rel_bucket(dist):
    n = jnp.maximum(dist, 0)
    max_exact = REL_BUCKETS // 2
    nf = jnp.maximum(n, 1).astype(F32)
    large = max_exact + (jnp.log(nf / max_exact) / math.log(REL_MAX_DIST / max_exact)
                         * (REL_BUCKETS - max_exact)).astype(I32)
    large = jnp.minimum(large, REL_BUCKETS - 1)
    return jnp.where(n < max_exact, n, large)


def _shifted_bias(rel_bias, dist):
    t = rel_bias[_rel_bucket(dist)] - rel_bias[REL_BUCKETS - 1]
    t = jnp.where((dist >= 0)[..., None], t, 0.0)
    return jnp.moveaxis(t, -1, 0)


def _pick_tile(m_rows, target):
    return max(t for t in range(16, target + 1, 16) if m_rows % t == 0)


def _pad_rows(x, rows):
    return jnp.pad(x, ((0, rows - x.shape[0]), (0, 0)))


def kernel(x_prompt, x_sample, cache_k, cache_v, cache_ik, state_conv, page_table, w_in, conv_w, conv_b,
           conv_ln_g, conv_ln_b, sgu_ln_g, sgu_ln_b, sgu_w, sgu_b, w_pa, w_pb, w_pc, w_out, ln1_g, ln1_b,
           ln2_g, ln2_b, rel_bias, ffn_w1, ffn_w3, ffn_w2, moe_router, moe_router_b, moe_w1, moe_w3, moe_w2):
    batch, seq, d_model = x_prompt.shape
    db, n_tok, _ = x_sample.shape
    depth = w_in.shape[0]
    n_pool = cache_k.shape[1]
    n_exp = moe_router.shape[2]
    mp, ms = batch * seq, db * n_tok
    assert ms <= SAMPLE_TILE and mp % SAMPLE_TILE == 0
    m_all = mp + SAMPLE_TILE
    tm = m_all // N_ROW_TILES
    assert tm * N_ROW_TILES == m_all and tm % 16 == 0
    alpha = (2 * depth) ** 0.25
    tq = min(256, seq)

    xf = jnp.concatenate([x_prompt.reshape(mp, d_model), x_sample.reshape(ms, d_model),
                          jnp.zeros((m_all - mp - ms, d_model), F32)], axis=0)
    xb = xf.astype(BF16)
    w_in_t = jnp.swapaxes(w_in, 1, 2)
    ha_rows = ([W_Q_ROW0 + HA_TILE * t for t in range(QKV_W // HA_TILE)]
               + [W_GLU_ROW0 + HA_TILE * t for t in range(2 * CONV_DIM // HA_TILE)]
               + [W_UV_ROW0 + HA_TILE * t for t in range(2 * SGU_DIM // HA_TILE)]
               + [W_GATE_ROW0 + HA_TILE * t for t in range(N_BRANCH * d_model // HA_TILE)]
               + [W_IQ_ROW0 + HA_TILE * t for t in range(IQ_W // HA_TILE)])
    w_pa_b, w_pb_b, w_pc_b, w_out_b = (w.astype(BF16) for w in (w_pa, w_pb, w_pc, w_out))
    ffn_w2_b = ffn_w2.astype(BF16)

    kk = jnp.arange(2 * tq, dtype=I32)
    d_wrap = jnp.where(kk < tq, -kk, 2 * tq - kk)

    def toeplitz(offset):
        v = _shifted_bias(rel_bias, d_wrap + offset)
        return jnp.tile(v, (1, tq))[:, :tq * (2 * tq - 1)].reshape(N_HEADS, tq, 2 * tq - 1)[:, :, :tq]

    bias_tiles = jnp.stack([toeplitz(0), toeplitz(tq)])
    tok = jnp.arange(n_tok, dtype=I32)
    lane = jnp.arange(LANES, dtype=I32)
    d_last = PAGE_SIZE + tok[:, None] - lane[None, :]
    bias_last = _shifted_bias(rel_bias, d_last).reshape(N_HEADS * n_tok, LANES)
    bias_new = _shifted_bias(rel_bias, tok[:, None] - lane[None, :]).reshape(N_HEADS * n_tok, LANES)

    ck = cache_k.reshape(depth, n_pool, PAGE_SIZE * N_KV_HEADS, HEAD_DIM)
    cv = cache_v.reshape(depth, n_pool, PAGE_SIZE * N_KV_HEADS, HEAD_DIM)
    cik_t = jnp.swapaxes(cache_ik, 2, 3)
    n_pages = page_table.shape[1]
    idx_group = math.gcd(n_pages, IDX_PAGE_GROUP)
    attn_group = math.gcd(n_pages, ATTN_PAGE_GROUP)
    n_slots = _round_up(TOP_K_EXPERTS * (mp + ms) + n_exp * (MOE_SUB - 1), DISPATCH_ROWS)
    n_items = -(-n_slots // MOE_SUPER) + n_exp

    outs = {k: [] for k in ("k", "v", "ik", "conv_p", "conv_s", "sgu_s")}
    for l in range(depth):
        ha = _in_proj(xb, w_in_t, l, ha_rows, tm, HA_TILE)
        kv, ikw, k4, v4 = _kv_proj(xb, w_in_t, l, tm)

        a_p = _attn_prompt(ha, kv, ikw, bias_tiles, batch, seq, tq)
        conv_w_rep = jnp.broadcast_to(conv_w[l].reshape(CONV_WIDTH, CONV_CHUNKS, 1, LANES).transpose(1, 0, 2, 3),
                                      (CONV_CHUNKS, CONV_WIDTH, F32_SUBLANES, LANES))
        c_p, conv_state_p = _conv_prompt(ha, conv_w_rep, conv_b[l][None], conv_ln_g[l][None], conv_ln_b[l][None],
                                         batch, seq, min(256, seq))
        s_p = _sgu_prompt(ha, sgu_ln_g[l][None], sgu_ln_b[l][None], sgu_w[l], sgu_b[l].T, batch, seq,
                          min(512, seq))

        has, kvs, ikws = ha[mp:mp + ms], kv[mp:mp + ms], ikw[mp:mp + ms]
        iq_rows = has[:, HA_IQ:HA_IQ + IQ_W].reshape(db, n_tok * IDX_HEADS, IDX_DIM)
        iw_rows = (ikws[:, IKW_IW:IKW_IW + IDX_HEADS] * (IDX_DIM ** -0.5 * IDX_HEADS ** -0.5)
                   ).reshape(db, n_tok * IDX_HEADS, 1)
        iw_rows = jnp.broadcast_to(iw_rows, (db, n_tok * IDX_HEADS, LANES))
        scores = _idx_sample(page_table, iq_rows, iw_rows, cik_t, l, n_tok, idx_group)
        pad_rows = LANES - n_tok
        ik_new = jnp.pad(ikws[:, :IDX_DIM].reshape(db, n_tok, IDX_DIM), ((0, 0), (0, pad_rows), (0, 0)))
        scores_new = _idx_new(iq_rows, iw_rows, ik_new, n_tok)
        q_rows = has[:, HA_Q:HA_Q + QKV_W].reshape(db, n_tok, N_HEADS, HEAD_DIM).transpose(0, 2, 1, 3)
        q_rows = q_rows.reshape(db, N_HEADS * n_tok, HEAD_DIM)
        k_new = jnp.pad(kvs[:, KV_K:KV_K + KV_W].reshape(db, n_tok, KV_W), ((0, 0), (0, pad_rows), (0, 0)))
        v_new = jnp.pad(kvs[:, KV_V:KV_V + KV_W].reshape(db, n_tok, KV_W), ((0, 0), (0, pad_rows), (0, 0)))
        a_s = _attn_sample(page_table, scores, scores_new, q_rows, k_new, v_new, bias_last, bias_new,
                           ck, cv, l, n_tok, attn_group)
        a_s = a_s.reshape(db, N_HEADS, n_tok, HEAD_DIM).transpose(0, 2, 1, 3).reshape(ms, QKV_W)
        gd = SGU_DIM // SGU_GROUPS
        wv = jnp.repeat(sgu_w[l][:, :n_tok, :n_tok].transpose(1, 2, 0), gd, axis=-1)
        bv = jnp.repeat(sgu_b[l][:, :n_tok].T, gd, axis=-1)
        c_s, conv_state_s, s_s, vn_s = _mix_sample(
            has[:, HA_GLU:HA_GLU + 2 * CONV_DIM], has[:, HA_UV:HA_UV + 2 * SGU_DIM], state_conv[l],
            conv_w[l], conv_b[l][None], conv_ln_g[l][None], conv_ln_b[l][None],
            sgu_ln_g[l][None], sgu_ln_b[l][None], wv, bv, db, n_tok)

        sample_rows = tuple(_pad_rows(t.astype(BF16), SAMPLE_TILE) for t in (a_s, c_s, s_s))
        merged = _merge((a_p, c_p, s_p), sample_rows, ha, w_pa_b, w_pb_b, w_pc_b, l, SAMPLE_TILE)
        x1f, x1b = _mm_ln(merged, w_out_b, l, xf, ln1_g[l][None], ln1_b[l][None], alpha, _pick_tile(m_all, 384))

        j = l // 2
        if l % 2 == 0:
            h = _swiglu_up(x1b, ffn_w1, ffn_w3, j, tm, 512)
            xf, xb = _mm_ln(h, ffn_w2_b, j, x1f, ln2_g[l][None], ln2_b[l][None], alpha, _pick_tile(m_all, 384))
        else:
            w_r = jnp.pad(moe_router[j], ((0, 0), (0, LANES - n_exp)))
            b_r = jnp.pad(moe_router_b[j], (0, LANES - n_exp))[None]
            ti, tg = _router(x1f, w_r, b_r, n_exp, _pick_tile(m_all, 528))
            pos, src, it_e, it_r0, it_ns, it_valid, it_nz = _moe_plan(ti[:, :TOP_K_EXPERTS], mp + ms, n_exp,
                                                                      n_slots, n_items, MOE_SUPER, MOE_SUB)
            xs = _dispatch(src, x1f, n_slots, DISPATCH_ROWS)
            ys = _moe_ffn(it_e, it_r0, it_ns, it_valid, it_nz, xs, moe_w1, moe_w3, moe_w2, j, n_items,
                          MOE_SUPER, MOE_SUB, MOE_TF)
            tb = COMBINE_TB if m_all % COMBINE_TB == 0 and mp % COMBINE_TB == 0 else 16
            if l == depth - 1:
                y_head, y_tail = _combine_ln(pos, ys, tg, x1f, ln2_g[l][None], ln2_b[l][None], alpha, tb,
                                             split_rows=mp)
            else:
                xf, xb = _combine_ln(pos, ys, tg, x1f, ln2_g[l][None], ln2_b[l][None], alpha, tb)

        outs["k"].append(k4)
        outs["v"].append(v4)
        outs["ik"].append(ikw[:, :IDX_DIM])
        outs["conv_p"].append(conv_state_p)
        outs["conv_s"].append(conv_state_s)
        outs["sgu_s"].append(vn_s.reshape(db, n_tok, SGU_DIM))

    def split(name, rows_per_token, width_shape):
        st = jnp.stack(outs[name])
        p = st[:, :mp * rows_per_token].reshape((depth, batch, seq) + width_shape)
        s = st[:, mp * rows_per_token:(mp + ms) * rows_per_token].reshape((depth, db, n_tok) + width_shape)
        return p, s

    k_p, k_s = split("k", N_KV_HEADS, (N_KV_HEADS, HEAD_DIM))
    v_p, v_s = split("v", N_KV_HEADS, (N_KV_HEADS, HEAD_DIM))
    ik_p, ik_s = split("ik", 1, (IDX_DIM,))
    if depth % 2 == 0:
        y_prompt = y_head.reshape(batch, seq, d_model)
        y_sample = y_tail[:ms].reshape(db, n_tok, d_model)
    else:
        y_prompt = xf[:mp].reshape(batch, seq, d_model)
        y_sample = xf[mp:mp + ms].reshape(db, n_tok, d_model)
    return (y_prompt, y_sample, k_p, v_p, ik_p, jnp.stack(outs["conv_p"]), k_s, v_s, ik_s,
            jnp.stack(outs["conv_s"]), jnp.stack(outs["sgu_s"]))
```

```python
import functools
import math

import jax
import jax.numpy as jnp
from jax import lax
from jax.experimental import pallas as pl
from jax.experimental.pallas import tpu as pltpu

F32 = jnp.float32
BF16 = jnp.bfloat16
I32 = jnp.int32

N_HEADS = 16
HEAD_DIM = 128
N_KV_HEADS = 4
KV_REP = N_HEADS // N_KV_HEADS
IDX_HEADS = 16
IDX_DIM = 64
TOPK_MAX = 256
PAGE_SIZE = 128
CONV_DIM = 1024
CONV_WIDTH = 31
SGU_DIM = 1024
SGU_GROUPS = 8
CHUNK = 128
N_BRANCH = 3
TOP_K_EXPERTS = 2
REL_BUCKETS = 32
REL_MAX_DIST = 128
LN_EPS = 1e-5

V7X_VMEM_BYTES = 64 * 2**20
VMEM_LIMIT = V7X_VMEM_BYTES - 8 * 2**20
LANES = 128
F32_SUBLANES = 8
INT_MIN = -(2**31)
MASK_NEG = -1e30

QKV_W = N_HEADS * HEAD_DIM
KV_W = N_KV_HEADS * HEAD_DIM
IQ_W = IDX_HEADS * IDX_DIM
HA_Q, HA_GLU, HA_UV, HA_GATE, HA_IQ = 0, 2048, 4096, 6144, 12288
HA_TILE = 1024
W_Q_ROW0 = 0
W_KV_ROW0 = W_Q_ROW0 + QKV_W
W_IQ_ROW0 = W_KV_ROW0 + 2 * KV_W
W_IK_ROW0 = W_IQ_ROW0 + IQ_W
W_GLU_ROW0 = W_IK_ROW0 + IDX_DIM + IDX_HEADS
W_UV_ROW0 = W_GLU_ROW0 + 2 * CONV_DIM
W_GATE_ROW0 = W_UV_ROW0 + 2 * SGU_DIM
KV_K, KV_V = 0, KV_W
IKW_IW = IDX_DIM

N_ROW_TILES = 8
SAMPLE_TILE = 256
MOE_SUB = 256
MOE_SUPER = 2048
MOE_TF = 256
MOE_WIDE = 2
COMBINE_TB = 128
IDX_PAGE_GROUP = 32
ATTN_PAGE_GROUP = 32


def _cp(sem, vmem=VMEM_LIMIT):
    return pltpu.CompilerParams(dimension_semantics=sem, vmem_limit_bytes=vmem)


def _round_up(x, m):
    return (x + m - 1) // m * m


def _ln_rows(z, g, b):
    mu = jnp.mean(z, axis=-1, keepdims=True)
    d = z - mu
    var = jnp.mean(d * d, axis=-1, keepdims=True)
    return d * lax.rsqrt(var + LN_EPS) * g + b


def _sigmoid(x):
    return 1.0 / (1.0 + jnp.exp(-x))


def _in_proj_body(off_ref, x_ref, w_ref, o_ref, wt_ref):
    @pl.when(pl.program_id(1) == 0)
    def _():
        wt_ref[...] = w_ref[0].T.astype(BF16)

    o_ref[...] = jnp.dot(x_ref[...], wt_ref[...], preferred_element_type=F32).astype(o_ref.dtype)


def _in_proj(x, w_t, layer, row_offsets, tm, tn):
    m, k = x.shape
    n_tiles = len(row_offsets)
    grid_spec = pltpu.PrefetchScalarGridSpec(
        num_scalar_prefetch=1,
        grid=(n_tiles, m // tm),
        in_specs=[pl.BlockSpec((tm, k), lambda j, i, off: (i, 0)),
                  pl.BlockSpec((pl.Element(1), pl.Element(tn), pl.Element(k)),
                               lambda j, i, off: (layer, pl.multiple_of(off[j], F32_SUBLANES), 0))],
        out_specs=pl.BlockSpec((tm, tn), lambda j, i, off: (i, j)),
        scratch_shapes=[pltpu.VMEM((k, tn), BF16)],
    )
    return pl.pallas_call(
        _in_proj_body,
        grid_spec=grid_spec,
        out_shape=jax.ShapeDtypeStruct((m, n_tiles * tn), BF16),
        compiler_params=_cp(("arbitrary", "arbitrary")),
        name="in_proj",
    )(jnp.asarray(row_offsets, I32), x, w_t)


def _kv_proj_body(x_ref, wkv_ref, wik_ref, kv_ref, ik_ref, k4_ref, v4_ref, wkv_t, wik_t):
    @pl.when(pl.program_id(0) == 0)
    def _():
        wkv_t[...] = wkv_ref[...].T.astype(BF16)
        wik_t[...] = wik_ref[...].T.astype(BF16)

    x = x_ref[...]
    tm = x.shape[0]
    kv = jnp.dot(x, wkv_t[...], preferred_element_type=F32)
    kv_ref[...] = kv
    ik_ref[...] = jnp.dot(x, wik_t[...], preferred_element_type=F32)
    for g in range(N_KV_HEADS):
        k4_ref[pl.ds(g, tm, stride=N_KV_HEADS), :] = kv[:, KV_K + g * HEAD_DIM:KV_K + (g + 1) * HEAD_DIM]
        v4_ref[pl.ds(g, tm, stride=N_KV_HEADS), :] = kv[:, KV_V + g * HEAD_DIM:KV_V + (g + 1) * HEAD_DIM]


def _kv_proj(x, w_t, layer, tm):
    m, k = x.shape
    assert W_KV_ROW0 % (2 * KV_W) == 0 and W_IK_ROW0 % LANES == 0
    row = lambda i: (i, 0)
    per_head = jax.ShapeDtypeStruct((m * N_KV_HEADS, HEAD_DIM), F32)
    return pl.pallas_call(
        _kv_proj_body,
        grid=(m // tm,),
        in_specs=[pl.BlockSpec((tm, k), row),
                  pl.BlockSpec((None, 2 * KV_W, k), lambda i: (layer, W_KV_ROW0 // (2 * KV_W), 0)),
                  pl.BlockSpec((None, LANES, k), lambda i: (layer, W_IK_ROW0 // LANES, 0))],
        out_specs=[pl.BlockSpec((tm, 2 * KV_W), row), pl.BlockSpec((tm, LANES), row),
                   pl.BlockSpec((tm * N_KV_HEADS, HEAD_DIM), row), pl.BlockSpec((tm * N_KV_HEADS, HEAD_DIM), row)],
        out_shape=[jax.ShapeDtypeStruct((m, 2 * KV_W), F32), jax.ShapeDtypeStruct((m, LANES), F32),
                   per_head, per_head],
        scratch_shapes=[pltpu.VMEM((k, 2 * KV_W), BF16), pltpu.VMEM((k, LANES), BF16)],
        compiler_params=_cp(("arbitrary",)),
        name="kv_proj",
    )(x, w_t, w_t)


def _row_halves(rows):
    half = rows // 2 // 16 * 16
    return (slice(0, half), slice(half, rows)) if half > 0 else (slice(0, rows),)


def _mm_ln_body(x_ref, w_ref, r_ref, g_ref, b_ref, of_ref, ob_ref, *, alpha):
    for rs in _row_halves(x_ref.shape[0]):
        z = jnp.dot(x_ref[rs, :], w_ref[...], preferred_element_type=F32) + alpha * r_ref[rs, :]
        y = _ln_rows(z, g_ref[...], b_ref[...])
        of_ref[rs, :] = y
        ob_ref[rs, :] = y.astype(BF16)


def _mm_ln(x, w, layer, resid, g, b, alpha, tm):
    m, k = x.shape
    n = w.shape[2]
    row = lambda i: (i, 0)
    return pl.pallas_call(
        functools.partial(_mm_ln_body, alpha=alpha),
        grid=(m // tm,),
        in_specs=[pl.BlockSpec((tm, k), row),
                  pl.BlockSpec((None, k, n), lambda i: (layer, 0, 0), pipeline_mode=pl.Buffered(1)),
                  pl.BlockSpec((tm, n), row),
                  pl.BlockSpec((1, n), lambda i: (0, 0)),
                  pl.BlockSpec((1, n), lambda i: (0, 0))],
        out_specs=[pl.BlockSpec((tm, n), row), pl.BlockSpec((tm, n), row)],
        out_shape=[jax.ShapeDtypeStruct((m, n), F32), jax.ShapeDtypeStruct((m, n), BF16)],
        compiler_params=_cp(("arbitrary",)),
        name="proj_ln",
    )(x, w, resid, g, b)


def _swiglu_up_body(x_ref, w1_ref, w3_ref, o_ref, w1b, w3b):
    @pl.when(pl.program_id(1) == 0)
    def _():
        w1b[...] = w1_ref[...].astype(BF16)
        w3b[...] = w3_ref[...].astype(BF16)

    for rs in _row_halves(x_ref.shape[0]):
        x = x_ref[rs, :]
        a = jnp.dot(x, w1b[...], preferred_element_type=F32)
        c = jnp.dot(x, w3b[...], preferred_element_type=F32)
        o_ref[rs, :] = (a * _sigmoid(a) * c).astype(o_ref.dtype)


def _swiglu_up(x, w1, w3, layer, tm, tf):
    m, k = x.shape
    f = w1.shape[2]
    return pl.pallas_call(
        _swiglu_up_body,
        grid=(f // tf, m // tm),
        in_specs=[pl.BlockSpec((tm, k), lambda j, i: (i, 0)),
                  pl.BlockSpec((None, k, tf), lambda j, i: (layer, 0, j)),
                  pl.BlockSpec((None, k, tf), lambda j, i: (layer, 0, j))],
        out_specs=pl.BlockSpec((tm, tf), lambda j, i: (i, j)),
        out_shape=jax.ShapeDtypeStruct((m, f), BF16),
        scratch_shapes=[pltpu.VMEM((k, tf), BF16), pltpu.VMEM((k, tf), BF16)],
        compiler_params=_cp(("arbitrary", "arbitrary")),
        name="ffn_up",
    )(x, w1, w3)


def _merge_body(ap_ref, cp_ref, sp_ref, as_ref, cs_ref, ss_ref, g_ref, wa_ref, wb_ref, wc_ref, o_ref, *, n_prompt):
    i = pl.program_id(0)
    d = o_ref.shape[1]

    def merge(a_ref, c_ref, s_ref):
        for rs in _row_halves(o_ref.shape[0]):
            pa = jnp.dot(a_ref[rs, :], wa_ref[...], preferred_element_type=F32)
            pb = jnp.dot(c_ref[rs, :], wb_ref[...], preferred_element_type=F32)
            pc = jnp.dot(s_ref[rs, :], wc_ref[...], preferred_element_type=F32)
            o = (_sigmoid(g_ref[rs, :d].astype(F32)) * pa + _sigmoid(g_ref[rs, d:2 * d].astype(F32)) * pb
                 + _sigmoid(g_ref[rs, 2 * d:].astype(F32)) * pc)
            o_ref[rs, :] = o.astype(o_ref.dtype)

    @pl.when(i < n_prompt)
    def _():
        merge(ap_ref, cp_ref, sp_ref)

    @pl.when(i >= n_prompt)
    def _():
        merge(as_ref, cs_ref, ss_ref)


def _merge(prompt, sample, ha, w_pa, w_pb, w_pc, layer, tm):
    m = ha.shape[0]
    d = w_pa.shape[2]
    n_prompt = prompt[0].shape[0] // tm
    assert prompt[0].shape[0] % tm == 0 and sample[0].shape[0] == tm and m == (n_prompt + 1) * tm
    assert HA_GATE % (N_BRANCH * d) == 0
    p_row = lambda i: (jnp.minimum(i, n_prompt - 1), 0)
    fixed = lambda i: (0, 0)
    return pl.pallas_call(
        functools.partial(_merge_body, n_prompt=n_prompt),
        grid=(n_prompt + 1,),
        in_specs=[pl.BlockSpec((tm, t.shape[1]), p_row) for t in prompt]
        + [pl.BlockSpec((tm, t.shape[1]), fixed) for t in sample]
        + [pl.BlockSpec((tm, N_BRANCH * d), lambda i: (i, HA_GATE // (N_BRANCH * d)))]
        + [pl.BlockSpec((None,) + w.shape[1:], lambda i: (layer, 0, 0), pipeline_mode=pl.Buffered(1))
           for w in (w_pa, w_pb, w_pc)],
        out_specs=pl.BlockSpec((tm, d), lambda i: (i, 0)),
        out_shape=jax.ShapeDtypeStruct((m, d), BF16),
        compiler_params=_cp(("arbitrary",)),
        name="branch_merge",
    )(*prompt, *sample, ha, w_pa, w_pb, w_pc)


def _sortable_key(score):
    bits = pltpu.bitcast(score, I32)
    return bits ^ ((bits >> 31) & jnp.int32(0x7FFFFFFF))


def _kth_largest_key(count_ge, rows, k):
    def step(it, t):
        cand = t + jnp.left_shift(jnp.int32(1), 31 - it)
        return jnp.where(count_ge(cand) >= k, cand, t)

    return lax.fori_loop(0, 32, step, jnp.full((rows, 1), INT_MIN, I32))


RADIX4_STEPS = 16
RADIX4_FIELD_BITS = 5
RADIX4_WIDE = 4096.0


def _kth_largest_key_radix4(count3, rows, k, n_valid):
    kf = float(k)

    def cond(state):
        it, _, _, pending = state
        return jnp.logical_and(it < RADIX4_STEPS, pending > 0.0)

    def body(state):
        it, t, cnt, _ = state
        one = jnp.left_shift(jnp.int32(1), 30 - 2 * it)
        c1 = t + one
        c2 = c1 + one
        c3 = c2 + one
        n1, n2, n3 = count3(c1, c2, c3)
        t = jnp.where(n3 >= kf, c3, jnp.where(n2 >= kf, c2, jnp.where(n1 >= kf, c1, t)))
        cnt = jnp.where(n3 >= kf, n3, jnp.where(n2 >= kf, n2, jnp.where(n1 >= kf, n1, cnt)))
        settled = jnp.logical_or(cnt == kf, n_valid <= kf)
        return it + 1, t, cnt, jnp.max(jnp.where(settled, 0.0, 1.0))

    init = (jnp.int32(0), jnp.full((rows, 1), INT_MIN, I32), jnp.full((rows, 1), -1.0, F32), jnp.float32(1.0))
    return lax.while_loop(cond, body, init)[1]


def _lane_tile(x, width):
    return x if width == LANES else jnp.concatenate([x] * (width // LANES), axis=1)


def _attn_prompt_body(q_ref, iq_ref, iw_ref, kv_ref, ik_ref, bias_ref, o_ref,
                      keys_ref, mb_ref, iwb_ref, m_ref, acc_ref, *, tq, k_sel):
    i = pl.program_id(1)
    nkb = i + 1
    tk = tq
    row = lax.broadcasted_iota(I32, (tq, tk), 0)
    col = lax.broadcasted_iota(I32, (tq, tk), 1)
    low_half = lax.broadcasted_iota(I32, (tk, LANES), 1) < IDX_DIM
    iw = iw_ref[...] * (IDX_DIM ** -0.5 * IDX_HEADS ** -0.5)
    for h in range(IDX_HEADS):
        iwb_ref[h] = jnp.broadcast_to(iw[:, IKW_IW + h:IKW_IW + h + 1], (tq, LANES))
    iq_pairs = [iq_ref[:, p * LANES:(p + 1) * LANES] for p in range(IDX_HEADS // 2)]

    def idx_block(kb, carry):
        ik_lo = jnp.where(low_half, ik_ref[pl.ds(pl.multiple_of(kb * tk, tk), tk), :], 0.0)
        ik_sides = (ik_lo.astype(BF16), pltpu.roll(ik_lo, IDX_DIM, axis=1).astype(BF16))
        acc = jnp.zeros((tq, tk), F32)
        for h in range(IDX_HEADS):
            s = lax.dot_general(iq_pairs[h // 2], ik_sides[h % 2], (((1,), (1,)), ((), ())),
                                preferred_element_type=F32)
            acc = acc + jnp.maximum(s, 0.0) * _lane_tile(iwb_ref[h], tk)
        causal = (col + kb * tk) <= (row + i * tq)
        keys_ref[kb] = jnp.where(causal, _sortable_key(acc), jnp.int32(INT_MIN))
        return carry

    def idx_pair(t, carry):
        idx_block(2 * t, carry)
        return idx_block(2 * t + 1, carry)

    lax.fori_loop(0, nkb // 2, idx_pair, 0)

    @pl.when(nkb % 2 == 1)
    def _():
        idx_block(nkb - 1, 0)
        keys_ref[nkb] = jnp.full((tq, tk), INT_MIN, I32)

    fb = RADIX4_FIELD_BITS
    field_mask = (1 << fb) - 1
    inc1, inc2, inc3 = 1, 1 + (1 << fb), 1 + (1 << fb) + (1 << 2 * fb)

    def count3(c1, c2, c3):
        b1, b2, b3 = (jnp.broadcast_to(c, (tq, LANES)) for c in (c1, c2, c3))

        def cb(pair, acc):
            for kb in (2 * pair, 2 * pair + 1):
                keys = keys_ref[kb]
                for t in range(tk // LANES):
                    kk = keys[:, t * LANES:(t + 1) * LANES]
                    acc = acc + jnp.where(kk >= b3, inc3, jnp.where(kk >= b2, inc2, jnp.where(kk >= b1, inc1, 0)))
            return acc

        acc = lax.fori_loop(0, (nkb + 1) // 2, cb, jnp.zeros((tq, LANES), I32))
        n3 = (acc >> (2 * fb)).astype(F32).sum(axis=-1, keepdims=True)
        low = ((acc & field_mask).astype(F32) + ((acc >> fb) & field_mask).astype(F32) * RADIX4_WIDE)
        low = low.sum(axis=-1, keepdims=True)
        n2 = jnp.floor(low * (1.0 / RADIX4_WIDE))
        return low - RADIX4_WIDE * n2, n2, n3

    n_valid = (lax.broadcasted_iota(I32, (tq, 1), 0) + (i * tq + 1)).astype(F32)
    thr = _kth_largest_key_radix4(count3, tq, k_sel, n_valid)
    thr = jnp.maximum(thr, jnp.int32(INT_MIN + 1))
    thr_b = _lane_tile(jnp.broadcast_to(thr, (tq, LANES)), tk)

    def mask_block(kb, carry):
        mb_ref[kb] = jnp.where(keys_ref[kb] >= thr_b, 0.0, MASK_NEG)
        return carry

    lax.fori_loop(0, nkb, mask_block, 0)

    scale = HEAD_DIM ** -0.5
    for g in range(N_KV_HEADS):
        qg = jnp.concatenate(
            [q_ref[:, (KV_REP * g + r) * HEAD_DIM:(KV_REP * g + r + 1) * HEAD_DIM] for r in range(KV_REP)], axis=0)
        m_ref[...] = jnp.full(m_ref.shape, MASK_NEG, F32)
        acc_ref[...] = jnp.zeros(acc_ref.shape, F32)

        def block(kb, bias_idx, g=g, qg=qg):
            r0 = pl.multiple_of(kb * tk, tk)
            kblk = kv_ref[pl.ds(r0, tk), KV_K + g * HEAD_DIM:KV_K + (g + 1) * HEAD_DIM].astype(BF16)
            vblk = kv_ref[pl.ds(r0, tk), KV_V + g * HEAD_DIM:KV_V + (g + 1) * HEAD_DIM].astype(BF16)
            s = lax.dot_general(qg, kblk, (((1,), (1,)), ((), ())), preferred_element_type=F32) * scale
            s = s.reshape(KV_REP, tq, tk) + mb_ref[kb][None]
            if bias_idx is not None:
                s = s + bias_ref[bias_idx, KV_REP * g:KV_REP * (g + 1)]
            s = s.reshape(KV_REP * tq, tk)
            m_old = m_ref[...]
            m_new = jnp.maximum(m_old, s.max(axis=-1, keepdims=True))
            p = jnp.exp(s - _lane_tile(m_new, tk))
            alpha = jnp.exp(m_old - m_new)
            v_ones = jnp.concatenate([vblk, jnp.ones((tk, LANES), BF16)], axis=1)
            acc_ref[...] = (_lane_tile(alpha, HEAD_DIM + LANES) * acc_ref[...]
                            + jnp.dot(p.astype(BF16), v_ones, preferred_element_type=F32))
            m_ref[...] = m_new

        n_far = jnp.maximum(i - 1, 0)

        def far_pair(t, carry):
            block(2 * t, None)
            block(2 * t + 1, None)
            return carry

        lax.fori_loop(0, n_far // 2, far_pair, 0)

        @pl.when(n_far % 2 == 1)
        def _():
            block(n_far - 1, None)

        @pl.when(i >= 1)
        def _():
            block(i - 1, 1)
            block(i, 0)

        @pl.when(i == 0)
        def _():
            block(i, 0)
        out = acc_ref[:, :HEAD_DIM] / acc_ref[:, HEAD_DIM:]
        for r in range(KV_REP):
            h = KV_REP * g + r
            o_ref[:, h * HEAD_DIM:(h + 1) * HEAD_DIM] = out[r * tq:(r + 1) * tq].astype(o_ref.dtype)


def _attn_prompt(ha, kv, ikw, bias_tiles, batch, seq, tq):
    nq = seq // tq
    k_sel = min(TOPK_MAX, seq // 4)
    assert seq // LANES < (1 << RADIX4_FIELD_BITS)
    body = functools.partial(_attn_prompt_body, tq=tq, k_sel=k_sel)
    return pl.pallas_call(
        body,
        grid=(batch, nq),
        in_specs=[pl.BlockSpec((tq, QKV_W), lambda b, i: (b * nq + i, HA_Q // QKV_W)),
                  pl.BlockSpec((tq, IQ_W), lambda b, i: (b * nq + i, HA_IQ // IQ_W)),
                  pl.BlockSpec((tq, LANES), lambda b, i: (b * nq + i, 0)),
                  pl.BlockSpec((seq, 2 * KV_W), lambda b, i: (b, 0), pipeline_mode=pl.Buffered(1)),
                  pl.BlockSpec((seq, LANES), lambda b, i: (b, 0), pipeline_mode=pl.Buffered(1)),
                  pl.BlockSpec((2, N_HEADS, tq, tq), lambda b, i: (0, 0, 0, 0), pipeline_mode=pl.Buffered(1))],
        out_specs=pl.BlockSpec((tq, QKV_W), lambda b, i: (b * nq + i, 0)),
        out_shape=jax.ShapeDtypeStruct((batch * seq, QKV_W), BF16),
        scratch_shapes=[pltpu.VMEM((nq + nq % 2, tq, tq), I32),
                        pltpu.VMEM((nq, tq, tq), F32),
                        pltpu.VMEM((IDX_HEADS, tq, LANES), F32),
                        pltpu.VMEM((KV_REP * tq, LANES), F32),
                        pltpu.VMEM((KV_REP * tq, HEAD_DIM + LANES), F32)],
        compiler_params=_cp(("arbitrary", "arbitrary")),
        name="attn_prompt",
    )(ha, ha, ikw, kv, ikw, bias_tiles)


def _page_spec(block, layer, group, slot):
    return pl.BlockSpec((None, None) + block, lambda b, p, pt: (layer, pt[b, p * group + slot], 0, 0))


def _idx_sample_body(pt_ref, iq_ref, iw_ref, *refs, n_tok, group):
    pages, o_ref = refs[:group], refs[group]
    ikt = jnp.concatenate([r[...].astype(BF16) for r in pages], axis=1)
    s = jnp.dot(iq_ref[0], ikt, preferred_element_type=F32)
    r = jnp.maximum(s, 0.0) * _lane_tile(iw_ref[0], group * PAGE_SIZE)
    tok_scores = r.reshape(n_tok, IDX_HEADS, group * PAGE_SIZE).sum(axis=1)
    for t in range(group):
        o_ref[0, t] = tok_scores[:, t * PAGE_SIZE:(t + 1) * PAGE_SIZE]


def _idx_sample(page_table, iq_rows, iw_rows, cache_ik_t, layer, n_tok, group):
    db, n_pages = page_table.shape
    rows = n_tok * IDX_HEADS
    body = functools.partial(_idx_sample_body, n_tok=n_tok, group=group)
    grid_spec = pltpu.PrefetchScalarGridSpec(
        num_scalar_prefetch=1,
        grid=(db, n_pages // group),
        in_specs=[pl.BlockSpec((1, rows, IDX_DIM), lambda b, p, pt: (b, 0, 0)),
                  pl.BlockSpec((1, rows, LANES), lambda b, p, pt: (b, 0, 0))]
        + [_page_spec((IDX_DIM, PAGE_SIZE), layer, group, t) for t in range(group)],
        out_specs=pl.BlockSpec((1, group, n_tok, PAGE_SIZE), lambda b, p, pt: (b, p, 0, 0)),
    )
    return pl.pallas_call(
        body,
        grid_spec=grid_spec,
        out_shape=jax.ShapeDtypeStruct((db, n_pages, n_tok, PAGE_SIZE), F32),
        compiler_params=_cp(("arbitrary", "arbitrary")),
        name="idx_sample",
    )(page_table, iq_rows, iw_rows, *([cache_ik_t] * group))


def _attn_sample_body(pt_ref, sc_ref, scn_ref, q_ref, kn_ref, vn_ref, bl_ref, bn_ref, *refs,
                      n_tok, n_pages, k_sel, group):
    kp_refs, vp_refs = refs[:group], refs[group:2 * group]
    o_ref, thr_ref, m_ref, l_ref, acc_ref = refs[2 * group:]
    p = pl.program_id(1)
    n_steps = n_pages // group
    grp = KV_REP * n_tok
    width = group * PAGE_SIZE
    scale = HEAD_DIM ** -0.5
    r_i = lax.broadcasted_iota(I32, (n_tok, LANES), 0)
    c_i = lax.broadcasted_iota(I32, (n_tok, LANES), 1)
    keys_new = jnp.where(c_i <= r_i, _sortable_key(scn_ref[0]), jnp.int32(INT_MIN))

    @pl.when(p == 0)
    def _():
        keys = _sortable_key(sc_ref[0])

        def count_ge(cand):
            cand_b = jnp.broadcast_to(cand, (n_tok, LANES))
            c = jnp.where(keys >= cand_b[None], 1.0, 0.0).sum(axis=0) + jnp.where(keys_new >= cand_b, 1.0, 0.0)
            return c.sum(axis=-1, keepdims=True)

        thr = _kth_largest_key(count_ge, n_tok, float(k_sel))
        thr_ref[...] = jnp.broadcast_to(jnp.maximum(thr, jnp.int32(INT_MIN + 1)), (n_tok, LANES))
        m_ref[...] = jnp.full(m_ref.shape, MASK_NEG, F32)
        l_ref[...] = jnp.zeros(l_ref.shape, F32)
        acc_ref[...] = jnp.zeros(acc_ref.shape, F32)

    thr_b = thr_ref[...]

    def update(g, s, vblk):
        sl = slice(g * grp, (g + 1) * grp)
        m_old = m_ref[sl, :]
        m_new = jnp.maximum(m_old, s.max(axis=-1, keepdims=True))
        pr = jnp.exp(s - _lane_tile(m_new, s.shape[1]))
        alpha = jnp.exp(m_old - m_new)
        l_ref[sl, :] = alpha * l_ref[sl, :] + pr.sum(axis=-1, keepdims=True)
        acc_ref[sl, :] = alpha * acc_ref[sl, :] + jnp.dot(pr.astype(BF16), vblk, preferred_element_type=F32)
        m_ref[sl, :] = m_new

    def group_rows(mask_tok):
        return jnp.concatenate([mask_tok] * KV_REP, axis=0)

    base = p * group
    mb = jnp.concatenate(
        [jnp.where(_sortable_key(sc_ref[0, base + t]) >= thr_b, 0.0, MASK_NEG) for t in range(group)], axis=1)
    mb_g = group_rows(mb)
    is_last = p == n_steps - 1
    zeros_head = jnp.zeros((grp, width - PAGE_SIZE), F32)
    for g in range(N_KV_HEADS):
        qg = q_ref[0, g * grp:(g + 1) * grp, :]
        kg = jnp.concatenate([r[pl.ds(g, PAGE_SIZE, stride=N_KV_HEADS), :].astype(BF16) for r in kp_refs], axis=0)
        vg = jnp.concatenate([r[pl.ds(g, PAGE_SIZE, stride=N_KV_HEADS), :].astype(BF16) for r in vp_refs], axis=0)
        s = lax.dot_general(qg, kg, (((1,), (1,)), ((), ())), preferred_element_type=F32) * scale
        near = jnp.where(is_last, bl_ref[g * grp:(g + 1) * grp, :], 0.0)
        s = s + mb_g + (near if group == 1 else jnp.concatenate([zeros_head, near], axis=1))
        update(g, s, vg)

    @pl.when(is_last)
    def _():
        mbn_g = group_rows(jnp.where(keys_new >= thr_b, 0.0, MASK_NEG))
        for g in range(N_KV_HEADS):
            qg = q_ref[0, g * grp:(g + 1) * grp, :]
            kblk = kn_ref[0, :, g * HEAD_DIM:(g + 1) * HEAD_DIM].astype(BF16)
            vblk = vn_ref[0, :, g * HEAD_DIM:(g + 1) * HEAD_DIM].astype(BF16)
            s = lax.dot_general(qg, kblk, (((1,), (1,)), ((), ())), preferred_element_type=F32) * scale
            s = s + mbn_g + bn_ref[g * grp:(g + 1) * grp, :]
            update(g, s, vblk)
        o_ref[0] = acc_ref[...] / l_ref[...]


def _attn_sample(page_table, scores, scores_new, q_rows, k_new, v_new, bias_last, bias_new,
                 cache_k, cache_v, layer, n_tok, group):
    db, n_pages = page_table.shape
    past = n_pages * PAGE_SIZE
    rows = N_HEADS * n_tok
    k_sel = min(TOPK_MAX, (past + n_tok) // 4)
    body = functools.partial(_attn_sample_body, n_tok=n_tok, n_pages=n_pages, k_sel=k_sel, group=group)
    page = (PAGE_SIZE * N_KV_HEADS, HEAD_DIM)
    grid_spec = pltpu.PrefetchScalarGridSpec(
        num_scalar_prefetch=1,
        grid=(db, n_pages // group),
        in_specs=[pl.BlockSpec((1, n_pages, n_tok, PAGE_SIZE), lambda b, p, pt: (b, 0, 0, 0)),
                  pl.BlockSpec((1, n_tok, LANES), lambda b, p, pt: (b, 0, 0)),
                  pl.BlockSpec((1, rows, HEAD_DIM), lambda b, p, pt: (b, 0, 0)),
                  pl.BlockSpec((1, LANES, KV_W), lambda b, p, pt: (b, 0, 0)),
                  pl.BlockSpec((1, LANES, KV_W), lambda b, p, pt: (b, 0, 0)),
                  pl.BlockSpec((rows, LANES), lambda b, p, pt: (0, 0)),
                  pl.BlockSpec((rows, LANES), lambda b, p, pt: (0, 0))]
        + [_page_spec(page, layer, group, t) for t in range(group)]
        + [_page_spec(page, layer, group, t) for t in range(group)],
        out_specs=pl.BlockSpec((1, rows, HEAD_DIM), lambda b, p, pt: (b, 0, 0)),
        scratch_shapes=[pltpu.VMEM((n_tok, LANES), I32),
                        pltpu.VMEM((rows, LANES), F32),
                        pltpu.VMEM((rows, LANES), F32),
                        pltpu.VMEM((rows, HEAD_DIM), F32)],
    )
    return pl.pallas_call(
        body,
        grid_spec=grid_spec,
        out_shape=jax.ShapeDtypeStruct((db, rows, HEAD_DIM), F32),
        compiler_params=_cp(("arbitrary", "arbitrary")),
        name="attn_sample",
    )(page_table, scores, scores_new, q_rows, k_new, v_new, bias_last, bias_new,
      *([cache_k] * group), *([cache_v] * group))


def _idx_new_body(iq_ref, iw_ref, ik_ref, o_ref, *, n_tok):
    for b in range(iq_ref.shape[0]):
        s = lax.dot_general(iq_ref[b], ik_ref[b].astype(BF16), (((1,), (1,)), ((), ())),
                            preferred_element_type=F32)
        r = jnp.maximum(s, 0.0) * iw_ref[b]
        o_ref[b] = r.reshape(n_tok, IDX_HEADS, LANES).sum(axis=1)


def _idx_new(iq_rows, iw_rows, ik_new, n_tok):
    db = iq_rows.shape[0]
    return pl.pallas_call(
        functools.partial(_idx_new_body, n_tok=n_tok),
        out_shape=jax.ShapeDtypeStruct((db, n_tok, LANES), F32),
        name="idx_new",
    )(iq_rows, iw_rows, ik_new)


HALO = CONV_WIDTH - 1
HALO_PAD = 32
CONV_ROW_BLOCK = 128


CONV_CHUNKS = CONV_DIM // LANES


def _conv_prompt_body(glu_ref, cw_ref, cb_ref, g_ref, b_ref, o_ref, st_ref, hp_ref, c_ref, sh_ref, *, ts, ns):
    s_idx = pl.program_id(1)

    @pl.when(s_idx == 0)
    def _():
        hp_ref[:, 0:HALO_PAD, :] = jnp.zeros((CONV_CHUNKS, HALO_PAD, LANES), F32)

    for c in range(CONV_CHUNKS):
        cs = slice(c * LANES, (c + 1) * LANES)
        a = glu_ref[:, cs].astype(F32)
        gt = glu_ref[:, CONV_DIM + c * LANES:CONV_DIM + (c + 1) * LANES].astype(F32)
        hp_ref[c, HALO_PAD:HALO_PAD + ts, :] = a * _sigmoid(gt)
    off = HALO_PAD - HALO
    rows = min(ts, CONV_ROW_BLOCK)

    def chunk(c, carry):
        for r0 in range(0, ts, rows):
            acc = jnp.zeros((rows // F32_SUBLANES, F32_SUBLANES, LANES), F32)
            for shift in range(F32_SUBLANES):
                taps = range(shift, CONV_WIDTH, F32_SUBLANES)
                span = rows + taps[-1] - shift
                sh_ref[0:span, :] = hp_ref[c, off + r0 + shift:off + r0 + shift + span, :]
                for w in taps:
                    x = sh_ref[w - shift:w - shift + rows, :].reshape(rows // F32_SUBLANES, F32_SUBLANES, LANES)
                    acc = acc + x * cw_ref[c, w][None]
            c_ref[c, r0:r0 + rows, :] = acc.reshape(rows, LANES)
        return carry

    lax.fori_loop(0, CONV_CHUNKS, chunk, 0)

    conv = [c_ref[c] + cb_ref[:, c * LANES:(c + 1) * LANES] for c in range(CONV_CHUNKS)]
    mu = sum(v.sum(axis=-1, keepdims=True) for v in conv) / CONV_DIM
    var = sum(((v - mu) * (v - mu)).sum(axis=-1, keepdims=True) for v in conv) / CONV_DIM
    inv = lax.rsqrt(var + LN_EPS)
    for c in range(CONV_CHUNKS):
        cs = slice(c * LANES, (c + 1) * LANES)
        y = (conv[c] - mu) * inv * g_ref[:, cs] + b_ref[:, cs]
        o_ref[:, cs] = (y * _sigmoid(y)).astype(o_ref.dtype)

    @pl.when(s_idx == ns - 1)
    def _():
        for c in range(CONV_CHUNKS):
            st_ref[0, :, c * LANES:(c + 1) * LANES] = hp_ref[c, ts + off:ts + HALO_PAD, :]

    hp_ref[:, off:HALO_PAD, :] = hp_ref[:, ts + off:ts + HALO_PAD, :]


def _conv_prompt(ha, conv_w, conv_b, ln_g, ln_b, batch, seq, ts):
    ns = seq // ts
    body = functools.partial(_conv_prompt_body, ts=ts, ns=ns)
    vec = pl.BlockSpec((1, CONV_DIM), lambda b, s: (0, 0))
    return pl.pallas_call(
        body,
        grid=(batch, ns),
        in_specs=[pl.BlockSpec((ts, 2 * CONV_DIM), lambda b, s: (b * ns + s, HA_GLU // (2 * CONV_DIM))),
                  pl.BlockSpec((CONV_CHUNKS, CONV_WIDTH, F32_SUBLANES, LANES), lambda b, s: (0, 0, 0, 0)),
                  vec, vec, vec],
        out_specs=[pl.BlockSpec((ts, CONV_DIM), lambda b, s: (b * ns + s, 0)),
                   pl.BlockSpec((1, HALO, CONV_DIM), lambda b, s: (b, 0, 0))],
        out_shape=[jax.ShapeDtypeStruct((batch * seq, CONV_DIM), BF16),
                   jax.ShapeDtypeStruct((batch, HALO, CONV_DIM), F32)],
        scratch_shapes=[pltpu.VMEM((CONV_CHUNKS, HALO_PAD + ts, LANES), F32),
                        pltpu.VMEM((CONV_CHUNKS, ts, LANES), F32),
                        pltpu.VMEM((HALO_PAD + min(ts, CONV_ROW_BLOCK), LANES), F32)],
        compiler_params=_cp(("arbitrary", "arbitrary")),
        name="conv_prompt",
    )(ha, conv_w, conv_b, ln_g, ln_b)


def _sgu_prompt_body(uv_ref, g_ref, b_ref, w_ref, bs_ref, o_ref, *, ts):
    u = uv_ref[:, :SGU_DIM].astype(F32)
    vn = _ln_rows(uv_ref[:, SGU_DIM:].astype(F32), g_ref[...], b_ref[...]).astype(BF16)
    r_i = lax.broadcasted_iota(I32, (CHUNK, CHUNK), 0)
    c_i = lax.broadcasted_iota(I32, (CHUNK, CHUNK), 1)
    gd = SGU_DIM // SGU_GROUPS
    for g in range(SGU_GROUPS):
        wm = jnp.where(c_i <= r_i, w_ref[g], 0.0).astype(BF16)
        bias = bs_ref[:, g:g + 1]
        for c in range(ts // CHUNK):
            rs = slice(c * CHUNK, (c + 1) * CHUNK)
            gs = slice(g * gd, (g + 1) * gd)
            mixed = jnp.dot(wm, vn[rs, gs], preferred_element_type=F32) + bias
            o_ref[rs, gs] = (u[rs, gs] * mixed).astype(o_ref.dtype)


def _sgu_prompt(ha, ln_g, ln_b, w_s, b_s_t, batch, seq, ts):
    ns = seq // ts
    vec = pl.BlockSpec((1, SGU_DIM), lambda b, s: (0, 0))
    return pl.pallas_call(
        functools.partial(_sgu_prompt_body, ts=ts),
        grid=(batch, ns),
        in_specs=[pl.BlockSpec((ts, 2 * SGU_DIM), lambda b, s: (b * ns + s, HA_UV // (2 * SGU_DIM))),
                  vec, vec,
                  pl.BlockSpec((SGU_GROUPS, CHUNK, CHUNK), lambda b, s: (0, 0, 0)),
                  pl.BlockSpec((CHUNK, SGU_GROUPS), lambda b, s: (0, 0))],
        out_specs=pl.BlockSpec((ts, SGU_DIM), lambda b, s: (b * ns + s, 0)),
        out_shape=jax.ShapeDtypeStruct((batch * seq, SGU_DIM), BF16),
        compiler_params=_cp(("arbitrary", "arbitrary")),
        name="sgu_prompt",
    )(ha, ln_g, ln_b, w_s, b_s_t)


def _mix_sample_body(glu_ref, uv_ref, st_ref, cw_ref, cb_ref, cg_ref, cbt_ref, sg_ref, sb_ref, wv_ref, bv_ref,
                     co_ref, nst_ref, so_ref, vn_ref, hp_ref, *, db, n_tok):
    cw = cw_ref[...]
    for b in range(db):
        rs = slice(b * n_tok, (b + 1) * n_tok)
        a = glu_ref[rs, :CONV_DIM].astype(F32)
        gt = glu_ref[rs, CONV_DIM:].astype(F32)
        hp_ref[0:HALO, :] = st_ref[b]
        hp_ref[HALO:HALO + n_tok, :] = a * _sigmoid(gt)
        rows = [jnp.sum(hp_ref[t:t + CONV_WIDTH, :] * cw, axis=0, keepdims=True) for t in range(n_tok)]
        c = jnp.concatenate(rows, axis=0) + cb_ref[...]
        y = _ln_rows(c, cg_ref[...], cbt_ref[...])
        co_ref[rs, :] = y * _sigmoid(y)
        nst_ref[b] = hp_ref[n_tok:n_tok + HALO, :]
        u = uv_ref[rs, :SGU_DIM].astype(F32)
        vn = _ln_rows(uv_ref[rs, SGU_DIM:].astype(F32), sg_ref[...], sb_ref[...])
        vn_ref[rs, :] = vn
        mixed = []
        for t in range(n_tok):
            acc = bv_ref[t:t + 1, :]
            for s in range(t + 1):
                acc = acc + wv_ref[t, s:s + 1, :] * vn[s:s + 1, :]
            mixed.append(acc)
        so_ref[rs, :] = u * jnp.concatenate(mixed, axis=0)


def _mix_sample(glu, uv, state, conv_w, conv_b, cln_g, cln_b, sln_g, sln_b, wv, bv, db, n_tok):
    rows = db * n_tok
    return pl.pallas_call(
        functools.partial(_mix_sample_body, db=db, n_tok=n_tok),
        out_shape=[jax.ShapeDtypeStruct((rows, CONV_DIM), F32),
                   jax.ShapeDtypeStruct((db, HALO, CONV_DIM), F32),
                   jax.ShapeDtypeStruct((rows, SGU_DIM), F32),
                   jax.ShapeDtypeStruct((rows, SGU_DIM), F32)],
        scratch_shapes=[pltpu.VMEM((_round_up(HALO + n_tok, 8), CONV_DIM), F32)],
        name="mix_sample",
    )(glu, uv, state, conv_w, conv_b, cln_g, cln_b, sln_g, sln_b, wv, bv)


def _router_body(x_ref, w_ref, b_ref, ti_ref, tg_ref, *, n_exp):
    logits = jnp.dot(x_ref[...], w_ref[...], preferred_element_type=F32,
                     precision=lax.Precision.HIGHEST) + b_ref[...]
    lane = lax.broadcasted_iota(I32, logits.shape, 1)
    lane_f = lane.astype(F32)
    neg = -jnp.inf
    lg = jnp.where(lane < n_exp, logits, neg)
    v1 = lg.max(axis=-1, keepdims=True)
    i1 = jnp.where(lg == v1, lane_f, float(LANES)).min(axis=-1, keepdims=True)
    lg2 = jnp.where(lane_f == i1, neg, lg)
    v2 = lg2.max(axis=-1, keepdims=True)
    i2 = jnp.where(lg2 == v2, lane_f, float(LANES)).min(axis=-1, keepdims=True)
    e = jnp.exp(v2 - v1)
    den = 1.0 + e
    ti_ref[...] = jnp.where(lane == 0, i1, jnp.where(lane == 1, i2, 0.0)).astype(I32)
    tg_ref[...] = jnp.where(lane == 0, 1.0 / den, jnp.where(lane == 1, e / den, 0.0))


def _router(x, w_pad, b_pad, n_exp, tm):
    m, d = x.shape
    return pl.pallas_call(
        functools.partial(_router_body, n_exp=n_exp),
        grid=(m // tm,),
        in_specs=[pl.BlockSpec((tm, d), lambda i: (i, 0)),
                  pl.BlockSpec((d, LANES), lambda i: (0, 0)),
                  pl.BlockSpec((1, LANES), lambda i: (0, 0))],
        out_specs=[pl.BlockSpec((tm, LANES), lambda i: (i, 0)),
                   pl.BlockSpec((tm, LANES), lambda i: (i, 0))],
        out_shape=[jax.ShapeDtypeStruct((m, LANES), I32), jax.ShapeDtypeStruct((m, LANES), F32)],
        compiler_params=_cp(("arbitrary",)),
        name="moe_router",
    )(x, w_pad, b_pad)


def _row_copy(src_hbm, dst, src_row, dst_row, sem):
    return pltpu.make_async_copy(src_hbm.at[pl.ds(src_row, 1)], dst.at[pl.ds(dst_row, 1)], sem)


ROW_DMA_UNROLL = 8
ROW_DMA_SLOTS = 2


def _prefetch_chunks(issue, wait):
    c = pl.program_id(0)
    ahead = ROW_DMA_SLOTS - 1

    @pl.when(c == 0)
    def _():
        for first in range(ahead):
            @pl.when(first < pl.num_programs(0))
            def _():
                issue(first, first)

    @pl.when(c + ahead < pl.num_programs(0))
    def _():
        issue(c + ahead, (c + ahead) % ROW_DMA_SLOTS)

    slot = c % ROW_DMA_SLOTS
    wait(slot)
    return slot


def _dispatch_body(src_ref, x_hbm, o_ref, buf, sem, *, rows):
    def issue(chunk, slot):
        def start(u, carry):
            for v in range(ROW_DMA_UNROLL):
                r = v * (rows // ROW_DMA_UNROLL) + u
                _row_copy(x_hbm, buf.at[slot], src_ref[chunk * rows + r], r, sem.at[slot]).start()
            return carry

        lax.fori_loop(0, rows // ROW_DMA_UNROLL, start, 0)

    def wait(slot):
        def w(r, carry):
            _row_copy(x_hbm, buf.at[slot], 0, r, sem.at[slot]).wait()
            return carry

        lax.fori_loop(0, rows, w, 0)

    slot = _prefetch_chunks(issue, wait)
    o_ref[...] = buf[slot].astype(o_ref.dtype)


def _dispatch(src, x, n_slots, rows):
    d = x.shape[1]
    grid_spec = pltpu.PrefetchScalarGridSpec(
        num_scalar_prefetch=1,
        grid=(n_slots // rows,),
        in_specs=[pl.BlockSpec(memory_space=pl.ANY)],
        out_specs=pl.BlockSpec((rows, d), lambda i, s: (i, 0)),
        scratch_shapes=[pltpu.VMEM((ROW_DMA_SLOTS, rows, d), x.dtype), pltpu.SemaphoreType.DMA((ROW_DMA_SLOTS,))],
    )
    assert rows % ROW_DMA_UNROLL == 0
    return pl.pallas_call(
        functools.partial(_dispatch_body, rows=rows),
        grid_spec=grid_spec,
        out_shape=jax.ShapeDtypeStruct((n_slots, d), BF16),
        compiler_params=_cp(("arbitrary",)),
        name="moe_dispatch",
    )(src, x)


def _tile_copy(src, dst, sem):
    return pltpu.make_async_copy(src, dst, sem)


def _moe_ffn_body(exp_ref, row0_ref, nsub_ref, jmap_ref, nzero_ref, xs_hbm, w1_ref, w3_ref, w2_ref, ys_hbm,
                  xbuf, acc, w1b, w3b, w2b, sem, in_sem, *, nj, sub):
    w = pl.program_id(0)
    j = pl.program_id(1)
    nsub = nsub_ref[w]
    row0 = row0_ref[w]
    nzero = nzero_ref[w]

    def zero_copy(t):
        r = pl.multiple_of(row0 + t * sub, sub)
        return _tile_copy(acc.at[pl.ds(0, sub)], ys_hbm.at[pl.ds(r, sub)], sem)

    @pl.when(jnp.logical_and(j == 0, nzero > 0))
    def _():
        acc[0:sub, :] = jnp.zeros((sub, acc.shape[1]), F32)

        def zs(t, carry):
            zero_copy(t).start()
            return carry

        def zw(t, carry):
            zero_copy(t).wait()
            return carry

        lax.fori_loop(0, nzero, zs, 0)
        lax.fori_loop(0, nzero, zw, 0)

    slot = w % 2

    def in_copy(item, t):
        r = pl.multiple_of(t * sub, sub)
        src = xs_hbm.at[pl.ds(pl.multiple_of(row0_ref[item] + r, sub), sub)]
        return _tile_copy(src, xbuf.at[item % 2, pl.ds(r, sub)], in_sem.at[item % 2])

    def for_item_tiles(item, fn):
        def it(t, carry):
            fn(t)
            return carry
        lax.fori_loop(0, nsub_ref[item], it, 0)

    def out_copy(t):
        r = pl.multiple_of(t * sub, sub)
        return _tile_copy(acc.at[pl.ds(r, sub)], ys_hbm.at[pl.ds(pl.multiple_of(row0 + r, sub), sub)], sem)

    def for_tiles(fn):
        def it(t, carry):
            fn(t)
            return carry
        lax.fori_loop(0, nsub, it, 0)

    def zero_tile(t):
        acc[pl.ds(pl.multiple_of(t * sub, sub), sub), :] = jnp.zeros((sub, acc.shape[1]), F32)

    @pl.when(jnp.logical_and(j == 0, w == 0))
    def _():
        for_item_tiles(w, lambda t: in_copy(w, t).start())

    @pl.when(j == 0)
    def _():
        for_tiles(zero_tile)
        for_item_tiles(w, lambda t: in_copy(w, t).wait())

        @pl.when(w + 1 < pl.num_programs(0))
        def _():
            for_item_tiles(w + 1, lambda t: in_copy(w + 1, t).start())

    def tile(t, n, w1v, w3v, w2v):
        r = pl.multiple_of(t * sub, sub)
        x = xbuf[slot, pl.ds(r, n * sub), :]
        a = jnp.dot(x, w1v, preferred_element_type=F32)
        c = jnp.dot(x, w3v, preferred_element_type=F32)
        h = (a * _sigmoid(a) * c).astype(BF16)
        acc[pl.ds(r, n * sub), :] += jnp.dot(h, w2v, preferred_element_type=F32)

    def cast_weights():
        return w1_ref[...].astype(BF16), w3_ref[...].astype(BF16), w2_ref[...].astype(BF16)

    per_trip = 2 * MOE_WIDE
    n_trips = nsub // per_trip

    def trip(t, wv):
        tile(per_trip * t, MOE_WIDE, *wv)
        tile(per_trip * t + MOE_WIDE, MOE_WIDE, *wv)

    @pl.when(n_trips >= 1)
    def _():
        wv = cast_weights()
        w1b[...], w3b[...], w2b[...] = wv
        trip(0, wv)

    @pl.when(jnp.logical_and(n_trips == 0, nsub > 0))
    def _():
        w1b[...], w3b[...], w2b[...] = cast_weights()

    def later_trip(t, carry):
        trip(t, (w1b[...], w3b[...], w2b[...]))
        return carry

    lax.fori_loop(1, n_trips, later_trip, 0)

    def rest(t, carry):
        tile(t, 1, w1b[...], w3b[...], w2b[...])
        return carry

    lax.fori_loop(n_trips * per_trip, nsub, rest, 0)

    @pl.when(j == nj - 1)
    def _():
        for_tiles(lambda t: out_copy(t).start())
        for_tiles(lambda t: out_copy(t).wait())


def _moe_ffn(item_exp, item_row0, item_nsub, item_jlast, item_nzero, xs, w1, w3, w2, layer, n_items, super_rows,
             sub, tf):
    n_slots, d = xs.shape
    f = w1.shape[3]
    nj = f // tf

    def jj(w, j, jl):
        return jnp.where(jl[w] > 0, j, nj - 1)

    grid_spec = pltpu.PrefetchScalarGridSpec(
        num_scalar_prefetch=5,
        grid=(n_items, nj),
        in_specs=[pl.BlockSpec(memory_space=pl.ANY),
                  pl.BlockSpec((None, None, d, tf), lambda w, j, ex, r0, ns, jl, nz: (layer, ex[w], 0, jj(w, j, jl))),
                  pl.BlockSpec((None, None, d, tf), lambda w, j, ex, r0, ns, jl, nz: (layer, ex[w], 0, jj(w, j, jl))),
                  pl.BlockSpec((None, None, tf, d), lambda w, j, ex, r0, ns, jl, nz: (layer, ex[w], jj(w, j, jl), 0))],
        out_specs=pl.BlockSpec(memory_space=pl.ANY),
        scratch_shapes=[pltpu.VMEM((2, super_rows, d), BF16),
                        pltpu.VMEM((super_rows, d), F32),
                        pltpu.VMEM((d, tf), BF16),
                        pltpu.VMEM((d, tf), BF16),
                        pltpu.VMEM((tf, d), BF16),
                        pltpu.SemaphoreType.DMA(()),
                        pltpu.SemaphoreType.DMA((2,))],
    )
    return pl.pallas_call(
        functools.partial(_moe_ffn_body, nj=nj, sub=sub),
        grid_spec=grid_spec,
        out_shape=jax.ShapeDtypeStruct((n_slots, d), F32),
        compiler_params=_cp(("arbitrary", "arbitrary")),
        name="moe_ffn",
    )(item_exp, item_row0, item_nsub, item_jlast, item_nzero, xs, w1, w3, w2)


def _combine_body(pos_ref, ys_hbm, tg_ref, x_ref, g_ref, b_ref, *refs, tb, alpha, head_chunks):
    out_refs, (buf, sem) = refs[:-2], refs[-2:]

    def issue(chunk, slot):
        def start(u, carry):
            for v in range(ROW_DMA_UNROLL // TOP_K_EXPERTS):
                r = u * (ROW_DMA_UNROLL // TOP_K_EXPERTS) + v
                for k in range(TOP_K_EXPERTS):
                    p = pos_ref[TOP_K_EXPERTS * (chunk * tb + r) + k]
                    _row_copy(ys_hbm, buf.at[slot, k], p, r, sem.at[slot]).start()
            return carry

        lax.fori_loop(0, tb * TOP_K_EXPERTS // ROW_DMA_UNROLL, start, 0)

    def wait(slot):
        def w(r, carry):
            for k in range(TOP_K_EXPERTS):
                _row_copy(ys_hbm, buf.at[slot, k], 0, r, sem.at[slot]).wait()
            return carry

        lax.fori_loop(0, tb, w, 0)

    slot = _prefetch_chunks(issue, wait)
    f = tg_ref[:, 0:1] * buf[slot, 0] + tg_ref[:, 1:2] * buf[slot, 1]
    y = _ln_rows(alpha * x_ref[...] + f, g_ref[...], b_ref[...])
    if head_chunks is None:
        of_ref, ob_ref = out_refs
        of_ref[...] = y
        ob_ref[...] = y.astype(BF16)
    else:
        head_ref, tail_ref = out_refs
        c = pl.program_id(0)

        @pl.when(c < head_chunks)
        def _():
            head_ref[...] = y

        @pl.when(c >= head_chunks)
        def _():
            tail_ref[...] = y


def _combine_ln(pos, ys, tg, x, g, b, alpha, tb, split_rows=None):
    m, d = x.shape
    row = lambda i, p: (i, 0)
    if split_rows is None:
        head_chunks = None
        out_specs = [pl.BlockSpec((tb, d), row), pl.BlockSpec((tb, d), row)]
        out_shape = [jax.ShapeDtypeStruct((m, d), F32), jax.ShapeDtypeStruct((m, d), BF16)]
    else:
        assert split_rows % tb == 0 and 0 < split_rows < m
        head_chunks = split_rows // tb
        out_specs = [pl.BlockSpec((tb, d), lambda i, p: (jnp.minimum(i, head_chunks - 1), 0)),
                     pl.BlockSpec((tb, d), lambda i, p: (jnp.maximum(i - head_chunks, 0), 0))]
        out_shape = [jax.ShapeDtypeStruct((split_rows, d), F32), jax.ShapeDtypeStruct((m - split_rows, d), F32)]
    grid_spec = pltpu.PrefetchScalarGridSpec(
        num_scalar_prefetch=1,
        grid=(m // tb,),
        in_specs=[pl.BlockSpec(memory_space=pl.ANY),
                  pl.BlockSpec((tb, LANES), row),
                  pl.BlockSpec((tb, d), row),
                  pl.BlockSpec((1, d), lambda i, p: (0, 0)),
                  pl.BlockSpec((1, d), lambda i, p: (0, 0))],
        out_specs=out_specs,
        scratch_shapes=[pltpu.VMEM((ROW_DMA_SLOTS, TOP_K_EXPERTS, tb, d), F32),
                        pltpu.SemaphoreType.DMA((ROW_DMA_SLOTS,))],
    )
    assert tb * TOP_K_EXPERTS % ROW_DMA_UNROLL == 0
    return pl.pallas_call(
        functools.partial(_combine_body, tb=tb, alpha=alpha, head_chunks=head_chunks),
        grid_spec=grid_spec,
        out_shape=out_shape,
        compiler_params=_cp(("arbitrary",)),
        name="moe_combine_ln",
    )(pos, ys, tg, x, g, b)


def _moe_plan(top_i, n_real, n_exp, n_slots, n_items, super_rows, sub):
    a = top_i.shape[0] * TOP_K_EXPERTS
    e_flat = top_i.reshape(-1)
    tok = jnp.arange(a, dtype=I32) // TOP_K_EXPERTS
    real = tok < n_real
    onehot = jnp.logical_and(e_flat[:, None] == jnp.arange(n_exp, dtype=I32)[None, :], real[:, None]).astype(I32)
    csum = jnp.cumsum(onehot, axis=0)
    rank = jnp.take_along_axis(csum, e_flat[:, None], axis=1)[:, 0] - 1
    counts = csum[-1]
    padded = (counts + sub - 1) // sub * sub
    gstart = jnp.cumsum(padded) - padded
    pos = jnp.where(real, gstart[e_flat] + rank, jnp.arange(a, dtype=I32) % n_slots).astype(I32)
    src = (jnp.arange(n_slots, dtype=I32) % n_real).at[jnp.where(real, pos, n_slots)].set(tok, mode="drop")
    n_super = (padded + super_rows - 1) // super_rows
    iend = jnp.cumsum(n_super)
    istart = iend - n_super
    total = iend[-1]
    wi = jnp.arange(n_items, dtype=I32)
    valid = wi < total
    e_w = jnp.minimum(jnp.searchsorted(iend, wi, side="right"), n_exp - 1).astype(I32)
    last_e = e_w[jnp.maximum(total - 1, 0)]
    e_w = jnp.where(valid, e_w, last_e)
    k_w = wi - istart[e_w]
    subs_e = padded[e_w] // sub
    items_e = jnp.maximum(n_super[e_w], 1)
    base, extra = subs_e // items_e, subs_e % items_e
    nsub = jnp.where(valid, base + (k_w < extra).astype(I32), 0).astype(I32)
    first_sub = k_w * base + jnp.minimum(k_w, extra)
    z0 = jnp.sum(padded) + (wi - total) * super_rows
    nzero = jnp.where(valid, 0, jnp.clip((n_slots - z0) // sub, 0, super_rows // sub)).astype(I32)
    row0 = jnp.where(valid, gstart[e_w] + first_sub * sub, jnp.where(nzero > 0, z0, 0)).astype(I32)
    return pos, src, e_w, row0, nsub, valid.astype(I32), nzero


def _rel_bucket(dist):
    n = jnp.maximum(dist, 0)
    max_exact = REL_BUCKETS // 2
    nf = jnp.maximum(n, 1).astype(F32)
    large = max_exact + (jnp.log(nf / max_exact) / math.log(REL_MAX_DIST / max_exact)
                         * (REL_BUCKETS - max_exact)).astype(I32)
    large = jnp.minimum(large, REL_BUCKETS - 1)
    return jnp.where(n < max_exact, n, large)


def _shifted_bias(rel_bias, dist):
    t = rel_bias[_rel_bucket(dist)] - rel_bias[REL_BUCKETS - 1]
    t = jnp.where((dist >= 0)[..., None], t, 0.0)
    return jnp.moveaxis(t, -1, 0)


def _pick_tile(m_rows, target):
    return max(t for t in range(16, target + 1, 16) if m_rows % t == 0)


def _pad_rows(x, rows):
    return jnp.pad(x, ((0, rows - x.shape[0]), (0, 0)))


def kernel(x_prompt, x_sample, cache_k, cache_v, cache_ik, state_conv, page_table, w_in, conv_w, conv_b,
           conv_ln_g, conv_ln_b, sgu_ln_g, sgu_ln_b, sgu_w, sgu_b, w_pa, w_pb, w_pc, w_out, ln1_g, ln1_b,
           ln2_g, ln2_b, rel_bias, ffn_w1, ffn_w3, ffn_w2, moe_router, moe_router_b, moe_w1, moe_w3, moe_w2):
    batch, seq, d_model = x_prompt.shape
    db, n_tok, _ = x_sample.shape
    depth = w_in.shape[0]
    n_pool = cache_k.shape[1]
    n_exp = moe_router.shape[2]
    mp, ms = batch * seq, db * n_tok
    assert ms <= SAMPLE_TILE and mp % SAMPLE_TILE == 0
    m_all = mp + SAMPLE_TILE
    tm = m_all // N_ROW_TILES
    assert tm * N_ROW_TILES == m_all and tm % 16 == 0
    alpha = (2 * depth) ** 0.25
    tq = min(256, seq)

    xf = jnp.concatenate([x_prompt.reshape(mp, d_model), x_sample.reshape(ms, d_model),
                          jnp.zeros((m_all - mp - ms, d_model), F32)], axis=0)
    xb = xf.astype(BF16)
    w_in_t = jnp.swapaxes(w_in, 1, 2)
    ha_rows = ([W_Q_ROW0 + HA_TILE * t for t in range(QKV_W // HA_TILE)]
               + [W_GLU_ROW0 + HA_TILE * t for t in range(2 * CONV_DIM // HA_TILE)]
               + [W_UV_ROW0 + HA_TILE * t for t in range(2 * SGU_DIM // HA_TILE)]
               + [W_GATE_ROW0 + HA_TILE * t for t in range(N_BRANCH * d_model // HA_TILE)]
               + [W_IQ_ROW0 + HA_TILE * t for t in range(IQ_W // HA_TILE)])
    w_pa_b, w_pb_b, w_pc_b, w_out_b = (w.astype(BF16) for w in (w_pa, w_pb, w_pc, w_out))
    ffn_w2_b = ffn_w2.astype(BF16)

    kk = jnp.arange(2 * tq, dtype=I32)
    d_wrap = jnp.where(kk < tq, -kk, 2 * tq - kk)

    def toeplitz(offset):
        v = _shifted_bias(rel_bias, d_wrap + offset)
        return jnp.tile(v, (1, tq))[:, :tq * (2 * tq - 1)].reshape(N_HEADS, tq, 2 * tq - 1)[:, :, :tq]

    bias_tiles = jnp.stack([toeplitz(0), toeplitz(tq)])
    tok = jnp.arange(n_tok, dtype=I32)
    lane = jnp.arange(LANES, dtype=I32)
    d_last = PAGE_SIZE + tok[:, None] - lane[None, :]
    bias_last = _shifted_bias(rel_bias, d_last).reshape(N_HEADS * n_tok, LANES)
    bias_new = _shifted_bias(rel_bias, tok[:, None] - lane[None, :]).reshape(N_HEADS * n_tok, LANES)

    ck = cache_k.reshape(depth, n_pool, PAGE_SIZE * N_KV_HEADS, HEAD_DIM)
    cv = cache_v.reshape(depth, n_pool, PAGE_SIZE * N_KV_HEADS, HEAD_DIM)
    cik_t = jnp.swapaxes(cache_ik, 2, 3)
    n_pages = page_table.shape[1]
    idx_group = math.gcd(n_pages, IDX_PAGE_GROUP)
    attn_group = math.gcd(n_pages, ATTN_PAGE_GROUP)
    n_slots = _round_up(TOP_K_EXPERTS * (mp + ms) + n_exp * (MOE_SUB - 1), MOE_SUB)
    n_items = -(-n_slots // MOE_SUPER) + n_exp

    outs = {k: [] for k in ("k", "v", "ik", "conv_p", "conv_s", "sgu_s")}
    for l in range(depth):
        ha = _in_proj(xb, w_in_t, l, ha_rows, tm, HA_TILE)
        kv, ikw, k4, v4 = _kv_proj(xb, w_in_t, l, tm)

        a_p = _attn_prompt(ha, kv, ikw, bias_tiles, batch, seq, tq)
        conv_w_rep = jnp.broadcast_to(conv_w[l].reshape(CONV_WIDTH, CONV_CHUNKS, 1, LANES).transpose(1, 0, 2, 3),
                                      (CONV_CHUNKS, CONV_WIDTH, F32_SUBLANES, LANES))
        c_p, conv_state_p = _conv_prompt(ha, conv_w_rep, conv_b[l][None], conv_ln_g[l][None], conv_ln_b[l][None],
                                         batch, seq, min(256, seq))
        s_p = _sgu_prompt(ha, sgu_ln_g[l][None], sgu_ln_b[l][None], sgu_w[l], sgu_b[l].T, batch, seq,
                          min(512, seq))

        has, kvs, ikws = ha[mp:mp + ms], kv[mp:mp + ms], ikw[mp:mp + ms]
        iq_rows = has[:, HA_IQ:HA_IQ + IQ_W].reshape(db, n_tok * IDX_HEADS, IDX_DIM)
        iw_rows = (ikws[:, IKW_IW:IKW_IW + IDX_HEADS] * (IDX_DIM ** -0.5 * IDX_HEADS ** -0.5)
                   ).reshape(db, n_tok * IDX_HEADS, 1)
        iw_rows = jnp.broadcast_to(iw_rows, (db, n_tok * IDX_HEADS, LANES))
        scores = _idx_sample(page_table, iq_rows, iw_rows, cik_t, l, n_tok, idx_group)
        pad_rows = LANES - n_tok
        ik_new = jnp.pad(ikws[:, :IDX_DIM].reshape(db, n_tok, IDX_DIM), ((0, 0), (0, pad_rows), (0, 0)))
        scores_new = _idx_new(iq_rows, iw_rows, ik_new, n_tok)
        q_rows = has[:, HA_Q:HA_Q + QKV_W].reshape(db, n_tok, N_HEADS, HEAD_DIM).transpose(0, 2, 1, 3)
        q_rows = q_rows.reshape(db, N_HEADS * n_tok, HEAD_DIM)
        k_new = jnp.pad(kvs[:, KV_K:KV_K + KV_W].reshape(db, n_tok, KV_W), ((0, 0), (0, pad_rows), (0, 0)))
        v_new = jnp.pad(kvs[:, KV_V:KV_V + KV_W].reshape(db, n_tok, KV_W), ((0, 0), (0, pad_rows), (0, 0)))
        a_s = _attn_sample(page_table, scores, scores_new, q_rows, k_new, v_new, bias_last, bias_new,
                           ck, cv, l, n_tok, attn_group)
        a_s = a_s.reshape(db, N_HEADS, n_tok, HEAD_DIM).transpose(0, 2, 1, 3).reshape(ms, QKV_W)
        gd = SGU_DIM // SGU_GROUPS
        wv = jnp.repeat(sgu_w[l][:, :n_tok, :n_tok].transpose(1, 2, 0), gd, axis=-1)
        bv = jnp.repeat(sgu_b[l][:, :n_tok].T, gd, axis=-1)
        c_s, conv_state_s, s_s, vn_s = _mix_sample(
            has[:, HA_GLU:HA_GLU + 2 * CONV_DIM], has[:, HA_UV:HA_UV + 2 * SGU_DIM], state_conv[l],
            conv_w[l], conv_b[l][None], conv_ln_g[l][None], conv_ln_b[l][None],
            sgu_ln_g[l][None], sgu_ln_b[l][None], wv, bv, db, n_tok)

        sample_rows = tuple(_pad_rows(t.astype(BF16), SAMPLE_TILE) for t in (a_s, c_s, s_s))
        merged = _merge((a_p, c_p, s_p), sample_rows, ha, w_pa_b, w_pb_b, w_pc_b, l, SAMPLE_TILE)
        x1f, x1b = _mm_ln(merged, w_out_b, l, xf, ln1_g[l][None], ln1_b[l][None], alpha, _pick_tile(m_all, 384))

        j = l // 2
        if l % 2 == 0:
            h = _swiglu_up(x1b, ffn_w1, ffn_w3, j, tm, 512)
            xf, xb = _mm_ln(h, ffn_w2_b, j, x1f, ln2_g[l][None], ln2_b[l][None], alpha, _pick_tile(m_all, 384))
        else:
            w_r = jnp.pad(moe_router[j], ((0, 0), (0, LANES - n_exp)))
            b_r = jnp.pad(moe_router_b[j], (0, LANES - n_exp))[None]
            ti, tg = _router(x1f, w_r, b_r, n_exp, _pick_tile(m_all, 528))
            pos, src, it_e, it_r0, it_ns, it_valid, it_nz = _moe_plan(ti[:, :TOP_K_EXPERTS], mp + ms, n_exp,
                                                                      n_slots, n_items, MOE_SUPER, MOE_SUB)
            xs = _dispatch(src, x1f, n_slots, MOE_SUB)
            ys = _moe_ffn(it_e, it_r0, it_ns, it_valid, it_nz, xs, moe_w1, moe_w3, moe_w2, j, n_items,
                          MOE_SUPER, MOE_SUB, MOE_TF)
            tb = COMBINE_TB if m_all % COMBINE_TB == 0 and mp % COMBINE_TB == 0 else 16
            if l == depth - 1:
                y_head, y_tail = _combine_ln(pos, ys, tg, x1f, ln2_g[l][None], ln2_b[l][None], alpha, tb,
                                             split_rows=mp)
            else:
                xf, xb = _combine_ln(pos, ys, tg, x1f, ln2_g[l][None], ln2_b[l][None], alpha, tb)

        outs["k"].append(k4)
        outs["v"].append(v4)
        outs["ik"].append(ikw[:, :IDX_DIM])
        outs["conv_p"].append(conv_state_p)
        outs["conv_s"].append(conv_state_s)
        outs["sgu_s"].append(vn_s.reshape(db, n_tok, SGU_DIM))

    def split(name, rows_per_token, width_shape):
        st = jnp.stack(outs[name])
        p = st[:, :mp * rows_per_token].reshape((depth, batch, seq) + width_shape)
        s = st[:, mp * rows_per_token:(mp + ms) * rows_per_token].reshape((depth, db, n_tok) + width_shape)
        return p, s

    k_p, k_s = split("k", N_KV_HEADS, (N_KV_HEADS, HEAD_DIM))
    v_p, v_s = split("v", N_KV_HEADS, (N_KV_HEADS, HEAD_DIM))
    ik_p, ik_s = split("ik", 1, (IDX_DIM,))
    if depth % 2 == 0:
        y_prompt = y_head.reshape(batch, seq, d_model)
        y_sample = y_tail[:ms].reshape(db, n_tok, d_model)
    else:
        y_prompt = xf[:mp].reshape(batch, seq, d_model)
        y_sample = xf[mp:mp + ms].reshape(db, n_tok, d_model)
    return (y_prompt, y_sample, k_p, v_p, ik_p, jnp.stack(outs["conv_p"]), k_s, v_s, ik_s,
            jnp.stack(outs["conv_s"]), jnp.stack(outs["sgu_s"]))
```
